```python
import math
import jax, jax.numpy as jnp
from jax import lax
import numpy as np

D_MODEL = 1024
BATCH = 8
SEQ = 8192
DEPTH = 2

CHUNK = 64
D_FF = 2816
BRANCH_W = 512
N_BRANCH = 3
CONV_W = 3
H_RET = 4
DK_RET = BRANCH_W // H_RET
DV_RET = BRANCH_W // H_RET
H_ATT = 8
DH_ATT = BRANCH_W // H_ATT
N_PREV_CHUNKS = 8
BAND = (N_PREV_CHUNKS + 1) * CHUNK
REL_CLIP = 128
N_REL = 2 * REL_CLIP + 1
IN_COLS = 3 * BRANCH_W + 4 * BRANCH_W + 3 * BRANCH_W
EPS = 1e-6
NEG_INF = -1e30
ROPE_BASE = 10000.0

kernel_name = "hybrid_gated_conv_retention_chunkattn_macaron"


def _rmsnorm(x, w):
    xf = x.astype(jnp.float32)
    xf = xf * lax.rsqrt(jnp.mean(xf * xf, axis=-1, keepdims=True) + EPS)
    return (xf * w.astype(jnp.float32)).astype(x.dtype)


def _swiglu(h, w_gate, w_up, w_down):
    return (jax.nn.silu(h @ w_gate) * (h @ w_up)) @ w_down


def _short_gated_conv(u, b_gate, c_gate, conv_w):
    z = c_gate * u
    zp = jnp.pad(z, ((0, 0), (CONV_W - 1, 0), (0, 0)))
    s = z.shape[1]
    conv = sum(conv_w[j] * zp[:, j:j + s] for j in range(CONV_W))
    return b_gate * conv


def _rotary(x, cos, sin):
    half = x.shape[-1] // 2
    x1, x2 = x[..., :half], x[..., half:]
    c = cos[None, :, None, :]
    s_ = sin[None, :, None, :]
    return jnp.concatenate([x1 * c - x2 * s_, x1 * s_ + x2 * c], axis=-1).astype(x.dtype)


def _retention(q, k, v, g):
    bsz, s = q.shape[:2]
    nc = s // CHUNK
    log_gamma = jnp.log1p(-jnp.exp2(-5.0 - jnp.arange(H_RET, dtype=jnp.float32)))
    pos = jnp.arange(CHUNK, dtype=jnp.float32)
    d_intra = jnp.exp(log_gamma[:, None, None] * jnp.abs(pos[:, None] - pos[None, :]))
    q_decay = jnp.exp(log_gamma[:, None] * (pos + 1.0))
    k_decay = jnp.exp(log_gamma[:, None] * (CHUNK - 1.0 - pos))
    chunk_decay = jnp.exp(log_gamma * CHUNK)

    def to_chunks(t):
        return t.astype(jnp.float32).reshape(bsz, nc, CHUNK, H_RET, -1).transpose(1, 0, 3, 2, 4)

    qc = to_chunks(q) * (DK_RET ** -0.5)
    kc, vc = to_chunks(k), to_chunks(v)

    def step(state, qkv):
        qb, kb, vb = qkv
        inner = jnp.einsum('bhnk,bhmk->bhnm', qb, kb) * d_intra[None]
        o = jnp.einsum('bhnm,bhmv->bhnv', inner, vb) \
            + jnp.einsum('bhnk,bhkv->bhnv', qb * q_decay[None, :, :, None], state)
        state = state * chunk_decay[None, :, None, None] \
            + jnp.einsum('bhmk,bhmv->bhkv', kb * k_decay[None, :, :, None], vb)
        return state, o

    s0 = jnp.zeros((bsz, H_RET, DK_RET, DV_RET), jnp.float32)
    _, o = lax.scan(step, s0, (qc, kc, vc))
    o = o.transpose(1, 0, 3, 2, 4).reshape(bsz, s, H_RET, DV_RET)
    o = o * lax.rsqrt(jnp.mean(o * o, axis=-1, keepdims=True) + EPS)
    o = o.reshape(bsz, s, BRANCH_W).astype(g.dtype)
    return jax.nn.silu(g) * o


def _chunk_band_attention(q, k, v, rel_bias):
    bsz, s = q.shape[:2]
    nc = s // CHUNK
    pad = N_PREV_CHUNKS * CHUNK
    qc = q.reshape(bsz, nc, CHUNK, H_ATT, DH_ATT).transpose(1, 0, 3, 2, 4)
    kp = jnp.pad(k, ((0, 0), (pad, 0), (0, 0), (0, 0))).transpose(0, 2, 1, 3)
    vp = jnp.pad(v, ((0, 0), (pad, 0), (0, 0), (0, 0))).transpose(0, 2, 1, 3)
    n = jnp.arange(CHUNK)
    m = jnp.arange(BAND)
    dist = (pad + n)[:, None] - m[None, :]
    idx = jnp.clip(dist, -REL_CLIP, REL_CLIP) + REL_CLIP
    bias = rel_bias[:, idx].astype(jnp.float32)
    scale = DH_ATT ** -0.5

    def one_chunk(args):
        c, q_blk = args
        kb = lax.dynamic_slice_in_dim(kp, c * CHUNK, BAND, axis=2)
        vb = lax.dynamic_slice_in_dim(vp, c * CHUNK, BAND, axis=2)
        sc = jnp.einsum('bhnd,bhmd->bhnm', q_blk, kb).astype(jnp.float32) * scale + bias[None]
        valid = m >= (N_PREV_CHUNKS - c) * CHUNK
        sc = jnp.where(valid[None, None, None, :], sc, NEG_INF)
        p = jax.nn.softmax(sc, axis=-1).astype(vb.dtype)
        return jnp.einsum('bhnm,bhmd->bhnd', p, vb)

    out = lax.map(one_chunk, (jnp.arange(nc), qc))
    return out.transpose(1, 0, 3, 2, 4).reshape(bsz, s, BRANCH_W)


def _fwd_setup_inputs(seed: int = 0) -> dict:
    key = jax.random.key(seed)
    ks = jax.random.split(key, 20)
    f32 = jnp.float32

    def w(k, shape, fan_in):
        return jax.random.normal(k, shape, f32) * (fan_in ** -0.5)

    def gain(k, shape):
        return 1.0 + 0.05 * jax.random.normal(k, shape, f32)

    return {
        "x": jax.random.normal(ks[0], (BATCH, SEQ, D_MODEL), f32),
        "ffn1_norm": gain(ks[1], (DEPTH, D_MODEL)),
        "ffn1_w_gate": w(ks[2], (DEPTH, D_MODEL, D_FF), D_MODEL),
        "ffn1_w_up": w(ks[3], (DEPTH, D_MODEL, D_FF), D_MODEL),
        "ffn1_w_down": w(ks[4], (DEPTH, D_FF, D_MODEL), D_FF),
        "mix_norm": gain(ks[5], (DEPTH, D_MODEL)),
        "w_in": w(ks[6], (DEPTH, D_MODEL, IN_COLS), D_MODEL),
        "conv_w": w(ks[7], (DEPTH, CONV_W, BRANCH_W), CONV_W),
        "rel_bias": 0.5 * jax.random.normal(ks[8], (DEPTH, H_ATT, N_REL), f32),
        "w_branch": w(ks[9], (DEPTH, N_BRANCH, BRANCH_W, D_MODEL), BRANCH_W),
        "w_merge_gate": w(ks[10], (DEPTH, N_BRANCH, D_MODEL, D_MODEL), D_MODEL),
        "w_out": w(ks[11], (DEPTH, D_MODEL, D_MODEL), D_MODEL),
        "ffn2_norm": gain(ks[12], (DEPTH, D_MODEL)),
        "ffn2_w_gate": w(ks[13], (DEPTH, D_MODEL, D_FF), D_MODEL),
        "ffn2_w_up": w(ks[14], (DEPTH, D_MODEL, D_FF), D_MODEL),
        "ffn2_w_down": w(ks[15], (DEPTH, D_FF, D_MODEL), D_FF),
        "final_norm": gain(ks[16], (D_MODEL,)),
    }


def _fwd_reference(x, ffn1_norm, ffn1_w_gate, ffn1_w_up, ffn1_w_down, mix_norm, w_in, conv_w,
              rel_bias, w_branch, w_merge_gate, w_out, ffn2_norm, ffn2_w_gate, ffn2_w_up,
              ffn2_w_down, final_norm):
    bsz, s, _ = x.shape
    inv_freq = ROPE_BASE ** (-jnp.linspace(0.0, 1.0, DK_RET // 2, dtype=jnp.float32))
    ang = jnp.arange(s, dtype=jnp.float32)[:, None] * inv_freq[None, :]
    cos, sin = jnp.cos(ang), jnp.sin(ang)
    split_pts = [BRANCH_W * i for i in range(1, IN_COLS // BRANCH_W)]

    for l in range(DEPTH):
        x = x + 0.5 * _swiglu(_rmsnorm(x, ffn1_norm[l]), ffn1_w_gate[l], ffn1_w_up[l], ffn1_w_down[l])

        h = _rmsnorm(x, mix_norm[l])
        cols = jnp.split(h @ w_in[l], split_pts, axis=-1)
        cu, cb, cc, rq, rk, rv, rg, aq, ak, av = cols

        y_conv = _short_gated_conv(cu, cb, cc, conv_w[l])

        rq = _rotary(rq.reshape(bsz, s, H_RET, DK_RET), cos, sin)
        rk = _rotary(rk.reshape(bsz, s, H_RET, DK_RET), cos, sin)
        y_ret = _retention(rq, rk, rv.reshape(bsz, s, H_RET, DV_RET), rg)

        y_att = _chunk_band_attention(aq.reshape(bsz, s, H_ATT, DH_ATT),
                                      ak.reshape(bsz, s, H_ATT, DH_ATT),
                                      av.reshape(bsz, s, H_ATT, DH_ATT), rel_bias[l])

        merged = sum(jax.nn.sigmoid(h @ w_merge_gate[l, i]) * (y @ w_branch[l, i])
                     for i, y in enumerate((y_conv, y_ret, y_att)))
        x = x + merged @ w_out[l]

        x = x + 0.5 * _swiglu(_rmsnorm(x, ffn2_norm[l]), ffn2_w_gate[l], ffn2_w_up[l], ffn2_w_down[l])

    return _rmsnorm(x, final_norm)


import jax as _jax
import jax.numpy as _jnp

TWIN_FORMAT = 'train_step'
FWD_PARAMS = ['x', 'ffn1_norm', 'ffn1_w_gate', 'ffn1_w_up', 'ffn1_w_down', 'mix_norm', 'w_in', 'conv_w', 'rel_bias', 'w_branch', 'w_merge_gate', 'w_out', 'ffn2_norm', 'ffn2_w_gate', 'ffn2_w_up', 'ffn2_w_down', 'final_norm']
TWIN_WEIGHTS = ['ffn1_norm', 'ffn1_w_gate', 'ffn1_w_up', 'ffn1_w_down', 'mix_norm', 'w_in', 'conv_w', 'rel_bias', 'w_branch', 'w_merge_gate', 'w_out', 'ffn2_norm', 'ffn2_w_gate', 'ffn2_w_up', 'ffn2_w_down', 'final_norm']
TWIN_DIFF_INPUT = 'x'
TWIN_INPUTS = ['x', 'ffn1_norm', 'ffn1_w_gate', 'ffn1_w_up', 'ffn1_w_down', 'mix_norm', 'w_in', 'conv_w', 'rel_bias', 'w_branch', 'w_merge_gate', 'w_out', 'ffn2_norm', 'ffn2_w_gate', 'ffn2_w_up', 'ffn2_w_down', 'final_norm', 'loss_target', 'm_ffn1_norm', 'm_ffn1_w_gate', 'm_ffn1_w_up', 'm_ffn1_w_down', 'm_mix_norm', 'm_w_in', 'm_conv_w', 'm_rel_bias', 'm_w_branch', 'm_w_merge_gate', 'm_w_out', 'm_ffn2_norm', 'm_ffn2_w_gate', 'm_ffn2_w_up', 'm_ffn2_w_down', 'm_final_norm', 'v_ffn1_norm', 'v_ffn1_w_gate', 'v_ffn1_w_up', 'v_ffn1_w_down', 'v_mix_norm', 'v_w_in', 'v_conv_w', 'v_rel_bias', 'v_w_branch', 'v_w_merge_gate', 'v_w_out', 'v_ffn2_norm', 'v_ffn2_w_gate', 'v_ffn2_w_up', 'v_ffn2_w_down', 'v_final_norm']
TWIN_OUTPUTS = ['loss', 'grad_x', 'grad_ffn1_norm', 'grad_ffn1_w_gate', 'grad_ffn1_w_up', 'grad_ffn1_w_down', 'grad_mix_norm', 'grad_w_in', 'grad_conv_w', 'grad_rel_bias', 'grad_w_branch', 'grad_w_merge_gate', 'grad_w_out', 'grad_ffn2_norm', 'grad_ffn2_w_gate', 'grad_ffn2_w_up', 'grad_ffn2_w_down', 'grad_final_norm', 'delta_ffn1_norm', 'delta_ffn1_w_gate', 'delta_ffn1_w_up', 'delta_ffn1_w_down', 'delta_mix_norm', 'delta_w_in', 'delta_conv_w', 'delta_rel_bias', 'delta_w_branch', 'delta_w_merge_gate', 'delta_w_out', 'delta_ffn2_norm', 'delta_ffn2_w_gate', 'delta_ffn2_w_up', 'delta_ffn2_w_down', 'delta_final_norm', 'new_m_ffn1_norm', 'new_m_ffn1_w_gate', 'new_m_ffn1_w_up', 'new_m_ffn1_w_down', 'new_m_mix_norm', 'new_m_w_in', 'new_m_conv_w', 'new_m_rel_bias', 'new_m_w_branch', 'new_m_w_merge_gate', 'new_m_w_out', 'new_m_ffn2_norm', 'new_m_ffn2_w_gate', 'new_m_ffn2_w_up', 'new_m_ffn2_w_down', 'new_m_final_norm', 'new_v_ffn1_norm', 'new_v_ffn1_w_gate', 'new_v_ffn1_w_up', 'new_v_ffn1_w_down', 'new_v_mix_norm', 'new_v_w_in', 'new_v_conv_w', 'new_v_rel_bias', 'new_v_w_branch', 'new_v_w_merge_gate', 'new_v_w_out', 'new_v_ffn2_norm', 'new_v_ffn2_w_gate', 'new_v_ffn2_w_up', 'new_v_ffn2_w_down', 'new_v_final_norm']
TWIN_LEAF_KINDS = {'loss': 'loss', 'grad_x': 'grad_x', 'grad_ffn1_norm': 'grad_w', 'grad_ffn1_w_gate': 'grad_w', 'grad_ffn1_w_up': 'grad_w', 'grad_ffn1_w_down': 'grad_w', 'grad_mix_norm': 'grad_w', 'grad_w_in': 'grad_w', 'grad_conv_w': 'grad_w', 'grad_rel_bias': 'grad_w', 'grad_w_branch': 'grad_w', 'grad_w_merge_gate': 'grad_w', 'grad_w_out': 'grad_w', 'grad_ffn2_norm': 'grad_w', 'grad_ffn2_w_gate': 'grad_w', 'grad_ffn2_w_up': 'grad_w', 'grad_ffn2_w_down': 'grad_w', 'grad_final_norm': 'grad_w', 'delta_ffn1_norm': 'delta_w', 'delta_ffn1_w_gate': 'delta_w', 'delta_ffn1_w_up': 'delta_w', 'delta_ffn1_w_down': 'delta_w', 'delta_mix_norm': 'delta_w', 'delta_w_in': 'delta_w', 'delta_conv_w': 'delta_w', 'delta_rel_bias': 'delta_w', 'delta_w_branch': 'delta_w', 'delta_w_merge_gate': 'delta_w', 'delta_w_out': 'delta_w', 'delta_ffn2_norm': 'delta_w', 'delta_ffn2_w_gate': 'delta_w', 'delta_ffn2_w_up': 'delta_w', 'delta_ffn2_w_down': 'delta_w', 'delta_final_norm': 'delta_w', 'new_m_ffn1_norm': 'new_m', 'new_m_ffn1_w_gate': 'new_m', 'new_m_ffn1_w_up': 'new_m', 'new_m_ffn1_w_down': 'new_m', 'new_m_mix_norm': 'new_m', 'new_m_w_in': 'new_m', 'new_m_conv_w': 'new_m', 'new_m_rel_bias': 'new_m', 'new_m_w_branch': 'new_m', 'new_m_w_merge_gate': 'new_m', 'new_m_w_out': 'new_m', 'new_m_ffn2_norm': 'new_m', 'new_m_ffn2_w_gate': 'new_m', 'new_m_ffn2_w_up': 'new_m', 'new_m_ffn2_w_down': 'new_m', 'new_m_final_norm': 'new_m', 'new_v_ffn1_norm': 'new_v', 'new_v_ffn1_w_gate': 'new_v', 'new_v_ffn1_w_up': 'new_v', 'new_v_ffn1_w_down': 'new_v', 'new_v_mix_norm': 'new_v', 'new_v_w_in': 'new_v', 'new_v_conv_w': 'new_v', 'new_v_rel_bias': 'new_v', 'new_v_w_branch': 'new_v', 'new_v_w_merge_gate': 'new_v', 'new_v_w_out': 'new_v', 'new_v_ffn2_norm': 'new_v', 'new_v_ffn2_w_gate': 'new_v', 'new_v_ffn2_w_up': 'new_v', 'new_v_ffn2_w_down': 'new_v', 'new_v_final_norm': 'new_v'}


def _forward(args):
    return _fwd_reference(*[args[k] for k in FWD_PARAMS])


def _output_shape():
    out = _jax.eval_shape(lambda: _forward(_fwd_setup_inputs(0)))
    return out.shape, out.dtype

N_MICROBATCH = 1
ADAM_LR = 0.001
ADAM_B1 = 0.9
ADAM_B2 = 0.999
ADAM_EPS = 1e-08
ADAM_WD = 0.01
ADAM_STEP = 10
PER_EXAMPLE_BATCH_AXIS = {'x': 0, 'loss_target': 0}
SHARED_INPUTS = []
_WEIGHT_DTYPES = {'ffn1_norm': _jnp.float32, 'ffn1_w_gate': _jnp.float32, 'ffn1_w_up': _jnp.float32, 'ffn1_w_down': _jnp.float32, 'mix_norm': _jnp.float32, 'w_in': _jnp.float32, 'conv_w': _jnp.float32, 'rel_bias': _jnp.float32, 'w_branch': _jnp.float32, 'w_merge_gate': _jnp.float32, 'w_out': _jnp.float32, 'ffn2_norm': _jnp.float32, 'ffn2_w_gate': _jnp.float32, 'ffn2_w_up': _jnp.float32, 'ffn2_w_down': _jnp.float32, 'final_norm': _jnp.float32}
MOMENT_SCALE = {'ffn1_norm': 1.461140e-01, 'ffn1_w_gate': 6.242929e-02, 'ffn1_w_up': 6.031747e-02, 'ffn1_w_down': 1.002012e-01, 'mix_norm': 2.759599e-01, 'w_in': 1.181921e-01, 'conv_w': 1.833057e-01, 'rel_bias': 1.261872e-02, 'w_branch': 8.404826e-02, 'w_merge_gate': 3.283174e-02, 'w_out': 1.455899e-01, 'ffn2_norm': 9.809366e-02, 'ffn2_w_gate': 4.074384e-02, 'ffn2_w_up': 3.963922e-02, 'ffn2_w_down': 6.572631e-02, 'final_norm': 6.395245e+01}


def _to_microbatches(a, axis):
    t = _jnp.moveaxis(a, axis, 0)
    t = t.reshape((N_MICROBATCH, t.shape[0] // N_MICROBATCH) + t.shape[1:])
    return _jnp.moveaxis(t, 1, axis + 1)


def setup_inputs(seed: int = 0) -> dict:
    inp = _fwd_setup_inputs(seed)
    key = _jax.random.fold_in(_jax.random.key(seed), 7919)
    shape, _ = _output_shape()
    out = dict(inp)
    out["loss_target"] = _jax.random.normal(_jax.random.fold_in(key, 0), shape, _jnp.float32)
    for i, name in enumerate(TWIN_WEIGHTS):
        w = inp[name].astype(_jnp.float32)
        if MOMENT_SCALE is None:
            s = _jnp.sqrt(_jnp.mean(_jnp.square(w)) + 1e-30)
        else:
            s = MOMENT_SCALE[name]
        km, kv = _jax.random.split(_jax.random.fold_in(key, i + 1))
        out[name] = w
        out["m_" + name] = s * _jax.random.normal(km, w.shape, _jnp.float32)
        out["v_" + name] = (s * s) * _jax.random.uniform(kv, w.shape, _jnp.float32, 0.5, 1.5)
    if N_MICROBATCH > 1:
        for name, axis in PER_EXAMPLE_BATCH_AXIS.items():
            out[name] = _to_microbatches(out[name], axis)
    return {'x': out['x'], 'ffn1_norm': out['ffn1_norm'], 'ffn1_w_gate': out['ffn1_w_gate'], 'ffn1_w_up': out['ffn1_w_up'], 'ffn1_w_down': out['ffn1_w_down'], 'mix_norm': out['mix_norm'], 'w_in': out['w_in'], 'conv_w': out['conv_w'], 'rel_bias': out['rel_bias'], 'w_branch': out['w_branch'], 'w_merge_gate': out['w_merge_gate'], 'w_out': out['w_out'], 'ffn2_norm': out['ffn2_norm'], 'ffn2_w_gate': out['ffn2_w_gate'], 'ffn2_w_up': out['ffn2_w_up'], 'ffn2_w_down': out['ffn2_w_down'], 'final_norm': out['final_norm'], 'loss_target': out['loss_target'], 'm_ffn1_norm': out['m_ffn1_norm'], 'm_ffn1_w_gate': out['m_ffn1_w_gate'], 'm_ffn1_w_up': out['m_ffn1_w_up'], 'm_ffn1_w_down': out['m_ffn1_w_down'], 'm_mix_norm': out['m_mix_norm'], 'm_w_in': out['m_w_in'], 'm_conv_w': out['m_conv_w'], 'm_rel_bias': out['m_rel_bias'], 'm_w_branch': out['m_w_branch'], 'm_w_merge_gate': out['m_w_merge_gate'], 'm_w_out': out['m_w_out'], 'm_ffn2_norm': out['m_ffn2_norm'], 'm_ffn2_w_gate': out['m_ffn2_w_gate'], 'm_ffn2_w_up': out['m_ffn2_w_up'], 'm_ffn2_w_down': out['m_ffn2_w_down'], 'm_final_norm': out['m_final_norm'], 'v_ffn1_norm': out['v_ffn1_norm'], 'v_ffn1_w_gate': out['v_ffn1_w_gate'], 'v_ffn1_w_up': out['v_ffn1_w_up'], 'v_ffn1_w_down': out['v_ffn1_w_down'], 'v_mix_norm': out['v_mix_norm'], 'v_w_in': out['v_w_in'], 'v_conv_w': out['v_conv_w'], 'v_rel_bias': out['v_rel_bias'], 'v_w_branch': out['v_w_branch'], 'v_w_merge_gate': out['v_w_merge_gate'], 'v_w_out': out['v_w_out'], 'v_ffn2_norm': out['v_ffn2_norm'], 'v_ffn2_w_gate': out['v_ffn2_w_gate'], 'v_ffn2_w_up': out['v_ffn2_w_up'], 'v_ffn2_w_down': out['v_ffn2_w_down'], 'v_final_norm': out['v_final_norm']}


def _loss(weights, diff, rest, loss_target):
    with _jax.named_scope("forward"):
        args = {**rest, TWIN_DIFF_INPUT: diff, **{k: w.astype(_WEIGHT_DTYPES[k]) for k, w in weights.items()}}
        y = _forward(args)
    with _jax.named_scope("loss_head"):
        err = _jnp.square(y.astype(_jnp.float32) - loss_target)
        return 0.5 * _jnp.sum(_jnp.mean(err, axis=-1)) if err.ndim else 0.5 * err


def _adamw(w, g, m, v):
    m = ADAM_B1 * m + (1.0 - ADAM_B1) * g
    v = ADAM_B2 * v + (1.0 - ADAM_B2) * _jnp.square(g)
    m_hat = m / (1.0 - ADAM_B1 ** ADAM_STEP)
    v_hat = v / (1.0 - ADAM_B2 ** ADAM_STEP)
    delta = -ADAM_LR * (m_hat / (_jnp.sqrt(v_hat) + ADAM_EPS) + ADAM_WD * w)
    return delta, m, v


def reference(x, ffn1_norm, ffn1_w_gate, ffn1_w_up, ffn1_w_down, mix_norm, w_in, conv_w, rel_bias, w_branch, w_merge_gate, w_out, ffn2_norm, ffn2_w_gate, ffn2_w_up, ffn2_w_down, final_norm, loss_target, m_ffn1_norm, m_ffn1_w_gate, m_ffn1_w_up, m_ffn1_w_down, m_mix_norm, m_w_in, m_conv_w, m_rel_bias, m_w_branch, m_w_merge_gate, m_w_out, m_ffn2_norm, m_ffn2_w_gate, m_ffn2_w_up, m_ffn2_w_down, m_final_norm, v_ffn1_norm, v_ffn1_w_gate, v_ffn1_w_up, v_ffn1_w_down, v_mix_norm, v_w_in, v_conv_w, v_rel_bias, v_w_branch, v_w_merge_gate, v_w_out, v_ffn2_norm, v_ffn2_w_gate, v_ffn2_w_up, v_ffn2_w_down, v_final_norm):
    given = dict(x=x, ffn1_norm=ffn1_norm, ffn1_w_gate=ffn1_w_gate, ffn1_w_up=ffn1_w_up, ffn1_w_down=ffn1_w_down, mix_norm=mix_norm, w_in=w_in, conv_w=conv_w, rel_bias=rel_bias, w_branch=w_branch, w_merge_gate=w_merge_gate, w_out=w_out, ffn2_norm=ffn2_norm, ffn2_w_gate=ffn2_w_gate, ffn2_w_up=ffn2_w_up, ffn2_w_down=ffn2_w_down, final_norm=final_norm, loss_target=loss_target, m_ffn1_norm=m_ffn1_norm, m_ffn1_w_gate=m_ffn1_w_gate, m_ffn1_w_up=m_ffn1_w_up, m_ffn1_w_down=m_ffn1_w_down, m_mix_norm=m_mix_norm, m_w_in=m_w_in, m_conv_w=m_conv_w, m_rel_bias=m_rel_bias, m_w_branch=m_w_branch, m_w_merge_gate=m_w_merge_gate, m_w_out=m_w_out, m_ffn2_norm=m_ffn2_norm, m_ffn2_w_gate=m_ffn2_w_gate, m_ffn2_w_up=m_ffn2_w_up, m_ffn2_w_down=m_ffn2_w_down, m_final_norm=m_final_norm, v_ffn1_norm=v_ffn1_norm, v_ffn1_w_gate=v_ffn1_w_gate, v_ffn1_w_up=v_ffn1_w_up, v_ffn1_w_down=v_ffn1_w_down, v_mix_norm=v_mix_norm, v_w_in=v_w_in, v_conv_w=v_conv_w, v_rel_bias=v_rel_bias, v_w_branch=v_w_branch, v_w_merge_gate=v_w_merge_gate, v_w_out=v_w_out, v_ffn2_norm=v_ffn2_norm, v_ffn2_w_gate=v_ffn2_w_gate, v_ffn2_w_up=v_ffn2_w_up, v_ffn2_w_down=v_ffn2_w_down, v_final_norm=v_final_norm)
    weights = {n: given[n] for n in TWIN_WEIGHTS}
    shared = {n: given[n] for n in SHARED_INPUTS}
    per_example = {n: given[n] for n in ['x']}
    grad_fn = _jax.value_and_grad(_loss, argnums=(0, 1))

    def one_microbatch(ex, loss_target):
        ex = dict(ex)
        diff = ex.pop(TWIN_DIFF_INPUT)
        return grad_fn(weights, diff, {**shared, **ex}, loss_target)

    if N_MICROBATCH == 1:
        loss, (grad_w, grad_x) = one_microbatch(per_example, given["loss_target"])
    else:
        def body(carry, xs):
            loss_sum, grad_sum = carry
            l_k, (gw_k, gx_k) = one_microbatch(xs[0], xs[1])
            with _jax.named_scope("update"):
                return (loss_sum + l_k, _jax.tree.map(_jnp.add, grad_sum, gw_k)), gx_k

        init = (_jnp.zeros((), _jnp.float32), _jax.tree.map(_jnp.zeros_like, weights))
        (loss, grad_w), grad_x = _jax.lax.scan(body, init, (per_example, given["loss_target"]))
    with _jax.named_scope("update"):
        delta_w, new_m, new_v = {}, {}, {}
        for n in TWIN_WEIGHTS:
            delta_w[n], new_m[n], new_v[n] = _adamw(weights[n], grad_w[n], given["m_" + n], given["v_" + n])
    return (loss, grad_x, *[grad_w[n] for n in TWIN_WEIGHTS], *[delta_w[n] for n in TWIN_WEIGHTS],
            *[new_m[n] for n in TWIN_WEIGHTS], *[new_v[n] for n in TWIN_WEIGHTS])
```

```python
import functools
import math

import jax
import jax.numpy as jnp
from jax import lax
from jax.experimental import pallas as pl
from jax.experimental.pallas import tpu as pltpu

F32 = jnp.float32
BF = jnp.bfloat16
MESH = pl.DeviceIdType.MESH
ARB = "arbitrary"
PAR = "parallel"

EPS = 1e-6
NEG_INF = -1e30
ROPE_BASE = 10000.0
CHUNK = 64
BRANCH_W = 512
H_RET = 4
DK_RET = 128
H_ATT = 8
DH_ATT = 64
N_PREV = 8
REL_CLIP = 128
N_REL = 2 * REL_CLIP + 1
N_SHARD = 4
LANE = 128
RET_L = 256
ATT_TQ = 128
ATT_PAD = N_PREV * CHUNK
ATT_SPAN = ATT_TQ + ATT_PAD
ATT_TOEP = 2 * REL_CLIP
RB_PAD = 264
TM = 512

ADAM_LR = 0.001
ADAM_B1 = 0.9
ADAM_B2 = 0.999
ADAM_EPS = 1e-08
ADAM_WD = 0.01
ADAM_STEP = 10

NT_DIMS = (((1,), (1,)), ((), ()))
TN_DIMS = (((0,), (0,)), ((), ()))


def _cp(sem, vmem_mb=48):
    return pltpu.CompilerParams(dimension_semantics=sem, vmem_limit_bytes=vmem_mb << 20)


def _sds(shape, dtype):
    return jax.ShapeDtypeStruct(tuple(shape), dtype)


def _rms_r(x):
    return lax.rsqrt(jnp.mean(x * x, axis=-1, keepdims=True) + EPS)


def _sigmoid(x):
    return jax.nn.sigmoid(x)


def _rms_bwd(dh, xv, nw):
    r = _rms_r(xv)
    xh = xv * r
    dxh = dh * nw
    dx = r * (dxh - xh * jnp.mean(dxh * xh, axis=-1, keepdims=True))
    return dx, jnp.sum(dh * xh, axis=0, keepdims=True)


def _ffn_fwd(x, nw, wg, wu, wd, l, name):
    T, D = x.shape
    ns, _, _, fs = wg.shape
    tm = min(TM, T)

    def body(x_ref, nw_ref, wg_ref, wu_ref, wd_ref, xo_ref, g_ref, u_ref, h_s, acc_s):
        j = pl.program_id(1)

        @pl.when(j == 0)
        def _():
            xv = x_ref[...]
            h_s[...] = (xv * _rms_r(xv) * nw_ref[...]).astype(BF)
            acc_s[...] = jnp.zeros_like(acc_s)

        h = h_s[...]
        gb = jnp.dot(h, wg_ref[...], preferred_element_type=F32).astype(BF)
        ub = jnp.dot(h, wu_ref[...], preferred_element_type=F32).astype(BF)
        g_ref[...] = gb
        u_ref[...] = ub
        g = gb.astype(F32)
        a = (g * _sigmoid(g) * ub.astype(F32)).astype(BF)
        acc_s[...] += jnp.dot(a, wd_ref[...], preferred_element_type=F32)

        @pl.when(j == ns - 1)
        def _():
            xo_ref[...] = x_ref[...] + 0.5 * acc_s[...]

    wspec = pl.BlockSpec((None, None, D, fs), lambda i, j: (j, l, 0, 0))
    return pl.pallas_call(
        body, name=name, grid=(T // tm, ns),
        in_specs=[pl.BlockSpec((tm, D), lambda i, j: (i, 0)),
                  pl.BlockSpec((1, D), lambda i, j: (0, 0)),
                  wspec, wspec,
                  pl.BlockSpec((None, None, fs, D), lambda i, j: (j, l, 0, 0))],
        out_specs=[pl.BlockSpec((tm, D), lambda i, j: (i, 0)),
                   pl.BlockSpec((None, tm, fs), lambda i, j: (j, i, 0)),
                   pl.BlockSpec((None, tm, fs), lambda i, j: (j, i, 0))],
        out_shape=[_sds((T, D), F32), _sds((ns, T, fs), BF), _sds((ns, T, fs), BF)],
        scratch_shapes=[pltpu.VMEM((tm, D), BF), pltpu.VMEM((tm, D), F32)],
        compiler_params=_cp((PAR, ARB)),
    )(x, nw, wg, wu, wd)


def _ffn_bwd(dxo, x, nw, g, u, wg, wu, wd, l, name):
    T, D = x.shape
    ns, _, _, fs = wg.shape
    tm = min(TM, T)

    def body(dxo_ref, x_ref, nw_ref, g_ref, u_ref, wg_ref, wu_ref, wd_ref,
             dx_ref, dg_ref, du_ref, a_ref, h_ref, dacc_ref, dnw_ref, dacc_s, acc_s):
        i = pl.program_id(0)
        j = pl.program_id(1)

        @pl.when(j == 0)
        def _():
            xv = x_ref[...]
            h_ref[...] = (xv * _rms_r(xv) * nw_ref[...]).astype(BF)
            db = (0.5 * dxo_ref[...]).astype(BF)
            dacc_ref[...] = db
            dacc_s[...] = db
            acc_s[...] = jnp.zeros_like(acc_s)

        @pl.when((i == 0) & (j == 0))
        def _():
            dnw_ref[...] = jnp.zeros_like(dnw_ref)

        da = lax.dot_general(dacc_s[...], wd_ref[...], NT_DIMS, preferred_element_type=F32)
        gv = g_ref[...].astype(F32)
        uv = u_ref[...].astype(F32)
        s = _sigmoid(gv)
        sg = gv * s
        a_ref[...] = (sg * uv).astype(BF)
        dub = (da * sg).astype(BF)
        dgb = (da * uv * (s * (1.0 + gv * (1.0 - s)))).astype(BF)
        dg_ref[...] = dgb
        du_ref[...] = dub
        acc_s[...] += (lax.dot_general(dgb, wg_ref[...], NT_DIMS, preferred_element_type=F32)
                       + lax.dot_general(dub, wu_ref[...], NT_DIMS, preferred_element_type=F32))

        @pl.when(j == ns - 1)
        def _():
            dx, dn = _rms_bwd(acc_s[...], x_ref[...], nw_ref[...])
            dx_ref[...] = dxo_ref[...] + dx
            dnw_ref[...] += dn

    tok = pl.BlockSpec((tm, D), lambda i, j: (i, 0))
    row = pl.BlockSpec((1, D), lambda i, j: (0, 0))
    hid = pl.BlockSpec((None, tm, fs), lambda i, j: (j, i, 0))
    wspec = pl.BlockSpec((None, None, D, fs), lambda i, j: (j, l, 0, 0))
    return pl.pallas_call(
        body, name=name, grid=(T // tm, ns),
        in_specs=[tok, tok, row, hid, hid, wspec, wspec,
                  pl.BlockSpec((None, None, fs, D), lambda i, j: (j, l, 0, 0))],
        out_specs=[tok, hid, hid, hid, tok, tok, row],
        out_shape=[_sds((T, D), F32), _sds((ns, T, fs), BF), _sds((ns, T, fs), BF), _sds((ns, T, fs), BF),
                   _sds((T, D), BF), _sds((T, D), BF), _sds((1, D), F32)],
        scratch_shapes=[pltpu.VMEM((tm, D), BF), pltpu.VMEM((tm, D), F32)],
        compiler_params=_cp((ARB, ARB)),
    )(dxo, x, nw, g, u, wg, wu, wd)


def _tn(a, b, a_spec, b_spec, out_shape, out_spec, grid, name, prev=None):
    nk = grid[-1]
    acc_shape = tuple(d for d in out_spec.block_shape if d is not None)

    def body(*refs):
        a_ref, b_ref = refs[0], refs[1]
        o_ref, acc = refs[-2], refs[-1]
        k = pl.program_id(2)
        prod = lax.dot_general(a_ref[...], b_ref[...], TN_DIMS, preferred_element_type=F32)

        @pl.when(k == 0)
        def _():
            acc[...] = prod

        @pl.when(k > 0)
        def _():
            acc[...] += prod

        @pl.when(k == nk - 1)
        def _():
            o_ref[...] = acc[...].astype(o_ref.dtype)

    in_specs = [a_spec, b_spec]
    args = [a, b]
    aliases = {}
    if prev is not None:
        in_specs.append(pl.BlockSpec(memory_space=pl.ANY))
        args.append(prev)
        aliases = {2: 0}
    return pl.pallas_call(
        body, name=name, grid=grid, in_specs=in_specs, out_specs=out_spec,
        out_shape=out_shape, scratch_shapes=[pltpu.VMEM(acc_shape, F32)],
        input_output_aliases=aliases, compiler_params=_cp((PAR, PAR, ARB)),
    )(*args)


def _inproj_fwd(x, nw, wbig, l, name):
    T, D = x.shape
    nb = wbig.shape[-1]
    tm = min(2 * TM, T)
    bn = min(2048, nb)

    def body(x_ref, nw_ref, w_ref, o_ref, h_ref, h_s):
        @pl.when(pl.program_id(1) == 0)
        def _():
            xv = x_ref[...]
            hb = (xv * _rms_r(xv) * nw_ref[...]).astype(BF)
            h_s[...] = hb
            h_ref[...] = hb

        o_ref[...] = jnp.dot(h_s[...], w_ref[...], preferred_element_type=F32).astype(BF)

    return pl.pallas_call(
        body, name=name, grid=(T // tm, nb // bn),
        in_specs=[pl.BlockSpec((tm, D), lambda i, n: (i, 0)),
                  pl.BlockSpec((1, D), lambda i, n: (0, 0)),
                  pl.BlockSpec((None, D, bn), lambda i, n: (l, 0, n))],
        out_specs=[pl.BlockSpec((tm, bn), lambda i, n: (i, n)),
                   pl.BlockSpec((tm, D), lambda i, n: (i, 0))],
        out_shape=[_sds((T, nb), BF), _sds((T, D), BF)],
        scratch_shapes=[pltpu.VMEM((tm, D), BF)],
        compiler_params=_cp((PAR, ARB)),
    )(x, nw, wbig)


def _inproj_bwd(dbig, wbig, x, nw, dxin, l, name):
    T, D = x.shape
    nb = wbig.shape[-1]
    tm = min(TM, T)
    tk = min(2048, nb)
    nk = nb // tk

    def body(a_ref, w_ref, x_ref, nw_ref, dxin_ref, dx_ref, dnw_ref, acc_s):
        i = pl.program_id(0)
        k = pl.program_id(1)
        prod = lax.dot_general(a_ref[...], w_ref[...], NT_DIMS, preferred_element_type=F32)

        @pl.when((i == 0) & (k == 0))
        def _():
            dnw_ref[...] = jnp.zeros_like(dnw_ref)

        @pl.when(k == 0)
        def _():
            acc_s[...] = prod

        @pl.when(k > 0)
        def _():
            acc_s[...] += prod

        @pl.when(k == nk - 1)
        def _():
            dx, dn = _rms_bwd(acc_s[...], x_ref[...], nw_ref[...])
            dx_ref[...] = dxin_ref[...] + dx
            dnw_ref[...] += dn

    tok = pl.BlockSpec((tm, D), lambda i, k: (i, 0))
    row = pl.BlockSpec((1, D), lambda i, k: (0, 0))
    return pl.pallas_call(
        body, name=name, grid=(T // tm, nk),
        in_specs=[pl.BlockSpec((tm, tk), lambda i, k: (i, k)),
                  pl.BlockSpec((None, D, tk), lambda i, k: (l, 0, k)),
                  tok, row, tok],
        out_specs=[tok, row],
        out_shape=[_sds((T, D), F32), _sds((1, D), F32)],
        scratch_shapes=[pltpu.VMEM((tm, D), F32)],
        compiler_params=_cp((ARB, ARB)),
    )(dbig, wbig, x, nw, dxin)


CONV_R = 512
COL_CU, COL_CB, COL_CC = 0, 4, 8
COL_RQ, COL_RK, COL_RV, COL_RG = 12, 16, 20, 24
COL_AQ, COL_AK, COL_AV = 28, 32, 36


def _seg0(big):
    return (big.shape[1] - 10 * BRANCH_W) // LANE


def _conv_fwd(big, cw, name):
    T = big.shape[0]
    R = min(CONV_R, T)

    def body(cu_ref, cb_ref, cc_ref, w_ref, y_ref, z_s):
        z_s[pl.ds(0, 8), :] = jnp.zeros((8, LANE), F32)

        def fill(t, c):
            sl = pl.ds(pl.multiple_of(t * R, R), R)
            z_s[pl.ds(pl.multiple_of(t * R + 8, 8), R), :] = cc_ref[sl, :].astype(F32) * cu_ref[sl, :].astype(F32)
            return c

        lax.fori_loop(0, T // R, fill, 0)
        w0, w1, w2 = w_ref[0:1, :], w_ref[1:2, :], w_ref[2:3, :]

        def step(t, c):
            zz = z_s[pl.ds(pl.multiple_of(t * R, R), R + 8), :]
            z0 = zz[8:]
            z1 = pltpu.roll(zz, 1, 0)[8:]
            z2 = pltpu.roll(zz, 2, 0)[8:]
            sl = pl.ds(pl.multiple_of(t * R, R), R)
            y_ref[sl, :] = (cb_ref[sl, :].astype(F32) * (w2 * z0 + w1 * z1 + w0 * z2)).astype(BF)
            return c

        lax.fori_loop(0, T // R, step, 0)

    def col(base):
        return pl.BlockSpec((T, LANE), lambda j: (0, _seg0(big) + base + j))

    return pl.pallas_call(
        body, name=name, grid=(BRANCH_W // LANE,),
        in_specs=[col(COL_CU), col(COL_CB), col(COL_CC), pl.BlockSpec((3, LANE), lambda j: (0, j))],
        out_specs=pl.BlockSpec((T, LANE), lambda j: (0, j)),
        out_shape=_sds((T, BRANCH_W), BF),
        scratch_shapes=[pltpu.VMEM((T + 8, LANE), F32)],
        compiler_params=_cp((PAR,)),
    )(big, big, big, cw)


def _conv_bwd(big, dy, cw, name):
    T = big.shape[0]
    R = min(CONV_R, T)

    def body(cu_ref, cb_ref, cc_ref, dy_ref, w_ref, dcu_ref, dcb_ref, dcc_ref, dw_ref, z_s, d_s):
        z_s[pl.ds(0, 8), :] = jnp.zeros((8, LANE), F32)
        d_s[pl.ds(T, 8), :] = jnp.zeros((8, LANE), F32)

        def fill(t, c):
            sl = pl.ds(pl.multiple_of(t * R, R), R)
            z_s[pl.ds(pl.multiple_of(t * R + 8, 8), R), :] = cc_ref[sl, :].astype(F32) * cu_ref[sl, :].astype(F32)
            d_s[sl, :] = dy_ref[sl, :].astype(F32) * cb_ref[sl, :].astype(F32)
            return c

        lax.fori_loop(0, T // R, fill, 0)
        w0, w1, w2 = w_ref[0:1, :], w_ref[1:2, :], w_ref[2:3, :]

        def step(t, carry):
            a0, a1, a2 = carry
            zz = z_s[pl.ds(pl.multiple_of(t * R, R), R + 8), :]
            z0 = zz[8:]
            z1 = pltpu.roll(zz, 1, 0)[8:]
            z2 = pltpu.roll(zz, 2, 0)[8:]
            sl = pl.ds(pl.multiple_of(t * R, R), R)
            dyv = dy_ref[sl, :].astype(F32)
            dcb_ref[sl, :] = (dyv * (w2 * z0 + w1 * z1 + w0 * z2)).astype(BF)
            dd = d_s[pl.ds(pl.multiple_of(t * R, R), R + 8), :]
            d0 = dd[:R]
            d1 = pltpu.roll(dd, R + 7, 0)[:R]
            d2 = pltpu.roll(dd, R + 6, 0)[:R]
            dz = w2 * d0 + w1 * d1 + w0 * d2
            dcc_ref[sl, :] = (dz * cu_ref[sl, :].astype(F32)).astype(BF)
            dcu_ref[sl, :] = (dz * cc_ref[sl, :].astype(F32)).astype(BF)
            a0 = a0 + jnp.sum(d0 * z2, axis=0, keepdims=True)
            a1 = a1 + jnp.sum(d0 * z1, axis=0, keepdims=True)
            a2 = a2 + jnp.sum(d0 * z0, axis=0, keepdims=True)
            return a0, a1, a2

        zero = jnp.zeros((1, LANE), F32)
        a0, a1, a2 = lax.fori_loop(0, T // R, step, (zero, zero, zero))
        dw_ref[0:1, :] = a0
        dw_ref[1:2, :] = a1
        dw_ref[2:3, :] = a2

    def col(base):
        return pl.BlockSpec((T, LANE), lambda j: (0, _seg0(big) + base + j))

    out = pl.BlockSpec((T, LANE), lambda j: (0, j))
    w = pl.BlockSpec((3, LANE), lambda j: (0, j))
    return pl.pallas_call(
        body, name=name, grid=(BRANCH_W // LANE,),
        in_specs=[col(COL_CU), col(COL_CB), col(COL_CC), out, w],
        out_specs=[out, out, out, w],
        out_shape=[_sds((T, BRANCH_W), BF)] * 3 + [_sds((3, BRANCH_W), F32)],
        scratch_shapes=[pltpu.VMEM((T + 8, LANE), F32), pltpu.VMEM((T + 8, LANE), F32)],
        compiler_params=_cp((PAR,)),
    )(big, big, big, dy, cw)


def _ret_tables(T):
    L = min(RET_L, T)
    hh = jnp.arange(H_RET, dtype=F32)
    lg = jnp.log1p(-jnp.exp2(-5.0 - hh))
    n = jnp.arange(L, dtype=F32)
    a = jnp.exp(lg[:, None] * (n + 1.0))
    b = jnp.exp(lg[:, None] * (L - 1.0 - n))
    gl = jnp.exp(lg * L)
    ch = jnp.arange(L) // CHUNK
    m = jnp.exp(lg[:, None, None] * jnp.abs(n[:, None] - n[None, :])) * (ch[None, :] <= ch[:, None]).astype(F32)
    inv_freq = ROPE_BASE ** (-jnp.linspace(0.0, 1.0, DK_RET // 2, dtype=F32))
    ang = jnp.arange(T, dtype=F32)[:, None] * inv_freq[None, :]
    cos, sin = jnp.cos(ang), jnp.sin(ang)
    return dict(
        L=L, M=m,
        a=jnp.broadcast_to(a[:, :, None], (H_RET, L, DK_RET)),
        b=jnp.broadcast_to(b[:, :, None], (H_RET, L, DK_RET)),
        gl=jnp.broadcast_to(gl[:, None, None], (H_RET, 1, DK_RET)),
        cos=jnp.concatenate([cos, cos], axis=-1), sin=jnp.concatenate([-sin, sin], axis=-1))


def _rot(x, cs, sn):
    return x * cs + pltpu.roll(x, DK_RET // 2, 1) * sn


def _unrot(dy, cs, sn):
    return dy * cs + pltpu.roll(dy * sn, DK_RET // 2, 1)


def _ret_fwd(big, tb, name):
    T = big.shape[0]
    L = tb["L"]
    nsc = T // L
    scale = DK_RET ** -0.5

    def body(q_ref, k_ref, v_ref, g_ref, cos_ref, sin_ref, m_ref, a_ref, b_ref, gl_ref,
             y_ref, o_ref, st_ref, s_s):
        @pl.when(pl.program_id(1) == 0)
        def _():
            s_s[...] = jnp.zeros_like(s_s)

        cs, sn = cos_ref[...], sin_ref[...]
        qt = _rot(q_ref[...].astype(F32), cs, sn) * scale
        kt = _rot(k_ref[...].astype(F32), cs, sn)
        qb, kb, vb = qt.astype(BF), kt.astype(BF), v_ref[...]
        s_prev = s_s[...]
        st_ref[...] = s_prev
        p = lax.dot_general(qb, kb, NT_DIMS, preferred_element_type=F32) * m_ref[...]
        o = (jnp.dot(p.astype(BF), vb, preferred_element_type=F32)
             + jnp.dot((qt * a_ref[...]).astype(BF), s_prev.astype(BF), preferred_element_type=F32))
        s_s[...] = s_prev * gl_ref[...] + lax.dot_general((kt * b_ref[...]).astype(BF), vb, TN_DIMS,
                                                         preferred_element_type=F32)
        o_ref[...] = o
        gv = g_ref[...].astype(F32)
        y_ref[...] = (gv * _sigmoid(gv) * o * _rms_r(o)).astype(BF)

    def col(base):
        return pl.BlockSpec((L, LANE), lambda h, i: (i, _seg0(big) + base + h))

    tab = pl.BlockSpec((L, DK_RET), lambda h, i: (i, 0))
    per_head = pl.BlockSpec((None, L, DK_RET), lambda h, i: (h, 0, 0))
    out = pl.BlockSpec((L, LANE), lambda h, i: (i, h))
    return pl.pallas_call(
        body, name=name, grid=(H_RET, nsc),
        in_specs=[col(COL_RQ), col(COL_RK), col(COL_RV), col(COL_RG), tab, tab,
                  pl.BlockSpec((None, L, L), lambda h, i: (h, 0, 0)), per_head, per_head,
                  pl.BlockSpec((None, 1, DK_RET), lambda h, i: (h, 0, 0))],
        out_specs=[out, out, pl.BlockSpec((None, None, DK_RET, DK_RET), lambda h, i: (i, h, 0, 0))],
        out_shape=[_sds((T, BRANCH_W), BF), _sds((T, BRANCH_W), F32), _sds((nsc, H_RET, DK_RET, DK_RET), F32)],
        scratch_shapes=[pltpu.VMEM((DK_RET, DK_RET), F32)],
        compiler_params=_cp((PAR, ARB)),
    )(big, big, big, big, tb["cos"], tb["sin"], tb["M"], tb["a"], tb["b"], tb["gl"])


def _ret_bwd(big, o, st, dy, tb, name):
    T = big.shape[0]
    L = tb["L"]
    nsc = T // L
    scale = DK_RET ** -0.5

    def body(q_ref, k_ref, v_ref, g_ref, cos_ref, sin_ref, m_ref, a_ref, b_ref, gl_ref, o_ref, st_ref, dy_ref,
             dq_ref, dk_ref, dv_ref, dg_ref, ds_s):
        @pl.when(pl.program_id(1) == 0)
        def _():
            ds_s[...] = jnp.zeros_like(ds_s)

        cs, sn = cos_ref[...], sin_ref[...]
        mm, av, bv = m_ref[...], a_ref[...], b_ref[...]
        qt = _rot(q_ref[...].astype(F32), cs, sn) * scale
        kt = _rot(k_ref[...].astype(F32), cs, sn)
        qb, kb, vb = qt.astype(BF), kt.astype(BF), v_ref[...]
        pb = (lax.dot_general(qb, kb, NT_DIMS, preferred_element_type=F32) * mm).astype(BF)
        ov = o_ref[...]
        r = _rms_r(ov)
        oh = ov * r
        gv = g_ref[...].astype(F32)
        sg = _sigmoid(gv)
        dyv = dy_ref[...].astype(F32)
        dg_ref[...] = (dyv * oh * (sg * (1.0 + gv * (1.0 - sg)))).astype(BF)
        doh = dyv * gv * sg
        dob = (r * (doh - oh * jnp.mean(doh * oh, axis=-1, keepdims=True))).astype(BF)
        dsb = ds_s[...].astype(BF)
        spb = st_ref[...].astype(BF)
        dpb = (lax.dot_general(dob, vb, NT_DIMS, preferred_element_type=F32) * mm).astype(BF)
        dqt = (jnp.dot(dpb, kb, preferred_element_type=F32)
               + lax.dot_general(dob, spb, NT_DIMS, preferred_element_type=F32) * av)
        dkt = (lax.dot_general(dpb, qb, TN_DIMS, preferred_element_type=F32)
               + lax.dot_general(vb, dsb, NT_DIMS, preferred_element_type=F32) * bv)
        dv = (lax.dot_general(pb, dob, TN_DIMS, preferred_element_type=F32)
              + jnp.dot((kt * bv).astype(BF), dsb, preferred_element_type=F32))
        ds_s[...] = ds_s[...] * gl_ref[...] + lax.dot_general((qt * av).astype(BF), dob, TN_DIMS,
                                                              preferred_element_type=F32)
        dq_ref[...] = (_unrot(dqt, cs, sn) * scale).astype(BF)
        dk_ref[...] = _unrot(dkt, cs, sn).astype(BF)
        dv_ref[...] = dv.astype(BF)

    def rev(i):
        return nsc - 1 - i

    def col(base):
        return pl.BlockSpec((L, LANE), lambda h, i: (rev(i), _seg0(big) + base + h))

    tab = pl.BlockSpec((L, DK_RET), lambda h, i: (rev(i), 0))
    per_head = pl.BlockSpec((None, L, DK_RET), lambda h, i: (h, 0, 0))
    out = pl.BlockSpec((L, LANE), lambda h, i: (rev(i), h))
    return pl.pallas_call(
        body, name=name, grid=(H_RET, nsc),
        in_specs=[col(COL_RQ), col(COL_RK), col(COL_RV), col(COL_RG), tab, tab,
                  pl.BlockSpec((None, L, L), lambda h, i: (h, 0, 0)), per_head, per_head,
                  pl.BlockSpec((None, 1, DK_RET), lambda h, i: (h, 0, 0)),
                  out, pl.BlockSpec((None, None, DK_RET, DK_RET), lambda h, i: (rev(i), h, 0, 0)), out],
        out_specs=[out, out, out, out],
        out_shape=[_sds((T, BRANCH_W), BF)] * 4,
        scratch_shapes=[pltpu.VMEM((DK_RET, DK_RET), F32)],
        compiler_params=_cp((PAR, ARB)),
    )(big, big, big, big, tb["cos"], tb["sin"], tb["M"], tb["a"], tb["b"], tb["gl"], o, st, dy)


def _relbias_onehot(n):
    mm = lax.broadcasted_iota(jnp.int32, (RB_PAD, ATT_TOEP), 1)
    rr = lax.broadcasted_iota(jnp.int32, (RB_PAD, ATT_TOEP), 0)
    idx = jnp.clip(n + ATT_TOEP - mm, 0, 2 * REL_CLIP)
    return (rr == idx).astype(F32)


def _relbias_expand(rbp, name):
    far = ATT_SPAN - ATT_TOEP

    def body(rb_ref, o_ref):
        rb = rb_ref[...]
        const = jnp.broadcast_to(rb[:, 2 * REL_CLIP:2 * REL_CLIP + 1], (H_ATT, far))

        def row(n, c):
            toep = jnp.dot(rb, _relbias_onehot(n), preferred_element_type=F32, precision=lax.Precision.HIGHEST)
            m = lax.broadcasted_iota(jnp.int32, (1, ATT_SPAN), 1)
            d = n // CHUNK + N_PREV - m // CHUNK
            neg = jnp.where((d >= 0) & (d <= N_PREV), 0.0, NEG_INF).astype(F32)
            o_ref[n] = jnp.concatenate([const, toep], axis=1) + neg
            return c

        lax.fori_loop(0, ATT_TQ, row, 0)

    return pl.pallas_call(
        body, name=name,
        in_specs=[pl.BlockSpec(memory_space=pltpu.VMEM)],
        out_specs=pl.BlockSpec(memory_space=pltpu.VMEM),
        out_shape=_sds((ATT_TQ, H_ATT, ATT_SPAN), F32),
    )(rbp)


def _relbias_grad(dbt, name):
    far = ATT_SPAN - ATT_TOEP

    def body(d_ref, o_ref):
        def row(n, carry):
            acc, cs = carry
            dn = d_ref[n]
            acc = acc + lax.dot_general(dn[:, far:], _relbias_onehot(n), NT_DIMS, preferred_element_type=F32,
                                        precision=lax.Precision.HIGHEST)
            cs = cs + jnp.sum(dn[:, :far], axis=1, keepdims=True)
            return acc, cs

        acc, cs = lax.fori_loop(0, ATT_TQ, row, (jnp.zeros((H_ATT, RB_PAD), F32), jnp.zeros((H_ATT, 1), F32)))
        rr = lax.broadcasted_iota(jnp.int32, (H_ATT, RB_PAD), 1)
        o_ref[...] = acc + jnp.where(rr == 2 * REL_CLIP, cs, 0.0)

    return pl.pallas_call(
        body, name=name,
        in_specs=[pl.BlockSpec(memory_space=pltpu.VMEM)],
        out_specs=pl.BlockSpec(memory_space=pltpu.VMEM),
        out_shape=_sds((H_ATT, RB_PAD), F32),
    )(dbt)


def _att_pad_fill(dst_s, src_ref, T):
    dst_s[pl.ds(0, ATT_PAD), :] = jnp.zeros((ATT_PAD, LANE), dst_s.dtype)
    R = min(512, T)

    def cp(t, c):
        dst_s[pl.ds(pl.multiple_of(ATT_PAD + t * R, LANE), R), :] = src_ref[pl.ds(pl.multiple_of(t * R, R), R), :]
        return c

    lax.fori_loop(0, T // R, cp, 0)


def _att_probs(qh, kh, bias, valid):
    s = lax.dot_general(qh, kh, NT_DIMS, preferred_element_type=F32) * (DH_ATT ** -0.5) + bias
    s = jnp.where(valid, s, NEG_INF)
    p = jnp.exp(s - jnp.max(s, axis=-1, keepdims=True))
    return p / jnp.sum(p, axis=-1, keepdims=True)


def _att_fwd(big, bias, name):
    T = big.shape[0]
    nt = T // ATT_TQ

    def body(q_ref, k_ref, v_ref, b_ref, y_ref, kp_s, vp_s):
        i = pl.program_id(1)

        @pl.when(i == 0)
        def _():
            _att_pad_fill(kp_s, k_ref, T)
            _att_pad_fill(vp_s, v_ref, T)

        t0 = pl.multiple_of(i * ATT_TQ, ATT_TQ)
        kw = kp_s[pl.ds(t0, ATT_SPAN), :]
        vw = vp_s[pl.ds(t0, ATT_SPAN), :]
        valid = (t0 - ATT_PAD + lax.broadcasted_iota(jnp.int32, (1, ATT_SPAN), 1)) >= 0
        outs = []
        for hh in range(2):
            sl = slice(hh * DH_ATT, (hh + 1) * DH_ATT)
            pn = _att_probs(q_ref[:, sl], kw[:, sl], b_ref[hh], valid)
            outs.append(jnp.dot(pn.astype(BF), vw[:, sl], preferred_element_type=F32))
        y_ref[...] = jnp.concatenate(outs, axis=1).astype(BF)

    def whole(base):
        return pl.BlockSpec((T, LANE), lambda p, i: (0, _seg0(big) + base + p))

    return pl.pallas_call(
        body, name=name, grid=(H_ATT // 2, nt),
        in_specs=[pl.BlockSpec((ATT_TQ, LANE), lambda p, i: (i, _seg0(big) + COL_AQ + p)), whole(COL_AK), whole(COL_AV),
                  pl.BlockSpec((2, ATT_TQ, ATT_SPAN), lambda p, i: (p, 0, 0))],
        out_specs=pl.BlockSpec((ATT_TQ, LANE), lambda p, i: (i, p)),
        out_shape=_sds((T, BRANCH_W), BF),
        scratch_shapes=[pltpu.VMEM((T + ATT_PAD, LANE), BF), pltpu.VMEM((T + ATT_PAD, LANE), BF)],
        compiler_params=_cp((PAR, ARB)),
    )(big, big, big, bias)


def _att_bwd(big, bias, dy, name):
    T = big.shape[0]
    nt = T // ATT_TQ
    scale = DH_ATT ** -0.5

    def body(q_ref, k_ref, v_ref, b_ref, dy_ref, dq_ref, dk_ref, dv_ref, db_ref, kp_s, vp_s, dk_s, dv_s):
        i = pl.program_id(1)

        @pl.when(i == 0)
        def _():
            _att_pad_fill(kp_s, k_ref, T)
            _att_pad_fill(vp_s, v_ref, T)
            dk_s[...] = jnp.zeros_like(dk_s)
            dv_s[...] = jnp.zeros_like(dv_s)
            db_ref[...] = jnp.zeros_like(db_ref)

        t0 = pl.multiple_of(i * ATT_TQ, ATT_TQ)
        win = pl.ds(t0, ATT_SPAN)
        kw = kp_s[win, :]
        vw = vp_s[win, :]
        valid = (t0 - ATT_PAD + lax.broadcasted_iota(jnp.int32, (1, ATT_SPAN), 1)) >= 0
        dqs, dks, dvs = [], [], []
        for hh in range(2):
            sl = slice(hh * DH_ATT, (hh + 1) * DH_ATT)
            qh, kh, vh = q_ref[:, sl], kw[:, sl], vw[:, sl]
            pn = _att_probs(qh, kh, b_ref[hh], valid)
            doh = dy_ref[:, sl]
            dp = lax.dot_general(doh, vh, NT_DIMS, preferred_element_type=F32)
            ds = pn * (dp - jnp.sum(dp * pn, axis=-1, keepdims=True))
            db_ref[hh] += ds
            dsb = ds.astype(BF)
            dqs.append(jnp.dot(dsb, kh, preferred_element_type=F32) * scale)
            dks.append(lax.dot_general(dsb, qh, TN_DIMS, preferred_element_type=F32) * scale)
            dvs.append(lax.dot_general(pn.astype(BF), doh, TN_DIMS, preferred_element_type=F32))
        dq_ref[...] = jnp.concatenate(dqs, axis=1).astype(BF)
        dk_s[win, :] += jnp.concatenate(dks, axis=1)
        dv_s[win, :] += jnp.concatenate(dvs, axis=1)

        @pl.when(i == nt - 1)
        def _():
            R = min(512, T)

            def cp(t, c):
                src = pl.ds(pl.multiple_of(ATT_PAD + t * R, LANE), R)
                dst = pl.ds(pl.multiple_of(t * R, R), R)
                dk_ref[dst, :] = dk_s[src, :].astype(BF)
                dv_ref[dst, :] = dv_s[src, :].astype(BF)
                return c

            lax.fori_loop(0, T // R, cp, 0)

    def whole(base):
        return pl.BlockSpec((T, LANE), lambda p, i: (0, _seg0(big) + base + p))

    tile = pl.BlockSpec((ATT_TQ, LANE), lambda p, i: (i, p))
    bspec = pl.BlockSpec((2, ATT_TQ, ATT_SPAN), lambda p, i: (p, 0, 0))
    return pl.pallas_call(
        body, name=name, grid=(H_ATT // 2, nt),
        in_specs=[pl.BlockSpec((ATT_TQ, LANE), lambda p, i: (i, _seg0(big) + COL_AQ + p)), whole(COL_AK), whole(COL_AV), bspec, tile],
        out_specs=[tile, pl.BlockSpec((T, LANE), lambda p, i: (0, p)), pl.BlockSpec((T, LANE), lambda p, i: (0, p)), bspec],
        out_shape=[_sds((T, BRANCH_W), BF)] * 3 + [_sds((H_ATT, ATT_TQ, ATT_SPAN), F32)],
        scratch_shapes=[pltpu.VMEM((T + ATT_PAD, LANE), BF), pltpu.VMEM((T + ATT_PAD, LANE), BF),
                        pltpu.VMEM((T + ATT_PAD, LANE), F32), pltpu.VMEM((T + ATT_PAD, LANE), F32)],
        compiler_params=_cp((PAR, ARB)),
    )(big, big, big, bias, dy)


def _merge_fwd(x1, big, ys, wb, wo, l, name):
    T, D = x1.shape
    tm = min(TM, T)

    def body(x_ref, gp_ref, yc_ref, yr_ref, ya_ref, wb_ref, wo_ref, x2_ref, p_ref, mg_ref):
        merged = jnp.zeros((tm, D), F32)
        for i, y_ref in enumerate((yc_ref, yr_ref, ya_ref)):
            cols = slice(i * D, (i + 1) * D)
            pb = jnp.dot(y_ref[...], wb_ref[i], preferred_element_type=F32).astype(BF)
            p_ref[:, cols] = pb
            merged = merged + _sigmoid(gp_ref[:, cols].astype(F32)) * pb.astype(F32)
        mb = merged.astype(BF)
        mg_ref[...] = mb
        x2_ref[...] = x_ref[...] + jnp.dot(mb, wo_ref[...], preferred_element_type=F32)

    tok = pl.BlockSpec((tm, D), lambda i: (i, 0))
    wide = pl.BlockSpec((tm, 3 * D), lambda i: (i, 0))
    yspec = pl.BlockSpec((tm, BRANCH_W), lambda i: (i, 0))
    return pl.pallas_call(
        body, name=name, grid=(T // tm,),
        in_specs=[tok, wide, yspec, yspec, yspec,
                  pl.BlockSpec((None, 3, BRANCH_W, D), lambda i: (l, 0, 0, 0)),
                  pl.BlockSpec((None, D, D), lambda i: (l, 0, 0))],
        out_specs=[tok, wide, tok],
        out_shape=[_sds((T, D), F32), _sds((T, 3 * D), BF), _sds((T, D), BF)],
        compiler_params=_cp((PAR,)),
    )(x1, big, *ys, wb, wo)


def _merge_bwd(dx2, big, p, wb, wo, l, name):
    T, D = dx2.shape
    tm = min(TM, T)

    def body(dx_ref, gp_ref, p_ref, wb_ref, wo_ref, dp_ref, dgp_ref, dyc_ref, dyr_ref, dya_ref, dxb_ref):
        dxb = dx_ref[...].astype(BF)
        dxb_ref[...] = dxb
        dm = lax.dot_general(dxb, wo_ref[...], NT_DIMS, preferred_element_type=F32)
        for i, dy_ref in enumerate((dyc_ref, dyr_ref, dya_ref)):
            cols = slice(i * D, (i + 1) * D)
            gt = _sigmoid(gp_ref[:, cols].astype(F32))
            dpb = (dm * gt).astype(BF)
            dp_ref[:, cols] = dpb
            dgp_ref[:, cols] = (dm * p_ref[:, cols].astype(F32) * gt * (1.0 - gt)).astype(BF)
            dy_ref[...] = lax.dot_general(dpb, wb_ref[i], NT_DIMS, preferred_element_type=F32).astype(BF)

    tok = pl.BlockSpec((tm, D), lambda i: (i, 0))
    wide = pl.BlockSpec((tm, 3 * D), lambda i: (i, 0))
    yspec = pl.BlockSpec((tm, BRANCH_W), lambda i: (i, 0))
    return pl.pallas_call(
        body, name=name, grid=(T // tm,),
        in_specs=[tok, wide, wide,
                  pl.BlockSpec((None, 3, BRANCH_W, D), lambda i: (l, 0, 0, 0)),
                  pl.BlockSpec((None, D, D), lambda i: (l, 0, 0))],
        out_specs=[wide, wide, yspec, yspec, yspec, tok],
        out_shape=[_sds((T, 3 * D), BF), _sds((T, 3 * D), BF)] + [_sds((T, BRANCH_W), BF)] * 3 + [_sds((T, D), BF)],
        compiler_params=_cp((PAR,)),
    )(dx2, big, p, wb, wo)


def _loss_head(x, tgt, fw, name):
    T, D = x.shape
    tm = min(TM, T)

    def body(x_ref, t_ref, w_ref, loss_ref, dx_ref, dw_ref):
        @pl.when(pl.program_id(0) == 0)
        def _():
            loss_ref[...] = jnp.zeros_like(loss_ref)
            dw_ref[...] = jnp.zeros_like(dw_ref)

        xv = x_ref[...]
        wv = w_ref[...]
        e = xv * _rms_r(xv) * wv - t_ref[...]
        loss_ref[...] += 0.5 * jnp.sum(jnp.mean(e * e, axis=-1, keepdims=True))
        dx, dn = _rms_bwd(e * (1.0 / D), xv, wv)
        dx_ref[...] = dx
        dw_ref[...] += dn

    tok = pl.BlockSpec((tm, D), lambda i: (i, 0))
    return pl.pallas_call(
        body, name=name, grid=(T // tm,),
        in_specs=[tok, tok, pl.BlockSpec((1, D), lambda i: (0, 0))],
        out_specs=[pl.BlockSpec((8, LANE), lambda i: (0, 0)), tok, pl.BlockSpec((1, D), lambda i: (0, 0))],
        out_shape=[_sds((8, LANE), F32), _sds((T, D), F32), _sds((1, D), F32)],
        compiler_params=_cp((ARB,)),
    )(x, tgt, fw)


def _block_rows(rows, cols):
    cap = max(8, (1 << 18) // cols)
    best = None
    for r in range(8, rows + 1, 8):
        if rows % r == 0 and r <= cap:
            best = r
    return best if best is not None else rows


def _sum4(land, name):
    _, rows, cols = land.shape
    br = _block_rows(rows, cols)

    def body(l_ref, o_ref):
        o_ref[...] = ((l_ref[3].astype(F32) + l_ref[0].astype(F32)) + l_ref[1].astype(F32)) + l_ref[2].astype(F32)

    return pl.pallas_call(
        body, name=name, grid=(rows // br,),
        in_specs=[pl.BlockSpec((4, br, cols), lambda i: (0, i, 0))],
        out_specs=pl.BlockSpec((br, cols), lambda i: (i, 0)),
        out_shape=_sds((rows, cols), F32),
        compiler_params=_cp((PAR,)),
    )(land)


def _adamw_math(w, g, m, v):
    m = ADAM_B1 * m + (1.0 - ADAM_B1) * g
    v = ADAM_B2 * v + (1.0 - ADAM_B2) * (g * g)
    m_hat = m / (1.0 - ADAM_B1 ** ADAM_STEP)
    v_hat = v / (1.0 - ADAM_B2 ** ADAM_STEP)
    delta = -ADAM_LR * (m_hat / (jnp.sqrt(v_hat) + ADAM_EPS) + ADAM_WD * w)
    return delta, m, v


def _adamw(w, ga, gb, m, v, name):
    rows, cols = w.shape
    br = _block_rows(rows, cols)
    two = gb is not None

    def body(*refs):
        if two:
            w_ref, ga_ref, gb_ref, m_ref, v_ref, g_ref, d_ref, nm_ref, nv_ref = refs
            g = ga_ref[...] + gb_ref[...]
        else:
            w_ref, ga_ref, m_ref, v_ref, g_ref, d_ref, nm_ref, nv_ref = refs
            g = ga_ref[...]
        d, nm, nv = _adamw_math(w_ref[...], g, m_ref[...], v_ref[...])
        g_ref[...] = g
        d_ref[...] = d
        nm_ref[...] = nm
        nv_ref[...] = nv

    blk = pl.BlockSpec((br, cols), lambda i: (i, 0))
    args = [w, ga] + ([gb] if two else []) + [m, v]
    return pl.pallas_call(
        body, name=name, grid=(rows // br,),
        in_specs=[blk] * len(args), out_specs=[blk] * 4,
        out_shape=[_sds((rows, cols), F32)] * 4,
        compiler_params=_cp((PAR,)),
    )(*args)


def _place():
    return lax.axis_index("x"), lax.axis_index("y"), lax.axis_index("c")


def _other_chips(x, y):
    return [(1 - x, y), (x, 1 - y), (1 - x, 1 - y)]


def _gather_weights(ws, name):
    n = len(ws)

    def body(*refs):
        w_refs, o_refs = refs[:n], refs[n:2 * n]
        lsem, ssem, rsem = refs[2 * n:]
        x, y, c = _place()
        mine = 2 * x + y
        local = [pltpu.make_async_copy(w_refs[k], o_refs[k].at[mine], lsem.at[k]) for k in range(n)]
        for cp in local:
            cp.start()

        def copy(j, k, chip, slot):
            return pltpu.make_async_remote_copy(
                src_ref=w_refs[k], dst_ref=o_refs[k].at[slot], send_sem=ssem.at[j, k], recv_sem=rsem.at[j, k],
                device_id=(chip[0], chip[1], c), device_id_type=MESH)

        chips = _other_chips(x, y)
        sends = [copy(j, k, chip, mine) for j, chip in enumerate(chips) for k in range(n)]
        for cp in sends:
            cp.start()
        for j, chip in enumerate(chips):
            for k in range(n):
                copy(j, k, chip, 2 * chip[0] + chip[1]).wait_recv()
        for cp in sends:
            cp.wait_send()
        for cp in local:
            cp.wait()

    hbm = pl.BlockSpec(memory_space=pl.ANY)
    return pl.pallas_call(
        body, name=name,
        in_specs=[hbm] * n, out_specs=[hbm] * n,
        out_shape=[_sds((N_SHARD,) + w.shape, w.dtype) for w in ws],
        scratch_shapes=[pltpu.SemaphoreType.DMA((n,)), pltpu.SemaphoreType.DMA((3, n)), pltpu.SemaphoreType.DMA((3, n))],
    )(*ws)


def _scatter_grads(gs, name):
    n = len(gs)

    def body(*refs):
        g_refs, o_refs = refs[:n], refs[n:2 * n]
        lsem, ssem, rsem = refs[2 * n:]
        x, y, c = _place()
        mine = 2 * x + y
        local = [pltpu.make_async_copy(g_refs[k].at[mine], o_refs[k].at[3], lsem.at[k]) for k in range(n)]
        for cp in local:
            cp.start()

        def copy(j, k, chip):
            return pltpu.make_async_remote_copy(
                src_ref=g_refs[k].at[2 * chip[0] + chip[1]], dst_ref=o_refs[k].at[j],
                send_sem=ssem.at[j, k], recv_sem=rsem.at[j, k],
                device_id=(chip[0], chip[1], c), device_id_type=MESH)

        chips = _other_chips(x, y)
        sends = [copy(j, k, chip) for j, chip in enumerate(chips) for k in range(n)]
        for cp in sends:
            cp.start()
        for cp in sends:
            cp.wait_recv()
        for cp in sends:
            cp.wait_send()
        for cp in local:
            cp.wait()

    hbm = pl.BlockSpec(memory_space=pl.ANY)
    return pl.pallas_call(
        body, name=name,
        in_specs=[hbm] * n, out_specs=[hbm] * n,
        out_shape=[_sds(g.shape, g.dtype) for g in gs],
        scratch_shapes=[pltpu.SemaphoreType.DMA((n,)), pltpu.SemaphoreType.DMA((3, n)), pltpu.SemaphoreType.DMA((3, n))],
    )(*gs)


def _swap_cores(vs, name):
    n = len(vs)

    def body(*refs):
        v_refs, o_refs = refs[:n], refs[n:2 * n]
        ssem, rsem = refs[2 * n:]
        x, y, c = _place()
        copies = [pltpu.make_async_remote_copy(
            src_ref=v_refs[k], dst_ref=o_refs[k], send_sem=ssem.at[k], recv_sem=rsem.at[k],
            device_id=(x, y, 1 - c), device_id_type=MESH) for k in range(n)]
        for cp in copies:
            cp.start()
        for cp in copies:
            cp.wait()

    hbm = pl.BlockSpec(memory_space=pl.ANY)
    return pl.pallas_call(
        body, name=name,
        in_specs=[hbm] * n, out_specs=[hbm] * n,
        out_shape=[_sds(v.shape, v.dtype) for v in vs],
        scratch_shapes=[pltpu.SemaphoreType.DMA((n,)), pltpu.SemaphoreType.DMA((n,))],
    )(*vs)


def _allreduce_small(v, name):
    rows = v.shape[0]
    flips = [(fx, fy, fc) for fx in (0, 1) for fy in (0, 1) for fc in (0, 1) if fx or fy or fc]

    def body(v_ref, o_ref, all_s, ssem, rsem):
        x, y, c = _place()

        def peer(f):
            return (x + f[0] - 2 * x * f[0], y + f[1] - 2 * y * f[1], c + f[2] - 2 * c * f[2])

        def slot(p):
            return all_s.at[4 * p[0] + 2 * p[1] + p[2]]

        def copy(k, f, owner):
            return pltpu.make_async_remote_copy(
                src_ref=v_ref, dst_ref=slot(owner), send_sem=ssem.at[k], recv_sem=rsem.at[k],
                device_id=peer(f), device_id_type=MESH)

        sends = [copy(k, f, (x, y, c)) for k, f in enumerate(flips)]
        for cp in sends:
            cp.start()
        all_s[4 * x + 2 * y + c] = v_ref[...]
        for k, f in enumerate(flips):
            copy(k, f, peer(f)).wait_recv()
        for cp in sends:
            cp.wait_send()
        acc = all_s[0]
        for d in range(1, 8):
            acc = acc + all_s[d]
        o_ref[...] = acc

    return pl.pallas_call(
        body, name=name,
        in_specs=[pl.BlockSpec(memory_space=pltpu.VMEM)],
        out_specs=pl.BlockSpec(memory_space=pltpu.VMEM),
        out_shape=_sds((rows, LANE), F32),
        scratch_shapes=[pltpu.VMEM((8, rows, LANE), F32), pltpu.SemaphoreType.DMA((7,)), pltpu.SemaphoreType.DMA((7,))],
    )(v)


BIG_NAMES = ("ffn1_w_gate", "ffn1_w_up", "ffn1_w_down", "w_in", "w_branch", "w_merge_gate", "w_out",
             "ffn2_w_gate", "ffn2_w_up", "ffn2_w_down")


def _local_step(x, tgt, W, small, convw_full):
    T, D = x.shape
    L = W["w_in"].shape[1]
    ns = N_SHARD
    fs = W["ffn1_w_gate"].shape[-1]
    ics = W["w_in"].shape[-1]
    nb = 3 * D + ns * ics
    dq = D // ns

    wmg_full = jnp.transpose(W["w_merge_gate"], (1, 2, 0, 3, 4)).reshape(L, 3, D, D)
    win_full = jnp.transpose(W["w_in"], (1, 2, 0, 3)).reshape(L, D, ns * ics)
    wbig = jnp.concatenate([jnp.transpose(wmg_full, (0, 2, 1, 3)).reshape(L, D, 3 * D), win_full], axis=-1)
    wb_full = jnp.transpose(W["w_branch"], (1, 2, 3, 0, 4)).reshape(L, 3, BRANCH_W, D)
    wo_full = jnp.transpose(W["w_out"], (1, 0, 2, 3)).reshape(L, D, D)

    tb = _ret_tables(T)
    rb_pad = jnp.pad(small["rel_bias"], ((0, 0), (0, 0), (0, RB_PAD - N_REL)))

    saved = []
    h = x
    for l in range(L):
        s = {"x0": h}
        x1, s["g1"], s["u1"] = _ffn_fwd(h, small["ffn1_norm"][l][None], W["ffn1_w_gate"], W["ffn1_w_up"],
                                        W["ffn1_w_down"], l, f"ffn1_fwd_{l}")
        s["x1"] = x1
        big, s["h"] = _inproj_fwd(x1, small["mix_norm"][l][None], wbig, l, f"inproj_fwd_{l}")
        s["big"] = big
        s["bias"] = jnp.transpose(_relbias_expand(rb_pad[l], f"relbias_expand_{l}"), (1, 0, 2))
        s["yc"] = _conv_fwd(big, convw_full[l], f"conv_fwd_{l}")
        s["yr"], s["o"], s["st"] = _ret_fwd(big, tb, f"ret_fwd_{l}")
        s["ya"] = _att_fwd(big, s["bias"], f"att_fwd_{l}")
        x2, s["p"], s["mg"] = _merge_fwd(x1, big, (s["yc"], s["yr"], s["ya"]), wb_full, wo_full, l, f"merge_fwd_{l}")
        s["x2"] = x2
        h, s["g2"], s["u2"] = _ffn_fwd(x2, small["ffn2_norm"][l][None], W["ffn2_w_gate"], W["ffn2_w_up"],
                                       W["ffn2_w_down"], l, f"ffn2_fwd_{l}")
        saved.append(s)

    loss_p, dx, d_final = _loss_head(h, tgt, small["final_norm"][None], "loss_head")

    G = {k: None for k in BIG_NAMES}
    gs = {"final_norm": d_final[0]}
    for k in ("ffn1_norm", "mix_norm", "ffn2_norm", "rel_bias", "conv_w"):
        gs[k] = [None] * L
    tk = min(2048, T)
    nk = T // tk

    def ffn_grads(pre, l, hb, dgv, duv, av, dacc):
        hspec = pl.BlockSpec((tk, D), lambda p, q, k: (k, 0))
        sspec = pl.BlockSpec((None, tk, fs), lambda p, q, k: (p, k, 0))
        for nm, bv in ((pre + "_w_gate", dgv), (pre + "_w_up", duv)):
            G[nm] = _tn(hb, bv, hspec, sspec, _sds((ns, L, D, fs), BF),
                        pl.BlockSpec((None, None, D, fs), lambda p, q, k: (p, l, 0, 0)),
                        (ns, 1, nk), f"d{nm}_{l}", prev=G[nm])
        nm = pre + "_w_down"
        G[nm] = _tn(av, dacc, sspec, hspec, _sds((ns, L, fs, D), BF),
                    pl.BlockSpec((None, None, fs, D), lambda p, q, k: (p, l, 0, 0)),
                    (ns, 1, nk), f"d{nm}_{l}", prev=G[nm])

    for l in reversed(range(L)):
        s = saved[l]
        dx, dgv, duv, av, hb, dacc, dn = _ffn_bwd(dx, s["x2"], small["ffn2_norm"][l][None], s["g2"], s["u2"],
                                                  W["ffn2_w_gate"], W["ffn2_w_up"], W["ffn2_w_down"], l, f"ffn2_bwd_{l}")
        gs["ffn2_norm"][l] = dn[0]
        ffn_grads("ffn2", l, hb, dgv, duv, av, dacc)
        dp, dgp, dyc, dyr, dya, dxb = _merge_bwd(dx, s["big"], s["p"], wb_full, wo_full, l, f"merge_bwd_{l}")
        G["w_out"] = _tn(s["mg"], dxb,
                         pl.BlockSpec((tk, dq), lambda p, q, k: (k, p)), pl.BlockSpec((tk, D), lambda p, q, k: (k, 0)),
                         _sds((ns, L, dq, D), BF), pl.BlockSpec((None, None, dq, D), lambda p, q, k: (p, l, 0, 0)),
                         (ns, 1, nk), f"dw_out_{l}", prev=G["w_out"])
        for i, yv in enumerate((s["yc"], s["yr"], s["ya"])):
            G["w_branch"] = _tn(yv, dp,
                                pl.BlockSpec((tk, BRANCH_W), lambda p, q, k: (k, 0)),
                                pl.BlockSpec((tk, dq), lambda p, q, k, i=i: (k, i * ns + p)),
                                _sds((ns, L, 3, BRANCH_W, dq), BF),
                                pl.BlockSpec((None, None, None, BRANCH_W, dq), lambda p, q, k, i=i: (p, l, i, 0, 0)),
                                (ns, 1, nk), f"dw_branch{i}_{l}", prev=G["w_branch"])
        dcu, dcb, dcc, dcw = _conv_bwd(s["big"], dyc, convw_full[l], f"conv_bwd_{l}")
        gs["conv_w"][l] = dcw
        drq, drk, drv, drg = _ret_bwd(s["big"], s["o"], s["st"], dyr, tb, f"ret_bwd_{l}")
        daq, dak, dav, dbias = _att_bwd(s["big"], s["bias"], dya, f"att_bwd_{l}")
        gs["rel_bias"][l] = _relbias_grad(jnp.transpose(dbias, (1, 0, 2)), f"relbias_grad_{l}")[:, :N_REL]
        dbig = jnp.concatenate([dgp, dcu, dcb, dcc, drq, drk, drv, drg, daq, dak, dav], axis=1)
        nq = ics // 256
        G["w_in"] = _tn(s["h"], dbig,
                        pl.BlockSpec((tk, D), lambda p, q, k: (k, 0)),
                        pl.BlockSpec((tk, 256), lambda p, q, k: (k, 3 * D // 256 + p * nq + q)),
                        _sds((ns, L, D, ics), BF),
                        pl.BlockSpec((None, None, D, 256), lambda p, q, k: (p, l, 0, q)),
                        (ns, nq, nk), f"dw_in_{l}", prev=G["w_in"])
        G["w_merge_gate"] = _tn(s["h"], dbig,
                                pl.BlockSpec((tk, dq), lambda p, q, k: (k, p)),
                                pl.BlockSpec((tk, D), lambda p, q, k: (k, q)),
                                _sds((ns, L, 3, dq, D), BF),
                                pl.BlockSpec((None, None, None, dq, D), lambda p, q, k: (p, l, q, 0, 0)),
                                (ns, 3, nk), f"dw_merge_gate_{l}", prev=G["w_merge_gate"])
        dx, dn = _inproj_bwd(dbig, wbig, s["x1"], small["mix_norm"][l][None], dx, l, f"inproj_bwd_{l}")
        gs["mix_norm"][l] = dn[0]
        dx, dgv, duv, av, hb, dacc, dn = _ffn_bwd(dx, s["x0"], small["ffn1_norm"][l][None], s["g1"], s["u1"],
                                                  W["ffn1_w_gate"], W["ffn1_w_up"], W["ffn1_w_down"], l, f"ffn1_bwd_{l}")
        gs["ffn1_norm"][l] = dn[0]
        ffn_grads("ffn1", l, hb, dgv, duv, av, dacc)

    for k in ("ffn1_norm", "mix_norm", "ffn2_norm", "rel_bias", "conv_w"):
        gs[k] = jnp.stack(gs[k])
    return loss_p, dx, G, gs


SMALL_NAMES = ("ffn1_norm", "mix_norm", "ffn2_norm", "final_norm", "rel_bias", "conv_w")
W_NAMES = ("ffn1_norm", "ffn1_w_gate", "ffn1_w_up", "ffn1_w_down", "mix_norm", "w_in", "conv_w", "rel_bias", "w_branch",
           "w_merge_gate", "w_out", "ffn2_norm", "ffn2_w_gate", "ffn2_w_up", "ffn2_w_down", "final_norm")


def _as2d(a):
    return a.reshape(1, -1) if a.ndim == 1 else a.reshape(-1, a.shape[-1])


def kernel(x, ffn1_norm, ffn1_w_gate, ffn1_w_up, ffn1_w_down, mix_norm, w_in, conv_w, rel_bias, w_branch, w_merge_gate, w_out, ffn2_norm, ffn2_w_gate, ffn2_w_up, ffn2_w_down, final_norm, loss_target, m_ffn1_norm, m_ffn1_w_gate, m_ffn1_w_up, m_ffn1_w_down, m_mix_norm, m_w_in, m_conv_w, m_rel_bias, m_w_branch, m_w_merge_gate, m_w_out, m_ffn2_norm, m_ffn2_w_gate, m_ffn2_w_up, m_ffn2_w_down, m_final_norm, v_ffn1_norm, v_ffn1_w_gate, v_ffn1_w_up, v_ffn1_w_down, v_mix_norm, v_w_in, v_conv_w, v_rel_bias, v_w_branch, v_w_merge_gate, v_w_out, v_ffn2_norm, v_ffn2_w_gate, v_ffn2_w_up, v_ffn2_w_down, v_final_norm):
    given = dict(locals())
    w = {n: given[n] for n in W_NAMES}
    m = {n: given["m_" + n] for n in W_NAMES}
    v = {n: given["v_" + n] for n in W_NAMES}
    my_chip = 2 * lax.axis_index("x") + lax.axis_index("y")

    gathered = _gather_weights([w[n].astype(BF) for n in BIG_NAMES] + [conv_w], "gather_weights")
    W = dict(zip(BIG_NAMES, gathered[:-1]))
    convw_full = jnp.transpose(gathered[-1], (1, 2, 0, 3)).reshape(conv_w.shape[0], conv_w.shape[1], -1)

    small = {n: w[n] for n in ("ffn1_norm", "mix_norm", "ffn2_norm", "final_norm", "rel_bias")}
    loss_p, grad_x, G, gs = _local_step(x[0], loss_target[0], W, small, convw_full)

    landed = _scatter_grads([G[n] for n in BIG_NAMES], "scatter_grads")
    sums = [_sum4(a.reshape(4, -1, a.shape[-1]), f"sum4_{n}") for n, a in zip(BIG_NAMES, landed)]
    others = _swap_cores(sums, "swap_cores")

    parts = [gs["ffn1_norm"].reshape(-1), gs["mix_norm"].reshape(-1), gs["ffn2_norm"].reshape(-1),
             gs["final_norm"].reshape(-1), gs["rel_bias"].reshape(-1), gs["conv_w"].reshape(-1), loss_p[0]]
    sizes = [p.shape[0] for p in parts]
    flat = jnp.concatenate(parts)
    rows = -(-flat.shape[0] // (8 * LANE)) * 8
    flat = jnp.pad(flat, (0, rows * LANE - flat.shape[0])).reshape(rows, LANE)
    red = _allreduce_small(flat, "allreduce_small").reshape(-1)
    offs = [0]
    for sz in sizes:
        offs.append(offs[-1] + sz)
    sm = {}
    for i, n in enumerate(("ffn1_norm", "mix_norm", "ffn2_norm", "final_norm", "rel_bias", "conv_w")):
        sm[n] = red[offs[i]:offs[i + 1]]
    loss = red[offs[6]]
    sm["conv_w"] = lax.dynamic_slice_in_dim(sm["conv_w"].reshape(conv_w.shape[0], conv_w.shape[1], -1),
                                            my_chip * conv_w.shape[2], conv_w.shape[2], axis=2)

    grads, deltas, new_m, new_v = {}, {}, {}, {}
    big_sum = dict(zip(BIG_NAMES, zip(sums, others)))
    for n in W_NAMES:
        shape = w[n].shape
        if n in big_sum:
            ga, gb = big_sum[n]
        else:
            ga, gb = _as2d(sm[n].reshape(shape)), None
        out = _adamw(_as2d(w[n]), ga, gb, _as2d(m[n]), _as2d(v[n]), f"adamw_{n}")
        grads[n], deltas[n], new_m[n], new_v[n] = (o.reshape(shape) for o in out)

    return (loss, grad_x[None], *[grads[n] for n in W_NAMES], *[deltas[n] for n in W_NAMES],
            *[new_m[n] for n in W_NAMES], *[new_v[n] for n in W_NAMES])
```

```python
import functools
import math

import jax
import jax.numpy as jnp
from jax import lax
from jax.experimental import pallas as pl
from jax.experimental.pallas import tpu as pltpu

F32 = jnp.float32
BF = jnp.bfloat16
MESH = pl.DeviceIdType.MESH
ARB = "arbitrary"
PAR = "parallel"

EPS = 1e-6
NEG_INF = -1e30
ROPE_BASE = 10000.0
CHUNK = 64
BRANCH_W = 512
H_RET = 4
DK_RET = 128
H_ATT = 8
DH_ATT = 64
N_PREV = 8
REL_CLIP = 128
N_REL = 2 * REL_CLIP + 1
N_SHARD = 4
LANE = 128
RET_L = 256
ATT_TQ = 128
ATT_PAD = N_PREV * CHUNK
ATT_SPAN = ATT_TQ + ATT_PAD
ATT_TOEP = 2 * REL_CLIP
RB_PAD = 264
TM = 512

ADAM_LR = 0.001
ADAM_B1 = 0.9
ADAM_B2 = 0.999
ADAM_EPS = 1e-08
ADAM_WD = 0.01
ADAM_STEP = 10

NT_DIMS = (((1,), (1,)), ((), ()))
TN_DIMS = (((0,), (0,)), ((), ()))


def _cp(sem, vmem_mb=48):
    return pltpu.CompilerParams(dimension_semantics=sem, vmem_limit_bytes=vmem_mb << 20)


def _sds(shape, dtype):
    return jax.ShapeDtypeStruct(tuple(shape), dtype)


def _rms_r(x):
    return lax.rsqrt(jnp.mean(x * x, axis=-1, keepdims=True) + EPS)


def _sigmoid(x):
    return jax.nn.sigmoid(x)


def _rms_bwd(dh, xv, nw):
    r = _rms_r(xv)
    xh = xv * r
    dxh = dh * nw
    dx = r * (dxh - xh * jnp.mean(dxh * xh, axis=-1, keepdims=True))
    return dx, jnp.sum(dh * xh, axis=0, keepdims=True)


def _place():
    return lax.axis_index("x"), lax.axis_index("y"), lax.axis_index("c")


def _other_chips(x, y):
    return [(1 - x, y), (x, 1 - y), (1 - x, 1 - y)]


class _Pieces:
    def __init__(self, srcs, scatter):
        self.srcs = list(srcs)
        self.scatter = scatter
        n = len(self.srcs)
        self.out_shape = [_sds(s.shape if scatter else (N_SHARD,) + s.shape, s.dtype) for s in self.srcs]
        self.scratch = [pltpu.SemaphoreType.DMA((n,)), pltpu.SemaphoreType.DMA((3, n)), pltpu.SemaphoreType.DMA((3, n))]

    def _copies(self, src, dst, sems):
        lsem, ssem, rsem = sems
        x, y, c = _place()
        mine = 2 * x + y
        n = len(src)

        def remote(j, k, chip, s_ref, d_ref):
            return pltpu.make_async_remote_copy(
                src_ref=s_ref, dst_ref=d_ref, send_sem=ssem.at[j, k], recv_sem=rsem.at[j, k],
                device_id=(chip[0], chip[1], c), device_id_type=MESH)

        chips = list(enumerate(_other_chips(x, y)))
        if self.scatter:
            local = [pltpu.make_async_copy(src[k].at[mine], dst[k].at[3], lsem.at[k]) for k in range(n)]
            sends = [remote(j, k, ch, src[k].at[2 * ch[0] + ch[1]], dst[k].at[j]) for j, ch in chips for k in range(n)]
            recvs = sends
        else:
            local = [pltpu.make_async_copy(src[k], dst[k].at[mine], lsem.at[k]) for k in range(n)]
            sends = [remote(j, k, ch, src[k], dst[k].at[mine]) for j, ch in chips for k in range(n)]
            recvs = [remote(j, k, ch, src[k], dst[k].at[2 * ch[0] + ch[1]]) for j, ch in chips for k in range(n)]
        return local, sends, recvs

    def start(self, src, dst, sems):
        local, sends, _ = self._copies(src, dst, sems)
        for cp in local + sends:
            cp.start()

    def wait(self, src, dst, sems):
        local, sends, recvs = self._copies(src, dst, sems)
        for cp in recvs:
            cp.wait_recv()
        for cp in sends:
            cp.wait_send()
        for cp in local:
            cp.wait()


def _call(body, *, name, args, in_specs, out_specs, out_shape, grid=(), scratch_shapes=(), sem=None, comm=None,
          vmem_mb=48):
    in_specs, out_specs, out_shape = list(in_specs), list(out_specs), list(out_shape)
    scratch, args = list(scratch_shapes), list(args)
    n_in, n_out, n_scr = len(in_specs), len(out_specs), len(scratch)
    if comm is None:
        def kernel_body(*refs):
            body(*refs)
    else:
        c_in, c_out = len(comm.srcs), len(comm.out_shape)

        def kernel_body(*refs):
            o0 = n_in + c_in
            s0 = o0 + n_out + c_out
            cin, cout, sems = refs[n_in:o0], refs[o0 + n_out:s0], refs[s0 + n_scr:]
            main = refs[:n_in] + refs[o0:o0 + n_out] + refs[s0:s0 + n_scr]
            if grid:
                ids = [pl.program_id(a) for a in range(len(grid))]
                first = functools.reduce(lambda p, q: p & q, [i == 0 for i in ids])
                last = functools.reduce(lambda p, q: p & q, [i == g - 1 for i, g in zip(ids, grid)])

                @pl.when(first)
                def _():
                    comm.start(cin, cout, sems)

                body(*main)

                @pl.when(last)
                def _():
                    comm.wait(cin, cout, sems)
            else:
                comm.start(cin, cout, sems)
                body(*main)
                comm.wait(cin, cout, sems)

        hbm = pl.BlockSpec(memory_space=pl.ANY)
        in_specs += [hbm] * c_in
        out_specs += [hbm] * c_out
        out_shape += comm.out_shape
        scratch += comm.scratch
        args += comm.srcs
    params = dict(vmem_limit_bytes=vmem_mb << 20)
    if grid:
        params["dimension_semantics"] = sem
    outs = pl.pallas_call(
        kernel_body, name=name, grid=grid, in_specs=in_specs, out_specs=out_specs, out_shape=out_shape,
        scratch_shapes=scratch, compiler_params=pltpu.CompilerParams(**params),
    )(*args)
    return list(outs[:n_out]), list(outs[n_out:])


def _comm_alone(comm, name):
    return _call(lambda: None, name=name, args=[], in_specs=[], out_specs=[], out_shape=[], comm=comm)[1]


def _ffn_fwd(x, nw, wg, wu, wd, name, comm=None):
    T, D = x.shape
    ns, _, fs = wg.shape
    tm = min(TM, T)

    def body(x_ref, nw_ref, wg_ref, wu_ref, wd_ref, xo_ref, g_ref, u_ref, h_s, acc_s):
        j = pl.program_id(1)

        @pl.when(j == 0)
        def _():
            xv = x_ref[...]
            h_s[...] = (xv * _rms_r(xv) * nw_ref[...]).astype(BF)
            acc_s[...] = jnp.zeros_like(acc_s)

        h = h_s[...]
        gb = jnp.dot(h, wg_ref[...], preferred_element_type=F32).astype(BF)
        ub = jnp.dot(h, wu_ref[...], preferred_element_type=F32).astype(BF)
        g_ref[...] = gb
        u_ref[...] = ub
        g = gb.astype(F32)
        a = (g * _sigmoid(g) * ub.astype(F32)).astype(BF)
        acc_s[...] += jnp.dot(a, wd_ref[...], preferred_element_type=F32)

        @pl.when(j == ns - 1)
        def _():
            xo_ref[...] = x_ref[...] + 0.5 * acc_s[...]

    wspec = pl.BlockSpec((None, D, fs), lambda i, j: (j, 0, 0))
    return _call(
        body, name=name, grid=(T // tm, ns), args=(x, nw, wg, wu, wd), comm=comm,
        in_specs=[pl.BlockSpec((tm, D), lambda i, j: (i, 0)),
                  pl.BlockSpec((1, D), lambda i, j: (0, 0)),
                  wspec, wspec,
                  pl.BlockSpec((None, fs, D), lambda i, j: (j, 0, 0))],
        out_specs=[pl.BlockSpec((tm, D), lambda i, j: (i, 0)),
                   pl.BlockSpec((None, tm, fs), lambda i, j: (j, i, 0)),
                   pl.BlockSpec((None, tm, fs), lambda i, j: (j, i, 0))],
        out_shape=[_sds((T, D), F32), _sds((ns, T, fs), BF), _sds((ns, T, fs), BF)],
        scratch_shapes=[pltpu.VMEM((tm, D), BF), pltpu.VMEM((tm, D), F32)],
        sem=(ARB, ARB))


def _ffn_bwd(dxo, x, nw, g, u, wg, wu, wd, name, comm=None):
    T, D = x.shape
    ns, _, fs = wg.shape
    tm = min(TM, T)

    def body(dxo_ref, x_ref, nw_ref, g_ref, u_ref, wg_ref, wu_ref, wd_ref,
             dx_ref, dg_ref, du_ref, a_ref, h_ref, dacc_ref, dnw_ref, dacc_s, acc_s):
        i = pl.program_id(0)
        j = pl.program_id(1)

        @pl.when(j == 0)
        def _():
            xv = x_ref[...]
            h_ref[...] = (xv * _rms_r(xv) * nw_ref[...]).astype(BF)
            db = (0.5 * dxo_ref[...]).astype(BF)
            dacc_ref[...] = db
            dacc_s[...] = db
            acc_s[...] = jnp.zeros_like(acc_s)

        @pl.when((i == 0) & (j == 0))
        def _():
            dnw_ref[...] = jnp.zeros_like(dnw_ref)

        da = lax.dot_general(dacc_s[...], wd_ref[...], NT_DIMS, preferred_element_type=F32)
        gv = g_ref[...].astype(F32)
        uv = u_ref[...].astype(F32)
        s = _sigmoid(gv)
        sg = gv * s
        a_ref[...] = (sg * uv).astype(BF)
        dub = (da * sg).astype(BF)
        dgb = (da * uv * (s * (1.0 + gv * (1.0 - s)))).astype(BF)
        dg_ref[...] = dgb
        du_ref[...] = dub
        acc_s[...] += (lax.dot_general(dgb, wg_ref[...], NT_DIMS, preferred_element_type=F32)
                       + lax.dot_general(dub, wu_ref[...], NT_DIMS, preferred_element_type=F32))

        @pl.when(j == ns - 1)
        def _():
            dx, dn = _rms_bwd(acc_s[...], x_ref[...], nw_ref[...])
            dx_ref[...] = dxo_ref[...] + dx
            dnw_ref[...] += dn

    tok = pl.BlockSpec((tm, D), lambda i, j: (i, 0))
    row = pl.BlockSpec((1, D), lambda i, j: (0, 0))
    hid = pl.BlockSpec((None, tm, fs), lambda i, j: (j, i, 0))
    wspec = pl.BlockSpec((None, D, fs), lambda i, j: (j, 0, 0))
    return _call(
        body, name=name, grid=(T // tm, ns), args=(dxo, x, nw, g, u, wg, wu, wd), comm=comm,
        in_specs=[tok, tok, row, hid, hid, wspec, wspec,
                  pl.BlockSpec((None, fs, D), lambda i, j: (j, 0, 0))],
        out_specs=[tok, hid, hid, hid, tok, tok, row],
        out_shape=[_sds((T, D), F32), _sds((ns, T, fs), BF), _sds((ns, T, fs), BF), _sds((ns, T, fs), BF),
                   _sds((T, D), BF), _sds((T, D), BF), _sds((1, D), F32)],
        scratch_shapes=[pltpu.VMEM((tm, D), BF), pltpu.VMEM((tm, D), F32)],
        sem=(ARB, ARB))


def _tn(a, b, a_spec, b_spec, out_shape, out_spec, grid, name, prev=None):
    nk = grid[-1]
    acc_shape = tuple(d for d in out_spec.block_shape if d is not None)

    def body(*refs):
        a_ref, b_ref = refs[0], refs[1]
        o_ref, acc = refs[-2], refs[-1]
        k = pl.program_id(2)
        prod = lax.dot_general(a_ref[...], b_ref[...], TN_DIMS, preferred_element_type=F32)

        @pl.when(k == 0)
        def _():
            acc[...] = prod

        @pl.when(k > 0)
        def _():
            acc[...] += prod

        @pl.when(k == nk - 1)
        def _():
            o_ref[...] = acc[...].astype(o_ref.dtype)

    in_specs = [a_spec, b_spec]
    args = [a, b]
    aliases = {}
    if prev is not None:
        in_specs.append(pl.BlockSpec(memory_space=pl.ANY))
        args.append(prev)
        aliases = {2: 0}
    return pl.pallas_call(
        body, name=name, grid=grid, in_specs=in_specs, out_specs=out_spec,
        out_shape=out_shape, scratch_shapes=[pltpu.VMEM(acc_shape, F32)],
        input_output_aliases=aliases, compiler_params=_cp((PAR, PAR, ARB)),
    )(*args)


def _inproj_fwd(x, nw, wbig, name):
    T, D = x.shape
    nb = wbig.shape[-1]
    tm = min(2 * TM, T)
    bn = min(2048, nb)

    def body(x_ref, nw_ref, w_ref, o_ref, h_ref, h_s):
        @pl.when(pl.program_id(1) == 0)
        def _():
            xv = x_ref[...]
            hb = (xv * _rms_r(xv) * nw_ref[...]).astype(BF)
            h_s[...] = hb
            h_ref[...] = hb

        o_ref[...] = jnp.dot(h_s[...], w_ref[...], preferred_element_type=F32).astype(BF)

    return pl.pallas_call(
        body, name=name, grid=(T // tm, nb // bn),
        in_specs=[pl.BlockSpec((tm, D), lambda i, n: (i, 0)),
                  pl.BlockSpec((1, D), lambda i, n: (0, 0)),
                  pl.BlockSpec((D, bn), lambda i, n: (0, n))],
        out_specs=[pl.BlockSpec((tm, bn), lambda i, n: (i, n)),
                   pl.BlockSpec((tm, D), lambda i, n: (i, 0))],
        out_shape=[_sds((T, nb), BF), _sds((T, D), BF)],
        scratch_shapes=[pltpu.VMEM((tm, D), BF)],
        compiler_params=_cp((PAR, ARB)),
    )(x, nw, wbig)


def _inproj_bwd(dbig, wbig, x, nw, dxin, name):
    T, D = x.shape
    nb = wbig.shape[-1]
    tm = min(TM, T)
    tk = min(2048, nb)
    nk = nb // tk

    def body(a_ref, w_ref, x_ref, nw_ref, dxin_ref, dx_ref, dnw_ref, acc_s):
        i = pl.program_id(0)
        k = pl.program_id(1)
        prod = lax.dot_general(a_ref[...], w_ref[...], NT_DIMS, preferred_element_type=F32)

        @pl.when((i == 0) & (k == 0))
        def _():
            dnw_ref[...] = jnp.zeros_like(dnw_ref)

        @pl.when(k == 0)
        def _():
            acc_s[...] = prod

        @pl.when(k > 0)
        def _():
            acc_s[...] += prod

        @pl.when(k == nk - 1)
        def _():
            dx, dn = _rms_bwd(acc_s[...], x_ref[...], nw_ref[...])
            dx_ref[...] = dxin_ref[...] + dx
            dnw_ref[...] += dn

    tok = pl.BlockSpec((tm, D), lambda i, k: (i, 0))
    row = pl.BlockSpec((1, D), lambda i, k: (0, 0))
    return pl.pallas_call(
        body, name=name, grid=(T // tm, nk),
        in_specs=[pl.BlockSpec((tm, tk), lambda i, k: (i, k)),
                  pl.BlockSpec((D, tk), lambda i, k: (0, k)),
                  tok, row, tok],
        out_specs=[tok, row],
        out_shape=[_sds((T, D), F32), _sds((1, D), F32)],
        scratch_shapes=[pltpu.VMEM((tm, D), F32)],
        compiler_params=_cp((ARB, ARB)),
    )(dbig, wbig, x, nw, dxin)


CONV_R = 512
COL_CU, COL_CB, COL_CC = 0, 4, 8
COL_RQ, COL_RK, COL_RV, COL_RG = 12, 16, 20, 24
COL_AQ, COL_AK, COL_AV = 28, 32, 36


def _seg0(big):
    return (big.shape[1] - 10 * BRANCH_W) // LANE


def _conv_fwd(big, cw, name):
    T = big.shape[0]
    R = min(CONV_R, T)

    def body(cu_ref, cb_ref, cc_ref, w_ref, y_ref, z_s):
        z_s[pl.ds(0, 8), :] = jnp.zeros((8, LANE), F32)

        def fill(t, c):
            sl = pl.ds(pl.multiple_of(t * R, R), R)
            z_s[pl.ds(pl.multiple_of(t * R + 8, 8), R), :] = cc_ref[sl, :].astype(F32) * cu_ref[sl, :].astype(F32)
            return c

        lax.fori_loop(0, T // R, fill, 0)
        w0, w1, w2 = w_ref[0:1, :], w_ref[1:2, :], w_ref[2:3, :]

        def step(t, c):
            zz = z_s[pl.ds(pl.multiple_of(t * R, R), R + 8), :]
            z0 = zz[8:]
            z1 = pltpu.roll(zz, 1, 0)[8:]
            z2 = pltpu.roll(zz, 2, 0)[8:]
            sl = pl.ds(pl.multiple_of(t * R, R), R)
            y_ref[sl, :] = (cb_ref[sl, :].astype(F32) * (w2 * z0 + w1 * z1 + w0 * z2)).astype(BF)
            return c

        lax.fori_loop(0, T // R, step, 0)

    def col(base):
        return pl.BlockSpec((T, LANE), lambda j: (0, _seg0(big) + base + j))

    return pl.pallas_call(
        body, name=name, grid=(BRANCH_W // LANE,),
        in_specs=[col(COL_CU), col(COL_CB), col(COL_CC), pl.BlockSpec((3, LANE), lambda j: (0, j))],
        out_specs=pl.BlockSpec((T, LANE), lambda j: (0, j)),
        out_shape=_sds((T, BRANCH_W), BF),
        scratch_shapes=[pltpu.VMEM((T + 8, LANE), F32)],
        compiler_params=_cp((PAR,)),
    )(big, big, big, cw)


def _conv_bwd(big, dy, cw, name):
    T = big.shape[0]
    R = min(CONV_R, T)

    def body(cu_ref, cb_ref, cc_ref, dy_ref, w_ref, dcu_ref, dcb_ref, dcc_ref, dw_ref, z_s, d_s):
        z_s[pl.ds(0, 8), :] = jnp.zeros((8, LANE), F32)
        d_s[pl.ds(T, 8), :] = jnp.zeros((8, LANE), F32)

        def fill(t, c):
            sl = pl.ds(pl.multiple_of(t * R, R), R)
            z_s[pl.ds(pl.multiple_of(t * R + 8, 8), R), :] = cc_ref[sl, :].astype(F32) * cu_ref[sl, :].astype(F32)
            d_s[sl, :] = dy_ref[sl, :].astype(F32) * cb_ref[sl, :].astype(F32)
            return c

        lax.fori_loop(0, T // R, fill, 0)
        w0, w1, w2 = w_ref[0:1, :], w_ref[1:2, :], w_ref[2:3, :]

        def step(t, carry):
            a0, a1, a2 = carry
            zz = z_s[pl.ds(pl.multiple_of(t * R, R), R + 8), :]
            z0 = zz[8:]
            z1 = pltpu.roll(zz, 1, 0)[8:]
            z2 = pltpu.roll(zz, 2, 0)[8:]
            sl = pl.ds(pl.multiple_of(t * R, R), R)
            dyv = dy_ref[sl, :].astype(F32)
            dcb_ref[sl, :] = (dyv * (w2 * z0 + w1 * z1 + w0 * z2)).astype(BF)
            dd = d_s[pl.ds(pl.multiple_of(t * R, R), R + 8), :]
            d0 = dd[:R]
            d1 = pltpu.roll(dd, R + 7, 0)[:R]
            d2 = pltpu.roll(dd, R + 6, 0)[:R]
            dz = w2 * d0 + w1 * d1 + w0 * d2
            dcc_ref[sl, :] = (dz * cu_ref[sl, :].astype(F32)).astype(BF)
            dcu_ref[sl, :] = (dz * cc_ref[sl, :].astype(F32)).astype(BF)
            a0 = a0 + jnp.sum(d0 * z2, axis=0, keepdims=True)
            a1 = a1 + jnp.sum(d0 * z1, axis=0, keepdims=True)
            a2 = a2 + jnp.sum(d0 * z0, axis=0, keepdims=True)
            return a0, a1, a2

        zero = jnp.zeros((1, LANE), F32)
        a0, a1, a2 = lax.fori_loop(0, T // R, step, (zero, zero, zero))
        dw_ref[0:1, :] = a0
        dw_ref[1:2, :] = a1
        dw_ref[2:3, :] = a2

    def col(base):
        return pl.BlockSpec((T, LANE), lambda j: (0, _seg0(big) + base + j))

    out = pl.BlockSpec((T, LANE), lambda j: (0, j))
    w = pl.BlockSpec((3, LANE), lambda j: (0, j))
    return pl.pallas_call(
        body, name=name, grid=(BRANCH_W // LANE,),
        in_specs=[col(COL_CU), col(COL_CB), col(COL_CC), out, w],
        out_specs=[out, out, out, w],
        out_shape=[_sds((T, BRANCH_W), BF)] * 3 + [_sds((3, BRANCH_W), F32)],
        scratch_shapes=[pltpu.VMEM((T + 8, LANE), F32), pltpu.VMEM((T + 8, LANE), F32)],
        compiler_params=_cp((PAR,)),
    )(big, big, big, dy, cw)


def _ret_tables(T):
    L = min(RET_L, T)
    hh = jnp.arange(H_RET, dtype=F32)
    lg = jnp.log1p(-jnp.exp2(-5.0 - hh))
    n = jnp.arange(L, dtype=F32)
    a = jnp.exp(lg[:, None] * (n + 1.0))
    b = jnp.exp(lg[:, None] * (L - 1.0 - n))
    gl = jnp.exp(lg * L)
    ch = jnp.arange(L) // CHUNK
    m = jnp.exp(lg[:, None, None] * jnp.abs(n[:, None] - n[None, :])) * (ch[None, :] <= ch[:, None]).astype(F32)
    inv_freq = ROPE_BASE ** (-jnp.linspace(0.0, 1.0, DK_RET // 2, dtype=F32))
    ang = jnp.arange(T, dtype=F32)[:, None] * inv_freq[None, :]
    cos, sin = jnp.cos(ang), jnp.sin(ang)
    return dict(
        L=L, M=m,
        a=jnp.broadcast_to(a[:, :, None], (H_RET, L, DK_RET)),
        b=jnp.broadcast_to(b[:, :, None], (H_RET, L, DK_RET)),
        gl=jnp.broadcast_to(gl[:, None, None], (H_RET, 1, DK_RET)),
        cos=jnp.concatenate([cos, cos], axis=-1), sin=jnp.concatenate([-sin, sin], axis=-1))


def _rot(x, cs, sn):
    return x * cs + pltpu.roll(x, DK_RET // 2, 1) * sn


def _unrot(dy, cs, sn):
    return dy * cs + pltpu.roll(dy * sn, DK_RET // 2, 1)


def _ret_fwd(big, tb, name, comm=None):
    T = big.shape[0]
    L = tb["L"]
    nsc = T // L
    scale = DK_RET ** -0.5

    def body(q_ref, k_ref, v_ref, g_ref, cos_ref, sin_ref, m_ref, a_ref, b_ref, gl_ref,
             y_ref, o_ref, st_ref, s_s):
        @pl.when(pl.program_id(1) == 0)
        def _():
            s_s[...] = jnp.zeros_like(s_s)

        cs, sn = cos_ref[...], sin_ref[...]
        qt = _rot(q_ref[...].astype(F32), cs, sn) * scale
        kt = _rot(k_ref[...].astype(F32), cs, sn)
        qb, kb, vb = qt.astype(BF), kt.astype(BF), v_ref[...]
        s_prev = s_s[...]
        st_ref[...] = s_prev
        p = lax.dot_general(qb, kb, NT_DIMS, preferred_element_type=F32) * m_ref[...]
        o = (jnp.dot(p.astype(BF), vb, preferred_element_type=F32)
             + jnp.dot((qt * a_ref[...]).astype(BF), s_prev.astype(BF), preferred_element_type=F32))
        s_s[...] = s_prev * gl_ref[...] + lax.dot_general((kt * b_ref[...]).astype(BF), vb, TN_DIMS,
                                                         preferred_element_type=F32)
        o_ref[...] = o
        gv = g_ref[...].astype(F32)
        y_ref[...] = (gv * _sigmoid(gv) * o * _rms_r(o)).astype(BF)

    def col(base):
        return pl.BlockSpec((L, LANE), lambda h, i: (i, _seg0(big) + base + h))

    tab = pl.BlockSpec((L, DK_RET), lambda h, i: (i, 0))
    per_head = pl.BlockSpec((None, L, DK_RET), lambda h, i: (h, 0, 0))
    out = pl.BlockSpec((L, LANE), lambda h, i: (i, h))
    return _call(
        body, name=name, grid=(H_RET, nsc), comm=comm,
        args=(big, big, big, big, tb["cos"], tb["sin"], tb["M"], tb["a"], tb["b"], tb["gl"]),
        in_specs=[col(COL_RQ), col(COL_RK), col(COL_RV), col(COL_RG), tab, tab,
                  pl.BlockSpec((None, L, L), lambda h, i: (h, 0, 0)), per_head, per_head,
                  pl.BlockSpec((None, 1, DK_RET), lambda h, i: (h, 0, 0))],
        out_specs=[out, out, pl.BlockSpec((None, None, DK_RET, DK_RET), lambda h, i: (i, h, 0, 0))],
        out_shape=[_sds((T, BRANCH_W), BF), _sds((T, BRANCH_W), F32), _sds((nsc, H_RET, DK_RET, DK_RET), F32)],
        scratch_shapes=[pltpu.VMEM((DK_RET, DK_RET), F32)],
        sem=(ARB, ARB))


def _ret_bwd(big, o, st, dy, tb, name):
    T = big.shape[0]
    L = tb["L"]
    nsc = T // L
    scale = DK_RET ** -0.5

    def body(q_ref, k_ref, v_ref, g_ref, cos_ref, sin_ref, m_ref, a_ref, b_ref, gl_ref, o_ref, st_ref, dy_ref,
             dq_ref, dk_ref, dv_ref, dg_ref, ds_s):
        @pl.when(pl.program_id(1) == 0)
        def _():
            ds_s[...] = jnp.zeros_like(ds_s)

        cs, sn = cos_ref[...], sin_ref[...]
        mm, av, bv = m_ref[...], a_ref[...], b_ref[...]
        qt = _rot(q_ref[...].astype(F32), cs, sn) * scale
        kt = _rot(k_ref[...].astype(F32), cs, sn)
        qb, kb, vb = qt.astype(BF), kt.astype(BF), v_ref[...]
        pb = (lax.dot_general(qb, kb, NT_DIMS, preferred_element_type=F32) * mm).astype(BF)
        ov = o_ref[...]
        r = _rms_r(ov)
        oh = ov * r
        gv = g_ref[...].astype(F32)
        sg = _sigmoid(gv)
        dyv = dy_ref[...].astype(F32)
        dg_ref[...] = (dyv * oh * (sg * (1.0 + gv * (1.0 - sg)))).astype(BF)
        doh = dyv * gv * sg
        dob = (r * (doh - oh * jnp.mean(doh * oh, axis=-1, keepdims=True))).astype(BF)
        dsb = ds_s[...].astype(BF)
        spb = st_ref[...].astype(BF)
        dpb = (lax.dot_general(dob, vb, NT_DIMS, preferred_element_type=F32) * mm).astype(BF)
        dqt = (jnp.dot(dpb, kb, preferred_element_type=F32)
               + lax.dot_general(dob, spb, NT_DIMS, preferred_element_type=F32) * av)
        dkt = (lax.dot_general(dpb, qb, TN_DIMS, preferred_element_type=F32)
               + lax.dot_general(vb, dsb, NT_DIMS, preferred_element_type=F32) * bv)
        dv = (lax.dot_general(pb, dob, TN_DIMS, preferred_element_type=F32)
              + jnp.dot((kt * bv).astype(BF), dsb, preferred_element_type=F32))
        ds_s[...] = ds_s[...] * gl_ref[...] + lax.dot_general((qt * av).astype(BF), dob, TN_DIMS,
                                                              preferred_element_type=F32)
        dq_ref[...] = (_unrot(dqt, cs, sn) * scale).astype(BF)
        dk_ref[...] = _unrot(dkt, cs, sn).astype(BF)
        dv_ref[...] = dv.astype(BF)

    def rev(i):
        return nsc - 1 - i

    def col(base):
        return pl.BlockSpec((L, LANE), lambda h, i: (rev(i), _seg0(big) + base + h))

    tab = pl.BlockSpec((L, DK_RET), lambda h, i: (rev(i), 0))
    per_head = pl.BlockSpec((None, L, DK_RET), lambda h, i: (h, 0, 0))
    out = pl.BlockSpec((L, LANE), lambda h, i: (rev(i), h))
    return pl.pallas_call(
        body, name=name, grid=(H_RET, nsc),
        in_specs=[col(COL_RQ), col(COL_RK), col(COL_RV), col(COL_RG), tab, tab,
                  pl.BlockSpec((None, L, L), lambda h, i: (h, 0, 0)), per_head, per_head,
                  pl.BlockSpec((None, 1, DK_RET), lambda h, i: (h, 0, 0)),
                  out, pl.BlockSpec((None, None, DK_RET, DK_RET), lambda h, i: (rev(i), h, 0, 0)), out],
        out_specs=[out, out, out, out],
        out_shape=[_sds((T, BRANCH_W), BF)] * 4,
        scratch_shapes=[pltpu.VMEM((DK_RET, DK_RET), F32)],
        compiler_params=_cp((PAR, ARB)),
    )(big, big, big, big, tb["cos"], tb["sin"], tb["M"], tb["a"], tb["b"], tb["gl"], o, st, dy)


def _relbias_onehot(n):
    mm = lax.broadcasted_iota(jnp.int32, (RB_PAD, ATT_TOEP), 1)
    rr = lax.broadcasted_iota(jnp.int32, (RB_PAD, ATT_TOEP), 0)
    idx = jnp.clip(n + ATT_TOEP - mm, 0, 2 * REL_CLIP)
    return (rr == idx).astype(F32)


def _relbias_expand(rbp, name):
    far = ATT_SPAN - ATT_TOEP

    def body(rb_ref, o_ref):
        rb = rb_ref[...]
        const = jnp.broadcast_to(rb[:, 2 * REL_CLIP:2 * REL_CLIP + 1], (H_ATT, far))

        def row(n, c):
            toep = jnp.dot(rb, _relbias_onehot(n), preferred_element_type=F32, precision=lax.Precision.HIGHEST)
            m = lax.broadcasted_iota(jnp.int32, (1, ATT_SPAN), 1)
            d = n // CHUNK + N_PREV - m // CHUNK
            neg = jnp.where((d >= 0) & (d <= N_PREV), 0.0, NEG_INF).astype(F32)
            o_ref[n] = jnp.concatenate([const, toep], axis=1) + neg
            return c

        lax.fori_loop(0, ATT_TQ, row, 0)

    return pl.pallas_call(
        body, name=name,
        in_specs=[pl.BlockSpec(memory_space=pltpu.VMEM)],
        out_specs=pl.BlockSpec(memory_space=pltpu.VMEM),
        out_shape=_sds((ATT_TQ, H_ATT, ATT_SPAN), F32),
    )(rbp)


def _relbias_grad(dbt, name):
    far = ATT_SPAN - ATT_TOEP

    def body(d_ref, o_ref):
        def row(n, carry):
            acc, cs = carry
            dn = d_ref[n]
            acc = acc + lax.dot_general(dn[:, far:], _relbias_onehot(n), NT_DIMS, preferred_element_type=F32,
                                        precision=lax.Precision.HIGHEST)
            cs = cs + jnp.sum(dn[:, :far], axis=1, keepdims=True)
            return acc, cs

        acc, cs = lax.fori_loop(0, ATT_TQ, row, (jnp.zeros((H_ATT, RB_PAD), F32), jnp.zeros((H_ATT, 1), F32)))
        rr = lax.broadcasted_iota(jnp.int32, (H_ATT, RB_PAD), 1)
        o_ref[...] = acc + jnp.where(rr == 2 * REL_CLIP, cs, 0.0)

    return pl.pallas_call(
        body, name=name,
        in_specs=[pl.BlockSpec(memory_space=pltpu.VMEM)],
        out_specs=pl.BlockSpec(memory_space=pltpu.VMEM),
        out_shape=_sds((H_ATT, RB_PAD), F32),
    )(dbt)


def _att_pad_fill(dst_s, src_ref, T):
    dst_s[pl.ds(0, ATT_PAD), :] = jnp.zeros((ATT_PAD, LANE), dst_s.dtype)
    R = min(512, T)

    def cp(t, c):
        dst_s[pl.ds(pl.multiple_of(ATT_PAD + t * R, LANE), R), :] = src_ref[pl.ds(pl.multiple_of(t * R, R), R), :]
        return c

    lax.fori_loop(0, T // R, cp, 0)


def _att_probs(qh, kh, bias, valid):
    s = lax.dot_general(qh, kh, NT_DIMS, preferred_element_type=F32) * (DH_ATT ** -0.5) + bias
    s = jnp.where(valid, s, NEG_INF)
    p = jnp.exp(s - jnp.max(s, axis=-1, keepdims=True))
    return p / jnp.sum(p, axis=-1, keepdims=True)


def _att_fwd(big, bias, name, comm=None):
    T = big.shape[0]
    nt = T // ATT_TQ

    def body(q_ref, k_ref, v_ref, b_ref, y_ref, kp_s, vp_s):
        i = pl.program_id(1)

        @pl.when(i == 0)
        def _():
            _att_pad_fill(kp_s, k_ref, T)
            _att_pad_fill(vp_s, v_ref, T)

        t0 = pl.multiple_of(i * ATT_TQ, ATT_TQ)
        kw = kp_s[pl.ds(t0, ATT_SPAN), :]
        vw = vp_s[pl.ds(t0, ATT_SPAN), :]
        valid = (t0 - ATT_PAD + lax.broadcasted_iota(jnp.int32, (1, ATT_SPAN), 1)) >= 0
        outs = []
        for hh in range(2):
            sl = slice(hh * DH_ATT, (hh + 1) * DH_ATT)
            pn = _att_probs(q_ref[:, sl], kw[:, sl], b_ref[hh], valid)
            outs.append(jnp.dot(pn.astype(BF), vw[:, sl], preferred_element_type=F32))
        y_ref[...] = jnp.concatenate(outs, axis=1).astype(BF)

    def whole(base):
        return pl.BlockSpec((T, LANE), lambda p, i: (0, _seg0(big) + base + p))

    return _call(
        body, name=name, grid=(H_ATT // 2, nt), args=(big, big, big, bias), comm=comm,
        in_specs=[pl.BlockSpec((ATT_TQ, LANE), lambda p, i: (i, _seg0(big) + COL_AQ + p)), whole(COL_AK), whole(COL_AV),
                  pl.BlockSpec((2, ATT_TQ, ATT_SPAN), lambda p, i: (p, 0, 0))],
        out_specs=[pl.BlockSpec((ATT_TQ, LANE), lambda p, i: (i, p))],
        out_shape=[_sds((T, BRANCH_W), BF)],
        scratch_shapes=[pltpu.VMEM((T + ATT_PAD, LANE), BF), pltpu.VMEM((T + ATT_PAD, LANE), BF)],
        sem=(ARB, ARB))


def _att_bwd(big, bias, dy, name, comm=None):
    T = big.shape[0]
    nt = T // ATT_TQ
    scale = DH_ATT ** -0.5

    def body(q_ref, k_ref, v_ref, b_ref, dy_ref, dq_ref, dk_ref, dv_ref, db_ref, kp_s, vp_s, dk_s, dv_s):
        i = pl.program_id(1)

        @pl.when(i == 0)
        def _():
            _att_pad_fill(kp_s, k_ref, T)
            _att_pad_fill(vp_s, v_ref, T)
            dk_s[...] = jnp.zeros_like(dk_s)
            dv_s[...] = jnp.zeros_like(dv_s)
            db_ref[...] = jnp.zeros_like(db_ref)

        t0 = pl.multiple_of(i * ATT_TQ, ATT_TQ)
        win = pl.ds(t0, ATT_SPAN)
        kw = kp_s[win, :]
        vw = vp_s[win, :]
        valid = (t0 - ATT_PAD + lax.broadcasted_iota(jnp.int32, (1, ATT_SPAN), 1)) >= 0
        dqs, dks, dvs = [], [], []
        for hh in range(2):
            sl = slice(hh * DH_ATT, (hh + 1) * DH_ATT)
            qh, kh, vh = q_ref[:, sl], kw[:, sl], vw[:, sl]
            pn = _att_probs(qh, kh, b_ref[hh], valid)
            doh = dy_ref[:, sl]
            dp = lax.dot_general(doh, vh, NT_DIMS, preferred_element_type=F32)
            ds = pn * (dp - jnp.sum(dp * pn, axis=-1, keepdims=True))
            db_ref[hh] += ds
            dsb = ds.astype(BF)
            dqs.append(jnp.dot(dsb, kh, preferred_element_type=F32) * scale)
            dks.append(lax.dot_general(dsb, qh, TN_DIMS, preferred_element_type=F32) * scale)
            dvs.append(lax.dot_general(pn.astype(BF), doh, TN_DIMS, preferred_element_type=F32))
        dq_ref[...] = jnp.concatenate(dqs, axis=1).astype(BF)
        dk_s[win, :] += jnp.concatenate(dks, axis=1)
        dv_s[win, :] += jnp.concatenate(dvs, axis=1)

        @pl.when(i == nt - 1)
        def _():
            R = min(512, T)

            def cp(t, c):
                src = pl.ds(pl.multiple_of(ATT_PAD + t * R, LANE), R)
                dst = pl.ds(pl.multiple_of(t * R, R), R)
                dk_ref[dst, :] = dk_s[src, :].astype(BF)
                dv_ref[dst, :] = dv_s[src, :].astype(BF)
                return c

            lax.fori_loop(0, T // R, cp, 0)

    def whole(base):
        return pl.BlockSpec((T, LANE), lambda p, i: (0, _seg0(big) + base + p))

    tile = pl.BlockSpec((ATT_TQ, LANE), lambda p, i: (i, p))
    bspec = pl.BlockSpec((2, ATT_TQ, ATT_SPAN), lambda p, i: (p, 0, 0))
    return _call(
        body, name=name, grid=(H_ATT // 2, nt), args=(big, big, big, bias, dy), comm=comm,
        in_specs=[pl.BlockSpec((ATT_TQ, LANE), lambda p, i: (i, _seg0(big) + COL_AQ + p)), whole(COL_AK), whole(COL_AV), bspec, tile],
        out_specs=[tile, pl.BlockSpec((T, LANE), lambda p, i: (0, p)), pl.BlockSpec((T, LANE), lambda p, i: (0, p)), bspec],
        out_shape=[_sds((T, BRANCH_W), BF)] * 3 + [_sds((H_ATT, ATT_TQ, ATT_SPAN), F32)],
        scratch_shapes=[pltpu.VMEM((T + ATT_PAD, LANE), BF), pltpu.VMEM((T + ATT_PAD, LANE), BF),
                        pltpu.VMEM((T + ATT_PAD, LANE), F32), pltpu.VMEM((T + ATT_PAD, LANE), F32)],
        sem=(ARB, ARB))


def _merge_fwd(x1, big, ys, wb, wo, name):
    T, D = x1.shape
    tm = min(TM, T)

    def body(x_ref, gp_ref, yc_ref, yr_ref, ya_ref, wb_ref, wo_ref, x2_ref, p_ref, mg_ref):
        merged = jnp.zeros((tm, D), F32)
        for i, y_ref in enumerate((yc_ref, yr_ref, ya_ref)):
            cols = slice(i * D, (i + 1) * D)
            pb = jnp.dot(y_ref[...], wb_ref[i], preferred_element_type=F32).astype(BF)
            p_ref[:, cols] = pb
            merged = merged + _sigmoid(gp_ref[:, cols].astype(F32)) * pb.astype(F32)
        mb = merged.astype(BF)
        mg_ref[...] = mb
        x2_ref[...] = x_ref[...] + jnp.dot(mb, wo_ref[...], preferred_element_type=F32)

    tok = pl.BlockSpec((tm, D), lambda i: (i, 0))
    wide = pl.BlockSpec((tm, 3 * D), lambda i: (i, 0))
    yspec = pl.BlockSpec((tm, BRANCH_W), lambda i: (i, 0))
    return pl.pallas_call(
        body, name=name, grid=(T // tm,),
        in_specs=[tok, wide, yspec, yspec, yspec,
                  pl.BlockSpec((3, BRANCH_W, D), lambda i: (0, 0, 0)),
                  pl.BlockSpec((D, D), lambda i: (0, 0))],
        out_specs=[tok, wide, tok],
        out_shape=[_sds((T, D), F32), _sds((T, 3 * D), BF), _sds((T, D), BF)],
        compiler_params=_cp((PAR,)),
    )(x1, big, *ys, wb, wo)


def _merge_bwd(dx2, big, p, wb, wo, name):
    T, D = dx2.shape
    tm = min(TM, T)

    def body(dx_ref, gp_ref, p_ref, wb_ref, wo_ref, dp_ref, dgp_ref, dyc_ref, dyr_ref, dya_ref, dxb_ref):
        dxb = dx_ref[...].astype(BF)
        dxb_ref[...] = dxb
        dm = lax.dot_general(dxb, wo_ref[...], NT_DIMS, preferred_element_type=F32)
        for i, dy_ref in enumerate((dyc_ref, dyr_ref, dya_ref)):
            cols = slice(i * D, (i + 1) * D)
            gt = _sigmoid(gp_ref[:, cols].astype(F32))
            dpb = (dm * gt).astype(BF)
            dp_ref[:, cols] = dpb
            dgp_ref[:, cols] = (dm * p_ref[:, cols].astype(F32) * gt * (1.0 - gt)).astype(BF)
            dy_ref[...] = lax.dot_general(dpb, wb_ref[i], NT_DIMS, preferred_element_type=F32).astype(BF)

    tok = pl.BlockSpec((tm, D), lambda i: (i, 0))
    wide = pl.BlockSpec((tm, 3 * D), lambda i: (i, 0))
    yspec = pl.BlockSpec((tm, BRANCH_W), lambda i: (i, 0))
    return pl.pallas_call(
        body, name=name, grid=(T // tm,),
        in_specs=[tok, wide, wide,
                  pl.BlockSpec((3, BRANCH_W, D), lambda i: (0, 0, 0)),
                  pl.BlockSpec((D, D), lambda i: (0, 0))],
        out_specs=[wide, wide, yspec, yspec, yspec, tok],
        out_shape=[_sds((T, 3 * D), BF), _sds((T, 3 * D), BF)] + [_sds((T, BRANCH_W), BF)] * 3 + [_sds((T, D), BF)],
        compiler_params=_cp((PAR,)),
    )(dx2, big, p, wb, wo)


def _loss_head(x, tgt, fw, name):
    T, D = x.shape
    tm = min(TM, T)

    def body(x_ref, t_ref, w_ref, loss_ref, dx_ref, dw_ref):
        @pl.when(pl.program_id(0) == 0)
        def _():
            loss_ref[...] = jnp.zeros_like(loss_ref)
            dw_ref[...] = jnp.zeros_like(dw_ref)

        xv = x_ref[...]
        wv = w_ref[...]
        e = xv * _rms_r(xv) * wv - t_ref[...]
        loss_ref[...] += 0.5 * jnp.sum(jnp.mean(e * e, axis=-1, keepdims=True))
        dx, dn = _rms_bwd(e * (1.0 / D), xv, wv)
        dx_ref[...] = dx
        dw_ref[...] += dn

    tok = pl.BlockSpec((tm, D), lambda i: (i, 0))
    return pl.pallas_call(
        body, name=name, grid=(T // tm,),
        in_specs=[tok, tok, pl.BlockSpec((1, D), lambda i: (0, 0))],
        out_specs=[pl.BlockSpec((8, LANE), lambda i: (0, 0)), tok, pl.BlockSpec((1, D), lambda i: (0, 0))],
        out_shape=[_sds((8, LANE), F32), _sds((T, D), F32), _sds((1, D), F32)],
        compiler_params=_cp((ARB,)),
    )(x, tgt, fw)


def _block_rows(rows, cols):
    cap = max(8, (1 << 18) // cols)
    best = None
    for r in range(8, rows + 1, 8):
        if rows % r == 0 and r <= cap:
            best = r
    return best if best is not None else rows


def _sum4(land, l, n_layers, name, prev=None):
    _, rows, cols = land.shape
    br = _block_rows(rows, cols)

    def body(*refs):
        l_ref, o_ref = refs[0], refs[-1]
        o_ref[...] = ((l_ref[3].astype(F32) + l_ref[0].astype(F32)) + l_ref[1].astype(F32)) + l_ref[2].astype(F32)

    in_specs = [pl.BlockSpec((4, br, cols), lambda i: (0, i, 0))]
    args = [land]
    aliases = {}
    if prev is not None:
        in_specs.append(pl.BlockSpec(memory_space=pl.ANY))
        args.append(prev)
        aliases = {1: 0}
    return pl.pallas_call(
        body, name=name, grid=(rows // br,), in_specs=in_specs,
        out_specs=pl.BlockSpec((None, br, cols), lambda i: (l, i, 0)),
        out_shape=_sds((n_layers, rows, cols), F32),
        input_output_aliases=aliases, compiler_params=_cp((PAR,)),
    )(*args)


def _adamw_math(w, g, m, v):
    m = ADAM_B1 * m + (1.0 - ADAM_B1) * g
    v = ADAM_B2 * v + (1.0 - ADAM_B2) * (g * g)
    m_hat = m / (1.0 - ADAM_B1 ** ADAM_STEP)
    v_hat = v / (1.0 - ADAM_B2 ** ADAM_STEP)
    delta = -ADAM_LR * (m_hat / (jnp.sqrt(v_hat) + ADAM_EPS) + ADAM_WD * w)
    return delta, m, v


def _adamw(w, ga, gb, m, v, name):
    rows, cols = w.shape
    br = _block_rows(rows, cols)
    two = gb is not None

    def body(*refs):
        if two:
            w_ref, ga_ref, gb_ref, m_ref, v_ref, g_ref, d_ref, nm_ref, nv_ref = refs
            g = ga_ref[...] + gb_ref[...]
        else:
            w_ref, ga_ref, m_ref, v_ref, g_ref, d_ref, nm_ref, nv_ref = refs
            g = ga_ref[...]
        d, nm, nv = _adamw_math(w_ref[...], g, m_ref[...], v_ref[...])
        g_ref[...] = g
        d_ref[...] = d
        nm_ref[...] = nm
        nv_ref[...] = nv

    blk = pl.BlockSpec((br, cols), lambda i: (i, 0))
    args = [w, ga] + ([gb] if two else []) + [m, v]
    return pl.pallas_call(
        body, name=name, grid=(rows // br,),
        in_specs=[blk] * len(args), out_specs=[blk] * 4,
        out_shape=[_sds((rows, cols), F32)] * 4,
        compiler_params=_cp((PAR,)),
    )(*args)


def _swap_cores(vs, name):
    n = len(vs)

    def body(*refs):
        v_refs, o_refs = refs[:n], refs[n:2 * n]
        ssem, rsem = refs[2 * n:]
        x, y, c = _place()
        copies = [pltpu.make_async_remote_copy(
            src_ref=v_refs[k], dst_ref=o_refs[k], send_sem=ssem.at[k], recv_sem=rsem.at[k],
            device_id=(x, y, 1 - c), device_id_type=MESH) for k in range(n)]
        for cp in copies:
            cp.start()
        for cp in copies:
            cp.wait()

    hbm = pl.BlockSpec(memory_space=pl.ANY)
    return pl.pallas_call(
        body, name=name,
        in_specs=[hbm] * n, out_specs=[hbm] * n,
        out_shape=[_sds(v.shape, v.dtype) for v in vs],
        scratch_shapes=[pltpu.SemaphoreType.DMA((n,)), pltpu.SemaphoreType.DMA((n,))],
    )(*vs)


def _allreduce_small(v, name):
    rows = v.shape[0]
    flips = [(fx, fy, fc) for fx in (0, 1) for fy in (0, 1) for fc in (0, 1) if fx or fy or fc]

    def body(v_ref, o_ref, all_s, ssem, rsem):
        x, y, c = _place()

        def peer(f):
            return (x + f[0] - 2 * x * f[0], y + f[1] - 2 * y * f[1], c + f[2] - 2 * c * f[2])

        def slot(p):
            return all_s.at[4 * p[0] + 2 * p[1] + p[2]]

        def copy(k, f, owner):
            return pltpu.make_async_remote_copy(
                src_ref=v_ref, dst_ref=slot(owner), send_sem=ssem.at[k], recv_sem=rsem.at[k],
                device_id=peer(f), device_id_type=MESH)

        sends = [copy(k, f, (x, y, c)) for k, f in enumerate(flips)]
        for cp in sends:
            cp.start()
        all_s[4 * x + 2 * y + c] = v_ref[...]
        for k, f in enumerate(flips):
            copy(k, f, peer(f)).wait_recv()
        for cp in sends:
            cp.wait_send()
        acc = all_s[0]
        for d in range(1, 8):
            acc = acc + all_s[d]
        o_ref[...] = acc

    return pl.pallas_call(
        body, name=name,
        in_specs=[pl.BlockSpec(memory_space=pltpu.VMEM)],
        out_specs=pl.BlockSpec(memory_space=pltpu.VMEM),
        out_shape=_sds((rows, LANE), F32),
        scratch_shapes=[pltpu.VMEM((8, rows, LANE), F32), pltpu.SemaphoreType.DMA((7,)), pltpu.SemaphoreType.DMA((7,))],
    )(v)


BIG_NAMES = ("ffn1_w_gate", "ffn1_w_up", "ffn1_w_down", "w_in", "w_branch", "w_merge_gate", "w_out",
             "ffn2_w_gate", "ffn2_w_up", "ffn2_w_down")


FFN1 = ("ffn1_w_gate", "ffn1_w_up", "ffn1_w_down")
FFN2 = ("ffn2_w_gate", "ffn2_w_up", "ffn2_w_down")
MIX_IN = ("w_in", "w_merge_gate")
MIX_OUT = ("w_branch", "w_out")


def _keys(names, l):
    return [(n, l) for n in names]


def _local_step(x, tgt, small, convw_full, wx, n_layers):
    T, D = x.shape
    L = n_layers
    ns = N_SHARD
    dq = D // ns
    W = wx.w

    def hosted(call, keys, scatter=False):
        comm = wx.pieces(keys, scatter)
        main, extra = call(comm)
        if comm is not None:
            wx.arrived(keys, extra, scatter)
        return main

    def mixer_views(l):
        g4 = W[("w_merge_gate", l)]
        gates = jnp.transpose(g4, (0, 2, 1, 3)).reshape(D, 3 * D)
        win = jnp.transpose(W[("w_in", l)], (1, 0, 2)).reshape(D, -1)
        return jnp.concatenate([gates, win], axis=-1)

    def out_views(l):
        wb = jnp.transpose(W[("w_branch", l)], (1, 2, 0, 3)).reshape(3, BRANCH_W, D)
        wo = W[("w_out", l)].reshape(D, D)
        return wb, wo

    tb = _ret_tables(T)
    rb_pad = jnp.pad(small["rel_bias"], ((0, 0), (0, 0), (0, RB_PAD - N_REL)))

    saved = []
    h = x
    for l in range(L):
        s = {"x0": h}
        nxt = l + 1
        x1, s["g1"], s["u1"] = hosted(
            lambda c: _ffn_fwd(h, small["ffn1_norm"][l][None], W[("ffn1_w_gate", l)], W[("ffn1_w_up", l)],
                               W[("ffn1_w_down", l)], f"ffn1_fwd_{l}", comm=c), _keys(MIX_IN, l))
        s["x1"] = x1
        s["wbig"] = mixer_views(l)
        big, s["h"] = _inproj_fwd(x1, small["mix_norm"][l][None], s["wbig"], f"inproj_fwd_{l}")
        s["big"] = big
        s["bias"] = jnp.transpose(_relbias_expand(rb_pad[l], f"relbias_expand_{l}"), (1, 0, 2))
        s["yc"] = _conv_fwd(big, convw_full[l], f"conv_fwd_{l}")
        s["yr"], s["o"], s["st"] = hosted(lambda c: _ret_fwd(big, tb, f"ret_fwd_{l}", comm=c), _keys(MIX_OUT, l))
        (s["ya"],) = hosted(lambda c: _att_fwd(big, s["bias"], f"att_fwd_{l}", comm=c), _keys(FFN2, l))
        s["wb"], s["wo"] = out_views(l)
        x2, s["p"], s["mg"] = _merge_fwd(x1, big, (s["yc"], s["yr"], s["ya"]), s["wb"], s["wo"], f"merge_fwd_{l}")
        s["x2"] = x2
        h, s["g2"], s["u2"] = hosted(
            lambda c: _ffn_fwd(x2, small["ffn2_norm"][l][None], W[("ffn2_w_gate", l)], W[("ffn2_w_up", l)],
                               W[("ffn2_w_down", l)], f"ffn2_fwd_{l}", comm=c), _keys(FFN1, nxt) if nxt < L else [])
        saved.append(s)

    loss_p, dx, d_final = _loss_head(h, tgt, small["final_norm"][None], "loss_head")

    gs = {"final_norm": d_final[0]}
    for k in ("ffn1_norm", "mix_norm", "ffn2_norm", "rel_bias", "conv_w"):
        gs[k] = [None] * L
    tk = min(2048, T)
    nk = T // tk

    def ffn_grads(pre, l, hb, dgv, duv, av, dacc):
        fs = dgv.shape[-1]
        hspec = pl.BlockSpec((tk, D), lambda p, q, k: (k, 0))
        sspec = pl.BlockSpec((None, tk, fs), lambda p, q, k: (p, k, 0))
        for nm, bv in ((pre + "_w_gate", dgv), (pre + "_w_up", duv)):
            wx.g[(nm, l)] = _tn(hb, bv, hspec, sspec, _sds((ns, D, fs), BF),
                                pl.BlockSpec((None, D, fs), lambda p, q, k: (p, 0, 0)), (ns, 1, nk), f"d{nm}_{l}")
        nm = pre + "_w_down"
        wx.g[(nm, l)] = _tn(av, dacc, sspec, hspec, _sds((ns, fs, D), BF),
                            pl.BlockSpec((None, fs, D), lambda p, q, k: (p, 0, 0)), (ns, 1, nk), f"d{nm}_{l}")

    for l in reversed(range(L)):
        s = saved[l]
        dx, dgv, duv, av, hb, dacc, dn = hosted(
            lambda c: _ffn_bwd(dx, s["x2"], small["ffn2_norm"][l][None], s["g2"], s["u2"], W[("ffn2_w_gate", l)],
                               W[("ffn2_w_up", l)], W[("ffn2_w_down", l)], f"ffn2_bwd_{l}", comm=c),
            _keys(FFN1, l + 1) if l + 1 < L else [], scatter=True)
        gs["ffn2_norm"][l] = dn[0]
        ffn_grads("ffn2", l, hb, dgv, duv, av, dacc)
        dp, dgp, dyc, dyr, dya, dxb = _merge_bwd(dx, s["big"], s["p"], s["wb"], s["wo"], f"merge_bwd_{l}")
        wx.g[("w_out", l)] = _tn(
            s["mg"], dxb, pl.BlockSpec((tk, dq), lambda p, q, k: (k, p)), pl.BlockSpec((tk, D), lambda p, q, k: (k, 0)),
            _sds((ns, dq, D), BF), pl.BlockSpec((None, dq, D), lambda p, q, k: (p, 0, 0)), (ns, 1, nk), f"dw_out_{l}")
        gb = None
        for i, yv in enumerate((s["yc"], s["yr"], s["ya"])):
            gb = _tn(yv, dp,
                     pl.BlockSpec((tk, BRANCH_W), lambda p, q, k: (k, 0)),
                     pl.BlockSpec((tk, dq), lambda p, q, k, i=i: (k, i * ns + p)),
                     _sds((ns, 3, BRANCH_W, dq), BF),
                     pl.BlockSpec((None, None, BRANCH_W, dq), lambda p, q, k, i=i: (p, i, 0, 0)),
                     (ns, 1, nk), f"dw_branch{i}_{l}", prev=gb)
        wx.g[("w_branch", l)] = gb
        dcu, dcb, dcc, dcw = _conv_bwd(s["big"], dyc, convw_full[l], f"conv_bwd_{l}")
        gs["conv_w"][l] = dcw
        drq, drk, drv, drg = _ret_bwd(s["big"], s["o"], s["st"], dyr, tb, f"ret_bwd_{l}")
        daq, dak, dav, dbias = hosted(lambda c: _att_bwd(s["big"], s["bias"], dya, f"att_bwd_{l}", comm=c),
                                      _keys(FFN2, l), scatter=True)
        gs["rel_bias"][l] = _relbias_grad(jnp.transpose(dbias, (1, 0, 2)), f"relbias_grad_{l}")[:, :N_REL]
        dbig = jnp.concatenate([dgp, dcu, dcb, dcc, drq, drk, drv, drg, daq, dak, dav], axis=1)
        ics = W[("w_in", l)].shape[-1]
        nq = ics // 256
        wx.g[("w_in", l)] = _tn(
            s["h"], dbig, pl.BlockSpec((tk, D), lambda p, q, k: (k, 0)),
            pl.BlockSpec((tk, 256), lambda p, q, k: (k, 3 * D // 256 + p * nq + q)),
            _sds((ns, D, ics), BF), pl.BlockSpec((None, D, 256), lambda p, q, k: (p, 0, q)),
            (ns, nq, nk), f"dw_in_{l}")
        wx.g[("w_merge_gate", l)] = _tn(
            s["h"], dbig, pl.BlockSpec((tk, dq), lambda p, q, k: (k, p)), pl.BlockSpec((tk, D), lambda p, q, k: (k, q)),
            _sds((ns, 3, dq, D), BF), pl.BlockSpec((None, None, dq, D), lambda p, q, k: (p, q, 0, 0)),
            (ns, 3, nk), f"dw_merge_gate_{l}")
        dx, dn = _inproj_bwd(dbig, s["wbig"], s["x1"], small["mix_norm"][l][None], dx, f"inproj_bwd_{l}")
        gs["mix_norm"][l] = dn[0]
        dx, dgv, duv, av, hb, dacc, dn = hosted(
            lambda c: _ffn_bwd(dx, s["x0"], small["ffn1_norm"][l][None], s["g1"], s["u1"], W[("ffn1_w_gate", l)],
                               W[("ffn1_w_up", l)], W[("ffn1_w_down", l)], f"ffn1_bwd_{l}", comm=c),
            _keys(MIX_IN + MIX_OUT, l), scatter=True)
        gs["ffn1_norm"][l] = dn[0]
        ffn_grads("ffn1", l, hb, dgv, duv, av, dacc)

    for k in ("ffn1_norm", "mix_norm", "ffn2_norm", "rel_bias", "conv_w"):
        gs[k] = jnp.stack(gs[k])
    return loss_p, dx, gs


class _Exchange:
    def __init__(self, shards):
        self.shards = shards
        self.w = {}
        self.g = {}
        self.landed = {}

    def own(self, key):
        return self.shards[key[0]][key[1]].astype(BF)

    def pieces(self, keys, scatter):
        if not keys:
            return None
        return _Pieces([self.g[k] for k in keys] if scatter else [self.own(k) for k in keys], scatter)

    def arrived(self, keys, outs, scatter):
        for k, o in zip(keys, outs):
            (self.landed if scatter else self.w)[k] = o


W_NAMES = ("ffn1_norm", "ffn1_w_gate", "ffn1_w_up", "ffn1_w_down", "mix_norm", "w_in", "conv_w", "rel_bias", "w_branch",
           "w_merge_gate", "w_out", "ffn2_norm", "ffn2_w_gate", "ffn2_w_up", "ffn2_w_down", "final_norm")


def _as2d(a):
    return a.reshape(1, -1) if a.ndim == 1 else a.reshape(-1, a.shape[-1])


def kernel(x, ffn1_norm, ffn1_w_gate, ffn1_w_up, ffn1_w_down, mix_norm, w_in, conv_w, rel_bias, w_branch, w_merge_gate, w_out, ffn2_norm, ffn2_w_gate, ffn2_w_up, ffn2_w_down, final_norm, loss_target, m_ffn1_norm, m_ffn1_w_gate, m_ffn1_w_up, m_ffn1_w_down, m_mix_norm, m_w_in, m_conv_w, m_rel_bias, m_w_branch, m_w_merge_gate, m_w_out, m_ffn2_norm, m_ffn2_w_gate, m_ffn2_w_up, m_ffn2_w_down, m_final_norm, v_ffn1_norm, v_ffn1_w_gate, v_ffn1_w_up, v_ffn1_w_down, v_mix_norm, v_w_in, v_conv_w, v_rel_bias, v_w_branch, v_w_merge_gate, v_w_out, v_ffn2_norm, v_ffn2_w_gate, v_ffn2_w_up, v_ffn2_w_down, v_final_norm):
    given = dict(locals())
    w = {n: given[n] for n in W_NAMES}
    m = {n: given["m_" + n] for n in W_NAMES}
    v = {n: given["v_" + n] for n in W_NAMES}
    my_chip = 2 * lax.axis_index("x") + lax.axis_index("y")
    L = w_in.shape[0]

    wx = _Exchange({n: w[n] for n in BIG_NAMES})
    first = _keys(FFN1, 0)
    comm = _Pieces([wx.own(k) for k in first] + [conv_w], scatter=False)
    got = _comm_alone(comm, "gather_first")
    wx.arrived(first, got[:-1], False)
    convw_full = jnp.transpose(got[-1], (1, 2, 0, 3)).reshape(conv_w.shape[0], conv_w.shape[1], -1)

    small = {n: w[n] for n in ("ffn1_norm", "mix_norm", "ffn2_norm", "final_norm", "rel_bias")}
    loss_p, grad_x, gs = _local_step(x[0], loss_target[0], small, convw_full, wx, L)
    last = _keys(FFN1, 0)
    wx.arrived(last, _comm_alone(wx.pieces(last, True), "scatter_last"), True)

    sums = []
    for n in BIG_NAMES:
        acc = None
        for l in range(L):
            a = wx.landed[(n, l)]
            acc = _sum4(a.reshape(4, -1, a.shape[-1]), l, L, f"sum4_{n}_{l}", prev=acc)
        sums.append(acc.reshape(-1, acc.shape[-1]))
    others = _swap_cores(sums, "swap_cores")

    parts = [gs["ffn1_norm"].reshape(-1), gs["mix_norm"].reshape(-1), gs["ffn2_norm"].reshape(-1),
             gs["final_norm"].reshape(-1), gs["rel_bias"].reshape(-1), gs["conv_w"].reshape(-1), loss_p[0]]
    sizes = [p.shape[0] for p in parts]
    flat = jnp.concatenate(parts)
    rows = -(-flat.shape[0] // (8 * LANE)) * 8
    flat = jnp.pad(flat, (0, rows * LANE - flat.shape[0])).reshape(rows, LANE)
    red = _allreduce_small(flat, "allreduce_small").reshape(-1)
    offs = [0]
    for sz in sizes:
        offs.append(offs[-1] + sz)
    sm = {}
    for i, n in enumerate(("ffn1_norm", "mix_norm", "ffn2_norm", "final_norm", "rel_bias", "conv_w")):
        sm[n] = red[offs[i]:offs[i + 1]]
    loss = red[offs[6]]
    sm["conv_w"] = lax.dynamic_slice_in_dim(sm["conv_w"].reshape(conv_w.shape[0], conv_w.shape[1], -1),
                                            my_chip * conv_w.shape[2], conv_w.shape[2], axis=2)

    grads, deltas, new_m, new_v = {}, {}, {}, {}
    big_sum = dict(zip(BIG_NAMES, zip(sums, others)))
    for n in W_NAMES:
        shape = w[n].shape
        if n in big_sum:
            ga, gb = big_sum[n]
        else:
            ga, gb = _as2d(sm[n].reshape(shape)), None
        out = _adamw(_as2d(w[n]), ga, gb, _as2d(m[n]), _as2d(v[n]), f"adamw_{n}")
        grads[n], deltas[n], new_m[n], new_v[n] = (o.reshape(shape) for o in out)

    return (loss, grad_x[None], *[grads[n] for n in W_NAMES], *[deltas[n] for n in W_NAMES],
            *[new_m[n] for n in W_NAMES], *[new_v[n] for n in W_NAMES])
```

```python
import functools
import math

import jax
import jax.numpy as jnp
from jax import lax
from jax.experimental import pallas as pl
from jax.experimental.pallas import tpu as pltpu

F32 = jnp.float32
BF = jnp.bfloat16
MESH = pl.DeviceIdType.MESH
ARB = "arbitrary"
PAR = "parallel"

EPS = 1e-6
NEG_INF = -1e30
ROPE_BASE = 10000.0
CHUNK = 64
BRANCH_W = 512
H_RET = 4
DK_RET = 128
H_ATT = 8
DH_ATT = 64
N_PREV = 8
REL_CLIP = 128
N_REL = 2 * REL_CLIP + 1
N_SHARD = 4
LANE = 128
RET_L = 512
ATT_TQ = 128
ATT_SUB = 4
ATT_PAD = N_PREV * CHUNK
ATT_SPAN = ATT_TQ + ATT_PAD
ATT_TOEP = 2 * REL_CLIP
RB_PAD = 264
TM = 512

ADAM_LR = 0.001
ADAM_B1 = 0.9
ADAM_B2 = 0.999
ADAM_EPS = 1e-08
ADAM_WD = 0.01
ADAM_STEP = 10

NT_DIMS = (((1,), (1,)), ((), ()))
TN_DIMS = (((0,), (0,)), ((), ()))


def _cp(sem, vmem_mb=48):
    return pltpu.CompilerParams(dimension_semantics=sem, vmem_limit_bytes=vmem_mb << 20)


def _sds(shape, dtype):
    return jax.ShapeDtypeStruct(tuple(shape), dtype)


def _rms_r(x):
    return lax.rsqrt(jnp.mean(x * x, axis=-1, keepdims=True) + EPS)


def _sigmoid(x):
    return jax.nn.sigmoid(x)


def _rms_bwd(dh, xv, nw):
    r = _rms_r(xv)
    xh = xv * r
    dxh = dh * nw
    dx = r * (dxh - xh * jnp.mean(dxh * xh, axis=-1, keepdims=True))
    return dx, jnp.sum(dh * xh, axis=0, keepdims=True)


def _place():
    return lax.axis_index("x"), lax.axis_index("y"), lax.axis_index("c")


def _other_chips(x, y):
    return [(1 - x, y), (x, 1 - y), (1 - x, 1 - y)]


class _Pieces:
    def __init__(self, srcs, scatter):
        self.srcs = list(srcs)
        self.scatter = scatter
        n = len(self.srcs)
        self.out_shape = [_sds(s.shape if scatter else (N_SHARD,) + s.shape, s.dtype) for s in self.srcs]
        self.scratch = [pltpu.SemaphoreType.DMA((n,)), pltpu.SemaphoreType.DMA((3, n)), pltpu.SemaphoreType.DMA((3, n))]

    def _copies(self, src, dst, sems):
        lsem, ssem, rsem = sems
        x, y, c = _place()
        mine = 2 * x + y
        n = len(src)

        def remote(j, k, chip, s_ref, d_ref):
            return pltpu.make_async_remote_copy(
                src_ref=s_ref, dst_ref=d_ref, send_sem=ssem.at[j, k], recv_sem=rsem.at[j, k],
                device_id=(chip[0], chip[1], c), device_id_type=MESH)

        chips = list(enumerate(_other_chips(x, y)))
        if self.scatter:
            local = [pltpu.make_async_copy(src[k].at[mine], dst[k].at[3], lsem.at[k]) for k in range(n)]
            sends = [remote(j, k, ch, src[k].at[2 * ch[0] + ch[1]], dst[k].at[j]) for j, ch in chips for k in range(n)]
            recvs = sends
        else:
            local = [pltpu.make_async_copy(src[k], dst[k].at[mine], lsem.at[k]) for k in range(n)]
            sends = [remote(j, k, ch, src[k], dst[k].at[mine]) for j, ch in chips for k in range(n)]
            recvs = [remote(j, k, ch, src[k], dst[k].at[2 * ch[0] + ch[1]]) for j, ch in chips for k in range(n)]
        return local, sends, recvs

    def start(self, src, dst, sems):
        local, sends, _ = self._copies(src, dst, sems)
        for cp in local + sends:
            cp.start()

    def wait(self, src, dst, sems):
        local, sends, recvs = self._copies(src, dst, sems)
        for cp in recvs:
            cp.wait_recv()
        for cp in sends:
            cp.wait_send()
        for cp in local:
            cp.wait()


def _call(body, *, name, args, in_specs, out_specs, out_shape, grid=(), scratch_shapes=(), sem=None, comm=None,
          aliases=None, vmem_mb=48):
    in_specs, out_specs, out_shape = list(in_specs), list(out_specs), list(out_shape)
    scratch, args = list(scratch_shapes), list(args)
    n_in, n_out, n_scr = len(in_specs), len(out_specs), len(scratch)
    if comm is None:
        def kernel_body(*refs):
            body(*refs)
    else:
        c_in, c_out = len(comm.srcs), len(comm.out_shape)

        def kernel_body(*refs):
            o0 = n_in + c_in
            s0 = o0 + n_out + c_out
            cin, cout, sems = refs[n_in:o0], refs[o0 + n_out:s0], refs[s0 + n_scr:]
            main = refs[:n_in] + refs[o0:o0 + n_out] + refs[s0:s0 + n_scr]
            if grid:
                ids = [pl.program_id(a) for a in range(len(grid))]
                first = functools.reduce(lambda p, q: p & q, [i == 0 for i in ids])
                last = functools.reduce(lambda p, q: p & q, [i == g - 1 for i, g in zip(ids, grid)])

                @pl.when(first)
                def _():
                    comm.start(cin, cout, sems)

                body(*main)

                @pl.when(last)
                def _():
                    comm.wait(cin, cout, sems)
            else:
                comm.start(cin, cout, sems)
                body(*main)
                comm.wait(cin, cout, sems)

        hbm = pl.BlockSpec(memory_space=pl.ANY)
        in_specs += [hbm] * c_in
        out_specs += [hbm] * c_out
        out_shape += comm.out_shape
        scratch += comm.scratch
        args += comm.srcs
    params = dict(vmem_limit_bytes=vmem_mb << 20)
    if grid:
        params["dimension_semantics"] = sem
    outs = pl.pallas_call(
        kernel_body, name=name, grid=grid, in_specs=in_specs, out_specs=out_specs, out_shape=out_shape,
        scratch_shapes=scratch, input_output_aliases=aliases or {}, compiler_params=pltpu.CompilerParams(**params),
    )(*args)
    return list(outs[:n_out]), list(outs[n_out:])


def _comm_alone(comm, name):
    return _call(lambda: None, name=name, args=[], in_specs=[], out_specs=[], out_shape=[], comm=comm)[1]


def _ffn_fwd(x, nw, wg, wu, wd, name, comm=None):
    T, D = x.shape
    ns, _, fs = wg.shape
    tm = min(TM, T)

    def body(x_ref, nw_ref, wg_ref, wu_ref, wd_ref, xo_ref, g_ref, u_ref, h_s, acc_s):
        j = pl.program_id(1)

        @pl.when(j == 0)
        def _():
            xv = x_ref[...]
            h_s[...] = (xv * _rms_r(xv) * nw_ref[...]).astype(BF)
            acc_s[...] = jnp.zeros_like(acc_s)

        h = h_s[...]
        gb = jnp.dot(h, wg_ref[...], preferred_element_type=F32).astype(BF)
        ub = jnp.dot(h, wu_ref[...], preferred_element_type=F32).astype(BF)
        g_ref[...] = gb
        u_ref[...] = ub
        g = gb.astype(F32)
        a = (g * _sigmoid(g) * ub.astype(F32)).astype(BF)
        acc_s[...] += jnp.dot(a, wd_ref[...], preferred_element_type=F32)

        @pl.when(j == ns - 1)
        def _():
            xo_ref[...] = x_ref[...] + 0.5 * acc_s[...]

    wspec = pl.BlockSpec((None, D, fs), lambda i, j: (j, 0, 0))
    return _call(
        body, name=name, grid=(T // tm, ns), args=(x, nw, wg, wu, wd), comm=comm,
        in_specs=[pl.BlockSpec((tm, D), lambda i, j: (i, 0)),
                  pl.BlockSpec((1, D), lambda i, j: (0, 0)),
                  wspec, wspec,
                  pl.BlockSpec((None, fs, D), lambda i, j: (j, 0, 0))],
        out_specs=[pl.BlockSpec((tm, D), lambda i, j: (i, 0)),
                   pl.BlockSpec((None, tm, fs), lambda i, j: (j, i, 0)),
                   pl.BlockSpec((None, tm, fs), lambda i, j: (j, i, 0))],
        out_shape=[_sds((T, D), F32), _sds((ns, T, fs), BF), _sds((ns, T, fs), BF)],
        scratch_shapes=[pltpu.VMEM((tm, D), BF), pltpu.VMEM((tm, D), F32)],
        sem=(ARB, ARB))


def _ffn_bwd(dxo, x, nw, g, u, wg, wu, wd, name, comm=None):
    T, D = x.shape
    ns, _, fs = wg.shape
    tm = min(TM, T)

    def body(dxo_ref, x_ref, nw_ref, g_ref, u_ref, wg_ref, wu_ref, wd_ref,
             dx_ref, dg_ref, du_ref, a_ref, h_ref, dacc_ref, dnw_ref, dacc_s, acc_s):
        i = pl.program_id(0)
        j = pl.program_id(1)

        @pl.when(j == 0)
        def _():
            xv = x_ref[...]
            h_ref[...] = (xv * _rms_r(xv) * nw_ref[...]).astype(BF)
            db = (0.5 * dxo_ref[...]).astype(BF)
            dacc_ref[...] = db
            dacc_s[...] = db
            acc_s[...] = jnp.zeros_like(acc_s)

        @pl.when((i == 0) & (j == 0))
        def _():
            dnw_ref[...] = jnp.zeros_like(dnw_ref)

        da = lax.dot_general(dacc_s[...], wd_ref[...], NT_DIMS, preferred_element_type=F32)
        gv = g_ref[...].astype(F32)
        uv = u_ref[...].astype(F32)
        s = _sigmoid(gv)
        sg = gv * s
        a_ref[...] = (sg * uv).astype(BF)
        dub = (da * sg).astype(BF)
        dgb = (da * uv * (s * (1.0 + gv * (1.0 - s)))).astype(BF)
        dg_ref[...] = dgb
        du_ref[...] = dub
        acc_s[...] += (lax.dot_general(dgb, wg_ref[...], NT_DIMS, preferred_element_type=F32)
                       + lax.dot_general(dub, wu_ref[...], NT_DIMS, preferred_element_type=F32))

        @pl.when(j == ns - 1)
        def _():
            dx, dn = _rms_bwd(acc_s[...], x_ref[...], nw_ref[...])
            dx_ref[...] = dxo_ref[...] + dx
            dnw_ref[...] += dn

    tok = pl.BlockSpec((tm, D), lambda i, j: (i, 0))
    row = pl.BlockSpec((1, D), lambda i, j: (0, 0))
    hid = pl.BlockSpec((None, tm, fs), lambda i, j: (j, i, 0))
    wspec = pl.BlockSpec((None, D, fs), lambda i, j: (j, 0, 0))
    return _call(
        body, name=name, grid=(T // tm, ns), args=(dxo, x, nw, g, u, wg, wu, wd), comm=comm,
        in_specs=[tok, tok, row, hid, hid, wspec, wspec,
                  pl.BlockSpec((None, fs, D), lambda i, j: (j, 0, 0))],
        out_specs=[tok, hid, hid, hid, tok, tok, row],
        out_shape=[_sds((T, D), F32), _sds((ns, T, fs), BF), _sds((ns, T, fs), BF), _sds((ns, T, fs), BF),
                   _sds((T, D), BF), _sds((T, D), BF), _sds((1, D), F32)],
        scratch_shapes=[pltpu.VMEM((tm, D), BF), pltpu.VMEM((tm, D), F32)],
        sem=(ARB, ARB))


def _tn(a, b, a_spec, b_spec, out_shape, out_spec, grid, name, prev=None, comm=None):
    nk = grid[-1]
    acc_shape = tuple(d for d in out_spec.block_shape if d is not None)

    def body(*refs):
        a_ref, b_ref = refs[0], refs[1]
        o_ref, acc = refs[-2], refs[-1]
        k = pl.program_id(2)
        prod = lax.dot_general(a_ref[...], b_ref[...], TN_DIMS, preferred_element_type=F32)

        @pl.when(k == 0)
        def _():
            acc[...] = prod

        @pl.when(k > 0)
        def _():
            acc[...] += prod

        @pl.when(k == nk - 1)
        def _():
            o_ref[...] = acc[...].astype(o_ref.dtype)

    in_specs = [a_spec, b_spec]
    args = [a, b]
    aliases = {}
    if prev is not None:
        in_specs.append(pl.BlockSpec(memory_space=pl.ANY))
        args.append(prev)
        aliases = {2: 0}
    main, extra = _call(
        body, name=name, grid=grid, args=args, in_specs=in_specs, out_specs=[out_spec], out_shape=[out_shape],
        scratch_shapes=[pltpu.VMEM(acc_shape, F32)], aliases=aliases, sem=(ARB, ARB, ARB), comm=comm)
    return main[0] if comm is None else (main[0], extra)


def _inproj_fwd(x, nw, wbig, name):
    T, D = x.shape
    nb = wbig.shape[-1]
    tm = min(2 * TM, T)
    bn = min(2048, nb)

    def body(x_ref, nw_ref, w_ref, o_ref, h_ref, h_s):
        @pl.when(pl.program_id(1) == 0)
        def _():
            xv = x_ref[...]
            hb = (xv * _rms_r(xv) * nw_ref[...]).astype(BF)
            h_s[...] = hb
            h_ref[...] = hb

        o_ref[...] = jnp.dot(h_s[...], w_ref[...], preferred_element_type=F32).astype(BF)

    return pl.pallas_call(
        body, name=name, grid=(T // tm, nb // bn),
        in_specs=[pl.BlockSpec((tm, D), lambda i, n: (i, 0)),
                  pl.BlockSpec((1, D), lambda i, n: (0, 0)),
                  pl.BlockSpec((D, bn), lambda i, n: (0, n))],
        out_specs=[pl.BlockSpec((tm, bn), lambda i, n: (i, n)),
                   pl.BlockSpec((tm, D), lambda i, n: (i, 0))],
        out_shape=[_sds((T, nb), BF), _sds((T, D), BF)],
        scratch_shapes=[pltpu.VMEM((tm, D), BF)],
        compiler_params=_cp((PAR, ARB)),
    )(x, nw, wbig)


def _inproj_bwd(dbig, wbig, x, nw, dxin, name):
    T, D = x.shape
    nb = wbig.shape[-1]
    tm = min(TM, T)
    tk = min(2048, nb)
    nk = nb // tk

    def body(a_ref, w_ref, x_ref, nw_ref, dxin_ref, dx_ref, dnw_ref, acc_s):
        i = pl.program_id(0)
        k = pl.program_id(1)
        prod = lax.dot_general(a_ref[...], w_ref[...], NT_DIMS, preferred_element_type=F32)

        @pl.when((i == 0) & (k == 0))
        def _():
            dnw_ref[...] = jnp.zeros_like(dnw_ref)

        @pl.when(k == 0)
        def _():
            acc_s[...] = prod

        @pl.when(k > 0)
        def _():
            acc_s[...] += prod

        @pl.when(k == nk - 1)
        def _():
            dx, dn = _rms_bwd(acc_s[...], x_ref[...], nw_ref[...])
            dx_ref[...] = dxin_ref[...] + dx
            dnw_ref[...] += dn

    tok = pl.BlockSpec((tm, D), lambda i, k: (i, 0))
    row = pl.BlockSpec((1, D), lambda i, k: (0, 0))
    return pl.pallas_call(
        body, name=name, grid=(T // tm, nk),
        in_specs=[pl.BlockSpec((tm, tk), lambda i, k: (i, k)),
                  pl.BlockSpec((D, tk), lambda i, k: (0, k)),
                  tok, row, tok],
        out_specs=[tok, row],
        out_shape=[_sds((T, D), F32), _sds((1, D), F32)],
        scratch_shapes=[pltpu.VMEM((tm, D), F32)],
        compiler_params=_cp((ARB, ARB)),
    )(dbig, wbig, x, nw, dxin)


CONV_R = 512
COL_CU, COL_CB, COL_CC = 0, 4, 8
COL_RQ, COL_RK, COL_RV, COL_RG = 12, 16, 20, 24
COL_AQ, COL_AK, COL_AV = 28, 32, 36


def _seg0(big):
    return (big.shape[1] - 10 * BRANCH_W) // LANE


def _conv_fwd(big, cw, name):
    T = big.shape[0]
    R = min(CONV_R, T)

    def body(cu_ref, cb_ref, cc_ref, w_ref, y_ref, z_s):
        z_s[pl.ds(0, 8), :] = jnp.zeros((8, LANE), F32)

        def fill(t, c):
            sl = pl.ds(pl.multiple_of(t * R, R), R)
            z_s[pl.ds(pl.multiple_of(t * R + 8, 8), R), :] = cc_ref[sl, :].astype(F32) * cu_ref[sl, :].astype(F32)
            return c

        lax.fori_loop(0, T // R, fill, 0)
        w0, w1, w2 = w_ref[0:1, :], w_ref[1:2, :], w_ref[2:3, :]

        def step(t, c):
            zz = z_s[pl.ds(pl.multiple_of(t * R, R), R + 8), :]
            z0 = zz[8:]
            z1 = pltpu.roll(zz, 1, 0)[8:]
            z2 = pltpu.roll(zz, 2, 0)[8:]
            sl = pl.ds(pl.multiple_of(t * R, R), R)
            y_ref[sl, :] = (cb_ref[sl, :].astype(F32) * (w2 * z0 + w1 * z1 + w0 * z2)).astype(BF)
            return c

        lax.fori_loop(0, T // R, step, 0)

    def col(base):
        return pl.BlockSpec((T, LANE), lambda j: (0, _seg0(big) + base + j))

    return pl.pallas_call(
        body, name=name, grid=(BRANCH_W // LANE,),
        in_specs=[col(COL_CU), col(COL_CB), col(COL_CC), pl.BlockSpec((3, LANE), lambda j: (0, j))],
        out_specs=pl.BlockSpec((T, LANE), lambda j: (0, j)),
        out_shape=_sds((T, BRANCH_W), BF),
        scratch_shapes=[pltpu.VMEM((T + 8, LANE), F32)],
        compiler_params=_cp((PAR,)),
    )(big, big, big, cw)


def _conv_bwd(big, dy, cw, name):
    T = big.shape[0]
    R = min(CONV_R, T)

    def body(cu_ref, cb_ref, cc_ref, dy_ref, w_ref, dcu_ref, dcb_ref, dcc_ref, dw_ref, z_s, d_s):
        z_s[pl.ds(0, 8), :] = jnp.zeros((8, LANE), F32)
        d_s[pl.ds(T, 8), :] = jnp.zeros((8, LANE), F32)

        def fill(t, c):
            sl = pl.ds(pl.multiple_of(t * R, R), R)
            z_s[pl.ds(pl.multiple_of(t * R + 8, 8), R), :] = cc_ref[sl, :].astype(F32) * cu_ref[sl, :].astype(F32)
            d_s[sl, :] = dy_ref[sl, :].astype(F32) * cb_ref[sl, :].astype(F32)
            return c

        lax.fori_loop(0, T // R, fill, 0)
        w0, w1, w2 = w_ref[0:1, :], w_ref[1:2, :], w_ref[2:3, :]

        def step(t, carry):
            a0, a1, a2 = carry
            zz = z_s[pl.ds(pl.multiple_of(t * R, R), R + 8), :]
            z0 = zz[8:]
            z1 = pltpu.roll(zz, 1, 0)[8:]
            z2 = pltpu.roll(zz, 2, 0)[8:]
            sl = pl.ds(pl.multiple_of(t * R, R), R)
            dyv = dy_ref[sl, :].astype(F32)
            dcb_ref[sl, :] = (dyv * (w2 * z0 + w1 * z1 + w0 * z2)).astype(BF)
            dd = d_s[pl.ds(pl.multiple_of(t * R, R), R + 8), :]
            d0 = dd[:R]
            d1 = pltpu.roll(dd, R + 7, 0)[:R]
            d2 = pltpu.roll(dd, R + 6, 0)[:R]
            dz = w2 * d0 + w1 * d1 + w0 * d2
            dcc_ref[sl, :] = (dz * cu_ref[sl, :].astype(F32)).astype(BF)
            dcu_ref[sl, :] = (dz * cc_ref[sl, :].astype(F32)).astype(BF)
            a0 = a0 + jnp.sum(d0 * z2, axis=0, keepdims=True)
            a1 = a1 + jnp.sum(d0 * z1, axis=0, keepdims=True)
            a2 = a2 + jnp.sum(d0 * z0, axis=0, keepdims=True)
            return a0, a1, a2

        zero = jnp.zeros((1, LANE), F32)
        a0, a1, a2 = lax.fori_loop(0, T // R, step, (zero, zero, zero))
        dw_ref[0:1, :] = a0
        dw_ref[1:2, :] = a1
        dw_ref[2:3, :] = a2

    def col(base):
        return pl.BlockSpec((T, LANE), lambda j: (0, _seg0(big) + base + j))

    out = pl.BlockSpec((T, LANE), lambda j: (0, j))
    w = pl.BlockSpec((3, LANE), lambda j: (0, j))
    return pl.pallas_call(
        body, name=name, grid=(BRANCH_W // LANE,),
        in_specs=[col(COL_CU), col(COL_CB), col(COL_CC), out, w],
        out_specs=[out, out, out, w],
        out_shape=[_sds((T, BRANCH_W), BF)] * 3 + [_sds((3, BRANCH_W), F32)],
        scratch_shapes=[pltpu.VMEM((T + 8, LANE), F32), pltpu.VMEM((T + 8, LANE), F32)],
        compiler_params=_cp((PAR,)),
    )(big, big, big, dy, cw)


def _ret_tables(T):
    L = min(RET_L, T)
    hh = jnp.arange(H_RET, dtype=F32)
    lg = jnp.log1p(-jnp.exp2(-5.0 - hh))
    n = jnp.arange(L, dtype=F32)
    a = jnp.exp(lg[:, None] * (n + 1.0))
    b = jnp.exp(lg[:, None] * (L - 1.0 - n))
    gl = jnp.exp(lg * L)
    ch = jnp.arange(L) // CHUNK
    m = jnp.exp(lg[:, None, None] * jnp.abs(n[:, None] - n[None, :])) * (ch[None, :] <= ch[:, None]).astype(F32)
    inv_freq = ROPE_BASE ** (-jnp.linspace(0.0, 1.0, DK_RET // 2, dtype=F32))
    ang = jnp.arange(T, dtype=F32)[:, None] * inv_freq[None, :]
    cos, sin = jnp.cos(ang), jnp.sin(ang)
    return dict(
        L=L, M=m,
        a=jnp.broadcast_to(a[:, :, None], (H_RET, L, DK_RET)),
        b=jnp.broadcast_to(b[:, :, None], (H_RET, L, DK_RET)),
        gl=jnp.broadcast_to(gl[:, None, None], (H_RET, 1, DK_RET)),
        cos=jnp.concatenate([cos, cos], axis=-1), sin=jnp.concatenate([-sin, sin], axis=-1))


def _rot(x, cs, sn):
    return x * cs + pltpu.roll(x, DK_RET // 2, 1) * sn


def _unrot(dy, cs, sn):
    return dy * cs + pltpu.roll(dy * sn, DK_RET // 2, 1)


def _ret_fwd(big, tb, name, comm=None):
    T = big.shape[0]
    L = tb["L"]
    nsc = T // L
    scale = DK_RET ** -0.5

    def body(q_ref, k_ref, v_ref, g_ref, cos_ref, sin_ref, m_ref, a_ref, b_ref, gl_ref,
             y_ref, o_ref, st_ref, s_s):
        @pl.when(pl.program_id(1) == 0)
        def _():
            s_s[...] = jnp.zeros_like(s_s)

        cs, sn = cos_ref[...], sin_ref[...]
        qt = _rot(q_ref[...].astype(F32), cs, sn) * scale
        kt = _rot(k_ref[...].astype(F32), cs, sn)
        qb, kb, vb = qt.astype(BF), kt.astype(BF), v_ref[...]
        s_prev = s_s[...]
        st_ref[...] = s_prev
        p = lax.dot_general(qb, kb, NT_DIMS, preferred_element_type=F32) * m_ref[...]
        o = (jnp.dot(p.astype(BF), vb, preferred_element_type=F32)
             + jnp.dot((qt * a_ref[...]).astype(BF), s_prev.astype(BF), preferred_element_type=F32))
        s_s[...] = s_prev * gl_ref[...] + lax.dot_general((kt * b_ref[...]).astype(BF), vb, TN_DIMS,
                                                         preferred_element_type=F32)
        o_ref[...] = o
        gv = g_ref[...].astype(F32)
        y_ref[...] = (gv * _sigmoid(gv) * o * _rms_r(o)).astype(BF)

    def col(base):
        return pl.BlockSpec((L, LANE), lambda h, i: (i, _seg0(big) + base + h))

    tab = pl.BlockSpec((L, DK_RET), lambda h, i: (i, 0))
    per_head = pl.BlockSpec((None, L, DK_RET), lambda h, i: (h, 0, 0))
    out = pl.BlockSpec((L, LANE), lambda h, i: (i, h))
    return _call(
        body, name=name, grid=(H_RET, nsc), comm=comm,
        args=(big, big, big, big, tb["cos"], tb["sin"], tb["M"], tb["a"], tb["b"], tb["gl"]),
        in_specs=[col(COL_RQ), col(COL_RK), col(COL_RV), col(COL_RG), tab, tab,
                  pl.BlockSpec((None, L, L), lambda h, i: (h, 0, 0)), per_head, per_head,
                  pl.BlockSpec((None, 1, DK_RET), lambda h, i: (h, 0, 0))],
        out_specs=[out, out, pl.BlockSpec((None, None, DK_RET, DK_RET), lambda h, i: (i, h, 0, 0))],
        out_shape=[_sds((T, BRANCH_W), BF), _sds((T, BRANCH_W), F32), _sds((nsc, H_RET, DK_RET, DK_RET), F32)],
        scratch_shapes=[pltpu.VMEM((DK_RET, DK_RET), F32)],
        sem=(ARB, ARB))


def _ret_bwd(big, o, st, dy, tb, name):
    T = big.shape[0]
    L = tb["L"]
    nsc = T // L
    scale = DK_RET ** -0.5

    def body(q_ref, k_ref, v_ref, g_ref, cos_ref, sin_ref, m_ref, a_ref, b_ref, gl_ref, o_ref, st_ref, dy_ref,
             dq_ref, dk_ref, dv_ref, dg_ref, ds_s):
        @pl.when(pl.program_id(1) == 0)
        def _():
            ds_s[...] = jnp.zeros_like(ds_s)

        cs, sn = cos_ref[...], sin_ref[...]
        mm, av, bv = m_ref[...], a_ref[...], b_ref[...]
        qt = _rot(q_ref[...].astype(F32), cs, sn) * scale
        kt = _rot(k_ref[...].astype(F32), cs, sn)
        qb, kb, vb = qt.astype(BF), kt.astype(BF), v_ref[...]
        pb = (lax.dot_general(qb, kb, NT_DIMS, preferred_element_type=F32) * mm).astype(BF)
        ov = o_ref[...]
        r = _rms_r(ov)
        oh = ov * r
        gv = g_ref[...].astype(F32)
        sg = _sigmoid(gv)
        dyv = dy_ref[...].astype(F32)
        dg_ref[...] = (dyv * oh * (sg * (1.0 + gv * (1.0 - sg)))).astype(BF)
        doh = dyv * gv * sg
        dob = (r * (doh - oh * jnp.mean(doh * oh, axis=-1, keepdims=True))).astype(BF)
        dsb = ds_s[...].astype(BF)
        spb = st_ref[...].astype(BF)
        dpb = (lax.dot_general(dob, vb, NT_DIMS, preferred_element_type=F32) * mm).astype(BF)
        dqt = (jnp.dot(dpb, kb, preferred_element_type=F32)
               + lax.dot_general(dob, spb, NT_DIMS, preferred_element_type=F32) * av)
        dkt = (lax.dot_general(dpb, qb, TN_DIMS, preferred_element_type=F32)
               + lax.dot_general(vb, dsb, NT_DIMS, preferred_element_type=F32) * bv)
        dv = (lax.dot_general(pb, dob, TN_DIMS, preferred_element_type=F32)
              + jnp.dot((kt * bv).astype(BF), dsb, preferred_element_type=F32))
        ds_s[...] = ds_s[...] * gl_ref[...] + lax.dot_general((qt * av).astype(BF), dob, TN_DIMS,
                                                              preferred_element_type=F32)
        dq_ref[...] = (_unrot(dqt, cs, sn) * scale).astype(BF)
        dk_ref[...] = _unrot(dkt, cs, sn).astype(BF)
        dv_ref[...] = dv.astype(BF)

    def rev(i):
        return nsc - 1 - i

    def col(base):
        return pl.BlockSpec((L, LANE), lambda h, i: (rev(i), _seg0(big) + base + h))

    tab = pl.BlockSpec((L, DK_RET), lambda h, i: (rev(i), 0))
    per_head = pl.BlockSpec((None, L, DK_RET), lambda h, i: (h, 0, 0))
    out = pl.BlockSpec((L, LANE), lambda h, i: (rev(i), h))
    return pl.pallas_call(
        body, name=name, grid=(H_RET, nsc),
        in_specs=[col(COL_RQ), col(COL_RK), col(COL_RV), col(COL_RG), tab, tab,
                  pl.BlockSpec((None, L, L), lambda h, i: (h, 0, 0)), per_head, per_head,
                  pl.BlockSpec((None, 1, DK_RET), lambda h, i: (h, 0, 0)),
                  out, pl.BlockSpec((None, None, DK_RET, DK_RET), lambda h, i: (rev(i), h, 0, 0)), out],
        out_specs=[out, out, out, out],
        out_shape=[_sds((T, BRANCH_W), BF)] * 4,
        scratch_shapes=[pltpu.VMEM((DK_RET, DK_RET), F32)],
        compiler_params=_cp((PAR, ARB)),
    )(big, big, big, big, tb["cos"], tb["sin"], tb["M"], tb["a"], tb["b"], tb["gl"], o, st, dy)


def _relbias_onehot(n):
    mm = lax.broadcasted_iota(jnp.int32, (RB_PAD, ATT_TOEP), 1)
    rr = lax.broadcasted_iota(jnp.int32, (RB_PAD, ATT_TOEP), 0)
    idx = jnp.clip(n + ATT_TOEP - mm, 0, 2 * REL_CLIP)
    return (rr == idx).astype(F32)


def _relbias_expand(rbp, name):
    far = ATT_SPAN - ATT_TOEP

    def body(rb_ref, o_ref):
        rb = rb_ref[...]
        const = jnp.broadcast_to(rb[:, 2 * REL_CLIP:2 * REL_CLIP + 1], (H_ATT, far))

        def row(n, c):
            toep = jnp.dot(rb, _relbias_onehot(n), preferred_element_type=F32, precision=lax.Precision.HIGHEST)
            m = lax.broadcasted_iota(jnp.int32, (1, ATT_SPAN), 1)
            d = n // CHUNK + N_PREV - m // CHUNK
            neg = jnp.where((d >= 0) & (d <= N_PREV), 0.0, NEG_INF).astype(F32)
            o_ref[n] = jnp.concatenate([const, toep], axis=1) + neg
            return c

        lax.fori_loop(0, ATT_TQ, row, 0)

    return pl.pallas_call(
        body, name=name,
        in_specs=[pl.BlockSpec(memory_space=pltpu.VMEM)],
        out_specs=pl.BlockSpec(memory_space=pltpu.VMEM),
        out_shape=_sds((ATT_TQ, H_ATT, ATT_SPAN), F32),
    )(rbp)


def _relbias_grad(dbt, name):
    far = ATT_SPAN - ATT_TOEP

    def body(d_ref, o_ref):
        def row(n, carry):
            acc, cs = carry
            dn = d_ref[n]
            acc = acc + lax.dot_general(dn[:, far:], _relbias_onehot(n), NT_DIMS, preferred_element_type=F32,
                                        precision=lax.Precision.HIGHEST)
            cs = cs + jnp.sum(dn[:, :far], axis=1, keepdims=True)
            return acc, cs

        acc, cs = lax.fori_loop(0, ATT_TQ, row, (jnp.zeros((H_ATT, RB_PAD), F32), jnp.zeros((H_ATT, 1), F32)))
        rr = lax.broadcasted_iota(jnp.int32, (H_ATT, RB_PAD), 1)
        o_ref[...] = acc + jnp.where(rr == 2 * REL_CLIP, cs, 0.0)

    return pl.pallas_call(
        body, name=name,
        in_specs=[pl.BlockSpec(memory_space=pltpu.VMEM)],
        out_specs=pl.BlockSpec(memory_space=pltpu.VMEM),
        out_shape=_sds((H_ATT, RB_PAD), F32),
    )(dbt)


def _att_pad_fill(dst_s, src_ref, T):
    dst_s[pl.ds(0, ATT_PAD), :] = jnp.zeros((ATT_PAD, LANE), dst_s.dtype)
    R = min(512, T)

    def cp(t, c):
        dst_s[pl.ds(pl.multiple_of(ATT_PAD + t * R, LANE), R), :] = src_ref[pl.ds(pl.multiple_of(t * R, R), R), :]
        return c

    lax.fori_loop(0, T // R, cp, 0)


def _att_probs(qh, kh, bias, valid):
    s = lax.dot_general(qh, kh, NT_DIMS, preferred_element_type=F32) * (DH_ATT ** -0.5) + bias
    s = jnp.where(valid, s, NEG_INF)
    p = jnp.exp(s - jnp.max(s, axis=-1, keepdims=True))
    return p / jnp.sum(p, axis=-1, keepdims=True)


def _att_fwd(big, bias, name, comm=None):
    T = big.shape[0]
    rows = ATT_SUB * ATT_TQ
    nt = T // rows

    def body(q_ref, k_ref, v_ref, b_ref, y_ref, kp_s, vp_s):
        i = pl.program_id(1)

        @pl.when(i == 0)
        def _():
            _att_pad_fill(kp_s, k_ref, T)
            _att_pad_fill(vp_s, v_ref, T)

        for sub in range(ATT_SUB):
            t0 = pl.multiple_of(i * rows + sub * ATT_TQ, ATT_TQ)
            qs = pl.ds(sub * ATT_TQ, ATT_TQ)
            kw = kp_s[pl.ds(t0, ATT_SPAN), :]
            vw = vp_s[pl.ds(t0, ATT_SPAN), :]
            valid = (t0 - ATT_PAD + lax.broadcasted_iota(jnp.int32, (1, ATT_SPAN), 1)) >= 0
            outs = []
            for hh in range(2):
                sl = slice(hh * DH_ATT, (hh + 1) * DH_ATT)
                pn = _att_probs(q_ref[qs, sl], kw[:, sl], b_ref[hh], valid)
                outs.append(jnp.dot(pn.astype(BF), vw[:, sl], preferred_element_type=F32))
            y_ref[qs, :] = jnp.concatenate(outs, axis=1).astype(BF)

    def whole(base):
        return pl.BlockSpec((T, LANE), lambda p, i: (0, _seg0(big) + base + p))

    return _call(
        body, name=name, grid=(H_ATT // 2, nt), args=(big, big, big, bias), comm=comm,
        in_specs=[pl.BlockSpec((rows, LANE), lambda p, i: (i, _seg0(big) + COL_AQ + p)), whole(COL_AK), whole(COL_AV),
                  pl.BlockSpec((2, ATT_TQ, ATT_SPAN), lambda p, i: (p, 0, 0))],
        out_specs=[pl.BlockSpec((rows, LANE), lambda p, i: (i, p))],
        out_shape=[_sds((T, BRANCH_W), BF)],
        scratch_shapes=[pltpu.VMEM((T + ATT_PAD, LANE), BF), pltpu.VMEM((T + ATT_PAD, LANE), BF)],
        sem=(ARB, ARB))


def _att_bwd(big, bias, dy, name, comm=None):
    T = big.shape[0]
    rows = ATT_SUB * ATT_TQ
    nt = T // rows
    scale = DH_ATT ** -0.5

    def body(q_ref, k_ref, v_ref, b_ref, dy_ref, dq_ref, dk_ref, dv_ref, db_ref, kp_s, vp_s, dk_s, dv_s):
        i = pl.program_id(1)

        @pl.when(i == 0)
        def _():
            _att_pad_fill(kp_s, k_ref, T)
            _att_pad_fill(vp_s, v_ref, T)
            dk_s[...] = jnp.zeros_like(dk_s)
            dv_s[...] = jnp.zeros_like(dv_s)
            db_ref[...] = jnp.zeros_like(db_ref)

        dbs = [None, None]
        for sub in range(ATT_SUB):
            t0 = pl.multiple_of(i * rows + sub * ATT_TQ, ATT_TQ)
            qs = pl.ds(sub * ATT_TQ, ATT_TQ)
            win = pl.ds(t0, ATT_SPAN)
            kw = kp_s[win, :]
            vw = vp_s[win, :]
            valid = (t0 - ATT_PAD + lax.broadcasted_iota(jnp.int32, (1, ATT_SPAN), 1)) >= 0
            dqs, dks, dvs = [], [], []
            for hh in range(2):
                sl = slice(hh * DH_ATT, (hh + 1) * DH_ATT)
                qh, kh, vh = q_ref[qs, sl], kw[:, sl], vw[:, sl]
                pn = _att_probs(qh, kh, b_ref[hh], valid)
                doh = dy_ref[qs, sl]
                dp = lax.dot_general(doh, vh, NT_DIMS, preferred_element_type=F32)
                ds = pn * (dp - jnp.sum(dp * pn, axis=-1, keepdims=True))
                dbs[hh] = ds if dbs[hh] is None else dbs[hh] + ds
                dsb = ds.astype(BF)
                dqs.append(jnp.dot(dsb, kh, preferred_element_type=F32) * scale)
                dks.append(lax.dot_general(dsb, qh, TN_DIMS, preferred_element_type=F32) * scale)
                dvs.append(lax.dot_general(pn.astype(BF), doh, TN_DIMS, preferred_element_type=F32))
            dq_ref[qs, :] = jnp.concatenate(dqs, axis=1).astype(BF)
            dk_s[win, :] += jnp.concatenate(dks, axis=1)
            dv_s[win, :] += jnp.concatenate(dvs, axis=1)
        for hh in range(2):
            db_ref[hh] += dbs[hh]

        @pl.when(i == nt - 1)
        def _():
            R = min(512, T)

            def cp(t, c):
                src = pl.ds(pl.multiple_of(ATT_PAD + t * R, LANE), R)
                dst = pl.ds(pl.multiple_of(t * R, R), R)
                dk_ref[dst, :] = dk_s[src, :].astype(BF)
                dv_ref[dst, :] = dv_s[src, :].astype(BF)
                return c

            lax.fori_loop(0, T // R, cp, 0)

    def whole(base):
        return pl.BlockSpec((T, LANE), lambda p, i: (0, _seg0(big) + base + p))

    tile = pl.BlockSpec((rows, LANE), lambda p, i: (i, p))
    bspec = pl.BlockSpec((2, ATT_TQ, ATT_SPAN), lambda p, i: (p, 0, 0))
    return _call(
        body, name=name, grid=(H_ATT // 2, nt), args=(big, big, big, bias, dy), comm=comm,
        in_specs=[pl.BlockSpec((rows, LANE), lambda p, i: (i, _seg0(big) + COL_AQ + p)), whole(COL_AK), whole(COL_AV), bspec, tile],
        out_specs=[tile, pl.BlockSpec((T, LANE), lambda p, i: (0, p)), pl.BlockSpec((T, LANE), lambda p, i: (0, p)), bspec],
        out_shape=[_sds((T, BRANCH_W), BF)] * 3 + [_sds((H_ATT, ATT_TQ, ATT_SPAN), F32)],
        scratch_shapes=[pltpu.VMEM((T + ATT_PAD, LANE), BF), pltpu.VMEM((T + ATT_PAD, LANE), BF),
                        pltpu.VMEM((T + ATT_PAD, LANE), F32), pltpu.VMEM((T + ATT_PAD, LANE), F32)],
        sem=(ARB, ARB))


def _merge_fwd(x1, big, ys, wb, wo, name):
    T, D = x1.shape
    tm = min(TM, T)

    def body(x_ref, gp_ref, yc_ref, yr_ref, ya_ref, wb_ref, wo_ref, x2_ref, p_ref, mg_ref):
        merged = jnp.zeros((tm, D), F32)
        for i, y_ref in enumerate((yc_ref, yr_ref, ya_ref)):
            cols = slice(i * D, (i + 1) * D)
            pb = jnp.dot(y_ref[...], wb_ref[i], preferred_element_type=F32).astype(BF)
            p_ref[:, cols] = pb
            merged = merged + _sigmoid(gp_ref[:, cols].astype(F32)) * pb.astype(F32)
        mb = merged.astype(BF)
        mg_ref[...] = mb
        x2_ref[...] = x_ref[...] + jnp.dot(mb, wo_ref[...], preferred_element_type=F32)

    tok = pl.BlockSpec((tm, D), lambda i: (i, 0))
    wide = pl.BlockSpec((tm, 3 * D), lambda i: (i, 0))
    yspec = pl.BlockSpec((tm, BRANCH_W), lambda i: (i, 0))
    return pl.pallas_call(
        body, name=name, grid=(T // tm,),
        in_specs=[tok, wide, yspec, yspec, yspec,
                  pl.BlockSpec((3, BRANCH_W, D), lambda i: (0, 0, 0)),
                  pl.BlockSpec((D, D), lambda i: (0, 0))],
        out_specs=[tok, wide, tok],
        out_shape=[_sds((T, D), F32), _sds((T, 3 * D), BF), _sds((T, D), BF)],
        compiler_params=_cp((PAR,)),
    )(x1, big, *ys, wb, wo)


def _merge_bwd(dx2, big, p, wb, wo, name):
    T, D = dx2.shape
    tm = min(TM, T)

    def body(dx_ref, gp_ref, p_ref, wb_ref, wo_ref, dp_ref, dgp_ref, dyc_ref, dyr_ref, dya_ref, dxb_ref):
        dxb = dx_ref[...].astype(BF)
        dxb_ref[...] = dxb
        dm = lax.dot_general(dxb, wo_ref[...], NT_DIMS, preferred_element_type=F32)
        for i, dy_ref in enumerate((dyc_ref, dyr_ref, dya_ref)):
            cols = slice(i * D, (i + 1) * D)
            gt = _sigmoid(gp_ref[:, cols].astype(F32))
            dpb = (dm * gt).astype(BF)
            dp_ref[:, cols] = dpb
            dgp_ref[:, cols] = (dm * p_ref[:, cols].astype(F32) * gt * (1.0 - gt)).astype(BF)
            dy_ref[...] = lax.dot_general(dpb, wb_ref[i], NT_DIMS, preferred_element_type=F32).astype(BF)

    tok = pl.BlockSpec((tm, D), lambda i: (i, 0))
    wide = pl.BlockSpec((tm, 3 * D), lambda i: (i, 0))
    yspec = pl.BlockSpec((tm, BRANCH_W), lambda i: (i, 0))
    return pl.pallas_call(
        body, name=name, grid=(T // tm,),
        in_specs=[tok, wide, wide,
                  pl.BlockSpec((3, BRANCH_W, D), lambda i: (0, 0, 0)),
                  pl.BlockSpec((D, D), lambda i: (0, 0))],
        out_specs=[wide, wide, yspec, yspec, yspec, tok],
        out_shape=[_sds((T, 3 * D), BF), _sds((T, 3 * D), BF)] + [_sds((T, BRANCH_W), BF)] * 3 + [_sds((T, D), BF)],
        compiler_params=_cp((PAR,)),
    )(dx2, big, p, wb, wo)


def _loss_head(x, tgt, fw, name):
    T, D = x.shape
    tm = min(TM, T)

    def body(x_ref, t_ref, w_ref, loss_ref, dx_ref, dw_ref):
        @pl.when(pl.program_id(0) == 0)
        def _():
            loss_ref[...] = jnp.zeros_like(loss_ref)
            dw_ref[...] = jnp.zeros_like(dw_ref)

        xv = x_ref[...]
        wv = w_ref[...]
        e = xv * _rms_r(xv) * wv - t_ref[...]
        loss_ref[...] += 0.5 * jnp.sum(jnp.mean(e * e, axis=-1, keepdims=True))
        dx, dn = _rms_bwd(e * (1.0 / D), xv, wv)
        dx_ref[...] = dx
        dw_ref[...] += dn

    tok = pl.BlockSpec((tm, D), lambda i: (i, 0))
    return pl.pallas_call(
        body, name=name, grid=(T // tm,),
        in_specs=[tok, tok, pl.BlockSpec((1, D), lambda i: (0, 0))],
        out_specs=[pl.BlockSpec((8, LANE), lambda i: (0, 0)), tok, pl.BlockSpec((1, D), lambda i: (0, 0))],
        out_shape=[_sds((8, LANE), F32), _sds((T, D), F32), _sds((1, D), F32)],
        compiler_params=_cp((ARB,)),
    )(x, tgt, fw)


def _block_rows(rows, cols):
    cap = max(8, (1 << 18) // cols)
    best = None
    for r in range(8, rows + 1, 8):
        if rows % r == 0 and r <= cap:
            best = r
    return best if best is not None else rows


def _sum4(land, l, n_layers, name, prev=None):
    _, rows, cols = land.shape
    br = _block_rows(rows, cols)

    def body(*refs):
        l_ref, o_ref = refs[0], refs[-1]
        o_ref[...] = ((l_ref[3].astype(F32) + l_ref[0].astype(F32)) + l_ref[1].astype(F32)) + l_ref[2].astype(F32)

    in_specs = [pl.BlockSpec((4, br, cols), lambda i: (0, i, 0))]
    args = [land]
    aliases = {}
    if prev is not None:
        in_specs.append(pl.BlockSpec(memory_space=pl.ANY))
        args.append(prev)
        aliases = {1: 0}
    return pl.pallas_call(
        body, name=name, grid=(rows // br,), in_specs=in_specs,
        out_specs=pl.BlockSpec((None, br, cols), lambda i: (l, i, 0)),
        out_shape=_sds((n_layers, rows, cols), F32),
        input_output_aliases=aliases, compiler_params=_cp((PAR,)),
    )(*args)


def _adamw_math(w, g, m, v):
    m = ADAM_B1 * m + (1.0 - ADAM_B1) * g
    v = ADAM_B2 * v + (1.0 - ADAM_B2) * (g * g)
    m_hat = m / (1.0 - ADAM_B1 ** ADAM_STEP)
    v_hat = v / (1.0 - ADAM_B2 ** ADAM_STEP)
    delta = -ADAM_LR * (m_hat / (jnp.sqrt(v_hat) + ADAM_EPS) + ADAM_WD * w)
    return delta, m, v


def _adamw(w, ga, gb, m, v, name):
    rows, cols = w.shape
    br = _block_rows(rows, cols)
    two = gb is not None

    def body(*refs):
        if two:
            w_ref, ga_ref, gb_ref, m_ref, v_ref, g_ref, d_ref, nm_ref, nv_ref = refs
            g = ga_ref[...] + gb_ref[...]
        else:
            w_ref, ga_ref, m_ref, v_ref, g_ref, d_ref, nm_ref, nv_ref = refs
            g = ga_ref[...]
        d, nm, nv = _adamw_math(w_ref[...], g, m_ref[...], v_ref[...])
        g_ref[...] = g
        d_ref[...] = d
        nm_ref[...] = nm
        nv_ref[...] = nv

    blk = pl.BlockSpec((br, cols), lambda i: (i, 0))
    args = [w, ga] + ([gb] if two else []) + [m, v]
    return pl.pallas_call(
        body, name=name, grid=(rows // br,),
        in_specs=[blk] * len(args), out_specs=[blk] * 4,
        out_shape=[_sds((rows, cols), F32)] * 4,
        compiler_params=_cp((PAR,)),
    )(*args)


def _swap_cores(vs, name):
    n = len(vs)

    def body(*refs):
        v_refs, o_refs = refs[:n], refs[n:2 * n]
        ssem, rsem = refs[2 * n:]
        x, y, c = _place()
        copies = [pltpu.make_async_remote_copy(
            src_ref=v_refs[k], dst_ref=o_refs[k], send_sem=ssem.at[k], recv_sem=rsem.at[k],
            device_id=(x, y, 1 - c), device_id_type=MESH) for k in range(n)]
        for cp in copies:
            cp.start()
        for cp in copies:
            cp.wait()

    hbm = pl.BlockSpec(memory_space=pl.ANY)
    return pl.pallas_call(
        body, name=name,
        in_specs=[hbm] * n, out_specs=[hbm] * n,
        out_shape=[_sds(v.shape, v.dtype) for v in vs],
        scratch_shapes=[pltpu.SemaphoreType.DMA((n,)), pltpu.SemaphoreType.DMA((n,))],
    )(*vs)


def _allreduce_small(v, name):
    rows = v.shape[0]
    flips = [(fx, fy, fc) for fx in (0, 1) for fy in (0, 1) for fc in (0, 1) if fx or fy or fc]

    def body(v_ref, o_ref, all_s, ssem, rsem):
        x, y, c = _place()

        def peer(f):
            return (x + f[0] - 2 * x * f[0], y + f[1] - 2 * y * f[1], c + f[2] - 2 * c * f[2])

        def slot(p):
            return all_s.at[4 * p[0] + 2 * p[1] + p[2]]

        def copy(k, f, owner):
            return pltpu.make_async_remote_copy(
                src_ref=v_ref, dst_ref=slot(owner), send_sem=ssem.at[k], recv_sem=rsem.at[k],
                device_id=peer(f), device_id_type=MESH)

        sends = [copy(k, f, (x, y, c)) for k, f in enumerate(flips)]
        for cp in sends:
            cp.start()
        all_s[4 * x + 2 * y + c] = v_ref[...]
        for k, f in enumerate(flips):
            copy(k, f, peer(f)).wait_recv()
        for cp in sends:
            cp.wait_send()
        acc = all_s[0]
        for d in range(1, 8):
            acc = acc + all_s[d]
        o_ref[...] = acc

    return pl.pallas_call(
        body, name=name,
        in_specs=[pl.BlockSpec(memory_space=pltpu.VMEM)],
        out_specs=pl.BlockSpec(memory_space=pltpu.VMEM),
        out_shape=_sds((rows, LANE), F32),
        scratch_shapes=[pltpu.VMEM((8, rows, LANE), F32), pltpu.SemaphoreType.DMA((7,)), pltpu.SemaphoreType.DMA((7,))],
    )(v)


BIG_NAMES = ("ffn1_w_gate", "ffn1_w_up", "ffn1_w_down", "w_in", "w_branch", "w_merge_gate", "w_out",
             "ffn2_w_gate", "ffn2_w_up", "ffn2_w_down")


FFN1 = ("ffn1_w_gate", "ffn1_w_up", "ffn1_w_down")
FFN2 = ("ffn2_w_gate", "ffn2_w_up", "ffn2_w_down")
MIX_IN = ("w_in", "w_merge_gate")
MIX_OUT = ("w_branch", "w_out")


def _keys(names, l):
    return [(n, l) for n in names]


def _local_step(x, tgt, small, convw_full, wx, n_layers):
    T, D = x.shape
    L = n_layers
    ns = N_SHARD
    dq = D // ns
    W = wx.w

    def hosted(call, keys, scatter=False):
        comm = wx.pieces(keys, scatter)
        main, extra = call(comm)
        if comm is not None:
            wx.arrived(keys, extra, scatter)
        return main

    def mixer_views(l):
        g4 = W[("w_merge_gate", l)]
        gates = jnp.transpose(g4, (0, 2, 1, 3)).reshape(D, 3 * D)
        win = jnp.transpose(W[("w_in", l)], (1, 0, 2)).reshape(D, -1)
        return jnp.concatenate([gates, win], axis=-1)

    def out_views(l):
        wb = jnp.transpose(W[("w_branch", l)], (1, 2, 0, 3)).reshape(3, BRANCH_W, D)
        wo = W[("w_out", l)].reshape(D, D)
        return wb, wo

    tb = _ret_tables(T)
    rb_pad = jnp.pad(small["rel_bias"], ((0, 0), (0, 0), (0, RB_PAD - N_REL)))

    saved = []
    h = x
    for l in range(L):
        s = {"x0": h}
        nxt = l + 1
        x1, s["g1"], s["u1"] = hosted(
            lambda c: _ffn_fwd(h, small["ffn1_norm"][l][None], W[("ffn1_w_gate", l)], W[("ffn1_w_up", l)],
                               W[("ffn1_w_down", l)], f"ffn1_fwd_{l}", comm=c), _keys(MIX_IN, l))
        s["x1"] = x1
        s["wbig"] = mixer_views(l)
        big, s["h"] = _inproj_fwd(x1, small["mix_norm"][l][None], s["wbig"], f"inproj_fwd_{l}")
        s["big"] = big
        s["bias"] = jnp.transpose(_relbias_expand(rb_pad[l], f"relbias_expand_{l}"), (1, 0, 2))
        s["yc"] = _conv_fwd(big, convw_full[l], f"conv_fwd_{l}")
        s["yr"], s["o"], s["st"] = hosted(lambda c: _ret_fwd(big, tb, f"ret_fwd_{l}", comm=c), _keys(MIX_OUT, l))
        (s["ya"],) = hosted(lambda c: _att_fwd(big, s["bias"], f"att_fwd_{l}", comm=c), _keys(FFN2, l))
        s["wb"], s["wo"] = out_views(l)
        x2, s["p"], s["mg"] = _merge_fwd(x1, big, (s["yc"], s["yr"], s["ya"]), s["wb"], s["wo"], f"merge_fwd_{l}")
        s["x2"] = x2
        h, s["g2"], s["u2"] = hosted(
            lambda c: _ffn_fwd(x2, small["ffn2_norm"][l][None], W[("ffn2_w_gate", l)], W[("ffn2_w_up", l)],
                               W[("ffn2_w_down", l)], f"ffn2_fwd_{l}", comm=c), _keys(FFN1, nxt) if nxt < L else [])
        saved.append(s)

    loss_p, dx, d_final = _loss_head(h, tgt, small["final_norm"][None], "loss_head")

    gs = {"final_norm": d_final[0]}
    for k in ("ffn1_norm", "mix_norm", "ffn2_norm", "rel_bias", "conv_w"):
        gs[k] = [None] * L
    tk = min(2048, T)
    nk = T // tk

    def ffn_grads(pre, l, hb, dgv, duv, av, dacc, chain=False):
        fs = dgv.shape[-1]
        hspec = pl.BlockSpec((tk, D), lambda p, q, k: (k, 0))
        sspec = pl.BlockSpec((None, tk, fs), lambda p, q, k: (p, k, 0))
        up_spec = pl.BlockSpec((None, D, fs), lambda p, q, k: (p, 0, 0))
        down_spec = pl.BlockSpec((None, fs, D), lambda p, q, k: (p, 0, 0))
        jobs = [(pre + "_w_gate", hb, dgv, hspec, sspec, (ns, D, fs), up_spec),
                (pre + "_w_up", hb, duv, hspec, sspec, (ns, D, fs), up_spec),
                (pre + "_w_down", av, dacc, sspec, hspec, (ns, fs, D), down_spec)]
        before = None
        for nm, a, b, a_spec, b_spec, shape, o_spec in jobs:
            def product(c):
                r = _tn(a, b, a_spec, b_spec, _sds(shape, BF), o_spec, (ns, 1, nk), f"d{nm}_{l}", comm=c)
                return (r, []) if c is None else r
            wx.g[(nm, l)] = hosted(product, [before] if chain and before else [], scatter=True)
            before = (nm, l)

    for l in reversed(range(L)):
        s = saved[l]
        dx, dgv, duv, av, hb, dacc, dn = hosted(
            lambda c: _ffn_bwd(dx, s["x2"], small["ffn2_norm"][l][None], s["g2"], s["u2"], W[("ffn2_w_gate", l)],
                               W[("ffn2_w_up", l)], W[("ffn2_w_down", l)], f"ffn2_bwd_{l}", comm=c),
            _keys(FFN1, l + 1) if l + 1 < L else [], scatter=True)
        gs["ffn2_norm"][l] = dn[0]
        ffn_grads("ffn2", l, hb, dgv, duv, av, dacc)
        dp, dgp, dyc, dyr, dya, dxb = _merge_bwd(dx, s["big"], s["p"], s["wb"], s["wo"], f"merge_bwd_{l}")
        wx.g[("w_out", l)] = _tn(
            s["mg"], dxb, pl.BlockSpec((tk, dq), lambda p, q, k: (k, p)), pl.BlockSpec((tk, D), lambda p, q, k: (k, 0)),
            _sds((ns, dq, D), BF), pl.BlockSpec((None, dq, D), lambda p, q, k: (p, 0, 0)), (ns, 1, nk), f"dw_out_{l}")
        gb = None
        for i, yv in enumerate((s["yc"], s["yr"], s["ya"])):
            gb = _tn(yv, dp,
                     pl.BlockSpec((tk, BRANCH_W), lambda p, q, k: (k, 0)),
                     pl.BlockSpec((tk, dq), lambda p, q, k, i=i: (k, i * ns + p)),
                     _sds((ns, 3, BRANCH_W, dq), BF),
                     pl.BlockSpec((None, None, BRANCH_W, dq), lambda p, q, k, i=i: (p, i, 0, 0)),
                     (ns, 1, nk), f"dw_branch{i}_{l}", prev=gb)
        wx.g[("w_branch", l)] = gb
        dcu, dcb, dcc, dcw = _conv_bwd(s["big"], dyc, convw_full[l], f"conv_bwd_{l}")
        gs["conv_w"][l] = dcw
        drq, drk, drv, drg = _ret_bwd(s["big"], s["o"], s["st"], dyr, tb, f"ret_bwd_{l}")
        daq, dak, dav, dbias = hosted(lambda c: _att_bwd(s["big"], s["bias"], dya, f"att_bwd_{l}", comm=c),
                                      _keys(FFN2, l), scatter=True)
        gs["rel_bias"][l] = _relbias_grad(jnp.transpose(dbias, (1, 0, 2)), f"relbias_grad_{l}")[:, :N_REL]
        dbig = jnp.concatenate([dgp, dcu, dcb, dcc, drq, drk, drv, drg, daq, dak, dav], axis=1)
        ics = W[("w_in", l)].shape[-1]
        nq = ics // 256
        wx.g[("w_in", l)] = _tn(
            s["h"], dbig, pl.BlockSpec((tk, D), lambda p, q, k: (k, 0)),
            pl.BlockSpec((tk, 256), lambda p, q, k: (k, 3 * D // 256 + p * nq + q)),
            _sds((ns, D, ics), BF), pl.BlockSpec((None, D, 256), lambda p, q, k: (p, 0, q)),
            (ns, nq, nk), f"dw_in_{l}")
        wx.g[("w_merge_gate", l)] = _tn(
            s["h"], dbig, pl.BlockSpec((tk, dq), lambda p, q, k: (k, p)), pl.BlockSpec((tk, D), lambda p, q, k: (k, q)),
            _sds((ns, 3, dq, D), BF), pl.BlockSpec((None, None, dq, D), lambda p, q, k: (p, q, 0, 0)),
            (ns, 3, nk), f"dw_merge_gate_{l}")
        dx, dn = _inproj_bwd(dbig, s["wbig"], s["x1"], small["mix_norm"][l][None], dx, f"inproj_bwd_{l}")
        gs["mix_norm"][l] = dn[0]
        dx, dgv, duv, av, hb, dacc, dn = hosted(
            lambda c: _ffn_bwd(dx, s["x0"], small["ffn1_norm"][l][None], s["g1"], s["u1"], W[("ffn1_w_gate", l)],
                               W[("ffn1_w_up", l)], W[("ffn1_w_down", l)], f"ffn1_bwd_{l}", comm=c),
            _keys(MIX_IN + MIX_OUT, l), scatter=True)
        gs["ffn1_norm"][l] = dn[0]
        ffn_grads("ffn1", l, hb, dgv, duv, av, dacc, chain=(l == 0))

    for k in ("ffn1_norm", "mix_norm", "ffn2_norm", "rel_bias", "conv_w"):
        gs[k] = jnp.stack(gs[k])
    return loss_p, dx, gs


class _Exchange:
    def __init__(self, shards):
        self.shards = shards
        self.w = {}
        self.g = {}
        self.landed = {}

    def own(self, key):
        return self.shards[key[0]][key[1]].astype(BF)

    def pieces(self, keys, scatter):
        if not keys:
            return None
        return _Pieces([self.g[k] for k in keys] if scatter else [self.own(k) for k in keys], scatter)

    def arrived(self, keys, outs, scatter):
        for k, o in zip(keys, outs):
            (self.landed if scatter else self.w)[k] = o


W_NAMES = ("ffn1_norm", "ffn1_w_gate", "ffn1_w_up", "ffn1_w_down", "mix_norm", "w_in", "conv_w", "rel_bias", "w_branch",
           "w_merge_gate", "w_out", "ffn2_norm", "ffn2_w_gate", "ffn2_w_up", "ffn2_w_down", "final_norm")


def _as2d(a):
    return a.reshape(1, -1) if a.ndim == 1 else a.reshape(-1, a.shape[-1])


def kernel(x, ffn1_norm, ffn1_w_gate, ffn1_w_up, ffn1_w_down, mix_norm, w_in, conv_w, rel_bias, w_branch, w_merge_gate, w_out, ffn2_norm, ffn2_w_gate, ffn2_w_up, ffn2_w_down, final_norm, loss_target, m_ffn1_norm, m_ffn1_w_gate, m_ffn1_w_up, m_ffn1_w_down, m_mix_norm, m_w_in, m_conv_w, m_rel_bias, m_w_branch, m_w_merge_gate, m_w_out, m_ffn2_norm, m_ffn2_w_gate, m_ffn2_w_up, m_ffn2_w_down, m_final_norm, v_ffn1_norm, v_ffn1_w_gate, v_ffn1_w_up, v_ffn1_w_down, v_mix_norm, v_w_in, v_conv_w, v_rel_bias, v_w_branch, v_w_merge_gate, v_w_out, v_ffn2_norm, v_ffn2_w_gate, v_ffn2_w_up, v_ffn2_w_down, v_final_norm):
    given = dict(locals())
    w = {n: given[n] for n in W_NAMES}
    m = {n: given["m_" + n] for n in W_NAMES}
    v = {n: given["v_" + n] for n in W_NAMES}
    my_chip = 2 * lax.axis_index("x") + lax.axis_index("y")
    L = w_in.shape[0]

    wx = _Exchange({n: w[n] for n in BIG_NAMES})
    first = _keys(FFN1, 0)
    comm = _Pieces([wx.own(k) for k in first] + [conv_w], scatter=False)
    got = _comm_alone(comm, "gather_first")
    wx.arrived(first, got[:-1], False)
    convw_full = jnp.transpose(got[-1], (1, 2, 0, 3)).reshape(conv_w.shape[0], conv_w.shape[1], -1)

    small = {n: w[n] for n in ("ffn1_norm", "mix_norm", "ffn2_norm", "final_norm", "rel_bias")}
    loss_p, grad_x, gs = _local_step(x[0], loss_target[0], small, convw_full, wx, L)
    last = [(FFN1[-1], 0)]
    wx.arrived(last, _comm_alone(wx.pieces(last, True), "scatter_last"), True)

    sums = []
    for n in BIG_NAMES:
        acc = None
        for l in range(L):
            a = wx.landed[(n, l)]
            acc = _sum4(a.reshape(4, -1, a.shape[-1]), l, L, f"sum4_{n}_{l}", prev=acc)
        sums.append(acc.reshape(-1, acc.shape[-1]))
    others = _swap_cores(sums, "swap_cores")

    parts = [gs["ffn1_norm"].reshape(-1), gs["mix_norm"].reshape(-1), gs["ffn2_norm"].reshape(-1),
             gs["final_norm"].reshape(-1), gs["rel_bias"].reshape(-1), gs["conv_w"].reshape(-1), loss_p[0]]
    sizes = [p.shape[0] for p in parts]
    flat = jnp.concatenate(parts)
    rows = -(-flat.shape[0] // (8 * LANE)) * 8
    flat = jnp.pad(flat, (0, rows * LANE - flat.shape[0])).reshape(rows, LANE)
    red = _allreduce_small(flat, "allreduce_small").reshape(-1)
    offs = [0]
    for sz in sizes:
        offs.append(offs[-1] + sz)
    sm = {}
    for i, n in enumerate(("ffn1_norm", "mix_norm", "ffn2_norm", "final_norm", "rel_bias", "conv_w")):
        sm[n] = red[offs[i]:offs[i + 1]]
    loss = red[offs[6]]
    sm["conv_w"] = lax.dynamic_slice_in_dim(sm["conv_w"].reshape(conv_w.shape[0], conv_w.shape[1], -1),
                                            my_chip * conv_w.shape[2], conv_w.shape[2], axis=2)

    grads, deltas, new_m, new_v = {}, {}, {}, {}
    big_sum = dict(zip(BIG_NAMES, zip(sums, others)))
    for n in W_NAMES:
        shape = w[n].shape
        if n in big_sum:
            ga, gb = big_sum[n]
        else:
            ga, gb = _as2d(sm[n].reshape(shape)), None
        out = _adamw(_as2d(w[n]), ga, gb, _as2d(m[n]), _as2d(v[n]), f"adamw_{n}")
        grads[n], deltas[n], new_m[n], new_v[n] = (o.reshape(shape) for o in out)

    return (loss, grad_x[None], *[grads[n] for n in W_NAMES], *[deltas[n] for n in W_NAMES],
            *[new_m[n] for n in W_NAMES], *[new_v[n] for n in W_NAMES])
```

```python
import functools
import math

import jax
import jax.numpy as jnp
from jax import lax
from jax.experimental import pallas as pl
from jax.experimental.pallas import tpu as pltpu

F32 = jnp.float32
BF = jnp.bfloat16
MESH = pl.DeviceIdType.MESH
ARB = "arbitrary"
PAR = "parallel"

EPS = 1e-6
NEG_INF = -1e30
ROPE_BASE = 10000.0
CHUNK = 64
BRANCH_W = 512
H_RET = 4
DK_RET = 128
H_ATT = 8
DH_ATT = 64
N_PREV = 8
REL_CLIP = 128
N_REL = 2 * REL_CLIP + 1
N_SHARD = 4
LANE = 128
RET_L = 512
ATT_TQ = 128
ATT_SUB = 4
ATT_PAD = N_PREV * CHUNK
ATT_SPAN = ATT_TQ + ATT_PAD
ATT_TOEP = 2 * REL_CLIP
RB_PAD = 264
TM = 512

ADAM_LR = 0.001
ADAM_B1 = 0.9
ADAM_B2 = 0.999
ADAM_EPS = 1e-08
ADAM_WD = 0.01
ADAM_STEP = 10

NT_DIMS = (((1,), (1,)), ((), ()))
TN_DIMS = (((0,), (0,)), ((), ()))


def _cp(sem, vmem_mb=48):
    return pltpu.CompilerParams(dimension_semantics=sem, vmem_limit_bytes=vmem_mb << 20)


def _sds(shape, dtype):
    return jax.ShapeDtypeStruct(tuple(shape), dtype)


def _rms_r(x):
    return lax.rsqrt(jnp.mean(x * x, axis=-1, keepdims=True) + EPS)


def _sigmoid(x):
    return jax.nn.sigmoid(x)


def _rms_bwd(dh, xv, nw):
    r = _rms_r(xv)
    xh = xv * r
    dxh = dh * nw
    dx = r * (dxh - xh * jnp.mean(dxh * xh, axis=-1, keepdims=True))
    return dx, jnp.sum(dh * xh, axis=0, keepdims=True)


def _place():
    return lax.axis_index("x"), lax.axis_index("y"), lax.axis_index("c")


def _other_chips(x, y):
    return [(1 - x, y), (x, 1 - y), (1 - x, 1 - y)]


class _Pieces:
    def __init__(self, srcs, scatter):
        self.srcs = list(srcs)
        self.scatter = scatter
        n = len(self.srcs)
        self.out_shape = [_sds(s.shape if scatter else (N_SHARD,) + s.shape, s.dtype) for s in self.srcs]
        self.scratch = [pltpu.SemaphoreType.DMA((n,)), pltpu.SemaphoreType.DMA((3, n)), pltpu.SemaphoreType.DMA((3, n))]

    def _copies(self, src, dst, sems, waiting):
        lsem, ssem, rsem = sems
        x, y, c = _place()
        mine = 2 * x + y
        n = len(src)

        def remote(j, k, chip, s_ref, d_ref):
            return pltpu.make_async_remote_copy(
                src_ref=s_ref, dst_ref=d_ref, send_sem=ssem.at[j, k], recv_sem=rsem.at[j, k],
                device_id=(chip[0], chip[1], c), device_id_type=MESH)

        chips = list(enumerate(_other_chips(x, y)))
        if self.scatter:
            local = [pltpu.make_async_copy(src[k].at[mine], dst[k].at[3], lsem.at[k]) for k in range(n)]
            sends = [remote(j, k, ch, src[k].at[2 * ch[0] + ch[1]], dst[k].at[j]) for j, ch in chips for k in range(n)]
            recvs = sends
        else:
            local = [pltpu.make_async_copy(src[k], dst[k].at[mine], lsem.at[k]) for k in range(n)]
            sends = [remote(j, k, ch, src[k], dst[k].at[mine]) for j, ch in chips for k in range(n)]
            recvs = [remote(j, k, ch, src[k], dst[k].at[2 * ch[0] + ch[1]]) for j, ch in chips for k in range(n)
                     ] if waiting else []
        return local, sends, recvs

    def start(self, src, dst, sems):
        local, sends, _ = self._copies(src, dst, sems, False)
        for cp in local + sends:
            cp.start()

    def wait(self, src, dst, sems):
        local, sends, recvs = self._copies(src, dst, sems, True)
        for cp in recvs:
            cp.wait_recv()
        for cp in sends:
            cp.wait_send()
        for cp in local:
            cp.wait()


def _call(body, *, name, args, in_specs, out_specs, out_shape, grid=(), scratch_shapes=(), sem=None, comm=None,
          aliases=None, vmem_mb=48):
    in_specs, out_specs, out_shape = list(in_specs), list(out_specs), list(out_shape)
    scratch, args = list(scratch_shapes), list(args)
    n_in, n_out, n_scr = len(in_specs), len(out_specs), len(scratch)
    if comm is None:
        def kernel_body(*refs):
            body(*refs)
    else:
        c_in, c_out = len(comm.srcs), len(comm.out_shape)

        def kernel_body(*refs):
            o0 = n_in + c_in
            s0 = o0 + n_out + c_out
            cin, cout, sems = refs[n_in:o0], refs[o0 + n_out:s0], refs[s0 + n_scr:]
            main = refs[:n_in] + refs[o0:o0 + n_out] + refs[s0:s0 + n_scr]
            if grid:
                ids = [pl.program_id(a) for a in range(len(grid))]
                first = functools.reduce(lambda p, q: p & q, [i == 0 for i in ids])
                last = functools.reduce(lambda p, q: p & q, [i == g - 1 for i, g in zip(ids, grid)])

                @pl.when(first)
                def _():
                    comm.start(cin, cout, sems)

                body(*main)

                @pl.when(last)
                def _():
                    comm.wait(cin, cout, sems)
            else:
                comm.start(cin, cout, sems)
                body(*main)
                comm.wait(cin, cout, sems)

        hbm = pl.BlockSpec(memory_space=pl.ANY)
        in_specs += [hbm] * c_in
        out_specs += [hbm] * c_out
        out_shape += comm.out_shape
        scratch += comm.scratch
        args += comm.srcs
    params = dict(vmem_limit_bytes=vmem_mb << 20)
    if grid:
        params["dimension_semantics"] = sem
    outs = pl.pallas_call(
        kernel_body, name=name, grid=grid, in_specs=in_specs, out_specs=out_specs, out_shape=out_shape,
        scratch_shapes=scratch, input_output_aliases=aliases or {}, compiler_params=pltpu.CompilerParams(**params),
    )(*args)
    return list(outs[:n_out]), list(outs[n_out:])


def _comm_alone(comm, name):
    return _call(lambda: None, name=name, args=[], in_specs=[], out_specs=[], out_shape=[], comm=comm)[1]


def _ffn_fwd(x, nw, wg, wu, wd, name, comm=None):
    T, D = x.shape
    ns, _, fs = wg.shape
    tm = min(TM, T)

    def body(x_ref, nw_ref, wg_ref, wu_ref, wd_ref, xo_ref, g_ref, u_ref, h_s, acc_s):
        j = pl.program_id(1)

        @pl.when(j == 0)
        def _():
            xv = x_ref[...]
            h_s[...] = (xv * _rms_r(xv) * nw_ref[...]).astype(BF)
            acc_s[...] = jnp.zeros_like(acc_s)

        h = h_s[...]
        gb = jnp.dot(h, wg_ref[...], preferred_element_type=F32).astype(BF)
        ub = jnp.dot(h, wu_ref[...], preferred_element_type=F32).astype(BF)
        g_ref[...] = gb
        u_ref[...] = ub
        g = gb.astype(F32)
        a = (g * _sigmoid(g) * ub.astype(F32)).astype(BF)
        acc_s[...] += jnp.dot(a, wd_ref[...], preferred_element_type=F32)

        @pl.when(j == ns - 1)
        def _():
            xo_ref[...] = x_ref[...] + 0.5 * acc_s[...]

    wspec = pl.BlockSpec((None, D, fs), lambda i, j: (j, 0, 0))
    return _call(
        body, name=name, grid=(T // tm, ns), args=(x, nw, wg, wu, wd), comm=comm,
        in_specs=[pl.BlockSpec((tm, D), lambda i, j: (i, 0)),
                  pl.BlockSpec((1, D), lambda i, j: (0, 0)),
                  wspec, wspec,
                  pl.BlockSpec((None, fs, D), lambda i, j: (j, 0, 0))],
        out_specs=[pl.BlockSpec((tm, D), lambda i, j: (i, 0)),
                   pl.BlockSpec((None, tm, fs), lambda i, j: (j, i, 0)),
                   pl.BlockSpec((None, tm, fs), lambda i, j: (j, i, 0))],
        out_shape=[_sds((T, D), F32), _sds((ns, T, fs), BF), _sds((ns, T, fs), BF)],
        scratch_shapes=[pltpu.VMEM((tm, D), BF), pltpu.VMEM((tm, D), F32)],
        sem=(ARB, ARB))


def _ffn_bwd(dxo, x, nw, g, u, wg, wu, wd, name, comm=None):
    T, D = x.shape
    ns, _, fs = wg.shape
    tm = min(TM, T)

    def body(dxo_ref, x_ref, nw_ref, g_ref, u_ref, wg_ref, wu_ref, wd_ref,
             dx_ref, dg_ref, du_ref, a_ref, h_ref, dacc_ref, dnw_ref, dacc_s, acc_s):
        i = pl.program_id(0)
        j = pl.program_id(1)

        @pl.when(j == 0)
        def _():
            xv = x_ref[...]
            h_ref[...] = (xv * _rms_r(xv) * nw_ref[...]).astype(BF)
            db = (0.5 * dxo_ref[...]).astype(BF)
            dacc_ref[...] = db
            dacc_s[...] = db
            acc_s[...] = jnp.zeros_like(acc_s)

        @pl.when((i == 0) & (j == 0))
        def _():
            dnw_ref[...] = jnp.zeros_like(dnw_ref)

        da = lax.dot_general(dacc_s[...], wd_ref[...], NT_DIMS, preferred_element_type=F32)
        gv = g_ref[...].astype(F32)
        uv = u_ref[...].astype(F32)
        s = _sigmoid(gv)
        sg = gv * s
        a_ref[...] = (sg * uv).astype(BF)
        dub = (da * sg).astype(BF)
        dgb = (da * uv * (s * (1.0 + gv * (1.0 - s)))).astype(BF)
        dg_ref[...] = dgb
        du_ref[...] = dub
        acc_s[...] += (lax.dot_general(dgb, wg_ref[...], NT_DIMS, preferred_element_type=F32)
                       + lax.dot_general(dub, wu_ref[...], NT_DIMS, preferred_element_type=F32))

        @pl.when(j == ns - 1)
        def _():
            dx, dn = _rms_bwd(acc_s[...], x_ref[...], nw_ref[...])
            dx_ref[...] = dxo_ref[...] + dx
            dnw_ref[...] += dn

    tok = pl.BlockSpec((tm, D), lambda i, j: (i, 0))
    row = pl.BlockSpec((1, D), lambda i, j: (0, 0))
    hid = pl.BlockSpec((None, tm, fs), lambda i, j: (j, i, 0))
    wspec = pl.BlockSpec((None, D, fs), lambda i, j: (j, 0, 0))
    return _call(
        body, name=name, grid=(T // tm, ns), args=(dxo, x, nw, g, u, wg, wu, wd), comm=comm,
        in_specs=[tok, tok, row, hid, hid, wspec, wspec,
                  pl.BlockSpec((None, fs, D), lambda i, j: (j, 0, 0))],
        out_specs=[tok, hid, hid, hid, tok, tok, row],
        out_shape=[_sds((T, D), F32), _sds((ns, T, fs), BF), _sds((ns, T, fs), BF), _sds((ns, T, fs), BF),
                   _sds((T, D), BF), _sds((T, D), BF), _sds((1, D), F32)],
        scratch_shapes=[pltpu.VMEM((tm, D), BF), pltpu.VMEM((tm, D), F32)],
        sem=(ARB, ARB))


def _tn(a, b, a_spec, b_spec, out_shape, out_spec, grid, name, prev=None, comm=None):
    nk = grid[-1]
    acc_shape = tuple(d for d in out_spec.block_shape if d is not None)

    def body(*refs):
        a_ref, b_ref = refs[0], refs[1]
        o_ref, acc = refs[-2], refs[-1]
        k = pl.program_id(2)
        prod = lax.dot_general(a_ref[...], b_ref[...], TN_DIMS, preferred_element_type=F32)

        @pl.when(k == 0)
        def _():
            acc[...] = prod

        @pl.when(k > 0)
        def _():
            acc[...] += prod

        @pl.when(k == nk - 1)
        def _():
            o_ref[...] = acc[...].astype(o_ref.dtype)

    in_specs = [a_spec, b_spec]
    args = [a, b]
    aliases = {}
    if prev is not None:
        in_specs.append(pl.BlockSpec(memory_space=pl.ANY))
        args.append(prev)
        aliases = {2: 0}
    main, extra = _call(
        body, name=name, grid=grid, args=args, in_specs=in_specs, out_specs=[out_spec], out_shape=[out_shape],
        scratch_shapes=[pltpu.VMEM(acc_shape, F32)], aliases=aliases, sem=(ARB, ARB, ARB), comm=comm)
    return main[0] if comm is None else (main[0], extra)


def _inproj_fwd(x, nw, wbig, name):
    T, D = x.shape
    nb = wbig.shape[-1]
    tm = min(2 * TM, T)
    bn = min(2048, nb)

    def body(x_ref, nw_ref, w_ref, o_ref, h_ref, h_s):
        @pl.when(pl.program_id(1) == 0)
        def _():
            xv = x_ref[...]
            hb = (xv * _rms_r(xv) * nw_ref[...]).astype(BF)
            h_s[...] = hb
            h_ref[...] = hb

        o_ref[...] = jnp.dot(h_s[...], w_ref[...], preferred_element_type=F32).astype(BF)

    return pl.pallas_call(
        body, name=name, grid=(T // tm, nb // bn),
        in_specs=[pl.BlockSpec((tm, D), lambda i, n: (i, 0)),
                  pl.BlockSpec((1, D), lambda i, n: (0, 0)),
                  pl.BlockSpec((D, bn), lambda i, n: (0, n))],
        out_specs=[pl.BlockSpec((tm, bn), lambda i, n: (i, n)),
                   pl.BlockSpec((tm, D), lambda i, n: (i, 0))],
        out_shape=[_sds((T, nb), BF), _sds((T, D), BF)],
        scratch_shapes=[pltpu.VMEM((tm, D), BF)],
        compiler_params=_cp((PAR, ARB)),
    )(x, nw, wbig)


def _inproj_bwd(dbig, wbig, x, nw, dxin, name):
    T, D = x.shape
    nb = wbig.shape[-1]
    tm = min(TM, T)
    tk = min(2048, nb)
    nk = nb // tk

    def body(a_ref, w_ref, x_ref, nw_ref, dxin_ref, dx_ref, dnw_ref, acc_s):
        i = pl.program_id(0)
        k = pl.program_id(1)
        prod = lax.dot_general(a_ref[...], w_ref[...], NT_DIMS, preferred_element_type=F32)

        @pl.when((i == 0) & (k == 0))
        def _():
            dnw_ref[...] = jnp.zeros_like(dnw_ref)

        @pl.when(k == 0)
        def _():
            acc_s[...] = prod

        @pl.when(k > 0)
        def _():
            acc_s[...] += prod

        @pl.when(k == nk - 1)
        def _():
            dx, dn = _rms_bwd(acc_s[...], x_ref[...], nw_ref[...])
            dx_ref[...] = dxin_ref[...] + dx
            dnw_ref[...] += dn

    tok = pl.BlockSpec((tm, D), lambda i, k: (i, 0))
    row = pl.BlockSpec((1, D), lambda i, k: (0, 0))
    return pl.pallas_call(
        body, name=name, grid=(T // tm, nk),
        in_specs=[pl.BlockSpec((tm, tk), lambda i, k: (i, k)),
                  pl.BlockSpec((D, tk), lambda i, k: (0, k)),
                  tok, row, tok],
        out_specs=[tok, row],
        out_shape=[_sds((T, D), F32), _sds((1, D), F32)],
        scratch_shapes=[pltpu.VMEM((tm, D), F32)],
        compiler_params=_cp((ARB, ARB)),
    )(dbig, wbig, x, nw, dxin)


CONV_R = 512
COL_CU, COL_CB, COL_CC = 0, 4, 8
COL_RQ, COL_RK, COL_RV, COL_RG = 12, 16, 20, 24
COL_AQ, COL_AK, COL_AV = 28, 32, 36


def _seg0(big):
    return (big.shape[1] - 10 * BRANCH_W) // LANE


def _conv_fwd(big, cw, name):
    T = big.shape[0]
    R = min(CONV_R, T)

    def body(cu_ref, cb_ref, cc_ref, w_ref, y_ref, z_s):
        z_s[pl.ds(0, 8), :] = jnp.zeros((8, LANE), F32)

        def fill(t, c):
            sl = pl.ds(pl.multiple_of(t * R, R), R)
            z_s[pl.ds(pl.multiple_of(t * R + 8, 8), R), :] = cc_ref[sl, :].astype(F32) * cu_ref[sl, :].astype(F32)
            return c

        lax.fori_loop(0, T // R, fill, 0)
        w0, w1, w2 = w_ref[0:1, :], w_ref[1:2, :], w_ref[2:3, :]

        def step(t, c):
            zz = z_s[pl.ds(pl.multiple_of(t * R, R), R + 8), :]
            z0 = zz[8:]
            z1 = pltpu.roll(zz, 1, 0)[8:]
            z2 = pltpu.roll(zz, 2, 0)[8:]
            sl = pl.ds(pl.multiple_of(t * R, R), R)
            y_ref[sl, :] = (cb_ref[sl, :].astype(F32) * (w2 * z0 + w1 * z1 + w0 * z2)).astype(BF)
            return c

        lax.fori_loop(0, T // R, step, 0)

    def col(base):
        return pl.BlockSpec((T, LANE), lambda j: (0, _seg0(big) + base + j))

    return pl.pallas_call(
        body, name=name, grid=(BRANCH_W // LANE,),
        in_specs=[col(COL_CU), col(COL_CB), col(COL_CC), pl.BlockSpec((3, LANE), lambda j: (0, j))],
        out_specs=pl.BlockSpec((T, LANE), lambda j: (0, j)),
        out_shape=_sds((T, BRANCH_W), BF),
        scratch_shapes=[pltpu.VMEM((T + 8, LANE), F32)],
        compiler_params=_cp((PAR,)),
    )(big, big, big, cw)


def _conv_bwd(big, dy, cw, name):
    T = big.shape[0]
    R = min(CONV_R, T)

    def body(cu_ref, cb_ref, cc_ref, dy_ref, w_ref, dcu_ref, dcb_ref, dcc_ref, dw_ref, z_s, d_s):
        z_s[pl.ds(0, 8), :] = jnp.zeros((8, LANE), F32)
        d_s[pl.ds(T, 8), :] = jnp.zeros((8, LANE), F32)

        def fill(t, c):
            sl = pl.ds(pl.multiple_of(t * R, R), R)
            z_s[pl.ds(pl.multiple_of(t * R + 8, 8), R), :] = cc_ref[sl, :].astype(F32) * cu_ref[sl, :].astype(F32)
            d_s[sl, :] = dy_ref[sl, :].astype(F32) * cb_ref[sl, :].astype(F32)
            return c

        lax.fori_loop(0, T // R, fill, 0)
        w0, w1, w2 = w_ref[0:1, :], w_ref[1:2, :], w_ref[2:3, :]

        def step(t, carry):
            a0, a1, a2 = carry
            zz = z_s[pl.ds(pl.multiple_of(t * R, R), R + 8), :]
            z0 = zz[8:]
            z1 = pltpu.roll(zz, 1, 0)[8:]
            z2 = pltpu.roll(zz, 2, 0)[8:]
            sl = pl.ds(pl.multiple_of(t * R, R), R)
            dyv = dy_ref[sl, :].astype(F32)
            dcb_ref[sl, :] = (dyv * (w2 * z0 + w1 * z1 + w0 * z2)).astype(BF)
            dd = d_s[pl.ds(pl.multiple_of(t * R, R), R + 8), :]
            d0 = dd[:R]
            d1 = pltpu.roll(dd, R + 7, 0)[:R]
            d2 = pltpu.roll(dd, R + 6, 0)[:R]
            dz = w2 * d0 + w1 * d1 + w0 * d2
            dcc_ref[sl, :] = (dz * cu_ref[sl, :].astype(F32)).astype(BF)
            dcu_ref[sl, :] = (dz * cc_ref[sl, :].astype(F32)).astype(BF)
            a0 = a0 + jnp.sum(d0 * z2, axis=0, keepdims=True)
            a1 = a1 + jnp.sum(d0 * z1, axis=0, keepdims=True)
            a2 = a2 + jnp.sum(d0 * z0, axis=0, keepdims=True)
            return a0, a1, a2

        zero = jnp.zeros((1, LANE), F32)
        a0, a1, a2 = lax.fori_loop(0, T // R, step, (zero, zero, zero))
        dw_ref[0:1, :] = a0
        dw_ref[1:2, :] = a1
        dw_ref[2:3, :] = a2

    def col(base):
        return pl.BlockSpec((T, LANE), lambda j: (0, _seg0(big) + base + j))

    out = pl.BlockSpec((T, LANE), lambda j: (0, j))
    w = pl.BlockSpec((3, LANE), lambda j: (0, j))
    return pl.pallas_call(
        body, name=name, grid=(BRANCH_W // LANE,),
        in_specs=[col(COL_CU), col(COL_CB), col(COL_CC), out, w],
        out_specs=[out, out, out, w],
        out_shape=[_sds((T, BRANCH_W), BF)] * 3 + [_sds((3, BRANCH_W), F32)],
        scratch_shapes=[pltpu.VMEM((T + 8, LANE), F32), pltpu.VMEM((T + 8, LANE), F32)],
        compiler_params=_cp((PAR,)),
    )(big, big, big, dy, cw)


def _ret_tables(T):
    L = min(RET_L, T)
    hh = jnp.arange(H_RET, dtype=F32)
    lg = jnp.log1p(-jnp.exp2(-5.0 - hh))
    n = jnp.arange(L, dtype=F32)
    a = jnp.exp(lg[:, None] * (n + 1.0))
    b = jnp.exp(lg[:, None] * (L - 1.0 - n))
    gl = jnp.exp(lg * L)
    ch = jnp.arange(L) // CHUNK
    m = jnp.exp(lg[:, None, None] * jnp.abs(n[:, None] - n[None, :])) * (ch[None, :] <= ch[:, None]).astype(F32)
    inv_freq = ROPE_BASE ** (-jnp.linspace(0.0, 1.0, DK_RET // 2, dtype=F32))
    ang = jnp.arange(T, dtype=F32)[:, None] * inv_freq[None, :]
    cos, sin = jnp.cos(ang), jnp.sin(ang)
    return dict(
        L=L, M=m,
        a=jnp.broadcast_to(a[:, :, None], (H_RET, L, DK_RET)),
        b=jnp.broadcast_to(b[:, :, None], (H_RET, L, DK_RET)),
        gl=jnp.broadcast_to(gl[:, None, None], (H_RET, 1, DK_RET)),
        cos=jnp.concatenate([cos, cos], axis=-1), sin=jnp.concatenate([-sin, sin], axis=-1))


def _rot(x, cs, sn):
    return x * cs + pltpu.roll(x, DK_RET // 2, 1) * sn


def _unrot(dy, cs, sn):
    return dy * cs + pltpu.roll(dy * sn, DK_RET // 2, 1)


def _ret_fwd(big, tb, name, comm=None):
    T = big.shape[0]
    L = tb["L"]
    nsc = T // L
    scale = DK_RET ** -0.5

    def body(q_ref, k_ref, v_ref, g_ref, cos_ref, sin_ref, m_ref, a_ref, b_ref, gl_ref,
             y_ref, o_ref, st_ref, s_s):
        @pl.when(pl.program_id(1) == 0)
        def _():
            s_s[...] = jnp.zeros_like(s_s)

        cs, sn = cos_ref[...], sin_ref[...]
        qt = _rot(q_ref[...].astype(F32), cs, sn) * scale
        kt = _rot(k_ref[...].astype(F32), cs, sn)
        qb, kb, vb = qt.astype(BF), kt.astype(BF), v_ref[...]
        s_prev = s_s[...]
        st_ref[...] = s_prev
        p = lax.dot_general(qb, kb, NT_DIMS, preferred_element_type=F32) * m_ref[...]
        o = (jnp.dot(p.astype(BF), vb, preferred_element_type=F32)
             + jnp.dot((qt * a_ref[...]).astype(BF), s_prev.astype(BF), preferred_element_type=F32))
        s_s[...] = s_prev * gl_ref[...] + lax.dot_general((kt * b_ref[...]).astype(BF), vb, TN_DIMS,
                                                         preferred_element_type=F32)
        o_ref[...] = o
        gv = g_ref[...].astype(F32)
        y_ref[...] = (gv * _sigmoid(gv) * o * _rms_r(o)).astype(BF)

    def col(base):
        return pl.BlockSpec((L, LANE), lambda h, i: (i, _seg0(big) + base + h))

    tab = pl.BlockSpec((L, DK_RET), lambda h, i: (i, 0))
    per_head = pl.BlockSpec((None, L, DK_RET), lambda h, i: (h, 0, 0))
    out = pl.BlockSpec((L, LANE), lambda h, i: (i, h))
    return _call(
        body, name=name, grid=(H_RET, nsc), comm=comm,
        args=(big, big, big, big, tb["cos"], tb["sin"], tb["M"], tb["a"], tb["b"], tb["gl"]),
        in_specs=[col(COL_RQ), col(COL_RK), col(COL_RV), col(COL_RG), tab, tab,
                  pl.BlockSpec((None, L, L), lambda h, i: (h, 0, 0)), per_head, per_head,
                  pl.BlockSpec((None, 1, DK_RET), lambda h, i: (h, 0, 0))],
        out_specs=[out, out, pl.BlockSpec((None, None, DK_RET, DK_RET), lambda h, i: (i, h, 0, 0))],
        out_shape=[_sds((T, BRANCH_W), BF), _sds((T, BRANCH_W), F32), _sds((nsc, H_RET, DK_RET, DK_RET), F32)],
        scratch_shapes=[pltpu.VMEM((DK_RET, DK_RET), F32)],
        sem=(ARB, ARB))


def _ret_bwd(big, o, st, dy, tb, name):
    T = big.shape[0]
    L = tb["L"]
    nsc = T // L
    scale = DK_RET ** -0.5

    def body(q_ref, k_ref, v_ref, g_ref, cos_ref, sin_ref, m_ref, a_ref, b_ref, gl_ref, o_ref, st_ref, dy_ref,
             dq_ref, dk_ref, dv_ref, dg_ref, ds_s):
        @pl.when(pl.program_id(1) == 0)
        def _():
            ds_s[...] = jnp.zeros_like(ds_s)

        cs, sn = cos_ref[...], sin_ref[...]
        mm, av, bv = m_ref[...], a_ref[...], b_ref[...]
        qt = _rot(q_ref[...].astype(F32), cs, sn) * scale
        kt = _rot(k_ref[...].astype(F32), cs, sn)
        qb, kb, vb = qt.astype(BF), kt.astype(BF), v_ref[...]
        pb = (lax.dot_general(qb, kb, NT_DIMS, preferred_element_type=F32) * mm).astype(BF)
        ov = o_ref[...]
        r = _rms_r(ov)
        oh = ov * r
        gv = g_ref[...].astype(F32)
        sg = _sigmoid(gv)
        dyv = dy_ref[...].astype(F32)
        dg_ref[...] = (dyv * oh * (sg * (1.0 + gv * (1.0 - sg)))).astype(BF)
        doh = dyv * gv * sg
        dob = (r * (doh - oh * jnp.mean(doh * oh, axis=-1, keepdims=True))).astype(BF)
        dsb = ds_s[...].astype(BF)
        spb = st_ref[...].astype(BF)
        dpb = (lax.dot_general(dob, vb, NT_DIMS, preferred_element_type=F32) * mm).astype(BF)
        dqt = (jnp.dot(dpb, kb, preferred_element_type=F32)
               + lax.dot_general(dob, spb, NT_DIMS, preferred_element_type=F32) * av)
        dkt = (lax.dot_general(dpb, qb, TN_DIMS, preferred_element_type=F32)
               + lax.dot_general(vb, dsb, NT_DIMS, preferred_element_type=F32) * bv)
        dv = (lax.dot_general(pb, dob, TN_DIMS, preferred_element_type=F32)
              + jnp.dot((kt * bv).astype(BF), dsb, preferred_element_type=F32))
        ds_s[...] = ds_s[...] * gl_ref[...] + lax.dot_general((qt * av).astype(BF), dob, TN_DIMS,
                                                              preferred_element_type=F32)
        dq_ref[...] = (_unrot(dqt, cs, sn) * scale).astype(BF)
        dk_ref[...] = _unrot(dkt, cs, sn).astype(BF)
        dv_ref[...] = dv.astype(BF)

    def rev(i):
        return nsc - 1 - i

    def col(base):
        return pl.BlockSpec((L, LANE), lambda h, i: (rev(i), _seg0(big) + base + h))

    tab = pl.BlockSpec((L, DK_RET), lambda h, i: (rev(i), 0))
    per_head = pl.BlockSpec((None, L, DK_RET), lambda h, i: (h, 0, 0))
    out = pl.BlockSpec((L, LANE), lambda h, i: (rev(i), h))
    return pl.pallas_call(
        body, name=name, grid=(H_RET, nsc),
        in_specs=[col(COL_RQ), col(COL_RK), col(COL_RV), col(COL_RG), tab, tab,
                  pl.BlockSpec((None, L, L), lambda h, i: (h, 0, 0)), per_head, per_head,
                  pl.BlockSpec((None, 1, DK_RET), lambda h, i: (h, 0, 0)),
                  out, pl.BlockSpec((None, None, DK_RET, DK_RET), lambda h, i: (rev(i), h, 0, 0)), out],
        out_specs=[out, out, out, out],
        out_shape=[_sds((T, BRANCH_W), BF)] * 4,
        scratch_shapes=[pltpu.VMEM((DK_RET, DK_RET), F32)],
        compiler_params=_cp((PAR, ARB)),
    )(big, big, big, big, tb["cos"], tb["sin"], tb["M"], tb["a"], tb["b"], tb["gl"], o, st, dy)


def _relbias_onehot(n):
    mm = lax.broadcasted_iota(jnp.int32, (RB_PAD, ATT_TOEP), 1)
    rr = lax.broadcasted_iota(jnp.int32, (RB_PAD, ATT_TOEP), 0)
    idx = jnp.clip(n + ATT_TOEP - mm, 0, 2 * REL_CLIP)
    return (rr == idx).astype(F32)


def _relbias_expand(rbp, name):
    far = ATT_SPAN - ATT_TOEP

    def body(rb_ref, o_ref):
        rb = rb_ref[...]
        const = jnp.broadcast_to(rb[:, 2 * REL_CLIP:2 * REL_CLIP + 1], (H_ATT, far))

        def row(n, c):
            toep = jnp.dot(rb, _relbias_onehot(n), preferred_element_type=F32, precision=lax.Precision.HIGHEST)
            m = lax.broadcasted_iota(jnp.int32, (1, ATT_SPAN), 1)
            d = n // CHUNK + N_PREV - m // CHUNK
            neg = jnp.where((d >= 0) & (d <= N_PREV), 0.0, NEG_INF).astype(F32)
            o_ref[n] = jnp.concatenate([const, toep], axis=1) + neg
            return c

        lax.fori_loop(0, ATT_TQ, row, 0)

    return pl.pallas_call(
        body, name=name,
        in_specs=[pl.BlockSpec(memory_space=pltpu.VMEM)],
        out_specs=pl.BlockSpec(memory_space=pltpu.VMEM),
        out_shape=_sds((ATT_TQ, H_ATT, ATT_SPAN), F32),
    )(rbp)


def _relbias_grad(dbt, name):
    far = ATT_SPAN - ATT_TOEP

    def body(d_ref, o_ref):
        def row(n, carry):
            acc, cs = carry
            dn = d_ref[n]
            acc = acc + lax.dot_general(dn[:, far:], _relbias_onehot(n), NT_DIMS, preferred_element_type=F32,
                                        precision=lax.Precision.HIGHEST)
            cs = cs + jnp.sum(dn[:, :far], axis=1, keepdims=True)
            return acc, cs

        acc, cs = lax.fori_loop(0, ATT_TQ, row, (jnp.zeros((H_ATT, RB_PAD), F32), jnp.zeros((H_ATT, 1), F32)))
        rr = lax.broadcasted_iota(jnp.int32, (H_ATT, RB_PAD), 1)
        o_ref[...] = acc + jnp.where(rr == 2 * REL_CLIP, cs, 0.0)

    return pl.pallas_call(
        body, name=name,
        in_specs=[pl.BlockSpec(memory_space=pltpu.VMEM)],
        out_specs=pl.BlockSpec(memory_space=pltpu.VMEM),
        out_shape=_sds((H_ATT, RB_PAD), F32),
    )(dbt)


def _att_pad_fill(dst_s, src_ref, T):
    dst_s[pl.ds(0, ATT_PAD), :] = jnp.zeros((ATT_PAD, LANE), dst_s.dtype)
    R = min(512, T)

    def cp(t, c):
        dst_s[pl.ds(pl.multiple_of(ATT_PAD + t * R, LANE), R), :] = src_ref[pl.ds(pl.multiple_of(t * R, R), R), :]
        return c

    lax.fori_loop(0, T // R, cp, 0)


ATT_WIN = ATT_SUB * ATT_TQ + ATT_PAD


def _att_probs(s_full, sub, bias, t0):
    s = s_full[sub * ATT_TQ:(sub + 1) * ATT_TQ, sub * ATT_TQ:sub * ATT_TQ + ATT_SPAN] * (DH_ATT ** -0.5) + bias
    key_pos = t0 + sub * ATT_TQ - ATT_PAD + lax.broadcasted_iota(jnp.int32, (1, ATT_SPAN), 1)
    s = jnp.where(key_pos >= 0, s, NEG_INF)
    p = jnp.exp(s - jnp.max(s, axis=-1, keepdims=True))
    return p * (1.0 / jnp.sum(p, axis=-1, keepdims=True))


def _att_band(tiles):
    rows = []
    for sub, t in enumerate(tiles):
        parts = []
        if sub:
            parts.append(jnp.zeros((ATT_TQ, sub * ATT_TQ), BF))
        parts.append(t)
        if sub < ATT_SUB - 1:
            parts.append(jnp.zeros((ATT_TQ, (ATT_SUB - 1 - sub) * ATT_TQ), BF))
        rows.append(jnp.concatenate(parts, axis=1))
    return jnp.concatenate(rows, axis=0)


def _att_head_masks(x):
    first = lax.broadcasted_iota(jnp.int32, (1, LANE), 1) < DH_ATT
    zero = jnp.zeros_like(x)
    return first, (jnp.where(first, x, zero), jnp.where(first, zero, x))


def _att_fwd(big, bias, name, comm=None):
    T = big.shape[0]
    rows = ATT_SUB * ATT_TQ
    nt = T // rows

    def body(q_ref, k_ref, v_ref, b_ref, y_ref, kp_s, vp_s):
        i = pl.program_id(1)

        @pl.when(i == 0)
        def _():
            _att_pad_fill(kp_s, k_ref, T)
            _att_pad_fill(vp_s, v_ref, T)

        t0 = pl.multiple_of(i * rows, rows)
        kw = kp_s[pl.ds(t0, ATT_WIN), :]
        vw = vp_s[pl.ds(t0, ATT_WIN), :]
        first, qm = _att_head_masks(q_ref[...])
        outs = []
        for hh in range(2):
            s_full = lax.dot_general(qm[hh], kw, NT_DIMS, preferred_element_type=F32)
            band = _att_band([_att_probs(s_full, sub, b_ref[hh], t0).astype(BF) for sub in range(ATT_SUB)])
            outs.append(jnp.dot(band, vw, preferred_element_type=F32))
        y_ref[...] = jnp.where(first, outs[0], outs[1]).astype(BF)

    def whole(base):
        return pl.BlockSpec((T, LANE), lambda p, i: (0, _seg0(big) + base + p))

    return _call(
        body, name=name, grid=(H_ATT // 2, nt), args=(big, big, big, bias), comm=comm,
        in_specs=[pl.BlockSpec((rows, LANE), lambda p, i: (i, _seg0(big) + COL_AQ + p)), whole(COL_AK), whole(COL_AV),
                  pl.BlockSpec((2, ATT_TQ, ATT_SPAN), lambda p, i: (p, 0, 0))],
        out_specs=[pl.BlockSpec((rows, LANE), lambda p, i: (i, p))],
        out_shape=[_sds((T, BRANCH_W), BF)],
        scratch_shapes=[pltpu.VMEM((T + ATT_PAD, LANE), BF), pltpu.VMEM((T + ATT_PAD, LANE), BF)],
        sem=(ARB, ARB))


def _att_bwd(big, bias, dy, name, comm=None):
    T = big.shape[0]
    rows = ATT_SUB * ATT_TQ
    nt = T // rows
    scale = DH_ATT ** -0.5

    def body(q_ref, k_ref, v_ref, b_ref, dy_ref, dq_ref, dk_ref, dv_ref, db_ref, kp_s, vp_s, dk_s, dv_s):
        i = pl.program_id(1)

        @pl.when(i == 0)
        def _():
            _att_pad_fill(kp_s, k_ref, T)
            _att_pad_fill(vp_s, v_ref, T)
            dk_s[...] = jnp.zeros_like(dk_s)
            dv_s[...] = jnp.zeros_like(dv_s)
            db_ref[...] = jnp.zeros_like(db_ref)

        t0 = pl.multiple_of(i * rows, rows)
        win = pl.ds(t0, ATT_WIN)
        kw = kp_s[win, :]
        vw = vp_s[win, :]
        first, qm = _att_head_masks(q_ref[...])
        _, dom = _att_head_masks(dy_ref[...])
        dqs, dkt, dvt = [], None, None
        for hh in range(2):
            s_full = lax.dot_general(qm[hh], kw, NT_DIMS, preferred_element_type=F32)
            dp_full = lax.dot_general(dom[hh], vw, NT_DIMS, preferred_element_type=F32)
            ps, dss, db = [], [], None
            for sub in range(ATT_SUB):
                pn = _att_probs(s_full, sub, b_ref[hh], t0)
                dp = dp_full[sub * ATT_TQ:(sub + 1) * ATT_TQ, sub * ATT_TQ:sub * ATT_TQ + ATT_SPAN]
                ds = pn * (dp - jnp.sum(dp * pn, axis=-1, keepdims=True))
                db = ds if db is None else db + ds
                ps.append(pn.astype(BF))
                dss.append(ds.astype(BF))
            db_ref[hh] += db
            ds_band, p_band = _att_band(dss), _att_band(ps)
            dqs.append(jnp.dot(ds_band, kw, preferred_element_type=F32))
            qt = jnp.transpose(qm[hh].astype(F32)).astype(BF)
            dot_ = jnp.transpose(dom[hh].astype(F32)).astype(BF)
            dk_h = jnp.dot(qt, ds_band, preferred_element_type=F32)
            dv_h = jnp.dot(dot_, p_band, preferred_element_type=F32)
            dkt = dk_h if dkt is None else dkt + dk_h
            dvt = dv_h if dvt is None else dvt + dv_h
        dq_ref[...] = (jnp.where(first, dqs[0], dqs[1]) * scale).astype(BF)
        dk_s[win, :] += jnp.transpose(dkt) * scale
        dv_s[win, :] += jnp.transpose(dvt)

        @pl.when(i == nt - 1)
        def _():
            R = min(512, T)

            def cp(t, c):
                src = pl.ds(pl.multiple_of(ATT_PAD + t * R, LANE), R)
                dst = pl.ds(pl.multiple_of(t * R, R), R)
                dk_ref[dst, :] = dk_s[src, :].astype(BF)
                dv_ref[dst, :] = dv_s[src, :].astype(BF)
                return c

            lax.fori_loop(0, T // R, cp, 0)

    def whole(base):
        return pl.BlockSpec((T, LANE), lambda p, i: (0, _seg0(big) + base + p))

    tile = pl.BlockSpec((rows, LANE), lambda p, i: (i, p))
    bspec = pl.BlockSpec((2, ATT_TQ, ATT_SPAN), lambda p, i: (p, 0, 0))
    return _call(
        body, name=name, grid=(H_ATT // 2, nt), args=(big, big, big, bias, dy), comm=comm,
        in_specs=[pl.BlockSpec((rows, LANE), lambda p, i: (i, _seg0(big) + COL_AQ + p)), whole(COL_AK), whole(COL_AV), bspec, tile],
        out_specs=[tile, pl.BlockSpec((T, LANE), lambda p, i: (0, p)), pl.BlockSpec((T, LANE), lambda p, i: (0, p)), bspec],
        out_shape=[_sds((T, BRANCH_W), BF)] * 3 + [_sds((H_ATT, ATT_TQ, ATT_SPAN), F32)],
        scratch_shapes=[pltpu.VMEM((T + ATT_PAD, LANE), BF), pltpu.VMEM((T + ATT_PAD, LANE), BF),
                        pltpu.VMEM((T + ATT_PAD, LANE), F32), pltpu.VMEM((T + ATT_PAD, LANE), F32)],
        sem=(ARB, ARB))


def _merge_fwd(x1, big, ys, wb, wo, name):
    T, D = x1.shape
    tm = min(TM, T)

    def body(x_ref, gp_ref, yc_ref, yr_ref, ya_ref, wb_ref, wo_ref, x2_ref, p_ref, mg_ref):
        merged = jnp.zeros((tm, D), F32)
        for i, y_ref in enumerate((yc_ref, yr_ref, ya_ref)):
            cols = slice(i * D, (i + 1) * D)
            pb = jnp.dot(y_ref[...], wb_ref[i], preferred_element_type=F32).astype(BF)
            p_ref[:, cols] = pb
            merged = merged + _sigmoid(gp_ref[:, cols].astype(F32)) * pb.astype(F32)
        mb = merged.astype(BF)
        mg_ref[...] = mb
        x2_ref[...] = x_ref[...] + jnp.dot(mb, wo_ref[...], preferred_element_type=F32)

    tok = pl.BlockSpec((tm, D), lambda i: (i, 0))
    wide = pl.BlockSpec((tm, 3 * D), lambda i: (i, 0))
    yspec = pl.BlockSpec((tm, BRANCH_W), lambda i: (i, 0))
    return pl.pallas_call(
        body, name=name, grid=(T // tm,),
        in_specs=[tok, wide, yspec, yspec, yspec,
                  pl.BlockSpec((3, BRANCH_W, D), lambda i: (0, 0, 0)),
                  pl.BlockSpec((D, D), lambda i: (0, 0))],
        out_specs=[tok, wide, tok],
        out_shape=[_sds((T, D), F32), _sds((T, 3 * D), BF), _sds((T, D), BF)],
        compiler_params=_cp((PAR,)),
    )(x1, big, *ys, wb, wo)


def _merge_bwd(dx2, big, p, wb, wo, name):
    T, D = dx2.shape
    tm = min(TM, T)

    def body(dx_ref, gp_ref, p_ref, wb_ref, wo_ref, dp_ref, dgp_ref, dyc_ref, dyr_ref, dya_ref, dxb_ref):
        dxb = dx_ref[...].astype(BF)
        dxb_ref[...] = dxb
        dm = lax.dot_general(dxb, wo_ref[...], NT_DIMS, preferred_element_type=F32)
        for i, dy_ref in enumerate((dyc_ref, dyr_ref, dya_ref)):
            cols = slice(i * D, (i + 1) * D)
            gt = _sigmoid(gp_ref[:, cols].astype(F32))
            dpb = (dm * gt).astype(BF)
            dp_ref[:, cols] = dpb
            dgp_ref[:, cols] = (dm * p_ref[:, cols].astype(F32) * gt * (1.0 - gt)).astype(BF)
            dy_ref[...] = lax.dot_general(dpb, wb_ref[i], NT_DIMS, preferred_element_type=F32).astype(BF)

    tok = pl.BlockSpec((tm, D), lambda i: (i, 0))
    wide = pl.BlockSpec((tm, 3 * D), lambda i: (i, 0))
    yspec = pl.BlockSpec((tm, BRANCH_W), lambda i: (i, 0))
    return pl.pallas_call(
        body, name=name, grid=(T // tm,),
        in_specs=[tok, wide, wide,
                  pl.BlockSpec((3, BRANCH_W, D), lambda i: (0, 0, 0)),
                  pl.BlockSpec((D, D), lambda i: (0, 0))],
        out_specs=[wide, wide, yspec, yspec, yspec, tok],
        out_shape=[_sds((T, 3 * D), BF), _sds((T, 3 * D), BF)] + [_sds((T, BRANCH_W), BF)] * 3 + [_sds((T, D), BF)],
        compiler_params=_cp((PAR,)),
    )(dx2, big, p, wb, wo)


def _loss_head(x, tgt, fw, name):
    T, D = x.shape
    tm = min(TM, T)

    def body(x_ref, t_ref, w_ref, loss_ref, dx_ref, dw_ref):
        @pl.when(pl.program_id(0) == 0)
        def _():
            loss_ref[...] = jnp.zeros_like(loss_ref)
            dw_ref[...] = jnp.zeros_like(dw_ref)

        xv = x_ref[...]
        wv = w_ref[...]
        e = xv * _rms_r(xv) * wv - t_ref[...]
        loss_ref[...] += 0.5 * jnp.sum(jnp.mean(e * e, axis=-1, keepdims=True))
        dx, dn = _rms_bwd(e * (1.0 / D), xv, wv)
        dx_ref[...] = dx
        dw_ref[...] += dn

    tok = pl.BlockSpec((tm, D), lambda i: (i, 0))
    return pl.pallas_call(
        body, name=name, grid=(T // tm,),
        in_specs=[tok, tok, pl.BlockSpec((1, D), lambda i: (0, 0))],
        out_specs=[pl.BlockSpec((8, LANE), lambda i: (0, 0)), tok, pl.BlockSpec((1, D), lambda i: (0, 0))],
        out_shape=[_sds((8, LANE), F32), _sds((T, D), F32), _sds((1, D), F32)],
        compiler_params=_cp((ARB,)),
    )(x, tgt, fw)


def _block_rows(rows, cols):
    cap = max(8, (1 << 18) // cols)
    best = None
    for r in range(8, rows + 1, 8):
        if rows % r == 0 and r <= cap:
            best = r
    return best if best is not None else rows


def _sum4(land, l, n_layers, name, prev=None):
    _, rows, cols = land.shape
    br = _block_rows(rows, cols)

    def body(*refs):
        l_ref, o_ref = refs[0], refs[-1]
        o_ref[...] = ((l_ref[3].astype(F32) + l_ref[0].astype(F32)) + l_ref[1].astype(F32)) + l_ref[2].astype(F32)

    in_specs = [pl.BlockSpec((4, br, cols), lambda i: (0, i, 0))]
    args = [land]
    aliases = {}
    if prev is not None:
        in_specs.append(pl.BlockSpec(memory_space=pl.ANY))
        args.append(prev)
        aliases = {1: 0}
    return pl.pallas_call(
        body, name=name, grid=(rows // br,), in_specs=in_specs,
        out_specs=pl.BlockSpec((None, br, cols), lambda i: (l, i, 0)),
        out_shape=_sds((n_layers, rows, cols), F32),
        input_output_aliases=aliases, compiler_params=_cp((PAR,)),
    )(*args)


def _adamw_math(w, g, m, v):
    m = ADAM_B1 * m + (1.0 - ADAM_B1) * g
    v = ADAM_B2 * v + (1.0 - ADAM_B2) * (g * g)
    m_hat = m / (1.0 - ADAM_B1 ** ADAM_STEP)
    v_hat = v / (1.0 - ADAM_B2 ** ADAM_STEP)
    delta = -ADAM_LR * (m_hat / (jnp.sqrt(v_hat) + ADAM_EPS) + ADAM_WD * w)
    return delta, m, v


def _adamw(w, ga, gb, m, v, name):
    rows, cols = w.shape
    br = _block_rows(rows, cols)
    two = gb is not None

    def body(*refs):
        if two:
            w_ref, ga_ref, gb_ref, m_ref, v_ref, g_ref, d_ref, nm_ref, nv_ref = refs
            g = ga_ref[...] + gb_ref[...]
        else:
            w_ref, ga_ref, m_ref, v_ref, g_ref, d_ref, nm_ref, nv_ref = refs
            g = ga_ref[...]
        d, nm, nv = _adamw_math(w_ref[...], g, m_ref[...], v_ref[...])
        g_ref[...] = g
        d_ref[...] = d
        nm_ref[...] = nm
        nv_ref[...] = nv

    blk = pl.BlockSpec((br, cols), lambda i: (i, 0))
    args = [w, ga] + ([gb] if two else []) + [m, v]
    return pl.pallas_call(
        body, name=name, grid=(rows // br,),
        in_specs=[blk] * len(args), out_specs=[blk] * 4,
        out_shape=[_sds((rows, cols), F32)] * 4,
        compiler_params=_cp((PAR,)),
    )(*args)


def _swap_cores(vs, name):
    n = len(vs)

    def body(*refs):
        v_refs, o_refs = refs[:n], refs[n:2 * n]
        ssem, rsem = refs[2 * n:]
        x, y, c = _place()
        copies = [pltpu.make_async_remote_copy(
            src_ref=v_refs[k], dst_ref=o_refs[k], send_sem=ssem.at[k], recv_sem=rsem.at[k],
            device_id=(x, y, 1 - c), device_id_type=MESH) for k in range(n)]
        for cp in copies:
            cp.start()
        for cp in copies:
            cp.wait()

    hbm = pl.BlockSpec(memory_space=pl.ANY)
    return pl.pallas_call(
        body, name=name,
        in_specs=[hbm] * n, out_specs=[hbm] * n,
        out_shape=[_sds(v.shape, v.dtype) for v in vs],
        scratch_shapes=[pltpu.SemaphoreType.DMA((n,)), pltpu.SemaphoreType.DMA((n,))],
    )(*vs)


def _allreduce_small(v, name):
    rows = v.shape[0]
    flips = [(fx, fy, fc) for fx in (0, 1) for fy in (0, 1) for fc in (0, 1) if fx or fy or fc]

    def body(v_ref, o_ref, all_s, ssem, rsem):
        x, y, c = _place()

        def peer(f):
            return (x + f[0] - 2 * x * f[0], y + f[1] - 2 * y * f[1], c + f[2] - 2 * c * f[2])

        def slot(p):
            return all_s.at[4 * p[0] + 2 * p[1] + p[2]]

        def copy(k, f, owner):
            return pltpu.make_async_remote_copy(
                src_ref=v_ref, dst_ref=slot(owner), send_sem=ssem.at[k], recv_sem=rsem.at[k],
                device_id=peer(f), device_id_type=MESH)

        sends = [copy(k, f, (x, y, c)) for k, f in enumerate(flips)]
        for cp in sends:
            cp.start()
        all_s[4 * x + 2 * y + c] = v_ref[...]
        for k, f in enumerate(flips):
            copy(k, f, peer(f)).wait_recv()
        for cp in sends:
            cp.wait_send()
        acc = all_s[0]
        for d in range(1, 8):
            acc = acc + all_s[d]
        o_ref[...] = acc

    return pl.pallas_call(
        body, name=name,
        in_specs=[pl.BlockSpec(memory_space=pltpu.VMEM)],
        out_specs=pl.BlockSpec(memory_space=pltpu.VMEM),
        out_shape=_sds((rows, LANE), F32),
        scratch_shapes=[pltpu.VMEM((8, rows, LANE), F32), pltpu.SemaphoreType.DMA((7,)), pltpu.SemaphoreType.DMA((7,))],
    )(v)


BIG_NAMES = ("ffn1_w_gate", "ffn1_w_up", "ffn1_w_down", "w_in", "w_branch", "w_merge_gate", "w_out",
             "ffn2_w_gate", "ffn2_w_up", "ffn2_w_down")


FFN1 = ("ffn1_w_gate", "ffn1_w_up", "ffn1_w_down")
FFN2 = ("ffn2_w_gate", "ffn2_w_up", "ffn2_w_down")
MIX_IN = ("w_in", "w_merge_gate")
MIX_OUT = ("w_branch", "w_out")


def _keys(names, l):
    return [(n, l) for n in names]


def _local_step(x, tgt, small, convw_full, wx, n_layers):
    T, D = x.shape
    L = n_layers
    ns = N_SHARD
    dq = D // ns
    W = wx.w

    def hosted(call, keys, scatter=False):
        comm = wx.pieces(keys, scatter)
        main, extra = call(comm)
        if comm is not None:
            wx.arrived(keys, extra, scatter)
        return main

    def mixer_views(l):
        g4 = W[("w_merge_gate", l)]
        gates = jnp.transpose(g4, (0, 2, 1, 3)).reshape(D, 3 * D)
        win = jnp.transpose(W[("w_in", l)], (1, 0, 2)).reshape(D, -1)
        return jnp.concatenate([gates, win], axis=-1)

    def out_views(l):
        wb = jnp.transpose(W[("w_branch", l)], (1, 2, 0, 3)).reshape(3, BRANCH_W, D)
        wo = W[("w_out", l)].reshape(D, D)
        return wb, wo

    tb = _ret_tables(T)
    rb_pad = jnp.pad(small["rel_bias"], ((0, 0), (0, 0), (0, RB_PAD - N_REL)))

    saved = []
    h = x
    for l in range(L):
        s = {"x0": h}
        nxt = l + 1
        x1, s["g1"], s["u1"] = hosted(
            lambda c: _ffn_fwd(h, small["ffn1_norm"][l][None], W[("ffn1_w_gate", l)], W[("ffn1_w_up", l)],
                               W[("ffn1_w_down", l)], f"ffn1_fwd_{l}", comm=c), _keys(MIX_IN, l))
        s["x1"] = x1
        s["wbig"] = mixer_views(l)
        big, s["h"] = _inproj_fwd(x1, small["mix_norm"][l][None], s["wbig"], f"inproj_fwd_{l}")
        s["big"] = big
        s["bias"] = jnp.transpose(_relbias_expand(rb_pad[l], f"relbias_expand_{l}"), (1, 0, 2))
        s["yc"] = _conv_fwd(big, convw_full[l], f"conv_fwd_{l}")
        s["yr"], s["o"], s["st"] = hosted(lambda c: _ret_fwd(big, tb, f"ret_fwd_{l}", comm=c), _keys(MIX_OUT, l))
        (s["ya"],) = hosted(lambda c: _att_fwd(big, s["bias"], f"att_fwd_{l}", comm=c), _keys(FFN2, l))
        s["wb"], s["wo"] = out_views(l)
        x2, s["p"], s["mg"] = _merge_fwd(x1, big, (s["yc"], s["yr"], s["ya"]), s["wb"], s["wo"], f"merge_fwd_{l}")
        s["x2"] = x2
        h, s["g2"], s["u2"] = hosted(
            lambda c: _ffn_fwd(x2, small["ffn2_norm"][l][None], W[("ffn2_w_gate", l)], W[("ffn2_w_up", l)],
                               W[("ffn2_w_down", l)], f"ffn2_fwd_{l}", comm=c), _keys(FFN1, nxt) if nxt < L else [])
        saved.append(s)

    loss_p, dx, d_final = _loss_head(h, tgt, small["final_norm"][None], "loss_head")

    gs = {"final_norm": d_final[0]}
    for k in ("ffn1_norm", "mix_norm", "ffn2_norm", "rel_bias", "conv_w"):
        gs[k] = [None] * L
    tk = min(2048, T)
    nk = T // tk

    def ffn_grads(pre, l, hb, dgv, duv, av, dacc, chain=False):
        fs = dgv.shape[-1]
        hspec = pl.BlockSpec((tk, D), lambda p, q, k: (k, 0))
        sspec = pl.BlockSpec((None, tk, fs), lambda p, q, k: (p, k, 0))
        up_spec = pl.BlockSpec((None, D, fs), lambda p, q, k: (p, 0, 0))
        down_spec = pl.BlockSpec((None, fs, D), lambda p, q, k: (p, 0, 0))
        jobs = [(pre + "_w_gate", hb, dgv, hspec, sspec, (ns, D, fs), up_spec),
                (pre + "_w_up", hb, duv, hspec, sspec, (ns, D, fs), up_spec),
                (pre + "_w_down", av, dacc, sspec, hspec, (ns, fs, D), down_spec)]
        before = None
        for nm, a, b, a_spec, b_spec, shape, o_spec in jobs:
            def product(c):
                r = _tn(a, b, a_spec, b_spec, _sds(shape, BF), o_spec, (ns, 1, nk), f"d{nm}_{l}", comm=c)
                return (r, []) if c is None else r
            wx.g[(nm, l)] = hosted(product, [before] if chain and before else [], scatter=True)
            before = (nm, l)

    for l in reversed(range(L)):
        s = saved[l]
        dx, dgv, duv, av, hb, dacc, dn = hosted(
            lambda c: _ffn_bwd(dx, s["x2"], small["ffn2_norm"][l][None], s["g2"], s["u2"], W[("ffn2_w_gate", l)],
                               W[("ffn2_w_up", l)], W[("ffn2_w_down", l)], f"ffn2_bwd_{l}", comm=c),
            _keys(FFN1, l + 1) if l + 1 < L else [], scatter=True)
        gs["ffn2_norm"][l] = dn[0]
        ffn_grads("ffn2", l, hb, dgv, duv, av, dacc)
        dp, dgp, dyc, dyr, dya, dxb = _merge_bwd(dx, s["big"], s["p"], s["wb"], s["wo"], f"merge_bwd_{l}")
        wx.g[("w_out", l)] = _tn(
            s["mg"], dxb, pl.BlockSpec((tk, dq), lambda p, q, k: (k, p)), pl.BlockSpec((tk, D), lambda p, q, k: (k, 0)),
            _sds((ns, dq, D), BF), pl.BlockSpec((None, dq, D), lambda p, q, k: (p, 0, 0)), (ns, 1, nk), f"dw_out_{l}")
        gb = None
        for i, yv in enumerate((s["yc"], s["yr"], s["ya"])):
            gb = _tn(yv, dp,
                     pl.BlockSpec((tk, BRANCH_W), lambda p, q, k: (k, 0)),
                     pl.BlockSpec((tk, dq), lambda p, q, k, i=i: (k, i * ns + p)),
                     _sds((ns, 3, BRANCH_W, dq), BF),
                     pl.BlockSpec((None, None, BRANCH_W, dq), lambda p, q, k, i=i: (p, i, 0, 0)),
                     (ns, 1, nk), f"dw_branch{i}_{l}", prev=gb)
        wx.g[("w_branch", l)] = gb
        dcu, dcb, dcc, dcw = _conv_bwd(s["big"], dyc, convw_full[l], f"conv_bwd_{l}")
        gs["conv_w"][l] = dcw
        drq, drk, drv, drg = _ret_bwd(s["big"], s["o"], s["st"], dyr, tb, f"ret_bwd_{l}")
        daq, dak, dav, dbias = hosted(lambda c: _att_bwd(s["big"], s["bias"], dya, f"att_bwd_{l}", comm=c),
                                      _keys(FFN2, l), scatter=True)
        gs["rel_bias"][l] = _relbias_grad(jnp.transpose(dbias, (1, 0, 2)), f"relbias_grad_{l}")[:, :N_REL]
        dbig = jnp.concatenate([dgp, dcu, dcb, dcc, drq, drk, drv, drg, daq, dak, dav], axis=1)
        ics = W[("w_in", l)].shape[-1]
        nq = ics // 256
        wx.g[("w_in", l)] = _tn(
            s["h"], dbig, pl.BlockSpec((tk, D), lambda p, q, k: (k, 0)),
            pl.BlockSpec((tk, 256), lambda p, q, k: (k, 3 * D // 256 + p * nq + q)),
            _sds((ns, D, ics), BF), pl.BlockSpec((None, D, 256), lambda p, q, k: (p, 0, q)),
            (ns, nq, nk), f"dw_in_{l}")
        wx.g[("w_merge_gate", l)] = _tn(
            s["h"], dbig, pl.BlockSpec((tk, dq), lambda p, q, k: (k, p)), pl.BlockSpec((tk, D), lambda p, q, k: (k, q)),
            _sds((ns, 3, dq, D), BF), pl.BlockSpec((None, None, dq, D), lambda p, q, k: (p, q, 0, 0)),
            (ns, 3, nk), f"dw_merge_gate_{l}")
        dx, dn = _inproj_bwd(dbig, s["wbig"], s["x1"], small["mix_norm"][l][None], dx, f"inproj_bwd_{l}")
        gs["mix_norm"][l] = dn[0]
        dx, dgv, duv, av, hb, dacc, dn = hosted(
            lambda c: _ffn_bwd(dx, s["x0"], small["ffn1_norm"][l][None], s["g1"], s["u1"], W[("ffn1_w_gate", l)],
                               W[("ffn1_w_up", l)], W[("ffn1_w_down", l)], f"ffn1_bwd_{l}", comm=c),
            _keys(MIX_IN + MIX_OUT, l), scatter=True)
        gs["ffn1_norm"][l] = dn[0]
        ffn_grads("ffn1", l, hb, dgv, duv, av, dacc, chain=(l == 0))

    for k in ("ffn1_norm", "mix_norm", "ffn2_norm", "rel_bias", "conv_w"):
        gs[k] = jnp.stack(gs[k])
    return loss_p, dx, gs


class _Exchange:
    def __init__(self, shards):
        self.shards = shards
        self.w = {}
        self.g = {}
        self.landed = {}

    def own(self, key):
        return self.shards[key[0]][key[1]].astype(BF)

    def pieces(self, keys, scatter):
        if not keys:
            return None
        return _Pieces([self.g[k] for k in keys] if scatter else [self.own(k) for k in keys], scatter)

    def arrived(self, keys, outs, scatter):
        for k, o in zip(keys, outs):
            (self.landed if scatter else self.w)[k] = o


W_NAMES = ("ffn1_norm", "ffn1_w_gate", "ffn1_w_up", "ffn1_w_down", "mix_norm", "w_in", "conv_w", "rel_bias", "w_branch",
           "w_merge_gate", "w_out", "ffn2_norm", "ffn2_w_gate", "ffn2_w_up", "ffn2_w_down", "final_norm")


def _as2d(a):
    return a.reshape(1, -1) if a.ndim == 1 else a.reshape(-1, a.shape[-1])


def kernel(x, ffn1_norm, ffn1_w_gate, ffn1_w_up, ffn1_w_down, mix_norm, w_in, conv_w, rel_bias, w_branch, w_merge_gate, w_out, ffn2_norm, ffn2_w_gate, ffn2_w_up, ffn2_w_down, final_norm, loss_target, m_ffn1_norm, m_ffn1_w_gate, m_ffn1_w_up, m_ffn1_w_down, m_mix_norm, m_w_in, m_conv_w, m_rel_bias, m_w_branch, m_w_merge_gate, m_w_out, m_ffn2_norm, m_ffn2_w_gate, m_ffn2_w_up, m_ffn2_w_down, m_final_norm, v_ffn1_norm, v_ffn1_w_gate, v_ffn1_w_up, v_ffn1_w_down, v_mix_norm, v_w_in, v_conv_w, v_rel_bias, v_w_branch, v_w_merge_gate, v_w_out, v_ffn2_norm, v_ffn2_w_gate, v_ffn2_w_up, v_ffn2_w_down, v_final_norm):
    given = dict(locals())
    w = {n: given[n] for n in W_NAMES}
    m = {n: given["m_" + n] for n in W_NAMES}
    v = {n: given["v_" + n] for n in W_NAMES}
    my_chip = 2 * lax.axis_index("x") + lax.axis_index("y")
    L = w_in.shape[0]

    wx = _Exchange({n: w[n] for n in BIG_NAMES})
    first = _keys(FFN1, 0)
    comm = _Pieces([wx.own(k) for k in first] + [conv_w], scatter=False)
    got = _comm_alone(comm, "gather_first")
    wx.arrived(first, got[:-1], False)
    convw_full = jnp.transpose(got[-1], (1, 2, 0, 3)).reshape(conv_w.shape[0], conv_w.shape[1], -1)

    small = {n: w[n] for n in ("ffn1_norm", "mix_norm", "ffn2_norm", "final_norm", "rel_bias")}
    loss_p, grad_x, gs = _local_step(x[0], loss_target[0], small, convw_full, wx, L)
    last = [(FFN1[-1], 0)]
    wx.arrived(last, _comm_alone(wx.pieces(last, True), "scatter_last"), True)

    sums = []
    for n in BIG_NAMES:
        acc = None
        for l in range(L):
            a = wx.landed[(n, l)]
            acc = _sum4(a.reshape(4, -1, a.shape[-1]), l, L, f"sum4_{n}_{l}", prev=acc)
        sums.append(acc.reshape(-1, acc.shape[-1]))
    others = _swap_cores(sums, "swap_cores")

    parts = [gs["ffn1_norm"].reshape(-1), gs["mix_norm"].reshape(-1), gs["ffn2_norm"].reshape(-1),
             gs["final_norm"].reshape(-1), gs["rel_bias"].reshape(-1), gs["conv_w"].reshape(-1), loss_p[0]]
    sizes = [p.shape[0] for p in parts]
    flat = jnp.concatenate(parts)
    rows = -(-flat.shape[0] // (8 * LANE)) * 8
    flat = jnp.pad(flat, (0, rows * LANE - flat.shape[0])).reshape(rows, LANE)
    red = _allreduce_small(flat, "allreduce_small").reshape(-1)
    offs = [0]
    for sz in sizes:
        offs.append(offs[-1] + sz)
    sm = {}
    for i, n in enumerate(("ffn1_norm", "mix_norm", "ffn2_norm", "final_norm", "rel_bias", "conv_w")):
        sm[n] = red[offs[i]:offs[i + 1]]
    loss = red[offs[6]]
    sm["conv_w"] = lax.dynamic_slice_in_dim(sm["conv_w"].reshape(conv_w.shape[0], conv_w.shape[1], -1),
                                            my_chip * conv_w.shape[2], conv_w.shape[2], axis=2)

    grads, deltas, new_m, new_v = {}, {}, {}, {}
    big_sum = dict(zip(BIG_NAMES, zip(sums, others)))
    for n in W_NAMES:
        shape = w[n].shape
        if n in big_sum:
            ga, gb = big_sum[n]
        else:
            ga, gb = _as2d(sm[n].reshape(shape)), None
        out = _adamw(_as2d(w[n]), ga, gb, _as2d(m[n]), _as2d(v[n]), f"adamw_{n}")
        grads[n], deltas[n], new_m[n], new_v[n] = (o.reshape(shape) for o in out)

    return (loss, grad_x[None], *[grads[n] for n in W_NAMES], *[deltas[n] for n in W_NAMES],
            *[new_m[n] for n in W_NAMES], *[new_v[n] for n in W_NAMES])
```

```python
import functools
import math

import jax
import jax.numpy as jnp
from jax import lax
from jax.experimental import pallas as pl
from jax.experimental.pallas import tpu as pltpu

F32 = jnp.float32
BF = jnp.bfloat16
MESH = pl.DeviceIdType.MESH
ARB = "arbitrary"
PAR = "parallel"

EPS = 1e-6
NEG_INF = -1e30
ROPE_BASE = 10000.0
CHUNK = 64
BRANCH_W = 512
H_RET = 4
DK_RET = 128
H_ATT = 8
DH_ATT = 64
N_PREV = 8
REL_CLIP = 128
N_REL = 2 * REL_CLIP + 1
N_SHARD = 4
LANE = 128
RET_L = 512
ATT_TQ = 128
ATT_SUB = 4
ATT_PAD = N_PREV * CHUNK
ATT_SPAN = ATT_TQ + ATT_PAD
ATT_TOEP = 2 * REL_CLIP
RB_PAD = 264
TM = 512

ADAM_LR = 0.001
ADAM_B1 = 0.9
ADAM_B2 = 0.999
ADAM_EPS = 1e-08
ADAM_WD = 0.01
ADAM_STEP = 10

NT_DIMS = (((1,), (1,)), ((), ()))
TN_DIMS = (((0,), (0,)), ((), ()))


def _cp(sem, vmem_mb=48):
    return pltpu.CompilerParams(dimension_semantics=sem, vmem_limit_bytes=vmem_mb << 20)


def _sds(shape, dtype):
    return jax.ShapeDtypeStruct(tuple(shape), dtype)


def _rms_r(x):
    return lax.rsqrt(jnp.mean(x * x, axis=-1, keepdims=True) + EPS)


def _sigmoid(x):
    return 0.5 * jnp.tanh(0.5 * x) + 0.5


def _rms_bwd(dh, xv, nw):
    r = _rms_r(xv)
    xh = xv * r
    dxh = dh * nw
    dx = r * (dxh - xh * jnp.mean(dxh * xh, axis=-1, keepdims=True))
    return dx, jnp.sum(dh * xh, axis=0, keepdims=True)


def _place():
    return lax.axis_index("x"), lax.axis_index("y"), lax.axis_index("c")


def _other_chips(x, y):
    return [(1 - x, y), (x, 1 - y), (1 - x, 1 - y)]


class _Pieces:
    def __init__(self, srcs, scatter):
        self.srcs = list(srcs)
        self.scatter = scatter
        n = len(self.srcs)
        self.out_shape = [_sds(s.shape if scatter else (N_SHARD,) + s.shape, s.dtype) for s in self.srcs]
        self.scratch = [pltpu.SemaphoreType.DMA((n,)), pltpu.SemaphoreType.DMA((3, n)), pltpu.SemaphoreType.DMA((3, n))]

    def _copies(self, src, dst, sems, waiting):
        lsem, ssem, rsem = sems
        x, y, c = _place()
        mine = 2 * x + y
        n = len(src)

        def remote(j, k, chip, s_ref, d_ref):
            return pltpu.make_async_remote_copy(
                src_ref=s_ref, dst_ref=d_ref, send_sem=ssem.at[j, k], recv_sem=rsem.at[j, k],
                device_id=(chip[0], chip[1], c), device_id_type=MESH)

        chips = list(enumerate(_other_chips(x, y)))
        if self.scatter:
            local = [pltpu.make_async_copy(src[k].at[mine], dst[k].at[3], lsem.at[k]) for k in range(n)]
            sends = [remote(j, k, ch, src[k].at[2 * ch[0] + ch[1]], dst[k].at[j]) for j, ch in chips for k in range(n)]
            recvs = sends
        else:
            local = [pltpu.make_async_copy(src[k], dst[k].at[mine], lsem.at[k]) for k in range(n)]
            sends = [remote(j, k, ch, src[k], dst[k].at[mine]) for j, ch in chips for k in range(n)]
            recvs = [remote(j, k, ch, src[k], dst[k].at[2 * ch[0] + ch[1]]) for j, ch in chips for k in range(n)
                     ] if waiting else []
        return local, sends, recvs

    def start(self, src, dst, sems):
        local, sends, _ = self._copies(src, dst, sems, False)
        for cp in local + sends:
            cp.start()

    def wait(self, src, dst, sems):
        local, sends, recvs = self._copies(src, dst, sems, True)
        for cp in recvs:
            cp.wait_recv()
        for cp in sends:
            cp.wait_send()
        for cp in local:
            cp.wait()


def _call(body, *, name, args, in_specs, out_specs, out_shape, grid=(), scratch_shapes=(), sem=None, comm=None,
          aliases=None, vmem_mb=48):
    in_specs, out_specs, out_shape = list(in_specs), list(out_specs), list(out_shape)
    scratch, args = list(scratch_shapes), list(args)
    n_in, n_out, n_scr = len(in_specs), len(out_specs), len(scratch)
    if comm is None:
        def kernel_body(*refs):
            body(*refs)
    else:
        c_in, c_out = len(comm.srcs), len(comm.out_shape)

        def kernel_body(*refs):
            o0 = n_in + c_in
            s0 = o0 + n_out + c_out
            cin, cout, sems = refs[n_in:o0], refs[o0 + n_out:s0], refs[s0 + n_scr:]
            main = refs[:n_in] + refs[o0:o0 + n_out] + refs[s0:s0 + n_scr]
            if grid:
                ids = [pl.program_id(a) for a in range(len(grid))]
                first = functools.reduce(lambda p, q: p & q, [i == 0 for i in ids])
                last = functools.reduce(lambda p, q: p & q, [i == g - 1 for i, g in zip(ids, grid)])

                @pl.when(first)
                def _():
                    comm.start(cin, cout, sems)

                body(*main)

                @pl.when(last)
                def _():
                    comm.wait(cin, cout, sems)
            else:
                comm.start(cin, cout, sems)
                body(*main)
                comm.wait(cin, cout, sems)

        hbm = pl.BlockSpec(memory_space=pl.ANY)
        in_specs += [hbm] * c_in
        out_specs += [hbm] * c_out
        out_shape += comm.out_shape
        scratch += comm.scratch
        args += comm.srcs
    params = dict(vmem_limit_bytes=vmem_mb << 20)
    if grid:
        params["dimension_semantics"] = sem
    outs = pl.pallas_call(
        kernel_body, name=name, grid=grid, in_specs=in_specs, out_specs=out_specs, out_shape=out_shape,
        scratch_shapes=scratch, input_output_aliases=aliases or {}, compiler_params=pltpu.CompilerParams(**params),
    )(*args)
    return list(outs[:n_out]), list(outs[n_out:])


def _comm_alone(comm, name):
    return _call(lambda: None, name=name, args=[], in_specs=[], out_specs=[], out_shape=[], comm=comm)[1]


def _ffn_fwd(x, nw, wg, wu, wd, name, comm=None):
    T, D = x.shape
    ns, _, fs = wg.shape
    tm = min(TM, T)

    def body(x_ref, nw_ref, wg_ref, wu_ref, wd_ref, xo_ref, g_ref, u_ref, h_s, acc_s):
        j = pl.program_id(1)

        @pl.when(j == 0)
        def _():
            xv = x_ref[...]
            h_s[...] = (xv * _rms_r(xv) * nw_ref[...]).astype(BF)
            acc_s[...] = jnp.zeros_like(acc_s)

        h = h_s[...]
        gb = jnp.dot(h, wg_ref[...], preferred_element_type=F32).astype(BF)
        ub = jnp.dot(h, wu_ref[...], preferred_element_type=F32).astype(BF)
        g_ref[...] = gb
        u_ref[...] = ub
        g = gb.astype(F32)
        a = (g * _sigmoid(g) * ub.astype(F32)).astype(BF)
        acc_s[...] += jnp.dot(a, wd_ref[...], preferred_element_type=F32)

        @pl.when(j == ns - 1)
        def _():
            xo_ref[...] = x_ref[...] + 0.5 * acc_s[...]

    wspec = pl.BlockSpec((None, D, fs), lambda i, j: (j, 0, 0))
    return _call(
        body, name=name, grid=(T // tm, ns), args=(x, nw, wg, wu, wd), comm=comm,
        in_specs=[pl.BlockSpec((tm, D), lambda i, j: (i, 0)),
                  pl.BlockSpec((1, D), lambda i, j: (0, 0)),
                  wspec, wspec,
                  pl.BlockSpec((None, fs, D), lambda i, j: (j, 0, 0))],
        out_specs=[pl.BlockSpec((tm, D), lambda i, j: (i, 0)),
                   pl.BlockSpec((None, tm, fs), lambda i, j: (j, i, 0)),
                   pl.BlockSpec((None, tm, fs), lambda i, j: (j, i, 0))],
        out_shape=[_sds((T, D), F32), _sds((ns, T, fs), BF), _sds((ns, T, fs), BF)],
        scratch_shapes=[pltpu.VMEM((tm, D), BF), pltpu.VMEM((tm, D), F32)],
        sem=(ARB, ARB))


def _ffn_bwd(dxo, x, nw, g, u, wg, wu, wd, name, comm=None):
    T, D = x.shape
    ns, _, fs = wg.shape
    tm = min(TM, T)

    def body(dxo_ref, x_ref, nw_ref, g_ref, u_ref, wg_ref, wu_ref, wd_ref,
             dx_ref, dg_ref, du_ref, a_ref, h_ref, dacc_ref, dnw_ref, dacc_s, acc_s):
        i = pl.program_id(0)
        j = pl.program_id(1)

        @pl.when(j == 0)
        def _():
            xv = x_ref[...]
            h_ref[...] = (xv * _rms_r(xv) * nw_ref[...]).astype(BF)
            db = (0.5 * dxo_ref[...]).astype(BF)
            dacc_ref[...] = db
            dacc_s[...] = db
            acc_s[...] = jnp.zeros_like(acc_s)

        @pl.when((i == 0) & (j == 0))
        def _():
            dnw_ref[...] = jnp.zeros_like(dnw_ref)

        da = lax.dot_general(dacc_s[...], wd_ref[...], NT_DIMS, preferred_element_type=F32)
        gv = g_ref[...].astype(F32)
        uv = u_ref[...].astype(F32)
        s = _sigmoid(gv)
        sg = gv * s
        a_ref[...] = (sg * uv).astype(BF)
        dub = (da * sg).astype(BF)
        dgb = (da * uv * (s * (1.0 + gv * (1.0 - s)))).astype(BF)
        dg_ref[...] = dgb
        du_ref[...] = dub
        acc_s[...] += (lax.dot_general(dgb, wg_ref[...], NT_DIMS, preferred_element_type=F32)
                       + lax.dot_general(dub, wu_ref[...], NT_DIMS, preferred_element_type=F32))

        @pl.when(j == ns - 1)
        def _():
            dx, dn = _rms_bwd(acc_s[...], x_ref[...], nw_ref[...])
            dx_ref[...] = dxo_ref[...] + dx
            dnw_ref[...] += dn

    tok = pl.BlockSpec((tm, D), lambda i, j: (i, 0))
    row = pl.BlockSpec((1, D), lambda i, j: (0, 0))
    hid = pl.BlockSpec((None, tm, fs), lambda i, j: (j, i, 0))
    wspec = pl.BlockSpec((None, D, fs), lambda i, j: (j, 0, 0))
    return _call(
        body, name=name, grid=(T // tm, ns), args=(dxo, x, nw, g, u, wg, wu, wd), comm=comm,
        in_specs=[tok, tok, row, hid, hid, wspec, wspec,
                  pl.BlockSpec((None, fs, D), lambda i, j: (j, 0, 0))],
        out_specs=[tok, hid, hid, hid, tok, tok, row],
        out_shape=[_sds((T, D), F32), _sds((ns, T, fs), BF), _sds((ns, T, fs), BF), _sds((ns, T, fs), BF),
                   _sds((T, D), BF), _sds((T, D), BF), _sds((1, D), F32)],
        scratch_shapes=[pltpu.VMEM((tm, D), BF), pltpu.VMEM((tm, D), F32)],
        sem=(ARB, ARB))


def _tn(a, b, a_spec, b_spec, out_shape, out_spec, grid, name, prev=None, comm=None):
    nk = grid[-1]
    acc_shape = tuple(d for d in out_spec.block_shape if d is not None)

    def body(*refs):
        a_ref, b_ref = refs[0], refs[1]
        o_ref, acc = refs[-2], refs[-1]
        k = pl.program_id(2)
        prod = lax.dot_general(a_ref[...], b_ref[...], TN_DIMS, preferred_element_type=F32)

        @pl.when(k == 0)
        def _():
            acc[...] = prod

        @pl.when(k > 0)
        def _():
            acc[...] += prod

        @pl.when(k == nk - 1)
        def _():
            o_ref[...] = acc[...].astype(o_ref.dtype)

    in_specs = [a_spec, b_spec]
    args = [a, b]
    aliases = {}
    if prev is not None:
        in_specs.append(pl.BlockSpec(memory_space=pl.ANY))
        args.append(prev)
        aliases = {2: 0}
    main, extra = _call(
        body, name=name, grid=grid, args=args, in_specs=in_specs, out_specs=[out_spec], out_shape=[out_shape],
        scratch_shapes=[pltpu.VMEM(acc_shape, F32)], aliases=aliases, sem=(ARB, ARB, ARB), comm=comm)
    return main[0] if comm is None else (main[0], extra)


def _tn_gates(h, dbig, ns, tk, name):
    T, D = h.shape
    dq = D // ns
    nk = T // tk

    def body(a_ref, b_ref, o_ref, acc):
        k = pl.program_id(1)
        prod = lax.dot_general(a_ref[...], b_ref[...], TN_DIMS, preferred_element_type=F32)

        @pl.when(k == 0)
        def _():
            acc[...] = prod

        @pl.when(k > 0)
        def _():
            acc[...] += prod

        @pl.when(k == nk - 1)
        def _():
            for s in range(ns):
                o_ref[s] = acc[s * dq:(s + 1) * dq, :].astype(o_ref.dtype)

    return pl.pallas_call(
        body, name=name, grid=(3, nk),
        in_specs=[pl.BlockSpec((tk, D), lambda q, k: (k, 0)), pl.BlockSpec((tk, D), lambda q, k: (k, q))],
        out_specs=pl.BlockSpec((ns, None, dq, D), lambda q, k: (0, q, 0, 0)),
        out_shape=_sds((ns, 3, dq, D), BF),
        scratch_shapes=[pltpu.VMEM((D, D), F32)],
        compiler_params=_cp((PAR, ARB)),
    )(h, dbig)


def _inproj_fwd(x, nw, wbig, name):
    T, D = x.shape
    nb = wbig.shape[-1]
    tm = min(2 * TM, T)
    bn = min(2048, nb)

    def body(x_ref, nw_ref, w_ref, o_ref, h_ref, h_s):
        @pl.when(pl.program_id(1) == 0)
        def _():
            xv = x_ref[...]
            hb = (xv * _rms_r(xv) * nw_ref[...]).astype(BF)
            h_s[...] = hb
            h_ref[...] = hb

        o_ref[...] = jnp.dot(h_s[...], w_ref[...], preferred_element_type=F32).astype(BF)

    return pl.pallas_call(
        body, name=name, grid=(T // tm, nb // bn),
        in_specs=[pl.BlockSpec((tm, D), lambda i, n: (i, 0)),
                  pl.BlockSpec((1, D), lambda i, n: (0, 0)),
                  pl.BlockSpec((D, bn), lambda i, n: (0, n))],
        out_specs=[pl.BlockSpec((tm, bn), lambda i, n: (i, n)),
                   pl.BlockSpec((tm, D), lambda i, n: (i, 0))],
        out_shape=[_sds((T, nb), BF), _sds((T, D), BF)],
        scratch_shapes=[pltpu.VMEM((tm, D), BF)],
        compiler_params=_cp((PAR, ARB)),
    )(x, nw, wbig)


def _inproj_bwd(dbig, wbig, x, nw, dxin, name):
    T, D = x.shape
    nb = wbig.shape[-1]
    tm = min(TM, T)
    tk = min(2048, nb)
    nk = nb // tk

    def body(a_ref, w_ref, x_ref, nw_ref, dxin_ref, dx_ref, dnw_ref, acc_s):
        i = pl.program_id(0)
        k = pl.program_id(1)
        prod = lax.dot_general(a_ref[...], w_ref[...], NT_DIMS, preferred_element_type=F32)

        @pl.when((i == 0) & (k == 0))
        def _():
            dnw_ref[...] = jnp.zeros_like(dnw_ref)

        @pl.when(k == 0)
        def _():
            acc_s[...] = prod

        @pl.when(k > 0)
        def _():
            acc_s[...] += prod

        @pl.when(k == nk - 1)
        def _():
            dx, dn = _rms_bwd(acc_s[...], x_ref[...], nw_ref[...])
            dx_ref[...] = dxin_ref[...] + dx
            dnw_ref[...] += dn

    tok = pl.BlockSpec((tm, D), lambda i, k: (i, 0))
    row = pl.BlockSpec((1, D), lambda i, k: (0, 0))
    return pl.pallas_call(
        body, name=name, grid=(T // tm, nk),
        in_specs=[pl.BlockSpec((tm, tk), lambda i, k: (i, k)),
                  pl.BlockSpec((D, tk), lambda i, k: (0, k)),
                  tok, row, tok],
        out_specs=[tok, row],
        out_shape=[_sds((T, D), F32), _sds((1, D), F32)],
        scratch_shapes=[pltpu.VMEM((tm, D), F32)],
        compiler_params=_cp((ARB, ARB)),
    )(dbig, wbig, x, nw, dxin)


CONV_R = 512
CONV_BASE, CONV_GROUP = 0, 3
ATT_BASE, ATT_GROUP = 12, 3
RET_BASE, RET_GROUP = 24, 4
N_SEG = 10


def _permute_in_cols(w):
    lead = w.shape[:-1]
    w4 = w.reshape(lead + (N_SEG, BRANCH_W // LANE, LANE))

    def grouped(lo, hi):
        return jnp.swapaxes(w4[..., lo:hi, :, :], -3, -2).reshape(lead + (-1,))

    return jnp.concatenate([grouped(0, 3), grouped(7, 10), grouped(3, 7)], axis=-1)


def _unpermute_in_cols(w):
    lead = w.shape[:-1]
    nblk = BRANCH_W // LANE

    def segs(lo, n):
        part = w[..., lo * LANE:(lo + nblk * n) * LANE].reshape(lead + (nblk, n, LANE))
        return jnp.swapaxes(part, -3, -2)

    conv, att, ret = segs(CONV_BASE, 3), segs(ATT_BASE, 3), segs(RET_BASE, 4)
    return jnp.concatenate([conv, ret, att], axis=-3).reshape(lead + (-1,))


def _seg0(big):
    return (big.shape[1] - N_SEG * BRANCH_W) // LANE


def _group_spec(big, base, group, rows, where):
    first = (_seg0(big) + base) // group
    assert first * group == _seg0(big) + base

    def index(*ids):
        r, g = where(*ids)
        return r, first + g

    return pl.BlockSpec((rows, group * LANE), index)


CU, CB, CC = (slice(k * LANE, (k + 1) * LANE) for k in range(3))
AQ, AK, AV = CU, CB, CC
RQ, RK, RV, RG = (slice(k * LANE, (k + 1) * LANE) for k in range(4))


def _conv_fwd(big, cw, name):
    T = big.shape[0]
    R = min(CONV_R, T)

    def body(g_ref, w_ref, y_ref, z_s):
        z_s[pl.ds(0, 8), :] = jnp.zeros((8, LANE), F32)

        def fill(t, c):
            sl = pl.ds(pl.multiple_of(t * R, R), R)
            z_s[pl.ds(pl.multiple_of(t * R + 8, 8), R), :] = g_ref[sl, CC].astype(F32) * g_ref[sl, CU].astype(F32)
            return c

        lax.fori_loop(0, T // R, fill, 0)
        w0, w1, w2 = w_ref[0:1, :], w_ref[1:2, :], w_ref[2:3, :]

        def step(t, c):
            zz = z_s[pl.ds(pl.multiple_of(t * R, R), R + 8), :]
            z0 = zz[8:]
            z1 = pltpu.roll(zz, 1, 0)[8:]
            z2 = pltpu.roll(zz, 2, 0)[8:]
            sl = pl.ds(pl.multiple_of(t * R, R), R)
            y_ref[sl, :] = (g_ref[sl, CB].astype(F32) * (w2 * z0 + w1 * z1 + w0 * z2)).astype(BF)
            return c

        lax.fori_loop(0, T // R, step, 0)

    return pl.pallas_call(
        body, name=name, grid=(BRANCH_W // LANE,),
        in_specs=[_group_spec(big, CONV_BASE, CONV_GROUP, T, lambda j: (0, j)),
                  pl.BlockSpec((3, LANE), lambda j: (0, j))],
        out_specs=pl.BlockSpec((T, LANE), lambda j: (0, j)),
        out_shape=_sds((T, BRANCH_W), BF),
        scratch_shapes=[pltpu.VMEM((T + 8, LANE), F32)],
        compiler_params=_cp((PAR,)),
    )(big, cw)


def _conv_bwd(big, dy, cw, dbig, name):
    T = big.shape[0]
    R = min(CONV_R, T)

    def body(g_ref, dy_ref, w_ref, _, o_ref, dw_ref, z_s, d_s):
        z_s[pl.ds(0, 8), :] = jnp.zeros((8, LANE), F32)
        d_s[pl.ds(T, 8), :] = jnp.zeros((8, LANE), F32)

        def fill(t, c):
            sl = pl.ds(pl.multiple_of(t * R, R), R)
            z_s[pl.ds(pl.multiple_of(t * R + 8, 8), R), :] = g_ref[sl, CC].astype(F32) * g_ref[sl, CU].astype(F32)
            d_s[sl, :] = dy_ref[sl, :].astype(F32) * g_ref[sl, CB].astype(F32)
            return c

        lax.fori_loop(0, T // R, fill, 0)
        w0, w1, w2 = w_ref[0:1, :], w_ref[1:2, :], w_ref[2:3, :]

        def step(t, carry):
            a0, a1, a2 = carry
            zz = z_s[pl.ds(pl.multiple_of(t * R, R), R + 8), :]
            z0 = zz[8:]
            z1 = pltpu.roll(zz, 1, 0)[8:]
            z2 = pltpu.roll(zz, 2, 0)[8:]
            sl = pl.ds(pl.multiple_of(t * R, R), R)
            dyv = dy_ref[sl, :].astype(F32)
            o_ref[sl, CB] = (dyv * (w2 * z0 + w1 * z1 + w0 * z2)).astype(BF)
            dd = d_s[pl.ds(pl.multiple_of(t * R, R), R + 8), :]
            d0 = dd[:R]
            d1 = pltpu.roll(dd, R + 7, 0)[:R]
            d2 = pltpu.roll(dd, R + 6, 0)[:R]
            dz = w2 * d0 + w1 * d1 + w0 * d2
            o_ref[sl, CC] = (dz * g_ref[sl, CU].astype(F32)).astype(BF)
            o_ref[sl, CU] = (dz * g_ref[sl, CC].astype(F32)).astype(BF)
            a0 = a0 + jnp.sum(d0 * z2, axis=0, keepdims=True)
            a1 = a1 + jnp.sum(d0 * z1, axis=0, keepdims=True)
            a2 = a2 + jnp.sum(d0 * z0, axis=0, keepdims=True)
            return a0, a1, a2

        zero = jnp.zeros((1, LANE), F32)
        a0, a1, a2 = lax.fori_loop(0, T // R, step, (zero, zero, zero))
        dw_ref[0:1, :] = a0
        dw_ref[1:2, :] = a1
        dw_ref[2:3, :] = a2

    group = _group_spec(big, CONV_BASE, CONV_GROUP, T, lambda j: (0, j))
    w = pl.BlockSpec((3, LANE), lambda j: (0, j))
    return pl.pallas_call(
        body, name=name, grid=(BRANCH_W // LANE,),
        in_specs=[group, pl.BlockSpec((T, LANE), lambda j: (0, j)), w, pl.BlockSpec(memory_space=pl.ANY)],
        out_specs=[group, w],
        out_shape=[_sds(dbig.shape, BF), _sds((3, BRANCH_W), F32)],
        scratch_shapes=[pltpu.VMEM((T + 8, LANE), F32), pltpu.VMEM((T + 8, LANE), F32)],
        input_output_aliases={3: 0}, compiler_params=_cp((PAR,)),
    )(big, dy, cw, dbig)


def _ret_tables(T):
    L = min(RET_L, T)
    hh = jnp.arange(H_RET, dtype=F32)
    lg = jnp.log1p(-jnp.exp2(-5.0 - hh))
    n = jnp.arange(L, dtype=F32)
    a = jnp.exp(lg[:, None] * (n + 1.0))
    b = jnp.exp(lg[:, None] * (L - 1.0 - n))
    gl = jnp.exp(lg * L)
    ch = jnp.arange(L) // CHUNK
    m = jnp.exp(lg[:, None, None] * jnp.abs(n[:, None] - n[None, :])) * (ch[None, :] <= ch[:, None]).astype(F32)
    inv_freq = ROPE_BASE ** (-jnp.linspace(0.0, 1.0, DK_RET // 2, dtype=F32))
    ang = jnp.arange(T, dtype=F32)[:, None] * inv_freq[None, :]
    cos, sin = jnp.cos(ang), jnp.sin(ang)
    return dict(
        L=L, M=m,
        a=jnp.broadcast_to(a[:, :, None], (H_RET, L, DK_RET)),
        b=jnp.broadcast_to(b[:, :, None], (H_RET, L, DK_RET)),
        gl=jnp.broadcast_to(gl[:, None, None], (H_RET, 1, DK_RET)),
        cos=jnp.concatenate([cos, cos], axis=-1), sin=jnp.concatenate([-sin, sin], axis=-1))


def _rot(x, cs, sn):
    return x * cs + pltpu.roll(x, DK_RET // 2, 1) * sn


def _unrot(dy, cs, sn):
    return dy * cs + pltpu.roll(dy * sn, DK_RET // 2, 1)


def _ret_fwd(big, tb, name, comm=None):
    T = big.shape[0]
    L = tb["L"]
    nsc = T // L
    scale = DK_RET ** -0.5

    def body(x_ref, cos_ref, sin_ref, m_ref, a_ref, b_ref, gl_ref, y_ref, o_ref, st_ref, s_s):
        @pl.when(pl.program_id(1) == 0)
        def _():
            s_s[...] = jnp.zeros_like(s_s)

        cs, sn = cos_ref[...], sin_ref[...]
        qt = _rot(x_ref[:, RQ].astype(F32), cs, sn) * scale
        kt = _rot(x_ref[:, RK].astype(F32), cs, sn)
        qb, kb, vb = qt.astype(BF), kt.astype(BF), x_ref[:, RV]
        s_prev = s_s[...]
        st_ref[...] = s_prev
        p = lax.dot_general(qb, kb, NT_DIMS, preferred_element_type=F32) * m_ref[...]
        o = (jnp.dot(p.astype(BF), vb, preferred_element_type=F32)
             + jnp.dot((qt * a_ref[...]).astype(BF), s_prev.astype(BF), preferred_element_type=F32))
        s_s[...] = s_prev * gl_ref[...] + lax.dot_general((kt * b_ref[...]).astype(BF), vb, TN_DIMS,
                                                         preferred_element_type=F32)
        o_ref[...] = o
        gv = x_ref[:, RG].astype(F32)
        y_ref[...] = (gv * _sigmoid(gv) * o * _rms_r(o)).astype(BF)

    tab = pl.BlockSpec((L, DK_RET), lambda h, i: (i, 0))
    per_head = pl.BlockSpec((None, L, DK_RET), lambda h, i: (h, 0, 0))
    out = pl.BlockSpec((L, LANE), lambda h, i: (i, h))
    return _call(
        body, name=name, grid=(H_RET, nsc), comm=comm,
        args=(big, tb["cos"], tb["sin"], tb["M"], tb["a"], tb["b"], tb["gl"]),
        in_specs=[_group_spec(big, RET_BASE, RET_GROUP, L, lambda h, i: (i, h)), tab, tab,
                  pl.BlockSpec((None, L, L), lambda h, i: (h, 0, 0)), per_head, per_head,
                  pl.BlockSpec((None, 1, DK_RET), lambda h, i: (h, 0, 0))],
        out_specs=[out, out, pl.BlockSpec((None, None, DK_RET, DK_RET), lambda h, i: (i, h, 0, 0))],
        out_shape=[_sds((T, BRANCH_W), BF), _sds((T, BRANCH_W), F32), _sds((nsc, H_RET, DK_RET, DK_RET), F32)],
        scratch_shapes=[pltpu.VMEM((DK_RET, DK_RET), F32)],
        sem=(ARB, ARB))


def _ret_bwd(big, o, st, dy, tb, dbig, name):
    T = big.shape[0]
    L = tb["L"]
    nsc = T // L
    scale = DK_RET ** -0.5

    def body(x_ref, cos_ref, sin_ref, m_ref, a_ref, b_ref, gl_ref, o_ref, st_ref, dy_ref, _, d_ref, ds_s):
        @pl.when(pl.program_id(1) == 0)
        def _():
            ds_s[...] = jnp.zeros_like(ds_s)

        cs, sn = cos_ref[...], sin_ref[...]
        mm, av, bv = m_ref[...], a_ref[...], b_ref[...]
        qt = _rot(x_ref[:, RQ].astype(F32), cs, sn) * scale
        kt = _rot(x_ref[:, RK].astype(F32), cs, sn)
        qb, kb, vb = qt.astype(BF), kt.astype(BF), x_ref[:, RV]
        pb = (lax.dot_general(qb, kb, NT_DIMS, preferred_element_type=F32) * mm).astype(BF)
        ov = o_ref[...]
        r = _rms_r(ov)
        oh = ov * r
        gv = x_ref[:, RG].astype(F32)
        sg = _sigmoid(gv)
        dyv = dy_ref[...].astype(F32)
        d_ref[:, RG] = (dyv * oh * (sg * (1.0 + gv * (1.0 - sg)))).astype(BF)
        doh = dyv * gv * sg
        dob = (r * (doh - oh * jnp.mean(doh * oh, axis=-1, keepdims=True))).astype(BF)
        dsb = ds_s[...].astype(BF)
        spb = st_ref[...].astype(BF)
        dpb = (lax.dot_general(dob, vb, NT_DIMS, preferred_element_type=F32) * mm).astype(BF)
        dqt = (jnp.dot(dpb, kb, preferred_element_type=F32)
               + lax.dot_general(dob, spb, NT_DIMS, preferred_element_type=F32) * av)
        dkt = (lax.dot_general(dpb, qb, TN_DIMS, preferred_element_type=F32)
               + lax.dot_general(vb, dsb, NT_DIMS, preferred_element_type=F32) * bv)
        dv = (lax.dot_general(pb, dob, TN_DIMS, preferred_element_type=F32)
              + jnp.dot((kt * bv).astype(BF), dsb, preferred_element_type=F32))
        ds_s[...] = ds_s[...] * gl_ref[...] + lax.dot_general((qt * av).astype(BF), dob, TN_DIMS,
                                                              preferred_element_type=F32)
        d_ref[:, RQ] = (_unrot(dqt, cs, sn) * scale).astype(BF)
        d_ref[:, RK] = _unrot(dkt, cs, sn).astype(BF)
        d_ref[:, RV] = dv.astype(BF)

    def rev(i):
        return nsc - 1 - i

    group = _group_spec(big, RET_BASE, RET_GROUP, L, lambda h, i: (rev(i), h))
    tab = pl.BlockSpec((L, DK_RET), lambda h, i: (rev(i), 0))
    per_head = pl.BlockSpec((None, L, DK_RET), lambda h, i: (h, 0, 0))
    out = pl.BlockSpec((L, LANE), lambda h, i: (rev(i), h))
    return pl.pallas_call(
        body, name=name, grid=(H_RET, nsc),
        in_specs=[group, tab, tab,
                  pl.BlockSpec((None, L, L), lambda h, i: (h, 0, 0)), per_head, per_head,
                  pl.BlockSpec((None, 1, DK_RET), lambda h, i: (h, 0, 0)),
                  out, pl.BlockSpec((None, None, DK_RET, DK_RET), lambda h, i: (rev(i), h, 0, 0)), out,
                  pl.BlockSpec(memory_space=pl.ANY)],
        out_specs=group,
        out_shape=_sds(dbig.shape, BF),
        scratch_shapes=[pltpu.VMEM((DK_RET, DK_RET), F32)],
        input_output_aliases={10: 0}, compiler_params=_cp((PAR, ARB)),
    )(big, tb["cos"], tb["sin"], tb["M"], tb["a"], tb["b"], tb["gl"], o, st, dy, dbig)


def _relbias_onehot(n):
    mm = lax.broadcasted_iota(jnp.int32, (RB_PAD, ATT_TOEP), 1)
    rr = lax.broadcasted_iota(jnp.int32, (RB_PAD, ATT_TOEP), 0)
    idx = jnp.clip(n + ATT_TOEP - mm, 0, 2 * REL_CLIP)
    return (rr == idx).astype(F32)


def _relbias_expand(rbp, name):
    far = ATT_SPAN - ATT_TOEP

    def body(rb_ref, o_ref):
        rb = rb_ref[...]
        const = jnp.broadcast_to(rb[:, 2 * REL_CLIP:2 * REL_CLIP + 1], (H_ATT, far))

        def row(n, c):
            toep = jnp.dot(rb, _relbias_onehot(n), preferred_element_type=F32, precision=lax.Precision.HIGHEST)
            m = lax.broadcasted_iota(jnp.int32, (1, ATT_SPAN), 1)
            d = n // CHUNK + N_PREV - m // CHUNK
            neg = jnp.where((d >= 0) & (d <= N_PREV), 0.0, NEG_INF).astype(F32)
            o_ref[n] = jnp.concatenate([const, toep], axis=1) + neg
            return c

        lax.fori_loop(0, ATT_TQ, row, 0)

    return pl.pallas_call(
        body, name=name,
        in_specs=[pl.BlockSpec(memory_space=pltpu.VMEM)],
        out_specs=pl.BlockSpec(memory_space=pltpu.VMEM),
        out_shape=_sds((ATT_TQ, H_ATT, ATT_SPAN), F32),
    )(rbp)


def _relbias_grad(dbt, name):
    far = ATT_SPAN - ATT_TOEP

    def body(d_ref, o_ref):
        def row(n, carry):
            acc, cs = carry
            dn = d_ref[n]
            acc = acc + lax.dot_general(dn[:, far:], _relbias_onehot(n), NT_DIMS, preferred_element_type=F32,
                                        precision=lax.Precision.HIGHEST)
            cs = cs + jnp.sum(dn[:, :far], axis=1, keepdims=True)
            return acc, cs

        acc, cs = lax.fori_loop(0, ATT_TQ, row, (jnp.zeros((H_ATT, RB_PAD), F32), jnp.zeros((H_ATT, 1), F32)))
        rr = lax.broadcasted_iota(jnp.int32, (H_ATT, RB_PAD), 1)
        o_ref[...] = acc + jnp.where(rr == 2 * REL_CLIP, cs, 0.0)

    return pl.pallas_call(
        body, name=name,
        in_specs=[pl.BlockSpec(memory_space=pltpu.VMEM)],
        out_specs=pl.BlockSpec(memory_space=pltpu.VMEM),
        out_shape=_sds((H_ATT, RB_PAD), F32),
    )(dbt)


def _att_pad_fill(dst_s, src_ref, cols, T):
    dst_s[pl.ds(0, ATT_PAD), :] = jnp.zeros((ATT_PAD, LANE), dst_s.dtype)
    R = min(512, T)

    def cp(t, c):
        dst_s[pl.ds(pl.multiple_of(ATT_PAD + t * R, LANE), R), :] = src_ref[pl.ds(pl.multiple_of(t * R, R), R), cols]
        return c

    lax.fori_loop(0, T // R, cp, 0)


ATT_WIN = ATT_SUB * ATT_TQ + ATT_PAD


def _att_probs(s_full, sub, bias, t0):
    s = s_full[sub * ATT_TQ:(sub + 1) * ATT_TQ, sub * ATT_TQ:sub * ATT_TQ + ATT_SPAN] * (DH_ATT ** -0.5) + bias
    key_pos = t0 + sub * ATT_TQ - ATT_PAD + lax.broadcasted_iota(jnp.int32, (1, ATT_SPAN), 1)
    s = jnp.where(key_pos >= 0, s, NEG_INF)
    p = jnp.exp(s - jnp.max(s, axis=-1, keepdims=True))
    return p * (1.0 / jnp.sum(p, axis=-1, keepdims=True))


def _att_band(tiles):
    rows = []
    for sub, t in enumerate(tiles):
        parts = []
        if sub:
            parts.append(jnp.zeros((ATT_TQ, sub * ATT_TQ), BF))
        parts.append(t)
        if sub < ATT_SUB - 1:
            parts.append(jnp.zeros((ATT_TQ, (ATT_SUB - 1 - sub) * ATT_TQ), BF))
        rows.append(jnp.concatenate(parts, axis=1))
    return jnp.concatenate(rows, axis=0)


def _att_head_masks(x):
    first = lax.broadcasted_iota(jnp.int32, (1, LANE), 1) < DH_ATT
    zero = jnp.zeros_like(x)
    return first, (jnp.where(first, x, zero), jnp.where(first, zero, x))


def _att_fwd(big, bias, name, comm=None):
    T = big.shape[0]
    rows = ATT_SUB * ATT_TQ
    nt = T // rows

    def body(x_ref, b_ref, y_ref, kp_s, vp_s):
        i = pl.program_id(1)

        @pl.when(i == 0)
        def _():
            _att_pad_fill(kp_s, x_ref, AK, T)
            _att_pad_fill(vp_s, x_ref, AV, T)

        t0 = pl.multiple_of(i * rows, rows)
        kw = kp_s[pl.ds(t0, ATT_WIN), :]
        vw = vp_s[pl.ds(t0, ATT_WIN), :]
        first, qm = _att_head_masks(x_ref[pl.ds(t0, rows), AQ])
        outs = []
        for hh in range(2):
            s_full = lax.dot_general(qm[hh], kw, NT_DIMS, preferred_element_type=F32)
            band = _att_band([_att_probs(s_full, sub, b_ref[hh], t0).astype(BF) for sub in range(ATT_SUB)])
            outs.append(jnp.dot(band, vw, preferred_element_type=F32))
        y_ref[...] = jnp.where(first, outs[0], outs[1]).astype(BF)

    return _call(
        body, name=name, grid=(H_ATT // 2, nt), args=(big, bias), comm=comm,
        in_specs=[_group_spec(big, ATT_BASE, ATT_GROUP, T, lambda p, i: (0, p)),
                  pl.BlockSpec((2, ATT_TQ, ATT_SPAN), lambda p, i: (p, 0, 0))],
        out_specs=[pl.BlockSpec((rows, LANE), lambda p, i: (i, p))],
        out_shape=[_sds((T, BRANCH_W), BF)],
        scratch_shapes=[pltpu.VMEM((T + ATT_PAD, LANE), BF), pltpu.VMEM((T + ATT_PAD, LANE), BF)],
        sem=(ARB, ARB))


def _att_bwd(big, bias, dy, dbig, name, comm=None):
    T = big.shape[0]
    rows = ATT_SUB * ATT_TQ
    nt = T // rows
    scale = DH_ATT ** -0.5

    def body(x_ref, b_ref, dy_ref, _, d_ref, db_ref, kp_s, vp_s, dk_s, dv_s):
        i = pl.program_id(1)

        @pl.when(i == 0)
        def _():
            _att_pad_fill(kp_s, x_ref, AK, T)
            _att_pad_fill(vp_s, x_ref, AV, T)
            dk_s[...] = jnp.zeros_like(dk_s)
            dv_s[...] = jnp.zeros_like(dv_s)
            db_ref[...] = jnp.zeros_like(db_ref)

        t0 = pl.multiple_of(i * rows, rows)
        win = pl.ds(t0, ATT_WIN)
        kw = kp_s[win, :]
        vw = vp_s[win, :]
        first, qm = _att_head_masks(x_ref[pl.ds(t0, rows), AQ])
        _, dom = _att_head_masks(dy_ref[...])
        dqs, dkt, dvt = [], None, None
        for hh in range(2):
            s_full = lax.dot_general(qm[hh], kw, NT_DIMS, preferred_element_type=F32)
            dp_full = lax.dot_general(dom[hh], vw, NT_DIMS, preferred_element_type=F32)
            ps, dss, db = [], [], None
            for sub in range(ATT_SUB):
                pn = _att_probs(s_full, sub, b_ref[hh], t0)
                dp = dp_full[sub * ATT_TQ:(sub + 1) * ATT_TQ, sub * ATT_TQ:sub * ATT_TQ + ATT_SPAN]
                ds = pn * (dp - jnp.sum(dp * pn, axis=-1, keepdims=True))
                db = ds if db is None else db + ds
                ps.append(pn.astype(BF))
                dss.append(ds.astype(BF))
            db_ref[hh] += db
            ds_band, p_band = _att_band(dss), _att_band(ps)
            dqs.append(jnp.dot(ds_band, kw, preferred_element_type=F32))
            qt = jnp.transpose(qm[hh].astype(F32)).astype(BF)
            dot_ = jnp.transpose(dom[hh].astype(F32)).astype(BF)
            dk_h = jnp.dot(qt, ds_band, preferred_element_type=F32)
            dv_h = jnp.dot(dot_, p_band, preferred_element_type=F32)
            dkt = dk_h if dkt is None else dkt + dk_h
            dvt = dv_h if dvt is None else dvt + dv_h
        d_ref[pl.ds(t0, rows), AQ] = (jnp.where(first, dqs[0], dqs[1]) * scale).astype(BF)
        dk_s[win, :] += jnp.transpose(dkt) * scale
        dv_s[win, :] += jnp.transpose(dvt)

        @pl.when(i == nt - 1)
        def _():
            R = min(512, T)

            def cp(t, c):
                src = pl.ds(pl.multiple_of(ATT_PAD + t * R, LANE), R)
                dst = pl.ds(pl.multiple_of(t * R, R), R)
                d_ref[dst, AK] = dk_s[src, :].astype(BF)
                d_ref[dst, AV] = dv_s[src, :].astype(BF)
                return c

            lax.fori_loop(0, T // R, cp, 0)

    group = _group_spec(big, ATT_BASE, ATT_GROUP, T, lambda p, i: (0, p))
    tile = pl.BlockSpec((rows, LANE), lambda p, i: (i, p))
    bspec = pl.BlockSpec((2, ATT_TQ, ATT_SPAN), lambda p, i: (p, 0, 0))
    return _call(
        body, name=name, grid=(H_ATT // 2, nt), args=(big, bias, dy, dbig), comm=comm, aliases={3: 0}, vmem_mb=56,
        in_specs=[group, bspec, tile, pl.BlockSpec(memory_space=pl.ANY)],
        out_specs=[group, bspec],
        out_shape=[_sds(dbig.shape, BF), _sds((H_ATT, ATT_TQ, ATT_SPAN), F32)],
        scratch_shapes=[pltpu.VMEM((T + ATT_PAD, LANE), BF), pltpu.VMEM((T + ATT_PAD, LANE), BF),
                        pltpu.VMEM((T + ATT_PAD, LANE), F32), pltpu.VMEM((T + ATT_PAD, LANE), F32)],
        sem=(ARB, ARB))


def _merge_fwd(x1, big, ys, wb, wo, name):
    T, D = x1.shape
    tm = min(TM, T)

    def body(x_ref, gp_ref, yc_ref, yr_ref, ya_ref, wb_ref, wo_ref, x2_ref, p_ref, mg_ref):
        merged = jnp.zeros((tm, D), F32)
        for i, y_ref in enumerate((yc_ref, yr_ref, ya_ref)):
            cols = slice(i * D, (i + 1) * D)
            pb = jnp.dot(y_ref[...], wb_ref[i], preferred_element_type=F32).astype(BF)
            p_ref[:, cols] = pb
            merged = merged + _sigmoid(gp_ref[:, cols].astype(F32)) * pb.astype(F32)
        mb = merged.astype(BF)
        mg_ref[...] = mb
        x2_ref[...] = x_ref[...] + jnp.dot(mb, wo_ref[...], preferred_element_type=F32)

    tok = pl.BlockSpec((tm, D), lambda i: (i, 0))
    wide = pl.BlockSpec((tm, 3 * D), lambda i: (i, 0))
    yspec = pl.BlockSpec((tm, BRANCH_W), lambda i: (i, 0))
    return pl.pallas_call(
        body, name=name, grid=(T // tm,),
        in_specs=[tok, wide, yspec, yspec, yspec,
                  pl.BlockSpec((3, BRANCH_W, D), lambda i: (0, 0, 0)),
                  pl.BlockSpec((D, D), lambda i: (0, 0))],
        out_specs=[tok, wide, tok],
        out_shape=[_sds((T, D), F32), _sds((T, 3 * D), BF), _sds((T, D), BF)],
        compiler_params=_cp((PAR,)),
    )(x1, big, *ys, wb, wo)


def _merge_bwd(dx2, big, p, wb, wo, name):
    T, D = dx2.shape
    tm = min(TM, T)

    def body(dx_ref, gp_ref, p_ref, wb_ref, wo_ref, dp_ref, dgp_ref, dyc_ref, dyr_ref, dya_ref, dxb_ref):
        dxb = dx_ref[...].astype(BF)
        dxb_ref[...] = dxb
        dm = lax.dot_general(dxb, wo_ref[...], NT_DIMS, preferred_element_type=F32)
        for i, dy_ref in enumerate((dyc_ref, dyr_ref, dya_ref)):
            cols = slice(i * D, (i + 1) * D)
            gt = _sigmoid(gp_ref[:, cols].astype(F32))
            dpb = (dm * gt).astype(BF)
            dp_ref[:, cols] = dpb
            dgp_ref[:, cols] = (dm * p_ref[:, cols].astype(F32) * gt * (1.0 - gt)).astype(BF)
            dy_ref[...] = lax.dot_general(dpb, wb_ref[i], NT_DIMS, preferred_element_type=F32).astype(BF)

    tok = pl.BlockSpec((tm, D), lambda i: (i, 0))
    wide = pl.BlockSpec((tm, 3 * D), lambda i: (i, 0))
    yspec = pl.BlockSpec((tm, BRANCH_W), lambda i: (i, 0))
    return pl.pallas_call(
        body, name=name, grid=(T // tm,),
        in_specs=[tok, wide, wide,
                  pl.BlockSpec((3, BRANCH_W, D), lambda i: (0, 0, 0)),
                  pl.BlockSpec((D, D), lambda i: (0, 0))],
        out_specs=[wide, wide, yspec, yspec, yspec, tok],
        out_shape=[_sds((T, 3 * D), BF), _sds(big.shape, BF)] + [_sds((T, BRANCH_W), BF)] * 3 + [_sds((T, D), BF)],
        compiler_params=_cp((PAR,)),
    )(dx2, big, p, wb, wo)


def _loss_head(x, tgt, fw, name):
    T, D = x.shape
    tm = min(TM, T)

    def body(x_ref, t_ref, w_ref, loss_ref, dx_ref, dw_ref):
        @pl.when(pl.program_id(0) == 0)
        def _():
            loss_ref[...] = jnp.zeros_like(loss_ref)
            dw_ref[...] = jnp.zeros_like(dw_ref)

        xv = x_ref[...]
        wv = w_ref[...]
        e = xv * _rms_r(xv) * wv - t_ref[...]
        loss_ref[...] += 0.5 * jnp.sum(jnp.mean(e * e, axis=-1, keepdims=True))
        dx, dn = _rms_bwd(e * (1.0 / D), xv, wv)
        dx_ref[...] = dx
        dw_ref[...] += dn

    tok = pl.BlockSpec((tm, D), lambda i: (i, 0))
    return pl.pallas_call(
        body, name=name, grid=(T // tm,),
        in_specs=[tok, tok, pl.BlockSpec((1, D), lambda i: (0, 0))],
        out_specs=[pl.BlockSpec((8, LANE), lambda i: (0, 0)), tok, pl.BlockSpec((1, D), lambda i: (0, 0))],
        out_shape=[_sds((8, LANE), F32), _sds((T, D), F32), _sds((1, D), F32)],
        compiler_params=_cp((ARB,)),
    )(x, tgt, fw)


def _block_rows(rows, cols):
    cap = max(8, (1 << 18) // cols)
    best = None
    for r in range(8, rows + 1, 8):
        if rows % r == 0 and r <= cap:
            best = r
    return best if best is not None else rows


def _sum4(land, l, n_layers, name, prev=None):
    _, rows, cols = land.shape
    br = _block_rows(rows, cols)

    def body(*refs):
        l_ref, o_ref = refs[0], refs[-1]
        o_ref[...] = ((l_ref[3].astype(F32) + l_ref[0].astype(F32)) + l_ref[1].astype(F32)) + l_ref[2].astype(F32)

    in_specs = [pl.BlockSpec((4, br, cols), lambda i: (0, i, 0))]
    args = [land]
    aliases = {}
    if prev is not None:
        in_specs.append(pl.BlockSpec(memory_space=pl.ANY))
        args.append(prev)
        aliases = {1: 0}
    return pl.pallas_call(
        body, name=name, grid=(rows // br,), in_specs=in_specs,
        out_specs=pl.BlockSpec((None, br, cols), lambda i: (l, i, 0)),
        out_shape=_sds((n_layers, rows, cols), F32),
        input_output_aliases=aliases, compiler_params=_cp((PAR,)),
    )(*args)


def _adamw_math(w, g, m, v):
    m = ADAM_B1 * m + (1.0 - ADAM_B1) * g
    v = ADAM_B2 * v + (1.0 - ADAM_B2) * (g * g)
    m_hat = m / (1.0 - ADAM_B1 ** ADAM_STEP)
    v_hat = v / (1.0 - ADAM_B2 ** ADAM_STEP)
    delta = -ADAM_LR * (m_hat / (jnp.sqrt(v_hat) + ADAM_EPS) + ADAM_WD * w)
    return delta, m, v


def _adamw(w, ga, gb, m, v, name):
    rows, cols = w.shape
    br = _block_rows(rows, cols)
    two = gb is not None

    def body(*refs):
        if two:
            w_ref, ga_ref, gb_ref, m_ref, v_ref, g_ref, d_ref, nm_ref, nv_ref = refs
            g = ga_ref[...] + gb_ref[...]
        else:
            w_ref, ga_ref, m_ref, v_ref, g_ref, d_ref, nm_ref, nv_ref = refs
            g = ga_ref[...]
        d, nm, nv = _adamw_math(w_ref[...], g, m_ref[...], v_ref[...])
        g_ref[...] = g
        d_ref[...] = d
        nm_ref[...] = nm
        nv_ref[...] = nv

    blk = pl.BlockSpec((br, cols), lambda i: (i, 0))
    args = [w, ga] + ([gb] if two else []) + [m, v]
    return pl.pallas_call(
        body, name=name, grid=(rows // br,),
        in_specs=[blk] * len(args), out_specs=[blk] * 4,
        out_shape=[_sds((rows, cols), F32)] * 4,
        compiler_params=_cp((PAR,)),
    )(*args)


def _swap_cores(vs, name):
    n = len(vs)

    def body(*refs):
        v_refs, o_refs = refs[:n], refs[n:2 * n]
        ssem, rsem = refs[2 * n:]
        x, y, c = _place()
        copies = [pltpu.make_async_remote_copy(
            src_ref=v_refs[k], dst_ref=o_refs[k], send_sem=ssem.at[k], recv_sem=rsem.at[k],
            device_id=(x, y, 1 - c), device_id_type=MESH) for k in range(n)]
        for cp in copies:
            cp.start()
        for cp in copies:
            cp.wait()

    hbm = pl.BlockSpec(memory_space=pl.ANY)
    return pl.pallas_call(
        body, name=name,
        in_specs=[hbm] * n, out_specs=[hbm] * n,
        out_shape=[_sds(v.shape, v.dtype) for v in vs],
        scratch_shapes=[pltpu.SemaphoreType.DMA((n,)), pltpu.SemaphoreType.DMA((n,))],
    )(*vs)


def _allreduce_small(v, name):
    rows = v.shape[0]
    flips = [(fx, fy, fc) for fx in (0, 1) for fy in (0, 1) for fc in (0, 1) if fx or fy or fc]

    def body(v_ref, o_ref, all_s, ssem, rsem):
        x, y, c = _place()

        def peer(f):
            return (x + f[0] - 2 * x * f[0], y + f[1] - 2 * y * f[1], c + f[2] - 2 * c * f[2])

        def slot(p):
            return all_s.at[4 * p[0] + 2 * p[1] + p[2]]

        def copy(k, f, owner):
            return pltpu.make_async_remote_copy(
                src_ref=v_ref, dst_ref=slot(owner), send_sem=ssem.at[k], recv_sem=rsem.at[k],
                device_id=peer(f), device_id_type=MESH)

        sends = [copy(k, f, (x, y, c)) for k, f in enumerate(flips)]
        for cp in sends:
            cp.start()
        all_s[4 * x + 2 * y + c] = v_ref[...]
        for k, f in enumerate(flips):
            copy(k, f, peer(f)).wait_recv()
        for cp in sends:
            cp.wait_send()
        acc = all_s[0]
        for d in range(1, 8):
            acc = acc + all_s[d]
        o_ref[...] = acc

    return pl.pallas_call(
        body, name=name,
        in_specs=[pl.BlockSpec(memory_space=pltpu.VMEM)],
        out_specs=pl.BlockSpec(memory_space=pltpu.VMEM),
        out_shape=_sds((rows, LANE), F32),
        scratch_shapes=[pltpu.VMEM((8, rows, LANE), F32), pltpu.SemaphoreType.DMA((7,)), pltpu.SemaphoreType.DMA((7,))],
    )(v)


BIG_NAMES = ("ffn1_w_gate", "ffn1_w_up", "ffn1_w_down", "w_in", "w_branch", "w_merge_gate", "w_out",
             "ffn2_w_gate", "ffn2_w_up", "ffn2_w_down")


FFN1 = ("ffn1_w_gate", "ffn1_w_up", "ffn1_w_down")
FFN2 = ("ffn2_w_gate", "ffn2_w_up", "ffn2_w_down")
MIX_IN = ("w_in", "w_merge_gate")
MIX_OUT = ("w_branch", "w_out")


def _keys(names, l):
    return [(n, l) for n in names]


def _local_step(x, tgt, small, convw_full, wx, n_layers):
    T, D = x.shape
    L = n_layers
    ns = N_SHARD
    dq = D // ns
    W = wx.w

    def hosted(call, keys, scatter=False):
        comm = wx.pieces(keys, scatter)
        main, extra = call(comm)
        if comm is not None:
            wx.arrived(keys, extra, scatter)
        return main

    def mixer_views(l):
        g4 = W[("w_merge_gate", l)]
        gates = jnp.transpose(g4, (0, 2, 1, 3)).reshape(D, 3 * D)
        win = jnp.transpose(W[("w_in", l)], (1, 0, 2)).reshape(D, -1)
        return jnp.concatenate([gates, _permute_in_cols(win)], axis=-1)

    def out_views(l):
        wb = jnp.transpose(W[("w_branch", l)], (1, 2, 0, 3)).reshape(3, BRANCH_W, D)
        wo = W[("w_out", l)].reshape(D, D)
        return wb, wo

    tb = _ret_tables(T)
    rb_pad = jnp.pad(small["rel_bias"], ((0, 0), (0, 0), (0, RB_PAD - N_REL)))

    saved = []
    h = x
    for l in range(L):
        s = {"x0": h}
        nxt = l + 1
        x1, s["g1"], s["u1"] = hosted(
            lambda c: _ffn_fwd(h, small["ffn1_norm"][l][None], W[("ffn1_w_gate", l)], W[("ffn1_w_up", l)],
                               W[("ffn1_w_down", l)], f"ffn1_fwd_{l}", comm=c), _keys(MIX_IN, l))
        s["x1"] = x1
        s["wbig"] = mixer_views(l)
        big, s["h"] = _inproj_fwd(x1, small["mix_norm"][l][None], s["wbig"], f"inproj_fwd_{l}")
        s["big"] = big
        s["bias"] = jnp.transpose(_relbias_expand(rb_pad[l], f"relbias_expand_{l}"), (1, 0, 2))
        s["yc"] = _conv_fwd(big, convw_full[l], f"conv_fwd_{l}")
        s["yr"], s["o"], s["st"] = hosted(lambda c: _ret_fwd(big, tb, f"ret_fwd_{l}", comm=c), _keys(MIX_OUT, l))
        (s["ya"],) = hosted(lambda c: _att_fwd(big, s["bias"], f"att_fwd_{l}", comm=c), _keys(FFN2, l))
        s["wb"], s["wo"] = out_views(l)
        x2, s["p"], s["mg"] = _merge_fwd(x1, big, (s["yc"], s["yr"], s["ya"]), s["wb"], s["wo"], f"merge_fwd_{l}")
        s["x2"] = x2
        h, s["g2"], s["u2"] = hosted(
            lambda c: _ffn_fwd(x2, small["ffn2_norm"][l][None], W[("ffn2_w_gate", l)], W[("ffn2_w_up", l)],
                               W[("ffn2_w_down", l)], f"ffn2_fwd_{l}", comm=c), _keys(FFN1, nxt) if nxt < L else [])
        saved.append(s)

    loss_p, dx, d_final = _loss_head(h, tgt, small["final_norm"][None], "loss_head")

    gs = {"final_norm": d_final[0]}
    for k in ("ffn1_norm", "mix_norm", "ffn2_norm", "rel_bias", "conv_w"):
        gs[k] = [None] * L
    tk = min(2048, T)
    nk = T // tk

    def ffn_grads(pre, l, hb, dgv, duv, av, dacc, chain=False):
        fs = dgv.shape[-1]
        hspec = pl.BlockSpec((tk, D), lambda p, q, k: (k, 0))
        sspec = pl.BlockSpec((None, tk, fs), lambda p, q, k: (p, k, 0))
        up_spec = pl.BlockSpec((None, D, fs), lambda p, q, k: (p, 0, 0))
        down_spec = pl.BlockSpec((None, fs, D), lambda p, q, k: (p, 0, 0))
        jobs = [(pre + "_w_gate", hb, dgv, hspec, sspec, (ns, D, fs), up_spec),
                (pre + "_w_up", hb, duv, hspec, sspec, (ns, D, fs), up_spec),
                (pre + "_w_down", av, dacc, sspec, hspec, (ns, fs, D), down_spec)]
        before = None
        for nm, a, b, a_spec, b_spec, shape, o_spec in jobs:
            def product(c):
                r = _tn(a, b, a_spec, b_spec, _sds(shape, BF), o_spec, (ns, 1, nk), f"d{nm}_{l}", comm=c)
                return (r, []) if c is None else r
            wx.g[(nm, l)] = hosted(product, [before] if chain and before else [], scatter=True)
            before = (nm, l)

    for l in reversed(range(L)):
        s = saved[l]
        dx, dgv, duv, av, hb, dacc, dn = hosted(
            lambda c: _ffn_bwd(dx, s["x2"], small["ffn2_norm"][l][None], s["g2"], s["u2"], W[("ffn2_w_gate", l)],
                               W[("ffn2_w_up", l)], W[("ffn2_w_down", l)], f"ffn2_bwd_{l}", comm=c),
            _keys(FFN1, l + 1) if l + 1 < L else [], scatter=True)
        gs["ffn2_norm"][l] = dn[0]
        ffn_grads("ffn2", l, hb, dgv, duv, av, dacc)
        dp, dbig, dyc, dyr, dya, dxb = _merge_bwd(dx, s["big"], s["p"], s["wb"], s["wo"], f"merge_bwd_{l}")
        wx.g[("w_out", l)] = _tn(
            s["mg"], dxb, pl.BlockSpec((tk, dq), lambda p, q, k: (k, p)), pl.BlockSpec((tk, D), lambda p, q, k: (k, 0)),
            _sds((ns, dq, D), BF), pl.BlockSpec((None, dq, D), lambda p, q, k: (p, 0, 0)), (ns, 1, nk), f"dw_out_{l}")
        gb = None
        for i, yv in enumerate((s["yc"], s["yr"], s["ya"])):
            gb = _tn(yv, dp,
                     pl.BlockSpec((tk, BRANCH_W), lambda p, q, k: (k, 0)),
                     pl.BlockSpec((tk, dq), lambda p, q, k, i=i: (k, i * ns + p)),
                     _sds((ns, 3, BRANCH_W, dq), BF),
                     pl.BlockSpec((None, None, BRANCH_W, dq), lambda p, q, k, i=i: (p, i, 0, 0)),
                     (ns, 1, nk), f"dw_branch{i}_{l}", prev=gb)
        wx.g[("w_branch", l)] = gb
        dbig, dcw = _conv_bwd(s["big"], dyc, convw_full[l], dbig, f"conv_bwd_{l}")
        gs["conv_w"][l] = dcw
        dbig = _ret_bwd(s["big"], s["o"], s["st"], dyr, tb, dbig, f"ret_bwd_{l}")
        dbig, dbias = hosted(lambda c: _att_bwd(s["big"], s["bias"], dya, dbig, f"att_bwd_{l}", comm=c),
                             _keys(FFN2, l), scatter=True)
        gs["rel_bias"][l] = _relbias_grad(jnp.transpose(dbias, (1, 0, 2)), f"relbias_grad_{l}")[:, :N_REL]
        n_in = N_SEG * BRANCH_W
        bn = 1024 if (3 * D) % 1024 == 0 else BRANCH_W
        dwp = _tn(s["h"], dbig, pl.BlockSpec((tk, D), lambda p, q, k: (k, 0)),
                  pl.BlockSpec((tk, bn), lambda p, q, k: (k, 3 * D // bn + q)),
                  _sds((D, n_in), BF), pl.BlockSpec((D, bn), lambda p, q, k: (0, q)), (1, n_in // bn, nk), f"dw_in_{l}")
        wx.g[("w_in", l)] = jnp.transpose(_unpermute_in_cols(dwp).reshape(D, ns, n_in // ns), (1, 0, 2))
        wx.g[("w_merge_gate", l)] = _tn_gates(s["h"], dbig, ns, tk, f"dw_merge_gate_{l}")
        dx, dn = _inproj_bwd(dbig, s["wbig"], s["x1"], small["mix_norm"][l][None], dx, f"inproj_bwd_{l}")
        gs["mix_norm"][l] = dn[0]
        dx, dgv, duv, av, hb, dacc, dn = hosted(
            lambda c: _ffn_bwd(dx, s["x0"], small["ffn1_norm"][l][None], s["g1"], s["u1"], W[("ffn1_w_gate", l)],
                               W[("ffn1_w_up", l)], W[("ffn1_w_down", l)], f"ffn1_bwd_{l}", comm=c),
            _keys(MIX_IN + MIX_OUT, l), scatter=True)
        gs["ffn1_norm"][l] = dn[0]
        ffn_grads("ffn1", l, hb, dgv, duv, av, dacc, chain=(l == 0))

    for k in ("ffn1_norm", "mix_norm", "ffn2_norm", "rel_bias", "conv_w"):
        gs[k] = jnp.stack(gs[k])
    return loss_p, dx, gs


class _Exchange:
    def __init__(self, shards):
        self.shards = shards
        self.w = {}
        self.g = {}
        self.landed = {}

    def own(self, key):
        return self.shards[key[0]][key[1]].astype(BF)

    def pieces(self, keys, scatter):
        if not keys:
            return None
        return _Pieces([self.g[k] for k in keys] if scatter else [self.own(k) for k in keys], scatter)

    def arrived(self, keys, outs, scatter):
        for k, o in zip(keys, outs):
            (self.landed if scatter else self.w)[k] = o


W_NAMES = ("ffn1_norm", "ffn1_w_gate", "ffn1_w_up", "ffn1_w_down", "mix_norm", "w_in", "conv_w", "rel_bias", "w_branch",
           "w_merge_gate", "w_out", "ffn2_norm", "ffn2_w_gate", "ffn2_w_up", "ffn2_w_down", "final_norm")


def _as2d(a):
    return a.reshape(1, -1) if a.ndim == 1 else a.reshape(-1, a.shape[-1])


def kernel(x, ffn1_norm, ffn1_w_gate, ffn1_w_up, ffn1_w_down, mix_norm, w_in, conv_w, rel_bias, w_branch, w_merge_gate, w_out, ffn2_norm, ffn2_w_gate, ffn2_w_up, ffn2_w_down, final_norm, loss_target, m_ffn1_norm, m_ffn1_w_gate, m_ffn1_w_up, m_ffn1_w_down, m_mix_norm, m_w_in, m_conv_w, m_rel_bias, m_w_branch, m_w_merge_gate, m_w_out, m_ffn2_norm, m_ffn2_w_gate, m_ffn2_w_up, m_ffn2_w_down, m_final_norm, v_ffn1_norm, v_ffn1_w_gate, v_ffn1_w_up, v_ffn1_w_down, v_mix_norm, v_w_in, v_conv_w, v_rel_bias, v_w_branch, v_w_merge_gate, v_w_out, v_ffn2_norm, v_ffn2_w_gate, v_ffn2_w_up, v_ffn2_w_down, v_final_norm):
    given = dict(locals())
    w = {n: given[n] for n in W_NAMES}
    m = {n: given["m_" + n] for n in W_NAMES}
    v = {n: given["v_" + n] for n in W_NAMES}
    my_chip = 2 * lax.axis_index("x") + lax.axis_index("y")
    L = w_in.shape[0]

    wx = _Exchange({n: w[n] for n in BIG_NAMES})
    first = _keys(FFN1, 0)
    comm = _Pieces([wx.own(k) for k in first] + [conv_w], scatter=False)
    got = _comm_alone(comm, "gather_first")
    wx.arrived(first, got[:-1], False)
    convw_full = jnp.transpose(got[-1], (1, 2, 0, 3)).reshape(conv_w.shape[0], conv_w.shape[1], -1)

    small = {n: w[n] for n in ("ffn1_norm", "mix_norm", "ffn2_norm", "final_norm", "rel_bias")}
    loss_p, grad_x, gs = _local_step(x[0], loss_target[0], small, convw_full, wx, L)
    last = [(FFN1[-1], 0)]
    wx.arrived(last, _comm_alone(wx.pieces(last, True), "scatter_last"), True)

    sums = []
    for n in BIG_NAMES:
        acc = None
        for l in range(L):
            a = wx.landed[(n, l)]
            acc = _sum4(a.reshape(4, -1, a.shape[-1]), l, L, f"sum4_{n}_{l}", prev=acc)
        sums.append(acc.reshape(-1, acc.shape[-1]))
    others = _swap_cores(sums, "swap_cores")

    parts = [gs["ffn1_norm"].reshape(-1), gs["mix_norm"].reshape(-1), gs["ffn2_norm"].reshape(-1),
             gs["final_norm"].reshape(-1), gs["rel_bias"].reshape(-1), gs["conv_w"].reshape(-1), loss_p[0]]
    sizes = [p.shape[0] for p in parts]
    flat = jnp.concatenate(parts)
    rows = -(-flat.shape[0] // (8 * LANE)) * 8
    flat = jnp.pad(flat, (0, rows * LANE - flat.shape[0])).reshape(rows, LANE)
    red = _allreduce_small(flat, "allreduce_small").reshape(-1)
    offs = [0]
    for sz in sizes:
        offs.append(offs[-1] + sz)
    sm = {}
    for i, n in enumerate(("ffn1_norm", "mix_norm", "ffn2_norm", "final_norm", "rel_bias", "conv_w")):
        sm[n] = red[offs[i]:offs[i + 1]]
    loss = red[offs[6]]
    sm["conv_w"] = lax.dynamic_slice_in_dim(sm["conv_w"].reshape(conv_w.shape[0], conv_w.shape[1], -1),
                                            my_chip * conv_w.shape[2], conv_w.shape[2], axis=2)

    grads, deltas, new_m, new_v = {}, {}, {}, {}
    big_sum = dict(zip(BIG_NAMES, zip(sums, others)))
    for n in W_NAMES:
        shape = w[n].shape
        if n in big_sum:
            ga, gb = big_sum[n]
        else:
            ga, gb = _as2d(sm[n].reshape(shape)), None
        out = _adamw(_as2d(w[n]), ga, gb, _as2d(m[n]), _as2d(v[n]), f"adamw_{n}")
        grads[n], deltas[n], new_m[n], new_v[n] = (o.reshape(shape) for o in out)

    return (loss, grad_x[None], *[grads[n] for n in W_NAMES], *[deltas[n] for n in W_NAMES],
            *[new_m[n] for n in W_NAMES], *[new_v[n] for n in W_NAMES])
```

```python
import functools
import math

import jax
import jax.numpy as jnp
from jax import lax
from jax.experimental import pallas as pl
from jax.experimental.pallas import tpu as pltpu

F32 = jnp.float32
BF = jnp.bfloat16
MESH = pl.DeviceIdType.MESH
ARB = "arbitrary"
PAR = "parallel"

EPS = 1e-6
NEG_INF = -1e30
ROPE_BASE = 10000.0
CHUNK = 64
BRANCH_W = 512
H_RET = 4
DK_RET = 128
H_ATT = 8
DH_ATT = 64
N_PREV = 8
REL_CLIP = 128
N_REL = 2 * REL_CLIP + 1
N_SHARD = 4
LANE = 128
RET_L = 512
ATT_TQ = 128
ATT_SUB = 4
ATT_PAD = N_PREV * CHUNK
ATT_SPAN = ATT_TQ + ATT_PAD
ATT_TOEP = 2 * REL_CLIP
RB_PAD = 264
TM = 512
TM_FFN = 1024

ADAM_LR = 0.001
ADAM_B1 = 0.9
ADAM_B2 = 0.999
ADAM_EPS = 1e-08
ADAM_WD = 0.01
ADAM_STEP = 10

NT_DIMS = (((1,), (1,)), ((), ()))
TN_DIMS = (((0,), (0,)), ((), ()))


def _cp(sem, vmem_mb=48):
    return pltpu.CompilerParams(dimension_semantics=sem, vmem_limit_bytes=vmem_mb << 20)


def _sds(shape, dtype):
    return jax.ShapeDtypeStruct(tuple(shape), dtype)


def _rms_r(x):
    return lax.rsqrt(jnp.mean(x * x, axis=-1, keepdims=True) + EPS)


def _sigmoid(x):
    return 0.5 * jnp.tanh(0.5 * x) + 0.5


def _rms_bwd(dh, xv, nw):
    r = _rms_r(xv)
    xh = xv * r
    dxh = dh * nw
    dx = r * (dxh - xh * jnp.mean(dxh * xh, axis=-1, keepdims=True))
    return dx, jnp.sum(dh * xh, axis=0, keepdims=True)


def _place():
    return lax.axis_index("x"), lax.axis_index("y"), lax.axis_index("c")


def _other_chips(x, y):
    return [(1 - x, y), (x, 1 - y), (1 - x, 1 - y)]


class _Pieces:
    def __init__(self, srcs, scatter):
        self.srcs = list(srcs)
        self.scatter = scatter
        n = len(self.srcs)
        self.out_shape = [_sds(s.shape if scatter else (N_SHARD,) + s.shape, s.dtype) for s in self.srcs]
        self.scratch = [pltpu.SemaphoreType.DMA((n,)), pltpu.SemaphoreType.DMA((3, n)), pltpu.SemaphoreType.DMA((3, n))]

    def _copies(self, src, dst, sems, waiting):
        lsem, ssem, rsem = sems
        x, y, c = _place()
        mine = 2 * x + y
        n = len(src)

        def remote(j, k, chip, s_ref, d_ref):
            return pltpu.make_async_remote_copy(
                src_ref=s_ref, dst_ref=d_ref, send_sem=ssem.at[j, k], recv_sem=rsem.at[j, k],
                device_id=(chip[0], chip[1], c), device_id_type=MESH)

        chips = list(enumerate(_other_chips(x, y)))
        if self.scatter:
            local = [pltpu.make_async_copy(src[k].at[mine], dst[k].at[3], lsem.at[k]) for k in range(n)]
            sends = [remote(j, k, ch, src[k].at[2 * ch[0] + ch[1]], dst[k].at[j]) for j, ch in chips for k in range(n)]
            recvs = sends
        else:
            local = [pltpu.make_async_copy(src[k], dst[k].at[mine], lsem.at[k]) for k in range(n)]
            sends = [remote(j, k, ch, src[k], dst[k].at[mine]) for j, ch in chips for k in range(n)]
            recvs = [remote(j, k, ch, src[k], dst[k].at[2 * ch[0] + ch[1]]) for j, ch in chips for k in range(n)
                     ] if waiting else []
        return local, sends, recvs

    def start(self, src, dst, sems):
        local, sends, _ = self._copies(src, dst, sems, False)
        for cp in local + sends:
            cp.start()

    def wait(self, src, dst, sems):
        local, sends, recvs = self._copies(src, dst, sems, True)
        for cp in recvs:
            cp.wait_recv()
        for cp in sends:
            cp.wait_send()
        for cp in local:
            cp.wait()


def _call(body, *, name, args, in_specs, out_specs, out_shape, grid=(), scratch_shapes=(), sem=None, comm=None,
          aliases=None, vmem_mb=48):
    in_specs, out_specs, out_shape = list(in_specs), list(out_specs), list(out_shape)
    scratch, args = list(scratch_shapes), list(args)
    n_in, n_out, n_scr = len(in_specs), len(out_specs), len(scratch)
    if comm is None:
        def kernel_body(*refs):
            body(*refs)
    else:
        c_in, c_out = len(comm.srcs), len(comm.out_shape)

        def kernel_body(*refs):
            o0 = n_in + c_in
            s0 = o0 + n_out + c_out
            cin, cout, sems = refs[n_in:o0], refs[o0 + n_out:s0], refs[s0 + n_scr:]
            main = refs[:n_in] + refs[o0:o0 + n_out] + refs[s0:s0 + n_scr]
            if grid:
                ids = [pl.program_id(a) for a in range(len(grid))]
                first = functools.reduce(lambda p, q: p & q, [i == 0 for i in ids])
                last = functools.reduce(lambda p, q: p & q, [i == g - 1 for i, g in zip(ids, grid)])

                @pl.when(first)
                def _():
                    comm.start(cin, cout, sems)

                body(*main)

                @pl.when(last)
                def _():
                    comm.wait(cin, cout, sems)
            else:
                comm.start(cin, cout, sems)
                body(*main)
                comm.wait(cin, cout, sems)

        hbm = pl.BlockSpec(memory_space=pl.ANY)
        in_specs += [hbm] * c_in
        out_specs += [hbm] * c_out
        out_shape += comm.out_shape
        scratch += comm.scratch
        args += comm.srcs
    params = dict(vmem_limit_bytes=vmem_mb << 20)
    if grid:
        params["dimension_semantics"] = sem
    outs = pl.pallas_call(
        kernel_body, name=name, grid=grid, in_specs=in_specs, out_specs=out_specs, out_shape=out_shape,
        scratch_shapes=scratch, input_output_aliases=aliases or {}, compiler_params=pltpu.CompilerParams(**params),
    )(*args)
    return list(outs[:n_out]), list(outs[n_out:])


def _comm_alone(comm, name):
    return _call(lambda: None, name=name, args=[], in_specs=[], out_specs=[], out_shape=[], comm=comm)[1]


def _ffn_fwd(x, nw, wg, wu, wd, name, comm=None):
    T, D = x.shape
    ns, _, fs = wg.shape
    tm = min(TM_FFN, T)

    def body(x_ref, nw_ref, wg_ref, wu_ref, wd_ref, xo_ref, g_ref, u_ref, h_s, acc_s):
        j = pl.program_id(1)

        @pl.when(j == 0)
        def _():
            xv = x_ref[...]
            h_s[...] = (xv * _rms_r(xv) * nw_ref[...]).astype(BF)
            acc_s[...] = jnp.zeros_like(acc_s)

        h = h_s[...]
        gb = jnp.dot(h, wg_ref[...], preferred_element_type=F32).astype(BF)
        ub = jnp.dot(h, wu_ref[...], preferred_element_type=F32).astype(BF)
        g_ref[...] = gb
        u_ref[...] = ub
        g = gb.astype(F32)
        a = (g * _sigmoid(g) * ub.astype(F32)).astype(BF)
        acc_s[...] += jnp.dot(a, wd_ref[...], preferred_element_type=F32)

        @pl.when(j == ns - 1)
        def _():
            xo_ref[...] = x_ref[...] + 0.5 * acc_s[...]

    wspec = pl.BlockSpec((None, D, fs), lambda i, j: (j, 0, 0))
    return _call(
        body, name=name, grid=(T // tm, ns), args=(x, nw, wg, wu, wd), comm=comm, vmem_mb=56,
        in_specs=[pl.BlockSpec((tm, D), lambda i, j: (i, 0)),
                  pl.BlockSpec((1, D), lambda i, j: (0, 0)),
                  wspec, wspec,
                  pl.BlockSpec((None, fs, D), lambda i, j: (j, 0, 0))],
        out_specs=[pl.BlockSpec((tm, D), lambda i, j: (i, 0)),
                   pl.BlockSpec((None, tm, fs), lambda i, j: (j, i, 0)),
                   pl.BlockSpec((None, tm, fs), lambda i, j: (j, i, 0))],
        out_shape=[_sds((T, D), F32), _sds((ns, T, fs), BF), _sds((ns, T, fs), BF)],
        scratch_shapes=[pltpu.VMEM((tm, D), BF), pltpu.VMEM((tm, D), F32)],
        sem=(ARB, ARB))


def _ffn_bwd_hidden(dxo, x, nw, g, u, wd, name, comm=None):
    T, D = x.shape
    ns, fs, _ = wd.shape
    tm = min(TM_FFN, T)

    def body(dxo_ref, x_ref, nw_ref, g_ref, u_ref, wd_ref, dg_ref, du_ref, a_ref, h_ref, dacc_ref, dacc_s):
        @pl.when(pl.program_id(1) == 0)
        def _():
            xv = x_ref[...]
            h_ref[...] = (xv * _rms_r(xv) * nw_ref[...]).astype(BF)
            db = (0.5 * dxo_ref[...]).astype(BF)
            dacc_ref[...] = db
            dacc_s[...] = db

        da = lax.dot_general(dacc_s[...], wd_ref[...], NT_DIMS, preferred_element_type=F32)
        gv = g_ref[...].astype(F32)
        uv = u_ref[...].astype(F32)
        s = _sigmoid(gv)
        sg = gv * s
        a_ref[...] = (sg * uv).astype(BF)
        du_ref[...] = (da * sg).astype(BF)
        dg_ref[...] = (da * uv * (s * (1.0 + gv * (1.0 - s)))).astype(BF)

    tok = pl.BlockSpec((tm, D), lambda i, j: (i, 0))
    hid = pl.BlockSpec((None, tm, fs), lambda i, j: (j, i, 0))
    return _call(
        body, name=name, grid=(T // tm, ns), args=(dxo, x, nw, g, u, wd), comm=comm, vmem_mb=56,
        in_specs=[tok, tok, pl.BlockSpec((1, D), lambda i, j: (0, 0)), hid, hid,
                  pl.BlockSpec((None, fs, D), lambda i, j: (j, 0, 0))],
        out_specs=[hid, hid, hid, tok, tok],
        out_shape=[_sds((ns, T, fs), BF)] * 3 + [_sds((T, D), BF)] * 2,
        scratch_shapes=[pltpu.VMEM((tm, D), BF)],
        sem=(ARB, ARB))


def _ffn_bwd_resid(dg, du, wg, wu, x, nw, dxo, name, comm=None):
    T, D = x.shape
    ns, _, fs = wg.shape
    tm = min(TM_FFN, T)

    def body(dg_ref, du_ref, wg_ref, wu_ref, x_ref, nw_ref, dxo_ref, dx_ref, dnw_ref, acc_s):
        i = pl.program_id(0)
        j = pl.program_id(1)
        prod = (lax.dot_general(dg_ref[...], wg_ref[...], NT_DIMS, preferred_element_type=F32)
                + lax.dot_general(du_ref[...], wu_ref[...], NT_DIMS, preferred_element_type=F32))

        @pl.when((i == 0) & (j == 0))
        def _():
            dnw_ref[...] = jnp.zeros_like(dnw_ref)

        @pl.when(j == 0)
        def _():
            acc_s[...] = prod

        @pl.when(j > 0)
        def _():
            acc_s[...] += prod

        @pl.when(j == ns - 1)
        def _():
            dx, dn = _rms_bwd(acc_s[...], x_ref[...], nw_ref[...])
            dx_ref[...] = dxo_ref[...] + dx
            dnw_ref[...] += dn

    tok = pl.BlockSpec((tm, D), lambda i, j: (i, 0))
    row = pl.BlockSpec((1, D), lambda i, j: (0, 0))
    hid = pl.BlockSpec((None, tm, fs), lambda i, j: (j, i, 0))
    wspec = pl.BlockSpec((None, D, fs), lambda i, j: (j, 0, 0))
    return _call(
        body, name=name, grid=(T // tm, ns), args=(dg, du, wg, wu, x, nw, dxo), comm=comm, vmem_mb=56,
        in_specs=[hid, hid, wspec, wspec, tok, row, tok],
        out_specs=[tok, row],
        out_shape=[_sds((T, D), F32), _sds((1, D), F32)],
        scratch_shapes=[pltpu.VMEM((tm, D), F32)],
        sem=(ARB, ARB))


def _tn(a, b, a_spec, b_spec, out_shape, out_spec, grid, name, prev=None, comm=None):
    nk = grid[-1]
    acc_shape = tuple(d for d in out_spec.block_shape if d is not None)

    def body(*refs):
        a_ref, b_ref = refs[0], refs[1]
        o_ref, acc = refs[-2], refs[-1]
        k = pl.program_id(2)
        prod = lax.dot_general(a_ref[...], b_ref[...], TN_DIMS, preferred_element_type=F32)

        @pl.when(k == 0)
        def _():
            acc[...] = prod

        @pl.when(k > 0)
        def _():
            acc[...] += prod

        @pl.when(k == nk - 1)
        def _():
            o_ref[...] = acc[...].astype(o_ref.dtype)

    in_specs = [a_spec, b_spec]
    args = [a, b]
    aliases = {}
    if prev is not None:
        in_specs.append(pl.BlockSpec(memory_space=pl.ANY))
        args.append(prev)
        aliases = {2: 0}
    main, extra = _call(
        body, name=name, grid=grid, args=args, in_specs=in_specs, out_specs=[out_spec], out_shape=[out_shape],
        scratch_shapes=[pltpu.VMEM(acc_shape, F32)], aliases=aliases, sem=(ARB, ARB, ARB), comm=comm)
    return main[0] if comm is None else (main[0], extra)


def _tn_gates(h, dbig, ns, tk, name):
    T, D = h.shape
    dq = D // ns
    nk = T // tk

    def body(a_ref, b_ref, o_ref, acc):
        k = pl.program_id(1)
        prod = lax.dot_general(a_ref[...], b_ref[...], TN_DIMS, preferred_element_type=F32)

        @pl.when(k == 0)
        def _():
            acc[...] = prod

        @pl.when(k > 0)
        def _():
            acc[...] += prod

        @pl.when(k == nk - 1)
        def _():
            for s in range(ns):
                o_ref[s] = acc[s * dq:(s + 1) * dq, :].astype(o_ref.dtype)

    return pl.pallas_call(
        body, name=name, grid=(3, nk),
        in_specs=[pl.BlockSpec((tk, D), lambda q, k: (k, 0)), pl.BlockSpec((tk, D), lambda q, k: (k, q))],
        out_specs=pl.BlockSpec((ns, None, dq, D), lambda q, k: (0, q, 0, 0)),
        out_shape=_sds((ns, 3, dq, D), BF),
        scratch_shapes=[pltpu.VMEM((D, D), F32)],
        compiler_params=_cp((PAR, ARB)),
    )(h, dbig)


def _inproj_fwd(x, nw, wbig, name):
    T, D = x.shape
    nb = wbig.shape[-1]
    tm = min(2 * TM, T)
    bn = min(2048, nb)

    def body(x_ref, nw_ref, w_ref, o_ref, h_ref, h_s):
        @pl.when(pl.program_id(1) == 0)
        def _():
            xv = x_ref[...]
            hb = (xv * _rms_r(xv) * nw_ref[...]).astype(BF)
            h_s[...] = hb
            h_ref[...] = hb

        o_ref[...] = jnp.dot(h_s[...], w_ref[...], preferred_element_type=F32).astype(BF)

    return pl.pallas_call(
        body, name=name, grid=(T // tm, nb // bn),
        in_specs=[pl.BlockSpec((tm, D), lambda i, n: (i, 0)),
                  pl.BlockSpec((1, D), lambda i, n: (0, 0)),
                  pl.BlockSpec((D, bn), lambda i, n: (0, n))],
        out_specs=[pl.BlockSpec((tm, bn), lambda i, n: (i, n)),
                   pl.BlockSpec((tm, D), lambda i, n: (i, 0))],
        out_shape=[_sds((T, nb), BF), _sds((T, D), BF)],
        scratch_shapes=[pltpu.VMEM((tm, D), BF)],
        compiler_params=_cp((PAR, ARB)),
    )(x, nw, wbig)


def _inproj_bwd(dbig, wbig, x, nw, dxin, name, comm=None):
    T, D = x.shape
    nb = wbig.shape[-1]
    tm = min(TM_FFN, T)
    tk = min(1024, nb)
    nk = nb // tk

    def body(a_ref, w_ref, x_ref, nw_ref, dxin_ref, dx_ref, dnw_ref, acc_s):
        i = pl.program_id(0)
        k = pl.program_id(1)
        prod = lax.dot_general(a_ref[...], w_ref[...], NT_DIMS, preferred_element_type=F32)

        @pl.when((i == 0) & (k == 0))
        def _():
            dnw_ref[...] = jnp.zeros_like(dnw_ref)

        @pl.when(k == 0)
        def _():
            acc_s[...] = prod

        @pl.when(k > 0)
        def _():
            acc_s[...] += prod

        @pl.when(k == nk - 1)
        def _():
            dx, dn = _rms_bwd(acc_s[...], x_ref[...], nw_ref[...])
            dx_ref[...] = dxin_ref[...] + dx
            dnw_ref[...] += dn

    tok = pl.BlockSpec((tm, D), lambda i, k: (i, 0))
    row = pl.BlockSpec((1, D), lambda i, k: (0, 0))
    return _call(
        body, name=name, grid=(T // tm, nk), args=(dbig, wbig, x, nw, dxin), comm=comm, vmem_mb=56,
        in_specs=[pl.BlockSpec((tm, tk), lambda i, k: (i, k)),
                  pl.BlockSpec((D, tk), lambda i, k: (0, k)),
                  tok, row, tok],
        out_specs=[tok, row],
        out_shape=[_sds((T, D), F32), _sds((1, D), F32)],
        scratch_shapes=[pltpu.VMEM((tm, D), F32)],
        sem=(ARB, ARB))


CONV_R = 512
CONV_BASE, CONV_GROUP = 0, 3
ATT_BASE, ATT_GROUP = 12, 3
RET_BASE, RET_GROUP = 24, 4
N_SEG = 10


def _permute_in_cols(w):
    lead = w.shape[:-1]
    w4 = w.reshape(lead + (N_SEG, BRANCH_W // LANE, LANE))

    def grouped(lo, hi):
        return jnp.swapaxes(w4[..., lo:hi, :, :], -3, -2).reshape(lead + (-1,))

    return jnp.concatenate([grouped(0, 3), grouped(7, 10), grouped(3, 7)], axis=-1)


def _unpermute_in_cols(w):
    lead = w.shape[:-1]
    nblk = BRANCH_W // LANE

    def segs(lo, n):
        part = w[..., lo * LANE:(lo + nblk * n) * LANE].reshape(lead + (nblk, n, LANE))
        return jnp.swapaxes(part, -3, -2)

    conv, att, ret = segs(CONV_BASE, 3), segs(ATT_BASE, 3), segs(RET_BASE, 4)
    return jnp.concatenate([conv, ret, att], axis=-3).reshape(lead + (-1,))


def _seg0(big):
    return (big.shape[1] - N_SEG * BRANCH_W) // LANE


def _group_spec(big, base, group, rows, where):
    first = (_seg0(big) + base) // group
    assert first * group == _seg0(big) + base

    def index(*ids):
        r, g = where(*ids)
        return r, first + g

    return pl.BlockSpec((rows, group * LANE), index)


CU, CB, CC = (slice(k * LANE, (k + 1) * LANE) for k in range(3))
AQ, AK, AV = CU, CB, CC
RQ, RK, RV, RG = (slice(k * LANE, (k + 1) * LANE) for k in range(4))


def _conv_fwd(big, cw, name):
    T = big.shape[0]
    R = min(CONV_R, T)

    def body(g_ref, w_ref, y_ref, z_s):
        z_s[pl.ds(0, 8), :] = jnp.zeros((8, LANE), F32)

        def fill(t, c):
            sl = pl.ds(pl.multiple_of(t * R, R), R)
            z_s[pl.ds(pl.multiple_of(t * R + 8, 8), R), :] = g_ref[sl, CC].astype(F32) * g_ref[sl, CU].astype(F32)
            return c

        lax.fori_loop(0, T // R, fill, 0)
        w0, w1, w2 = w_ref[0:1, :], w_ref[1:2, :], w_ref[2:3, :]

        def step(t, c):
            zz = z_s[pl.ds(pl.multiple_of(t * R, R), R + 8), :]
            z0 = zz[8:]
            z1 = pltpu.roll(zz, 1, 0)[8:]
            z2 = pltpu.roll(zz, 2, 0)[8:]
            sl = pl.ds(pl.multiple_of(t * R, R), R)
            y_ref[sl, :] = (g_ref[sl, CB].astype(F32) * (w2 * z0 + w1 * z1 + w0 * z2)).astype(BF)
            return c

        lax.fori_loop(0, T // R, step, 0)

    return pl.pallas_call(
        body, name=name, grid=(BRANCH_W // LANE,),
        in_specs=[_group_spec(big, CONV_BASE, CONV_GROUP, T, lambda j: (0, j)),
                  pl.BlockSpec((3, LANE), lambda j: (0, j))],
        out_specs=pl.BlockSpec((T, LANE), lambda j: (0, j)),
        out_shape=_sds((T, BRANCH_W), BF),
        scratch_shapes=[pltpu.VMEM((T + 8, LANE), F32)],
        compiler_params=_cp((PAR,)),
    )(big, cw)


def _conv_bwd(big, dy, cw, dbig, name):
    T = big.shape[0]
    R = min(CONV_R, T)

    def body(g_ref, dy_ref, w_ref, _, o_ref, dw_ref, z_s, d_s):
        z_s[pl.ds(0, 8), :] = jnp.zeros((8, LANE), F32)
        d_s[pl.ds(T, 8), :] = jnp.zeros((8, LANE), F32)

        def fill(t, c):
            sl = pl.ds(pl.multiple_of(t * R, R), R)
            z_s[pl.ds(pl.multiple_of(t * R + 8, 8), R), :] = g_ref[sl, CC].astype(F32) * g_ref[sl, CU].astype(F32)
            d_s[sl, :] = dy_ref[sl, :].astype(F32) * g_ref[sl, CB].astype(F32)
            return c

        lax.fori_loop(0, T // R, fill, 0)
        w0, w1, w2 = w_ref[0:1, :], w_ref[1:2, :], w_ref[2:3, :]

        def step(t, carry):
            a0, a1, a2 = carry
            zz = z_s[pl.ds(pl.multiple_of(t * R, R), R + 8), :]
            z0 = zz[8:]
            z1 = pltpu.roll(zz, 1, 0)[8:]
            z2 = pltpu.roll(zz, 2, 0)[8:]
            sl = pl.ds(pl.multiple_of(t * R, R), R)
            dyv = dy_ref[sl, :].astype(F32)
            o_ref[sl, CB] = (dyv * (w2 * z0 + w1 * z1 + w0 * z2)).astype(BF)
            dd = d_s[pl.ds(pl.multiple_of(t * R, R), R + 8), :]
            d0 = dd[:R]
            d1 = pltpu.roll(dd, R + 7, 0)[:R]
            d2 = pltpu.roll(dd, R + 6, 0)[:R]
            dz = w2 * d0 + w1 * d1 + w0 * d2
            o_ref[sl, CC] = (dz * g_ref[sl, CU].astype(F32)).astype(BF)
            o_ref[sl, CU] = (dz * g_ref[sl, CC].astype(F32)).astype(BF)
            a0 = a0 + jnp.sum(d0 * z2, axis=0, keepdims=True)
            a1 = a1 + jnp.sum(d0 * z1, axis=0, keepdims=True)
            a2 = a2 + jnp.sum(d0 * z0, axis=0, keepdims=True)
            return a0, a1, a2

        zero = jnp.zeros((1, LANE), F32)
        a0, a1, a2 = lax.fori_loop(0, T // R, step, (zero, zero, zero))
        dw_ref[0:1, :] = a0
        dw_ref[1:2, :] = a1
        dw_ref[2:3, :] = a2

    group = _group_spec(big, CONV_BASE, CONV_GROUP, T, lambda j: (0, j))
    w = pl.BlockSpec((3, LANE), lambda j: (0, j))
    return pl.pallas_call(
        body, name=name, grid=(BRANCH_W // LANE,),
        in_specs=[group, pl.BlockSpec((T, LANE), lambda j: (0, j)), w, pl.BlockSpec(memory_space=pl.ANY)],
        out_specs=[group, w],
        out_shape=[_sds(dbig.shape, BF), _sds((3, BRANCH_W), F32)],
        scratch_shapes=[pltpu.VMEM((T + 8, LANE), F32), pltpu.VMEM((T + 8, LANE), F32)],
        input_output_aliases={3: 0}, compiler_params=_cp((PAR,)),
    )(big, dy, cw, dbig)


def _ret_tables(T):
    L = min(RET_L, T)
    hh = jnp.arange(H_RET, dtype=F32)
    lg = jnp.log1p(-jnp.exp2(-5.0 - hh))
    n = jnp.arange(L, dtype=F32)
    a = jnp.exp(lg[:, None] * (n + 1.0))
    b = jnp.exp(lg[:, None] * (L - 1.0 - n))
    gl = jnp.exp(lg * L)
    ch = jnp.arange(L) // CHUNK
    m = jnp.exp(lg[:, None, None] * jnp.abs(n[:, None] - n[None, :])) * (ch[None, :] <= ch[:, None]).astype(F32)
    inv_freq = ROPE_BASE ** (-jnp.linspace(0.0, 1.0, DK_RET // 2, dtype=F32))
    ang = jnp.arange(T, dtype=F32)[:, None] * inv_freq[None, :]
    cos, sin = jnp.cos(ang), jnp.sin(ang)
    return dict(
        L=L, M=m,
        a=jnp.broadcast_to(a[:, :, None], (H_RET, L, DK_RET)),
        b=jnp.broadcast_to(b[:, :, None], (H_RET, L, DK_RET)),
        gl=jnp.broadcast_to(gl[:, None, None], (H_RET, 1, DK_RET)),
        cos=jnp.concatenate([cos, cos], axis=-1), sin=jnp.concatenate([-sin, sin], axis=-1))


def _rot(x, cs, sn):
    return x * cs + pltpu.roll(x, DK_RET // 2, 1) * sn


def _unrot(dy, cs, sn):
    return dy * cs + pltpu.roll(dy * sn, DK_RET // 2, 1)


def _ret_fwd(big, tb, name, comm=None):
    T = big.shape[0]
    L = tb["L"]
    nsc = T // L
    scale = DK_RET ** -0.5

    def body(x_ref, cos_ref, sin_ref, m_ref, a_ref, b_ref, gl_ref, y_ref, o_ref, st_ref, s_s):
        @pl.when(pl.program_id(1) == 0)
        def _():
            s_s[...] = jnp.zeros_like(s_s)

        cs, sn = cos_ref[...], sin_ref[...]
        qt = _rot(x_ref[:, RQ].astype(F32), cs, sn) * scale
        kt = _rot(x_ref[:, RK].astype(F32), cs, sn)
        qb, kb, vb = qt.astype(BF), kt.astype(BF), x_ref[:, RV]
        s_prev = s_s[...]
        st_ref[...] = s_prev
        p = lax.dot_general(qb, kb, NT_DIMS, preferred_element_type=F32) * m_ref[...]
        o = (jnp.dot(p.astype(BF), vb, preferred_element_type=F32)
             + jnp.dot((qt * a_ref[...]).astype(BF), s_prev.astype(BF), preferred_element_type=F32))
        s_s[...] = s_prev * gl_ref[...] + lax.dot_general((kt * b_ref[...]).astype(BF), vb, TN_DIMS,
                                                         preferred_element_type=F32)
        o_ref[...] = o
        gv = x_ref[:, RG].astype(F32)
        y_ref[...] = (gv * _sigmoid(gv) * o * _rms_r(o)).astype(BF)

    tab = pl.BlockSpec((L, DK_RET), lambda h, i: (i, 0))
    per_head = pl.BlockSpec((None, L, DK_RET), lambda h, i: (h, 0, 0))
    out = pl.BlockSpec((L, LANE), lambda h, i: (i, h))
    return _call(
        body, name=name, grid=(H_RET, nsc), comm=comm,
        args=(big, tb["cos"], tb["sin"], tb["M"], tb["a"], tb["b"], tb["gl"]),
        in_specs=[_group_spec(big, RET_BASE, RET_GROUP, L, lambda h, i: (i, h)), tab, tab,
                  pl.BlockSpec((None, L, L), lambda h, i: (h, 0, 0)), per_head, per_head,
                  pl.BlockSpec((None, 1, DK_RET), lambda h, i: (h, 0, 0))],
        out_specs=[out, out, pl.BlockSpec((None, None, DK_RET, DK_RET), lambda h, i: (i, h, 0, 0))],
        out_shape=[_sds((T, BRANCH_W), BF), _sds((T, BRANCH_W), F32), _sds((nsc, H_RET, DK_RET, DK_RET), F32)],
        scratch_shapes=[pltpu.VMEM((DK_RET, DK_RET), F32)],
        sem=(ARB, ARB))


def _ret_bwd(big, o, st, dy, tb, dbig, name):
    T = big.shape[0]
    L = tb["L"]
    nsc = T // L
    scale = DK_RET ** -0.5

    def body(x_ref, cos_ref, sin_ref, m_ref, a_ref, b_ref, gl_ref, o_ref, st_ref, dy_ref, _, d_ref, ds_s):
        @pl.when(pl.program_id(1) == 0)
        def _():
            ds_s[...] = jnp.zeros_like(ds_s)

        cs, sn = cos_ref[...], sin_ref[...]
        mm, av, bv = m_ref[...], a_ref[...], b_ref[...]
        qt = _rot(x_ref[:, RQ].astype(F32), cs, sn) * scale
        kt = _rot(x_ref[:, RK].astype(F32), cs, sn)
        qb, kb, vb = qt.astype(BF), kt.astype(BF), x_ref[:, RV]
        pb = (lax.dot_general(qb, kb, NT_DIMS, preferred_element_type=F32) * mm).astype(BF)
        ov = o_ref[...]
        r = _rms_r(ov)
        oh = ov * r
        gv = x_ref[:, RG].astype(F32)
        sg = _sigmoid(gv)
        dyv = dy_ref[...].astype(F32)
        d_ref[:, RG] = (dyv * oh * (sg * (1.0 + gv * (1.0 - sg)))).astype(BF)
        doh = dyv * gv * sg
        dob = (r * (doh - oh * jnp.mean(doh * oh, axis=-1, keepdims=True))).astype(BF)
        dsb = ds_s[...].astype(BF)
        spb = st_ref[...].astype(BF)
        dpb = (lax.dot_general(dob, vb, NT_DIMS, preferred_element_type=F32) * mm).astype(BF)
        dqt = (jnp.dot(dpb, kb, preferred_element_type=F32)
               + lax.dot_general(dob, spb, NT_DIMS, preferred_element_type=F32) * av)
        dkt = (lax.dot_general(dpb, qb, TN_DIMS, preferred_element_type=F32)
               + lax.dot_general(vb, dsb, NT_DIMS, preferred_element_type=F32) * bv)
        dv = (lax.dot_general(pb, dob, TN_DIMS, preferred_element_type=F32)
              + jnp.dot((kt * bv).astype(BF), dsb, preferred_element_type=F32))
        ds_s[...] = ds_s[...] * gl_ref[...] + lax.dot_general((qt * av).astype(BF), dob, TN_DIMS,
                                                              preferred_element_type=F32)
        d_ref[:, RQ] = (_unrot(dqt, cs, sn) * scale).astype(BF)
        d_ref[:, RK] = _unrot(dkt, cs, sn).astype(BF)
        d_ref[:, RV] = dv.astype(BF)

    def rev(i):
        return nsc - 1 - i

    group = _group_spec(big, RET_BASE, RET_GROUP, L, lambda h, i: (rev(i), h))
    tab = pl.BlockSpec((L, DK_RET), lambda h, i: (rev(i), 0))
    per_head = pl.BlockSpec((None, L, DK_RET), lambda h, i: (h, 0, 0))
    out = pl.BlockSpec((L, LANE), lambda h, i: (rev(i), h))
    return pl.pallas_call(
        body, name=name, grid=(H_RET, nsc),
        in_specs=[group, tab, tab,
                  pl.BlockSpec((None, L, L), lambda h, i: (h, 0, 0)), per_head, per_head,
                  pl.BlockSpec((None, 1, DK_RET), lambda h, i: (h, 0, 0)),
                  out, pl.BlockSpec((None, None, DK_RET, DK_RET), lambda h, i: (rev(i), h, 0, 0)), out,
                  pl.BlockSpec(memory_space=pl.ANY)],
        out_specs=group,
        out_shape=_sds(dbig.shape, BF),
        scratch_shapes=[pltpu.VMEM((DK_RET, DK_RET), F32)],
        input_output_aliases={10: 0}, compiler_params=_cp((PAR, ARB)),
    )(big, tb["cos"], tb["sin"], tb["M"], tb["a"], tb["b"], tb["gl"], o, st, dy, dbig)


def _relbias_onehot(n):
    mm = lax.broadcasted_iota(jnp.int32, (RB_PAD, ATT_TOEP), 1)
    rr = lax.broadcasted_iota(jnp.int32, (RB_PAD, ATT_TOEP), 0)
    idx = jnp.clip(n + ATT_TOEP - mm, 0, 2 * REL_CLIP)
    return (rr == idx).astype(F32)


def _relbias_expand(rbp, name):
    far = ATT_SPAN - ATT_TOEP

    def body(rb_ref, o_ref):
        rb = rb_ref[...]
        const = jnp.broadcast_to(rb[:, 2 * REL_CLIP:2 * REL_CLIP + 1], (H_ATT, far))

        def row(n, c):
            toep = jnp.dot(rb, _relbias_onehot(n), preferred_element_type=F32, precision=lax.Precision.HIGHEST)
            m = lax.broadcasted_iota(jnp.int32, (1, ATT_SPAN), 1)
            d = n // CHUNK + N_PREV - m // CHUNK
            neg = jnp.where((d >= 0) & (d <= N_PREV), 0.0, NEG_INF).astype(F32)
            o_ref[n] = jnp.concatenate([const, toep], axis=1) + neg
            return c

        lax.fori_loop(0, ATT_TQ, row, 0)

    return pl.pallas_call(
        body, name=name,
        in_specs=[pl.BlockSpec(memory_space=pltpu.VMEM)],
        out_specs=pl.BlockSpec(memory_space=pltpu.VMEM),
        out_shape=_sds((ATT_TQ, H_ATT, ATT_SPAN), F32),
    )(rbp)


def _relbias_grad(dbt, name):
    far = ATT_SPAN - ATT_TOEP

    def body(d_ref, o_ref):
        def row(n, carry):
            acc, cs = carry
            dn = d_ref[n]
            acc = acc + lax.dot_general(dn[:, far:], _relbias_onehot(n), NT_DIMS, preferred_element_type=F32,
                                        precision=lax.Precision.HIGHEST)
            cs = cs + jnp.sum(dn[:, :far], axis=1, keepdims=True)
            return acc, cs

        acc, cs = lax.fori_loop(0, ATT_TQ, row, (jnp.zeros((H_ATT, RB_PAD), F32), jnp.zeros((H_ATT, 1), F32)))
        rr = lax.broadcasted_iota(jnp.int32, (H_ATT, RB_PAD), 1)
        o_ref[...] = acc + jnp.where(rr == 2 * REL_CLIP, cs, 0.0)

    return pl.pallas_call(
        body, name=name,
        in_specs=[pl.BlockSpec(memory_space=pltpu.VMEM)],
        out_specs=pl.BlockSpec(memory_space=pltpu.VMEM),
        out_shape=_sds((H_ATT, RB_PAD), F32),
    )(dbt)


def _att_pad_fill(dst_s, src_ref, cols, T):
    dst_s[pl.ds(0, ATT_PAD), :] = jnp.zeros((ATT_PAD, LANE), dst_s.dtype)
    R = min(512, T)

    def cp(t, c):
        dst_s[pl.ds(pl.multiple_of(ATT_PAD + t * R, LANE), R), :] = src_ref[pl.ds(pl.multiple_of(t * R, R), R), cols]
        return c

    lax.fori_loop(0, T // R, cp, 0)


ATT_WIN = ATT_SUB * ATT_TQ + ATT_PAD


def _att_probs(s_full, sub, bias, t0):
    s = s_full[sub * ATT_TQ:(sub + 1) * ATT_TQ, sub * ATT_TQ:sub * ATT_TQ + ATT_SPAN] * (DH_ATT ** -0.5) + bias
    key_pos = t0 + sub * ATT_TQ - ATT_PAD + lax.broadcasted_iota(jnp.int32, (1, ATT_SPAN), 1)
    s = jnp.where(key_pos >= 0, s, NEG_INF)
    p = jnp.exp(s - jnp.max(s, axis=-1, keepdims=True))
    return p * (1.0 / jnp.sum(p, axis=-1, keepdims=True))


def _att_band(tiles):
    rows = []
    for sub, t in enumerate(tiles):
        parts = []
        if sub:
            parts.append(jnp.zeros((ATT_TQ, sub * ATT_TQ), BF))
        parts.append(t)
        if sub < ATT_SUB - 1:
            parts.append(jnp.zeros((ATT_TQ, (ATT_SUB - 1 - sub) * ATT_TQ), BF))
        rows.append(jnp.concatenate(parts, axis=1))
    return jnp.concatenate(rows, axis=0)


def _att_head_masks(x):
    first = lax.broadcasted_iota(jnp.int32, (1, LANE), 1) < DH_ATT
    zero = jnp.zeros_like(x)
    return first, (jnp.where(first, x, zero), jnp.where(first, zero, x))


def _att_fwd(big, bias, name, comm=None):
    T = big.shape[0]
    rows = ATT_SUB * ATT_TQ
    nt = T // rows

    def body(x_ref, b_ref, y_ref, kp_s, vp_s):
        i = pl.program_id(1)

        @pl.when(i == 0)
        def _():
            _att_pad_fill(kp_s, x_ref, AK, T)
            _att_pad_fill(vp_s, x_ref, AV, T)

        t0 = pl.multiple_of(i * rows, rows)
        kw = kp_s[pl.ds(t0, ATT_WIN), :]
        vw = vp_s[pl.ds(t0, ATT_WIN), :]
        first, qm = _att_head_masks(x_ref[pl.ds(t0, rows), AQ])
        outs = []
        for hh in range(2):
            s_full = lax.dot_general(qm[hh], kw, NT_DIMS, preferred_element_type=F32)
            band = _att_band([_att_probs(s_full, sub, b_ref[hh], t0).astype(BF) for sub in range(ATT_SUB)])
            outs.append(jnp.dot(band, vw, preferred_element_type=F32))
        y_ref[...] = jnp.where(first, outs[0], outs[1]).astype(BF)

    return _call(
        body, name=name, grid=(H_ATT // 2, nt), args=(big, bias), comm=comm,
        in_specs=[_group_spec(big, ATT_BASE, ATT_GROUP, T, lambda p, i: (0, p)),
                  pl.BlockSpec((2, ATT_TQ, ATT_SPAN), lambda p, i: (p, 0, 0))],
        out_specs=[pl.BlockSpec((rows, LANE), lambda p, i: (i, p))],
        out_shape=[_sds((T, BRANCH_W), BF)],
        scratch_shapes=[pltpu.VMEM((T + ATT_PAD, LANE), BF), pltpu.VMEM((T + ATT_PAD, LANE), BF)],
        sem=(ARB, ARB))


def _att_bwd(big, bias, dy, dbig, name, comm=None):
    T = big.shape[0]
    rows = ATT_SUB * ATT_TQ
    nt = T // rows
    scale = DH_ATT ** -0.5

    def body(x_ref, b_ref, dy_ref, _, d_ref, db_ref, kp_s, vp_s, dk_s, dv_s):
        i = pl.program_id(1)

        @pl.when(i == 0)
        def _():
            _att_pad_fill(kp_s, x_ref, AK, T)
            _att_pad_fill(vp_s, x_ref, AV, T)
            dk_s[...] = jnp.zeros_like(dk_s)
            dv_s[...] = jnp.zeros_like(dv_s)
            db_ref[...] = jnp.zeros_like(db_ref)

        t0 = pl.multiple_of(i * rows, rows)
        win = pl.ds(t0, ATT_WIN)
        kw = kp_s[win, :]
        vw = vp_s[win, :]
        first, qm = _att_head_masks(x_ref[pl.ds(t0, rows), AQ])
        _, dom = _att_head_masks(dy_ref[...])
        dqs, dkt, dvt = [], None, None
        for hh in range(2):
            s_full = lax.dot_general(qm[hh], kw, NT_DIMS, preferred_element_type=F32)
            dp_full = lax.dot_general(dom[hh], vw, NT_DIMS, preferred_element_type=F32)
            ps, dss, db = [], [], None
            for sub in range(ATT_SUB):
                pn = _att_probs(s_full, sub, b_ref[hh], t0)
                dp = dp_full[sub * ATT_TQ:(sub + 1) * ATT_TQ, sub * ATT_TQ:sub * ATT_TQ + ATT_SPAN]
                ds = pn * (dp - jnp.sum(dp * pn, axis=-1, keepdims=True))
                db = ds if db is None else db + ds
                ps.append(pn.astype(BF))
                dss.append(ds.astype(BF))
            db_ref[hh] += db
            ds_band, p_band = _att_band(dss), _att_band(ps)
            dqs.append(jnp.dot(ds_band, kw, preferred_element_type=F32))
            qt = jnp.transpose(qm[hh].astype(F32)).astype(BF)
            dot_ = jnp.transpose(dom[hh].astype(F32)).astype(BF)
            dk_h = jnp.dot(qt, ds_band, preferred_element_type=F32)
            dv_h = jnp.dot(dot_, p_band, preferred_element_type=F32)
            dkt = dk_h if dkt is None else dkt + dk_h
            dvt = dv_h if dvt is None else dvt + dv_h
        d_ref[pl.ds(t0, rows), AQ] = (jnp.where(first, dqs[0], dqs[1]) * scale).astype(BF)
        dk_s[win, :] += jnp.transpose(dkt) * scale
        dv_s[win, :] += jnp.transpose(dvt)

        @pl.when(i == nt - 1)
        def _():
            R = min(512, T)

            def cp(t, c):
                src = pl.ds(pl.multiple_of(ATT_PAD + t * R, LANE), R)
                dst = pl.ds(pl.multiple_of(t * R, R), R)
                d_ref[dst, AK] = dk_s[src, :].astype(BF)
                d_ref[dst, AV] = dv_s[src, :].astype(BF)
                return c

            lax.fori_loop(0, T // R, cp, 0)

    group = _group_spec(big, ATT_BASE, ATT_GROUP, T, lambda p, i: (0, p))
    tile = pl.BlockSpec((rows, LANE), lambda p, i: (i, p))
    bspec = pl.BlockSpec((2, ATT_TQ, ATT_SPAN), lambda p, i: (p, 0, 0))
    return _call(
        body, name=name, grid=(H_ATT // 2, nt), args=(big, bias, dy, dbig), comm=comm, aliases={3: 0}, vmem_mb=56,
        in_specs=[group, bspec, tile, pl.BlockSpec(memory_space=pl.ANY)],
        out_specs=[group, bspec],
        out_shape=[_sds(dbig.shape, BF), _sds((H_ATT, ATT_TQ, ATT_SPAN), F32)],
        scratch_shapes=[pltpu.VMEM((T + ATT_PAD, LANE), BF), pltpu.VMEM((T + ATT_PAD, LANE), BF),
                        pltpu.VMEM((T + ATT_PAD, LANE), F32), pltpu.VMEM((T + ATT_PAD, LANE), F32)],
        sem=(ARB, ARB))


def _merge_fwd(x1, big, ys, wb, wo, name):
    T, D = x1.shape
    tm = min(TM, T)

    def body(x_ref, gp_ref, yc_ref, yr_ref, ya_ref, wb_ref, wo_ref, x2_ref, p_ref, mg_ref):
        merged = jnp.zeros((tm, D), F32)
        for i, y_ref in enumerate((yc_ref, yr_ref, ya_ref)):
            cols = slice(i * D, (i + 1) * D)
            pb = jnp.dot(y_ref[...], wb_ref[i], preferred_element_type=F32).astype(BF)
            p_ref[:, cols] = pb
            merged = merged + _sigmoid(gp_ref[:, cols].astype(F32)) * pb.astype(F32)
        mb = merged.astype(BF)
        mg_ref[...] = mb
        x2_ref[...] = x_ref[...] + jnp.dot(mb, wo_ref[...], preferred_element_type=F32)

    tok = pl.BlockSpec((tm, D), lambda i: (i, 0))
    wide = pl.BlockSpec((tm, 3 * D), lambda i: (i, 0))
    yspec = pl.BlockSpec((tm, BRANCH_W), lambda i: (i, 0))
    return pl.pallas_call(
        body, name=name, grid=(T // tm,),
        in_specs=[tok, wide, yspec, yspec, yspec,
                  pl.BlockSpec((3, BRANCH_W, D), lambda i: (0, 0, 0)),
                  pl.BlockSpec((D, D), lambda i: (0, 0))],
        out_specs=[tok, wide, tok],
        out_shape=[_sds((T, D), F32), _sds((T, 3 * D), BF), _sds((T, D), BF)],
        compiler_params=_cp((PAR,)),
    )(x1, big, *ys, wb, wo)


def _merge_bwd(dx2, big, p, wb, wo, name):
    T, D = dx2.shape
    tm = min(TM, T)

    def body(dx_ref, gp_ref, p_ref, wb_ref, wo_ref, dp_ref, dgp_ref, dyc_ref, dyr_ref, dya_ref, dxb_ref):
        dxb = dx_ref[...].astype(BF)
        dxb_ref[...] = dxb
        dm = lax.dot_general(dxb, wo_ref[...], NT_DIMS, preferred_element_type=F32)
        for i, dy_ref in enumerate((dyc_ref, dyr_ref, dya_ref)):
            cols = slice(i * D, (i + 1) * D)
            gt = _sigmoid(gp_ref[:, cols].astype(F32))
            dpb = (dm * gt).astype(BF)
            dp_ref[:, cols] = dpb
            dgp_ref[:, cols] = (dm * p_ref[:, cols].astype(F32) * gt * (1.0 - gt)).astype(BF)
            dy_ref[...] = lax.dot_general(dpb, wb_ref[i], NT_DIMS, preferred_element_type=F32).astype(BF)

    tok = pl.BlockSpec((tm, D), lambda i: (i, 0))
    wide = pl.BlockSpec((tm, 3 * D), lambda i: (i, 0))
    yspec = pl.BlockSpec((tm, BRANCH_W), lambda i: (i, 0))
    return pl.pallas_call(
        body, name=name, grid=(T // tm,),
        in_specs=[tok, wide, wide,
                  pl.BlockSpec((3, BRANCH_W, D), lambda i: (0, 0, 0)),
                  pl.BlockSpec((D, D), lambda i: (0, 0))],
        out_specs=[wide, wide, yspec, yspec, yspec, tok],
        out_shape=[_sds((T, 3 * D), BF), _sds(big.shape, BF)] + [_sds((T, BRANCH_W), BF)] * 3 + [_sds((T, D), BF)],
        compiler_params=_cp((PAR,)),
    )(dx2, big, p, wb, wo)


def _loss_head(x, tgt, fw, name):
    T, D = x.shape
    tm = min(TM, T)

    def body(x_ref, t_ref, w_ref, loss_ref, dx_ref, dw_ref):
        @pl.when(pl.program_id(0) == 0)
        def _():
            loss_ref[...] = jnp.zeros_like(loss_ref)
            dw_ref[...] = jnp.zeros_like(dw_ref)

        xv = x_ref[...]
        wv = w_ref[...]
        e = xv * _rms_r(xv) * wv - t_ref[...]
        loss_ref[...] += 0.5 * jnp.sum(jnp.mean(e * e, axis=-1, keepdims=True))
        dx, dn = _rms_bwd(e * (1.0 / D), xv, wv)
        dx_ref[...] = dx
        dw_ref[...] += dn

    tok = pl.BlockSpec((tm, D), lambda i: (i, 0))
    return pl.pallas_call(
        body, name=name, grid=(T // tm,),
        in_specs=[tok, tok, pl.BlockSpec((1, D), lambda i: (0, 0))],
        out_specs=[pl.BlockSpec((8, LANE), lambda i: (0, 0)), tok, pl.BlockSpec((1, D), lambda i: (0, 0))],
        out_shape=[_sds((8, LANE), F32), _sds((T, D), F32), _sds((1, D), F32)],
        compiler_params=_cp((ARB,)),
    )(x, tgt, fw)


def _block_rows(rows, cols):
    cap = max(8, (1 << 18) // cols)
    best = None
    for r in range(8, rows + 1, 8):
        if rows % r == 0 and r <= cap:
            best = r
    return best if best is not None else rows


def _sum4(land, l, n_layers, name, prev=None):
    _, rows, cols = land.shape
    br = _block_rows(rows, cols)

    def body(*refs):
        l_ref, o_ref = refs[0], refs[-1]
        o_ref[...] = ((l_ref[3].astype(F32) + l_ref[0].astype(F32)) + l_ref[1].astype(F32)) + l_ref[2].astype(F32)

    in_specs = [pl.BlockSpec((4, br, cols), lambda i: (0, i, 0))]
    args = [land]
    aliases = {}
    if prev is not None:
        in_specs.append(pl.BlockSpec(memory_space=pl.ANY))
        args.append(prev)
        aliases = {1: 0}
    return pl.pallas_call(
        body, name=name, grid=(rows // br,), in_specs=in_specs,
        out_specs=pl.BlockSpec((None, br, cols), lambda i: (l, i, 0)),
        out_shape=_sds((n_layers, rows, cols), F32),
        input_output_aliases=aliases, compiler_params=_cp((PAR,)),
    )(*args)


def _adamw_math(w, g, m, v):
    m = ADAM_B1 * m + (1.0 - ADAM_B1) * g
    v = ADAM_B2 * v + (1.0 - ADAM_B2) * (g * g)
    m_hat = m / (1.0 - ADAM_B1 ** ADAM_STEP)
    v_hat = v / (1.0 - ADAM_B2 ** ADAM_STEP)
    delta = -ADAM_LR * (m_hat / (jnp.sqrt(v_hat) + ADAM_EPS) + ADAM_WD * w)
    return delta, m, v


def _adamw(w, ga, gb, m, v, name):
    rows, cols = w.shape
    br = _block_rows(rows, cols)
    two = gb is not None

    def body(*refs):
        if two:
            w_ref, ga_ref, gb_ref, m_ref, v_ref, g_ref, d_ref, nm_ref, nv_ref = refs
            g = ga_ref[...] + gb_ref[...]
        else:
            w_ref, ga_ref, m_ref, v_ref, g_ref, d_ref, nm_ref, nv_ref = refs
            g = ga_ref[...]
        d, nm, nv = _adamw_math(w_ref[...], g, m_ref[...], v_ref[...])
        g_ref[...] = g
        d_ref[...] = d
        nm_ref[...] = nm
        nv_ref[...] = nv

    blk = pl.BlockSpec((br, cols), lambda i: (i, 0))
    args = [w, ga] + ([gb] if two else []) + [m, v]
    return pl.pallas_call(
        body, name=name, grid=(rows // br,),
        in_specs=[blk] * len(args), out_specs=[blk] * 4,
        out_shape=[_sds((rows, cols), F32)] * 4,
        compiler_params=_cp((PAR,)),
    )(*args)


def _swap_cores(vs, name):
    n = len(vs)

    def body(*refs):
        v_refs, o_refs = refs[:n], refs[n:2 * n]
        ssem, rsem = refs[2 * n:]
        x, y, c = _place()
        copies = [pltpu.make_async_remote_copy(
            src_ref=v_refs[k], dst_ref=o_refs[k], send_sem=ssem.at[k], recv_sem=rsem.at[k],
            device_id=(x, y, 1 - c), device_id_type=MESH) for k in range(n)]
        for cp in copies:
            cp.start()
        for cp in copies:
            cp.wait()

    hbm = pl.BlockSpec(memory_space=pl.ANY)
    return pl.pallas_call(
        body, name=name,
        in_specs=[hbm] * n, out_specs=[hbm] * n,
        out_shape=[_sds(v.shape, v.dtype) for v in vs],
        scratch_shapes=[pltpu.SemaphoreType.DMA((n,)), pltpu.SemaphoreType.DMA((n,))],
    )(*vs)


def _allreduce_small(v, name):
    rows = v.shape[0]
    flips = [(fx, fy, fc) for fx in (0, 1) for fy in (0, 1) for fc in (0, 1) if fx or fy or fc]

    def body(v_ref, o_ref, all_s, ssem, rsem):
        x, y, c = _place()

        def peer(f):
            return (x + f[0] - 2 * x * f[0], y + f[1] - 2 * y * f[1], c + f[2] - 2 * c * f[2])

        def slot(p):
            return all_s.at[4 * p[0] + 2 * p[1] + p[2]]

        def copy(k, f, owner):
            return pltpu.make_async_remote_copy(
                src_ref=v_ref, dst_ref=slot(owner), send_sem=ssem.at[k], recv_sem=rsem.at[k],
                device_id=peer(f), device_id_type=MESH)

        sends = [copy(k, f, (x, y, c)) for k, f in enumerate(flips)]
        for cp in sends:
            cp.start()
        all_s[4 * x + 2 * y + c] = v_ref[...]
        for k, f in enumerate(flips):
            copy(k, f, peer(f)).wait_recv()
        for cp in sends:
            cp.wait_send()
        acc = all_s[0]
        for d in range(1, 8):
            acc = acc + all_s[d]
        o_ref[...] = acc

    return pl.pallas_call(
        body, name=name,
        in_specs=[pl.BlockSpec(memory_space=pltpu.VMEM)],
        out_specs=pl.BlockSpec(memory_space=pltpu.VMEM),
        out_shape=_sds((rows, LANE), F32),
        scratch_shapes=[pltpu.VMEM((8, rows, LANE), F32), pltpu.SemaphoreType.DMA((7,)), pltpu.SemaphoreType.DMA((7,))],
    )(v)


BIG_NAMES = ("ffn1_w_gate", "ffn1_w_up", "ffn1_w_down", "w_in", "w_branch", "w_merge_gate", "w_out",
             "ffn2_w_gate", "ffn2_w_up", "ffn2_w_down")


FFN1 = ("ffn1_w_gate", "ffn1_w_up", "ffn1_w_down")
FFN2 = ("ffn2_w_gate", "ffn2_w_up", "ffn2_w_down")
MIX_IN = ("w_in", "w_merge_gate")
MIX_OUT = ("w_branch", "w_out")


def _keys(names, l):
    return [(n, l) for n in names]


def _local_step(x, tgt, small, convw_full, wx, n_layers):
    T, D = x.shape
    L = n_layers
    ns = N_SHARD
    dq = D // ns
    W = wx.w

    def hosted(call, keys, scatter=False):
        comm = wx.pieces(keys, scatter)
        main, extra = call(comm)
        if comm is not None:
            wx.arrived(keys, extra, scatter)
        return main

    def mixer_views(l):
        g4 = W[("w_merge_gate", l)]
        gates = jnp.transpose(g4, (0, 2, 1, 3)).reshape(D, 3 * D)
        win = jnp.transpose(W[("w_in", l)], (1, 0, 2)).reshape(D, -1)
        return jnp.concatenate([gates, _permute_in_cols(win)], axis=-1)

    def out_views(l):
        wb = jnp.transpose(W[("w_branch", l)], (1, 2, 0, 3)).reshape(3, BRANCH_W, D)
        wo = W[("w_out", l)].reshape(D, D)
        return wb, wo

    tb = _ret_tables(T)
    rb_pad = jnp.pad(small["rel_bias"], ((0, 0), (0, 0), (0, RB_PAD - N_REL)))

    saved = []
    h = x
    for l in range(L):
        s = {"x0": h}
        nxt = l + 1
        x1, s["g1"], s["u1"] = hosted(
            lambda c: _ffn_fwd(h, small["ffn1_norm"][l][None], W[("ffn1_w_gate", l)], W[("ffn1_w_up", l)],
                               W[("ffn1_w_down", l)], f"ffn1_fwd_{l}", comm=c), _keys(MIX_IN, l))
        s["x1"] = x1
        s["wbig"] = mixer_views(l)
        big, s["h"] = _inproj_fwd(x1, small["mix_norm"][l][None], s["wbig"], f"inproj_fwd_{l}")
        s["big"] = big
        s["bias"] = jnp.transpose(_relbias_expand(rb_pad[l], f"relbias_expand_{l}"), (1, 0, 2))
        s["yc"] = _conv_fwd(big, convw_full[l], f"conv_fwd_{l}")
        s["yr"], s["o"], s["st"] = hosted(lambda c: _ret_fwd(big, tb, f"ret_fwd_{l}", comm=c), _keys(MIX_OUT, l))
        (s["ya"],) = hosted(lambda c: _att_fwd(big, s["bias"], f"att_fwd_{l}", comm=c), _keys(FFN2, l))
        s["wb"], s["wo"] = out_views(l)
        x2, s["p"], s["mg"] = _merge_fwd(x1, big, (s["yc"], s["yr"], s["ya"]), s["wb"], s["wo"], f"merge_fwd_{l}")
        s["x2"] = x2
        h, s["g2"], s["u2"] = hosted(
            lambda c: _ffn_fwd(x2, small["ffn2_norm"][l][None], W[("ffn2_w_gate", l)], W[("ffn2_w_up", l)],
                               W[("ffn2_w_down", l)], f"ffn2_fwd_{l}", comm=c), _keys(FFN1, nxt) if nxt < L else [])
        saved.append(s)

    loss_p, dx, d_final = _loss_head(h, tgt, small["final_norm"][None], "loss_head")

    gs = {"final_norm": d_final[0]}
    for k in ("ffn1_norm", "mix_norm", "ffn2_norm", "rel_bias", "conv_w"):
        gs[k] = [None] * L
    tk = min(2048, T)
    nk = T // tk

    def ffn_back(pre, l, dxo, x_in, g, u, first_keys, second_keys):
        nw = small[pre + "_norm"][l][None]
        dgv, duv, av, hb, dacc = hosted(
            lambda c: _ffn_bwd_hidden(dxo, x_in, nw, g, u, W[(pre + "_w_down", l)], f"{pre}_bwd_hidden_{l}", comm=c),
            first_keys, scatter=True)
        dxn, dn = hosted(
            lambda c: _ffn_bwd_resid(dgv, duv, W[(pre + "_w_gate", l)], W[(pre + "_w_up", l)], x_in, nw, dxo,
                                     f"{pre}_bwd_resid_{l}", comm=c),
            second_keys, scatter=True)
        gs[pre + "_norm"][l] = dn[0]
        return dxn, (hb, dgv, duv, av, dacc)

    def ffn_grads(pre, l, hb, dgv, duv, av, dacc, chain=False, carry=()):
        fs = dgv.shape[-1]
        hspec = pl.BlockSpec((tk, D), lambda p, q, k: (k, 0))
        sspec = pl.BlockSpec((None, tk, fs), lambda p, q, k: (p, k, 0))
        up_spec = pl.BlockSpec((None, D, fs), lambda p, q, k: (p, 0, 0))
        down_spec = pl.BlockSpec((None, fs, D), lambda p, q, k: (p, 0, 0))
        jobs = [(pre + "_w_gate", hb, dgv, hspec, sspec, (ns, D, fs), up_spec),
                (pre + "_w_up", hb, duv, hspec, sspec, (ns, D, fs), up_spec),
                (pre + "_w_down", av, dacc, sspec, hspec, (ns, fs, D), down_spec)]
        before = None
        for nm, a, b, a_spec, b_spec, shape, o_spec in jobs:
            def product(c):
                r = _tn(a, b, a_spec, b_spec, _sds(shape, BF), o_spec, (ns, 1, nk), f"d{nm}_{l}", comm=c)
                return (r, []) if c is None else r
            if before is None:
                keys = list(carry)
            else:
                keys = [before] if chain else []
            wx.g[(nm, l)] = hosted(product, keys, scatter=True)
            before = (nm, l)

    for l in reversed(range(L)):
        s = saved[l]
        above = _keys(FFN1, l + 1) if l + 1 < L else [None] * 3
        dx, parts = ffn_back("ffn2", l, dx, s["x2"], s["g2"], s["u2"], [k for k in above[:1] if k],
                             [k for k in above[1:2] if k])
        ffn_grads("ffn2", l, *parts, carry=[k for k in above[2:] if k])
        dp, dbig, dyc, dyr, dya, dxb = _merge_bwd(dx, s["big"], s["p"], s["wb"], s["wo"], f"merge_bwd_{l}")
        wx.g[("w_out", l)] = _tn(
            s["mg"], dxb, pl.BlockSpec((tk, dq), lambda p, q, k: (k, p)), pl.BlockSpec((tk, D), lambda p, q, k: (k, 0)),
            _sds((ns, dq, D), BF), pl.BlockSpec((None, dq, D), lambda p, q, k: (p, 0, 0)), (ns, 1, nk), f"dw_out_{l}")
        gb = None
        for i, yv in enumerate((s["yc"], s["yr"], s["ya"])):
            gb = _tn(yv, dp,
                     pl.BlockSpec((tk, BRANCH_W), lambda p, q, k: (k, 0)),
                     pl.BlockSpec((tk, dq), lambda p, q, k, i=i: (k, i * ns + p)),
                     _sds((ns, 3, BRANCH_W, dq), BF),
                     pl.BlockSpec((None, None, BRANCH_W, dq), lambda p, q, k, i=i: (p, i, 0, 0)),
                     (ns, 1, nk), f"dw_branch{i}_{l}", prev=gb)
        wx.g[("w_branch", l)] = gb
        dbig, dcw = _conv_bwd(s["big"], dyc, convw_full[l], dbig, f"conv_bwd_{l}")
        gs["conv_w"][l] = dcw
        dbig = _ret_bwd(s["big"], s["o"], s["st"], dyr, tb, dbig, f"ret_bwd_{l}")
        dbig, dbias = hosted(lambda c: _att_bwd(s["big"], s["bias"], dya, dbig, f"att_bwd_{l}", comm=c),
                             _keys(FFN2, l), scatter=True)
        gs["rel_bias"][l] = _relbias_grad(jnp.transpose(dbias, (1, 0, 2)), f"relbias_grad_{l}")[:, :N_REL]
        n_in = N_SEG * BRANCH_W
        bn = 1024 if (3 * D) % 1024 == 0 else BRANCH_W
        dwp = _tn(s["h"], dbig, pl.BlockSpec((tk, D), lambda p, q, k: (k, 0)),
                  pl.BlockSpec((tk, bn), lambda p, q, k: (k, 3 * D // bn + q)),
                  _sds((D, n_in), BF), pl.BlockSpec((D, bn), lambda p, q, k: (0, q)), (1, n_in // bn, nk), f"dw_in_{l}")
        wx.g[("w_in", l)] = jnp.transpose(_unpermute_in_cols(dwp).reshape(D, ns, n_in // ns), (1, 0, 2))
        wx.g[("w_merge_gate", l)] = _tn_gates(s["h"], dbig, ns, tk, f"dw_merge_gate_{l}")
        dx, dn = hosted(
            lambda c: _inproj_bwd(dbig, s["wbig"], s["x1"], small["mix_norm"][l][None], dx, f"inproj_bwd_{l}", comm=c),
            [("w_in", l)], scatter=True)
        gs["mix_norm"][l] = dn[0]
        dx, parts = ffn_back("ffn1", l, dx, s["x0"], s["g1"], s["u1"],
                             [("w_merge_gate", l), ("w_branch", l), ("w_out", l)], [])
        ffn_grads("ffn1", l, *parts, chain=(l == 0))

    for k in ("ffn1_norm", "mix_norm", "ffn2_norm", "rel_bias", "conv_w"):
        gs[k] = jnp.stack(gs[k])
    return loss_p, dx, gs


class _Exchange:
    def __init__(self, shards):
        self.shards = shards
        self.w = {}
        self.g = {}
        self.landed = {}

    def own(self, key):
        return self.shards[key[0]][key[1]].astype(BF)

    def pieces(self, keys, scatter):
        if not keys:
            return None
        return _Pieces([self.g[k] for k in keys] if scatter else [self.own(k) for k in keys], scatter)

    def arrived(self, keys, outs, scatter):
        for k, o in zip(keys, outs):
            (self.landed if scatter else self.w)[k] = o


W_NAMES = ("ffn1_norm", "ffn1_w_gate", "ffn1_w_up", "ffn1_w_down", "mix_norm", "w_in", "conv_w", "rel_bias", "w_branch",
           "w_merge_gate", "w_out", "ffn2_norm", "ffn2_w_gate", "ffn2_w_up", "ffn2_w_down", "final_norm")


def _as2d(a):
    return a.reshape(1, -1) if a.ndim == 1 else a.reshape(-1, a.shape[-1])


def kernel(x, ffn1_norm, ffn1_w_gate, ffn1_w_up, ffn1_w_down, mix_norm, w_in, conv_w, rel_bias, w_branch, w_merge_gate, w_out, ffn2_norm, ffn2_w_gate, ffn2_w_up, ffn2_w_down, final_norm, loss_target, m_ffn1_norm, m_ffn1_w_gate, m_ffn1_w_up, m_ffn1_w_down, m_mix_norm, m_w_in, m_conv_w, m_rel_bias, m_w_branch, m_w_merge_gate, m_w_out, m_ffn2_norm, m_ffn2_w_gate, m_ffn2_w_up, m_ffn2_w_down, m_final_norm, v_ffn1_norm, v_ffn1_w_gate, v_ffn1_w_up, v_ffn1_w_down, v_mix_norm, v_w_in, v_conv_w, v_rel_bias, v_w_branch, v_w_merge_gate, v_w_out, v_ffn2_norm, v_ffn2_w_gate, v_ffn2_w_up, v_ffn2_w_down, v_final_norm):
    given = dict(locals())
    w = {n: given[n] for n in W_NAMES}
    m = {n: given["m_" + n] for n in W_NAMES}
    v = {n: given["v_" + n] for n in W_NAMES}
    my_chip = 2 * lax.axis_index("x") + lax.axis_index("y")
    L = w_in.shape[0]

    wx = _Exchange({n: w[n] for n in BIG_NAMES})
    first = _keys(FFN1, 0)
    comm = _Pieces([wx.own(k) for k in first] + [conv_w], scatter=False)
    got = _comm_alone(comm, "gather_first")
    wx.arrived(first, got[:-1], False)
    convw_full = jnp.transpose(got[-1], (1, 2, 0, 3)).reshape(conv_w.shape[0], conv_w.shape[1], -1)

    small = {n: w[n] for n in ("ffn1_norm", "mix_norm", "ffn2_norm", "final_norm", "rel_bias")}
    loss_p, grad_x, gs = _local_step(x[0], loss_target[0], small, convw_full, wx, L)
    last = [(FFN1[-1], 0)]
    wx.arrived(last, _comm_alone(wx.pieces(last, True), "scatter_last"), True)

    sums = []
    for n in BIG_NAMES:
        acc = None
        for l in range(L):
            a = wx.landed[(n, l)]
            acc = _sum4(a.reshape(4, -1, a.shape[-1]), l, L, f"sum4_{n}_{l}", prev=acc)
        sums.append(acc.reshape(-1, acc.shape[-1]))
    others = _swap_cores(sums, "swap_cores")

    parts = [gs["ffn1_norm"].reshape(-1), gs["mix_norm"].reshape(-1), gs["ffn2_norm"].reshape(-1),
             gs["final_norm"].reshape(-1), gs["rel_bias"].reshape(-1), gs["conv_w"].reshape(-1), loss_p[0]]
    sizes = [p.shape[0] for p in parts]
    flat = jnp.concatenate(parts)
    rows = -(-flat.shape[0] // (8 * LANE)) * 8
    flat = jnp.pad(flat, (0, rows * LANE - flat.shape[0])).reshape(rows, LANE)
    red = _allreduce_small(flat, "allreduce_small").reshape(-1)
    offs = [0]
    for sz in sizes:
        offs.append(offs[-1] + sz)
    sm = {}
    for i, n in enumerate(("ffn1_norm", "mix_norm", "ffn2_norm", "final_norm", "rel_bias", "conv_w")):
        sm[n] = red[offs[i]:offs[i + 1]]
    loss = red[offs[6]]
    sm["conv_w"] = lax.dynamic_slice_in_dim(sm["conv_w"].reshape(conv_w.shape[0], conv_w.shape[1], -1),
                                            my_chip * conv_w.shape[2], conv_w.shape[2], axis=2)

    grads, deltas, new_m, new_v = {}, {}, {}, {}
    big_sum = dict(zip(BIG_NAMES, zip(sums, others)))
    for n in W_NAMES:
        shape = w[n].shape
        if n in big_sum:
            ga, gb = big_sum[n]
        else:
            ga, gb = _as2d(sm[n].reshape(shape)), None
        out = _adamw(_as2d(w[n]), ga, gb, _as2d(m[n]), _as2d(v[n]), f"adamw_{n}")
        grads[n], deltas[n], new_m[n], new_v[n] = (o.reshape(shape) for o in out)

    return (loss, grad_x[None], *[grads[n] for n in W_NAMES], *[deltas[n] for n in W_NAMES],
            *[new_m[n] for n in W_NAMES], *[new_v[n] for n in W_NAMES])
```

```python
import functools
import math

import jax
import jax.numpy as jnp
from jax import lax
from jax.experimental import pallas as pl
from jax.experimental.pallas import tpu as pltpu

F32 = jnp.float32
BF = jnp.bfloat16
MESH = pl.DeviceIdType.MESH
ARB = "arbitrary"
PAR = "parallel"

EPS = 1e-6
NEG_INF = -1e30
ROPE_BASE = 10000.0
CHUNK = 64
BRANCH_W = 512
H_RET = 4
DK_RET = 128
H_ATT = 8
DH_ATT = 64
N_PREV = 8
REL_CLIP = 128
N_REL = 2 * REL_CLIP + 1
N_SHARD = 4
LANE = 128
RET_L = 512
ATT_TQ = 128
ATT_SUB = 4
ATT_PAD = N_PREV * CHUNK
ATT_SPAN = ATT_TQ + ATT_PAD
ATT_TOEP = 2 * REL_CLIP
RB_PAD = 264
TM = 512
TM_FFN = 1024

ADAM_LR = 0.001
ADAM_B1 = 0.9
ADAM_B2 = 0.999
ADAM_EPS = 1e-08
ADAM_WD = 0.01
ADAM_STEP = 10

NT_DIMS = (((1,), (1,)), ((), ()))
TN_DIMS = (((0,), (0,)), ((), ()))


def _cp(sem, vmem_mb=48):
    return pltpu.CompilerParams(dimension_semantics=sem, vmem_limit_bytes=vmem_mb << 20)


def _sds(shape, dtype):
    return jax.ShapeDtypeStruct(tuple(shape), dtype)


def _rms_r(x):
    return lax.rsqrt(jnp.mean(x * x, axis=-1, keepdims=True) + EPS)


def _sigmoid(x):
    return 0.5 * jnp.tanh(0.5 * x) + 0.5


def _rms_bwd(dh, xv, nw):
    r = _rms_r(xv)
    xh = xv * r
    dxh = dh * nw
    dx = r * (dxh - xh * jnp.mean(dxh * xh, axis=-1, keepdims=True))
    return dx, jnp.sum(dh * xh, axis=0, keepdims=True)


def _place():
    return lax.axis_index("x"), lax.axis_index("y"), lax.axis_index("c")


def _other_chips(x, y):
    return [(1 - x, y), (x, 1 - y), (1 - x, 1 - y)]


class _Pieces:
    def __init__(self, srcs, scatter):
        self.srcs = list(srcs)
        self.scatter = scatter
        n = len(self.srcs)
        self.out_shape = [_sds(s.shape if scatter else (N_SHARD,) + s.shape, s.dtype) for s in self.srcs]
        self.scratch = [pltpu.SemaphoreType.DMA((n,)), pltpu.SemaphoreType.DMA((3, n)), pltpu.SemaphoreType.DMA((3, n))]

    def _copies(self, src, dst, sems, waiting):
        lsem, ssem, rsem = sems
        x, y, c = _place()
        mine = 2 * x + y
        n = len(src)

        def remote(j, k, chip, s_ref, d_ref):
            return pltpu.make_async_remote_copy(
                src_ref=s_ref, dst_ref=d_ref, send_sem=ssem.at[j, k], recv_sem=rsem.at[j, k],
                device_id=(chip[0], chip[1], c), device_id_type=MESH)

        chips = list(enumerate(_other_chips(x, y)))
        if self.scatter:
            local = [pltpu.make_async_copy(src[k].at[mine], dst[k].at[3], lsem.at[k]) for k in range(n)]
            sends = [remote(j, k, ch, src[k].at[2 * ch[0] + ch[1]], dst[k].at[j]) for j, ch in chips for k in range(n)]
            recvs = sends
        else:
            local = [pltpu.make_async_copy(src[k], dst[k].at[mine], lsem.at[k]) for k in range(n)]
            sends = [remote(j, k, ch, src[k], dst[k].at[mine]) for j, ch in chips for k in range(n)]
            recvs = [remote(j, k, ch, src[k], dst[k].at[2 * ch[0] + ch[1]]) for j, ch in chips for k in range(n)
                     ] if waiting else []
        return local, sends, recvs

    def start(self, src, dst, sems):
        local, sends, _ = self._copies(src, dst, sems, False)
        for cp in local + sends:
            cp.start()

    def wait(self, src, dst, sems):
        local, sends, recvs = self._copies(src, dst, sems, True)
        for cp in recvs:
            cp.wait_recv()
        for cp in sends:
            cp.wait_send()
        for cp in local:
            cp.wait()


class _HalfGather:
    def __init__(self, srcs):
        self.srcs = list(srcs)
        n = len(self.srcs)
        self.out_shape = [_sds((N_SHARD,) + s.shape, s.dtype) for s in self.srcs]
        self.scratch = [pltpu.SemaphoreType.DMA((n,))] + [pltpu.SemaphoreType.DMA((3, n)) for _ in range(4)]

    def _plan(self, src, dst, sems):
        lsem, s1, r1, s2, r2 = sems
        x, y, c = _place()
        mine = 2 * x + y
        n = len(src)
        chips = [(j, ch, 2 * ch[0] + ch[1]) for j, ch in enumerate(_other_chips(x, y))]

        def copy(s_ref, d_ref, ssem, rsem, to):
            return pltpu.make_async_remote_copy(src_ref=s_ref, dst_ref=d_ref, send_sem=ssem, recv_sem=rsem,
                                                device_id=to, device_id_type=MESH)

        local = [pltpu.make_async_copy(src[k], dst[k].at[mine], lsem.at[k]) for k in range(n)]
        sends = [copy(src[k].at[c], dst[k].at[mine, c], s1.at[j, k], r1.at[j, k], (ch[0], ch[1], c))
                 for j, ch, _ in chips for k in range(n)]
        lands = [copy(src[k].at[c], dst[k].at[slot, c], s1.at[j, k], r1.at[j, k], (ch[0], ch[1], c))
                 for j, ch, slot in chips for k in range(n)]
        passes = [copy(dst[k].at[slot, c], dst[k].at[slot, c], s2.at[j, k], r2.at[j, k], (x, y, 1 - c))
                  for j, ch, slot in chips for k in range(n)]
        gets = [copy(dst[k].at[slot, 1 - c], dst[k].at[slot, 1 - c], s2.at[j, k], r2.at[j, k], (x, y, 1 - c))
                for j, ch, slot in chips for k in range(n)]
        return local, sends, lands, passes, gets

    def start(self, src, dst, sems):
        lsem, s1, r1, s2, r2 = sems
        x, y, c = _place()
        mine = 2 * x + y
        for k in range(len(src)):
            pltpu.make_async_copy(src[k], dst[k].at[mine], lsem.at[k]).start()
        for j, ch in enumerate(_other_chips(x, y)):
            for k in range(len(src)):
                pltpu.make_async_remote_copy(
                    src_ref=src[k].at[c], dst_ref=dst[k].at[mine, c], send_sem=s1.at[j, k], recv_sem=r1.at[j, k],
                    device_id=(ch[0], ch[1], c), device_id_type=MESH).start()

    def wait(self, src, dst, sems):
        local, sends, lands, passes, gets = self._plan(src, dst, sems)
        for land, fwd in zip(lands, passes):
            land.wait_recv()
            fwd.start()
        for cp in gets:
            cp.wait_recv()
        for cp in sends + passes:
            cp.wait_send()
        for cp in local:
            cp.wait()


def _call(body, *, name, args, in_specs, out_specs, out_shape, grid=(), scratch_shapes=(), sem=None, comm=None,
          aliases=None, vmem_mb=48):
    in_specs, out_specs, out_shape = list(in_specs), list(out_specs), list(out_shape)
    scratch, args = list(scratch_shapes), list(args)
    n_in, n_out, n_scr = len(in_specs), len(out_specs), len(scratch)
    if comm is None:
        def kernel_body(*refs):
            body(*refs)
    else:
        c_in, c_out = len(comm.srcs), len(comm.out_shape)

        def kernel_body(*refs):
            o0 = n_in + c_in
            s0 = o0 + n_out + c_out
            cin, cout, sems = refs[n_in:o0], refs[o0 + n_out:s0], refs[s0 + n_scr:]
            main = refs[:n_in] + refs[o0:o0 + n_out] + refs[s0:s0 + n_scr]
            if grid:
                ids = [pl.program_id(a) for a in range(len(grid))]
                first = functools.reduce(lambda p, q: p & q, [i == 0 for i in ids])
                last = functools.reduce(lambda p, q: p & q, [i == g - 1 for i, g in zip(ids, grid)])

                @pl.when(first)
                def _():
                    comm.start(cin, cout, sems)

                body(*main)

                @pl.when(last)
                def _():
                    comm.wait(cin, cout, sems)
            else:
                comm.start(cin, cout, sems)
                body(*main)
                comm.wait(cin, cout, sems)

        hbm = pl.BlockSpec(memory_space=pl.ANY)
        in_specs += [hbm] * c_in
        out_specs += [hbm] * c_out
        out_shape += comm.out_shape
        scratch += comm.scratch
        args += comm.srcs
    params = dict(vmem_limit_bytes=vmem_mb << 20)
    if grid:
        params["dimension_semantics"] = sem
    outs = pl.pallas_call(
        kernel_body, name=name, grid=grid, in_specs=in_specs, out_specs=out_specs, out_shape=out_shape,
        scratch_shapes=scratch, input_output_aliases=aliases or {}, compiler_params=pltpu.CompilerParams(**params),
    )(*args)
    return list(outs[:n_out]), list(outs[n_out:])


def _comm_alone(comm, name):
    return _call(lambda: None, name=name, args=[], in_specs=[], out_specs=[], out_shape=[], comm=comm)[1]


def _ffn_fwd(x, nw, wg, wu, wd, name, comm=None):
    T, D = x.shape
    ns, _, fs = wg.shape
    tm = min(TM_FFN, T)

    def body(x_ref, nw_ref, wg_ref, wu_ref, wd_ref, xo_ref, g_ref, u_ref, h_s, acc_s):
        j = pl.program_id(1)

        @pl.when(j == 0)
        def _():
            xv = x_ref[...]
            h_s[...] = (xv * _rms_r(xv) * nw_ref[...]).astype(BF)
            acc_s[...] = jnp.zeros_like(acc_s)

        h = h_s[...]
        gb = jnp.dot(h, wg_ref[...], preferred_element_type=F32).astype(BF)
        ub = jnp.dot(h, wu_ref[...], preferred_element_type=F32).astype(BF)
        g_ref[...] = gb
        u_ref[...] = ub
        g = gb.astype(F32)
        a = (g * _sigmoid(g) * ub.astype(F32)).astype(BF)
        acc_s[...] += jnp.dot(a, wd_ref[...], preferred_element_type=F32)

        @pl.when(j == ns - 1)
        def _():
            xo_ref[...] = x_ref[...] + 0.5 * acc_s[...]

    wspec = pl.BlockSpec((None, D, fs), lambda i, j: (j, 0, 0))
    return _call(
        body, name=name, grid=(T // tm, ns), args=(x, nw, wg, wu, wd), comm=comm, vmem_mb=56,
        in_specs=[pl.BlockSpec((tm, D), lambda i, j: (i, 0)),
                  pl.BlockSpec((1, D), lambda i, j: (0, 0)),
                  wspec, wspec,
                  pl.BlockSpec((None, fs, D), lambda i, j: (j, 0, 0))],
        out_specs=[pl.BlockSpec((tm, D), lambda i, j: (i, 0)),
                   pl.BlockSpec((None, tm, fs), lambda i, j: (j, i, 0)),
                   pl.BlockSpec((None, tm, fs), lambda i, j: (j, i, 0))],
        out_shape=[_sds((T, D), F32), _sds((ns, T, fs), BF), _sds((ns, T, fs), BF)],
        scratch_shapes=[pltpu.VMEM((tm, D), BF), pltpu.VMEM((tm, D), F32)],
        sem=(ARB, ARB))


def _ffn_bwd_hidden(dxo, x, nw, g, u, wd, name, comm=None):
    T, D = x.shape
    ns, fs, _ = wd.shape
    tm = min(TM_FFN, T)

    def body(dxo_ref, x_ref, nw_ref, g_ref, u_ref, wd_ref, dg_ref, du_ref, a_ref, h_ref, dacc_ref, dacc_s):
        @pl.when(pl.program_id(1) == 0)
        def _():
            xv = x_ref[...]
            h_ref[...] = (xv * _rms_r(xv) * nw_ref[...]).astype(BF)
            db = (0.5 * dxo_ref[...]).astype(BF)
            dacc_ref[...] = db
            dacc_s[...] = db

        da = lax.dot_general(dacc_s[...], wd_ref[...], NT_DIMS, preferred_element_type=F32)
        gv = g_ref[...].astype(F32)
        uv = u_ref[...].astype(F32)
        s = _sigmoid(gv)
        sg = gv * s
        a_ref[...] = (sg * uv).astype(BF)
        du_ref[...] = (da * sg).astype(BF)
        dg_ref[...] = (da * uv * (s * (1.0 + gv * (1.0 - s)))).astype(BF)

    tok = pl.BlockSpec((tm, D), lambda i, j: (i, 0))
    hid = pl.BlockSpec((None, tm, fs), lambda i, j: (j, i, 0))
    return _call(
        body, name=name, grid=(T // tm, ns), args=(dxo, x, nw, g, u, wd), comm=comm, vmem_mb=56,
        in_specs=[tok, tok, pl.BlockSpec((1, D), lambda i, j: (0, 0)), hid, hid,
                  pl.BlockSpec((None, fs, D), lambda i, j: (j, 0, 0))],
        out_specs=[hid, hid, hid, tok, tok],
        out_shape=[_sds((ns, T, fs), BF)] * 3 + [_sds((T, D), BF)] * 2,
        scratch_shapes=[pltpu.VMEM((tm, D), BF)],
        sem=(ARB, ARB))


def _ffn_bwd_resid(dg, du, wg, wu, x, nw, dxo, name, comm=None):
    T, D = x.shape
    ns, _, fs = wg.shape
    tm = min(TM_FFN, T)

    def body(dg_ref, du_ref, wg_ref, wu_ref, x_ref, nw_ref, dxo_ref, dx_ref, dnw_ref, acc_s):
        i = pl.program_id(0)
        j = pl.program_id(1)
        prod = (lax.dot_general(dg_ref[...], wg_ref[...], NT_DIMS, preferred_element_type=F32)
                + lax.dot_general(du_ref[...], wu_ref[...], NT_DIMS, preferred_element_type=F32))

        @pl.when((i == 0) & (j == 0))
        def _():
            dnw_ref[...] = jnp.zeros_like(dnw_ref)

        @pl.when(j == 0)
        def _():
            acc_s[...] = prod

        @pl.when(j > 0)
        def _():
            acc_s[...] += prod

        @pl.when(j == ns - 1)
        def _():
            dx, dn = _rms_bwd(acc_s[...], x_ref[...], nw_ref[...])
            dx_ref[...] = dxo_ref[...] + dx
            dnw_ref[...] += dn

    tok = pl.BlockSpec((tm, D), lambda i, j: (i, 0))
    row = pl.BlockSpec((1, D), lambda i, j: (0, 0))
    hid = pl.BlockSpec((None, tm, fs), lambda i, j: (j, i, 0))
    wspec = pl.BlockSpec((None, D, fs), lambda i, j: (j, 0, 0))
    return _call(
        body, name=name, grid=(T // tm, ns), args=(dg, du, wg, wu, x, nw, dxo), comm=comm, vmem_mb=56,
        in_specs=[hid, hid, wspec, wspec, tok, row, tok],
        out_specs=[tok, row],
        out_shape=[_sds((T, D), F32), _sds((1, D), F32)],
        scratch_shapes=[pltpu.VMEM((tm, D), F32)],
        sem=(ARB, ARB))


def _tn(a, b, a_spec, b_spec, out_shape, out_spec, grid, name, prev=None, comm=None):
    nk = grid[-1]
    acc_shape = tuple(d for d in out_spec.block_shape if d is not None)

    def body(*refs):
        a_ref, b_ref = refs[0], refs[1]
        o_ref, acc = refs[-2], refs[-1]
        k = pl.program_id(2)
        prod = lax.dot_general(a_ref[...], b_ref[...], TN_DIMS, preferred_element_type=F32)

        @pl.when(k == 0)
        def _():
            acc[...] = prod

        @pl.when(k > 0)
        def _():
            acc[...] += prod

        @pl.when(k == nk - 1)
        def _():
            o_ref[...] = acc[...].astype(o_ref.dtype)

    in_specs = [a_spec, b_spec]
    args = [a, b]
    aliases = {}
    if prev is not None:
        in_specs.append(pl.BlockSpec(memory_space=pl.ANY))
        args.append(prev)
        aliases = {2: 0}
    main, extra = _call(
        body, name=name, grid=grid, args=args, in_specs=in_specs, out_specs=[out_spec], out_shape=[out_shape],
        scratch_shapes=[pltpu.VMEM(acc_shape, F32)], aliases=aliases, sem=(ARB, ARB, ARB), comm=comm)
    return main[0] if comm is None else (main[0], extra)


def _tn_gates(h, dbig, ns, tk, name):
    T, D = h.shape
    dq = D // ns
    nk = T // tk

    def body(a_ref, b_ref, o_ref, acc):
        k = pl.program_id(1)
        prod = lax.dot_general(a_ref[...], b_ref[...], TN_DIMS, preferred_element_type=F32)

        @pl.when(k == 0)
        def _():
            acc[...] = prod

        @pl.when(k > 0)
        def _():
            acc[...] += prod

        @pl.when(k == nk - 1)
        def _():
            for s in range(ns):
                o_ref[s] = acc[s * dq:(s + 1) * dq, :].astype(o_ref.dtype)

    return pl.pallas_call(
        body, name=name, grid=(3, nk),
        in_specs=[pl.BlockSpec((tk, D), lambda q, k: (k, 0)), pl.BlockSpec((tk, D), lambda q, k: (k, q))],
        out_specs=pl.BlockSpec((ns, None, dq, D), lambda q, k: (0, q, 0, 0)),
        out_shape=_sds((ns, 3, dq, D), BF),
        scratch_shapes=[pltpu.VMEM((D, D), F32)],
        compiler_params=_cp((PAR, ARB)),
    )(h, dbig)


def _inproj_fwd(x, nw, wbig, name):
    T, D = x.shape
    nb = wbig.shape[-1]
    tm = min(2 * TM, T)
    bn = min(2048, nb)

    def body(x_ref, nw_ref, w_ref, o_ref, h_ref, h_s):
        @pl.when(pl.program_id(1) == 0)
        def _():
            xv = x_ref[...]
            hb = (xv * _rms_r(xv) * nw_ref[...]).astype(BF)
            h_s[...] = hb
            h_ref[...] = hb

        o_ref[...] = jnp.dot(h_s[...], w_ref[...], preferred_element_type=F32).astype(BF)

    return pl.pallas_call(
        body, name=name, grid=(T // tm, nb // bn),
        in_specs=[pl.BlockSpec((tm, D), lambda i, n: (i, 0)),
                  pl.BlockSpec((1, D), lambda i, n: (0, 0)),
                  pl.BlockSpec((D, bn), lambda i, n: (0, n))],
        out_specs=[pl.BlockSpec((tm, bn), lambda i, n: (i, n)),
                   pl.BlockSpec((tm, D), lambda i, n: (i, 0))],
        out_shape=[_sds((T, nb), BF), _sds((T, D), BF)],
        scratch_shapes=[pltpu.VMEM((tm, D), BF)],
        compiler_params=_cp((PAR, ARB)),
    )(x, nw, wbig)


def _inproj_bwd(dbig, wbig, x, nw, dxin, name, comm=None):
    T, D = x.shape
    nb = wbig.shape[-1]
    tm = min(TM_FFN, T)
    tk = min(2048, nb)
    nk = nb // tk

    def body(a_ref, w_ref, x_ref, nw_ref, dxin_ref, dx_ref, dnw_ref, acc_s):
        i = pl.program_id(0)
        k = pl.program_id(1)
        prod = lax.dot_general(a_ref[...], w_ref[...], NT_DIMS, preferred_element_type=F32)

        @pl.when((i == 0) & (k == 0))
        def _():
            dnw_ref[...] = jnp.zeros_like(dnw_ref)

        @pl.when(k == 0)
        def _():
            acc_s[...] = prod

        @pl.when(k > 0)
        def _():
            acc_s[...] += prod

        @pl.when(k == nk - 1)
        def _():
            dx, dn = _rms_bwd(acc_s[...], x_ref[...], nw_ref[...])
            dx_ref[...] = dxin_ref[...] + dx
            dnw_ref[...] += dn

    tok = pl.BlockSpec((tm, D), lambda i, k: (i, 0))
    row = pl.BlockSpec((1, D), lambda i, k: (0, 0))
    return _call(
        body, name=name, grid=(T // tm, nk), args=(dbig, wbig, x, nw, dxin), comm=comm, vmem_mb=56,
        in_specs=[pl.BlockSpec((tm, tk), lambda i, k: (i, k)),
                  pl.BlockSpec((D, tk), lambda i, k: (0, k)),
                  tok, row, tok],
        out_specs=[tok, row],
        out_shape=[_sds((T, D), F32), _sds((1, D), F32)],
        scratch_shapes=[pltpu.VMEM((tm, D), F32)],
        sem=(ARB, ARB))


CONV_R = 512
CONV_BASE, CONV_GROUP = 0, 3
ATT_BASE, ATT_GROUP = 12, 3
RET_BASE, RET_GROUP = 24, 4
N_SEG = 10


def _permute_in_cols(w):
    lead = w.shape[:-1]
    w4 = w.reshape(lead + (N_SEG, BRANCH_W // LANE, LANE))

    def grouped(lo, hi):
        return jnp.swapaxes(w4[..., lo:hi, :, :], -3, -2).reshape(lead + (-1,))

    return jnp.concatenate([grouped(0, 3), grouped(7, 10), grouped(3, 7)], axis=-1)


def _unpermute_in_cols(w):
    lead = w.shape[:-1]
    nblk = BRANCH_W // LANE

    def segs(lo, n):
        part = w[..., lo * LANE:(lo + nblk * n) * LANE].reshape(lead + (nblk, n, LANE))
        return jnp.swapaxes(part, -3, -2)

    conv, att, ret = segs(CONV_BASE, 3), segs(ATT_BASE, 3), segs(RET_BASE, 4)
    return jnp.concatenate([conv, ret, att], axis=-3).reshape(lead + (-1,))


def _seg0(big):
    return (big.shape[1] - N_SEG * BRANCH_W) // LANE


def _group_spec(big, base, group, rows, where):
    first = (_seg0(big) + base) // group
    assert first * group == _seg0(big) + base

    def index(*ids):
        r, g = where(*ids)
        return r, first + g

    return pl.BlockSpec((rows, group * LANE), index)


CU, CB, CC = (slice(k * LANE, (k + 1) * LANE) for k in range(3))
AQ, AK, AV = CU, CB, CC
RQ, RK, RV, RG = (slice(k * LANE, (k + 1) * LANE) for k in range(4))


def _conv_fwd(big, cw, name):
    T = big.shape[0]
    R = min(CONV_R, T)

    def body(g_ref, w_ref, y_ref, z_s):
        z_s[pl.ds(0, 8), :] = jnp.zeros((8, LANE), F32)

        def fill(t, c):
            sl = pl.ds(pl.multiple_of(t * R, R), R)
            z_s[pl.ds(pl.multiple_of(t * R + 8, 8), R), :] = g_ref[sl, CC].astype(F32) * g_ref[sl, CU].astype(F32)
            return c

        lax.fori_loop(0, T // R, fill, 0)
        w0, w1, w2 = w_ref[0:1, :], w_ref[1:2, :], w_ref[2:3, :]

        def step(t, c):
            zz = z_s[pl.ds(pl.multiple_of(t * R, R), R + 8), :]
            z0 = zz[8:]
            z1 = pltpu.roll(zz, 1, 0)[8:]
            z2 = pltpu.roll(zz, 2, 0)[8:]
            sl = pl.ds(pl.multiple_of(t * R, R), R)
            y_ref[sl, :] = (g_ref[sl, CB].astype(F32) * (w2 * z0 + w1 * z1 + w0 * z2)).astype(BF)
            return c

        lax.fori_loop(0, T // R, step, 0)

    return pl.pallas_call(
        body, name=name, grid=(BRANCH_W // LANE,),
        in_specs=[_group_spec(big, CONV_BASE, CONV_GROUP, T, lambda j: (0, j)),
                  pl.BlockSpec((3, LANE), lambda j: (0, j))],
        out_specs=pl.BlockSpec((T, LANE), lambda j: (0, j)),
        out_shape=_sds((T, BRANCH_W), BF),
        scratch_shapes=[pltpu.VMEM((T + 8, LANE), F32)],
        compiler_params=_cp((PAR,)),
    )(big, cw)


def _conv_bwd(big, dy, cw, dbig, name):
    T = big.shape[0]
    R = min(CONV_R, T)

    def body(g_ref, dy_ref, w_ref, _, o_ref, dw_ref, z_s, d_s):
        z_s[pl.ds(0, 8), :] = jnp.zeros((8, LANE), F32)
        d_s[pl.ds(T, 8), :] = jnp.zeros((8, LANE), F32)

        def fill(t, c):
            sl = pl.ds(pl.multiple_of(t * R, R), R)
            z_s[pl.ds(pl.multiple_of(t * R + 8, 8), R), :] = g_ref[sl, CC].astype(F32) * g_ref[sl, CU].astype(F32)
            d_s[sl, :] = dy_ref[sl, :].astype(F32) * g_ref[sl, CB].astype(F32)
            return c

        lax.fori_loop(0, T // R, fill, 0)
        w0, w1, w2 = w_ref[0:1, :], w_ref[1:2, :], w_ref[2:3, :]

        def step(t, carry):
            a0, a1, a2 = carry
            zz = z_s[pl.ds(pl.multiple_of(t * R, R), R + 8), :]
            z0 = zz[8:]
            z1 = pltpu.roll(zz, 1, 0)[8:]
            z2 = pltpu.roll(zz, 2, 0)[8:]
            sl = pl.ds(pl.multiple_of(t * R, R), R)
            dyv = dy_ref[sl, :].astype(F32)
            o_ref[sl, CB] = (dyv * (w2 * z0 + w1 * z1 + w0 * z2)).astype(BF)
            dd = d_s[pl.ds(pl.multiple_of(t * R, R), R + 8), :]
            d0 = dd[:R]
            d1 = pltpu.roll(dd, R + 7, 0)[:R]
            d2 = pltpu.roll(dd, R + 6, 0)[:R]
            dz = w2 * d0 + w1 * d1 + w0 * d2
            o_ref[sl, CC] = (dz * g_ref[sl, CU].astype(F32)).astype(BF)
            o_ref[sl, CU] = (dz * g_ref[sl, CC].astype(F32)).astype(BF)
            a0 = a0 + jnp.sum(d0 * z2, axis=0, keepdims=True)
            a1 = a1 + jnp.sum(d0 * z1, axis=0, keepdims=True)
            a2 = a2 + jnp.sum(d0 * z0, axis=0, keepdims=True)
            return a0, a1, a2

        zero = jnp.zeros((1, LANE), F32)
        a0, a1, a2 = lax.fori_loop(0, T // R, step, (zero, zero, zero))
        dw_ref[0:1, :] = a0
        dw_ref[1:2, :] = a1
        dw_ref[2:3, :] = a2

    group = _group_spec(big, CONV_BASE, CONV_GROUP, T, lambda j: (0, j))
    w = pl.BlockSpec((3, LANE), lambda j: (0, j))
    return pl.pallas_call(
        body, name=name, grid=(BRANCH_W // LANE,),
        in_specs=[group, pl.BlockSpec((T, LANE), lambda j: (0, j)), w, pl.BlockSpec(memory_space=pl.ANY)],
        out_specs=[group, w],
        out_shape=[_sds(dbig.shape, BF), _sds((3, BRANCH_W), F32)],
        scratch_shapes=[pltpu.VMEM((T + 8, LANE), F32), pltpu.VMEM((T + 8, LANE), F32)],
        input_output_aliases={3: 0}, compiler_params=_cp((PAR,)),
    )(big, dy, cw, dbig)


def _ret_tables(T):
    L = min(RET_L, T)
    hh = jnp.arange(H_RET, dtype=F32)
    lg = jnp.log1p(-jnp.exp2(-5.0 - hh))
    n = jnp.arange(L, dtype=F32)
    a = jnp.exp(lg[:, None] * (n + 1.0))
    b = jnp.exp(lg[:, None] * (L - 1.0 - n))
    gl = jnp.exp(lg * L)
    ch = jnp.arange(L) // CHUNK
    m = jnp.exp(lg[:, None, None] * jnp.abs(n[:, None] - n[None, :])) * (ch[None, :] <= ch[:, None]).astype(F32)
    inv_freq = ROPE_BASE ** (-jnp.linspace(0.0, 1.0, DK_RET // 2, dtype=F32))
    ang = jnp.arange(T, dtype=F32)[:, None] * inv_freq[None, :]
    cos, sin = jnp.cos(ang), jnp.sin(ang)
    return dict(
        L=L, M=m,
        a=jnp.broadcast_to(a[:, :, None], (H_RET, L, DK_RET)),
        b=jnp.broadcast_to(b[:, :, None], (H_RET, L, DK_RET)),
        gl=jnp.broadcast_to(gl[:, None, None], (H_RET, 1, DK_RET)),
        cos=jnp.concatenate([cos, cos], axis=-1), sin=jnp.concatenate([-sin, sin], axis=-1))


def _rot(x, cs, sn):
    return x * cs + pltpu.roll(x, DK_RET // 2, 1) * sn


def _unrot(dy, cs, sn):
    return dy * cs + pltpu.roll(dy * sn, DK_RET // 2, 1)


def _ret_fwd(big, tb, name, comm=None):
    T = big.shape[0]
    L = tb["L"]
    nsc = T // L
    scale = DK_RET ** -0.5

    def body(x_ref, cos_ref, sin_ref, m_ref, a_ref, b_ref, gl_ref, y_ref, o_ref, st_ref, s_s):
        @pl.when(pl.program_id(1) == 0)
        def _():
            s_s[...] = jnp.zeros_like(s_s)

        cs, sn = cos_ref[...], sin_ref[...]
        qt = _rot(x_ref[:, RQ].astype(F32), cs, sn) * scale
        kt = _rot(x_ref[:, RK].astype(F32), cs, sn)
        qb, kb, vb = qt.astype(BF), kt.astype(BF), x_ref[:, RV]
        s_prev = s_s[...]
        st_ref[...] = s_prev
        p = lax.dot_general(qb, kb, NT_DIMS, preferred_element_type=F32) * m_ref[...]
        o = (jnp.dot(p.astype(BF), vb, preferred_element_type=F32)
             + jnp.dot((qt * a_ref[...]).astype(BF), s_prev.astype(BF), preferred_element_type=F32))
        s_s[...] = s_prev * gl_ref[...] + lax.dot_general((kt * b_ref[...]).astype(BF), vb, TN_DIMS,
                                                         preferred_element_type=F32)
        o_ref[...] = o
        gv = x_ref[:, RG].astype(F32)
        y_ref[...] = (gv * _sigmoid(gv) * o * _rms_r(o)).astype(BF)

    tab = pl.BlockSpec((L, DK_RET), lambda h, i: (i, 0))
    per_head = pl.BlockSpec((None, L, DK_RET), lambda h, i: (h, 0, 0))
    out = pl.BlockSpec((L, LANE), lambda h, i: (i, h))
    return _call(
        body, name=name, grid=(H_RET, nsc), comm=comm,
        args=(big, tb["cos"], tb["sin"], tb["M"], tb["a"], tb["b"], tb["gl"]),
        in_specs=[_group_spec(big, RET_BASE, RET_GROUP, L, lambda h, i: (i, h)), tab, tab,
                  pl.BlockSpec((None, L, L), lambda h, i: (h, 0, 0)), per_head, per_head,
                  pl.BlockSpec((None, 1, DK_RET), lambda h, i: (h, 0, 0))],
        out_specs=[out, out, pl.BlockSpec((None, None, DK_RET, DK_RET), lambda h, i: (i, h, 0, 0))],
        out_shape=[_sds((T, BRANCH_W), BF), _sds((T, BRANCH_W), F32), _sds((nsc, H_RET, DK_RET, DK_RET), F32)],
        scratch_shapes=[pltpu.VMEM((DK_RET, DK_RET), F32)],
        sem=(ARB, ARB))


def _ret_bwd(big, o, st, dy, tb, dbig, name):
    T = big.shape[0]
    L = tb["L"]
    nsc = T // L
    scale = DK_RET ** -0.5

    def body(x_ref, cos_ref, sin_ref, m_ref, a_ref, b_ref, gl_ref, o_ref, st_ref, dy_ref, _, d_ref, ds_s):
        @pl.when(pl.program_id(1) == 0)
        def _():
            ds_s[...] = jnp.zeros_like(ds_s)

        cs, sn = cos_ref[...], sin_ref[...]
        mm, av, bv = m_ref[...], a_ref[...], b_ref[...]
        qt = _rot(x_ref[:, RQ].astype(F32), cs, sn) * scale
        kt = _rot(x_ref[:, RK].astype(F32), cs, sn)
        qb, kb, vb = qt.astype(BF), kt.astype(BF), x_ref[:, RV]
        pb = (lax.dot_general(qb, kb, NT_DIMS, preferred_element_type=F32) * mm).astype(BF)
        ov = o_ref[...]
        r = _rms_r(ov)
        oh = ov * r
        gv = x_ref[:, RG].astype(F32)
        sg = _sigmoid(gv)
        dyv = dy_ref[...].astype(F32)
        d_ref[:, RG] = (dyv * oh * (sg * (1.0 + gv * (1.0 - sg)))).astype(BF)
        doh = dyv * gv * sg
        dob = (r * (doh - oh * jnp.mean(doh * oh, axis=-1, keepdims=True))).astype(BF)
        dsb = ds_s[...].astype(BF)
        spb = st_ref[...].astype(BF)
        dpb = (lax.dot_general(dob, vb, NT_DIMS, preferred_element_type=F32) * mm).astype(BF)
        dqt = (jnp.dot(dpb, kb, preferred_element_type=F32)
               + lax.dot_general(dob, spb, NT_DIMS, preferred_element_type=F32) * av)
        dkt = (lax.dot_general(dpb, qb, TN_DIMS, preferred_element_type=F32)
               + lax.dot_general(vb, dsb, NT_DIMS, preferred_element_type=F32) * bv)
        dv = (lax.dot_general(pb, dob, TN_DIMS, preferred_element_type=F32)
              + jnp.dot((kt * bv).astype(BF), dsb, preferred_element_type=F32))
        ds_s[...] = ds_s[...] * gl_ref[...] + lax.dot_general((qt * av).astype(BF), dob, TN_DIMS,
                                                              preferred_element_type=F32)
        d_ref[:, RQ] = (_unrot(dqt, cs, sn) * scale).astype(BF)
        d_ref[:, RK] = _unrot(dkt, cs, sn).astype(BF)
        d_ref[:, RV] = dv.astype(BF)

    def rev(i):
        return nsc - 1 - i

    group = _group_spec(big, RET_BASE, RET_GROUP, L, lambda h, i: (rev(i), h))
    tab = pl.BlockSpec((L, DK_RET), lambda h, i: (rev(i), 0))
    per_head = pl.BlockSpec((None, L, DK_RET), lambda h, i: (h, 0, 0))
    out = pl.BlockSpec((L, LANE), lambda h, i: (rev(i), h))
    return pl.pallas_call(
        body, name=name, grid=(H_RET, nsc),
        in_specs=[group, tab, tab,
                  pl.BlockSpec((None, L, L), lambda h, i: (h, 0, 0)), per_head, per_head,
                  pl.BlockSpec((None, 1, DK_RET), lambda h, i: (h, 0, 0)),
                  out, pl.BlockSpec((None, None, DK_RET, DK_RET), lambda h, i: (rev(i), h, 0, 0)), out,
                  pl.BlockSpec(memory_space=pl.ANY)],
        out_specs=group,
        out_shape=_sds(dbig.shape, BF),
        scratch_shapes=[pltpu.VMEM((DK_RET, DK_RET), F32)],
        input_output_aliases={10: 0}, compiler_params=_cp((PAR, ARB)),
    )(big, tb["cos"], tb["sin"], tb["M"], tb["a"], tb["b"], tb["gl"], o, st, dy, dbig)


def _relbias_onehot(n):
    mm = lax.broadcasted_iota(jnp.int32, (RB_PAD, ATT_TOEP), 1)
    rr = lax.broadcasted_iota(jnp.int32, (RB_PAD, ATT_TOEP), 0)
    idx = jnp.clip(n + ATT_TOEP - mm, 0, 2 * REL_CLIP)
    return (rr == idx).astype(F32)


def _split3(x):
    hi = x.astype(BF).astype(F32)
    mid = (x - hi).astype(BF).astype(F32)
    lo = x - hi - mid
    return jnp.concatenate([hi, mid, lo], axis=0).astype(BF)


def _join3(y):
    k = y.shape[0] // 3
    return (y[:k] + y[k:2 * k]) + y[2 * k:]


def _relbias_expand(rbp, name):
    far = ATT_SPAN - ATT_TOEP

    def body(rb_ref, o_ref):
        rb = rb_ref[...]
        const = jnp.broadcast_to(rb[:, 2 * REL_CLIP:2 * REL_CLIP + 1], (H_ATT, far))

        rb3 = _split3(rb)

        def row(n, c):
            toep = _join3(jnp.dot(rb3, _relbias_onehot(n).astype(BF), preferred_element_type=F32))
            m = lax.broadcasted_iota(jnp.int32, (1, ATT_SPAN), 1)
            d = n // CHUNK + N_PREV - m // CHUNK
            neg = jnp.where((d >= 0) & (d <= N_PREV), 0.0, NEG_INF).astype(F32)
            o_ref[n] = jnp.concatenate([const, toep], axis=1) + neg
            return c

        lax.fori_loop(0, ATT_TQ, row, 0)

    return pl.pallas_call(
        body, name=name,
        in_specs=[pl.BlockSpec(memory_space=pltpu.VMEM)],
        out_specs=pl.BlockSpec(memory_space=pltpu.VMEM),
        out_shape=_sds((ATT_TQ, H_ATT, ATT_SPAN), F32),
    )(rbp)


def _relbias_grad(dbt, name):
    far = ATT_SPAN - ATT_TOEP

    def body(d_ref, o_ref):
        def row(n, carry):
            acc, cs = carry
            dn = d_ref[n]
            acc = acc + _join3(lax.dot_general(_split3(dn[:, far:]), _relbias_onehot(n).astype(BF), NT_DIMS,
                                               preferred_element_type=F32))
            cs = cs + jnp.sum(dn[:, :far], axis=1, keepdims=True)
            return acc, cs

        acc, cs = lax.fori_loop(0, ATT_TQ, row, (jnp.zeros((H_ATT, RB_PAD), F32), jnp.zeros((H_ATT, 1), F32)))
        rr = lax.broadcasted_iota(jnp.int32, (H_ATT, RB_PAD), 1)
        o_ref[...] = acc + jnp.where(rr == 2 * REL_CLIP, cs, 0.0)

    return pl.pallas_call(
        body, name=name,
        in_specs=[pl.BlockSpec(memory_space=pltpu.VMEM)],
        out_specs=pl.BlockSpec(memory_space=pltpu.VMEM),
        out_shape=_sds((H_ATT, RB_PAD), F32),
    )(dbt)


def _att_pad_fill(dst_s, src_ref, cols, T):
    dst_s[pl.ds(0, ATT_PAD), :] = jnp.zeros((ATT_PAD, LANE), dst_s.dtype)
    R = min(512, T)

    def cp(t, c):
        dst_s[pl.ds(pl.multiple_of(ATT_PAD + t * R, LANE), R), :] = src_ref[pl.ds(pl.multiple_of(t * R, R), R), cols]
        return c

    lax.fori_loop(0, T // R, cp, 0)


ATT_WIN = ATT_SUB * ATT_TQ + ATT_PAD


def _att_probs(s_full, sub, bias, t0):
    s = s_full[sub * ATT_TQ:(sub + 1) * ATT_TQ, sub * ATT_TQ:sub * ATT_TQ + ATT_SPAN] * (DH_ATT ** -0.5) + bias
    key_pos = t0 + sub * ATT_TQ - ATT_PAD + lax.broadcasted_iota(jnp.int32, (1, ATT_SPAN), 1)
    s = jnp.where(key_pos >= 0, s, NEG_INF)
    p = jnp.exp(s - jnp.max(s, axis=-1, keepdims=True))
    return p * (1.0 / jnp.sum(p, axis=-1, keepdims=True))


def _att_band(tiles):
    rows = []
    for sub, t in enumerate(tiles):
        parts = []
        if sub:
            parts.append(jnp.zeros((ATT_TQ, sub * ATT_TQ), BF))
        parts.append(t)
        if sub < ATT_SUB - 1:
            parts.append(jnp.zeros((ATT_TQ, (ATT_SUB - 1 - sub) * ATT_TQ), BF))
        rows.append(jnp.concatenate(parts, axis=1))
    return jnp.concatenate(rows, axis=0)


def _att_head_masks(x):
    first = lax.broadcasted_iota(jnp.int32, (1, LANE), 1) < DH_ATT
    zero = jnp.zeros_like(x)
    return first, (jnp.where(first, x, zero), jnp.where(first, zero, x))


def _att_fwd(big, bias, name, comm=None):
    T = big.shape[0]
    rows = ATT_SUB * ATT_TQ
    nt = T // rows

    def body(x_ref, b_ref, y_ref, kp_s, vp_s):
        i = pl.program_id(1)

        @pl.when(i == 0)
        def _():
            _att_pad_fill(kp_s, x_ref, AK, T)
            _att_pad_fill(vp_s, x_ref, AV, T)

        t0 = pl.multiple_of(i * rows, rows)
        kw = kp_s[pl.ds(t0, ATT_WIN), :]
        vw = vp_s[pl.ds(t0, ATT_WIN), :]
        first, qm = _att_head_masks(x_ref[pl.ds(t0, rows), AQ])
        outs = []
        for hh in range(2):
            s_full = lax.dot_general(qm[hh], kw, NT_DIMS, preferred_element_type=F32)
            band = _att_band([_att_probs(s_full, sub, b_ref[hh], t0).astype(BF) for sub in range(ATT_SUB)])
            outs.append(jnp.dot(band, vw, preferred_element_type=F32))
        y_ref[...] = jnp.where(first, outs[0], outs[1]).astype(BF)

    return _call(
        body, name=name, grid=(H_ATT // 2, nt), args=(big, bias), comm=comm,
        in_specs=[_group_spec(big, ATT_BASE, ATT_GROUP, T, lambda p, i: (0, p)),
                  pl.BlockSpec((2, ATT_TQ, ATT_SPAN), lambda p, i: (p, 0, 0))],
        out_specs=[pl.BlockSpec((rows, LANE), lambda p, i: (i, p))],
        out_shape=[_sds((T, BRANCH_W), BF)],
        scratch_shapes=[pltpu.VMEM((T + ATT_PAD, LANE), BF), pltpu.VMEM((T + ATT_PAD, LANE), BF)],
        sem=(ARB, ARB))


def _att_bwd(big, bias, dy, dbig, name, comm=None):
    T = big.shape[0]
    rows = ATT_SUB * ATT_TQ
    nt = T // rows
    scale = DH_ATT ** -0.5

    def body(x_ref, b_ref, dy_ref, _, d_ref, db_ref, kp_s, vp_s, dk_s, dv_s):
        i = pl.program_id(1)

        @pl.when(i == 0)
        def _():
            _att_pad_fill(kp_s, x_ref, AK, T)
            _att_pad_fill(vp_s, x_ref, AV, T)
            dk_s[...] = jnp.zeros_like(dk_s)
            dv_s[...] = jnp.zeros_like(dv_s)
            db_ref[...] = jnp.zeros_like(db_ref)

        t0 = pl.multiple_of(i * rows, rows)
        win = pl.ds(t0, ATT_WIN)
        kw = kp_s[win, :]
        vw = vp_s[win, :]
        first, qm = _att_head_masks(x_ref[pl.ds(t0, rows), AQ])
        _, dom = _att_head_masks(dy_ref[...])
        dqs, dkt, dvt = [], None, None
        for hh in range(2):
            s_full = lax.dot_general(qm[hh], kw, NT_DIMS, preferred_element_type=F32)
            dp_full = lax.dot_general(dom[hh], vw, NT_DIMS, preferred_element_type=F32)
            ps, dss, db = [], [], None
            for sub in range(ATT_SUB):
                pn = _att_probs(s_full, sub, b_ref[hh], t0)
                dp = dp_full[sub * ATT_TQ:(sub + 1) * ATT_TQ, sub * ATT_TQ:sub * ATT_TQ + ATT_SPAN]
                ds = pn * (dp - jnp.sum(dp * pn, axis=-1, keepdims=True))
                db = ds if db is None else db + ds
                ps.append(pn.astype(BF))
                dss.append(ds.astype(BF))
            db_ref[hh] += db
            ds_band, p_band = _att_band(dss), _att_band(ps)
            dqs.append(jnp.dot(ds_band, kw, preferred_element_type=F32))
            qt = jnp.transpose(qm[hh].astype(F32)).astype(BF)
            dot_ = jnp.transpose(dom[hh].astype(F32)).astype(BF)
            dk_h = jnp.dot(qt, ds_band, preferred_element_type=F32)
            dv_h = jnp.dot(dot_, p_band, preferred_element_type=F32)
            dkt = dk_h if dkt is None else dkt + dk_h
            dvt = dv_h if dvt is None else dvt + dv_h
        d_ref[pl.ds(t0, rows), AQ] = (jnp.where(first, dqs[0], dqs[1]) * scale).astype(BF)
        dk_s[win, :] += jnp.transpose(dkt) * scale
        dv_s[win, :] += jnp.transpose(dvt)

        @pl.when(i == nt - 1)
        def _():
            R = min(512, T)

            def cp(t, c):
                src = pl.ds(pl.multiple_of(ATT_PAD + t * R, LANE), R)
                dst = pl.ds(pl.multiple_of(t * R, R), R)
                d_ref[dst, AK] = dk_s[src, :].astype(BF)
                d_ref[dst, AV] = dv_s[src, :].astype(BF)
                return c

            lax.fori_loop(0, T // R, cp, 0)

    group = _group_spec(big, ATT_BASE, ATT_GROUP, T, lambda p, i: (0, p))
    tile = pl.BlockSpec((rows, LANE), lambda p, i: (i, p))
    bspec = pl.BlockSpec((2, ATT_TQ, ATT_SPAN), lambda p, i: (p, 0, 0))
    return _call(
        body, name=name, grid=(H_ATT // 2, nt), args=(big, bias, dy, dbig), comm=comm, aliases={3: 0}, vmem_mb=56,
        in_specs=[group, bspec, tile, pl.BlockSpec(memory_space=pl.ANY)],
        out_specs=[group, bspec],
        out_shape=[_sds(dbig.shape, BF), _sds((H_ATT, ATT_TQ, ATT_SPAN), F32)],
        scratch_shapes=[pltpu.VMEM((T + ATT_PAD, LANE), BF), pltpu.VMEM((T + ATT_PAD, LANE), BF),
                        pltpu.VMEM((T + ATT_PAD, LANE), F32), pltpu.VMEM((T + ATT_PAD, LANE), F32)],
        sem=(ARB, ARB))


def _merge_fwd(x1, big, ys, wb, wo, name):
    T, D = x1.shape
    tm = min(TM, T)

    def body(x_ref, gp_ref, yc_ref, yr_ref, ya_ref, wb_ref, wo_ref, x2_ref, p_ref, mg_ref):
        merged = jnp.zeros((tm, D), F32)
        for i, y_ref in enumerate((yc_ref, yr_ref, ya_ref)):
            cols = slice(i * D, (i + 1) * D)
            pb = jnp.dot(y_ref[...], wb_ref[i], preferred_element_type=F32).astype(BF)
            p_ref[:, cols] = pb
            merged = merged + _sigmoid(gp_ref[:, cols].astype(F32)) * pb.astype(F32)
        mb = merged.astype(BF)
        mg_ref[...] = mb
        x2_ref[...] = x_ref[...] + jnp.dot(mb, wo_ref[...], preferred_element_type=F32)

    tok = pl.BlockSpec((tm, D), lambda i: (i, 0))
    wide = pl.BlockSpec((tm, 3 * D), lambda i: (i, 0))
    yspec = pl.BlockSpec((tm, BRANCH_W), lambda i: (i, 0))
    return pl.pallas_call(
        body, name=name, grid=(T // tm,),
        in_specs=[tok, wide, yspec, yspec, yspec,
                  pl.BlockSpec((3, BRANCH_W, D), lambda i: (0, 0, 0)),
                  pl.BlockSpec((D, D), lambda i: (0, 0))],
        out_specs=[tok, wide, tok],
        out_shape=[_sds((T, D), F32), _sds((T, 3 * D), BF), _sds((T, D), BF)],
        compiler_params=_cp((PAR,)),
    )(x1, big, *ys, wb, wo)


def _merge_bwd(dx2, big, p, wb, wo, name):
    T, D = dx2.shape
    tm = min(TM, T)

    def body(dx_ref, gp_ref, p_ref, wb_ref, wo_ref, dp_ref, dgp_ref, dyc_ref, dyr_ref, dya_ref, dxb_ref):
        dxb = dx_ref[...].astype(BF)
        dxb_ref[...] = dxb
        dm = lax.dot_general(dxb, wo_ref[...], NT_DIMS, preferred_element_type=F32)
        for i, dy_ref in enumerate((dyc_ref, dyr_ref, dya_ref)):
            cols = slice(i * D, (i + 1) * D)
            gt = _sigmoid(gp_ref[:, cols].astype(F32))
            dpb = (dm * gt).astype(BF)
            dp_ref[:, cols] = dpb
            dgp_ref[:, cols] = (dm * p_ref[:, cols].astype(F32) * gt * (1.0 - gt)).astype(BF)
            dy_ref[...] = lax.dot_general(dpb, wb_ref[i], NT_DIMS, preferred_element_type=F32).astype(BF)

    tok = pl.BlockSpec((tm, D), lambda i: (i, 0))
    wide = pl.BlockSpec((tm, 3 * D), lambda i: (i, 0))
    yspec = pl.BlockSpec((tm, BRANCH_W), lambda i: (i, 0))
    return pl.pallas_call(
        body, name=name, grid=(T // tm,),
        in_specs=[tok, wide, wide,
                  pl.BlockSpec((3, BRANCH_W, D), lambda i: (0, 0, 0)),
                  pl.BlockSpec((D, D), lambda i: (0, 0))],
        out_specs=[wide, wide, yspec, yspec, yspec, tok],
        out_shape=[_sds((T, 3 * D), BF), _sds(big.shape, BF)] + [_sds((T, BRANCH_W), BF)] * 3 + [_sds((T, D), BF)],
        compiler_params=_cp((PAR,)),
    )(dx2, big, p, wb, wo)


def _loss_head(x, tgt, fw, name):
    T, D = x.shape
    tm = min(TM, T)

    def body(x_ref, t_ref, w_ref, loss_ref, dx_ref, dw_ref):
        @pl.when(pl.program_id(0) == 0)
        def _():
            loss_ref[...] = jnp.zeros_like(loss_ref)
            dw_ref[...] = jnp.zeros_like(dw_ref)

        xv = x_ref[...]
        wv = w_ref[...]
        e = xv * _rms_r(xv) * wv - t_ref[...]
        loss_ref[...] += 0.5 * jnp.sum(jnp.mean(e * e, axis=-1, keepdims=True))
        dx, dn = _rms_bwd(e * (1.0 / D), xv, wv)
        dx_ref[...] = dx
        dw_ref[...] += dn

    tok = pl.BlockSpec((tm, D), lambda i: (i, 0))
    return pl.pallas_call(
        body, name=name, grid=(T // tm,),
        in_specs=[tok, tok, pl.BlockSpec((1, D), lambda i: (0, 0))],
        out_specs=[pl.BlockSpec((8, LANE), lambda i: (0, 0)), tok, pl.BlockSpec((1, D), lambda i: (0, 0))],
        out_shape=[_sds((8, LANE), F32), _sds((T, D), F32), _sds((1, D), F32)],
        compiler_params=_cp((ARB,)),
    )(x, tgt, fw)


def _block_rows(rows, cols):
    cap = max(8, (1 << 18) // cols)
    best = None
    for r in range(8, rows + 1, 8):
        if rows % r == 0 and r <= cap:
            best = r
    return best if best is not None else rows


def _sum4(land, l, n_layers, name, prev=None):
    _, rows, cols = land.shape
    br = _block_rows(rows, cols)

    def body(*refs):
        l_ref, o_ref = refs[0], refs[-1]
        o_ref[...] = ((l_ref[3].astype(F32) + l_ref[0].astype(F32)) + l_ref[1].astype(F32)) + l_ref[2].astype(F32)

    in_specs = [pl.BlockSpec((4, br, cols), lambda i: (0, i, 0))]
    args = [land]
    aliases = {}
    if prev is not None:
        in_specs.append(pl.BlockSpec(memory_space=pl.ANY))
        args.append(prev)
        aliases = {1: 0}
    return pl.pallas_call(
        body, name=name, grid=(rows // br,), in_specs=in_specs,
        out_specs=pl.BlockSpec((None, br, cols), lambda i: (l, i, 0)),
        out_shape=_sds((n_layers, rows, cols), F32),
        input_output_aliases=aliases, compiler_params=_cp((PAR,)),
    )(*args)


def _adamw_math(w, g, m, v):
    m = ADAM_B1 * m + (1.0 - ADAM_B1) * g
    v = ADAM_B2 * v + (1.0 - ADAM_B2) * (g * g)
    m_hat = m / (1.0 - ADAM_B1 ** ADAM_STEP)
    v_hat = v / (1.0 - ADAM_B2 ** ADAM_STEP)
    delta = -ADAM_LR * (m_hat / (jnp.sqrt(v_hat) + ADAM_EPS) + ADAM_WD * w)
    return delta, m, v


def _adamw(w, ga, gb, m, v, name):
    rows, cols = w.shape
    br = _block_rows(rows, cols)
    two = gb is not None

    def body(*refs):
        if two:
            w_ref, ga_ref, gb_ref, m_ref, v_ref, g_ref, d_ref, nm_ref, nv_ref = refs
            g = ga_ref[...] + gb_ref[...]
        else:
            w_ref, ga_ref, m_ref, v_ref, g_ref, d_ref, nm_ref, nv_ref = refs
            g = ga_ref[...]
        d, nm, nv = _adamw_math(w_ref[...], g, m_ref[...], v_ref[...])
        g_ref[...] = g
        d_ref[...] = d
        nm_ref[...] = nm
        nv_ref[...] = nv

    blk = pl.BlockSpec((br, cols), lambda i: (i, 0))
    args = [w, ga] + ([gb] if two else []) + [m, v]
    return pl.pallas_call(
        body, name=name, grid=(rows // br,),
        in_specs=[blk] * len(args), out_specs=[blk] * 4,
        out_shape=[_sds((rows, cols), F32)] * 4,
        compiler_params=_cp((PAR,)),
    )(*args)


def _swap_cores(vs, name):
    n = len(vs)

    def body(*refs):
        v_refs, o_refs = refs[:n], refs[n:2 * n]
        ssem, rsem = refs[2 * n:]
        x, y, c = _place()
        copies = [pltpu.make_async_remote_copy(
            src_ref=v_refs[k], dst_ref=o_refs[k], send_sem=ssem.at[k], recv_sem=rsem.at[k],
            device_id=(x, y, 1 - c), device_id_type=MESH) for k in range(n)]
        for cp in copies:
            cp.start()
        for cp in copies:
            cp.wait()

    hbm = pl.BlockSpec(memory_space=pl.ANY)
    return pl.pallas_call(
        body, name=name,
        in_specs=[hbm] * n, out_specs=[hbm] * n,
        out_shape=[_sds(v.shape, v.dtype) for v in vs],
        scratch_shapes=[pltpu.SemaphoreType.DMA((n,)), pltpu.SemaphoreType.DMA((n,))],
    )(*vs)


def _allreduce_small(v, name):
    rows = v.shape[0]
    flips = [(fx, fy, fc) for fx in (0, 1) for fy in (0, 1) for fc in (0, 1) if fx or fy or fc]

    def body(v_ref, o_ref, all_s, ssem, rsem):
        x, y, c = _place()

        def peer(f):
            return (x + f[0] - 2 * x * f[0], y + f[1] - 2 * y * f[1], c + f[2] - 2 * c * f[2])

        def slot(p):
            return all_s.at[4 * p[0] + 2 * p[1] + p[2]]

        def copy(k, f, owner):
            return pltpu.make_async_remote_copy(
                src_ref=v_ref, dst_ref=slot(owner), send_sem=ssem.at[k], recv_sem=rsem.at[k],
                device_id=peer(f), device_id_type=MESH)

        sends = [copy(k, f, (x, y, c)) for k, f in enumerate(flips)]
        for cp in sends:
            cp.start()
        all_s[4 * x + 2 * y + c] = v_ref[...]
        for k, f in enumerate(flips):
            copy(k, f, peer(f)).wait_recv()
        for cp in sends:
            cp.wait_send()
        acc = all_s[0]
        for d in range(1, 8):
            acc = acc + all_s[d]
        o_ref[...] = acc

    return pl.pallas_call(
        body, name=name,
        in_specs=[pl.BlockSpec(memory_space=pltpu.VMEM)],
        out_specs=pl.BlockSpec(memory_space=pltpu.VMEM),
        out_shape=_sds((rows, LANE), F32),
        scratch_shapes=[pltpu.VMEM((8, rows, LANE), F32), pltpu.SemaphoreType.DMA((7,)), pltpu.SemaphoreType.DMA((7,))],
    )(v)


BIG_NAMES = ("ffn1_w_gate", "ffn1_w_up", "ffn1_w_down", "w_in", "w_branch", "w_merge_gate", "w_out",
             "ffn2_w_gate", "ffn2_w_up", "ffn2_w_down")


FFN1 = ("ffn1_w_gate", "ffn1_w_up", "ffn1_w_down")
FFN2 = ("ffn2_w_gate", "ffn2_w_up", "ffn2_w_down")
MIX_IN = ("w_in", "w_merge_gate")
MIX_OUT = ("w_branch", "w_out")


def _keys(names, l):
    return [(n, l) for n in names]


def _local_step(x, tgt, small, convw_full, wx, n_layers):
    T, D = x.shape
    L = n_layers
    ns = N_SHARD
    dq = D // ns
    W = wx.w

    def hosted(call, keys, scatter=False):
        comm = wx.pieces(keys, scatter)
        main, extra = call(comm)
        if comm is not None:
            wx.arrived(keys, extra, scatter)
        return main

    def mixer_views(l):
        g4 = W[("w_merge_gate", l)]
        gates = jnp.transpose(g4, (0, 2, 1, 3)).reshape(D, 3 * D)
        win = jnp.transpose(W[("w_in", l)], (1, 0, 2)).reshape(D, -1)
        return jnp.concatenate([gates, _permute_in_cols(win)], axis=-1)

    def out_views(l):
        wb = jnp.transpose(W[("w_branch", l)], (1, 2, 0, 3)).reshape(3, BRANCH_W, D)
        wo = W[("w_out", l)].reshape(D, D)
        return wb, wo

    tb = _ret_tables(T)
    rb_pad = jnp.pad(small["rel_bias"], ((0, 0), (0, 0), (0, RB_PAD - N_REL)))

    saved = []
    h = x
    for l in range(L):
        s = {"x0": h}
        nxt = l + 1
        x1, s["g1"], s["u1"] = hosted(
            lambda c: _ffn_fwd(h, small["ffn1_norm"][l][None], W[("ffn1_w_gate", l)], W[("ffn1_w_up", l)],
                               W[("ffn1_w_down", l)], f"ffn1_fwd_{l}", comm=c), _keys(MIX_IN, l))
        s["x1"] = x1
        s["wbig"] = mixer_views(l)
        big, s["h"] = _inproj_fwd(x1, small["mix_norm"][l][None], s["wbig"], f"inproj_fwd_{l}")
        s["big"] = big
        s["bias"] = jnp.transpose(_relbias_expand(rb_pad[l], f"relbias_expand_{l}"), (1, 0, 2))
        s["yc"] = _conv_fwd(big, convw_full[l], f"conv_fwd_{l}")
        s["yr"], s["o"], s["st"] = hosted(lambda c: _ret_fwd(big, tb, f"ret_fwd_{l}", comm=c), _keys(MIX_OUT, l))
        (s["ya"],) = hosted(lambda c: _att_fwd(big, s["bias"], f"att_fwd_{l}", comm=c), _keys(FFN2, l))
        s["wb"], s["wo"] = out_views(l)
        x2, s["p"], s["mg"] = _merge_fwd(x1, big, (s["yc"], s["yr"], s["ya"]), s["wb"], s["wo"], f"merge_fwd_{l}")
        s["x2"] = x2
        h, s["g2"], s["u2"] = hosted(
            lambda c: _ffn_fwd(x2, small["ffn2_norm"][l][None], W[("ffn2_w_gate", l)], W[("ffn2_w_up", l)],
                               W[("ffn2_w_down", l)], f"ffn2_fwd_{l}", comm=c), _keys(FFN1, nxt) if nxt < L else [])
        saved.append(s)

    loss_p, dx, d_final = _loss_head(h, tgt, small["final_norm"][None], "loss_head")

    gs = {"final_norm": d_final[0]}
    for k in ("ffn1_norm", "mix_norm", "ffn2_norm", "rel_bias", "conv_w"):
        gs[k] = [None] * L
    tk = min(2048, T)
    nk = T // tk

    def ffn_back(pre, l, dxo, x_in, g, u, first_keys, second_keys):
        nw = small[pre + "_norm"][l][None]
        dgv, duv, av, hb, dacc = hosted(
            lambda c: _ffn_bwd_hidden(dxo, x_in, nw, g, u, W[(pre + "_w_down", l)], f"{pre}_bwd_hidden_{l}", comm=c),
            first_keys, scatter=True)
        dxn, dn = hosted(
            lambda c: _ffn_bwd_resid(dgv, duv, W[(pre + "_w_gate", l)], W[(pre + "_w_up", l)], x_in, nw, dxo,
                                     f"{pre}_bwd_resid_{l}", comm=c),
            second_keys, scatter=True)
        gs[pre + "_norm"][l] = dn[0]
        return dxn, (hb, dgv, duv, av, dacc)

    def ffn_grads(pre, l, hb, dgv, duv, av, dacc, chain=False, carry=()):
        fs = dgv.shape[-1]
        hspec = pl.BlockSpec((tk, D), lambda p, q, k: (k, 0))
        sspec = pl.BlockSpec((None, tk, fs), lambda p, q, k: (p, k, 0))
        up_spec = pl.BlockSpec((None, D, fs), lambda p, q, k: (p, 0, 0))
        down_spec = pl.BlockSpec((None, fs, D), lambda p, q, k: (p, 0, 0))
        jobs = [(pre + "_w_gate", hb, dgv, hspec, sspec, (ns, D, fs), up_spec),
                (pre + "_w_up", hb, duv, hspec, sspec, (ns, D, fs), up_spec),
                (pre + "_w_down", av, dacc, sspec, hspec, (ns, fs, D), down_spec)]
        before = None
        for nm, a, b, a_spec, b_spec, shape, o_spec in jobs:
            def product(c):
                r = _tn(a, b, a_spec, b_spec, _sds(shape, BF), o_spec, (ns, 1, nk), f"d{nm}_{l}", comm=c)
                return (r, []) if c is None else r
            if before is None:
                keys = list(carry)
            else:
                keys = [before] if chain else []
            wx.g[(nm, l)] = hosted(product, keys, scatter=True)
            before = (nm, l)

    for l in reversed(range(L)):
        s = saved[l]
        above = _keys(FFN1, l + 1) if l + 1 < L else [None] * 3
        dx, parts = ffn_back("ffn2", l, dx, s["x2"], s["g2"], s["u2"], [k for k in above[:1] if k],
                             [k for k in above[1:2] if k])
        ffn_grads("ffn2", l, *parts, carry=[k for k in above[2:] if k])
        dp, dbig, dyc, dyr, dya, dxb = _merge_bwd(dx, s["big"], s["p"], s["wb"], s["wo"], f"merge_bwd_{l}")
        wx.g[("w_out", l)] = _tn(
            s["mg"], dxb, pl.BlockSpec((tk, dq), lambda p, q, k: (k, p)), pl.BlockSpec((tk, D), lambda p, q, k: (k, 0)),
            _sds((ns, dq, D), BF), pl.BlockSpec((None, dq, D), lambda p, q, k: (p, 0, 0)), (ns, 1, nk), f"dw_out_{l}")
        gb = None
        for i, yv in enumerate((s["yc"], s["yr"], s["ya"])):
            gb = _tn(yv, dp,
                     pl.BlockSpec((tk, BRANCH_W), lambda p, q, k: (k, 0)),
                     pl.BlockSpec((tk, dq), lambda p, q, k, i=i: (k, i * ns + p)),
                     _sds((ns, 3, BRANCH_W, dq), BF),
                     pl.BlockSpec((None, None, BRANCH_W, dq), lambda p, q, k, i=i: (p, i, 0, 0)),
                     (ns, 1, nk), f"dw_branch{i}_{l}", prev=gb)
        wx.g[("w_branch", l)] = gb
        dbig, dcw = _conv_bwd(s["big"], dyc, convw_full[l], dbig, f"conv_bwd_{l}")
        gs["conv_w"][l] = dcw
        dbig = _ret_bwd(s["big"], s["o"], s["st"], dyr, tb, dbig, f"ret_bwd_{l}")
        dbig, dbias = hosted(lambda c: _att_bwd(s["big"], s["bias"], dya, dbig, f"att_bwd_{l}", comm=c),
                             _keys(FFN2, l), scatter=True)
        gs["rel_bias"][l] = _relbias_grad(jnp.transpose(dbias, (1, 0, 2)), f"relbias_grad_{l}")[:, :N_REL]
        n_in = N_SEG * BRANCH_W
        bn = 1024 if (3 * D) % 1024 == 0 else BRANCH_W
        dwp = _tn(s["h"], dbig, pl.BlockSpec((tk, D), lambda p, q, k: (k, 0)),
                  pl.BlockSpec((tk, bn), lambda p, q, k: (k, 3 * D // bn + q)),
                  _sds((D, n_in), BF), pl.BlockSpec((D, bn), lambda p, q, k: (0, q)), (1, n_in // bn, nk), f"dw_in_{l}")
        wx.g[("w_in", l)] = jnp.transpose(_unpermute_in_cols(dwp).reshape(D, ns, n_in // ns), (1, 0, 2))
        wx.g[("w_merge_gate", l)] = _tn_gates(s["h"], dbig, ns, tk, f"dw_merge_gate_{l}")
        dx, dn = hosted(
            lambda c: _inproj_bwd(dbig, s["wbig"], s["x1"], small["mix_norm"][l][None], dx, f"inproj_bwd_{l}", comm=c),
            [("w_in", l)], scatter=True)
        gs["mix_norm"][l] = dn[0]
        dx, parts = ffn_back("ffn1", l, dx, s["x0"], s["g1"], s["u1"],
                             [("w_merge_gate", l), ("w_branch", l), ("w_out", l)], [])
        ffn_grads("ffn1", l, *parts, chain=(l == 0))

    for k in ("ffn1_norm", "mix_norm", "ffn2_norm", "rel_bias", "conv_w"):
        gs[k] = jnp.stack(gs[k])
    return loss_p, dx, gs


class _Exchange:
    def __init__(self, shards):
        self.shards = shards
        self.w = {}
        self.g = {}
        self.landed = {}

    def own(self, key):
        return self.shards[key[0]][key[1]].astype(BF)

    def pieces(self, keys, scatter):
        if not keys:
            return None
        return _Pieces([self.g[k] for k in keys] if scatter else [self.own(k) for k in keys], scatter)

    def arrived(self, keys, outs, scatter):
        for k, o in zip(keys, outs):
            (self.landed if scatter else self.w)[k] = o


W_NAMES = ("ffn1_norm", "ffn1_w_gate", "ffn1_w_up", "ffn1_w_down", "mix_norm", "w_in", "conv_w", "rel_bias", "w_branch",
           "w_merge_gate", "w_out", "ffn2_norm", "ffn2_w_gate", "ffn2_w_up", "ffn2_w_down", "final_norm")


def _as2d(a):
    return a.reshape(1, -1) if a.ndim == 1 else a.reshape(-1, a.shape[-1])


def kernel(x, ffn1_norm, ffn1_w_gate, ffn1_w_up, ffn1_w_down, mix_norm, w_in, conv_w, rel_bias, w_branch, w_merge_gate, w_out, ffn2_norm, ffn2_w_gate, ffn2_w_up, ffn2_w_down, final_norm, loss_target, m_ffn1_norm, m_ffn1_w_gate, m_ffn1_w_up, m_ffn1_w_down, m_mix_norm, m_w_in, m_conv_w, m_rel_bias, m_w_branch, m_w_merge_gate, m_w_out, m_ffn2_norm, m_ffn2_w_gate, m_ffn2_w_up, m_ffn2_w_down, m_final_norm, v_ffn1_norm, v_ffn1_w_gate, v_ffn1_w_up, v_ffn1_w_down, v_mix_norm, v_w_in, v_conv_w, v_rel_bias, v_w_branch, v_w_merge_gate, v_w_out, v_ffn2_norm, v_ffn2_w_gate, v_ffn2_w_up, v_ffn2_w_down, v_final_norm):
    given = dict(locals())
    w = {n: given[n] for n in W_NAMES}
    m = {n: given["m_" + n] for n in W_NAMES}
    v = {n: given["v_" + n] for n in W_NAMES}
    my_chip = 2 * lax.axis_index("x") + lax.axis_index("y")
    L = w_in.shape[0]

    wx = _Exchange({n: w[n] for n in BIG_NAMES})
    first = _keys(FFN1, 0)
    whole = [wx.own(k) for k in first] + [conv_w]
    got = _comm_alone(_HalfGather([a.reshape((2, a.shape[0] // 2) + a.shape[1:]) for a in whole]), "gather_first")
    got = [g.reshape((N_SHARD,) + a.shape) for g, a in zip(got, whole)]
    wx.arrived(first, got[:-1], False)
    convw_full = jnp.transpose(got[-1], (1, 2, 0, 3)).reshape(conv_w.shape[0], conv_w.shape[1], -1)

    small = {n: w[n] for n in ("ffn1_norm", "mix_norm", "ffn2_norm", "final_norm", "rel_bias")}
    loss_p, grad_x, gs = _local_step(x[0], loss_target[0], small, convw_full, wx, L)
    last = [(FFN1[-1], 0)]
    wx.arrived(last, _comm_alone(wx.pieces(last, True), "scatter_last"), True)

    sums = []
    for n in BIG_NAMES:
        acc = None
        for l in range(L):
            a = wx.landed[(n, l)]
            acc = _sum4(a.reshape(4, -1, a.shape[-1]), l, L, f"sum4_{n}_{l}", prev=acc)
        sums.append(acc.reshape(-1, acc.shape[-1]))
    others = _swap_cores(sums, "swap_cores")

    parts = [gs["ffn1_norm"].reshape(-1), gs["mix_norm"].reshape(-1), gs["ffn2_norm"].reshape(-1),
             gs["final_norm"].reshape(-1), gs["rel_bias"].reshape(-1), gs["conv_w"].reshape(-1), loss_p[0]]
    sizes = [p.shape[0] for p in parts]
    flat = jnp.concatenate(parts)
    rows = -(-flat.shape[0] // (8 * LANE)) * 8
    flat = jnp.pad(flat, (0, rows * LANE - flat.shape[0])).reshape(rows, LANE)
    red = _allreduce_small(flat, "allreduce_small").reshape(-1)
    offs = [0]
    for sz in sizes:
        offs.append(offs[-1] + sz)
    sm = {}
    for i, n in enumerate(("ffn1_norm", "mix_norm", "ffn2_norm", "final_norm", "rel_bias", "conv_w")):
        sm[n] = red[offs[i]:offs[i + 1]]
    loss = red[offs[6]]
    sm["conv_w"] = lax.dynamic_slice_in_dim(sm["conv_w"].reshape(conv_w.shape[0], conv_w.shape[1], -1),
                                            my_chip * conv_w.shape[2], conv_w.shape[2], axis=2)

    grads, deltas, new_m, new_v = {}, {}, {}, {}
    big_sum = dict(zip(BIG_NAMES, zip(sums, others)))
    for n in W_NAMES:
        shape = w[n].shape
        if n in big_sum:
            ga, gb = big_sum[n]
        else:
            ga, gb = _as2d(sm[n].reshape(shape)), None
        out = _adamw(_as2d(w[n]), ga, gb, _as2d(m[n]), _as2d(v[n]), f"adamw_{n}")
        grads[n], deltas[n], new_m[n], new_v[n] = (o.reshape(shape) for o in out)

    return (loss, grad_x[None], *[grads[n] for n in W_NAMES], *[deltas[n] for n in W_NAMES],
            *[new_m[n] for n in W_NAMES], *[new_v[n] for n in W_NAMES])
```

```python
import functools
import math

import jax
import jax.numpy as jnp
from jax import lax
from jax.experimental import pallas as pl
from jax.experimental.pallas import tpu as pltpu

F32 = jnp.float32
BF = jnp.bfloat16
MESH = pl.DeviceIdType.MESH
ARB = "arbitrary"
PAR = "parallel"

EPS = 1e-6
NEG_INF = -1e30
ROPE_BASE = 10000.0
CHUNK = 64
BRANCH_W = 512
H_RET = 4
DK_RET = 128
H_ATT = 8
DH_ATT = 64
N_PREV = 8
REL_CLIP = 128
N_REL = 2 * REL_CLIP + 1
N_SHARD = 4
LANE = 128
RET_L = 512
ATT_TQ = 128
ATT_SUB = 4
ATT_PAD = N_PREV * CHUNK
ATT_SPAN = ATT_TQ + ATT_PAD
ATT_TOEP = 2 * REL_CLIP
RB_PAD = 264
TM = 512
TM_FFN = 1024

ADAM_LR = 0.001
ADAM_B1 = 0.9
ADAM_B2 = 0.999
ADAM_EPS = 1e-08
ADAM_WD = 0.01
ADAM_STEP = 10

NT_DIMS = (((1,), (1,)), ((), ()))
TN_DIMS = (((0,), (0,)), ((), ()))


def _cp(sem, vmem_mb=48):
    return pltpu.CompilerParams(dimension_semantics=sem, vmem_limit_bytes=vmem_mb << 20)


def _sds(shape, dtype):
    return jax.ShapeDtypeStruct(tuple(shape), dtype)


def _rms_r(x):
    return lax.rsqrt(jnp.mean(x * x, axis=-1, keepdims=True) + EPS)


def _sigmoid(x):
    return 0.5 * jnp.tanh(0.5 * x) + 0.5


def _rms_bwd(dh, xv, nw):
    r = _rms_r(xv)
    xh = xv * r
    dxh = dh * nw
    dx = r * (dxh - xh * jnp.mean(dxh * xh, axis=-1, keepdims=True))
    return dx, jnp.sum(dh * xh, axis=0, keepdims=True)


def _place():
    return lax.axis_index("x"), lax.axis_index("y"), lax.axis_index("c")


def _other_chips(x, y):
    return [(1 - x, y), (x, 1 - y), (1 - x, 1 - y)]


class _Pieces:
    def __init__(self, srcs, scatter):
        self.srcs = list(srcs)
        self.scatter = scatter
        n = len(self.srcs)
        self.out_shape = [_sds(s.shape if scatter else (N_SHARD,) + s.shape, s.dtype) for s in self.srcs]
        self.scratch = [pltpu.SemaphoreType.DMA((n,)), pltpu.SemaphoreType.DMA((3, n)), pltpu.SemaphoreType.DMA((3, n))]

    def _copies(self, src, dst, sems, waiting):
        lsem, ssem, rsem = sems
        x, y, c = _place()
        mine = 2 * x + y
        n = len(src)

        def remote(j, k, chip, s_ref, d_ref):
            return pltpu.make_async_remote_copy(
                src_ref=s_ref, dst_ref=d_ref, send_sem=ssem.at[j, k], recv_sem=rsem.at[j, k],
                device_id=(chip[0], chip[1], c), device_id_type=MESH)

        chips = list(enumerate(_other_chips(x, y)))
        if self.scatter:
            local = [pltpu.make_async_copy(src[k].at[mine], dst[k].at[3], lsem.at[k]) for k in range(n)]
            sends = [remote(j, k, ch, src[k].at[2 * ch[0] + ch[1]], dst[k].at[j]) for j, ch in chips for k in range(n)]
            recvs = sends
        else:
            local = [pltpu.make_async_copy(src[k], dst[k].at[mine], lsem.at[k]) for k in range(n)]
            sends = [remote(j, k, ch, src[k], dst[k].at[mine]) for j, ch in chips for k in range(n)]
            recvs = [remote(j, k, ch, src[k], dst[k].at[2 * ch[0] + ch[1]]) for j, ch in chips for k in range(n)
                     ] if waiting else []
        return local, sends, recvs

    def start(self, src, dst, sems):
        local, sends, _ = self._copies(src, dst, sems, False)
        for cp in local + sends:
            cp.start()

    def wait(self, src, dst, sems):
        local, sends, recvs = self._copies(src, dst, sems, True)
        for cp in recvs:
            cp.wait_recv()
        for cp in sends:
            cp.wait_send()
        for cp in local:
            cp.wait()


class _HalfGather:
    def __init__(self, srcs):
        self.srcs = list(srcs)
        n = len(self.srcs)
        self.out_shape = [_sds((N_SHARD,) + s.shape, s.dtype) for s in self.srcs]
        self.scratch = [pltpu.SemaphoreType.DMA((n,))] + [pltpu.SemaphoreType.DMA((3, n)) for _ in range(4)]

    def _plan(self, src, dst, sems):
        lsem, s1, r1, s2, r2 = sems
        x, y, c = _place()
        mine = 2 * x + y
        n = len(src)
        chips = [(j, ch, 2 * ch[0] + ch[1]) for j, ch in enumerate(_other_chips(x, y))]

        def copy(s_ref, d_ref, ssem, rsem, to):
            return pltpu.make_async_remote_copy(src_ref=s_ref, dst_ref=d_ref, send_sem=ssem, recv_sem=rsem,
                                                device_id=to, device_id_type=MESH)

        local = [pltpu.make_async_copy(src[k], dst[k].at[mine], lsem.at[k]) for k in range(n)]
        sends = [copy(src[k].at[c], dst[k].at[mine, c], s1.at[j, k], r1.at[j, k], (ch[0], ch[1], c))
                 for j, ch, _ in chips for k in range(n)]
        lands = [copy(src[k].at[c], dst[k].at[slot, c], s1.at[j, k], r1.at[j, k], (ch[0], ch[1], c))
                 for j, ch, slot in chips for k in range(n)]
        passes = [copy(dst[k].at[slot, c], dst[k].at[slot, c], s2.at[j, k], r2.at[j, k], (x, y, 1 - c))
                  for j, ch, slot in chips for k in range(n)]
        gets = [copy(dst[k].at[slot, 1 - c], dst[k].at[slot, 1 - c], s2.at[j, k], r2.at[j, k], (x, y, 1 - c))
                for j, ch, slot in chips for k in range(n)]
        return local, sends, lands, passes, gets

    def start(self, src, dst, sems):
        lsem, s1, r1, s2, r2 = sems
        x, y, c = _place()
        mine = 2 * x + y
        for k in range(len(src)):
            pltpu.make_async_copy(src[k], dst[k].at[mine], lsem.at[k]).start()
        for j, ch in enumerate(_other_chips(x, y)):
            for k in range(len(src)):
                pltpu.make_async_remote_copy(
                    src_ref=src[k].at[c], dst_ref=dst[k].at[mine, c], send_sem=s1.at[j, k], recv_sem=r1.at[j, k],
                    device_id=(ch[0], ch[1], c), device_id_type=MESH).start()

    def wait(self, src, dst, sems):
        local, sends, lands, passes, gets = self._plan(src, dst, sems)
        for land, fwd in zip(lands, passes):
            land.wait_recv()
            fwd.start()
        for cp in gets:
            cp.wait_recv()
        for cp in sends + passes:
            cp.wait_send()
        for cp in local:
            cp.wait()


def _call(body, *, name, args, in_specs, out_specs, out_shape, grid=(), scratch_shapes=(), sem=None, comm=None,
          aliases=None, vmem_mb=48):
    in_specs, out_specs, out_shape = list(in_specs), list(out_specs), list(out_shape)
    scratch, args = list(scratch_shapes), list(args)
    n_in, n_out, n_scr = len(in_specs), len(out_specs), len(scratch)
    if comm is None:
        def kernel_body(*refs):
            body(*refs)
    else:
        c_in, c_out = len(comm.srcs), len(comm.out_shape)

        def kernel_body(*refs):
            o0 = n_in + c_in
            s0 = o0 + n_out + c_out
            cin, cout, sems = refs[n_in:o0], refs[o0 + n_out:s0], refs[s0 + n_scr:]
            main = refs[:n_in] + refs[o0:o0 + n_out] + refs[s0:s0 + n_scr]
            if grid:
                ids = [pl.program_id(a) for a in range(len(grid))]
                first = functools.reduce(lambda p, q: p & q, [i == 0 for i in ids])
                last = functools.reduce(lambda p, q: p & q, [i == g - 1 for i, g in zip(ids, grid)])

                @pl.when(first)
                def _():
                    comm.start(cin, cout, sems)

                body(*main)

                @pl.when(last)
                def _():
                    comm.wait(cin, cout, sems)
            else:
                comm.start(cin, cout, sems)
                body(*main)
                comm.wait(cin, cout, sems)

        hbm = pl.BlockSpec(memory_space=pl.ANY)
        in_specs += [hbm] * c_in
        out_specs += [hbm] * c_out
        out_shape += comm.out_shape
        scratch += comm.scratch
        args += comm.srcs
    params = dict(vmem_limit_bytes=vmem_mb << 20)
    if grid:
        params["dimension_semantics"] = sem
    outs = pl.pallas_call(
        kernel_body, name=name, grid=grid, in_specs=in_specs, out_specs=out_specs, out_shape=out_shape,
        scratch_shapes=scratch, input_output_aliases=aliases or {}, compiler_params=pltpu.CompilerParams(**params),
    )(*args)
    return list(outs[:n_out]), list(outs[n_out:])


def _comm_alone(comm, name):
    return _call(lambda: None, name=name, args=[], in_specs=[], out_specs=[], out_shape=[], comm=comm)[1]


def _ffn_fwd(x, nw, wg, wu, wd, name, comm=None):
    T, D = x.shape
    ns, _, fs = wg.shape
    tm = min(TM_FFN, T)

    def body(x_ref, nw_ref, wg_ref, wu_ref, wd_ref, xo_ref, g_ref, u_ref, h_s, acc_s):
        j = pl.program_id(1)

        @pl.when(j == 0)
        def _():
            xv = x_ref[...]
            h_s[...] = (xv * _rms_r(xv) * nw_ref[...]).astype(BF)
            acc_s[...] = jnp.zeros_like(acc_s)

        h = h_s[...]
        gb = jnp.dot(h, wg_ref[...], preferred_element_type=F32).astype(BF)
        ub = jnp.dot(h, wu_ref[...], preferred_element_type=F32).astype(BF)
        g_ref[...] = gb
        u_ref[...] = ub
        g = gb.astype(F32)
        a = (g * _sigmoid(g) * ub.astype(F32)).astype(BF)
        acc_s[...] += jnp.dot(a, wd_ref[...], preferred_element_type=F32)

        @pl.when(j == ns - 1)
        def _():
            xo_ref[...] = x_ref[...] + 0.5 * acc_s[...]

    wspec = pl.BlockSpec((None, D, fs), lambda i, j: (j, 0, 0))
    return _call(
        body, name=name, grid=(T // tm, ns), args=(x, nw, wg, wu, wd), comm=comm, vmem_mb=56,
        in_specs=[pl.BlockSpec((tm, D), lambda i, j: (i, 0)),
                  pl.BlockSpec((1, D), lambda i, j: (0, 0)),
                  wspec, wspec,
                  pl.BlockSpec((None, fs, D), lambda i, j: (j, 0, 0))],
        out_specs=[pl.BlockSpec((tm, D), lambda i, j: (i, 0)),
                   pl.BlockSpec((None, tm, fs), lambda i, j: (j, i, 0)),
                   pl.BlockSpec((None, tm, fs), lambda i, j: (j, i, 0))],
        out_shape=[_sds((T, D), F32), _sds((ns, T, fs), BF), _sds((ns, T, fs), BF)],
        scratch_shapes=[pltpu.VMEM((tm, D), BF), pltpu.VMEM((tm, D), F32)],
        sem=(ARB, ARB))


def _ffn_bwd_hidden(dxo, x, nw, g, u, wd, name, comm=None):
    T, D = x.shape
    ns, fs, _ = wd.shape
    tm = min(TM_FFN, T)

    def body(dxo_ref, x_ref, nw_ref, g_ref, u_ref, wd_ref, dg_ref, du_ref, a_ref, h_ref, dacc_ref, dacc_s):
        @pl.when(pl.program_id(1) == 0)
        def _():
            xv = x_ref[...]
            h_ref[...] = (xv * _rms_r(xv) * nw_ref[...]).astype(BF)
            db = (0.5 * dxo_ref[...]).astype(BF)
            dacc_ref[...] = db
            dacc_s[...] = db

        da = lax.dot_general(dacc_s[...], wd_ref[...], NT_DIMS, preferred_element_type=F32)
        gv = g_ref[...].astype(F32)
        uv = u_ref[...].astype(F32)
        s = _sigmoid(gv)
        sg = gv * s
        a_ref[...] = (sg * uv).astype(BF)
        du_ref[...] = (da * sg).astype(BF)
        dg_ref[...] = (da * uv * (s * (1.0 + gv * (1.0 - s)))).astype(BF)

    tok = pl.BlockSpec((tm, D), lambda i, j: (i, 0))
    hid = pl.BlockSpec((None, tm, fs), lambda i, j: (j, i, 0))
    return _call(
        body, name=name, grid=(T // tm, ns), args=(dxo, x, nw, g, u, wd), comm=comm, vmem_mb=56,
        in_specs=[tok, tok, pl.BlockSpec((1, D), lambda i, j: (0, 0)), hid, hid,
                  pl.BlockSpec((None, fs, D), lambda i, j: (j, 0, 0))],
        out_specs=[hid, hid, hid, tok, tok],
        out_shape=[_sds((ns, T, fs), BF)] * 3 + [_sds((T, D), BF)] * 2,
        scratch_shapes=[pltpu.VMEM((tm, D), BF)],
        sem=(ARB, ARB))


def _ffn_bwd_resid(dg, du, wg, wu, x, nw, dxo, name, comm=None):
    T, D = x.shape
    ns, _, fs = wg.shape
    tm = min(TM_FFN, T)

    def body(dg_ref, du_ref, wg_ref, wu_ref, x_ref, nw_ref, dxo_ref, dx_ref, dnw_ref, acc_s):
        i = pl.program_id(0)
        j = pl.program_id(1)
        prod = (lax.dot_general(dg_ref[...], wg_ref[...], NT_DIMS, preferred_element_type=F32)
                + lax.dot_general(du_ref[...], wu_ref[...], NT_DIMS, preferred_element_type=F32))

        @pl.when((i == 0) & (j == 0))
        def _():
            dnw_ref[...] = jnp.zeros_like(dnw_ref)

        @pl.when(j == 0)
        def _():
            acc_s[...] = prod

        @pl.when(j > 0)
        def _():
            acc_s[...] += prod

        @pl.when(j == ns - 1)
        def _():
            dx, dn = _rms_bwd(acc_s[...], x_ref[...], nw_ref[...])
            dx_ref[...] = dxo_ref[...] + dx
            dnw_ref[...] += dn

    tok = pl.BlockSpec((tm, D), lambda i, j: (i, 0))
    row = pl.BlockSpec((1, D), lambda i, j: (0, 0))
    hid = pl.BlockSpec((None, tm, fs), lambda i, j: (j, i, 0))
    wspec = pl.BlockSpec((None, D, fs), lambda i, j: (j, 0, 0))
    return _call(
        body, name=name, grid=(T // tm, ns), args=(dg, du, wg, wu, x, nw, dxo), comm=comm, vmem_mb=56,
        in_specs=[hid, hid, wspec, wspec, tok, row, tok],
        out_specs=[tok, row],
        out_shape=[_sds((T, D), F32), _sds((1, D), F32)],
        scratch_shapes=[pltpu.VMEM((tm, D), F32)],
        sem=(ARB, ARB))


def _tn(a, b, a_spec, b_spec, out_shape, out_spec, grid, name, prev=None, comm=None):
    nk = grid[-1]
    acc_shape = tuple(d for d in out_spec.block_shape if d is not None)

    def body(*refs):
        a_ref, b_ref = refs[0], refs[1]
        o_ref, acc = refs[-2], refs[-1]
        k = pl.program_id(2)
        prod = lax.dot_general(a_ref[...], b_ref[...], TN_DIMS, preferred_element_type=F32)

        @pl.when(k == 0)
        def _():
            acc[...] = prod

        @pl.when(k > 0)
        def _():
            acc[...] += prod

        @pl.when(k == nk - 1)
        def _():
            o_ref[...] = acc[...].astype(o_ref.dtype)

    in_specs = [a_spec, b_spec]
    args = [a, b]
    aliases = {}
    if prev is not None:
        in_specs.append(pl.BlockSpec(memory_space=pl.ANY))
        args.append(prev)
        aliases = {2: 0}
    main, extra = _call(
        body, name=name, grid=grid, args=args, in_specs=in_specs, out_specs=[out_spec], out_shape=[out_shape],
        scratch_shapes=[pltpu.VMEM(acc_shape, F32)], aliases=aliases, sem=(ARB, ARB, ARB), comm=comm)
    return main[0] if comm is None else (main[0], extra)


def _tn_gates(h, dbig, ns, tk, name):
    T, D = h.shape
    dq = D // ns
    nk = T // tk

    def body(a_ref, b_ref, o_ref, acc):
        k = pl.program_id(1)
        prod = lax.dot_general(a_ref[...], b_ref[...], TN_DIMS, preferred_element_type=F32)

        @pl.when(k == 0)
        def _():
            acc[...] = prod

        @pl.when(k > 0)
        def _():
            acc[...] += prod

        @pl.when(k == nk - 1)
        def _():
            for s in range(ns):
                o_ref[s] = acc[s * dq:(s + 1) * dq, :].astype(o_ref.dtype)

    return pl.pallas_call(
        body, name=name, grid=(3, nk),
        in_specs=[pl.BlockSpec((tk, D), lambda q, k: (k, 0)), pl.BlockSpec((tk, D), lambda q, k: (k, q))],
        out_specs=pl.BlockSpec((ns, None, dq, D), lambda q, k: (0, q, 0, 0)),
        out_shape=_sds((ns, 3, dq, D), BF),
        scratch_shapes=[pltpu.VMEM((D, D), F32)],
        compiler_params=_cp((PAR, ARB)),
    )(h, dbig)


def _inproj_fwd(x, nw, wbig, name):
    T, D = x.shape
    nb = wbig.shape[-1]
    tm = min(2 * TM, T)
    bn = min(2048, nb)

    def body(x_ref, nw_ref, w_ref, o_ref, h_ref, h_s):
        @pl.when(pl.program_id(1) == 0)
        def _():
            xv = x_ref[...]
            hb = (xv * _rms_r(xv) * nw_ref[...]).astype(BF)
            h_s[...] = hb
            h_ref[...] = hb

        o_ref[...] = jnp.dot(h_s[...], w_ref[...], preferred_element_type=F32).astype(BF)

    return pl.pallas_call(
        body, name=name, grid=(T // tm, nb // bn),
        in_specs=[pl.BlockSpec((tm, D), lambda i, n: (i, 0)),
                  pl.BlockSpec((1, D), lambda i, n: (0, 0)),
                  pl.BlockSpec((D, bn), lambda i, n: (0, n))],
        out_specs=[pl.BlockSpec((tm, bn), lambda i, n: (i, n)),
                   pl.BlockSpec((tm, D), lambda i, n: (i, 0))],
        out_shape=[_sds((T, nb), BF), _sds((T, D), BF)],
        scratch_shapes=[pltpu.VMEM((tm, D), BF)],
        compiler_params=_cp((PAR, ARB)),
    )(x, nw, wbig)


def _inproj_bwd(dbig, wbig, x, nw, dxin, name, comm=None):
    T, D = x.shape
    nb = wbig.shape[-1]
    tm = min(TM_FFN, T)
    tk = min(2048, nb)
    nk = nb // tk

    def body(a_ref, w_ref, x_ref, nw_ref, dxin_ref, dx_ref, dnw_ref, acc_s):
        i = pl.program_id(0)
        k = pl.program_id(1)
        prod = lax.dot_general(a_ref[...], w_ref[...], NT_DIMS, preferred_element_type=F32)

        @pl.when((i == 0) & (k == 0))
        def _():
            dnw_ref[...] = jnp.zeros_like(dnw_ref)

        @pl.when(k == 0)
        def _():
            acc_s[...] = prod

        @pl.when(k > 0)
        def _():
            acc_s[...] += prod

        @pl.when(k == nk - 1)
        def _():
            dx, dn = _rms_bwd(acc_s[...], x_ref[...], nw_ref[...])
            dx_ref[...] = dxin_ref[...] + dx
            dnw_ref[...] += dn

    tok = pl.BlockSpec((tm, D), lambda i, k: (i, 0))
    row = pl.BlockSpec((1, D), lambda i, k: (0, 0))
    return _call(
        body, name=name, grid=(T // tm, nk), args=(dbig, wbig, x, nw, dxin), comm=comm, vmem_mb=56,
        in_specs=[pl.BlockSpec((tm, tk), lambda i, k: (i, k)),
                  pl.BlockSpec((D, tk), lambda i, k: (0, k)),
                  tok, row, tok],
        out_specs=[tok, row],
        out_shape=[_sds((T, D), F32), _sds((1, D), F32)],
        scratch_shapes=[pltpu.VMEM((tm, D), F32)],
        sem=(ARB, ARB))


CONV_R = 512
CONV_BASE, CONV_GROUP = 0, 3
ATT_BASE, ATT_GROUP = 12, 3
RET_BASE, RET_GROUP = 24, 4
N_SEG = 10


def _permute_in_cols(w):
    lead = w.shape[:-1]
    w4 = w.reshape(lead + (N_SEG, BRANCH_W // LANE, LANE))

    def grouped(lo, hi):
        return jnp.swapaxes(w4[..., lo:hi, :, :], -3, -2).reshape(lead + (-1,))

    return jnp.concatenate([grouped(0, 3), grouped(7, 10), grouped(3, 7)], axis=-1)


def _unpermute_in_cols(w):
    lead = w.shape[:-1]
    nblk = BRANCH_W // LANE

    def segs(lo, n):
        part = w[..., lo * LANE:(lo + nblk * n) * LANE].reshape(lead + (nblk, n, LANE))
        return jnp.swapaxes(part, -3, -2)

    conv, att, ret = segs(CONV_BASE, 3), segs(ATT_BASE, 3), segs(RET_BASE, 4)
    return jnp.concatenate([conv, ret, att], axis=-3).reshape(lead + (-1,))


def _seg0(big):
    return (big.shape[1] - N_SEG * BRANCH_W) // LANE


def _group_spec(big, base, group, rows, where):
    first = (_seg0(big) + base) // group
    assert first * group == _seg0(big) + base

    def index(*ids):
        r, g = where(*ids)
        return r, first + g

    return pl.BlockSpec((rows, group * LANE), index)


CU, CB, CC = (slice(k * LANE, (k + 1) * LANE) for k in range(3))
AQ, AK, AV = CU, CB, CC
RQ, RK, RV, RG = (slice(k * LANE, (k + 1) * LANE) for k in range(4))


def _conv_fwd(big, cw, name):
    T = big.shape[0]
    R = min(CONV_R, T)

    def body(g_ref, w_ref, y_ref, z_s):
        z_s[pl.ds(0, 8), :] = jnp.zeros((8, LANE), F32)

        def fill(t, c):
            sl = pl.ds(pl.multiple_of(t * R, R), R)
            z_s[pl.ds(pl.multiple_of(t * R + 8, 8), R), :] = g_ref[sl, CC].astype(F32) * g_ref[sl, CU].astype(F32)
            return c

        lax.fori_loop(0, T // R, fill, 0)
        w0, w1, w2 = w_ref[0:1, :], w_ref[1:2, :], w_ref[2:3, :]

        def step(t, c):
            zz = z_s[pl.ds(pl.multiple_of(t * R, R), R + 8), :]
            z0 = zz[8:]
            z1 = pltpu.roll(zz, 1, 0)[8:]
            z2 = pltpu.roll(zz, 2, 0)[8:]
            sl = pl.ds(pl.multiple_of(t * R, R), R)
            y_ref[sl, :] = (g_ref[sl, CB].astype(F32) * (w2 * z0 + w1 * z1 + w0 * z2)).astype(BF)
            return c

        lax.fori_loop(0, T // R, step, 0)

    return pl.pallas_call(
        body, name=name, grid=(BRANCH_W // LANE,),
        in_specs=[_group_spec(big, CONV_BASE, CONV_GROUP, T, lambda j: (0, j)),
                  pl.BlockSpec((3, LANE), lambda j: (0, j))],
        out_specs=pl.BlockSpec((T, LANE), lambda j: (0, j)),
        out_shape=_sds((T, BRANCH_W), BF),
        scratch_shapes=[pltpu.VMEM((T + 8, LANE), F32)],
        compiler_params=_cp((PAR,)),
    )(big, cw)


def _conv_bwd(big, dy, cw, dbig, name):
    T = big.shape[0]
    R = min(CONV_R, T)

    def body(g_ref, dy_ref, w_ref, _, o_ref, dw_ref, z_s, d_s):
        z_s[pl.ds(0, 8), :] = jnp.zeros((8, LANE), F32)
        d_s[pl.ds(T, 8), :] = jnp.zeros((8, LANE), F32)

        def fill(t, c):
            sl = pl.ds(pl.multiple_of(t * R, R), R)
            z_s[pl.ds(pl.multiple_of(t * R + 8, 8), R), :] = g_ref[sl, CC].astype(F32) * g_ref[sl, CU].astype(F32)
            d_s[sl, :] = dy_ref[sl, :].astype(F32) * g_ref[sl, CB].astype(F32)
            return c

        lax.fori_loop(0, T // R, fill, 0)
        w0, w1, w2 = w_ref[0:1, :], w_ref[1:2, :], w_ref[2:3, :]

        def step(t, carry):
            a0, a1, a2 = carry
            zz = z_s[pl.ds(pl.multiple_of(t * R, R), R + 8), :]
            z0 = zz[8:]
            z1 = pltpu.roll(zz, 1, 0)[8:]
            z2 = pltpu.roll(zz, 2, 0)[8:]
            sl = pl.ds(pl.multiple_of(t * R, R), R)
            dyv = dy_ref[sl, :].astype(F32)
            o_ref[sl, CB] = (dyv * (w2 * z0 + w1 * z1 + w0 * z2)).astype(BF)
            dd = d_s[pl.ds(pl.multiple_of(t * R, R), R + 8), :]
            d0 = dd[:R]
            d1 = pltpu.roll(dd, R + 7, 0)[:R]
            d2 = pltpu.roll(dd, R + 6, 0)[:R]
            dz = w2 * d0 + w1 * d1 + w0 * d2
            o_ref[sl, CC] = (dz * g_ref[sl, CU].astype(F32)).astype(BF)
            o_ref[sl, CU] = (dz * g_ref[sl, CC].astype(F32)).astype(BF)
            a0 = a0 + jnp.sum(d0 * z2, axis=0, keepdims=True)
            a1 = a1 + jnp.sum(d0 * z1, axis=0, keepdims=True)
            a2 = a2 + jnp.sum(d0 * z0, axis=0, keepdims=True)
            return a0, a1, a2

        zero = jnp.zeros((1, LANE), F32)
        a0, a1, a2 = lax.fori_loop(0, T // R, step, (zero, zero, zero))
        dw_ref[0:1, :] = a0
        dw_ref[1:2, :] = a1
        dw_ref[2:3, :] = a2

    group = _group_spec(big, CONV_BASE, CONV_GROUP, T, lambda j: (0, j))
    w = pl.BlockSpec((3, LANE), lambda j: (0, j))
    return pl.pallas_call(
        body, name=name, grid=(BRANCH_W // LANE,),
        in_specs=[group, pl.BlockSpec((T, LANE), lambda j: (0, j)), w, pl.BlockSpec(memory_space=pl.ANY)],
        out_specs=[group, w],
        out_shape=[_sds(dbig.shape, BF), _sds((3, BRANCH_W), F32)],
        scratch_shapes=[pltpu.VMEM((T + 8, LANE), F32), pltpu.VMEM((T + 8, LANE), F32)],
        input_output_aliases={3: 0}, compiler_params=_cp((PAR,)),
    )(big, dy, cw, dbig)


def _ret_tables(T):
    L = min(RET_L, T)
    hh = jnp.arange(H_RET, dtype=F32)
    lg = jnp.log1p(-jnp.exp2(-5.0 - hh))
    n = jnp.arange(L, dtype=F32)
    a = jnp.exp(lg[:, None] * (n + 1.0))
    b = jnp.exp(lg[:, None] * (L - 1.0 - n))
    gl = jnp.exp(lg * L)
    ch = jnp.arange(L) // CHUNK
    m = jnp.exp(lg[:, None, None] * jnp.abs(n[:, None] - n[None, :])) * (ch[None, :] <= ch[:, None]).astype(F32)
    inv_freq = ROPE_BASE ** (-jnp.linspace(0.0, 1.0, DK_RET // 2, dtype=F32))
    ang = jnp.arange(T, dtype=F32)[:, None] * inv_freq[None, :]
    cos, sin = jnp.cos(ang), jnp.sin(ang)
    return dict(
        L=L, M=m,
        a=jnp.broadcast_to(a[:, :, None], (H_RET, L, DK_RET)),
        b=jnp.broadcast_to(b[:, :, None], (H_RET, L, DK_RET)),
        gl=jnp.broadcast_to(gl[:, None, None], (H_RET, 1, DK_RET)),
        cos=jnp.concatenate([cos, cos], axis=-1), sin=jnp.concatenate([-sin, sin], axis=-1))


def _rot(x, cs, sn):
    return x * cs + pltpu.roll(x, DK_RET // 2, 1) * sn


def _unrot(dy, cs, sn):
    return dy * cs + pltpu.roll(dy * sn, DK_RET // 2, 1)


def _ret_fwd(big, tb, name, comm=None):
    T = big.shape[0]
    L = tb["L"]
    nsc = T // L
    scale = DK_RET ** -0.5

    def body(x_ref, cos_ref, sin_ref, m_ref, a_ref, b_ref, gl_ref, y_ref, o_ref, st_ref, s_s):
        @pl.when(pl.program_id(1) == 0)
        def _():
            s_s[...] = jnp.zeros_like(s_s)

        cs, sn = cos_ref[...], sin_ref[...]
        qt = _rot(x_ref[:, RQ].astype(F32), cs, sn) * scale
        kt = _rot(x_ref[:, RK].astype(F32), cs, sn)
        qb, kb, vb = qt.astype(BF), kt.astype(BF), x_ref[:, RV]
        s_prev = s_s[...]
        st_ref[...] = s_prev
        p = lax.dot_general(qb, kb, NT_DIMS, preferred_element_type=F32) * m_ref[...]
        o = (jnp.dot(p.astype(BF), vb, preferred_element_type=F32)
             + jnp.dot((qt * a_ref[...]).astype(BF), s_prev.astype(BF), preferred_element_type=F32))
        s_s[...] = s_prev * gl_ref[...] + lax.dot_general((kt * b_ref[...]).astype(BF), vb, TN_DIMS,
                                                         preferred_element_type=F32)
        o_ref[...] = o
        gv = x_ref[:, RG].astype(F32)
        y_ref[...] = (gv * _sigmoid(gv) * o * _rms_r(o)).astype(BF)

    tab = pl.BlockSpec((L, DK_RET), lambda h, i: (i, 0))
    per_head = pl.BlockSpec((None, L, DK_RET), lambda h, i: (h, 0, 0))
    out = pl.BlockSpec((L, LANE), lambda h, i: (i, h))
    return _call(
        body, name=name, grid=(H_RET, nsc), comm=comm,
        args=(big, tb["cos"], tb["sin"], tb["M"], tb["a"], tb["b"], tb["gl"]),
        in_specs=[_group_spec(big, RET_BASE, RET_GROUP, L, lambda h, i: (i, h)), tab, tab,
                  pl.BlockSpec((None, L, L), lambda h, i: (h, 0, 0)), per_head, per_head,
                  pl.BlockSpec((None, 1, DK_RET), lambda h, i: (h, 0, 0))],
        out_specs=[out, out, pl.BlockSpec((None, None, DK_RET, DK_RET), lambda h, i: (i, h, 0, 0))],
        out_shape=[_sds((T, BRANCH_W), BF), _sds((T, BRANCH_W), F32), _sds((nsc, H_RET, DK_RET, DK_RET), F32)],
        scratch_shapes=[pltpu.VMEM((DK_RET, DK_RET), F32)],
        sem=(ARB, ARB))


def _ret_bwd(big, o, st, dy, tb, dbig, name):
    T = big.shape[0]
    L = tb["L"]
    nsc = T // L
    scale = DK_RET ** -0.5

    def body(x_ref, cos_ref, sin_ref, m_ref, a_ref, b_ref, gl_ref, o_ref, st_ref, dy_ref, _, d_ref, ds_s):
        @pl.when(pl.program_id(1) == 0)
        def _():
            ds_s[...] = jnp.zeros_like(ds_s)

        cs, sn = cos_ref[...], sin_ref[...]
        mm, av, bv = m_ref[...], a_ref[...], b_ref[...]
        qt = _rot(x_ref[:, RQ].astype(F32), cs, sn) * scale
        kt = _rot(x_ref[:, RK].astype(F32), cs, sn)
        qb, kb, vb = qt.astype(BF), kt.astype(BF), x_ref[:, RV]
        pb = (lax.dot_general(qb, kb, NT_DIMS, preferred_element_type=F32) * mm).astype(BF)
        ov = o_ref[...]
        r = _rms_r(ov)
        oh = ov * r
        gv = x_ref[:, RG].astype(F32)
        sg = _sigmoid(gv)
        dyv = dy_ref[...].astype(F32)
        d_ref[:, RG] = (dyv * oh * (sg * (1.0 + gv * (1.0 - sg)))).astype(BF)
        doh = dyv * gv * sg
        dob = (r * (doh - oh * jnp.mean(doh * oh, axis=-1, keepdims=True))).astype(BF)
        dsb = ds_s[...].astype(BF)
        spb = st_ref[...].astype(BF)
        dpb = (lax.dot_general(dob, vb, NT_DIMS, preferred_element_type=F32) * mm).astype(BF)
        dqt = (jnp.dot(dpb, kb, preferred_element_type=F32)
               + lax.dot_general(dob, spb, NT_DIMS, preferred_element_type=F32) * av)
        dkt = (lax.dot_general(dpb, qb, TN_DIMS, preferred_element_type=F32)
               + lax.dot_general(vb, dsb, NT_DIMS, preferred_element_type=F32) * bv)
        dv = (lax.dot_general(pb, dob, TN_DIMS, preferred_element_type=F32)
              + jnp.dot((kt * bv).astype(BF), dsb, preferred_element_type=F32))
        ds_s[...] = ds_s[...] * gl_ref[...] + lax.dot_general((qt * av).astype(BF), dob, TN_DIMS,
                                                              preferred_element_type=F32)
        d_ref[:, RQ] = (_unrot(dqt, cs, sn) * scale).astype(BF)
        d_ref[:, RK] = _unrot(dkt, cs, sn).astype(BF)
        d_ref[:, RV] = dv.astype(BF)

    def rev(i):
        return nsc - 1 - i

    group = _group_spec(big, RET_BASE, RET_GROUP, L, lambda h, i: (rev(i), h))
    tab = pl.BlockSpec((L, DK_RET), lambda h, i: (rev(i), 0))
    per_head = pl.BlockSpec((None, L, DK_RET), lambda h, i: (h, 0, 0))
    out = pl.BlockSpec((L, LANE), lambda h, i: (rev(i), h))
    return pl.pallas_call(
        body, name=name, grid=(H_RET, nsc),
        in_specs=[group, tab, tab,
                  pl.BlockSpec((None, L, L), lambda h, i: (h, 0, 0)), per_head, per_head,
                  pl.BlockSpec((None, 1, DK_RET), lambda h, i: (h, 0, 0)),
                  out, pl.BlockSpec((None, None, DK_RET, DK_RET), lambda h, i: (rev(i), h, 0, 0)), out,
                  pl.BlockSpec(memory_space=pl.ANY)],
        out_specs=group,
        out_shape=_sds(dbig.shape, BF),
        scratch_shapes=[pltpu.VMEM((DK_RET, DK_RET), F32)],
        input_output_aliases={10: 0}, compiler_params=_cp((PAR, ARB)),
    )(big, tb["cos"], tb["sin"], tb["M"], tb["a"], tb["b"], tb["gl"], o, st, dy, dbig)


def _relbias_onehot(n):
    mm = lax.broadcasted_iota(jnp.int32, (RB_PAD, ATT_TOEP), 1)
    rr = lax.broadcasted_iota(jnp.int32, (RB_PAD, ATT_TOEP), 0)
    idx = jnp.clip(n + ATT_TOEP - mm, 0, 2 * REL_CLIP)
    return (rr == idx).astype(F32)


def _split3(x):
    hi = x.astype(BF).astype(F32)
    mid = (x - hi).astype(BF).astype(F32)
    lo = x - hi - mid
    return jnp.concatenate([hi, mid, lo], axis=0).astype(BF)


def _join3(y):
    k = y.shape[0] // 3
    return (y[:k] + y[k:2 * k]) + y[2 * k:]


def _relbias_expand(rbp, name):
    far = ATT_SPAN - ATT_TOEP

    def body(rb_ref, o_ref):
        rb = rb_ref[...]
        const = jnp.broadcast_to(rb[:, 2 * REL_CLIP:2 * REL_CLIP + 1], (H_ATT, far))

        rb3 = _split3(rb)

        def row(n, c):
            toep = _join3(jnp.dot(rb3, _relbias_onehot(n).astype(BF), preferred_element_type=F32))
            m = lax.broadcasted_iota(jnp.int32, (1, ATT_SPAN), 1)
            d = n // CHUNK + N_PREV - m // CHUNK
            neg = jnp.where((d >= 0) & (d <= N_PREV), 0.0, NEG_INF).astype(F32)
            o_ref[n] = jnp.concatenate([const, toep], axis=1) + neg
            return c

        lax.fori_loop(0, ATT_TQ, row, 0)

    return pl.pallas_call(
        body, name=name,
        in_specs=[pl.BlockSpec(memory_space=pltpu.VMEM)],
        out_specs=pl.BlockSpec(memory_space=pltpu.VMEM),
        out_shape=_sds((ATT_TQ, H_ATT, ATT_SPAN), F32),
    )(rbp)


def _relbias_grad(dbt, name):
    far = ATT_SPAN - ATT_TOEP

    def body(d_ref, o_ref):
        def row(n, carry):
            acc, cs = carry
            dn = d_ref[n]
            acc = acc + _join3(lax.dot_general(_split3(dn[:, far:]), _relbias_onehot(n).astype(BF), NT_DIMS,
                                               preferred_element_type=F32))
            cs = cs + jnp.sum(dn[:, :far], axis=1, keepdims=True)
            return acc, cs

        acc, cs = lax.fori_loop(0, ATT_TQ, row, (jnp.zeros((H_ATT, RB_PAD), F32), jnp.zeros((H_ATT, 1), F32)))
        rr = lax.broadcasted_iota(jnp.int32, (H_ATT, RB_PAD), 1)
        o_ref[...] = acc + jnp.where(rr == 2 * REL_CLIP, cs, 0.0)

    return pl.pallas_call(
        body, name=name,
        in_specs=[pl.BlockSpec(memory_space=pltpu.VMEM)],
        out_specs=pl.BlockSpec(memory_space=pltpu.VMEM),
        out_shape=_sds((H_ATT, RB_PAD), F32),
    )(dbt)


def _att_pad_fill(dst_s, src_ref, cols, T):
    dst_s[pl.ds(0, ATT_PAD), :] = jnp.zeros((ATT_PAD, LANE), dst_s.dtype)
    R = min(512, T)

    def cp(t, c):
        dst_s[pl.ds(pl.multiple_of(ATT_PAD + t * R, LANE), R), :] = src_ref[pl.ds(pl.multiple_of(t * R, R), R), cols]
        return c

    lax.fori_loop(0, T // R, cp, 0)


ATT_WIN = ATT_SUB * ATT_TQ + ATT_PAD


def _att_probs(s_full, sub, bias, t0):
    s = s_full[sub * ATT_TQ:(sub + 1) * ATT_TQ, sub * ATT_TQ:sub * ATT_TQ + ATT_SPAN] * (DH_ATT ** -0.5) + bias
    key_pos = t0 + sub * ATT_TQ - ATT_PAD + lax.broadcasted_iota(jnp.int32, (1, ATT_SPAN), 1)
    s = jnp.where(key_pos >= 0, s, NEG_INF)
    p = jnp.exp(s - jnp.max(s, axis=-1, keepdims=True))
    return p * (1.0 / jnp.sum(p, axis=-1, keepdims=True))


def _att_band(tiles):
    rows = []
    for sub, t in enumerate(tiles):
        parts = []
        if sub:
            parts.append(jnp.zeros((ATT_TQ, sub * ATT_TQ), BF))
        parts.append(t)
        if sub < ATT_SUB - 1:
            parts.append(jnp.zeros((ATT_TQ, (ATT_SUB - 1 - sub) * ATT_TQ), BF))
        rows.append(jnp.concatenate(parts, axis=1))
    return jnp.concatenate(rows, axis=0)


def _att_head_masks(x):
    first = lax.broadcasted_iota(jnp.int32, (1, LANE), 1) < DH_ATT
    zero = jnp.zeros_like(x)
    return first, (jnp.where(first, x, zero), jnp.where(first, zero, x))


def _att_fwd(big, bias, name, comm=None):
    T = big.shape[0]
    rows = ATT_SUB * ATT_TQ
    nt = T // rows

    def body(x_ref, b_ref, y_ref, kp_s, vp_s):
        i = pl.program_id(1)

        @pl.when(i == 0)
        def _():
            _att_pad_fill(kp_s, x_ref, AK, T)
            _att_pad_fill(vp_s, x_ref, AV, T)

        t0 = pl.multiple_of(i * rows, rows)
        kw = kp_s[pl.ds(t0, ATT_WIN), :]
        vw = vp_s[pl.ds(t0, ATT_WIN), :]
        first, qm = _att_head_masks(x_ref[pl.ds(t0, rows), AQ])
        outs = []
        for hh in range(2):
            s_full = lax.dot_general(qm[hh], kw, NT_DIMS, preferred_element_type=F32)
            band = _att_band([_att_probs(s_full, sub, b_ref[hh], t0).astype(BF) for sub in range(ATT_SUB)])
            outs.append(jnp.dot(band, vw, preferred_element_type=F32))
        y_ref[...] = jnp.where(first, outs[0], outs[1]).astype(BF)

    return _call(
        body, name=name, grid=(H_ATT // 2, nt), args=(big, bias), comm=comm,
        in_specs=[_group_spec(big, ATT_BASE, ATT_GROUP, T, lambda p, i: (0, p)),
                  pl.BlockSpec((2, ATT_TQ, ATT_SPAN), lambda p, i: (p, 0, 0))],
        out_specs=[pl.BlockSpec((rows, LANE), lambda p, i: (i, p))],
        out_shape=[_sds((T, BRANCH_W), BF)],
        scratch_shapes=[pltpu.VMEM((T + ATT_PAD, LANE), BF), pltpu.VMEM((T + ATT_PAD, LANE), BF)],
        sem=(ARB, ARB))


def _att_bwd(big, bias, dy, dbig, name, comm=None):
    T = big.shape[0]
    rows = ATT_SUB * ATT_TQ
    nt = T // rows
    scale = DH_ATT ** -0.5

    def body(x_ref, b_ref, dy_ref, _, d_ref, db_ref, kp_s, vp_s, dk_s, dv_s):
        i = pl.program_id(1)

        @pl.when(i == 0)
        def _():
            _att_pad_fill(kp_s, x_ref, AK, T)
            _att_pad_fill(vp_s, x_ref, AV, T)
            dk_s[...] = jnp.zeros_like(dk_s)
            dv_s[...] = jnp.zeros_like(dv_s)
            db_ref[...] = jnp.zeros_like(db_ref)

        t0 = pl.multiple_of(i * rows, rows)
        win = pl.ds(t0, ATT_WIN)
        kw = kp_s[win, :]
        vw = vp_s[win, :]
        first, qm = _att_head_masks(x_ref[pl.ds(t0, rows), AQ])
        _, dom = _att_head_masks(dy_ref[...])
        dqs, dkt, dvt = [], None, None
        for hh in range(2):
            s_full = lax.dot_general(qm[hh], kw, NT_DIMS, preferred_element_type=F32)
            dp_full = lax.dot_general(dom[hh], vw, NT_DIMS, preferred_element_type=F32)
            ps, dss, db = [], [], None
            for sub in range(ATT_SUB):
                pn = _att_probs(s_full, sub, b_ref[hh], t0)
                dp = dp_full[sub * ATT_TQ:(sub + 1) * ATT_TQ, sub * ATT_TQ:sub * ATT_TQ + ATT_SPAN]
                ds = pn * (dp - jnp.sum(dp * pn, axis=-1, keepdims=True))
                db = ds if db is None else db + ds
                ps.append(pn.astype(BF))
                dss.append(ds.astype(BF))
            db_ref[hh] += db
            ds_band, p_band = _att_band(dss), _att_band(ps)
            dqs.append(jnp.dot(ds_band, kw, preferred_element_type=F32))
            qt = jnp.transpose(qm[hh].astype(F32)).astype(BF)
            dot_ = jnp.transpose(dom[hh].astype(F32)).astype(BF)
            dk_h = jnp.dot(qt, ds_band, preferred_element_type=F32)
            dv_h = jnp.dot(dot_, p_band, preferred_element_type=F32)
            dkt = dk_h if dkt is None else dkt + dk_h
            dvt = dv_h if dvt is None else dvt + dv_h
        d_ref[pl.ds(t0, rows), AQ] = (jnp.where(first, dqs[0], dqs[1]) * scale).astype(BF)
        dk_s[win, :] += jnp.transpose(dkt) * scale
        dv_s[win, :] += jnp.transpose(dvt)

        @pl.when(i == nt - 1)
        def _():
            R = min(512, T)

            def cp(t, c):
                src = pl.ds(pl.multiple_of(ATT_PAD + t * R, LANE), R)
                dst = pl.ds(pl.multiple_of(t * R, R), R)
                d_ref[dst, AK] = dk_s[src, :].astype(BF)
                d_ref[dst, AV] = dv_s[src, :].astype(BF)
                return c

            lax.fori_loop(0, T // R, cp, 0)

    group = _group_spec(big, ATT_BASE, ATT_GROUP, T, lambda p, i: (0, p))
    tile = pl.BlockSpec((rows, LANE), lambda p, i: (i, p))
    bspec = pl.BlockSpec((2, ATT_TQ, ATT_SPAN), lambda p, i: (p, 0, 0))
    return _call(
        body, name=name, grid=(H_ATT // 2, nt), args=(big, bias, dy, dbig), comm=comm, aliases={3: 0}, vmem_mb=56,
        in_specs=[group, bspec, tile, pl.BlockSpec(memory_space=pl.ANY)],
        out_specs=[group, bspec],
        out_shape=[_sds(dbig.shape, BF), _sds((H_ATT, ATT_TQ, ATT_SPAN), F32)],
        scratch_shapes=[pltpu.VMEM((T + ATT_PAD, LANE), BF), pltpu.VMEM((T + ATT_PAD, LANE), BF),
                        pltpu.VMEM((T + ATT_PAD, LANE), F32), pltpu.VMEM((T + ATT_PAD, LANE), F32)],
        sem=(ARB, ARB))


def _merge_fwd(x1, big, ys, wb, wo, name):
    T, D = x1.shape
    tm = min(TM, T)

    def body(x_ref, gp_ref, yc_ref, yr_ref, ya_ref, wb_ref, wo_ref, x2_ref, p_ref, mg_ref):
        merged = jnp.zeros((tm, D), F32)
        for i, y_ref in enumerate((yc_ref, yr_ref, ya_ref)):
            cols = slice(i * D, (i + 1) * D)
            pb = jnp.dot(y_ref[...], wb_ref[i], preferred_element_type=F32).astype(BF)
            p_ref[:, cols] = pb
            merged = merged + _sigmoid(gp_ref[:, cols].astype(F32)) * pb.astype(F32)
        mb = merged.astype(BF)
        mg_ref[...] = mb
        x2_ref[...] = x_ref[...] + jnp.dot(mb, wo_ref[...], preferred_element_type=F32)

    tok = pl.BlockSpec((tm, D), lambda i: (i, 0))
    wide = pl.BlockSpec((tm, 3 * D), lambda i: (i, 0))
    yspec = pl.BlockSpec((tm, BRANCH_W), lambda i: (i, 0))
    return pl.pallas_call(
        body, name=name, grid=(T // tm,),
        in_specs=[tok, wide, yspec, yspec, yspec,
                  pl.BlockSpec((3, BRANCH_W, D), lambda i: (0, 0, 0)),
                  pl.BlockSpec((D, D), lambda i: (0, 0))],
        out_specs=[tok, wide, tok],
        out_shape=[_sds((T, D), F32), _sds((T, 3 * D), BF), _sds((T, D), BF)],
        compiler_params=_cp((PAR,)),
    )(x1, big, *ys, wb, wo)


def _merge_bwd(dx2, big, p, wb, wo, name):
    T, D = dx2.shape
    tm = min(TM, T)

    def body(dx_ref, gp_ref, p_ref, wb_ref, wo_ref, dp_ref, dgp_ref, dyc_ref, dyr_ref, dya_ref, dxb_ref):
        dxb = dx_ref[...].astype(BF)
        dxb_ref[...] = dxb
        dm = lax.dot_general(dxb, wo_ref[...], NT_DIMS, preferred_element_type=F32)
        for i, dy_ref in enumerate((dyc_ref, dyr_ref, dya_ref)):
            cols = slice(i * D, (i + 1) * D)
            gt = _sigmoid(gp_ref[:, cols].astype(F32))
            dpb = (dm * gt).astype(BF)
            dp_ref[:, cols] = dpb
            dgp_ref[:, cols] = (dm * p_ref[:, cols].astype(F32) * gt * (1.0 - gt)).astype(BF)
            dy_ref[...] = lax.dot_general(dpb, wb_ref[i], NT_DIMS, preferred_element_type=F32).astype(BF)

    tok = pl.BlockSpec((tm, D), lambda i: (i, 0))
    wide = pl.BlockSpec((tm, 3 * D), lambda i: (i, 0))
    yspec = pl.BlockSpec((tm, BRANCH_W), lambda i: (i, 0))
    return pl.pallas_call(
        body, name=name, grid=(T // tm,),
        in_specs=[tok, wide, wide,
                  pl.BlockSpec((3, BRANCH_W, D), lambda i: (0, 0, 0)),
                  pl.BlockSpec((D, D), lambda i: (0, 0))],
        out_specs=[wide, wide, yspec, yspec, yspec, tok],
        out_shape=[_sds((T, 3 * D), BF), _sds(big.shape, BF)] + [_sds((T, BRANCH_W), BF)] * 3 + [_sds((T, D), BF)],
        compiler_params=_cp((PAR,)),
    )(dx2, big, p, wb, wo)


def _loss_head(x, tgt, fw, name):
    T, D = x.shape
    tm = min(TM, T)

    def body(x_ref, t_ref, w_ref, loss_ref, dx_ref, dw_ref):
        @pl.when(pl.program_id(0) == 0)
        def _():
            loss_ref[...] = jnp.zeros_like(loss_ref)
            dw_ref[...] = jnp.zeros_like(dw_ref)

        xv = x_ref[...]
        wv = w_ref[...]
        e = xv * _rms_r(xv) * wv - t_ref[...]
        loss_ref[...] += 0.5 * jnp.sum(jnp.mean(e * e, axis=-1, keepdims=True))
        dx, dn = _rms_bwd(e * (1.0 / D), xv, wv)
        dx_ref[...] = dx
        dw_ref[...] += dn

    tok = pl.BlockSpec((tm, D), lambda i: (i, 0))
    return pl.pallas_call(
        body, name=name, grid=(T // tm,),
        in_specs=[tok, tok, pl.BlockSpec((1, D), lambda i: (0, 0))],
        out_specs=[pl.BlockSpec((8, LANE), lambda i: (0, 0)), tok, pl.BlockSpec((1, D), lambda i: (0, 0))],
        out_shape=[_sds((8, LANE), F32), _sds((T, D), F32), _sds((1, D), F32)],
        compiler_params=_cp((ARB,)),
    )(x, tgt, fw)


def _block_rows(rows, cols):
    cap = max(8, (1 << 18) // cols)
    best = None
    for r in range(8, rows + 1, 8):
        if rows % r == 0 and r <= cap:
            best = r
    return best if best is not None else rows


def _sum4(land, l, n_layers, name, prev=None, comm=None):
    _, rows, cols = land.shape
    br = _block_rows(rows, cols)

    def body(*refs):
        l_ref, o_ref = refs[0], refs[-1]
        o_ref[...] = ((l_ref[3].astype(F32) + l_ref[0].astype(F32)) + l_ref[1].astype(F32)) + l_ref[2].astype(F32)

    in_specs = [pl.BlockSpec((4, br, cols), lambda i: (0, i, 0))]
    args = [land]
    aliases = {}
    if prev is not None:
        in_specs.append(pl.BlockSpec(memory_space=pl.ANY))
        args.append(prev)
        aliases = {1: 0}
    main, extra = _call(
        body, name=name, grid=(rows // br,), args=args, in_specs=in_specs,
        out_specs=[pl.BlockSpec((None, br, cols), lambda i: (l, i, 0))],
        out_shape=[_sds((n_layers, rows, cols), F32)], aliases=aliases, sem=(ARB,), comm=comm)
    return main[0], extra


def _adamw_math(w, g, m, v):
    m = ADAM_B1 * m + (1.0 - ADAM_B1) * g
    v = ADAM_B2 * v + (1.0 - ADAM_B2) * (g * g)
    m_hat = m / (1.0 - ADAM_B1 ** ADAM_STEP)
    v_hat = v / (1.0 - ADAM_B2 ** ADAM_STEP)
    delta = -ADAM_LR * (m_hat / (jnp.sqrt(v_hat) + ADAM_EPS) + ADAM_WD * w)
    return delta, m, v


def _adamw(w, ga, gb, m, v, name):
    rows, cols = w.shape
    br = _block_rows(rows, cols)
    two = gb is not None

    def body(*refs):
        if two:
            w_ref, ga_ref, gb_ref, m_ref, v_ref, g_ref, d_ref, nm_ref, nv_ref = refs
            g = ga_ref[...] + gb_ref[...]
        else:
            w_ref, ga_ref, m_ref, v_ref, g_ref, d_ref, nm_ref, nv_ref = refs
            g = ga_ref[...]
        d, nm, nv = _adamw_math(w_ref[...], g, m_ref[...], v_ref[...])
        g_ref[...] = g
        d_ref[...] = d
        nm_ref[...] = nm
        nv_ref[...] = nv

    blk = pl.BlockSpec((br, cols), lambda i: (i, 0))
    args = [w, ga] + ([gb] if two else []) + [m, v]
    return pl.pallas_call(
        body, name=name, grid=(rows // br,),
        in_specs=[blk] * len(args), out_specs=[blk] * 4,
        out_shape=[_sds((rows, cols), F32)] * 4,
        compiler_params=_cp((PAR,)),
    )(*args)


class _CoreSwap:
    def __init__(self, srcs):
        self.srcs = list(srcs)
        n = len(self.srcs)
        self.out_shape = [_sds(s.shape, s.dtype) for s in self.srcs]
        self.scratch = [pltpu.SemaphoreType.DMA((n,)), pltpu.SemaphoreType.DMA((n,))]

    def _copies(self, src, dst, sems):
        x, y, c = _place()
        return [pltpu.make_async_remote_copy(
            src_ref=src[k], dst_ref=dst[k], send_sem=sems[0].at[k], recv_sem=sems[1].at[k],
            device_id=(x, y, 1 - c), device_id_type=MESH) for k in range(len(src))]

    def start(self, src, dst, sems):
        for cp in self._copies(src, dst, sems):
            cp.start()

    def wait(self, src, dst, sems):
        for cp in self._copies(src, dst, sems):
            cp.wait()


def _allreduce_small(v, name):
    rows = v.shape[0]
    flips = [(fx, fy, fc) for fx in (0, 1) for fy in (0, 1) for fc in (0, 1) if fx or fy or fc]

    def body(v_ref, o_ref, all_s, ssem, rsem):
        x, y, c = _place()

        def peer(f):
            return (x + f[0] - 2 * x * f[0], y + f[1] - 2 * y * f[1], c + f[2] - 2 * c * f[2])

        def slot(p):
            return all_s.at[4 * p[0] + 2 * p[1] + p[2]]

        def copy(k, f, owner):
            return pltpu.make_async_remote_copy(
                src_ref=v_ref, dst_ref=slot(owner), send_sem=ssem.at[k], recv_sem=rsem.at[k],
                device_id=peer(f), device_id_type=MESH)

        sends = [copy(k, f, (x, y, c)) for k, f in enumerate(flips)]
        for cp in sends:
            cp.start()
        all_s[4 * x + 2 * y + c] = v_ref[...]
        for k, f in enumerate(flips):
            copy(k, f, peer(f)).wait_recv()
        for cp in sends:
            cp.wait_send()
        acc = all_s[0]
        for d in range(1, 8):
            acc = acc + all_s[d]
        o_ref[...] = acc

    return pl.pallas_call(
        body, name=name,
        in_specs=[pl.BlockSpec(memory_space=pltpu.VMEM)],
        out_specs=pl.BlockSpec(memory_space=pltpu.VMEM),
        out_shape=_sds((rows, LANE), F32),
        scratch_shapes=[pltpu.VMEM((8, rows, LANE), F32), pltpu.SemaphoreType.DMA((7,)), pltpu.SemaphoreType.DMA((7,))],
    )(v)


BIG_NAMES = ("ffn1_w_gate", "ffn1_w_up", "ffn1_w_down", "w_in", "w_branch", "w_merge_gate", "w_out",
             "ffn2_w_gate", "ffn2_w_up", "ffn2_w_down")


FFN1 = ("ffn1_w_gate", "ffn1_w_up", "ffn1_w_down")
FFN2 = ("ffn2_w_gate", "ffn2_w_up", "ffn2_w_down")
MIX_IN = ("w_in", "w_merge_gate")
MIX_OUT = ("w_branch", "w_out")


def _keys(names, l):
    return [(n, l) for n in names]


def _local_step(x, tgt, small, convw_full, wx, n_layers):
    T, D = x.shape
    L = n_layers
    ns = N_SHARD
    dq = D // ns
    W = wx.w

    def hosted(call, keys, scatter=False):
        comm = wx.pieces(keys, scatter)
        main, extra = call(comm)
        if comm is not None:
            wx.arrived(keys, extra, scatter)
        return main

    def mixer_views(l):
        g4 = W[("w_merge_gate", l)]
        gates = jnp.transpose(g4, (0, 2, 1, 3)).reshape(D, 3 * D)
        win = jnp.transpose(W[("w_in", l)], (1, 0, 2)).reshape(D, -1)
        return jnp.concatenate([gates, _permute_in_cols(win)], axis=-1)

    def out_views(l):
        wb = jnp.transpose(W[("w_branch", l)], (1, 2, 0, 3)).reshape(3, BRANCH_W, D)
        wo = W[("w_out", l)].reshape(D, D)
        return wb, wo

    tb = _ret_tables(T)
    rb_pad = jnp.pad(small["rel_bias"], ((0, 0), (0, 0), (0, RB_PAD - N_REL)))

    saved = []
    h = x
    for l in range(L):
        s = {"x0": h}
        nxt = l + 1
        x1, s["g1"], s["u1"] = hosted(
            lambda c: _ffn_fwd(h, small["ffn1_norm"][l][None], W[("ffn1_w_gate", l)], W[("ffn1_w_up", l)],
                               W[("ffn1_w_down", l)], f"ffn1_fwd_{l}", comm=c), _keys(MIX_IN, l))
        s["x1"] = x1
        s["wbig"] = mixer_views(l)
        big, s["h"] = _inproj_fwd(x1, small["mix_norm"][l][None], s["wbig"], f"inproj_fwd_{l}")
        s["big"] = big
        s["bias"] = jnp.transpose(_relbias_expand(rb_pad[l], f"relbias_expand_{l}"), (1, 0, 2))
        s["yc"] = _conv_fwd(big, convw_full[l], f"conv_fwd_{l}")
        s["yr"], s["o"], s["st"] = hosted(lambda c: _ret_fwd(big, tb, f"ret_fwd_{l}", comm=c), _keys(MIX_OUT, l))
        (s["ya"],) = hosted(lambda c: _att_fwd(big, s["bias"], f"att_fwd_{l}", comm=c), _keys(FFN2, l))
        s["wb"], s["wo"] = out_views(l)
        x2, s["p"], s["mg"] = _merge_fwd(x1, big, (s["yc"], s["yr"], s["ya"]), s["wb"], s["wo"], f"merge_fwd_{l}")
        s["x2"] = x2
        h, s["g2"], s["u2"] = hosted(
            lambda c: _ffn_fwd(x2, small["ffn2_norm"][l][None], W[("ffn2_w_gate", l)], W[("ffn2_w_up", l)],
                               W[("ffn2_w_down", l)], f"ffn2_fwd_{l}", comm=c), _keys(FFN1, nxt) if nxt < L else [])
        saved.append(s)

    loss_p, dx, d_final = _loss_head(h, tgt, small["final_norm"][None], "loss_head")

    gs = {"final_norm": d_final[0]}
    for k in ("ffn1_norm", "mix_norm", "ffn2_norm", "rel_bias", "conv_w"):
        gs[k] = [None] * L
    tk = min(2048, T)
    nk = T // tk

    def ffn_back(pre, l, dxo, x_in, g, u, first_keys, second_keys):
        nw = small[pre + "_norm"][l][None]
        dgv, duv, av, hb, dacc = hosted(
            lambda c: _ffn_bwd_hidden(dxo, x_in, nw, g, u, W[(pre + "_w_down", l)], f"{pre}_bwd_hidden_{l}", comm=c),
            first_keys, scatter=True)
        dxn, dn = hosted(
            lambda c: _ffn_bwd_resid(dgv, duv, W[(pre + "_w_gate", l)], W[(pre + "_w_up", l)], x_in, nw, dxo,
                                     f"{pre}_bwd_resid_{l}", comm=c),
            second_keys, scatter=True)
        gs[pre + "_norm"][l] = dn[0]
        return dxn, (hb, dgv, duv, av, dacc)

    def ffn_grads(pre, l, hb, dgv, duv, av, dacc, chain=False, carry=()):
        fs = dgv.shape[-1]
        hspec = pl.BlockSpec((tk, D), lambda p, q, k: (k, 0))
        sspec = pl.BlockSpec((None, tk, fs), lambda p, q, k: (p, k, 0))
        up_spec = pl.BlockSpec((None, D, fs), lambda p, q, k: (p, 0, 0))
        down_spec = pl.BlockSpec((None, fs, D), lambda p, q, k: (p, 0, 0))
        jobs = [(pre + "_w_gate", hb, dgv, hspec, sspec, (ns, D, fs), up_spec),
                (pre + "_w_up", hb, duv, hspec, sspec, (ns, D, fs), up_spec),
                (pre + "_w_down", av, dacc, sspec, hspec, (ns, fs, D), down_spec)]
        before = None
        for nm, a, b, a_spec, b_spec, shape, o_spec in jobs:
            def product(c):
                r = _tn(a, b, a_spec, b_spec, _sds(shape, BF), o_spec, (ns, 1, nk), f"d{nm}_{l}", comm=c)
                return (r, []) if c is None else r
            if before is None:
                keys = list(carry)
            else:
                keys = [before] if chain else []
            wx.g[(nm, l)] = hosted(product, keys, scatter=True)
            before = (nm, l)

    for l in reversed(range(L)):
        s = saved[l]
        above = _keys(FFN1, l + 1) if l + 1 < L else [None] * 3
        dx, parts = ffn_back("ffn2", l, dx, s["x2"], s["g2"], s["u2"], [k for k in above[:1] if k],
                             [k for k in above[1:2] if k])
        ffn_grads("ffn2", l, *parts, carry=[k for k in above[2:] if k])
        dp, dbig, dyc, dyr, dya, dxb = _merge_bwd(dx, s["big"], s["p"], s["wb"], s["wo"], f"merge_bwd_{l}")
        wx.g[("w_out", l)] = _tn(
            s["mg"], dxb, pl.BlockSpec((tk, dq), lambda p, q, k: (k, p)), pl.BlockSpec((tk, D), lambda p, q, k: (k, 0)),
            _sds((ns, dq, D), BF), pl.BlockSpec((None, dq, D), lambda p, q, k: (p, 0, 0)), (ns, 1, nk), f"dw_out_{l}")
        gb = None
        for i, yv in enumerate((s["yc"], s["yr"], s["ya"])):
            gb = _tn(yv, dp,
                     pl.BlockSpec((tk, BRANCH_W), lambda p, q, k: (k, 0)),
                     pl.BlockSpec((tk, dq), lambda p, q, k, i=i: (k, i * ns + p)),
                     _sds((ns, 3, BRANCH_W, dq), BF),
                     pl.BlockSpec((None, None, BRANCH_W, dq), lambda p, q, k, i=i: (p, i, 0, 0)),
                     (ns, 1, nk), f"dw_branch{i}_{l}", prev=gb)
        wx.g[("w_branch", l)] = gb
        dbig, dcw = _conv_bwd(s["big"], dyc, convw_full[l], dbig, f"conv_bwd_{l}")
        gs["conv_w"][l] = dcw
        dbig = _ret_bwd(s["big"], s["o"], s["st"], dyr, tb, dbig, f"ret_bwd_{l}")
        dbig, dbias = hosted(lambda c: _att_bwd(s["big"], s["bias"], dya, dbig, f"att_bwd_{l}", comm=c),
                             _keys(FFN2, l), scatter=True)
        gs["rel_bias"][l] = _relbias_grad(jnp.transpose(dbias, (1, 0, 2)), f"relbias_grad_{l}")[:, :N_REL]
        n_in = N_SEG * BRANCH_W
        bn = 1024 if (3 * D) % 1024 == 0 else BRANCH_W
        dwp = _tn(s["h"], dbig, pl.BlockSpec((tk, D), lambda p, q, k: (k, 0)),
                  pl.BlockSpec((tk, bn), lambda p, q, k: (k, 3 * D // bn + q)),
                  _sds((D, n_in), BF), pl.BlockSpec((D, bn), lambda p, q, k: (0, q)), (1, n_in // bn, nk), f"dw_in_{l}")
        wx.g[("w_in", l)] = jnp.transpose(_unpermute_in_cols(dwp).reshape(D, ns, n_in // ns), (1, 0, 2))
        wx.g[("w_merge_gate", l)] = _tn_gates(s["h"], dbig, ns, tk, f"dw_merge_gate_{l}")
        dx, dn = hosted(
            lambda c: _inproj_bwd(dbig, s["wbig"], s["x1"], small["mix_norm"][l][None], dx, f"inproj_bwd_{l}", comm=c),
            [("w_in", l)], scatter=True)
        gs["mix_norm"][l] = dn[0]
        dx, parts = ffn_back("ffn1", l, dx, s["x0"], s["g1"], s["u1"],
                             [("w_merge_gate", l), ("w_branch", l), ("w_out", l)], [])
        ffn_grads("ffn1", l, *parts, chain=(l == 0))

    for k in ("ffn1_norm", "mix_norm", "ffn2_norm", "rel_bias", "conv_w"):
        gs[k] = jnp.stack(gs[k])
    return loss_p, dx, gs


class _Exchange:
    def __init__(self, shards):
        self.shards = shards
        self.w = {}
        self.g = {}
        self.landed = {}

    def own(self, key):
        return self.shards[key[0]][key[1]].astype(BF)

    def pieces(self, keys, scatter):
        if not keys:
            return None
        if scatter:
            return _Pieces([self.g[k] for k in keys], True)
        return _HalfGather([_halves(self.own(k)) for k in keys])

    def arrived(self, keys, outs, scatter):
        for k, o in zip(keys, outs):
            if scatter:
                self.landed[k] = o
            else:
                self.w[k] = o.reshape((N_SHARD,) + self.shards[k[0]].shape[1:])


def _halves(a):
    return a.reshape(2, -1, a.shape[-1])


W_NAMES = ("ffn1_norm", "ffn1_w_gate", "ffn1_w_up", "ffn1_w_down", "mix_norm", "w_in", "conv_w", "rel_bias", "w_branch",
           "w_merge_gate", "w_out", "ffn2_norm", "ffn2_w_gate", "ffn2_w_up", "ffn2_w_down", "final_norm")


def _as2d(a):
    return a.reshape(1, -1) if a.ndim == 1 else a.reshape(-1, a.shape[-1])


def kernel(x, ffn1_norm, ffn1_w_gate, ffn1_w_up, ffn1_w_down, mix_norm, w_in, conv_w, rel_bias, w_branch, w_merge_gate, w_out, ffn2_norm, ffn2_w_gate, ffn2_w_up, ffn2_w_down, final_norm, loss_target, m_ffn1_norm, m_ffn1_w_gate, m_ffn1_w_up, m_ffn1_w_down, m_mix_norm, m_w_in, m_conv_w, m_rel_bias, m_w_branch, m_w_merge_gate, m_w_out, m_ffn2_norm, m_ffn2_w_gate, m_ffn2_w_up, m_ffn2_w_down, m_final_norm, v_ffn1_norm, v_ffn1_w_gate, v_ffn1_w_up, v_ffn1_w_down, v_mix_norm, v_w_in, v_conv_w, v_rel_bias, v_w_branch, v_w_merge_gate, v_w_out, v_ffn2_norm, v_ffn2_w_gate, v_ffn2_w_up, v_ffn2_w_down, v_final_norm):
    given = dict(locals())
    w = {n: given[n] for n in W_NAMES}
    m = {n: given["m_" + n] for n in W_NAMES}
    v = {n: given["v_" + n] for n in W_NAMES}
    my_chip = 2 * lax.axis_index("x") + lax.axis_index("y")
    L = w_in.shape[0]

    wx = _Exchange({n: w[n] for n in BIG_NAMES})
    first = _keys(FFN1, 0)
    got = _comm_alone(_HalfGather([_halves(wx.own(k)) for k in first] + [_halves(conv_w)]), "gather_first")
    wx.arrived(first, got[:-1], False)
    convw_full = jnp.transpose(got[-1].reshape((N_SHARD,) + conv_w.shape), (1, 2, 0, 3)).reshape(
        conv_w.shape[0], conv_w.shape[1], -1)

    small = {n: w[n] for n in ("ffn1_norm", "mix_norm", "ffn2_norm", "final_norm", "rel_bias")}
    loss_p, grad_x, gs = _local_step(x[0], loss_target[0], small, convw_full, wx, L)
    last = [(FFN1[-1], 0)]
    wx.arrived(last, _comm_alone(wx.pieces(last, True), "scatter_last"), True)

    sums, others = [], []
    for n in BIG_NAMES:
        acc = None
        for l in range(L):
            a = wx.landed[(n, l)]
            swap = _CoreSwap(sums[-1:]) if sums and l == L - 1 else None
            acc, came = _sum4(a.reshape(4, -1, a.shape[-1]), l, L, f"sum4_{n}_{l}", prev=acc, comm=swap)
            others += came
        sums.append(acc.reshape(-1, acc.shape[-1]))
    others += _comm_alone(_CoreSwap(sums[-1:]), "swap_last")

    parts = [gs["ffn1_norm"].reshape(-1), gs["mix_norm"].reshape(-1), gs["ffn2_norm"].reshape(-1),
             gs["final_norm"].reshape(-1), gs["rel_bias"].reshape(-1), gs["conv_w"].reshape(-1), loss_p[0]]
    sizes = [p.shape[0] for p in parts]
    flat = jnp.concatenate(parts)
    rows = -(-flat.shape[0] // (8 * LANE)) * 8
    flat = jnp.pad(flat, (0, rows * LANE - flat.shape[0])).reshape(rows, LANE)
    red = _allreduce_small(flat, "allreduce_small").reshape(-1)
    offs = [0]
    for sz in sizes:
        offs.append(offs[-1] + sz)
    sm = {}
    for i, n in enumerate(("ffn1_norm", "mix_norm", "ffn2_norm", "final_norm", "rel_bias", "conv_w")):
        sm[n] = red[offs[i]:offs[i + 1]]
    loss = red[offs[6]]
    sm["conv_w"] = lax.dynamic_slice_in_dim(sm["conv_w"].reshape(conv_w.shape[0], conv_w.shape[1], -1),
                                            my_chip * conv_w.shape[2], conv_w.shape[2], axis=2)

    grads, deltas, new_m, new_v = {}, {}, {}, {}
    big_sum = dict(zip(BIG_NAMES, zip(sums, others)))
    for n in W_NAMES:
        shape = w[n].shape
        if n in big_sum:
            ga, gb = big_sum[n]
        else:
            ga, gb = _as2d(sm[n].reshape(shape)), None
        out = _adamw(_as2d(w[n]), ga, gb, _as2d(m[n]), _as2d(v[n]), f"adamw_{n}")
        grads[n], deltas[n], new_m[n], new_v[n] = (o.reshape(shape) for o in out)

    return (loss, grad_x[None], *[grads[n] for n in W_NAMES], *[deltas[n] for n in W_NAMES],
            *[new_m[n] for n in W_NAMES], *[new_v[n] for n in W_NAMES])
```

```python
import functools
import math

import jax
import jax.numpy as jnp
from jax import lax
from jax.experimental import pallas as pl
from jax.experimental.pallas import tpu as pltpu

F32 = jnp.float32
BF = jnp.bfloat16
MESH = pl.DeviceIdType.MESH
ARB = "arbitrary"
PAR = "parallel"

EPS = 1e-6
NEG_INF = -1e30
ROPE_BASE = 10000.0
CHUNK = 64
BRANCH_W = 512
H_RET = 4
DK_RET = 128
H_ATT = 8
DH_ATT = 64
N_PREV = 8
REL_CLIP = 128
N_REL = 2 * REL_CLIP + 1
N_SHARD = 4
LANE = 128
RET_L = 512
ATT_TQ = 128
ATT_SUB = 4
ATT_PAD = N_PREV * CHUNK
ATT_SPAN = ATT_TQ + ATT_PAD
ATT_TOEP = 2 * REL_CLIP
RB_PAD = 264
TM = 512
TM_FFN = 1024

ADAM_LR = 0.001
ADAM_B1 = 0.9
ADAM_B2 = 0.999
ADAM_EPS = 1e-08
ADAM_WD = 0.01
ADAM_STEP = 10

NT_DIMS = (((1,), (1,)), ((), ()))
TN_DIMS = (((0,), (0,)), ((), ()))


def _cp(sem, vmem_mb=48):
    return pltpu.CompilerParams(dimension_semantics=sem, vmem_limit_bytes=vmem_mb << 20)


def _sds(shape, dtype):
    return jax.ShapeDtypeStruct(tuple(shape), dtype)


def _rms_r(x):
    return lax.rsqrt(jnp.mean(x * x, axis=-1, keepdims=True) + EPS)


def _sigmoid(x):
    return 0.5 * jnp.tanh(0.5 * x) + 0.5


def _rms_bwd(dh, xv, nw):
    r = _rms_r(xv)
    xh = xv * r
    dxh = dh * nw
    dx = r * (dxh - xh * jnp.mean(dxh * xh, axis=-1, keepdims=True))
    return dx, jnp.sum(dh * xh, axis=0, keepdims=True)


def _place():
    return lax.axis_index("x"), lax.axis_index("y"), lax.axis_index("c")


def _other_chips(x, y):
    return [(1 - x, y), (x, 1 - y), (1 - x, 1 - y)]


class _Pieces:
    def __init__(self, srcs, scatter):
        self.srcs = list(srcs)
        self.scatter = scatter
        n = len(self.srcs)
        self.out_shape = [_sds(s.shape if scatter else (N_SHARD,) + s.shape, s.dtype) for s in self.srcs]
        self.scratch = [pltpu.SemaphoreType.DMA((n,)), pltpu.SemaphoreType.DMA((3, n)), pltpu.SemaphoreType.DMA((3, n))]

    def _copies(self, src, dst, sems, waiting):
        lsem, ssem, rsem = sems
        x, y, c = _place()
        mine = 2 * x + y
        n = len(src)

        def remote(j, k, chip, s_ref, d_ref):
            return pltpu.make_async_remote_copy(
                src_ref=s_ref, dst_ref=d_ref, send_sem=ssem.at[j, k], recv_sem=rsem.at[j, k],
                device_id=(chip[0], chip[1], c), device_id_type=MESH)

        chips = list(enumerate(_other_chips(x, y)))
        if self.scatter:
            local = [pltpu.make_async_copy(src[k].at[mine], dst[k].at[3], lsem.at[k]) for k in range(n)]
            sends = [remote(j, k, ch, src[k].at[2 * ch[0] + ch[1]], dst[k].at[j]) for j, ch in chips for k in range(n)]
            recvs = sends
        else:
            local = [pltpu.make_async_copy(src[k], dst[k].at[mine], lsem.at[k]) for k in range(n)]
            sends = [remote(j, k, ch, src[k], dst[k].at[mine]) for j, ch in chips for k in range(n)]
            recvs = [remote(j, k, ch, src[k], dst[k].at[2 * ch[0] + ch[1]]) for j, ch in chips for k in range(n)
                     ] if waiting else []
        return local, sends, recvs

    def start(self, src, dst, sems):
        local, sends, _ = self._copies(src, dst, sems, False)
        for cp in local + sends:
            cp.start()

    def wait(self, src, dst, sems):
        local, sends, recvs = self._copies(src, dst, sems, True)
        for cp in recvs:
            cp.wait_recv()
        for cp in sends:
            cp.wait_send()
        for cp in local:
            cp.wait()


class _HalfGather:
    def __init__(self, srcs):
        self.srcs = list(srcs)
        n = len(self.srcs)
        self.out_shape = [_sds((N_SHARD,) + s.shape, s.dtype) for s in self.srcs]
        self.scratch = [pltpu.SemaphoreType.DMA((n,))] + [pltpu.SemaphoreType.DMA((3, n)) for _ in range(4)]

    def _plan(self, src, dst, sems):
        lsem, s1, r1, s2, r2 = sems
        x, y, c = _place()
        mine = 2 * x + y
        n = len(src)
        chips = [(j, ch, 2 * ch[0] + ch[1]) for j, ch in enumerate(_other_chips(x, y))]

        def copy(s_ref, d_ref, ssem, rsem, to):
            return pltpu.make_async_remote_copy(src_ref=s_ref, dst_ref=d_ref, send_sem=ssem, recv_sem=rsem,
                                                device_id=to, device_id_type=MESH)

        local = [pltpu.make_async_copy(src[k], dst[k].at[mine], lsem.at[k]) for k in range(n)]
        sends = [copy(src[k].at[c], dst[k].at[mine, c], s1.at[j, k], r1.at[j, k], (ch[0], ch[1], c))
                 for j, ch, _ in chips for k in range(n)]
        lands = [copy(src[k].at[c], dst[k].at[slot, c], s1.at[j, k], r1.at[j, k], (ch[0], ch[1], c))
                 for j, ch, slot in chips for k in range(n)]
        passes = [copy(dst[k].at[slot, c], dst[k].at[slot, c], s2.at[j, k], r2.at[j, k], (x, y, 1 - c))
                  for j, ch, slot in chips for k in range(n)]
        gets = [copy(dst[k].at[slot, 1 - c], dst[k].at[slot, 1 - c], s2.at[j, k], r2.at[j, k], (x, y, 1 - c))
                for j, ch, slot in chips for k in range(n)]
        return local, sends, lands, passes, gets

    def start(self, src, dst, sems):
        lsem, s1, r1, s2, r2 = sems
        x, y, c = _place()
        mine = 2 * x + y
        for k in range(len(src)):
            pltpu.make_async_copy(src[k], dst[k].at[mine], lsem.at[k]).start()
        for j, ch in enumerate(_other_chips(x, y)):
            for k in range(len(src)):
                pltpu.make_async_remote_copy(
                    src_ref=src[k].at[c], dst_ref=dst[k].at[mine, c], send_sem=s1.at[j, k], recv_sem=r1.at[j, k],
                    device_id=(ch[0], ch[1], c), device_id_type=MESH).start()

    def wait(self, src, dst, sems):
        local, sends, lands, passes, gets = self._plan(src, dst, sems)
        for land, fwd in zip(lands, passes):
            land.wait_recv()
            fwd.start()
        for cp in gets:
            cp.wait_recv()
        for cp in sends + passes:
            cp.wait_send()
        for cp in local:
            cp.wait()


def _call(body, *, name, args, in_specs, out_specs, out_shape, grid=(), scratch_shapes=(), sem=None, comm=None,
          aliases=None, vmem_mb=48):
    in_specs, out_specs, out_shape = list(in_specs), list(out_specs), list(out_shape)
    scratch, args = list(scratch_shapes), list(args)
    n_in, n_out, n_scr = len(in_specs), len(out_specs), len(scratch)
    if comm is None:
        def kernel_body(*refs):
            body(*refs)
    else:
        c_in, c_out = len(comm.srcs), len(comm.out_shape)

        def kernel_body(*refs):
            o0 = n_in + c_in
            s0 = o0 + n_out + c_out
            cin, cout, sems = refs[n_in:o0], refs[o0 + n_out:s0], refs[s0 + n_scr:]
            main = refs[:n_in] + refs[o0:o0 + n_out] + refs[s0:s0 + n_scr]
            if grid:
                ids = [pl.program_id(a) for a in range(len(grid))]
                first = functools.reduce(lambda p, q: p & q, [i == 0 for i in ids])
                last = functools.reduce(lambda p, q: p & q, [i == g - 1 for i, g in zip(ids, grid)])

                @pl.when(first)
                def _():
                    comm.start(cin, cout, sems)

                body(*main)

                @pl.when(last)
                def _():
                    comm.wait(cin, cout, sems)
            else:
                comm.start(cin, cout, sems)
                body(*main)
                comm.wait(cin, cout, sems)

        hbm = pl.BlockSpec(memory_space=pl.ANY)
        in_specs += [hbm] * c_in
        out_specs += [hbm] * c_out
        out_shape += comm.out_shape
        scratch += comm.scratch
        args += comm.srcs
    params = dict(vmem_limit_bytes=vmem_mb << 20)
    if grid:
        params["dimension_semantics"] = sem
    outs = pl.pallas_call(
        kernel_body, name=name, grid=grid, in_specs=in_specs, out_specs=out_specs, out_shape=out_shape,
        scratch_shapes=scratch, input_output_aliases=aliases or {}, compiler_params=pltpu.CompilerParams(**params),
    )(*args)
    return list(outs[:n_out]), list(outs[n_out:])


def _comm_alone(comm, name):
    return _call(lambda: None, name=name, args=[], in_specs=[], out_specs=[], out_shape=[], comm=comm)[1]


def _ffn_fwd(x, nw, wg, wu, wd, name, comm=None):
    T, D = x.shape
    ns, fs, _ = wg.shape
    tm = min(TM_FFN, T)

    def body(x_ref, nw_ref, wg_ref, wu_ref, wd_ref, xo_ref, g_ref, u_ref, h_s, acc_s):
        j = pl.program_id(1)

        @pl.when(j == 0)
        def _():
            xv = x_ref[...]
            h_s[...] = (xv * _rms_r(xv) * nw_ref[...]).astype(BF)
            acc_s[...] = jnp.zeros_like(acc_s)

        h = h_s[...]
        gb = lax.dot_general(h, wg_ref[...], NT_DIMS, preferred_element_type=F32).astype(BF)
        ub = lax.dot_general(h, wu_ref[...], NT_DIMS, preferred_element_type=F32).astype(BF)
        g_ref[...] = gb
        u_ref[...] = ub
        g = gb.astype(F32)
        a = (g * _sigmoid(g) * ub.astype(F32)).astype(BF)
        acc_s[...] += jnp.dot(a, wd_ref[...], preferred_element_type=F32)

        @pl.when(j == ns - 1)
        def _():
            xo_ref[...] = x_ref[...] + 0.5 * acc_s[...]

    wspec = pl.BlockSpec((None, fs, D), lambda i, j: (j, 0, 0))
    return _call(
        body, name=name, grid=(T // tm, ns), args=(x, nw, wg, wu, wd), comm=comm, vmem_mb=56,
        in_specs=[pl.BlockSpec((tm, D), lambda i, j: (i, 0)),
                  pl.BlockSpec((1, D), lambda i, j: (0, 0)),
                  wspec, wspec,
                  pl.BlockSpec((None, fs, D), lambda i, j: (j, 0, 0))],
        out_specs=[pl.BlockSpec((tm, D), lambda i, j: (i, 0)),
                   pl.BlockSpec((None, tm, fs), lambda i, j: (j, i, 0)),
                   pl.BlockSpec((None, tm, fs), lambda i, j: (j, i, 0))],
        out_shape=[_sds((T, D), F32), _sds((ns, T, fs), BF), _sds((ns, T, fs), BF)],
        scratch_shapes=[pltpu.VMEM((tm, D), BF), pltpu.VMEM((tm, D), F32)],
        sem=(ARB, ARB))


def _ffn_bwd_hidden(dxo, x, nw, g, u, wd, name, comm=None):
    T, D = x.shape
    ns, fs, _ = wd.shape
    tm = min(TM_FFN, T)

    def body(dxo_ref, x_ref, nw_ref, g_ref, u_ref, wd_ref, dg_ref, du_ref, a_ref, h_ref, dacc_ref, dacc_s):
        @pl.when(pl.program_id(1) == 0)
        def _():
            xv = x_ref[...]
            h_ref[...] = (xv * _rms_r(xv) * nw_ref[...]).astype(BF)
            db = (0.5 * dxo_ref[...]).astype(BF)
            dacc_ref[...] = db
            dacc_s[...] = db

        da = lax.dot_general(dacc_s[...], wd_ref[...], NT_DIMS, preferred_element_type=F32)
        gv = g_ref[...].astype(F32)
        uv = u_ref[...].astype(F32)
        s = _sigmoid(gv)
        sg = gv * s
        a_ref[...] = (sg * uv).astype(BF)
        du_ref[...] = (da * sg).astype(BF)
        dg_ref[...] = (da * uv * (s * (1.0 + gv * (1.0 - s)))).astype(BF)

    tok = pl.BlockSpec((tm, D), lambda i, j: (i, 0))
    hid = pl.BlockSpec((None, tm, fs), lambda i, j: (j, i, 0))
    return _call(
        body, name=name, grid=(T // tm, ns), args=(dxo, x, nw, g, u, wd), comm=comm, vmem_mb=56,
        in_specs=[tok, tok, pl.BlockSpec((1, D), lambda i, j: (0, 0)), hid, hid,
                  pl.BlockSpec((None, fs, D), lambda i, j: (j, 0, 0))],
        out_specs=[hid, hid, hid, tok, tok],
        out_shape=[_sds((ns, T, fs), BF)] * 3 + [_sds((T, D), BF)] * 2,
        scratch_shapes=[pltpu.VMEM((tm, D), BF)],
        sem=(ARB, ARB))


def _ffn_bwd_resid(dg, du, wg, wu, x, nw, dxo, name, comm=None):
    T, D = x.shape
    ns, fs, _ = wg.shape
    tm = min(TM_FFN, T)

    def body(dg_ref, du_ref, wg_ref, wu_ref, x_ref, nw_ref, dxo_ref, dx_ref, dnw_ref, acc_s):
        i = pl.program_id(0)
        j = pl.program_id(1)
        prod = (jnp.dot(dg_ref[...], wg_ref[...], preferred_element_type=F32)
                + jnp.dot(du_ref[...], wu_ref[...], preferred_element_type=F32))

        @pl.when((i == 0) & (j == 0))
        def _():
            dnw_ref[...] = jnp.zeros_like(dnw_ref)

        @pl.when(j == 0)
        def _():
            acc_s[...] = prod

        @pl.when(j > 0)
        def _():
            acc_s[...] += prod

        @pl.when(j == ns - 1)
        def _():
            dx, dn = _rms_bwd(acc_s[...], x_ref[...], nw_ref[...])
            dx_ref[...] = dxo_ref[...] + dx
            dnw_ref[...] += dn

    tok = pl.BlockSpec((tm, D), lambda i, j: (i, 0))
    row = pl.BlockSpec((1, D), lambda i, j: (0, 0))
    hid = pl.BlockSpec((None, tm, fs), lambda i, j: (j, i, 0))
    wspec = pl.BlockSpec((None, fs, D), lambda i, j: (j, 0, 0))
    return _call(
        body, name=name, grid=(T // tm, ns), args=(dg, du, wg, wu, x, nw, dxo), comm=comm, vmem_mb=56,
        in_specs=[hid, hid, wspec, wspec, tok, row, tok],
        out_specs=[tok, row],
        out_shape=[_sds((T, D), F32), _sds((1, D), F32)],
        scratch_shapes=[pltpu.VMEM((tm, D), F32)],
        sem=(ARB, ARB))


def _tn(a, b, a_spec, b_spec, out_shape, out_spec, grid, name, prev=None, comm=None):
    nk = grid[-1]
    acc_shape = tuple(d for d in out_spec.block_shape if d is not None)

    def body(*refs):
        a_ref, b_ref = refs[0], refs[1]
        o_ref, acc = refs[-2], refs[-1]
        k = pl.program_id(2)
        prod = lax.dot_general(a_ref[...], b_ref[...], TN_DIMS, preferred_element_type=F32)

        @pl.when(k == 0)
        def _():
            acc[...] = prod

        @pl.when(k > 0)
        def _():
            acc[...] += prod

        @pl.when(k == nk - 1)
        def _():
            o_ref[...] = acc[...].astype(o_ref.dtype)

    in_specs = [a_spec, b_spec]
    args = [a, b]
    aliases = {}
    if prev is not None:
        in_specs.append(pl.BlockSpec(memory_space=pl.ANY))
        args.append(prev)
        aliases = {2: 0}
    main, extra = _call(
        body, name=name, grid=grid, args=args, in_specs=in_specs, out_specs=[out_spec], out_shape=[out_shape],
        scratch_shapes=[pltpu.VMEM(acc_shape, F32)], aliases=aliases, sem=(ARB, ARB, ARB), comm=comm)
    return main[0] if comm is None else (main[0], extra)


def _tn_gates(h, dbig, ns, tk, name):
    T, D = h.shape
    dq = D // ns
    nk = T // tk

    def body(a_ref, b_ref, o_ref, acc):
        k = pl.program_id(1)
        prod = lax.dot_general(a_ref[...], b_ref[...], TN_DIMS, preferred_element_type=F32)

        @pl.when(k == 0)
        def _():
            acc[...] = prod

        @pl.when(k > 0)
        def _():
            acc[...] += prod

        @pl.when(k == nk - 1)
        def _():
            for s in range(ns):
                o_ref[s] = acc[s * dq:(s + 1) * dq, :].astype(o_ref.dtype)

    return pl.pallas_call(
        body, name=name, grid=(3, nk),
        in_specs=[pl.BlockSpec((tk, D), lambda q, k: (k, 0)), pl.BlockSpec((tk, D), lambda q, k: (k, q))],
        out_specs=pl.BlockSpec((ns, None, dq, D), lambda q, k: (0, q, 0, 0)),
        out_shape=_sds((ns, 3, dq, D), BF),
        scratch_shapes=[pltpu.VMEM((D, D), F32)],
        compiler_params=_cp((PAR, ARB)),
    )(h, dbig)


def _inproj_fwd(x, nw, wbig, name):
    T, D = x.shape
    nb = wbig.shape[-1]
    tm = min(2 * TM, T)
    bn = min(2048, nb)

    def body(x_ref, nw_ref, w_ref, o_ref, h_ref, h_s):
        @pl.when(pl.program_id(1) == 0)
        def _():
            xv = x_ref[...]
            hb = (xv * _rms_r(xv) * nw_ref[...]).astype(BF)
            h_s[...] = hb
            h_ref[...] = hb

        o_ref[...] = jnp.dot(h_s[...], w_ref[...], preferred_element_type=F32).astype(BF)

    return pl.pallas_call(
        body, name=name, grid=(T // tm, nb // bn),
        in_specs=[pl.BlockSpec((tm, D), lambda i, n: (i, 0)),
                  pl.BlockSpec((1, D), lambda i, n: (0, 0)),
                  pl.BlockSpec((D, bn), lambda i, n: (0, n))],
        out_specs=[pl.BlockSpec((tm, bn), lambda i, n: (i, n)),
                   pl.BlockSpec((tm, D), lambda i, n: (i, 0))],
        out_shape=[_sds((T, nb), BF), _sds((T, D), BF)],
        scratch_shapes=[pltpu.VMEM((tm, D), BF)],
        compiler_params=_cp((PAR, ARB)),
    )(x, nw, wbig)


def _inproj_bwd(dbig, wbig, x, nw, dxin, name, comm=None):
    T, D = x.shape
    nb = wbig.shape[-1]
    tm = min(TM_FFN, T)
    tk = min(2048, nb)
    nk = nb // tk

    def body(a_ref, w_ref, x_ref, nw_ref, dxin_ref, dx_ref, dnw_ref, acc_s):
        i = pl.program_id(0)
        k = pl.program_id(1)
        prod = lax.dot_general(a_ref[...], w_ref[...], NT_DIMS, preferred_element_type=F32)

        @pl.when((i == 0) & (k == 0))
        def _():
            dnw_ref[...] = jnp.zeros_like(dnw_ref)

        @pl.when(k == 0)
        def _():
            acc_s[...] = prod

        @pl.when(k > 0)
        def _():
            acc_s[...] += prod

        @pl.when(k == nk - 1)
        def _():
            dx, dn = _rms_bwd(acc_s[...], x_ref[...], nw_ref[...])
            dx_ref[...] = dxin_ref[...] + dx
            dnw_ref[...] += dn

    tok = pl.BlockSpec((tm, D), lambda i, k: (i, 0))
    row = pl.BlockSpec((1, D), lambda i, k: (0, 0))
    return _call(
        body, name=name, grid=(T // tm, nk), args=(dbig, wbig, x, nw, dxin), comm=comm, vmem_mb=56,
        in_specs=[pl.BlockSpec((tm, tk), lambda i, k: (i, k)),
                  pl.BlockSpec((D, tk), lambda i, k: (0, k)),
                  tok, row, tok],
        out_specs=[tok, row],
        out_shape=[_sds((T, D), F32), _sds((1, D), F32)],
        scratch_shapes=[pltpu.VMEM((tm, D), F32)],
        sem=(ARB, ARB))


CONV_R = 512
CONV_BASE, CONV_GROUP = 0, 3
ATT_BASE, ATT_GROUP = 12, 3
RET_BASE, RET_GROUP = 24, 4
N_SEG = 10


def _permute_in_cols(w):
    lead = w.shape[:-1]
    w4 = w.reshape(lead + (N_SEG, BRANCH_W // LANE, LANE))

    def grouped(lo, hi):
        return jnp.swapaxes(w4[..., lo:hi, :, :], -3, -2).reshape(lead + (-1,))

    return jnp.concatenate([grouped(0, 3), grouped(7, 10), grouped(3, 7)], axis=-1)


def _unpermute_in_cols(w):
    lead = w.shape[:-1]
    nblk = BRANCH_W // LANE

    def segs(lo, n):
        part = w[..., lo * LANE:(lo + nblk * n) * LANE].reshape(lead + (nblk, n, LANE))
        return jnp.swapaxes(part, -3, -2)

    conv, att, ret = segs(CONV_BASE, 3), segs(ATT_BASE, 3), segs(RET_BASE, 4)
    return jnp.concatenate([conv, ret, att], axis=-3).reshape(lead + (-1,))


def _seg0(big):
    return (big.shape[1] - N_SEG * BRANCH_W) // LANE


def _group_spec(big, base, group, rows, where):
    first = (_seg0(big) + base) // group
    assert first * group == _seg0(big) + base

    def index(*ids):
        r, g = where(*ids)
        return r, first + g

    return pl.BlockSpec((rows, group * LANE), index)


CU, CB, CC = (slice(k * LANE, (k + 1) * LANE) for k in range(3))
AQ, AK, AV = CU, CB, CC
RQ, RK, RV, RG = (slice(k * LANE, (k + 1) * LANE) for k in range(4))


def _conv_fwd(big, cw, name):
    T = big.shape[0]
    R = min(CONV_R, T)

    def body(g_ref, w_ref, y_ref, z_s):
        z_s[pl.ds(0, 8), :] = jnp.zeros((8, LANE), F32)

        def fill(t, c):
            sl = pl.ds(pl.multiple_of(t * R, R), R)
            z_s[pl.ds(pl.multiple_of(t * R + 8, 8), R), :] = g_ref[sl, CC].astype(F32) * g_ref[sl, CU].astype(F32)
            return c

        lax.fori_loop(0, T // R, fill, 0)
        w0, w1, w2 = w_ref[0:1, :], w_ref[1:2, :], w_ref[2:3, :]

        def step(t, c):
            zz = z_s[pl.ds(pl.multiple_of(t * R, R), R + 8), :]
            z0 = zz[8:]
            z1 = pltpu.roll(zz, 1, 0)[8:]
            z2 = pltpu.roll(zz, 2, 0)[8:]
            sl = pl.ds(pl.multiple_of(t * R, R), R)
            y_ref[sl, :] = (g_ref[sl, CB].astype(F32) * (w2 * z0 + w1 * z1 + w0 * z2)).astype(BF)
            return c

        lax.fori_loop(0, T // R, step, 0)

    return pl.pallas_call(
        body, name=name, grid=(BRANCH_W // LANE,),
        in_specs=[_group_spec(big, CONV_BASE, CONV_GROUP, T, lambda j: (0, j)),
                  pl.BlockSpec((3, LANE), lambda j: (0, j))],
        out_specs=pl.BlockSpec((T, LANE), lambda j: (0, j)),
        out_shape=_sds((T, BRANCH_W), BF),
        scratch_shapes=[pltpu.VMEM((T + 8, LANE), F32)],
        compiler_params=_cp((PAR,)),
    )(big, cw)


def _conv_bwd(big, dy, cw, dbig, name):
    T = big.shape[0]
    R = min(CONV_R, T)

    def body(g_ref, dy_ref, w_ref, _, o_ref, dw_ref, z_s, d_s):
        z_s[pl.ds(0, 8), :] = jnp.zeros((8, LANE), F32)
        d_s[pl.ds(T, 8), :] = jnp.zeros((8, LANE), F32)

        def fill(t, c):
            sl = pl.ds(pl.multiple_of(t * R, R), R)
            z_s[pl.ds(pl.multiple_of(t * R + 8, 8), R), :] = g_ref[sl, CC].astype(F32) * g_ref[sl, CU].astype(F32)
            d_s[sl, :] = dy_ref[sl, :].astype(F32) * g_ref[sl, CB].astype(F32)
            return c

        lax.fori_loop(0, T // R, fill, 0)
        w0, w1, w2 = w_ref[0:1, :], w_ref[1:2, :], w_ref[2:3, :]

        def step(t, carry):
            a0, a1, a2 = carry
            zz = z_s[pl.ds(pl.multiple_of(t * R, R), R + 8), :]
            z0 = zz[8:]
            z1 = pltpu.roll(zz, 1, 0)[8:]
            z2 = pltpu.roll(zz, 2, 0)[8:]
            sl = pl.ds(pl.multiple_of(t * R, R), R)
            dyv = dy_ref[sl, :].astype(F32)
            o_ref[sl, CB] = (dyv * (w2 * z0 + w1 * z1 + w0 * z2)).astype(BF)
            dd = d_s[pl.ds(pl.multiple_of(t * R, R), R + 8), :]
            d0 = dd[:R]
            d1 = pltpu.roll(dd, R + 7, 0)[:R]
            d2 = pltpu.roll(dd, R + 6, 0)[:R]
            dz = w2 * d0 + w1 * d1 + w0 * d2
            o_ref[sl, CC] = (dz * g_ref[sl, CU].astype(F32)).astype(BF)
            o_ref[sl, CU] = (dz * g_ref[sl, CC].astype(F32)).astype(BF)
            a0 = a0 + jnp.sum(d0 * z2, axis=0, keepdims=True)
            a1 = a1 + jnp.sum(d0 * z1, axis=0, keepdims=True)
            a2 = a2 + jnp.sum(d0 * z0, axis=0, keepdims=True)
            return a0, a1, a2

        zero = jnp.zeros((1, LANE), F32)
        a0, a1, a2 = lax.fori_loop(0, T // R, step, (zero, zero, zero))
        dw_ref[0:1, :] = a0
        dw_ref[1:2, :] = a1
        dw_ref[2:3, :] = a2

    group = _group_spec(big, CONV_BASE, CONV_GROUP, T, lambda j: (0, j))
    w = pl.BlockSpec((3, LANE), lambda j: (0, j))
    return pl.pallas_call(
        body, name=name, grid=(BRANCH_W // LANE,),
        in_specs=[group, pl.BlockSpec((T, LANE), lambda j: (0, j)), w, pl.BlockSpec(memory_space=pl.ANY)],
        out_specs=[group, w],
        out_shape=[_sds(dbig.shape, BF), _sds((3, BRANCH_W), F32)],
        scratch_shapes=[pltpu.VMEM((T + 8, LANE), F32), pltpu.VMEM((T + 8, LANE), F32)],
        input_output_aliases={3: 0}, compiler_params=_cp((PAR,)),
    )(big, dy, cw, dbig)


def _ret_tables(T):
    L = min(RET_L, T)
    hh = jnp.arange(H_RET, dtype=F32)
    lg = jnp.log1p(-jnp.exp2(-5.0 - hh))
    n = jnp.arange(L, dtype=F32)
    a = jnp.exp(lg[:, None] * (n + 1.0))
    b = jnp.exp(lg[:, None] * (L - 1.0 - n))
    gl = jnp.exp(lg * L)
    ch = jnp.arange(L) // CHUNK
    m = jnp.exp(lg[:, None, None] * jnp.abs(n[:, None] - n[None, :])) * (ch[None, :] <= ch[:, None]).astype(F32)
    inv_freq = ROPE_BASE ** (-jnp.linspace(0.0, 1.0, DK_RET // 2, dtype=F32))
    ang = jnp.arange(T, dtype=F32)[:, None] * inv_freq[None, :]
    cos, sin = jnp.cos(ang), jnp.sin(ang)
    return dict(
        L=L, M=m,
        a=jnp.broadcast_to(a[:, :, None], (H_RET, L, DK_RET)),
        b=jnp.broadcast_to(b[:, :, None], (H_RET, L, DK_RET)),
        gl=jnp.broadcast_to(gl[:, None, None], (H_RET, 1, DK_RET)),
        cos=jnp.concatenate([cos, cos], axis=-1), sin=jnp.concatenate([-sin, sin], axis=-1))


def _rot(x, cs, sn):
    return x * cs + pltpu.roll(x, DK_RET // 2, 1) * sn


def _unrot(dy, cs, sn):
    return dy * cs + pltpu.roll(dy * sn, DK_RET // 2, 1)


def _ret_fwd(big, tb, name, comm=None):
    T = big.shape[0]
    L = tb["L"]
    nsc = T // L
    scale = DK_RET ** -0.5

    def body(x_ref, cos_ref, sin_ref, m_ref, a_ref, b_ref, gl_ref, y_ref, o_ref, st_ref, s_s):
        @pl.when(pl.program_id(1) == 0)
        def _():
            s_s[...] = jnp.zeros_like(s_s)

        cs, sn = cos_ref[...], sin_ref[...]
        qt = _rot(x_ref[:, RQ].astype(F32), cs, sn) * scale
        kt = _rot(x_ref[:, RK].astype(F32), cs, sn)
        qb, kb, vb = qt.astype(BF), kt.astype(BF), x_ref[:, RV]
        s_prev = s_s[...]
        st_ref[...] = s_prev
        p = lax.dot_general(qb, kb, NT_DIMS, preferred_element_type=F32) * m_ref[...]
        o = (jnp.dot(p.astype(BF), vb, preferred_element_type=F32)
             + jnp.dot((qt * a_ref[...]).astype(BF), s_prev.astype(BF), preferred_element_type=F32))
        s_s[...] = s_prev * gl_ref[...] + lax.dot_general((kt * b_ref[...]).astype(BF), vb, TN_DIMS,
                                                         preferred_element_type=F32)
        o_ref[...] = o
        gv = x_ref[:, RG].astype(F32)
        y_ref[...] = (gv * _sigmoid(gv) * o * _rms_r(o)).astype(BF)

    tab = pl.BlockSpec((L, DK_RET), lambda h, i: (i, 0))
    per_head = pl.BlockSpec((None, L, DK_RET), lambda h, i: (h, 0, 0))
    out = pl.BlockSpec((L, LANE), lambda h, i: (i, h))
    return _call(
        body, name=name, grid=(H_RET, nsc), comm=comm,
        args=(big, tb["cos"], tb["sin"], tb["M"], tb["a"], tb["b"], tb["gl"]),
        in_specs=[_group_spec(big, RET_BASE, RET_GROUP, L, lambda h, i: (i, h)), tab, tab,
                  pl.BlockSpec((None, L, L), lambda h, i: (h, 0, 0)), per_head, per_head,
                  pl.BlockSpec((None, 1, DK_RET), lambda h, i: (h, 0, 0))],
        out_specs=[out, out, pl.BlockSpec((None, None, DK_RET, DK_RET), lambda h, i: (i, h, 0, 0))],
        out_shape=[_sds((T, BRANCH_W), BF), _sds((T, BRANCH_W), F32), _sds((nsc, H_RET, DK_RET, DK_RET), F32)],
        scratch_shapes=[pltpu.VMEM((DK_RET, DK_RET), F32)],
        sem=(ARB, ARB))


def _ret_bwd(big, o, st, dy, tb, dbig, name):
    T = big.shape[0]
    L = tb["L"]
    nsc = T // L
    scale = DK_RET ** -0.5

    def body(x_ref, cos_ref, sin_ref, m_ref, a_ref, b_ref, gl_ref, o_ref, st_ref, dy_ref, _, d_ref, ds_s):
        @pl.when(pl.program_id(1) == 0)
        def _():
            ds_s[...] = jnp.zeros_like(ds_s)

        cs, sn = cos_ref[...], sin_ref[...]
        mm, av, bv = m_ref[...], a_ref[...], b_ref[...]
        qt = _rot(x_ref[:, RQ].astype(F32), cs, sn) * scale
        kt = _rot(x_ref[:, RK].astype(F32), cs, sn)
        qb, kb, vb = qt.astype(BF), kt.astype(BF), x_ref[:, RV]
        pb = (lax.dot_general(qb, kb, NT_DIMS, preferred_element_type=F32) * mm).astype(BF)
        ov = o_ref[...]
        r = _rms_r(ov)
        oh = ov * r
        gv = x_ref[:, RG].astype(F32)
        sg = _sigmoid(gv)
        dyv = dy_ref[...].astype(F32)
        d_ref[:, RG] = (dyv * oh * (sg * (1.0 + gv * (1.0 - sg)))).astype(BF)
        doh = dyv * gv * sg
        dob = (r * (doh - oh * jnp.mean(doh * oh, axis=-1, keepdims=True))).astype(BF)
        dsb = ds_s[...].astype(BF)
        spb = st_ref[...].astype(BF)
        dpb = (lax.dot_general(dob, vb, NT_DIMS, preferred_element_type=F32) * mm).astype(BF)
        dqt = (jnp.dot(dpb, kb, preferred_element_type=F32)
               + lax.dot_general(dob, spb, NT_DIMS, preferred_element_type=F32) * av)
        dkt = (lax.dot_general(dpb, qb, TN_DIMS, preferred_element_type=F32)
               + lax.dot_general(vb, dsb, NT_DIMS, preferred_element_type=F32) * bv)
        dv = (lax.dot_general(pb, dob, TN_DIMS, preferred_element_type=F32)
              + jnp.dot((kt * bv).astype(BF), dsb, preferred_element_type=F32))
        ds_s[...] = ds_s[...] * gl_ref[...] + lax.dot_general((qt * av).astype(BF), dob, TN_DIMS,
                                                              preferred_element_type=F32)
        d_ref[:, RQ] = (_unrot(dqt, cs, sn) * scale).astype(BF)
        d_ref[:, RK] = _unrot(dkt, cs, sn).astype(BF)
        d_ref[:, RV] = dv.astype(BF)

    def rev(i):
        return nsc - 1 - i

    group = _group_spec(big, RET_BASE, RET_GROUP, L, lambda h, i: (rev(i), h))
    tab = pl.BlockSpec((L, DK_RET), lambda h, i: (rev(i), 0))
    per_head = pl.BlockSpec((None, L, DK_RET), lambda h, i: (h, 0, 0))
    out = pl.BlockSpec((L, LANE), lambda h, i: (rev(i), h))
    return pl.pallas_call(
        body, name=name, grid=(H_RET, nsc),
        in_specs=[group, tab, tab,
                  pl.BlockSpec((None, L, L), lambda h, i: (h, 0, 0)), per_head, per_head,
                  pl.BlockSpec((None, 1, DK_RET), lambda h, i: (h, 0, 0)),
                  out, pl.BlockSpec((None, None, DK_RET, DK_RET), lambda h, i: (rev(i), h, 0, 0)), out,
                  pl.BlockSpec(memory_space=pl.ANY)],
        out_specs=group,
        out_shape=_sds(dbig.shape, BF),
        scratch_shapes=[pltpu.VMEM((DK_RET, DK_RET), F32)],
        input_output_aliases={10: 0}, compiler_params=_cp((PAR, ARB)),
    )(big, tb["cos"], tb["sin"], tb["M"], tb["a"], tb["b"], tb["gl"], o, st, dy, dbig)


def _relbias_onehot(n):
    mm = lax.broadcasted_iota(jnp.int32, (RB_PAD, ATT_TOEP), 1)
    rr = lax.broadcasted_iota(jnp.int32, (RB_PAD, ATT_TOEP), 0)
    idx = jnp.clip(n + ATT_TOEP - mm, 0, 2 * REL_CLIP)
    return (rr == idx).astype(F32)


def _split3(x):
    hi = x.astype(BF).astype(F32)
    mid = (x - hi).astype(BF).astype(F32)
    lo = x - hi - mid
    return jnp.concatenate([hi, mid, lo], axis=0).astype(BF)


def _join3(y):
    k = y.shape[0] // 3
    return (y[:k] + y[k:2 * k]) + y[2 * k:]


def _relbias_expand(rbp, name):
    far = ATT_SPAN - ATT_TOEP

    def body(rb_ref, o_ref):
        rb = rb_ref[...]
        const = jnp.broadcast_to(rb[:, 2 * REL_CLIP:2 * REL_CLIP + 1], (H_ATT, far))

        rb3 = _split3(rb)

        def row(n, c):
            toep = _join3(jnp.dot(rb3, _relbias_onehot(n).astype(BF), preferred_element_type=F32))
            m = lax.broadcasted_iota(jnp.int32, (1, ATT_SPAN), 1)
            d = n // CHUNK + N_PREV - m // CHUNK
            neg = jnp.where((d >= 0) & (d <= N_PREV), 0.0, NEG_INF).astype(F32)
            o_ref[n] = jnp.concatenate([const, toep], axis=1) + neg
            return c

        lax.fori_loop(0, ATT_TQ, row, 0)

    return pl.pallas_call(
        body, name=name,
        in_specs=[pl.BlockSpec(memory_space=pltpu.VMEM)],
        out_specs=pl.BlockSpec(memory_space=pltpu.VMEM),
        out_shape=_sds((ATT_TQ, H_ATT, ATT_SPAN), F32),
    )(rbp)


def _relbias_grad(dbt, name):
    far = ATT_SPAN - ATT_TOEP

    def body(d_ref, o_ref):
        def row(n, carry):
            acc, cs = carry
            dn = d_ref[n]
            acc = acc + _join3(lax.dot_general(_split3(dn[:, far:]), _relbias_onehot(n).astype(BF), NT_DIMS,
                                               preferred_element_type=F32))
            cs = cs + jnp.sum(dn[:, :far], axis=1, keepdims=True)
            return acc, cs

        acc, cs = lax.fori_loop(0, ATT_TQ, row, (jnp.zeros((H_ATT, RB_PAD), F32), jnp.zeros((H_ATT, 1), F32)))
        rr = lax.broadcasted_iota(jnp.int32, (H_ATT, RB_PAD), 1)
        o_ref[...] = acc + jnp.where(rr == 2 * REL_CLIP, cs, 0.0)

    return pl.pallas_call(
        body, name=name,
        in_specs=[pl.BlockSpec(memory_space=pltpu.VMEM)],
        out_specs=pl.BlockSpec(memory_space=pltpu.VMEM),
        out_shape=_sds((H_ATT, RB_PAD), F32),
    )(dbt)


def _att_pad_fill(dst_s, src_ref, cols, T):
    dst_s[pl.ds(0, ATT_PAD), :] = jnp.zeros((ATT_PAD, LANE), dst_s.dtype)
    R = min(512, T)

    def cp(t, c):
        dst_s[pl.ds(pl.multiple_of(ATT_PAD + t * R, LANE), R), :] = src_ref[pl.ds(pl.multiple_of(t * R, R), R), cols]
        return c

    lax.fori_loop(0, T // R, cp, 0)


ATT_WIN = ATT_SUB * ATT_TQ + ATT_PAD


def _att_probs(s_full, sub, bias, t0):
    s = s_full[sub * ATT_TQ:(sub + 1) * ATT_TQ, sub * ATT_TQ:sub * ATT_TQ + ATT_SPAN] * (DH_ATT ** -0.5) + bias
    key_pos = t0 + sub * ATT_TQ - ATT_PAD + lax.broadcasted_iota(jnp.int32, (1, ATT_SPAN), 1)
    s = jnp.where(key_pos >= 0, s, NEG_INF)
    p = jnp.exp(s - jnp.max(s, axis=-1, keepdims=True))
    return p * (1.0 / jnp.sum(p, axis=-1, keepdims=True))


def _att_band(tiles):
    rows = []
    for sub, t in enumerate(tiles):
        parts = []
        if sub:
            parts.append(jnp.zeros((ATT_TQ, sub * ATT_TQ), BF))
        parts.append(t)
        if sub < ATT_SUB - 1:
            parts.append(jnp.zeros((ATT_TQ, (ATT_SUB - 1 - sub) * ATT_TQ), BF))
        rows.append(jnp.concatenate(parts, axis=1))
    return jnp.concatenate(rows, axis=0)


def _att_head_masks(x):
    first = lax.broadcasted_iota(jnp.int32, (1, LANE), 1) < DH_ATT
    zero = jnp.zeros_like(x)
    return first, (jnp.where(first, x, zero), jnp.where(first, zero, x))


def _att_fwd(big, bias, name, comm=None):
    T = big.shape[0]
    rows = ATT_SUB * ATT_TQ
    nt = T // rows

    def body(x_ref, b_ref, y_ref, kp_s, vp_s):
        i = pl.program_id(1)

        @pl.when(i == 0)
        def _():
            _att_pad_fill(kp_s, x_ref, AK, T)
            _att_pad_fill(vp_s, x_ref, AV, T)

        t0 = pl.multiple_of(i * rows, rows)
        kw = kp_s[pl.ds(t0, ATT_WIN), :]
        vw = vp_s[pl.ds(t0, ATT_WIN), :]
        first, qm = _att_head_masks(x_ref[pl.ds(t0, rows), AQ])
        outs = []
        for hh in range(2):
            s_full = lax.dot_general(qm[hh], kw, NT_DIMS, preferred_element_type=F32)
            band = _att_band([_att_probs(s_full, sub, b_ref[hh], t0).astype(BF) for sub in range(ATT_SUB)])
            outs.append(jnp.dot(band, vw, preferred_element_type=F32))
        y_ref[...] = jnp.where(first, outs[0], outs[1]).astype(BF)

    return _call(
        body, name=name, grid=(H_ATT // 2, nt), args=(big, bias), comm=comm,
        in_specs=[_group_spec(big, ATT_BASE, ATT_GROUP, T, lambda p, i: (0, p)),
                  pl.BlockSpec((2, ATT_TQ, ATT_SPAN), lambda p, i: (p, 0, 0))],
        out_specs=[pl.BlockSpec((rows, LANE), lambda p, i: (i, p))],
        out_shape=[_sds((T, BRANCH_W), BF)],
        scratch_shapes=[pltpu.VMEM((T + ATT_PAD, LANE), BF), pltpu.VMEM((T + ATT_PAD, LANE), BF)],
        sem=(ARB, ARB))


def _att_bwd(big, bias, dy, dbig, name, comm=None):
    T = big.shape[0]
    rows = ATT_SUB * ATT_TQ
    nt = T // rows
    scale = DH_ATT ** -0.5

    def body(x_ref, b_ref, dy_ref, _, d_ref, db_ref, kp_s, vp_s, dk_s, dv_s):
        i = pl.program_id(1)

        @pl.when(i == 0)
        def _():
            _att_pad_fill(kp_s, x_ref, AK, T)
            _att_pad_fill(vp_s, x_ref, AV, T)
            dk_s[...] = jnp.zeros_like(dk_s)
            dv_s[...] = jnp.zeros_like(dv_s)
            db_ref[...] = jnp.zeros_like(db_ref)

        t0 = pl.multiple_of(i * rows, rows)
        win = pl.ds(t0, ATT_WIN)
        kw = kp_s[win, :]
        vw = vp_s[win, :]
        first, qm = _att_head_masks(x_ref[pl.ds(t0, rows), AQ])
        _, dom = _att_head_masks(dy_ref[...])
        dqs, dkt, dvt = [], None, None
        for hh in range(2):
            s_full = lax.dot_general(qm[hh], kw, NT_DIMS, preferred_element_type=F32)
            dp_full = lax.dot_general(dom[hh], vw, NT_DIMS, preferred_element_type=F32)
            ps, dss, db = [], [], None
            for sub in range(ATT_SUB):
                pn = _att_probs(s_full, sub, b_ref[hh], t0)
                dp = dp_full[sub * ATT_TQ:(sub + 1) * ATT_TQ, sub * ATT_TQ:sub * ATT_TQ + ATT_SPAN]
                ds = pn * (dp - jnp.sum(dp * pn, axis=-1, keepdims=True))
                db = ds if db is None else db + ds
                ps.append(pn.astype(BF))
                dss.append(ds.astype(BF))
            db_ref[hh] += db
            ds_band, p_band = _att_band(dss), _att_band(ps)
            dqs.append(jnp.dot(ds_band, kw, preferred_element_type=F32))
            qt = jnp.transpose(qm[hh].astype(F32)).astype(BF)
            dot_ = jnp.transpose(dom[hh].astype(F32)).astype(BF)
            dk_h = jnp.dot(qt, ds_band, preferred_element_type=F32)
            dv_h = jnp.dot(dot_, p_band, preferred_element_type=F32)
            dkt = dk_h if dkt is None else dkt + dk_h
            dvt = dv_h if dvt is None else dvt + dv_h
        d_ref[pl.ds(t0, rows), AQ] = (jnp.where(first, dqs[0], dqs[1]) * scale).astype(BF)
        dk_s[win, :] += jnp.transpose(dkt) * scale
        dv_s[win, :] += jnp.transpose(dvt)

        @pl.when(i == nt - 1)
        def _():
            R = min(512, T)

            def cp(t, c):
                src = pl.ds(pl.multiple_of(ATT_PAD + t * R, LANE), R)
                dst = pl.ds(pl.multiple_of(t * R, R), R)
                d_ref[dst, AK] = dk_s[src, :].astype(BF)
                d_ref[dst, AV] = dv_s[src, :].astype(BF)
                return c

            lax.fori_loop(0, T // R, cp, 0)

    group = _group_spec(big, ATT_BASE, ATT_GROUP, T, lambda p, i: (0, p))
    tile = pl.BlockSpec((rows, LANE), lambda p, i: (i, p))
    bspec = pl.BlockSpec((2, ATT_TQ, ATT_SPAN), lambda p, i: (p, 0, 0))
    return _call(
        body, name=name, grid=(H_ATT // 2, nt), args=(big, bias, dy, dbig), comm=comm, aliases={3: 0}, vmem_mb=56,
        in_specs=[group, bspec, tile, pl.BlockSpec(memory_space=pl.ANY)],
        out_specs=[group, bspec],
        out_shape=[_sds(dbig.shape, BF), _sds((H_ATT, ATT_TQ, ATT_SPAN), F32)],
        scratch_shapes=[pltpu.VMEM((T + ATT_PAD, LANE), BF), pltpu.VMEM((T + ATT_PAD, LANE), BF),
                        pltpu.VMEM((T + ATT_PAD, LANE), F32), pltpu.VMEM((T + ATT_PAD, LANE), F32)],
        sem=(ARB, ARB))


def _merge_fwd(x1, big, ys, wb, wo, name):
    T, D = x1.shape
    tm = min(TM, T)

    def body(x_ref, gp_ref, yc_ref, yr_ref, ya_ref, wb_ref, wo_ref, x2_ref, p_ref, mg_ref):
        merged = jnp.zeros((tm, D), F32)
        for i, y_ref in enumerate((yc_ref, yr_ref, ya_ref)):
            cols = slice(i * D, (i + 1) * D)
            pb = jnp.dot(y_ref[...], wb_ref[i], preferred_element_type=F32).astype(BF)
            p_ref[:, cols] = pb
            merged = merged + _sigmoid(gp_ref[:, cols].astype(F32)) * pb.astype(F32)
        mb = merged.astype(BF)
        mg_ref[...] = mb
        x2_ref[...] = x_ref[...] + jnp.dot(mb, wo_ref[...], preferred_element_type=F32)

    tok = pl.BlockSpec((tm, D), lambda i: (i, 0))
    wide = pl.BlockSpec((tm, 3 * D), lambda i: (i, 0))
    yspec = pl.BlockSpec((tm, BRANCH_W), lambda i: (i, 0))
    return pl.pallas_call(
        body, name=name, grid=(T // tm,),
        in_specs=[tok, wide, yspec, yspec, yspec,
                  pl.BlockSpec((3, BRANCH_W, D), lambda i: (0, 0, 0)),
                  pl.BlockSpec((D, D), lambda i: (0, 0))],
        out_specs=[tok, wide, tok],
        out_shape=[_sds((T, D), F32), _sds((T, 3 * D), BF), _sds((T, D), BF)],
        compiler_params=_cp((PAR,)),
    )(x1, big, *ys, wb, wo)


def _merge_bwd(dx2, big, p, wb, wo, name):
    T, D = dx2.shape
    tm = min(TM, T)

    def body(dx_ref, gp_ref, p_ref, wb_ref, wo_ref, dp_ref, dgp_ref, dyc_ref, dyr_ref, dya_ref, dxb_ref):
        dxb = dx_ref[...].astype(BF)
        dxb_ref[...] = dxb
        dm = lax.dot_general(dxb, wo_ref[...], NT_DIMS, preferred_element_type=F32)
        for i, dy_ref in enumerate((dyc_ref, dyr_ref, dya_ref)):
            cols = slice(i * D, (i + 1) * D)
            gt = _sigmoid(gp_ref[:, cols].astype(F32))
            dpb = (dm * gt).astype(BF)
            dp_ref[:, cols] = dpb
            dgp_ref[:, cols] = (dm * p_ref[:, cols].astype(F32) * gt * (1.0 - gt)).astype(BF)
            dy_ref[...] = lax.dot_general(dpb, wb_ref[i], NT_DIMS, preferred_element_type=F32).astype(BF)

    tok = pl.BlockSpec((tm, D), lambda i: (i, 0))
    wide = pl.BlockSpec((tm, 3 * D), lambda i: (i, 0))
    yspec = pl.BlockSpec((tm, BRANCH_W), lambda i: (i, 0))
    return pl.pallas_call(
        body, name=name, grid=(T // tm,),
        in_specs=[tok, wide, wide,
                  pl.BlockSpec((3, BRANCH_W, D), lambda i: (0, 0, 0)),
                  pl.BlockSpec((D, D), lambda i: (0, 0))],
        out_specs=[wide, wide, yspec, yspec, yspec, tok],
        out_shape=[_sds((T, 3 * D), BF), _sds(big.shape, BF)] + [_sds((T, BRANCH_W), BF)] * 3 + [_sds((T, D), BF)],
        compiler_params=_cp((PAR,)),
    )(dx2, big, p, wb, wo)


def _loss_head(x, tgt, fw, name):
    T, D = x.shape
    tm = min(TM, T)

    def body(x_ref, t_ref, w_ref, loss_ref, dx_ref, dw_ref):
        @pl.when(pl.program_id(0) == 0)
        def _():
            loss_ref[...] = jnp.zeros_like(loss_ref)
            dw_ref[...] = jnp.zeros_like(dw_ref)

        xv = x_ref[...]
        wv = w_ref[...]
        e = xv * _rms_r(xv) * wv - t_ref[...]
        loss_ref[...] += 0.5 * jnp.sum(jnp.mean(e * e, axis=-1, keepdims=True))
        dx, dn = _rms_bwd(e * (1.0 / D), xv, wv)
        dx_ref[...] = dx
        dw_ref[...] += dn

    tok = pl.BlockSpec((tm, D), lambda i: (i, 0))
    return pl.pallas_call(
        body, name=name, grid=(T // tm,),
        in_specs=[tok, tok, pl.BlockSpec((1, D), lambda i: (0, 0))],
        out_specs=[pl.BlockSpec((8, LANE), lambda i: (0, 0)), tok, pl.BlockSpec((1, D), lambda i: (0, 0))],
        out_shape=[_sds((8, LANE), F32), _sds((T, D), F32), _sds((1, D), F32)],
        compiler_params=_cp((ARB,)),
    )(x, tgt, fw)


def _block_rows(rows, cols):
    cap = max(8, (1 << 18) // cols)
    best = None
    for r in range(8, rows + 1, 8):
        if rows % r == 0 and r <= cap:
            best = r
    return best if best is not None else rows


def _sum4(land, l, n_layers, name, prev=None, comm=None):
    _, rows, cols = land.shape
    br = _block_rows(rows, cols)

    def body(*refs):
        l_ref, o_ref = refs[0], refs[-1]
        o_ref[...] = ((l_ref[3].astype(F32) + l_ref[0].astype(F32)) + l_ref[1].astype(F32)) + l_ref[2].astype(F32)

    in_specs = [pl.BlockSpec((4, br, cols), lambda i: (0, i, 0))]
    args = [land]
    aliases = {}
    if prev is not None:
        in_specs.append(pl.BlockSpec(memory_space=pl.ANY))
        args.append(prev)
        aliases = {1: 0}
    main, extra = _call(
        body, name=name, grid=(rows // br,), args=args, in_specs=in_specs,
        out_specs=[pl.BlockSpec((None, br, cols), lambda i: (l, i, 0))],
        out_shape=[_sds((n_layers, rows, cols), F32)], aliases=aliases, sem=(ARB,), comm=comm)
    return main[0], extra


def _adamw_math(w, g, m, v):
    m = ADAM_B1 * m + (1.0 - ADAM_B1) * g
    v = ADAM_B2 * v + (1.0 - ADAM_B2) * (g * g)
    m_hat = m / (1.0 - ADAM_B1 ** ADAM_STEP)
    v_hat = v / (1.0 - ADAM_B2 ** ADAM_STEP)
    delta = -ADAM_LR * (m_hat / (jnp.sqrt(v_hat) + ADAM_EPS) + ADAM_WD * w)
    return delta, m, v


def _adamw(w, ga, gb, m, v, name):
    rows, cols = w.shape
    br = _block_rows(rows, cols)
    two = gb is not None

    def body(*refs):
        if two:
            w_ref, ga_ref, gb_ref, m_ref, v_ref, g_ref, d_ref, nm_ref, nv_ref = refs
            g = ga_ref[...] + gb_ref[...]
        else:
            w_ref, ga_ref, m_ref, v_ref, g_ref, d_ref, nm_ref, nv_ref = refs
            g = ga_ref[...]
        d, nm, nv = _adamw_math(w_ref[...], g, m_ref[...], v_ref[...])
        g_ref[...] = g
        d_ref[...] = d
        nm_ref[...] = nm
        nv_ref[...] = nv

    blk = pl.BlockSpec((br, cols), lambda i: (i, 0))
    args = [w, ga] + ([gb] if two else []) + [m, v]
    return pl.pallas_call(
        body, name=name, grid=(rows // br,),
        in_specs=[blk] * len(args), out_specs=[blk] * 4,
        out_shape=[_sds((rows, cols), F32)] * 4,
        compiler_params=_cp((PAR,)),
    )(*args)


class _CoreSwap:
    def __init__(self, srcs):
        self.srcs = list(srcs)
        n = len(self.srcs)
        self.out_shape = [_sds(s.shape, s.dtype) for s in self.srcs]
        self.scratch = [pltpu.SemaphoreType.DMA((n,)), pltpu.SemaphoreType.DMA((n,))]

    def _copies(self, src, dst, sems):
        x, y, c = _place()
        return [pltpu.make_async_remote_copy(
            src_ref=src[k], dst_ref=dst[k], send_sem=sems[0].at[k], recv_sem=sems[1].at[k],
            device_id=(x, y, 1 - c), device_id_type=MESH) for k in range(len(src))]

    def start(self, src, dst, sems):
        for cp in self._copies(src, dst, sems):
            cp.start()

    def wait(self, src, dst, sems):
        for cp in self._copies(src, dst, sems):
            cp.wait()


def _allreduce_small(v, name):
    rows = v.shape[0]
    flips = [(fx, fy, fc) for fx in (0, 1) for fy in (0, 1) for fc in (0, 1) if fx or fy or fc]

    def body(v_ref, o_ref, all_s, ssem, rsem):
        x, y, c = _place()

        def peer(f):
            return (x + f[0] - 2 * x * f[0], y + f[1] - 2 * y * f[1], c + f[2] - 2 * c * f[2])

        def slot(p):
            return all_s.at[4 * p[0] + 2 * p[1] + p[2]]

        def copy(k, f, owner):
            return pltpu.make_async_remote_copy(
                src_ref=v_ref, dst_ref=slot(owner), send_sem=ssem.at[k], recv_sem=rsem.at[k],
                device_id=peer(f), device_id_type=MESH)

        sends = [copy(k, f, (x, y, c)) for k, f in enumerate(flips)]
        for cp in sends:
            cp.start()
        all_s[4 * x + 2 * y + c] = v_ref[...]
        for k, f in enumerate(flips):
            copy(k, f, peer(f)).wait_recv()
        for cp in sends:
            cp.wait_send()
        acc = all_s[0]
        for d in range(1, 8):
            acc = acc + all_s[d]
        o_ref[...] = acc

    return pl.pallas_call(
        body, name=name,
        in_specs=[pl.BlockSpec(memory_space=pltpu.VMEM)],
        out_specs=pl.BlockSpec(memory_space=pltpu.VMEM),
        out_shape=_sds((rows, LANE), F32),
        scratch_shapes=[pltpu.VMEM((8, rows, LANE), F32), pltpu.SemaphoreType.DMA((7,)), pltpu.SemaphoreType.DMA((7,))],
    )(v)


BIG_NAMES = ("ffn1_w_gate", "ffn1_w_up", "ffn1_w_down", "w_in", "w_branch", "w_merge_gate", "w_out",
             "ffn2_w_gate", "ffn2_w_up", "ffn2_w_down")


FFN1 = ("ffn1_w_gate", "ffn1_w_up", "ffn1_w_down")
FFN2 = ("ffn2_w_gate", "ffn2_w_up", "ffn2_w_down")
MIX_IN = ("w_in", "w_merge_gate")
MIX_OUT = ("w_branch", "w_out")


def _keys(names, l):
    return [(n, l) for n in names]


def _local_step(x, tgt, small, convw_full, wx, n_layers):
    T, D = x.shape
    L = n_layers
    ns = N_SHARD
    dq = D // ns
    W = wx.w

    def hosted(call, keys, scatter=False):
        comm = wx.pieces(keys, scatter)
        main, extra = call(comm)
        if comm is not None:
            wx.arrived(keys, extra, scatter)
        return main

    def mixer_views(l):
        g4 = W[("w_merge_gate", l)]
        gates = jnp.transpose(g4, (0, 2, 1, 3)).reshape(D, 3 * D)
        win = jnp.transpose(W[("w_in", l)], (1, 0, 2)).reshape(D, -1)
        return jnp.concatenate([gates, _permute_in_cols(win)], axis=-1)

    def out_views(l):
        wb = jnp.transpose(W[("w_branch", l)], (1, 2, 0, 3)).reshape(3, BRANCH_W, D)
        wo = W[("w_out", l)].reshape(D, D)
        return wb, wo

    tb = _ret_tables(T)
    rb_pad = jnp.pad(small["rel_bias"], ((0, 0), (0, 0), (0, RB_PAD - N_REL)))

    saved = []
    h = x
    for l in range(L):
        s = {"x0": h}
        nxt = l + 1
        x1, s["g1"], s["u1"] = hosted(
            lambda c: _ffn_fwd(h, small["ffn1_norm"][l][None], W[("ffn1_w_gate", l)], W[("ffn1_w_up", l)],
                               W[("ffn1_w_down", l)], f"ffn1_fwd_{l}", comm=c), _keys(MIX_IN, l))
        s["x1"] = x1
        s["wbig"] = mixer_views(l)
        big, s["h"] = _inproj_fwd(x1, small["mix_norm"][l][None], s["wbig"], f"inproj_fwd_{l}")
        s["big"] = big
        s["bias"] = jnp.transpose(_relbias_expand(rb_pad[l], f"relbias_expand_{l}"), (1, 0, 2))
        s["yc"] = _conv_fwd(big, convw_full[l], f"conv_fwd_{l}")
        s["yr"], s["o"], s["st"] = hosted(lambda c: _ret_fwd(big, tb, f"ret_fwd_{l}", comm=c), _keys(MIX_OUT, l))
        (s["ya"],) = hosted(lambda c: _att_fwd(big, s["bias"], f"att_fwd_{l}", comm=c), _keys(FFN2, l))
        s["wb"], s["wo"] = out_views(l)
        x2, s["p"], s["mg"] = _merge_fwd(x1, big, (s["yc"], s["yr"], s["ya"]), s["wb"], s["wo"], f"merge_fwd_{l}")
        s["x2"] = x2
        h, s["g2"], s["u2"] = hosted(
            lambda c: _ffn_fwd(x2, small["ffn2_norm"][l][None], W[("ffn2_w_gate", l)], W[("ffn2_w_up", l)],
                               W[("ffn2_w_down", l)], f"ffn2_fwd_{l}", comm=c), _keys(FFN1, nxt) if nxt < L else [])
        saved.append(s)

    loss_p, dx, d_final = _loss_head(h, tgt, small["final_norm"][None], "loss_head")

    gs = {"final_norm": d_final[0]}
    for k in ("ffn1_norm", "mix_norm", "ffn2_norm", "rel_bias", "conv_w"):
        gs[k] = [None] * L
    tk = min(2048, T)
    nk = T // tk

    def ffn_back(pre, l, dxo, x_in, g, u, first_keys, second_keys):
        nw = small[pre + "_norm"][l][None]
        dgv, duv, av, hb, dacc = hosted(
            lambda c: _ffn_bwd_hidden(dxo, x_in, nw, g, u, W[(pre + "_w_down", l)], f"{pre}_bwd_hidden_{l}", comm=c),
            first_keys, scatter=True)
        dxn, dn = hosted(
            lambda c: _ffn_bwd_resid(dgv, duv, W[(pre + "_w_gate", l)], W[(pre + "_w_up", l)], x_in, nw, dxo,
                                     f"{pre}_bwd_resid_{l}", comm=c),
            second_keys, scatter=True)
        gs[pre + "_norm"][l] = dn[0]
        return dxn, (hb, dgv, duv, av, dacc)

    def ffn_grads(pre, l, hb, dgv, duv, av, dacc, chain=False, carry=()):
        fs = dgv.shape[-1]
        hspec = pl.BlockSpec((tk, D), lambda p, q, k: (k, 0))
        sspec = pl.BlockSpec((None, tk, fs), lambda p, q, k: (p, k, 0))
        down_spec = pl.BlockSpec((None, fs, D), lambda p, q, k: (p, 0, 0))
        jobs = [(pre + "_w_gate", dgv, hb, sspec, hspec, (ns, fs, D), down_spec),
                (pre + "_w_up", duv, hb, sspec, hspec, (ns, fs, D), down_spec),
                (pre + "_w_down", av, dacc, sspec, hspec, (ns, fs, D), down_spec)]
        before = None
        for nm, a, b, a_spec, b_spec, shape, o_spec in jobs:
            def product(c):
                r = _tn(a, b, a_spec, b_spec, _sds(shape, BF), o_spec, (ns, 1, nk), f"d{nm}_{l}", comm=c)
                return (r, []) if c is None else r
            if before is None:
                keys = list(carry)
            else:
                keys = [before] if chain else []
            wx.g[(nm, l)] = hosted(product, keys, scatter=True)
            before = (nm, l)

    for l in reversed(range(L)):
        s = saved[l]
        above = _keys(FFN1, l + 1) if l + 1 < L else [None] * 3
        dx, parts = ffn_back("ffn2", l, dx, s["x2"], s["g2"], s["u2"], [k for k in above[:1] if k],
                             [k for k in above[1:2] if k])
        ffn_grads("ffn2", l, *parts, carry=[k for k in above[2:] if k])
        dp, dbig, dyc, dyr, dya, dxb = _merge_bwd(dx, s["big"], s["p"], s["wb"], s["wo"], f"merge_bwd_{l}")
        wx.g[("w_out", l)] = _tn(
            s["mg"], dxb, pl.BlockSpec((tk, dq), lambda p, q, k: (k, p)), pl.BlockSpec((tk, D), lambda p, q, k: (k, 0)),
            _sds((ns, dq, D), BF), pl.BlockSpec((None, dq, D), lambda p, q, k: (p, 0, 0)), (ns, 1, nk), f"dw_out_{l}")
        gb = None
        for i, yv in enumerate((s["yc"], s["yr"], s["ya"])):
            gb = _tn(yv, dp,
                     pl.BlockSpec((tk, BRANCH_W), lambda p, q, k: (k, 0)),
                     pl.BlockSpec((tk, dq), lambda p, q, k, i=i: (k, i * ns + p)),
                     _sds((ns, 3, BRANCH_W, dq), BF),
                     pl.BlockSpec((None, None, BRANCH_W, dq), lambda p, q, k, i=i: (p, i, 0, 0)),
                     (ns, 1, nk), f"dw_branch{i}_{l}", prev=gb)
        wx.g[("w_branch", l)] = gb
        dbig, dcw = _conv_bwd(s["big"], dyc, convw_full[l], dbig, f"conv_bwd_{l}")
        gs["conv_w"][l] = dcw
        dbig = _ret_bwd(s["big"], s["o"], s["st"], dyr, tb, dbig, f"ret_bwd_{l}")
        dbig, dbias = hosted(lambda c: _att_bwd(s["big"], s["bias"], dya, dbig, f"att_bwd_{l}", comm=c),
                             _keys(FFN2, l), scatter=True)
        gs["rel_bias"][l] = _relbias_grad(jnp.transpose(dbias, (1, 0, 2)), f"relbias_grad_{l}")[:, :N_REL]
        n_in = N_SEG * BRANCH_W
        bn = 1024 if (3 * D) % 1024 == 0 else BRANCH_W
        dwp = _tn(s["h"], dbig, pl.BlockSpec((tk, D), lambda p, q, k: (k, 0)),
                  pl.BlockSpec((tk, bn), lambda p, q, k: (k, 3 * D // bn + q)),
                  _sds((D, n_in), BF), pl.BlockSpec((D, bn), lambda p, q, k: (0, q)), (1, n_in // bn, nk), f"dw_in_{l}")
        wx.g[("w_in", l)] = jnp.transpose(_unpermute_in_cols(dwp).reshape(D, ns, n_in // ns), (1, 0, 2))
        wx.g[("w_merge_gate", l)] = _tn_gates(s["h"], dbig, ns, tk, f"dw_merge_gate_{l}")
        dx, dn = hosted(
            lambda c: _inproj_bwd(dbig, s["wbig"], s["x1"], small["mix_norm"][l][None], dx, f"inproj_bwd_{l}", comm=c),
            [("w_in", l)], scatter=True)
        gs["mix_norm"][l] = dn[0]
        dx, parts = ffn_back("ffn1", l, dx, s["x0"], s["g1"], s["u1"],
                             [("w_merge_gate", l), ("w_branch", l), ("w_out", l)], [])
        ffn_grads("ffn1", l, *parts, chain=(l == 0))

    for k in ("ffn1_norm", "mix_norm", "ffn2_norm", "rel_bias", "conv_w"):
        gs[k] = jnp.stack(gs[k])
    return loss_p, dx, gs


class _Exchange:
    def __init__(self, shards):
        self.shards = shards
        self.w = {}
        self.g = {}
        self.landed = {}

    def own(self, key):
        return self.shards[key[0]][key[1]].astype(BF)

    def pieces(self, keys, scatter):
        if not keys:
            return None
        if scatter:
            return _Pieces([self.g[k] for k in keys], True)
        return _HalfGather([_halves(self.own(k)) for k in keys])

    def arrived(self, keys, outs, scatter):
        for k, o in zip(keys, outs):
            if scatter:
                self.landed[k] = o
            else:
                self.w[k] = o.reshape((N_SHARD,) + self.shards[k[0]].shape[1:])


def _halves(a):
    return a.reshape(2, -1, a.shape[-1])


TRANSPOSED_GRADS = ("ffn1_w_gate", "ffn1_w_up", "ffn2_w_gate", "ffn2_w_up")
W_NAMES = ("ffn1_norm", "ffn1_w_gate", "ffn1_w_up", "ffn1_w_down", "mix_norm", "w_in", "conv_w", "rel_bias", "w_branch",
           "w_merge_gate", "w_out", "ffn2_norm", "ffn2_w_gate", "ffn2_w_up", "ffn2_w_down", "final_norm")


def _as2d(a):
    return a.reshape(1, -1) if a.ndim == 1 else a.reshape(-1, a.shape[-1])


def kernel(x, ffn1_norm, ffn1_w_gate, ffn1_w_up, ffn1_w_down, mix_norm, w_in, conv_w, rel_bias, w_branch, w_merge_gate, w_out, ffn2_norm, ffn2_w_gate, ffn2_w_up, ffn2_w_down, final_norm, loss_target, m_ffn1_norm, m_ffn1_w_gate, m_ffn1_w_up, m_ffn1_w_down, m_mix_norm, m_w_in, m_conv_w, m_rel_bias, m_w_branch, m_w_merge_gate, m_w_out, m_ffn2_norm, m_ffn2_w_gate, m_ffn2_w_up, m_ffn2_w_down, m_final_norm, v_ffn1_norm, v_ffn1_w_gate, v_ffn1_w_up, v_ffn1_w_down, v_mix_norm, v_w_in, v_conv_w, v_rel_bias, v_w_branch, v_w_merge_gate, v_w_out, v_ffn2_norm, v_ffn2_w_gate, v_ffn2_w_up, v_ffn2_w_down, v_final_norm):
    given = dict(locals())
    w = {n: given[n] for n in W_NAMES}
    m = {n: given["m_" + n] for n in W_NAMES}
    v = {n: given["v_" + n] for n in W_NAMES}
    my_chip = 2 * lax.axis_index("x") + lax.axis_index("y")
    L = w_in.shape[0]

    wx = _Exchange({n: jnp.swapaxes(w[n], 1, 2) if n in TRANSPOSED_GRADS else w[n] for n in BIG_NAMES})
    first = _keys(FFN1, 0)
    got = _comm_alone(_HalfGather([_halves(wx.own(k)) for k in first] + [_halves(conv_w)]), "gather_first")
    wx.arrived(first, got[:-1], False)
    convw_full = jnp.transpose(got[-1].reshape((N_SHARD,) + conv_w.shape), (1, 2, 0, 3)).reshape(
        conv_w.shape[0], conv_w.shape[1], -1)

    small = {n: w[n] for n in ("ffn1_norm", "mix_norm", "ffn2_norm", "final_norm", "rel_bias")}
    loss_p, grad_x, gs = _local_step(x[0], loss_target[0], small, convw_full, wx, L)
    last = [(FFN1[-1], 0)]
    wx.arrived(last, _comm_alone(wx.pieces(last, True), "scatter_last"), True)

    sums, others = [], []
    for n in BIG_NAMES:
        acc = None
        for l in range(L):
            a = wx.landed[(n, l)]
            swap = _CoreSwap(sums[-1:]) if sums and l == L - 1 else None
            acc, came = _sum4(a.reshape(4, -1, a.shape[-1]), l, L, f"sum4_{n}_{l}", prev=acc, comm=swap)
            others += came
        sums.append(acc.reshape(-1, acc.shape[-1]))
    others += _comm_alone(_CoreSwap(sums[-1:]), "swap_last")

    parts = [gs["ffn1_norm"].reshape(-1), gs["mix_norm"].reshape(-1), gs["ffn2_norm"].reshape(-1),
             gs["final_norm"].reshape(-1), gs["rel_bias"].reshape(-1), gs["conv_w"].reshape(-1), loss_p[0]]
    sizes = [p.shape[0] for p in parts]
    flat = jnp.concatenate(parts)
    rows = -(-flat.shape[0] // (8 * LANE)) * 8
    flat = jnp.pad(flat, (0, rows * LANE - flat.shape[0])).reshape(rows, LANE)
    red = _allreduce_small(flat, "allreduce_small").reshape(-1)
    offs = [0]
    for sz in sizes:
        offs.append(offs[-1] + sz)
    sm = {}
    for i, n in enumerate(("ffn1_norm", "mix_norm", "ffn2_norm", "final_norm", "rel_bias", "conv_w")):
        sm[n] = red[offs[i]:offs[i + 1]]
    loss = red[offs[6]]
    sm["conv_w"] = lax.dynamic_slice_in_dim(sm["conv_w"].reshape(conv_w.shape[0], conv_w.shape[1], -1),
                                            my_chip * conv_w.shape[2], conv_w.shape[2], axis=2)

    grads, deltas, new_m, new_v = {}, {}, {}, {}
    big_sum = dict(zip(BIG_NAMES, zip(sums, others)))
    for n in W_NAMES:
        flip = n in TRANSPOSED_GRADS

        def view(a):
            return jnp.swapaxes(a, 1, 2) if flip else a

        shape = view(w[n]).shape
        if n in big_sum:
            ga, gb = big_sum[n]
        else:
            ga, gb = _as2d(sm[n].reshape(shape)), None
        out = _adamw(_as2d(view(w[n])), ga, gb, _as2d(view(m[n])), _as2d(view(v[n])), f"adamw_{n}")
        grads[n], deltas[n], new_m[n], new_v[n] = (view(o.reshape(shape)) for o in out)

    return (loss, grad_x[None], *[grads[n] for n in W_NAMES], *[deltas[n] for n in W_NAMES],
            *[new_m[n] for n in W_NAMES], *[new_v[n] for n in W_NAMES])
```

```python
import functools
import math

import jax
import jax.numpy as jnp
from jax import lax
from jax.experimental import pallas as pl
from jax.experimental.pallas import tpu as pltpu

F32 = jnp.float32
BF = jnp.bfloat16
MESH = pl.DeviceIdType.MESH
ARB = "arbitrary"
PAR = "parallel"

EPS = 1e-6
NEG_INF = -1e30
ROPE_BASE = 10000.0
CHUNK = 64
BRANCH_W = 512
H_RET = 4
DK_RET = 128
H_ATT = 8
DH_ATT = 64
N_PREV = 8
REL_CLIP = 128
N_REL = 2 * REL_CLIP + 1
N_SHARD = 4
LANE = 128
RET_L = 512
ATT_TQ = 128
ATT_SUB = 4
ATT_PAD = N_PREV * CHUNK
ATT_SPAN = ATT_TQ + ATT_PAD
ATT_TOEP = 2 * REL_CLIP
RB_PAD = 264
TM = 512
TM_FFN = 1024

ADAM_LR = 0.001
ADAM_B1 = 0.9
ADAM_B2 = 0.999
ADAM_EPS = 1e-08
ADAM_WD = 0.01
ADAM_STEP = 10

NT_DIMS = (((1,), (1,)), ((), ()))
TN_DIMS = (((0,), (0,)), ((), ()))


def _cp(sem, vmem_mb=48):
    return pltpu.CompilerParams(dimension_semantics=sem, vmem_limit_bytes=vmem_mb << 20)


def _sds(shape, dtype):
    return jax.ShapeDtypeStruct(tuple(shape), dtype)


def _rms_r(x):
    return lax.rsqrt(jnp.mean(x * x, axis=-1, keepdims=True) + EPS)


def _sigmoid(x):
    return 0.5 * jnp.tanh(0.5 * x) + 0.5


def _rms_bwd(dh, xv, nw):
    r = _rms_r(xv)
    xh = xv * r
    dxh = dh * nw
    dx = r * (dxh - xh * jnp.mean(dxh * xh, axis=-1, keepdims=True))
    return dx, jnp.sum(dh * xh, axis=0, keepdims=True)


def _place():
    return lax.axis_index("x"), lax.axis_index("y"), lax.axis_index("c")


def _other_chips(x, y):
    return [(1 - x, y), (x, 1 - y), (1 - x, 1 - y)]


class _Pieces:
    def __init__(self, srcs, scatter):
        self.srcs = list(srcs)
        self.scatter = scatter
        n = len(self.srcs)
        self.out_shape = [_sds(s.shape if scatter else (N_SHARD,) + s.shape, s.dtype) for s in self.srcs]
        self.scratch = [pltpu.SemaphoreType.DMA((n,)), pltpu.SemaphoreType.DMA((3, n)), pltpu.SemaphoreType.DMA((3, n))]

    def _copies(self, src, dst, sems, waiting):
        lsem, ssem, rsem = sems
        x, y, c = _place()
        mine = 2 * x + y
        n = len(src)

        def remote(j, k, chip, s_ref, d_ref):
            return pltpu.make_async_remote_copy(
                src_ref=s_ref, dst_ref=d_ref, send_sem=ssem.at[j, k], recv_sem=rsem.at[j, k],
                device_id=(chip[0], chip[1], c), device_id_type=MESH)

        chips = list(enumerate(_other_chips(x, y)))
        if self.scatter:
            local = [pltpu.make_async_copy(src[k].at[mine], dst[k].at[3], lsem.at[k]) for k in range(n)]
            sends = [remote(j, k, ch, src[k].at[2 * ch[0] + ch[1]], dst[k].at[j]) for j, ch in chips for k in range(n)]
            recvs = sends
        else:
            local = [pltpu.make_async_copy(src[k], dst[k].at[mine], lsem.at[k]) for k in range(n)]
            sends = [remote(j, k, ch, src[k], dst[k].at[mine]) for j, ch in chips for k in range(n)]
            recvs = [remote(j, k, ch, src[k], dst[k].at[2 * ch[0] + ch[1]]) for j, ch in chips for k in range(n)
                     ] if waiting else []
        return local, sends, recvs

    def start(self, src, dst, sems):
        local, sends, _ = self._copies(src, dst, sems, False)
        for cp in local + sends:
            cp.start()

    def wait(self, src, dst, sems):
        local, sends, recvs = self._copies(src, dst, sems, True)
        for cp in recvs:
            cp.wait_recv()
        for cp in sends:
            cp.wait_send()
        for cp in local:
            cp.wait()


class _HalfGather:
    def __init__(self, srcs):
        self.srcs = list(srcs)
        n = len(self.srcs)
        self.out_shape = [_sds((N_SHARD,) + s.shape, s.dtype) for s in self.srcs]
        self.scratch = [pltpu.SemaphoreType.DMA((n,))] + [pltpu.SemaphoreType.DMA((3, n)) for _ in range(4)]

    def _plan(self, src, dst, sems, want):
        lsem, s1, r1, s2, r2 = sems
        x, y, c = _place()
        mine = 2 * x + y
        n = len(src)
        chips = [(j, ch, 2 * ch[0] + ch[1]) for j, ch in enumerate(_other_chips(x, y))]

        def copy(s_ref, d_ref, ssem, rsem, to):
            return pltpu.make_async_remote_copy(src_ref=s_ref, dst_ref=d_ref, send_sem=ssem, recv_sem=rsem,
                                                device_id=to, device_id_type=MESH)

        def over(kind, make):
            return [make(j, ch, slot, k) for j, ch, slot in chips for k in range(n)] if kind in want else []

        local = [pltpu.make_async_copy(src[k], dst[k].at[mine], lsem.at[k]) for k in range(n)] if "local" in want else []
        sends = over("sends", lambda j, ch, slot, k: copy(src[k].at[c], dst[k].at[mine, c], s1.at[j, k], r1.at[j, k],
                                                          (ch[0], ch[1], c)))
        lands = over("lands", lambda j, ch, slot, k: copy(src[k].at[c], dst[k].at[slot, c], s1.at[j, k], r1.at[j, k],
                                                          (ch[0], ch[1], c)))
        passes = over("passes", lambda j, ch, slot, k: copy(dst[k].at[slot, c], dst[k].at[slot, c], s2.at[j, k],
                                                            r2.at[j, k], (x, y, 1 - c)))
        gets = over("gets", lambda j, ch, slot, k: copy(dst[k].at[slot, 1 - c], dst[k].at[slot, 1 - c], s2.at[j, k],
                                                        r2.at[j, k], (x, y, 1 - c)))
        return local, sends, lands, passes, gets

    def start(self, src, dst, sems):
        lsem, s1, r1, s2, r2 = sems
        x, y, c = _place()
        mine = 2 * x + y
        for k in range(len(src)):
            pltpu.make_async_copy(src[k], dst[k].at[mine], lsem.at[k]).start()
        for j, ch in enumerate(_other_chips(x, y)):
            for k in range(len(src)):
                pltpu.make_async_remote_copy(
                    src_ref=src[k].at[c], dst_ref=dst[k].at[mine, c], send_sem=s1.at[j, k], recv_sem=r1.at[j, k],
                    device_id=(ch[0], ch[1], c), device_id_type=MESH).start()

    def relay(self, src, dst, sems):
        _, _, lands, passes, _ = self._plan(src, dst, sems, ("lands", "passes"))
        for land, fwd in zip(lands, passes):
            land.wait_recv()
            fwd.start()

    def finish(self, src, dst, sems):
        local, sends, _, passes, gets = self._plan(src, dst, sems, ("local", "sends", "passes", "gets"))
        for cp in gets:
            cp.wait_recv()
        for cp in sends + passes:
            cp.wait_send()
        for cp in local:
            cp.wait()

    def wait(self, src, dst, sems):
        self.relay(src, dst, sems)
        self.finish(src, dst, sems)


def _call(body, *, name, args, in_specs, out_specs, out_shape, grid=(), scratch_shapes=(), sem=None, comm=None,
          aliases=None, vmem_mb=48):
    in_specs, out_specs, out_shape = list(in_specs), list(out_specs), list(out_shape)
    scratch, args = list(scratch_shapes), list(args)
    n_in, n_out, n_scr = len(in_specs), len(out_specs), len(scratch)
    if comm is None:
        def kernel_body(*refs):
            body(*refs)
    else:
        c_in, c_out = len(comm.srcs), len(comm.out_shape)

        def kernel_body(*refs):
            o0 = n_in + c_in
            s0 = o0 + n_out + c_out
            cin, cout, sems = refs[n_in:o0], refs[o0 + n_out:s0], refs[s0 + n_scr:]
            main = refs[:n_in] + refs[o0:o0 + n_out] + refs[s0:s0 + n_scr]
            if grid:
                ids = [pl.program_id(a) for a in range(len(grid))]
                first = functools.reduce(lambda p, q: p & q, [i == 0 for i in ids])
                last = functools.reduce(lambda p, q: p & q, [i == g - 1 for i, g in zip(ids, grid)])

                @pl.when(first)
                def _():
                    comm.start(cin, cout, sems)

                body(*main)

                steps = math.prod(grid)
                if hasattr(comm, "relay") and steps >= 4:
                    flat = functools.reduce(lambda p, q: p + q, [i * math.prod(grid[a + 1:]) for a, i in enumerate(ids)])

                    @pl.when(flat == (2 * steps) // 3)
                    def _():
                        comm.relay(cin, cout, sems)

                    @pl.when(last)
                    def _():
                        comm.finish(cin, cout, sems)
                else:
                    @pl.when(last)
                    def _():
                        comm.wait(cin, cout, sems)
            else:
                comm.start(cin, cout, sems)
                body(*main)
                comm.wait(cin, cout, sems)

        hbm = pl.BlockSpec(memory_space=pl.ANY)
        in_specs += [hbm] * c_in
        out_specs += [hbm] * c_out
        out_shape += comm.out_shape
        scratch += comm.scratch
        args += comm.srcs
    params = dict(vmem_limit_bytes=vmem_mb << 20)
    if grid:
        params["dimension_semantics"] = sem
    outs = pl.pallas_call(
        kernel_body, name=name, grid=grid, in_specs=in_specs, out_specs=out_specs, out_shape=out_shape,
        scratch_shapes=scratch, input_output_aliases=aliases or {}, compiler_params=pltpu.CompilerParams(**params),
    )(*args)
    return list(outs[:n_out]), list(outs[n_out:])


def _comm_alone(comm, name):
    return _call(lambda: None, name=name, args=[], in_specs=[], out_specs=[], out_shape=[], comm=comm)[1]


def _ffn_fwd(x, nw, wg, wu, wd, name, comm=None):
    T, D = x.shape
    ns, fs, _ = wg.shape
    tm = min(TM_FFN, T)

    def body(x_ref, nw_ref, wg_ref, wu_ref, wd_ref, xo_ref, g_ref, u_ref, h_s, acc_s):
        j = pl.program_id(1)

        @pl.when(j == 0)
        def _():
            xv = x_ref[...]
            h_s[...] = (xv * _rms_r(xv) * nw_ref[...]).astype(BF)
            acc_s[...] = jnp.zeros_like(acc_s)

        h = h_s[...]
        gb = lax.dot_general(h, wg_ref[...], NT_DIMS, preferred_element_type=F32).astype(BF)
        ub = lax.dot_general(h, wu_ref[...], NT_DIMS, preferred_element_type=F32).astype(BF)
        g_ref[...] = gb
        u_ref[...] = ub
        g = gb.astype(F32)
        a = (g * _sigmoid(g) * ub.astype(F32)).astype(BF)
        acc_s[...] += jnp.dot(a, wd_ref[...], preferred_element_type=F32)

        @pl.when(j == ns - 1)
        def _():
            xo_ref[...] = x_ref[...] + 0.5 * acc_s[...]

    wspec = pl.BlockSpec((None, fs, D), lambda i, j: (j, 0, 0))
    return _call(
        body, name=name, grid=(T // tm, ns), args=(x, nw, wg, wu, wd), comm=comm, vmem_mb=56,
        in_specs=[pl.BlockSpec((tm, D), lambda i, j: (i, 0)),
                  pl.BlockSpec((1, D), lambda i, j: (0, 0)),
                  wspec, wspec,
                  pl.BlockSpec((None, fs, D), lambda i, j: (j, 0, 0))],
        out_specs=[pl.BlockSpec((tm, D), lambda i, j: (i, 0)),
                   pl.BlockSpec((None, tm, fs), lambda i, j: (j, i, 0)),
                   pl.BlockSpec((None, tm, fs), lambda i, j: (j, i, 0))],
        out_shape=[_sds((T, D), F32), _sds((ns, T, fs), BF), _sds((ns, T, fs), BF)],
        scratch_shapes=[pltpu.VMEM((tm, D), BF), pltpu.VMEM((tm, D), F32)],
        sem=(ARB, ARB))


def _ffn_bwd_hidden(dxo, x, nw, g, u, wd, name, comm=None):
    T, D = x.shape
    ns, fs, _ = wd.shape
    tm = min(TM_FFN, T)

    def body(dxo_ref, x_ref, nw_ref, g_ref, u_ref, wd_ref, dg_ref, du_ref, a_ref, h_ref, dacc_ref, dacc_s):
        @pl.when(pl.program_id(1) == 0)
        def _():
            xv = x_ref[...]
            h_ref[...] = (xv * _rms_r(xv) * nw_ref[...]).astype(BF)
            db = (0.5 * dxo_ref[...]).astype(BF)
            dacc_ref[...] = db
            dacc_s[...] = db

        da = lax.dot_general(dacc_s[...], wd_ref[...], NT_DIMS, preferred_element_type=F32)
        gv = g_ref[...].astype(F32)
        uv = u_ref[...].astype(F32)
        s = _sigmoid(gv)
        sg = gv * s
        a_ref[...] = (sg * uv).astype(BF)
        du_ref[...] = (da * sg).astype(BF)
        dg_ref[...] = (da * uv * (s * (1.0 + gv * (1.0 - s)))).astype(BF)

    tok = pl.BlockSpec((tm, D), lambda i, j: (i, 0))
    hid = pl.BlockSpec((None, tm, fs), lambda i, j: (j, i, 0))
    return _call(
        body, name=name, grid=(T // tm, ns), args=(dxo, x, nw, g, u, wd), comm=comm, vmem_mb=56,
        in_specs=[tok, tok, pl.BlockSpec((1, D), lambda i, j: (0, 0)), hid, hid,
                  pl.BlockSpec((None, fs, D), lambda i, j: (j, 0, 0))],
        out_specs=[hid, hid, hid, tok, tok],
        out_shape=[_sds((ns, T, fs), BF)] * 3 + [_sds((T, D), BF)] * 2,
        scratch_shapes=[pltpu.VMEM((tm, D), BF)],
        sem=(ARB, ARB))


def _ffn_bwd_resid(dg, du, wg, wu, x, nw, dxo, name, comm=None):
    T, D = x.shape
    ns, fs, _ = wg.shape
    tm = min(TM_FFN, T)

    def body(dg_ref, du_ref, wg_ref, wu_ref, x_ref, nw_ref, dxo_ref, dx_ref, dnw_ref, acc_s):
        i = pl.program_id(0)
        j = pl.program_id(1)
        prod = (jnp.dot(dg_ref[...], wg_ref[...], preferred_element_type=F32)
                + jnp.dot(du_ref[...], wu_ref[...], preferred_element_type=F32))

        @pl.when((i == 0) & (j == 0))
        def _():
            dnw_ref[...] = jnp.zeros_like(dnw_ref)

        @pl.when(j == 0)
        def _():
            acc_s[...] = prod

        @pl.when(j > 0)
        def _():
            acc_s[...] += prod

        @pl.when(j == ns - 1)
        def _():
            dx, dn = _rms_bwd(acc_s[...], x_ref[...], nw_ref[...])
            dx_ref[...] = dxo_ref[...] + dx
            dnw_ref[...] += dn

    tok = pl.BlockSpec((tm, D), lambda i, j: (i, 0))
    row = pl.BlockSpec((1, D), lambda i, j: (0, 0))
    hid = pl.BlockSpec((None, tm, fs), lambda i, j: (j, i, 0))
    wspec = pl.BlockSpec((None, fs, D), lambda i, j: (j, 0, 0))
    return _call(
        body, name=name, grid=(T // tm, ns), args=(dg, du, wg, wu, x, nw, dxo), comm=comm, vmem_mb=56,
        in_specs=[hid, hid, wspec, wspec, tok, row, tok],
        out_specs=[tok, row],
        out_shape=[_sds((T, D), F32), _sds((1, D), F32)],
        scratch_shapes=[pltpu.VMEM((tm, D), F32)],
        sem=(ARB, ARB))


def _tn(a, b, a_spec, b_spec, out_shape, out_spec, grid, name, prev=None, comm=None):
    nk = grid[-1]
    acc_shape = tuple(d for d in out_spec.block_shape if d is not None)

    def body(*refs):
        a_ref, b_ref = refs[0], refs[1]
        o_ref, acc = refs[-2], refs[-1]
        k = pl.program_id(2)
        prod = lax.dot_general(a_ref[...], b_ref[...], TN_DIMS, preferred_element_type=F32)

        @pl.when(k == 0)
        def _():
            acc[...] = prod

        @pl.when(k > 0)
        def _():
            acc[...] += prod

        @pl.when(k == nk - 1)
        def _():
            o_ref[...] = acc[...].astype(o_ref.dtype)

    in_specs = [a_spec, b_spec]
    args = [a, b]
    aliases = {}
    if prev is not None:
        in_specs.append(pl.BlockSpec(memory_space=pl.ANY))
        args.append(prev)
        aliases = {2: 0}
    main, extra = _call(
        body, name=name, grid=grid, args=args, in_specs=in_specs, out_specs=[out_spec], out_shape=[out_shape],
        scratch_shapes=[pltpu.VMEM(acc_shape, F32)], aliases=aliases, sem=(ARB, ARB, ARB), comm=comm)
    return main[0] if comm is None else (main[0], extra)


def _tn_gates(h, dbig, ns, tk, name):
    T, D = h.shape
    dq = D // ns
    nk = T // tk

    def body(a_ref, b_ref, o_ref, acc):
        k = pl.program_id(1)
        prod = lax.dot_general(a_ref[...], b_ref[...], TN_DIMS, preferred_element_type=F32)

        @pl.when(k == 0)
        def _():
            acc[...] = prod

        @pl.when(k > 0)
        def _():
            acc[...] += prod

        @pl.when(k == nk - 1)
        def _():
            for s in range(ns):
                o_ref[s] = acc[s * dq:(s + 1) * dq, :].astype(o_ref.dtype)

    return pl.pallas_call(
        body, name=name, grid=(3, nk),
        in_specs=[pl.BlockSpec((tk, D), lambda q, k: (k, 0)), pl.BlockSpec((tk, D), lambda q, k: (k, q))],
        out_specs=pl.BlockSpec((ns, None, dq, D), lambda q, k: (0, q, 0, 0)),
        out_shape=_sds((ns, 3, dq, D), BF),
        scratch_shapes=[pltpu.VMEM((D, D), F32)],
        compiler_params=_cp((PAR, ARB)),
    )(h, dbig)


def _inproj_fwd(x, nw, wbig, name):
    T, D = x.shape
    nb = wbig.shape[-1]
    tm = min(2 * TM, T)
    bn = min(2048, nb)

    def body(x_ref, nw_ref, w_ref, o_ref, h_ref, h_s):
        @pl.when(pl.program_id(1) == 0)
        def _():
            xv = x_ref[...]
            hb = (xv * _rms_r(xv) * nw_ref[...]).astype(BF)
            h_s[...] = hb
            h_ref[...] = hb

        o_ref[...] = jnp.dot(h_s[...], w_ref[...], preferred_element_type=F32).astype(BF)

    return pl.pallas_call(
        body, name=name, grid=(T // tm, nb // bn),
        in_specs=[pl.BlockSpec((tm, D), lambda i, n: (i, 0)),
                  pl.BlockSpec((1, D), lambda i, n: (0, 0)),
                  pl.BlockSpec((D, bn), lambda i, n: (0, n))],
        out_specs=[pl.BlockSpec((tm, bn), lambda i, n: (i, n)),
                   pl.BlockSpec((tm, D), lambda i, n: (i, 0))],
        out_shape=[_sds((T, nb), BF), _sds((T, D), BF)],
        scratch_shapes=[pltpu.VMEM((tm, D), BF)],
        compiler_params=_cp((PAR, ARB)),
    )(x, nw, wbig)


def _inproj_bwd(dbig, wbig, x, nw, dxin, name, comm=None):
    T, D = x.shape
    nb = wbig.shape[-1]
    tm = min(TM_FFN, T)
    tk = min(2048, nb)
    nk = nb // tk

    def body(a_ref, w_ref, x_ref, nw_ref, dxin_ref, dx_ref, dnw_ref, acc_s):
        i = pl.program_id(0)
        k = pl.program_id(1)
        prod = lax.dot_general(a_ref[...], w_ref[...], NT_DIMS, preferred_element_type=F32)

        @pl.when((i == 0) & (k == 0))
        def _():
            dnw_ref[...] = jnp.zeros_like(dnw_ref)

        @pl.when(k == 0)
        def _():
            acc_s[...] = prod

        @pl.when(k > 0)
        def _():
            acc_s[...] += prod

        @pl.when(k == nk - 1)
        def _():
            dx, dn = _rms_bwd(acc_s[...], x_ref[...], nw_ref[...])
            dx_ref[...] = dxin_ref[...] + dx
            dnw_ref[...] += dn

    tok = pl.BlockSpec((tm, D), lambda i, k: (i, 0))
    row = pl.BlockSpec((1, D), lambda i, k: (0, 0))
    return _call(
        body, name=name, grid=(T // tm, nk), args=(dbig, wbig, x, nw, dxin), comm=comm, vmem_mb=56,
        in_specs=[pl.BlockSpec((tm, tk), lambda i, k: (i, k)),
                  pl.BlockSpec((D, tk), lambda i, k: (0, k)),
                  tok, row, tok],
        out_specs=[tok, row],
        out_shape=[_sds((T, D), F32), _sds((1, D), F32)],
        scratch_shapes=[pltpu.VMEM((tm, D), F32)],
        sem=(ARB, ARB))


CONV_R = 512
CONV_BASE, CONV_GROUP = 0, 3
ATT_BASE, ATT_GROUP = 12, 3
RET_BASE, RET_GROUP = 24, 4
N_SEG = 10


def _permute_in_cols(w):
    lead = w.shape[:-1]
    w4 = w.reshape(lead + (N_SEG, BRANCH_W // LANE, LANE))

    def grouped(lo, hi):
        return jnp.swapaxes(w4[..., lo:hi, :, :], -3, -2).reshape(lead + (-1,))

    return jnp.concatenate([grouped(0, 3), grouped(7, 10), grouped(3, 7)], axis=-1)


def _unpermute_in_cols(w):
    lead = w.shape[:-1]
    nblk = BRANCH_W // LANE

    def segs(lo, n):
        part = w[..., lo * LANE:(lo + nblk * n) * LANE].reshape(lead + (nblk, n, LANE))
        return jnp.swapaxes(part, -3, -2)

    conv, att, ret = segs(CONV_BASE, 3), segs(ATT_BASE, 3), segs(RET_BASE, 4)
    return jnp.concatenate([conv, ret, att], axis=-3).reshape(lead + (-1,))


def _seg0(big):
    return (big.shape[1] - N_SEG * BRANCH_W) // LANE


def _group_spec(big, base, group, rows, where):
    first = (_seg0(big) + base) // group
    assert first * group == _seg0(big) + base

    def index(*ids):
        r, g = where(*ids)
        return r, first + g

    return pl.BlockSpec((rows, group * LANE), index)


CU, CB, CC = (slice(k * LANE, (k + 1) * LANE) for k in range(3))
AQ, AK, AV = CU, CB, CC
RQ, RK, RV, RG = (slice(k * LANE, (k + 1) * LANE) for k in range(4))


def _conv_fwd(big, cw, name):
    T = big.shape[0]
    R = min(CONV_R, T)

    def body(g_ref, w_ref, y_ref, z_s):
        z_s[pl.ds(0, 8), :] = jnp.zeros((8, LANE), F32)

        def fill(t, c):
            sl = pl.ds(pl.multiple_of(t * R, R), R)
            z_s[pl.ds(pl.multiple_of(t * R + 8, 8), R), :] = g_ref[sl, CC].astype(F32) * g_ref[sl, CU].astype(F32)
            return c

        lax.fori_loop(0, T // R, fill, 0)
        w0, w1, w2 = w_ref[0:1, :], w_ref[1:2, :], w_ref[2:3, :]

        def step(t, c):
            zz = z_s[pl.ds(pl.multiple_of(t * R, R), R + 8), :]
            z0 = zz[8:]
            z1 = pltpu.roll(zz, 1, 0)[8:]
            z2 = pltpu.roll(zz, 2, 0)[8:]
            sl = pl.ds(pl.multiple_of(t * R, R), R)
            y_ref[sl, :] = (g_ref[sl, CB].astype(F32) * (w2 * z0 + w1 * z1 + w0 * z2)).astype(BF)
            return c

        lax.fori_loop(0, T // R, step, 0)

    return pl.pallas_call(
        body, name=name, grid=(BRANCH_W // LANE,),
        in_specs=[_group_spec(big, CONV_BASE, CONV_GROUP, T, lambda j: (0, j)),
                  pl.BlockSpec((3, LANE), lambda j: (0, j))],
        out_specs=pl.BlockSpec((T, LANE), lambda j: (0, j)),
        out_shape=_sds((T, BRANCH_W), BF),
        scratch_shapes=[pltpu.VMEM((T + 8, LANE), F32)],
        compiler_params=_cp((PAR,)),
    )(big, cw)


def _conv_bwd(big, dy, cw, dbig, name):
    T = big.shape[0]
    R = min(CONV_R, T)

    def body(g_ref, dy_ref, w_ref, _, o_ref, dw_ref, z_s, d_s):
        z_s[pl.ds(0, 8), :] = jnp.zeros((8, LANE), F32)
        d_s[pl.ds(T, 8), :] = jnp.zeros((8, LANE), F32)

        def fill(t, c):
            sl = pl.ds(pl.multiple_of(t * R, R), R)
            z_s[pl.ds(pl.multiple_of(t * R + 8, 8), R), :] = g_ref[sl, CC].astype(F32) * g_ref[sl, CU].astype(F32)
            d_s[sl, :] = dy_ref[sl, :].astype(F32) * g_ref[sl, CB].astype(F32)
            return c

        lax.fori_loop(0, T // R, fill, 0)
        w0, w1, w2 = w_ref[0:1, :], w_ref[1:2, :], w_ref[2:3, :]

        def step(t, carry):
            a0, a1, a2 = carry
            zz = z_s[pl.ds(pl.multiple_of(t * R, R), R + 8), :]
            z0 = zz[8:]
            z1 = pltpu.roll(zz, 1, 0)[8:]
            z2 = pltpu.roll(zz, 2, 0)[8:]
            sl = pl.ds(pl.multiple_of(t * R, R), R)
            dyv = dy_ref[sl, :].astype(F32)
            o_ref[sl, CB] = (dyv * (w2 * z0 + w1 * z1 + w0 * z2)).astype(BF)
            dd = d_s[pl.ds(pl.multiple_of(t * R, R), R + 8), :]
            d0 = dd[:R]
            d1 = pltpu.roll(dd, R + 7, 0)[:R]
            d2 = pltpu.roll(dd, R + 6, 0)[:R]
            dz = w2 * d0 + w1 * d1 + w0 * d2
            o_ref[sl, CC] = (dz * g_ref[sl, CU].astype(F32)).astype(BF)
            o_ref[sl, CU] = (dz * g_ref[sl, CC].astype(F32)).astype(BF)
            a0 = a0 + jnp.sum(d0 * z2, axis=0, keepdims=True)
            a1 = a1 + jnp.sum(d0 * z1, axis=0, keepdims=True)
            a2 = a2 + jnp.sum(d0 * z0, axis=0, keepdims=True)
            return a0, a1, a2

        zero = jnp.zeros((1, LANE), F32)
        a0, a1, a2 = lax.fori_loop(0, T // R, step, (zero, zero, zero))
        dw_ref[0:1, :] = a0
        dw_ref[1:2, :] = a1
        dw_ref[2:3, :] = a2

    group = _group_spec(big, CONV_BASE, CONV_GROUP, T, lambda j: (0, j))
    w = pl.BlockSpec((3, LANE), lambda j: (0, j))
    return pl.pallas_call(
        body, name=name, grid=(BRANCH_W // LANE,),
        in_specs=[group, pl.BlockSpec((T, LANE), lambda j: (0, j)), w, pl.BlockSpec(memory_space=pl.ANY)],
        out_specs=[group, w],
        out_shape=[_sds(dbig.shape, BF), _sds((3, BRANCH_W), F32)],
        scratch_shapes=[pltpu.VMEM((T + 8, LANE), F32), pltpu.VMEM((T + 8, LANE), F32)],
        input_output_aliases={3: 0}, compiler_params=_cp((PAR,)),
    )(big, dy, cw, dbig)


def _ret_tables(T):
    L = min(RET_L, T)
    hh = jnp.arange(H_RET, dtype=F32)
    lg = jnp.log1p(-jnp.exp2(-5.0 - hh))
    n = jnp.arange(L, dtype=F32)
    a = jnp.exp(lg[:, None] * (n + 1.0))
    b = jnp.exp(lg[:, None] * (L - 1.0 - n))
    gl = jnp.exp(lg * L)
    ch = jnp.arange(L) // CHUNK
    m = jnp.exp(lg[:, None, None] * jnp.abs(n[:, None] - n[None, :])) * (ch[None, :] <= ch[:, None]).astype(F32)
    inv_freq = ROPE_BASE ** (-jnp.linspace(0.0, 1.0, DK_RET // 2, dtype=F32))
    ang = jnp.arange(T, dtype=F32)[:, None] * inv_freq[None, :]
    cos, sin = jnp.cos(ang), jnp.sin(ang)
    return dict(
        L=L, M=m,
        a=jnp.broadcast_to(a[:, :, None], (H_RET, L, DK_RET)),
        b=jnp.broadcast_to(b[:, :, None], (H_RET, L, DK_RET)),
        gl=jnp.broadcast_to(gl[:, None, None], (H_RET, 1, DK_RET)),
        cos=jnp.concatenate([cos, cos], axis=-1), sin=jnp.concatenate([-sin, sin], axis=-1))


def _rot(x, cs, sn):
    return x * cs + pltpu.roll(x, DK_RET // 2, 1) * sn


def _unrot(dy, cs, sn):
    return dy * cs + pltpu.roll(dy * sn, DK_RET // 2, 1)


def _ret_fwd(big, tb, name, comm=None):
    T = big.shape[0]
    L = tb["L"]
    nsc = T // L
    scale = DK_RET ** -0.5

    def body(x_ref, cos_ref, sin_ref, m_ref, a_ref, b_ref, gl_ref, y_ref, o_ref, st_ref, s_s):
        @pl.when(pl.program_id(1) == 0)
        def _():
            s_s[...] = jnp.zeros_like(s_s)

        cs, sn = cos_ref[...], sin_ref[...]
        qt = _rot(x_ref[:, RQ].astype(F32), cs, sn) * scale
        kt = _rot(x_ref[:, RK].astype(F32), cs, sn)
        qb, kb, vb = qt.astype(BF), kt.astype(BF), x_ref[:, RV]
        s_prev = s_s[...]
        st_ref[...] = s_prev
        p = lax.dot_general(qb, kb, NT_DIMS, preferred_element_type=F32) * m_ref[...]
        o = (jnp.dot(p.astype(BF), vb, preferred_element_type=F32)
             + jnp.dot((qt * a_ref[...]).astype(BF), s_prev.astype(BF), preferred_element_type=F32))
        s_s[...] = s_prev * gl_ref[...] + lax.dot_general((kt * b_ref[...]).astype(BF), vb, TN_DIMS,
                                                         preferred_element_type=F32)
        o_ref[...] = o
        gv = x_ref[:, RG].astype(F32)
        y_ref[...] = (gv * _sigmoid(gv) * o * _rms_r(o)).astype(BF)

    tab = pl.BlockSpec((L, DK_RET), lambda h, i: (i, 0))
    per_head = pl.BlockSpec((None, L, DK_RET), lambda h, i: (h, 0, 0))
    out = pl.BlockSpec((L, LANE), lambda h, i: (i, h))
    return _call(
        body, name=name, grid=(H_RET, nsc), comm=comm,
        args=(big, tb["cos"], tb["sin"], tb["M"], tb["a"], tb["b"], tb["gl"]),
        in_specs=[_group_spec(big, RET_BASE, RET_GROUP, L, lambda h, i: (i, h)), tab, tab,
                  pl.BlockSpec((None, L, L), lambda h, i: (h, 0, 0)), per_head, per_head,
                  pl.BlockSpec((None, 1, DK_RET), lambda h, i: (h, 0, 0))],
        out_specs=[out, out, pl.BlockSpec((None, None, DK_RET, DK_RET), lambda h, i: (i, h, 0, 0))],
        out_shape=[_sds((T, BRANCH_W), BF), _sds((T, BRANCH_W), F32), _sds((nsc, H_RET, DK_RET, DK_RET), F32)],
        scratch_shapes=[pltpu.VMEM((DK_RET, DK_RET), F32)],
        sem=(ARB, ARB))


def _ret_bwd(big, o, st, dy, tb, dbig, name):
    T = big.shape[0]
    L = tb["L"]
    nsc = T // L
    scale = DK_RET ** -0.5

    def body(x_ref, cos_ref, sin_ref, m_ref, a_ref, b_ref, gl_ref, o_ref, st_ref, dy_ref, _, d_ref, ds_s):
        @pl.when(pl.program_id(1) == 0)
        def _():
            ds_s[...] = jnp.zeros_like(ds_s)

        cs, sn = cos_ref[...], sin_ref[...]
        mm, av, bv = m_ref[...], a_ref[...], b_ref[...]
        qt = _rot(x_ref[:, RQ].astype(F32), cs, sn) * scale
        kt = _rot(x_ref[:, RK].astype(F32), cs, sn)
        qb, kb, vb = qt.astype(BF), kt.astype(BF), x_ref[:, RV]
        pb = (lax.dot_general(qb, kb, NT_DIMS, preferred_element_type=F32) * mm).astype(BF)
        ov = o_ref[...]
        r = _rms_r(ov)
        oh = ov * r
        gv = x_ref[:, RG].astype(F32)
        sg = _sigmoid(gv)
        dyv = dy_ref[...].astype(F32)
        d_ref[:, RG] = (dyv * oh * (sg * (1.0 + gv * (1.0 - sg)))).astype(BF)
        doh = dyv * gv * sg
        dob = (r * (doh - oh * jnp.mean(doh * oh, axis=-1, keepdims=True))).astype(BF)
        dsb = ds_s[...].astype(BF)
        spb = st_ref[...].astype(BF)
        dpb = (lax.dot_general(dob, vb, NT_DIMS, preferred_element_type=F32) * mm).astype(BF)
        dqt = (jnp.dot(dpb, kb, preferred_element_type=F32)
               + lax.dot_general(dob, spb, NT_DIMS, preferred_element_type=F32) * av)
        dkt = (lax.dot_general(dpb, qb, TN_DIMS, preferred_element_type=F32)
               + lax.dot_general(vb, dsb, NT_DIMS, preferred_element_type=F32) * bv)
        dv = (lax.dot_general(pb, dob, TN_DIMS, preferred_element_type=F32)
              + jnp.dot((kt * bv).astype(BF), dsb, preferred_element_type=F32))
        ds_s[...] = ds_s[...] * gl_ref[...] + lax.dot_general((qt * av).astype(BF), dob, TN_DIMS,
                                                              preferred_element_type=F32)
        d_ref[:, RQ] = (_unrot(dqt, cs, sn) * scale).astype(BF)
        d_ref[:, RK] = _unrot(dkt, cs, sn).astype(BF)
        d_ref[:, RV] = dv.astype(BF)

    def rev(i):
        return nsc - 1 - i

    group = _group_spec(big, RET_BASE, RET_GROUP, L, lambda h, i: (rev(i), h))
    tab = pl.BlockSpec((L, DK_RET), lambda h, i: (rev(i), 0))
    per_head = pl.BlockSpec((None, L, DK_RET), lambda h, i: (h, 0, 0))
    out = pl.BlockSpec((L, LANE), lambda h, i: (rev(i), h))
    return pl.pallas_call(
        body, name=name, grid=(H_RET, nsc),
        in_specs=[group, tab, tab,
                  pl.BlockSpec((None, L, L), lambda h, i: (h, 0, 0)), per_head, per_head,
                  pl.BlockSpec((None, 1, DK_RET), lambda h, i: (h, 0, 0)),
                  out, pl.BlockSpec((None, None, DK_RET, DK_RET), lambda h, i: (rev(i), h, 0, 0)), out,
                  pl.BlockSpec(memory_space=pl.ANY)],
        out_specs=group,
        out_shape=_sds(dbig.shape, BF),
        scratch_shapes=[pltpu.VMEM((DK_RET, DK_RET), F32)],
        input_output_aliases={10: 0}, compiler_params=_cp((PAR, ARB)),
    )(big, tb["cos"], tb["sin"], tb["M"], tb["a"], tb["b"], tb["gl"], o, st, dy, dbig)


def _relbias_onehot(n):
    mm = lax.broadcasted_iota(jnp.int32, (RB_PAD, ATT_TOEP), 1)
    rr = lax.broadcasted_iota(jnp.int32, (RB_PAD, ATT_TOEP), 0)
    idx = jnp.clip(n + ATT_TOEP - mm, 0, 2 * REL_CLIP)
    return (rr == idx).astype(F32)


def _split3(x):
    hi = x.astype(BF).astype(F32)
    mid = (x - hi).astype(BF).astype(F32)
    lo = x - hi - mid
    return jnp.concatenate([hi, mid, lo], axis=0).astype(BF)


def _join3(y):
    k = y.shape[0] // 3
    return (y[:k] + y[k:2 * k]) + y[2 * k:]


def _relbias_expand(rbp, name):
    far = ATT_SPAN - ATT_TOEP

    def body(rb_ref, o_ref):
        rb = rb_ref[...]
        const = jnp.broadcast_to(rb[:, 2 * REL_CLIP:2 * REL_CLIP + 1], (H_ATT, far))

        rb3 = _split3(rb)

        def row(n, c):
            toep = _join3(jnp.dot(rb3, _relbias_onehot(n).astype(BF), preferred_element_type=F32))
            m = lax.broadcasted_iota(jnp.int32, (1, ATT_SPAN), 1)
            d = n // CHUNK + N_PREV - m // CHUNK
            neg = jnp.where((d >= 0) & (d <= N_PREV), 0.0, NEG_INF).astype(F32)
            o_ref[n] = jnp.concatenate([const, toep], axis=1) + neg
            return c

        lax.fori_loop(0, ATT_TQ, row, 0)

    return pl.pallas_call(
        body, name=name,
        in_specs=[pl.BlockSpec(memory_space=pltpu.VMEM)],
        out_specs=pl.BlockSpec(memory_space=pltpu.VMEM),
        out_shape=_sds((ATT_TQ, H_ATT, ATT_SPAN), F32),
    )(rbp)


def _relbias_grad(dbt, name):
    far = ATT_SPAN - ATT_TOEP

    def body(d_ref, o_ref):
        def row(n, carry):
            acc, cs = carry
            dn = d_ref[n]
            acc = acc + _join3(lax.dot_general(_split3(dn[:, far:]), _relbias_onehot(n).astype(BF), NT_DIMS,
                                               preferred_element_type=F32))
            cs = cs + jnp.sum(dn[:, :far], axis=1, keepdims=True)
            return acc, cs

        acc, cs = lax.fori_loop(0, ATT_TQ, row, (jnp.zeros((H_ATT, RB_PAD), F32), jnp.zeros((H_ATT, 1), F32)))
        rr = lax.broadcasted_iota(jnp.int32, (H_ATT, RB_PAD), 1)
        o_ref[...] = acc + jnp.where(rr == 2 * REL_CLIP, cs, 0.0)

    return pl.pallas_call(
        body, name=name,
        in_specs=[pl.BlockSpec(memory_space=pltpu.VMEM)],
        out_specs=pl.BlockSpec(memory_space=pltpu.VMEM),
        out_shape=_sds((H_ATT, RB_PAD), F32),
    )(dbt)


def _att_pad_fill(dst_s, src_ref, cols, T):
    dst_s[pl.ds(0, ATT_PAD), :] = jnp.zeros((ATT_PAD, LANE), dst_s.dtype)
    R = min(512, T)

    def cp(t, c):
        dst_s[pl.ds(pl.multiple_of(ATT_PAD + t * R, LANE), R), :] = src_ref[pl.ds(pl.multiple_of(t * R, R), R), cols]
        return c

    lax.fori_loop(0, T // R, cp, 0)


ATT_WIN = ATT_SUB * ATT_TQ + ATT_PAD


def _att_probs(s_full, sub, bias, t0):
    s = s_full[sub * ATT_TQ:(sub + 1) * ATT_TQ, sub * ATT_TQ:sub * ATT_TQ + ATT_SPAN] * (DH_ATT ** -0.5) + bias
    key_pos = t0 + sub * ATT_TQ - ATT_PAD + lax.broadcasted_iota(jnp.int32, (1, ATT_SPAN), 1)
    s = jnp.where(key_pos >= 0, s, NEG_INF)
    p = jnp.exp(s - jnp.max(s, axis=-1, keepdims=True))
    return p * (1.0 / jnp.sum(p, axis=-1, keepdims=True))


def _att_band(tiles):
    rows = []
    for sub, t in enumerate(tiles):
        parts = []
        if sub:
            parts.append(jnp.zeros((ATT_TQ, sub * ATT_TQ), BF))
        parts.append(t)
        if sub < ATT_SUB - 1:
            parts.append(jnp.zeros((ATT_TQ, (ATT_SUB - 1 - sub) * ATT_TQ), BF))
        rows.append(jnp.concatenate(parts, axis=1))
    return jnp.concatenate(rows, axis=0)


def _att_head_masks(x):
    first = lax.broadcasted_iota(jnp.int32, (1, LANE), 1) < DH_ATT
    zero = jnp.zeros_like(x)
    return first, (jnp.where(first, x, zero), jnp.where(first, zero, x))


def _att_fwd(big, bias, name, comm=None):
    T = big.shape[0]
    rows = ATT_SUB * ATT_TQ
    nt = T // rows

    def body(x_ref, b_ref, y_ref, kp_s, vp_s):
        i = pl.program_id(1)

        @pl.when(i == 0)
        def _():
            _att_pad_fill(kp_s, x_ref, AK, T)
            _att_pad_fill(vp_s, x_ref, AV, T)

        t0 = pl.multiple_of(i * rows, rows)
        kw = kp_s[pl.ds(t0, ATT_WIN), :]
        vw = vp_s[pl.ds(t0, ATT_WIN), :]
        first, qm = _att_head_masks(x_ref[pl.ds(t0, rows), AQ])
        outs = []
        for hh in range(2):
            s_full = lax.dot_general(qm[hh], kw, NT_DIMS, preferred_element_type=F32)
            band = _att_band([_att_probs(s_full, sub, b_ref[hh], t0).astype(BF) for sub in range(ATT_SUB)])
            outs.append(jnp.dot(band, vw, preferred_element_type=F32))
        y_ref[...] = jnp.where(first, outs[0], outs[1]).astype(BF)

    return _call(
        body, name=name, grid=(H_ATT // 2, nt), args=(big, bias), comm=comm,
        in_specs=[_group_spec(big, ATT_BASE, ATT_GROUP, T, lambda p, i: (0, p)),
                  pl.BlockSpec((2, ATT_TQ, ATT_SPAN), lambda p, i: (p, 0, 0))],
        out_specs=[pl.BlockSpec((rows, LANE), lambda p, i: (i, p))],
        out_shape=[_sds((T, BRANCH_W), BF)],
        scratch_shapes=[pltpu.VMEM((T + ATT_PAD, LANE), BF), pltpu.VMEM((T + ATT_PAD, LANE), BF)],
        sem=(ARB, ARB))


def _att_bwd(big, bias, dy, dbig, name, comm=None):
    T = big.shape[0]
    rows = ATT_SUB * ATT_TQ
    nt = T // rows
    scale = DH_ATT ** -0.5

    def body(x_ref, b_ref, dy_ref, _, d_ref, db_ref, kp_s, vp_s, dk_s, dv_s):
        i = pl.program_id(1)

        @pl.when(i == 0)
        def _():
            _att_pad_fill(kp_s, x_ref, AK, T)
            _att_pad_fill(vp_s, x_ref, AV, T)
            dk_s[...] = jnp.zeros_like(dk_s)
            dv_s[...] = jnp.zeros_like(dv_s)
            db_ref[...] = jnp.zeros_like(db_ref)

        t0 = pl.multiple_of(i * rows, rows)
        win = pl.ds(t0, ATT_WIN)
        kw = kp_s[win, :]
        vw = vp_s[win, :]
        first, qm = _att_head_masks(x_ref[pl.ds(t0, rows), AQ])
        _, dom = _att_head_masks(dy_ref[...])
        dqs, dkt, dvt = [], None, None
        for hh in range(2):
            s_full = lax.dot_general(qm[hh], kw, NT_DIMS, preferred_element_type=F32)
            dp_full = lax.dot_general(dom[hh], vw, NT_DIMS, preferred_element_type=F32)
            ps, dss, db = [], [], None
            for sub in range(ATT_SUB):
                pn = _att_probs(s_full, sub, b_ref[hh], t0)
                dp = dp_full[sub * ATT_TQ:(sub + 1) * ATT_TQ, sub * ATT_TQ:sub * ATT_TQ + ATT_SPAN]
                ds = pn * (dp - jnp.sum(dp * pn, axis=-1, keepdims=True))
                db = ds if db is None else db + ds
                ps.append(pn.astype(BF))
                dss.append(ds.astype(BF))
            db_ref[hh] += db
            ds_band, p_band = _att_band(dss), _att_band(ps)
            dqs.append(jnp.dot(ds_band, kw, preferred_element_type=F32))
            qt = jnp.transpose(qm[hh].astype(F32)).astype(BF)
            dot_ = jnp.transpose(dom[hh].astype(F32)).astype(BF)
            dk_h = jnp.dot(qt, ds_band, preferred_element_type=F32)
            dv_h = jnp.dot(dot_, p_band, preferred_element_type=F32)
            dkt = dk_h if dkt is None else dkt + dk_h
            dvt = dv_h if dvt is None else dvt + dv_h
        d_ref[pl.ds(t0, rows), AQ] = (jnp.where(first, dqs[0], dqs[1]) * scale).astype(BF)
        dk_s[win, :] += jnp.transpose(dkt) * scale
        dv_s[win, :] += jnp.transpose(dvt)

        @pl.when(i == nt - 1)
        def _():
            R = min(512, T)

            def cp(t, c):
                src = pl.ds(pl.multiple_of(ATT_PAD + t * R, LANE), R)
                dst = pl.ds(pl.multiple_of(t * R, R), R)
                d_ref[dst, AK] = dk_s[src, :].astype(BF)
                d_ref[dst, AV] = dv_s[src, :].astype(BF)
                return c

            lax.fori_loop(0, T // R, cp, 0)

    group = _group_spec(big, ATT_BASE, ATT_GROUP, T, lambda p, i: (0, p))
    tile = pl.BlockSpec((rows, LANE), lambda p, i: (i, p))
    bspec = pl.BlockSpec((2, ATT_TQ, ATT_SPAN), lambda p, i: (p, 0, 0))
    return _call(
        body, name=name, grid=(H_ATT // 2, nt), args=(big, bias, dy, dbig), comm=comm, aliases={3: 0}, vmem_mb=56,
        in_specs=[group, bspec, tile, pl.BlockSpec(memory_space=pl.ANY)],
        out_specs=[group, bspec],
        out_shape=[_sds(dbig.shape, BF), _sds((H_ATT, ATT_TQ, ATT_SPAN), F32)],
        scratch_shapes=[pltpu.VMEM((T + ATT_PAD, LANE), BF), pltpu.VMEM((T + ATT_PAD, LANE), BF),
                        pltpu.VMEM((T + ATT_PAD, LANE), F32), pltpu.VMEM((T + ATT_PAD, LANE), F32)],
        sem=(ARB, ARB))


def _merge_fwd(x1, big, ys, wb, wo, name):
    T, D = x1.shape
    tm = min(TM, T)

    def body(x_ref, gp_ref, yc_ref, yr_ref, ya_ref, wb_ref, wo_ref, x2_ref, p_ref, mg_ref):
        merged = jnp.zeros((tm, D), F32)
        for i, y_ref in enumerate((yc_ref, yr_ref, ya_ref)):
            cols = slice(i * D, (i + 1) * D)
            pb = jnp.dot(y_ref[...], wb_ref[i], preferred_element_type=F32).astype(BF)
            p_ref[:, cols] = pb
            merged = merged + _sigmoid(gp_ref[:, cols].astype(F32)) * pb.astype(F32)
        mb = merged.astype(BF)
        mg_ref[...] = mb
        x2_ref[...] = x_ref[...] + jnp.dot(mb, wo_ref[...], preferred_element_type=F32)

    tok = pl.BlockSpec((tm, D), lambda i: (i, 0))
    wide = pl.BlockSpec((tm, 3 * D), lambda i: (i, 0))
    yspec = pl.BlockSpec((tm, BRANCH_W), lambda i: (i, 0))
    return pl.pallas_call(
        body, name=name, grid=(T // tm,),
        in_specs=[tok, wide, yspec, yspec, yspec,
                  pl.BlockSpec((3, BRANCH_W, D), lambda i: (0, 0, 0)),
                  pl.BlockSpec((D, D), lambda i: (0, 0))],
        out_specs=[tok, wide, tok],
        out_shape=[_sds((T, D), F32), _sds((T, 3 * D), BF), _sds((T, D), BF)],
        compiler_params=_cp((PAR,)),
    )(x1, big, *ys, wb, wo)


def _merge_bwd(dx2, big, p, wb, wo, name):
    T, D = dx2.shape
    tm = min(TM, T)

    def body(dx_ref, gp_ref, p_ref, wb_ref, wo_ref, dp_ref, dgp_ref, dyc_ref, dyr_ref, dya_ref, dxb_ref):
        dxb = dx_ref[...].astype(BF)
        dxb_ref[...] = dxb
        dm = lax.dot_general(dxb, wo_ref[...], NT_DIMS, preferred_element_type=F32)
        for i, dy_ref in enumerate((dyc_ref, dyr_ref, dya_ref)):
            cols = slice(i * D, (i + 1) * D)
            gt = _sigmoid(gp_ref[:, cols].astype(F32))
            dpb = (dm * gt).astype(BF)
            dp_ref[:, cols] = dpb
            dgp_ref[:, cols] = (dm * p_ref[:, cols].astype(F32) * gt * (1.0 - gt)).astype(BF)
            dy_ref[...] = lax.dot_general(dpb, wb_ref[i], NT_DIMS, preferred_element_type=F32).astype(BF)

    tok = pl.BlockSpec((tm, D), lambda i: (i, 0))
    wide = pl.BlockSpec((tm, 3 * D), lambda i: (i, 0))
    yspec = pl.BlockSpec((tm, BRANCH_W), lambda i: (i, 0))
    return pl.pallas_call(
        body, name=name, grid=(T // tm,),
        in_specs=[tok, wide, wide,
                  pl.BlockSpec((3, BRANCH_W, D), lambda i: (0, 0, 0)),
                  pl.BlockSpec((D, D), lambda i: (0, 0))],
        out_specs=[wide, wide, yspec, yspec, yspec, tok],
        out_shape=[_sds((T, 3 * D), BF), _sds(big.shape, BF)] + [_sds((T, BRANCH_W), BF)] * 3 + [_sds((T, D), BF)],
        compiler_params=_cp((PAR,)),
    )(dx2, big, p, wb, wo)


def _loss_head(x, tgt, fw, name):
    T, D = x.shape
    tm = min(TM, T)

    def body(x_ref, t_ref, w_ref, loss_ref, dx_ref, dw_ref):
        @pl.when(pl.program_id(0) == 0)
        def _():
            loss_ref[...] = jnp.zeros_like(loss_ref)
            dw_ref[...] = jnp.zeros_like(dw_ref)

        xv = x_ref[...]
        wv = w_ref[...]
        e = xv * _rms_r(xv) * wv - t_ref[...]
        loss_ref[...] += 0.5 * jnp.sum(jnp.mean(e * e, axis=-1, keepdims=True))
        dx, dn = _rms_bwd(e * (1.0 / D), xv, wv)
        dx_ref[...] = dx
        dw_ref[...] += dn

    tok = pl.BlockSpec((tm, D), lambda i: (i, 0))
    return pl.pallas_call(
        body, name=name, grid=(T // tm,),
        in_specs=[tok, tok, pl.BlockSpec((1, D), lambda i: (0, 0))],
        out_specs=[pl.BlockSpec((8, LANE), lambda i: (0, 0)), tok, pl.BlockSpec((1, D), lambda i: (0, 0))],
        out_shape=[_sds((8, LANE), F32), _sds((T, D), F32), _sds((1, D), F32)],
        compiler_params=_cp((ARB,)),
    )(x, tgt, fw)


def _block_rows(rows, cols):
    cap = max(8, (1 << 18) // cols)
    best = None
    for r in range(8, rows + 1, 8):
        if rows % r == 0 and r <= cap:
            best = r
    return best if best is not None else rows


def _sum4(land, l, n_layers, name, prev=None, comm=None):
    _, rows, cols = land.shape
    br = _block_rows(rows, cols)

    def body(*refs):
        l_ref, o_ref = refs[0], refs[-1]
        o_ref[...] = ((l_ref[3].astype(F32) + l_ref[0].astype(F32)) + l_ref[1].astype(F32)) + l_ref[2].astype(F32)

    in_specs = [pl.BlockSpec((4, br, cols), lambda i: (0, i, 0))]
    args = [land]
    aliases = {}
    if prev is not None:
        in_specs.append(pl.BlockSpec(memory_space=pl.ANY))
        args.append(prev)
        aliases = {1: 0}
    main, extra = _call(
        body, name=name, grid=(rows // br,), args=args, in_specs=in_specs,
        out_specs=[pl.BlockSpec((None, br, cols), lambda i: (l, i, 0))],
        out_shape=[_sds((n_layers, rows, cols), F32)], aliases=aliases, sem=(ARB,), comm=comm)
    return main[0], extra


def _adamw_math(w, g, m, v):
    m = ADAM_B1 * m + (1.0 - ADAM_B1) * g
    v = ADAM_B2 * v + (1.0 - ADAM_B2) * (g * g)
    m_hat = m / (1.0 - ADAM_B1 ** ADAM_STEP)
    v_hat = v / (1.0 - ADAM_B2 ** ADAM_STEP)
    delta = -ADAM_LR * (m_hat / (jnp.sqrt(v_hat) + ADAM_EPS) + ADAM_WD * w)
    return delta, m, v


def _adamw(w, ga, gb, m, v, name):
    rows, cols = w.shape
    br = _block_rows(rows, cols)
    two = gb is not None

    def body(*refs):
        if two:
            w_ref, ga_ref, gb_ref, m_ref, v_ref, g_ref, d_ref, nm_ref, nv_ref = refs
            g = ga_ref[...] + gb_ref[...]
        else:
            w_ref, ga_ref, m_ref, v_ref, g_ref, d_ref, nm_ref, nv_ref = refs
            g = ga_ref[...]
        d, nm, nv = _adamw_math(w_ref[...], g, m_ref[...], v_ref[...])
        g_ref[...] = g
        d_ref[...] = d
        nm_ref[...] = nm
        nv_ref[...] = nv

    blk = pl.BlockSpec((br, cols), lambda i: (i, 0))
    args = [w, ga] + ([gb] if two else []) + [m, v]
    return pl.pallas_call(
        body, name=name, grid=(rows // br,),
        in_specs=[blk] * len(args), out_specs=[blk] * 4,
        out_shape=[_sds((rows, cols), F32)] * 4,
        compiler_params=_cp((PAR,)),
    )(*args)


class _CoreSwap:
    def __init__(self, srcs):
        self.srcs = list(srcs)
        n = len(self.srcs)
        self.out_shape = [_sds(s.shape, s.dtype) for s in self.srcs]
        self.scratch = [pltpu.SemaphoreType.DMA((n,)), pltpu.SemaphoreType.DMA((n,))]

    def _copies(self, src, dst, sems):
        x, y, c = _place()
        return [pltpu.make_async_remote_copy(
            src_ref=src[k], dst_ref=dst[k], send_sem=sems[0].at[k], recv_sem=sems[1].at[k],
            device_id=(x, y, 1 - c), device_id_type=MESH) for k in range(len(src))]

    def start(self, src, dst, sems):
        for cp in self._copies(src, dst, sems):
            cp.start()

    def wait(self, src, dst, sems):
        for cp in self._copies(src, dst, sems):
            cp.wait()


def _allreduce_small(v, name):
    rows = v.shape[0]
    flips = [(fx, fy, fc) for fx in (0, 1) for fy in (0, 1) for fc in (0, 1) if fx or fy or fc]

    def body(v_ref, o_ref, all_s, ssem, rsem):
        x, y, c = _place()

        def peer(f):
            return (x + f[0] - 2 * x * f[0], y + f[1] - 2 * y * f[1], c + f[2] - 2 * c * f[2])

        def slot(p):
            return all_s.at[4 * p[0] + 2 * p[1] + p[2]]

        def copy(k, f, owner):
            return pltpu.make_async_remote_copy(
                src_ref=v_ref, dst_ref=slot(owner), send_sem=ssem.at[k], recv_sem=rsem.at[k],
                device_id=peer(f), device_id_type=MESH)

        sends = [copy(k, f, (x, y, c)) for k, f in enumerate(flips)]
        for cp in sends:
            cp.start()
        all_s[4 * x + 2 * y + c] = v_ref[...]
        for k, f in enumerate(flips):
            copy(k, f, peer(f)).wait_recv()
        for cp in sends:
            cp.wait_send()
        acc = all_s[0]
        for d in range(1, 8):
            acc = acc + all_s[d]
        o_ref[...] = acc

    return pl.pallas_call(
        body, name=name,
        in_specs=[pl.BlockSpec(memory_space=pltpu.VMEM)],
        out_specs=pl.BlockSpec(memory_space=pltpu.VMEM),
        out_shape=_sds((rows, LANE), F32),
        scratch_shapes=[pltpu.VMEM((8, rows, LANE), F32), pltpu.SemaphoreType.DMA((7,)), pltpu.SemaphoreType.DMA((7,))],
    )(v)


BIG_NAMES = ("ffn1_w_gate", "ffn1_w_up", "ffn1_w_down", "w_in", "w_branch", "w_merge_gate", "w_out",
             "ffn2_w_gate", "ffn2_w_up", "ffn2_w_down")


FFN1 = ("ffn1_w_gate", "ffn1_w_up", "ffn1_w_down")
FFN2 = ("ffn2_w_gate", "ffn2_w_up", "ffn2_w_down")
MIX_IN = ("w_in", "w_merge_gate")
MIX_OUT = ("w_branch", "w_out")


def _keys(names, l):
    return [(n, l) for n in names]


def _local_step(x, tgt, small, convw_full, wx, n_layers):
    T, D = x.shape
    L = n_layers
    ns = N_SHARD
    dq = D // ns
    W = wx.w

    def hosted(call, keys, scatter=False):
        comm = wx.pieces(keys, scatter)
        main, extra = call(comm)
        if comm is not None:
            wx.arrived(keys, extra, scatter)
        return main

    def mixer_views(l):
        g4 = W[("w_merge_gate", l)]
        gates = jnp.transpose(g4, (0, 2, 1, 3)).reshape(D, 3 * D)
        win = jnp.transpose(W[("w_in", l)], (1, 0, 2)).reshape(D, -1)
        return jnp.concatenate([gates, _permute_in_cols(win)], axis=-1)

    def out_views(l):
        wb = jnp.transpose(W[("w_branch", l)], (1, 2, 0, 3)).reshape(3, BRANCH_W, D)
        wo = W[("w_out", l)].reshape(D, D)
        return wb, wo

    tb = _ret_tables(T)
    rb_pad = jnp.pad(small["rel_bias"], ((0, 0), (0, 0), (0, RB_PAD - N_REL)))

    saved = []
    h = x
    for l in range(L):
        s = {"x0": h}
        nxt = l + 1
        x1, s["g1"], s["u1"] = hosted(
            lambda c: _ffn_fwd(h, small["ffn1_norm"][l][None], W[("ffn1_w_gate", l)], W[("ffn1_w_up", l)],
                               W[("ffn1_w_down", l)], f"ffn1_fwd_{l}", comm=c), _keys(MIX_IN, l))
        s["x1"] = x1
        s["wbig"] = mixer_views(l)
        big, s["h"] = _inproj_fwd(x1, small["mix_norm"][l][None], s["wbig"], f"inproj_fwd_{l}")
        s["big"] = big
        s["bias"] = jnp.transpose(_relbias_expand(rb_pad[l], f"relbias_expand_{l}"), (1, 0, 2))
        s["yc"] = _conv_fwd(big, convw_full[l], f"conv_fwd_{l}")
        s["yr"], s["o"], s["st"] = hosted(lambda c: _ret_fwd(big, tb, f"ret_fwd_{l}", comm=c), _keys(MIX_OUT, l))
        (s["ya"],) = hosted(lambda c: _att_fwd(big, s["bias"], f"att_fwd_{l}", comm=c), _keys(FFN2, l))
        s["wb"], s["wo"] = out_views(l)
        x2, s["p"], s["mg"] = _merge_fwd(x1, big, (s["yc"], s["yr"], s["ya"]), s["wb"], s["wo"], f"merge_fwd_{l}")
        s["x2"] = x2
        h, s["g2"], s["u2"] = hosted(
            lambda c: _ffn_fwd(x2, small["ffn2_norm"][l][None], W[("ffn2_w_gate", l)], W[("ffn2_w_up", l)],
                               W[("ffn2_w_down", l)], f"ffn2_fwd_{l}", comm=c), _keys(FFN1, nxt) if nxt < L else [])
        saved.append(s)

    loss_p, dx, d_final = _loss_head(h, tgt, small["final_norm"][None], "loss_head")

    gs = {"final_norm": d_final[0]}
    for k in ("ffn1_norm", "mix_norm", "ffn2_norm", "rel_bias", "conv_w"):
        gs[k] = [None] * L
    tk = min(2048, T)
    nk = T // tk

    def ffn_back(pre, l, dxo, x_in, g, u, first_keys, second_keys):
        nw = small[pre + "_norm"][l][None]
        dgv, duv, av, hb, dacc = hosted(
            lambda c: _ffn_bwd_hidden(dxo, x_in, nw, g, u, W[(pre + "_w_down", l)], f"{pre}_bwd_hidden_{l}", comm=c),
            first_keys, scatter=True)
        dxn, dn = hosted(
            lambda c: _ffn_bwd_resid(dgv, duv, W[(pre + "_w_gate", l)], W[(pre + "_w_up", l)], x_in, nw, dxo,
                                     f"{pre}_bwd_resid_{l}", comm=c),
            second_keys, scatter=True)
        gs[pre + "_norm"][l] = dn[0]
        return dxn, (hb, dgv, duv, av, dacc)

    def ffn_grads(pre, l, hb, dgv, duv, av, dacc, chain=False, carry=()):
        fs = dgv.shape[-1]
        hspec = pl.BlockSpec((tk, D), lambda p, q, k: (k, 0))
        sspec = pl.BlockSpec((None, tk, fs), lambda p, q, k: (p, k, 0))
        down_spec = pl.BlockSpec((None, fs, D), lambda p, q, k: (p, 0, 0))
        jobs = [(pre + "_w_gate", dgv, hb, sspec, hspec, (ns, fs, D), down_spec),
                (pre + "_w_up", duv, hb, sspec, hspec, (ns, fs, D), down_spec),
                (pre + "_w_down", av, dacc, sspec, hspec, (ns, fs, D), down_spec)]
        before = None
        for nm, a, b, a_spec, b_spec, shape, o_spec in jobs:
            def product(c):
                r = _tn(a, b, a_spec, b_spec, _sds(shape, BF), o_spec, (ns, 1, nk), f"d{nm}_{l}", comm=c)
                return (r, []) if c is None else r
            if before is None:
                keys = list(carry)
            else:
                keys = [before] if chain else []
            wx.g[(nm, l)] = hosted(product, keys, scatter=True)
            before = (nm, l)

    for l in reversed(range(L)):
        s = saved[l]
        above = _keys(FFN1, l + 1) if l + 1 < L else [None] * 3
        dx, parts = ffn_back("ffn2", l, dx, s["x2"], s["g2"], s["u2"], [k for k in above[:1] if k],
                             [k for k in above[1:2] if k])
        ffn_grads("ffn2", l, *parts, carry=[k for k in above[2:] if k])
        dp, dbig, dyc, dyr, dya, dxb = _merge_bwd(dx, s["big"], s["p"], s["wb"], s["wo"], f"merge_bwd_{l}")
        wx.g[("w_out", l)] = _tn(
            s["mg"], dxb, pl.BlockSpec((tk, dq), lambda p, q, k: (k, p)), pl.BlockSpec((tk, D), lambda p, q, k: (k, 0)),
            _sds((ns, dq, D), BF), pl.BlockSpec((None, dq, D), lambda p, q, k: (p, 0, 0)), (ns, 1, nk), f"dw_out_{l}")
        gb = None
        for i, yv in enumerate((s["yc"], s["yr"], s["ya"])):
            gb = _tn(yv, dp,
                     pl.BlockSpec((tk, BRANCH_W), lambda p, q, k: (k, 0)),
                     pl.BlockSpec((tk, dq), lambda p, q, k, i=i: (k, i * ns + p)),
                     _sds((ns, 3, BRANCH_W, dq), BF),
                     pl.BlockSpec((None, None, BRANCH_W, dq), lambda p, q, k, i=i: (p, i, 0, 0)),
                     (ns, 1, nk), f"dw_branch{i}_{l}", prev=gb)
        wx.g[("w_branch", l)] = gb
        dbig, dcw = _conv_bwd(s["big"], dyc, convw_full[l], dbig, f"conv_bwd_{l}")
        gs["conv_w"][l] = dcw
        dbig = _ret_bwd(s["big"], s["o"], s["st"], dyr, tb, dbig, f"ret_bwd_{l}")
        dbig, dbias = hosted(lambda c: _att_bwd(s["big"], s["bias"], dya, dbig, f"att_bwd_{l}", comm=c),
                             _keys(FFN2, l), scatter=True)
        gs["rel_bias"][l] = _relbias_grad(jnp.transpose(dbias, (1, 0, 2)), f"relbias_grad_{l}")[:, :N_REL]
        n_in = N_SEG * BRANCH_W
        bn = 1024 if (3 * D) % 1024 == 0 else BRANCH_W
        dwp = _tn(s["h"], dbig, pl.BlockSpec((tk, D), lambda p, q, k: (k, 0)),
                  pl.BlockSpec((tk, bn), lambda p, q, k: (k, 3 * D // bn + q)),
                  _sds((D, n_in), BF), pl.BlockSpec((D, bn), lambda p, q, k: (0, q)), (1, n_in // bn, nk), f"dw_in_{l}")
        wx.g[("w_in", l)] = jnp.transpose(_unpermute_in_cols(dwp).reshape(D, ns, n_in // ns), (1, 0, 2))
        wx.g[("w_merge_gate", l)] = _tn_gates(s["h"], dbig, ns, tk, f"dw_merge_gate_{l}")
        dx, dn = hosted(
            lambda c: _inproj_bwd(dbig, s["wbig"], s["x1"], small["mix_norm"][l][None], dx, f"inproj_bwd_{l}", comm=c),
            [("w_in", l)], scatter=True)
        gs["mix_norm"][l] = dn[0]
        dx, parts = ffn_back("ffn1", l, dx, s["x0"], s["g1"], s["u1"],
                             [("w_merge_gate", l), ("w_branch", l), ("w_out", l)], [])
        ffn_grads("ffn1", l, *parts, chain=(l == 0))

    for k in ("ffn1_norm", "mix_norm", "ffn2_norm", "rel_bias", "conv_w"):
        gs[k] = jnp.stack(gs[k])
    return loss_p, dx, gs


class _Exchange:
    def __init__(self, shards):
        self.shards = shards
        self.w = {}
        self.g = {}
        self.landed = {}

    def own(self, key):
        return self.shards[key[0]][key[1]].astype(BF)

    def pieces(self, keys, scatter):
        if not keys:
            return None
        if scatter:
            return _Pieces([self.g[k] for k in keys], True)
        return _HalfGather([_halves(self.own(k)) for k in keys])

    def arrived(self, keys, outs, scatter):
        for k, o in zip(keys, outs):
            if scatter:
                self.landed[k] = o
            else:
                self.w[k] = o.reshape((N_SHARD,) + self.shards[k[0]].shape[1:])


def _halves(a):
    return a.reshape(2, -1, a.shape[-1])


TRANSPOSED_GRADS = ("ffn1_w_gate", "ffn1_w_up", "ffn2_w_gate", "ffn2_w_up")
W_NAMES = ("ffn1_norm", "ffn1_w_gate", "ffn1_w_up", "ffn1_w_down", "mix_norm", "w_in", "conv_w", "rel_bias", "w_branch",
           "w_merge_gate", "w_out", "ffn2_norm", "ffn2_w_gate", "ffn2_w_up", "ffn2_w_down", "final_norm")


def _as2d(a):
    return a.reshape(1, -1) if a.ndim == 1 else a.reshape(-1, a.shape[-1])


def kernel(x, ffn1_norm, ffn1_w_gate, ffn1_w_up, ffn1_w_down, mix_norm, w_in, conv_w, rel_bias, w_branch, w_merge_gate, w_out, ffn2_norm, ffn2_w_gate, ffn2_w_up, ffn2_w_down, final_norm, loss_target, m_ffn1_norm, m_ffn1_w_gate, m_ffn1_w_up, m_ffn1_w_down, m_mix_norm, m_w_in, m_conv_w, m_rel_bias, m_w_branch, m_w_merge_gate, m_w_out, m_ffn2_norm, m_ffn2_w_gate, m_ffn2_w_up, m_ffn2_w_down, m_final_norm, v_ffn1_norm, v_ffn1_w_gate, v_ffn1_w_up, v_ffn1_w_down, v_mix_norm, v_w_in, v_conv_w, v_rel_bias, v_w_branch, v_w_merge_gate, v_w_out, v_ffn2_norm, v_ffn2_w_gate, v_ffn2_w_up, v_ffn2_w_down, v_final_norm):
    given = dict(locals())
    w = {n: given[n] for n in W_NAMES}
    m = {n: given["m_" + n] for n in W_NAMES}
    v = {n: given["v_" + n] for n in W_NAMES}
    my_chip = 2 * lax.axis_index("x") + lax.axis_index("y")
    L = w_in.shape[0]

    wx = _Exchange({n: jnp.swapaxes(w[n], 1, 2) if n in TRANSPOSED_GRADS else w[n] for n in BIG_NAMES})
    first = _keys(FFN1, 0)
    got = _comm_alone(_HalfGather([_halves(wx.own(k)) for k in first] + [_halves(conv_w)]), "gather_first")
    wx.arrived(first, got[:-1], False)
    convw_full = jnp.transpose(got[-1].reshape((N_SHARD,) + conv_w.shape), (1, 2, 0, 3)).reshape(
        conv_w.shape[0], conv_w.shape[1], -1)

    small = {n: w[n] for n in ("ffn1_norm", "mix_norm", "ffn2_norm", "final_norm", "rel_bias")}
    loss_p, grad_x, gs = _local_step(x[0], loss_target[0], small, convw_full, wx, L)
    last = [(FFN1[-1], 0)]
    wx.arrived(last, _comm_alone(wx.pieces(last, True), "scatter_last"), True)

    sums, others = [], []
    for n in BIG_NAMES:
        acc = None
        for l in range(L):
            a = wx.landed[(n, l)]
            swap = _CoreSwap(sums[-1:]) if sums and l == L - 1 else None
            acc, came = _sum4(a.reshape(4, -1, a.shape[-1]), l, L, f"sum4_{n}_{l}", prev=acc, comm=swap)
            others += came
        sums.append(acc.reshape(-1, acc.shape[-1]))
    others += _comm_alone(_CoreSwap(sums[-1:]), "swap_last")

    parts = [gs["ffn1_norm"].reshape(-1), gs["mix_norm"].reshape(-1), gs["ffn2_norm"].reshape(-1),
             gs["final_norm"].reshape(-1), gs["rel_bias"].reshape(-1), gs["conv_w"].reshape(-1), loss_p[0]]
    sizes = [p.shape[0] for p in parts]
    flat = jnp.concatenate(parts)
    rows = -(-flat.shape[0] // (8 * LANE)) * 8
    flat = jnp.pad(flat, (0, rows * LANE - flat.shape[0])).reshape(rows, LANE)
    red = _allreduce_small(flat, "allreduce_small").reshape(-1)
    offs = [0]
    for sz in sizes:
        offs.append(offs[-1] + sz)
    sm = {}
    for i, n in enumerate(("ffn1_norm", "mix_norm", "ffn2_norm", "final_norm", "rel_bias", "conv_w")):
        sm[n] = red[offs[i]:offs[i + 1]]
    loss = red[offs[6]]
    sm["conv_w"] = lax.dynamic_slice_in_dim(sm["conv_w"].reshape(conv_w.shape[0], conv_w.shape[1], -1),
                                            my_chip * conv_w.shape[2], conv_w.shape[2], axis=2)

    grads, deltas, new_m, new_v = {}, {}, {}, {}
    big_sum = dict(zip(BIG_NAMES, zip(sums, others)))
    for n in W_NAMES:
        flip = n in TRANSPOSED_GRADS

        def view(a):
            return jnp.swapaxes(a, 1, 2) if flip else a

        shape = view(w[n]).shape
        if n in big_sum:
            ga, gb = big_sum[n]
        else:
            ga, gb = _as2d(sm[n].reshape(shape)), None
        out = _adamw(_as2d(view(w[n])), ga, gb, _as2d(view(m[n])), _as2d(view(v[n])), f"adamw_{n}")
        grads[n], deltas[n], new_m[n], new_v[n] = (view(o.reshape(shape)) for o in out)

    return (loss, grad_x[None], *[grads[n] for n in W_NAMES], *[deltas[n] for n in W_NAMES],
            *[new_m[n] for n in W_NAMES], *[new_v[n] for n in W_NAMES])
```

```python
import functools
import math

import jax
import jax.numpy as jnp
from jax import lax
from jax.experimental import pallas as pl
from jax.experimental.pallas import tpu as pltpu

F32 = jnp.float32
BF = jnp.bfloat16
MESH = pl.DeviceIdType.MESH
ARB = "arbitrary"
PAR = "parallel"

EPS = 1e-6
NEG_INF = -1e30
ROPE_BASE = 10000.0
CHUNK = 64
BRANCH_W = 512
H_RET = 4
DK_RET = 128
H_ATT = 8
DH_ATT = 64
N_PREV = 8
REL_CLIP = 128
N_REL = 2 * REL_CLIP + 1
N_SHARD = 4
LANE = 128
RET_L = 512
ATT_TQ = 128
ATT_SUB = 4
ATT_PAD = N_PREV * CHUNK
ATT_SPAN = ATT_TQ + ATT_PAD
ATT_TOEP = 2 * REL_CLIP
RB_PAD = 264
TM = 512
TM_FFN = 1024

ADAM_LR = 0.001
ADAM_B1 = 0.9
ADAM_B2 = 0.999
ADAM_EPS = 1e-08
ADAM_WD = 0.01
ADAM_STEP = 10

NT_DIMS = (((1,), (1,)), ((), ()))
TN_DIMS = (((0,), (0,)), ((), ()))


def _cp(sem, vmem_mb=48):
    return pltpu.CompilerParams(dimension_semantics=sem, vmem_limit_bytes=vmem_mb << 20)


def _sds(shape, dtype):
    return jax.ShapeDtypeStruct(tuple(shape), dtype)


def _rms_r(x):
    return lax.rsqrt(jnp.mean(x * x, axis=-1, keepdims=True) + EPS)


def _sigmoid(x):
    return 0.5 * jnp.tanh(0.5 * x) + 0.5


def _rms_bwd(dh, xv, nw):
    r = _rms_r(xv)
    xh = xv * r
    dxh = dh * nw
    dx = r * (dxh - xh * jnp.mean(dxh * xh, axis=-1, keepdims=True))
    return dx, jnp.sum(dh * xh, axis=0, keepdims=True)


def _place():
    return lax.axis_index("x"), lax.axis_index("y"), lax.axis_index("c")


def _other_chips(x, y):
    return [(1 - x, y), (x, 1 - y), (1 - x, 1 - y)]


class _Pieces:
    def __init__(self, srcs, scatter):
        self.srcs = list(srcs)
        self.scatter = scatter
        n = len(self.srcs)
        self.out_shape = [_sds(s.shape if scatter else (N_SHARD,) + s.shape, s.dtype) for s in self.srcs]
        self.scratch = [pltpu.SemaphoreType.DMA((n,)), pltpu.SemaphoreType.DMA((3, n)), pltpu.SemaphoreType.DMA((3, n))]

    def _copies(self, src, dst, sems, waiting):
        lsem, ssem, rsem = sems
        x, y, c = _place()
        mine = 2 * x + y
        n = len(src)

        def remote(j, k, chip, s_ref, d_ref):
            return pltpu.make_async_remote_copy(
                src_ref=s_ref, dst_ref=d_ref, send_sem=ssem.at[j, k], recv_sem=rsem.at[j, k],
                device_id=(chip[0], chip[1], c), device_id_type=MESH)

        chips = list(enumerate(_other_chips(x, y)))
        if self.scatter:
            local = [pltpu.make_async_copy(src[k].at[mine], dst[k].at[3], lsem.at[k]) for k in range(n)]
            sends = [remote(j, k, ch, src[k].at[2 * ch[0] + ch[1]], dst[k].at[j]) for j, ch in chips for k in range(n)]
            recvs = sends
        else:
            local = [pltpu.make_async_copy(src[k], dst[k].at[mine], lsem.at[k]) for k in range(n)]
            sends = [remote(j, k, ch, src[k], dst[k].at[mine]) for j, ch in chips for k in range(n)]
            recvs = [remote(j, k, ch, src[k], dst[k].at[2 * ch[0] + ch[1]]) for j, ch in chips for k in range(n)
                     ] if waiting else []
        return local, sends, recvs

    def start(self, src, dst, sems):
        local, sends, _ = self._copies(src, dst, sems, False)
        for cp in local + sends:
            cp.start()

    def wait(self, src, dst, sems):
        local, sends, recvs = self._copies(src, dst, sems, True)
        for cp in recvs:
            cp.wait_recv()
        for cp in sends:
            cp.wait_send()
        for cp in local:
            cp.wait()


class _HalfGather:
    def __init__(self, srcs):
        self.srcs = list(srcs)
        n = len(self.srcs)
        self.out_shape = [_sds((N_SHARD,) + s.shape, s.dtype) for s in self.srcs]
        self.scratch = [pltpu.SemaphoreType.DMA((n,))] + [pltpu.SemaphoreType.DMA((3, n)) for _ in range(4)]

    def _plan(self, src, dst, sems, want):
        lsem, s1, r1, s2, r2 = sems
        x, y, c = _place()
        mine = 2 * x + y
        n = len(src)
        chips = [(j, ch, 2 * ch[0] + ch[1]) for j, ch in enumerate(_other_chips(x, y))]

        def copy(s_ref, d_ref, ssem, rsem, to):
            return pltpu.make_async_remote_copy(src_ref=s_ref, dst_ref=d_ref, send_sem=ssem, recv_sem=rsem,
                                                device_id=to, device_id_type=MESH)

        def over(kind, make):
            return [make(j, ch, slot, k) for j, ch, slot in chips for k in range(n)] if kind in want else []

        local = [pltpu.make_async_copy(src[k], dst[k].at[mine], lsem.at[k]) for k in range(n)] if "local" in want else []
        sends = over("sends", lambda j, ch, slot, k: copy(src[k].at[c], dst[k].at[mine, c], s1.at[j, k], r1.at[j, k],
                                                          (ch[0], ch[1], c)))
        lands = over("lands", lambda j, ch, slot, k: copy(src[k].at[c], dst[k].at[slot, c], s1.at[j, k], r1.at[j, k],
                                                          (ch[0], ch[1], c)))
        passes = over("passes", lambda j, ch, slot, k: copy(dst[k].at[slot, c], dst[k].at[slot, c], s2.at[j, k],
                                                            r2.at[j, k], (x, y, 1 - c)))
        gets = over("gets", lambda j, ch, slot, k: copy(dst[k].at[slot, 1 - c], dst[k].at[slot, 1 - c], s2.at[j, k],
                                                        r2.at[j, k], (x, y, 1 - c)))
        return local, sends, lands, passes, gets

    def start(self, src, dst, sems):
        lsem, s1, r1, s2, r2 = sems
        x, y, c = _place()
        mine = 2 * x + y
        for k in range(len(src)):
            pltpu.make_async_copy(src[k], dst[k].at[mine], lsem.at[k]).start()
        for j, ch in enumerate(_other_chips(x, y)):
            for k in range(len(src)):
                pltpu.make_async_remote_copy(
                    src_ref=src[k].at[c], dst_ref=dst[k].at[mine, c], send_sem=s1.at[j, k], recv_sem=r1.at[j, k],
                    device_id=(ch[0], ch[1], c), device_id_type=MESH).start()

    def relay(self, src, dst, sems):
        _, _, lands, passes, _ = self._plan(src, dst, sems, ("lands", "passes"))
        for land, fwd in zip(lands, passes):
            land.wait_recv()
            fwd.start()

    def finish(self, src, dst, sems):
        local, sends, _, passes, gets = self._plan(src, dst, sems, ("local", "sends", "passes", "gets"))
        for cp in gets:
            cp.wait_recv()
        for cp in sends + passes:
            cp.wait_send()
        for cp in local:
            cp.wait()

    def wait(self, src, dst, sems):
        self.relay(src, dst, sems)
        self.finish(src, dst, sems)


def _call(body, *, name, args, in_specs, out_specs, out_shape, grid=(), scratch_shapes=(), sem=None, comm=None,
          aliases=None, vmem_mb=48):
    in_specs, out_specs, out_shape = list(in_specs), list(out_specs), list(out_shape)
    scratch, args = list(scratch_shapes), list(args)
    n_in, n_out, n_scr = len(in_specs), len(out_specs), len(scratch)
    if comm is None:
        def kernel_body(*refs):
            body(*refs)
    else:
        c_in, c_out = len(comm.srcs), len(comm.out_shape)

        def kernel_body(*refs):
            o0 = n_in + c_in
            s0 = o0 + n_out + c_out
            cin, cout, sems = refs[n_in:o0], refs[o0 + n_out:s0], refs[s0 + n_scr:]
            main = refs[:n_in] + refs[o0:o0 + n_out] + refs[s0:s0 + n_scr]
            if grid:
                ids = [pl.program_id(a) for a in range(len(grid))]
                first = functools.reduce(lambda p, q: p & q, [i == 0 for i in ids])
                last = functools.reduce(lambda p, q: p & q, [i == g - 1 for i, g in zip(ids, grid)])

                @pl.when(first)
                def _():
                    comm.start(cin, cout, sems)

                body(*main)

                steps = math.prod(grid)
                if hasattr(comm, "relay") and steps >= 4:
                    flat = functools.reduce(lambda p, q: p + q, [i * math.prod(grid[a + 1:]) for a, i in enumerate(ids)])

                    @pl.when(flat == (2 * steps) // 3)
                    def _():
                        comm.relay(cin, cout, sems)

                    @pl.when(last)
                    def _():
                        comm.finish(cin, cout, sems)
                else:
                    @pl.when(last)
                    def _():
                        comm.wait(cin, cout, sems)
            else:
                comm.start(cin, cout, sems)
                body(*main)
                comm.wait(cin, cout, sems)

        hbm = pl.BlockSpec(memory_space=pl.ANY)
        in_specs += [hbm] * c_in
        out_specs += [hbm] * c_out
        out_shape += comm.out_shape
        scratch += comm.scratch
        args += comm.srcs
    params = dict(vmem_limit_bytes=vmem_mb << 20)
    if grid:
        params["dimension_semantics"] = sem
    outs = pl.pallas_call(
        kernel_body, name=name, grid=grid, in_specs=in_specs, out_specs=out_specs, out_shape=out_shape,
        scratch_shapes=scratch, input_output_aliases=aliases or {}, compiler_params=pltpu.CompilerParams(**params),
    )(*args)
    return list(outs[:n_out]), list(outs[n_out:])


def _comm_alone(comm, name):
    return _call(lambda: None, name=name, args=[], in_specs=[], out_specs=[], out_shape=[], comm=comm)[1]


def _ffn_fwd(x, nw, wg, wu, wd, name, comm=None):
    T, D = x.shape
    ns, fs, _ = wg.shape
    tm = min(TM_FFN, T)

    def body(x_ref, nw_ref, wg_ref, wu_ref, wd_ref, xo_ref, g_ref, u_ref, h_s, acc_s):
        j = pl.program_id(1)

        @pl.when(j == 0)
        def _():
            xv = x_ref[...]
            h_s[...] = (xv * _rms_r(xv) * nw_ref[...]).astype(BF)
            acc_s[...] = jnp.zeros_like(acc_s)

        h = h_s[...]
        gb = lax.dot_general(h, wg_ref[...], NT_DIMS, preferred_element_type=F32).astype(BF)
        ub = lax.dot_general(h, wu_ref[...], NT_DIMS, preferred_element_type=F32).astype(BF)
        g_ref[...] = gb
        u_ref[...] = ub
        g = gb.astype(F32)
        a = (g * _sigmoid(g) * ub.astype(F32)).astype(BF)
        acc_s[...] += jnp.dot(a, wd_ref[...], preferred_element_type=F32)

        @pl.when(j == ns - 1)
        def _():
            xo_ref[...] = x_ref[...] + 0.5 * acc_s[...]

    wspec = pl.BlockSpec((None, fs, D), lambda i, j: (j, 0, 0))
    return _call(
        body, name=name, grid=(T // tm, ns), args=(x, nw, wg, wu, wd), comm=comm, vmem_mb=56,
        in_specs=[pl.BlockSpec((tm, D), lambda i, j: (i, 0)),
                  pl.BlockSpec((1, D), lambda i, j: (0, 0)),
                  wspec, wspec,
                  pl.BlockSpec((None, fs, D), lambda i, j: (j, 0, 0))],
        out_specs=[pl.BlockSpec((tm, D), lambda i, j: (i, 0)),
                   pl.BlockSpec((None, tm, fs), lambda i, j: (j, i, 0)),
                   pl.BlockSpec((None, tm, fs), lambda i, j: (j, i, 0))],
        out_shape=[_sds((T, D), F32), _sds((ns, T, fs), BF), _sds((ns, T, fs), BF)],
        scratch_shapes=[pltpu.VMEM((tm, D), BF), pltpu.VMEM((tm, D), F32)],
        sem=(ARB, ARB))


def _ffn_bwd_hidden(dxo, x, nw, g, u, wd, name, comm=None):
    T, D = x.shape
    ns, fs, _ = wd.shape
    tm = min(TM_FFN, T)

    def body(dxo_ref, x_ref, nw_ref, g_ref, u_ref, wd_ref, dg_ref, du_ref, a_ref, h_ref, dacc_ref, dacc_s):
        @pl.when(pl.program_id(1) == 0)
        def _():
            xv = x_ref[...]
            h_ref[...] = (xv * _rms_r(xv) * nw_ref[...]).astype(BF)
            db = (0.5 * dxo_ref[...]).astype(BF)
            dacc_ref[...] = db
            dacc_s[...] = db

        da = lax.dot_general(dacc_s[...], wd_ref[...], NT_DIMS, preferred_element_type=F32)
        gv = g_ref[...].astype(F32)
        uv = u_ref[...].astype(F32)
        s = _sigmoid(gv)
        sg = gv * s
        a_ref[...] = (sg * uv).astype(BF)
        du_ref[...] = (da * sg).astype(BF)
        dg_ref[...] = (da * uv * (s * (1.0 + gv * (1.0 - s)))).astype(BF)

    tok = pl.BlockSpec((tm, D), lambda i, j: (i, 0))
    hid = pl.BlockSpec((None, tm, fs), lambda i, j: (j, i, 0))
    return _call(
        body, name=name, grid=(T // tm, ns), args=(dxo, x, nw, g, u, wd), comm=comm, vmem_mb=56,
        in_specs=[tok, tok, pl.BlockSpec((1, D), lambda i, j: (0, 0)), hid, hid,
                  pl.BlockSpec((None, fs, D), lambda i, j: (j, 0, 0))],
        out_specs=[hid, hid, hid, tok, tok],
        out_shape=[_sds((ns, T, fs), BF)] * 3 + [_sds((T, D), BF)] * 2,
        scratch_shapes=[pltpu.VMEM((tm, D), BF)],
        sem=(ARB, ARB))


def _ffn_bwd_resid(dg, du, wg, wu, x, nw, dxo, name, comm=None):
    T, D = x.shape
    ns, fs, _ = wg.shape
    tm = min(TM_FFN, T)

    def body(dg_ref, du_ref, wg_ref, wu_ref, x_ref, nw_ref, dxo_ref, dx_ref, dnw_ref, acc_s):
        i = pl.program_id(0)
        j = pl.program_id(1)
        prod = (jnp.dot(dg_ref[...], wg_ref[...], preferred_element_type=F32)
                + jnp.dot(du_ref[...], wu_ref[...], preferred_element_type=F32))

        @pl.when((i == 0) & (j == 0))
        def _():
            dnw_ref[...] = jnp.zeros_like(dnw_ref)

        @pl.when(j == 0)
        def _():
            acc_s[...] = prod

        @pl.when(j > 0)
        def _():
            acc_s[...] += prod

        @pl.when(j == ns - 1)
        def _():
            dx, dn = _rms_bwd(acc_s[...], x_ref[...], nw_ref[...])
            dx_ref[...] = dxo_ref[...] + dx
            dnw_ref[...] += dn

    tok = pl.BlockSpec((tm, D), lambda i, j: (i, 0))
    row = pl.BlockSpec((1, D), lambda i, j: (0, 0))
    hid = pl.BlockSpec((None, tm, fs), lambda i, j: (j, i, 0))
    wspec = pl.BlockSpec((None, fs, D), lambda i, j: (j, 0, 0))
    return _call(
        body, name=name, grid=(T // tm, ns), args=(dg, du, wg, wu, x, nw, dxo), comm=comm, vmem_mb=56,
        in_specs=[hid, hid, wspec, wspec, tok, row, tok],
        out_specs=[tok, row],
        out_shape=[_sds((T, D), F32), _sds((1, D), F32)],
        scratch_shapes=[pltpu.VMEM((tm, D), F32)],
        sem=(ARB, ARB))


def _tn(a, b, a_spec, b_spec, out_shape, out_spec, grid, name, prev=None, comm=None):
    nk = grid[-1]
    acc_shape = tuple(d for d in out_spec.block_shape if d is not None)

    def body(*refs):
        a_ref, b_ref = refs[0], refs[1]
        o_ref, acc = refs[-2], refs[-1]
        k = pl.program_id(2)
        prod = lax.dot_general(a_ref[...], b_ref[...], TN_DIMS, preferred_element_type=F32)

        @pl.when(k == 0)
        def _():
            acc[...] = prod

        @pl.when(k > 0)
        def _():
            acc[...] += prod

        @pl.when(k == nk - 1)
        def _():
            o_ref[...] = acc[...].astype(o_ref.dtype)

    in_specs = [a_spec, b_spec]
    args = [a, b]
    aliases = {}
    if prev is not None:
        in_specs.append(pl.BlockSpec(memory_space=pl.ANY))
        args.append(prev)
        aliases = {2: 0}
    main, extra = _call(
        body, name=name, grid=grid, args=args, in_specs=in_specs, out_specs=[out_spec], out_shape=[out_shape],
        scratch_shapes=[pltpu.VMEM(acc_shape, F32)], aliases=aliases, sem=(ARB, ARB, ARB), comm=comm)
    return main[0] if comm is None else (main[0], extra)


def _tn_gates(h, dbig, ns, tk, name):
    T, D = h.shape
    dq = D // ns
    nk = T // tk

    def body(a_ref, b_ref, o_ref, acc):
        k = pl.program_id(1)
        prod = lax.dot_general(a_ref[...], b_ref[...], TN_DIMS, preferred_element_type=F32)

        @pl.when(k == 0)
        def _():
            acc[...] = prod

        @pl.when(k > 0)
        def _():
            acc[...] += prod

        @pl.when(k == nk - 1)
        def _():
            for s in range(ns):
                o_ref[s] = acc[s * dq:(s + 1) * dq, :].astype(o_ref.dtype)

    return pl.pallas_call(
        body, name=name, grid=(3, nk),
        in_specs=[pl.BlockSpec((tk, D), lambda q, k: (k, 0)), pl.BlockSpec((tk, D), lambda q, k: (k, q))],
        out_specs=pl.BlockSpec((ns, None, dq, D), lambda q, k: (0, q, 0, 0)),
        out_shape=_sds((ns, 3, dq, D), BF),
        scratch_shapes=[pltpu.VMEM((D, D), F32)],
        compiler_params=_cp((PAR, ARB)),
    )(h, dbig)


def _inproj_fwd(x, nw, wbig, name):
    T, D = x.shape
    nb = wbig.shape[-1]
    tm = min(2 * TM, T)
    bn = min(2048, nb)

    def body(x_ref, nw_ref, w_ref, o_ref, h_ref, h_s):
        @pl.when(pl.program_id(1) == 0)
        def _():
            xv = x_ref[...]
            hb = (xv * _rms_r(xv) * nw_ref[...]).astype(BF)
            h_s[...] = hb
            h_ref[...] = hb

        o_ref[...] = jnp.dot(h_s[...], w_ref[...], preferred_element_type=F32).astype(BF)

    return pl.pallas_call(
        body, name=name, grid=(T // tm, nb // bn),
        in_specs=[pl.BlockSpec((tm, D), lambda i, n: (i, 0)),
                  pl.BlockSpec((1, D), lambda i, n: (0, 0)),
                  pl.BlockSpec((D, bn), lambda i, n: (0, n))],
        out_specs=[pl.BlockSpec((tm, bn), lambda i, n: (i, n)),
                   pl.BlockSpec((tm, D), lambda i, n: (i, 0))],
        out_shape=[_sds((T, nb), BF), _sds((T, D), BF)],
        scratch_shapes=[pltpu.VMEM((tm, D), BF)],
        compiler_params=_cp((PAR, ARB)),
    )(x, nw, wbig)


def _inproj_bwd(dbig, wbig, x, nw, dxin, name, comm=None):
    T, D = x.shape
    nb = wbig.shape[-1]
    tm = min(TM_FFN, T)
    tk = min(2048, nb)
    nk = nb // tk

    def body(a_ref, w_ref, x_ref, nw_ref, dxin_ref, dx_ref, dnw_ref, acc_s):
        i = pl.program_id(0)
        k = pl.program_id(1)
        prod = lax.dot_general(a_ref[...], w_ref[...], NT_DIMS, preferred_element_type=F32)

        @pl.when((i == 0) & (k == 0))
        def _():
            dnw_ref[...] = jnp.zeros_like(dnw_ref)

        @pl.when(k == 0)
        def _():
            acc_s[...] = prod

        @pl.when(k > 0)
        def _():
            acc_s[...] += prod

        @pl.when(k == nk - 1)
        def _():
            dx, dn = _rms_bwd(acc_s[...], x_ref[...], nw_ref[...])
            dx_ref[...] = dxin_ref[...] + dx
            dnw_ref[...] += dn

    tok = pl.BlockSpec((tm, D), lambda i, k: (i, 0))
    row = pl.BlockSpec((1, D), lambda i, k: (0, 0))
    return _call(
        body, name=name, grid=(T // tm, nk), args=(dbig, wbig, x, nw, dxin), comm=comm, vmem_mb=56,
        in_specs=[pl.BlockSpec((tm, tk), lambda i, k: (i, k)),
                  pl.BlockSpec((D, tk), lambda i, k: (0, k)),
                  tok, row, tok],
        out_specs=[tok, row],
        out_shape=[_sds((T, D), F32), _sds((1, D), F32)],
        scratch_shapes=[pltpu.VMEM((tm, D), F32)],
        sem=(ARB, ARB))


CONV_R = 512
CONV_BASE, CONV_GROUP = 0, 3
ATT_BASE, ATT_GROUP = 12, 3
RET_BASE, RET_GROUP = 24, 4
N_SEG = 10


N_IN_BLOCKS = N_SEG * BRANCH_W // LANE


def _orig_block(p):
    nblk = BRANCH_W // LANE
    qa, qr = p - ATT_BASE, p - RET_BASE
    conv = (p % CONV_GROUP) * nblk + p // CONV_GROUP
    att = (7 + qa % ATT_GROUP) * nblk + qa // ATT_GROUP
    ret = (3 + qr % RET_GROUP) * nblk + qr // RET_GROUP
    return jnp.where(p < ATT_BASE, conv, jnp.where(p < RET_BASE, att, ret))


def _copy_blocks(src, in_spec, out_shape, out_spec, grid, name, prev=None):
    def body(*refs):
        refs[-1][...] = refs[0][...]

    in_specs, args, aliases = [in_spec], [src], {}
    if prev is not None:
        in_specs.append(pl.BlockSpec(memory_space=pl.ANY))
        args.append(prev)
        aliases = {1: 0}
    return pl.pallas_call(
        body, name=name, grid=grid, in_specs=in_specs, out_specs=out_spec, out_shape=out_shape,
        input_output_aliases=aliases, compiler_params=_cp(tuple(PAR for _ in grid)),
    )(*args)


def _build_wbig(gates4, win4, name):
    ns, _, dq, D = gates4.shape
    per = win4.shape[-1] // LANE
    shape = _sds((D, 3 * D + N_IN_BLOCKS * LANE), gates4.dtype)
    out = _copy_blocks(gates4, pl.BlockSpec((None, None, dq, D), lambda s, i: (s, i, 0, 0)), shape,
                       pl.BlockSpec((dq, D), lambda s, i: (s, i)), (ns, 3), name + "_gates")
    return _copy_blocks(
        win4, pl.BlockSpec((None, D, LANE), lambda p: (_orig_block(p) // per, 0, _orig_block(p) % per)), shape,
        pl.BlockSpec((D, LANE), lambda p: (0, 3 * D // LANE + p)), (N_IN_BLOCKS,), name + "_in", prev=out)


def _ungroup_dw_in(dwp, ns, name):
    D = dwp.shape[0]
    per = N_IN_BLOCKS // ns
    return _copy_blocks(
        dwp, pl.BlockSpec((D, LANE), lambda p: (0, p)), _sds((ns, D, per * LANE), dwp.dtype),
        pl.BlockSpec((None, D, LANE), lambda p: (_orig_block(p) // per, 0, _orig_block(p) % per)), (N_IN_BLOCKS,), name)


def _seg0(big):
    return (big.shape[1] - N_SEG * BRANCH_W) // LANE


def _group_spec(big, base, group, rows, where):
    first = (_seg0(big) + base) // group
    assert first * group == _seg0(big) + base

    def index(*ids):
        r, g = where(*ids)
        return r, first + g

    return pl.BlockSpec((rows, group * LANE), index)


CU, CB, CC = (slice(k * LANE, (k + 1) * LANE) for k in range(3))
AQ, AK, AV = CU, CB, CC
RQ, RK, RV, RG = (slice(k * LANE, (k + 1) * LANE) for k in range(4))


def _conv_fwd(big, cw, name):
    T = big.shape[0]
    R = min(CONV_R, T)

    def body(g_ref, w_ref, y_ref, z_s):
        z_s[pl.ds(0, 8), :] = jnp.zeros((8, LANE), F32)

        def fill(t, c):
            sl = pl.ds(pl.multiple_of(t * R, R), R)
            z_s[pl.ds(pl.multiple_of(t * R + 8, 8), R), :] = g_ref[sl, CC].astype(F32) * g_ref[sl, CU].astype(F32)
            return c

        lax.fori_loop(0, T // R, fill, 0)
        w0, w1, w2 = w_ref[0:1, :], w_ref[1:2, :], w_ref[2:3, :]

        def step(t, c):
            zz = z_s[pl.ds(pl.multiple_of(t * R, R), R + 8), :]
            z0 = zz[8:]
            z1 = pltpu.roll(zz, 1, 0)[8:]
            z2 = pltpu.roll(zz, 2, 0)[8:]
            sl = pl.ds(pl.multiple_of(t * R, R), R)
            y_ref[sl, :] = (g_ref[sl, CB].astype(F32) * (w2 * z0 + w1 * z1 + w0 * z2)).astype(BF)
            return c

        lax.fori_loop(0, T // R, step, 0)

    return pl.pallas_call(
        body, name=name, grid=(BRANCH_W // LANE,),
        in_specs=[_group_spec(big, CONV_BASE, CONV_GROUP, T, lambda j: (0, j)),
                  pl.BlockSpec((3, LANE), lambda j: (0, j))],
        out_specs=pl.BlockSpec((T, LANE), lambda j: (0, j)),
        out_shape=_sds((T, BRANCH_W), BF),
        scratch_shapes=[pltpu.VMEM((T + 8, LANE), F32)],
        compiler_params=_cp((PAR,)),
    )(big, cw)


def _conv_bwd(big, dy, cw, dbig, name):
    T = big.shape[0]
    R = min(CONV_R, T)

    def body(g_ref, dy_ref, w_ref, _, o_ref, dw_ref, z_s, d_s):
        z_s[pl.ds(0, 8), :] = jnp.zeros((8, LANE), F32)
        d_s[pl.ds(T, 8), :] = jnp.zeros((8, LANE), F32)

        def fill(t, c):
            sl = pl.ds(pl.multiple_of(t * R, R), R)
            z_s[pl.ds(pl.multiple_of(t * R + 8, 8), R), :] = g_ref[sl, CC].astype(F32) * g_ref[sl, CU].astype(F32)
            d_s[sl, :] = dy_ref[sl, :].astype(F32) * g_ref[sl, CB].astype(F32)
            return c

        lax.fori_loop(0, T // R, fill, 0)
        w0, w1, w2 = w_ref[0:1, :], w_ref[1:2, :], w_ref[2:3, :]

        def step(t, carry):
            a0, a1, a2 = carry
            zz = z_s[pl.ds(pl.multiple_of(t * R, R), R + 8), :]
            z0 = zz[8:]
            z1 = pltpu.roll(zz, 1, 0)[8:]
            z2 = pltpu.roll(zz, 2, 0)[8:]
            sl = pl.ds(pl.multiple_of(t * R, R), R)
            dyv = dy_ref[sl, :].astype(F32)
            o_ref[sl, CB] = (dyv * (w2 * z0 + w1 * z1 + w0 * z2)).astype(BF)
            dd = d_s[pl.ds(pl.multiple_of(t * R, R), R + 8), :]
            d0 = dd[:R]
            d1 = pltpu.roll(dd, R + 7, 0)[:R]
            d2 = pltpu.roll(dd, R + 6, 0)[:R]
            dz = w2 * d0 + w1 * d1 + w0 * d2
            o_ref[sl, CC] = (dz * g_ref[sl, CU].astype(F32)).astype(BF)
            o_ref[sl, CU] = (dz * g_ref[sl, CC].astype(F32)).astype(BF)
            a0 = a0 + jnp.sum(d0 * z2, axis=0, keepdims=True)
            a1 = a1 + jnp.sum(d0 * z1, axis=0, keepdims=True)
            a2 = a2 + jnp.sum(d0 * z0, axis=0, keepdims=True)
            return a0, a1, a2

        zero = jnp.zeros((1, LANE), F32)
        a0, a1, a2 = lax.fori_loop(0, T // R, step, (zero, zero, zero))
        dw_ref[0:1, :] = a0
        dw_ref[1:2, :] = a1
        dw_ref[2:3, :] = a2

    group = _group_spec(big, CONV_BASE, CONV_GROUP, T, lambda j: (0, j))
    w = pl.BlockSpec((3, LANE), lambda j: (0, j))
    return pl.pallas_call(
        body, name=name, grid=(BRANCH_W // LANE,),
        in_specs=[group, pl.BlockSpec((T, LANE), lambda j: (0, j)), w, pl.BlockSpec(memory_space=pl.ANY)],
        out_specs=[group, w],
        out_shape=[_sds(dbig.shape, BF), _sds((3, BRANCH_W), F32)],
        scratch_shapes=[pltpu.VMEM((T + 8, LANE), F32), pltpu.VMEM((T + 8, LANE), F32)],
        input_output_aliases={3: 0}, compiler_params=_cp((PAR,)),
    )(big, dy, cw, dbig)


def _ret_tables(T):
    L = min(RET_L, T)
    hh = jnp.arange(H_RET, dtype=F32)
    lg = jnp.log1p(-jnp.exp2(-5.0 - hh))
    n = jnp.arange(L, dtype=F32)
    a = jnp.exp(lg[:, None] * (n + 1.0))
    b = jnp.exp(lg[:, None] * (L - 1.0 - n))
    gl = jnp.exp(lg * L)
    ch = jnp.arange(L) // CHUNK
    m = jnp.exp(lg[:, None, None] * jnp.abs(n[:, None] - n[None, :])) * (ch[None, :] <= ch[:, None]).astype(F32)
    inv_freq = ROPE_BASE ** (-jnp.linspace(0.0, 1.0, DK_RET // 2, dtype=F32))
    ang = jnp.arange(T, dtype=F32)[:, None] * inv_freq[None, :]
    cos, sin = jnp.cos(ang), jnp.sin(ang)
    return dict(
        L=L, M=m,
        a=jnp.broadcast_to(a[:, :, None], (H_RET, L, DK_RET)),
        b=jnp.broadcast_to(b[:, :, None], (H_RET, L, DK_RET)),
        gl=jnp.broadcast_to(gl[:, None, None], (H_RET, 1, DK_RET)),
        cos=jnp.concatenate([cos, cos], axis=-1), sin=jnp.concatenate([-sin, sin], axis=-1))


def _rot(x, cs, sn):
    return x * cs + pltpu.roll(x, DK_RET // 2, 1) * sn


def _unrot(dy, cs, sn):
    return dy * cs + pltpu.roll(dy * sn, DK_RET // 2, 1)


def _ret_fwd(big, tb, name, comm=None):
    T = big.shape[0]
    L = tb["L"]
    nsc = T // L
    scale = DK_RET ** -0.5

    def body(x_ref, cos_ref, sin_ref, m_ref, a_ref, b_ref, gl_ref, y_ref, o_ref, st_ref, s_s):
        @pl.when(pl.program_id(1) == 0)
        def _():
            s_s[...] = jnp.zeros_like(s_s)

        cs, sn = cos_ref[...], sin_ref[...]
        qt = _rot(x_ref[:, RQ].astype(F32), cs, sn) * scale
        kt = _rot(x_ref[:, RK].astype(F32), cs, sn)
        qb, kb, vb = qt.astype(BF), kt.astype(BF), x_ref[:, RV]
        s_prev = s_s[...]
        st_ref[...] = s_prev
        p = lax.dot_general(qb, kb, NT_DIMS, preferred_element_type=F32) * m_ref[...]
        o = (jnp.dot(p.astype(BF), vb, preferred_element_type=F32)
             + jnp.dot((qt * a_ref[...]).astype(BF), s_prev.astype(BF), preferred_element_type=F32))
        s_s[...] = s_prev * gl_ref[...] + lax.dot_general((kt * b_ref[...]).astype(BF), vb, TN_DIMS,
                                                         preferred_element_type=F32)
        o_ref[...] = o
        gv = x_ref[:, RG].astype(F32)
        y_ref[...] = (gv * _sigmoid(gv) * o * _rms_r(o)).astype(BF)

    tab = pl.BlockSpec((L, DK_RET), lambda h, i: (i, 0))
    per_head = pl.BlockSpec((None, L, DK_RET), lambda h, i: (h, 0, 0))
    out = pl.BlockSpec((L, LANE), lambda h, i: (i, h))
    return _call(
        body, name=name, grid=(H_RET, nsc), comm=comm,
        args=(big, tb["cos"], tb["sin"], tb["M"], tb["a"], tb["b"], tb["gl"]),
        in_specs=[_group_spec(big, RET_BASE, RET_GROUP, L, lambda h, i: (i, h)), tab, tab,
                  pl.BlockSpec((None, L, L), lambda h, i: (h, 0, 0)), per_head, per_head,
                  pl.BlockSpec((None, 1, DK_RET), lambda h, i: (h, 0, 0))],
        out_specs=[out, out, pl.BlockSpec((None, None, DK_RET, DK_RET), lambda h, i: (i, h, 0, 0))],
        out_shape=[_sds((T, BRANCH_W), BF), _sds((T, BRANCH_W), F32), _sds((nsc, H_RET, DK_RET, DK_RET), F32)],
        scratch_shapes=[pltpu.VMEM((DK_RET, DK_RET), F32)],
        sem=(ARB, ARB))


def _ret_bwd(big, o, st, dy, tb, dbig, name):
    T = big.shape[0]
    L = tb["L"]
    nsc = T // L
    scale = DK_RET ** -0.5

    def body(x_ref, cos_ref, sin_ref, m_ref, a_ref, b_ref, gl_ref, o_ref, st_ref, dy_ref, _, d_ref, ds_s):
        @pl.when(pl.program_id(1) == 0)
        def _():
            ds_s[...] = jnp.zeros_like(ds_s)

        cs, sn = cos_ref[...], sin_ref[...]
        mm, av, bv = m_ref[...], a_ref[...], b_ref[...]
        qt = _rot(x_ref[:, RQ].astype(F32), cs, sn) * scale
        kt = _rot(x_ref[:, RK].astype(F32), cs, sn)
        qb, kb, vb = qt.astype(BF), kt.astype(BF), x_ref[:, RV]
        pb = (lax.dot_general(qb, kb, NT_DIMS, preferred_element_type=F32) * mm).astype(BF)
        ov = o_ref[...]
        r = _rms_r(ov)
        oh = ov * r
        gv = x_ref[:, RG].astype(F32)
        sg = _sigmoid(gv)
        dyv = dy_ref[...].astype(F32)
        d_ref[:, RG] = (dyv * oh * (sg * (1.0 + gv * (1.0 - sg)))).astype(BF)
        doh = dyv * gv * sg
        dob = (r * (doh - oh * jnp.mean(doh * oh, axis=-1, keepdims=True))).astype(BF)
        dsb = ds_s[...].astype(BF)
        spb = st_ref[...].astype(BF)
        dpb = (lax.dot_general(dob, vb, NT_DIMS, preferred_element_type=F32) * mm).astype(BF)
        dqt = (jnp.dot(dpb, kb, preferred_element_type=F32)
               + lax.dot_general(dob, spb, NT_DIMS, preferred_element_type=F32) * av)
        dkt = (lax.dot_general(dpb, qb, TN_DIMS, preferred_element_type=F32)
               + lax.dot_general(vb, dsb, NT_DIMS, preferred_element_type=F32) * bv)
        dv = (lax.dot_general(pb, dob, TN_DIMS, preferred_element_type=F32)
              + jnp.dot((kt * bv).astype(BF), dsb, preferred_element_type=F32))
        ds_s[...] = ds_s[...] * gl_ref[...] + lax.dot_general((qt * av).astype(BF), dob, TN_DIMS,
                                                              preferred_element_type=F32)
        d_ref[:, RQ] = (_unrot(dqt, cs, sn) * scale).astype(BF)
        d_ref[:, RK] = _unrot(dkt, cs, sn).astype(BF)
        d_ref[:, RV] = dv.astype(BF)

    def rev(i):
        return nsc - 1 - i

    group = _group_spec(big, RET_BASE, RET_GROUP, L, lambda h, i: (rev(i), h))
    tab = pl.BlockSpec((L, DK_RET), lambda h, i: (rev(i), 0))
    per_head = pl.BlockSpec((None, L, DK_RET), lambda h, i: (h, 0, 0))
    out = pl.BlockSpec((L, LANE), lambda h, i: (rev(i), h))
    return pl.pallas_call(
        body, name=name, grid=(H_RET, nsc),
        in_specs=[group, tab, tab,
                  pl.BlockSpec((None, L, L), lambda h, i: (h, 0, 0)), per_head, per_head,
                  pl.BlockSpec((None, 1, DK_RET), lambda h, i: (h, 0, 0)),
                  out, pl.BlockSpec((None, None, DK_RET, DK_RET), lambda h, i: (rev(i), h, 0, 0)), out,
                  pl.BlockSpec(memory_space=pl.ANY)],
        out_specs=group,
        out_shape=_sds(dbig.shape, BF),
        scratch_shapes=[pltpu.VMEM((DK_RET, DK_RET), F32)],
        input_output_aliases={10: 0}, compiler_params=_cp((PAR, ARB)),
    )(big, tb["cos"], tb["sin"], tb["M"], tb["a"], tb["b"], tb["gl"], o, st, dy, dbig)


def _relbias_onehot(n):
    mm = lax.broadcasted_iota(jnp.int32, (RB_PAD, ATT_TOEP), 1)
    rr = lax.broadcasted_iota(jnp.int32, (RB_PAD, ATT_TOEP), 0)
    idx = jnp.clip(n + ATT_TOEP - mm, 0, 2 * REL_CLIP)
    return (rr == idx).astype(F32)


def _split3(x):
    hi = x.astype(BF).astype(F32)
    mid = (x - hi).astype(BF).astype(F32)
    lo = x - hi - mid
    return jnp.concatenate([hi, mid, lo], axis=0).astype(BF)


def _join3(y):
    k = y.shape[0] // 3
    return (y[:k] + y[k:2 * k]) + y[2 * k:]


def _relbias_expand(rbp, name):
    far = ATT_SPAN - ATT_TOEP

    def body(rb_ref, o_ref):
        rb = rb_ref[...]
        const = jnp.broadcast_to(rb[:, 2 * REL_CLIP:2 * REL_CLIP + 1], (H_ATT, far))

        rb3 = _split3(rb)

        def row(n, c):
            toep = _join3(jnp.dot(rb3, _relbias_onehot(n).astype(BF), preferred_element_type=F32))
            m = lax.broadcasted_iota(jnp.int32, (1, ATT_SPAN), 1)
            d = n // CHUNK + N_PREV - m // CHUNK
            neg = jnp.where((d >= 0) & (d <= N_PREV), 0.0, NEG_INF).astype(F32)
            o_ref[n] = jnp.concatenate([const, toep], axis=1) + neg
            return c

        lax.fori_loop(0, ATT_TQ, row, 0)

    return pl.pallas_call(
        body, name=name,
        in_specs=[pl.BlockSpec(memory_space=pltpu.VMEM)],
        out_specs=pl.BlockSpec(memory_space=pltpu.VMEM),
        out_shape=_sds((ATT_TQ, H_ATT, ATT_SPAN), F32),
    )(rbp)


def _relbias_grad(dbt, name):
    far = ATT_SPAN - ATT_TOEP

    def body(d_ref, o_ref):
        def row(n, carry):
            acc, cs = carry
            dn = d_ref[n]
            acc = acc + _join3(lax.dot_general(_split3(dn[:, far:]), _relbias_onehot(n).astype(BF), NT_DIMS,
                                               preferred_element_type=F32))
            cs = cs + jnp.sum(dn[:, :far], axis=1, keepdims=True)
            return acc, cs

        acc, cs = lax.fori_loop(0, ATT_TQ, row, (jnp.zeros((H_ATT, RB_PAD), F32), jnp.zeros((H_ATT, 1), F32)))
        rr = lax.broadcasted_iota(jnp.int32, (H_ATT, RB_PAD), 1)
        o_ref[...] = acc + jnp.where(rr == 2 * REL_CLIP, cs, 0.0)

    return pl.pallas_call(
        body, name=name,
        in_specs=[pl.BlockSpec(memory_space=pltpu.VMEM)],
        out_specs=pl.BlockSpec(memory_space=pltpu.VMEM),
        out_shape=_sds((H_ATT, RB_PAD), F32),
    )(dbt)


def _att_pad_fill(dst_s, src_ref, cols, T):
    dst_s[pl.ds(0, ATT_PAD), :] = jnp.zeros((ATT_PAD, LANE), dst_s.dtype)
    R = min(512, T)

    def cp(t, c):
        dst_s[pl.ds(pl.multiple_of(ATT_PAD + t * R, LANE), R), :] = src_ref[pl.ds(pl.multiple_of(t * R, R), R), cols]
        return c

    lax.fori_loop(0, T // R, cp, 0)


ATT_WIN = ATT_SUB * ATT_TQ + ATT_PAD


def _att_probs(s_full, sub, bias, t0):
    s = s_full[sub * ATT_TQ:(sub + 1) * ATT_TQ, sub * ATT_TQ:sub * ATT_TQ + ATT_SPAN] * (DH_ATT ** -0.5) + bias
    key_pos = t0 + sub * ATT_TQ - ATT_PAD + lax.broadcasted_iota(jnp.int32, (1, ATT_SPAN), 1)
    s = jnp.where(key_pos >= 0, s, NEG_INF)
    p = jnp.exp(s - jnp.max(s, axis=-1, keepdims=True))
    return p * (1.0 / jnp.sum(p, axis=-1, keepdims=True))


def _att_band(tiles):
    rows = []
    for sub, t in enumerate(tiles):
        parts = []
        if sub:
            parts.append(jnp.zeros((ATT_TQ, sub * ATT_TQ), BF))
        parts.append(t)
        if sub < ATT_SUB - 1:
            parts.append(jnp.zeros((ATT_TQ, (ATT_SUB - 1 - sub) * ATT_TQ), BF))
        rows.append(jnp.concatenate(parts, axis=1))
    return jnp.concatenate(rows, axis=0)


def _att_head_masks(x):
    first = lax.broadcasted_iota(jnp.int32, (1, LANE), 1) < DH_ATT
    zero = jnp.zeros_like(x)
    return first, (jnp.where(first, x, zero), jnp.where(first, zero, x))


def _att_fwd(big, bias, name, comm=None):
    T = big.shape[0]
    rows = ATT_SUB * ATT_TQ
    nt = T // rows

    def body(x_ref, b_ref, y_ref, kp_s, vp_s):
        i = pl.program_id(1)

        @pl.when(i == 0)
        def _():
            _att_pad_fill(kp_s, x_ref, AK, T)
            _att_pad_fill(vp_s, x_ref, AV, T)

        t0 = pl.multiple_of(i * rows, rows)
        kw = kp_s[pl.ds(t0, ATT_WIN), :]
        vw = vp_s[pl.ds(t0, ATT_WIN), :]
        first, qm = _att_head_masks(x_ref[pl.ds(t0, rows), AQ])
        outs = []
        for hh in range(2):
            s_full = lax.dot_general(qm[hh], kw, NT_DIMS, preferred_element_type=F32)
            band = _att_band([_att_probs(s_full, sub, b_ref[hh], t0).astype(BF) for sub in range(ATT_SUB)])
            outs.append(jnp.dot(band, vw, preferred_element_type=F32))
        y_ref[...] = jnp.where(first, outs[0], outs[1]).astype(BF)

    return _call(
        body, name=name, grid=(H_ATT // 2, nt), args=(big, bias), comm=comm,
        in_specs=[_group_spec(big, ATT_BASE, ATT_GROUP, T, lambda p, i: (0, p)),
                  pl.BlockSpec((2, ATT_TQ, ATT_SPAN), lambda p, i: (p, 0, 0))],
        out_specs=[pl.BlockSpec((rows, LANE), lambda p, i: (i, p))],
        out_shape=[_sds((T, BRANCH_W), BF)],
        scratch_shapes=[pltpu.VMEM((T + ATT_PAD, LANE), BF), pltpu.VMEM((T + ATT_PAD, LANE), BF)],
        sem=(ARB, ARB))


def _att_bwd(big, bias, dy, dbig, name, comm=None):
    T = big.shape[0]
    rows = ATT_SUB * ATT_TQ
    nt = T // rows
    scale = DH_ATT ** -0.5

    def body(x_ref, b_ref, dy_ref, _, d_ref, db_ref, kp_s, vp_s, dk_s, dv_s):
        i = pl.program_id(1)

        @pl.when(i == 0)
        def _():
            _att_pad_fill(kp_s, x_ref, AK, T)
            _att_pad_fill(vp_s, x_ref, AV, T)
            dk_s[...] = jnp.zeros_like(dk_s)
            dv_s[...] = jnp.zeros_like(dv_s)
            db_ref[...] = jnp.zeros_like(db_ref)

        t0 = pl.multiple_of(i * rows, rows)
        win = pl.ds(t0, ATT_WIN)
        kw = kp_s[win, :]
        vw = vp_s[win, :]
        first, qm = _att_head_masks(x_ref[pl.ds(t0, rows), AQ])
        _, dom = _att_head_masks(dy_ref[...])
        dqs, dkt, dvt = [], None, None
        for hh in range(2):
            s_full = lax.dot_general(qm[hh], kw, NT_DIMS, preferred_element_type=F32)
            dp_full = lax.dot_general(dom[hh], vw, NT_DIMS, preferred_element_type=F32)
            ps, dss, db = [], [], None
            for sub in range(ATT_SUB):
                pn = _att_probs(s_full, sub, b_ref[hh], t0)
                dp = dp_full[sub * ATT_TQ:(sub + 1) * ATT_TQ, sub * ATT_TQ:sub * ATT_TQ + ATT_SPAN]
                ds = pn * (dp - jnp.sum(dp * pn, axis=-1, keepdims=True))
                db = ds if db is None else db + ds
                ps.append(pn.astype(BF))
                dss.append(ds.astype(BF))
            db_ref[hh] += db
            ds_band, p_band = _att_band(dss), _att_band(ps)
            dqs.append(jnp.dot(ds_band, kw, preferred_element_type=F32))
            qt = jnp.transpose(qm[hh].astype(F32)).astype(BF)
            dot_ = jnp.transpose(dom[hh].astype(F32)).astype(BF)
            dk_h = jnp.dot(qt, ds_band, preferred_element_type=F32)
            dv_h = jnp.dot(dot_, p_band, preferred_element_type=F32)
            dkt = dk_h if dkt is None else dkt + dk_h
            dvt = dv_h if dvt is None else dvt + dv_h
        d_ref[pl.ds(t0, rows), AQ] = (jnp.where(first, dqs[0], dqs[1]) * scale).astype(BF)
        dk_s[win, :] += jnp.transpose(dkt) * scale
        dv_s[win, :] += jnp.transpose(dvt)

        @pl.when(i == nt - 1)
        def _():
            R = min(512, T)

            def cp(t, c):
                src = pl.ds(pl.multiple_of(ATT_PAD + t * R, LANE), R)
                dst = pl.ds(pl.multiple_of(t * R, R), R)
                d_ref[dst, AK] = dk_s[src, :].astype(BF)
                d_ref[dst, AV] = dv_s[src, :].astype(BF)
                return c

            lax.fori_loop(0, T // R, cp, 0)

    group = _group_spec(big, ATT_BASE, ATT_GROUP, T, lambda p, i: (0, p))
    tile = pl.BlockSpec((rows, LANE), lambda p, i: (i, p))
    bspec = pl.BlockSpec((2, ATT_TQ, ATT_SPAN), lambda p, i: (p, 0, 0))
    return _call(
        body, name=name, grid=(H_ATT // 2, nt), args=(big, bias, dy, dbig), comm=comm, aliases={3: 0}, vmem_mb=56,
        in_specs=[group, bspec, tile, pl.BlockSpec(memory_space=pl.ANY)],
        out_specs=[group, bspec],
        out_shape=[_sds(dbig.shape, BF), _sds((H_ATT, ATT_TQ, ATT_SPAN), F32)],
        scratch_shapes=[pltpu.VMEM((T + ATT_PAD, LANE), BF), pltpu.VMEM((T + ATT_PAD, LANE), BF),
                        pltpu.VMEM((T + ATT_PAD, LANE), F32), pltpu.VMEM((T + ATT_PAD, LANE), F32)],
        sem=(ARB, ARB))


def _merge_fwd(x1, big, ys, wb, wo, name):
    T, D = x1.shape
    tm = min(TM, T)

    def body(x_ref, gp_ref, yc_ref, yr_ref, ya_ref, wb_ref, wo_ref, x2_ref, p_ref, mg_ref):
        merged = jnp.zeros((tm, D), F32)
        for i, y_ref in enumerate((yc_ref, yr_ref, ya_ref)):
            cols = slice(i * D, (i + 1) * D)
            pb = jnp.dot(y_ref[...], wb_ref[i], preferred_element_type=F32).astype(BF)
            p_ref[:, cols] = pb
            merged = merged + _sigmoid(gp_ref[:, cols].astype(F32)) * pb.astype(F32)
        mb = merged.astype(BF)
        mg_ref[...] = mb
        x2_ref[...] = x_ref[...] + jnp.dot(mb, wo_ref[...], preferred_element_type=F32)

    tok = pl.BlockSpec((tm, D), lambda i: (i, 0))
    wide = pl.BlockSpec((tm, 3 * D), lambda i: (i, 0))
    yspec = pl.BlockSpec((tm, BRANCH_W), lambda i: (i, 0))
    return pl.pallas_call(
        body, name=name, grid=(T // tm,),
        in_specs=[tok, wide, yspec, yspec, yspec,
                  pl.BlockSpec((3, BRANCH_W, D), lambda i: (0, 0, 0)),
                  pl.BlockSpec((D, D), lambda i: (0, 0))],
        out_specs=[tok, wide, tok],
        out_shape=[_sds((T, D), F32), _sds((T, 3 * D), BF), _sds((T, D), BF)],
        compiler_params=_cp((PAR,)),
    )(x1, big, *ys, wb, wo)


def _merge_bwd(dx2, big, p, wb, wo, name):
    T, D = dx2.shape
    tm = min(TM, T)

    def body(dx_ref, gp_ref, p_ref, wb_ref, wo_ref, dp_ref, dgp_ref, dyc_ref, dyr_ref, dya_ref, dxb_ref):
        dxb = dx_ref[...].astype(BF)
        dxb_ref[...] = dxb
        dm = lax.dot_general(dxb, wo_ref[...], NT_DIMS, preferred_element_type=F32)
        for i, dy_ref in enumerate((dyc_ref, dyr_ref, dya_ref)):
            cols = slice(i * D, (i + 1) * D)
            gt = _sigmoid(gp_ref[:, cols].astype(F32))
            dpb = (dm * gt).astype(BF)
            dp_ref[:, cols] = dpb
            dgp_ref[:, cols] = (dm * p_ref[:, cols].astype(F32) * gt * (1.0 - gt)).astype(BF)
            dy_ref[...] = lax.dot_general(dpb, wb_ref[i], NT_DIMS, preferred_element_type=F32).astype(BF)

    tok = pl.BlockSpec((tm, D), lambda i: (i, 0))
    wide = pl.BlockSpec((tm, 3 * D), lambda i: (i, 0))
    yspec = pl.BlockSpec((tm, BRANCH_W), lambda i: (i, 0))
    return pl.pallas_call(
        body, name=name, grid=(T // tm,),
        in_specs=[tok, wide, wide,
                  pl.BlockSpec((3, BRANCH_W, D), lambda i: (0, 0, 0)),
                  pl.BlockSpec((D, D), lambda i: (0, 0))],
        out_specs=[wide, wide, yspec, yspec, yspec, tok],
        out_shape=[_sds((T, 3 * D), BF), _sds(big.shape, BF)] + [_sds((T, BRANCH_W), BF)] * 3 + [_sds((T, D), BF)],
        compiler_params=_cp((PAR,)),
    )(dx2, big, p, wb, wo)


def _loss_head(x, tgt, fw, name):
    T, D = x.shape
    tm = min(TM, T)

    def body(x_ref, t_ref, w_ref, loss_ref, dx_ref, dw_ref):
        @pl.when(pl.program_id(0) == 0)
        def _():
            loss_ref[...] = jnp.zeros_like(loss_ref)
            dw_ref[...] = jnp.zeros_like(dw_ref)

        xv = x_ref[...]
        wv = w_ref[...]
        e = xv * _rms_r(xv) * wv - t_ref[...]
        loss_ref[...] += 0.5 * jnp.sum(jnp.mean(e * e, axis=-1, keepdims=True))
        dx, dn = _rms_bwd(e * (1.0 / D), xv, wv)
        dx_ref[...] = dx
        dw_ref[...] += dn

    tok = pl.BlockSpec((tm, D), lambda i: (i, 0))
    return pl.pallas_call(
        body, name=name, grid=(T // tm,),
        in_specs=[tok, tok, pl.BlockSpec((1, D), lambda i: (0, 0))],
        out_specs=[pl.BlockSpec((8, LANE), lambda i: (0, 0)), tok, pl.BlockSpec((1, D), lambda i: (0, 0))],
        out_shape=[_sds((8, LANE), F32), _sds((T, D), F32), _sds((1, D), F32)],
        compiler_params=_cp((ARB,)),
    )(x, tgt, fw)


def _block_rows(rows, cols):
    cap = max(8, (1 << 18) // cols)
    best = None
    for r in range(8, rows + 1, 8):
        if rows % r == 0 and r <= cap:
            best = r
    return best if best is not None else rows


def _sum4(land, l, n_layers, name, prev=None, comm=None):
    _, rows, cols = land.shape
    br = _block_rows(rows, cols)

    def body(*refs):
        l_ref, o_ref = refs[0], refs[-1]
        o_ref[...] = ((l_ref[3].astype(F32) + l_ref[0].astype(F32)) + l_ref[1].astype(F32)) + l_ref[2].astype(F32)

    in_specs = [pl.BlockSpec((4, br, cols), lambda i: (0, i, 0))]
    args = [land]
    aliases = {}
    if prev is not None:
        in_specs.append(pl.BlockSpec(memory_space=pl.ANY))
        args.append(prev)
        aliases = {1: 0}
    main, extra = _call(
        body, name=name, grid=(rows // br,), args=args, in_specs=in_specs,
        out_specs=[pl.BlockSpec((None, br, cols), lambda i: (l, i, 0))],
        out_shape=[_sds((n_layers, rows, cols), F32)], aliases=aliases, sem=(ARB,), comm=comm)
    return main[0], extra


def _adamw_math(w, g, m, v):
    m = ADAM_B1 * m + (1.0 - ADAM_B1) * g
    v = ADAM_B2 * v + (1.0 - ADAM_B2) * (g * g)
    m_hat = m / (1.0 - ADAM_B1 ** ADAM_STEP)
    v_hat = v / (1.0 - ADAM_B2 ** ADAM_STEP)
    delta = -ADAM_LR * (m_hat / (jnp.sqrt(v_hat) + ADAM_EPS) + ADAM_WD * w)
    return delta, m, v


def _adamw(w, ga, gb, m, v, name):
    rows, cols = w.shape
    br = _block_rows(rows, cols)
    two = gb is not None

    def body(*refs):
        if two:
            w_ref, ga_ref, gb_ref, m_ref, v_ref, g_ref, d_ref, nm_ref, nv_ref = refs
            g = ga_ref[...] + gb_ref[...]
        else:
            w_ref, ga_ref, m_ref, v_ref, g_ref, d_ref, nm_ref, nv_ref = refs
            g = ga_ref[...]
        d, nm, nv = _adamw_math(w_ref[...], g, m_ref[...], v_ref[...])
        g_ref[...] = g
        d_ref[...] = d
        nm_ref[...] = nm
        nv_ref[...] = nv

    blk = pl.BlockSpec((br, cols), lambda i: (i, 0))
    args = [w, ga] + ([gb] if two else []) + [m, v]
    return pl.pallas_call(
        body, name=name, grid=(rows // br,),
        in_specs=[blk] * len(args), out_specs=[blk] * 4,
        out_shape=[_sds((rows, cols), F32)] * 4,
        compiler_params=_cp((PAR,)),
    )(*args)


class _CoreSwap:
    def __init__(self, srcs):
        self.srcs = list(srcs)
        n = len(self.srcs)
        self.out_shape = [_sds(s.shape, s.dtype) for s in self.srcs]
        self.scratch = [pltpu.SemaphoreType.DMA((n,)), pltpu.SemaphoreType.DMA((n,))]

    def _copies(self, src, dst, sems):
        x, y, c = _place()
        return [pltpu.make_async_remote_copy(
            src_ref=src[k], dst_ref=dst[k], send_sem=sems[0].at[k], recv_sem=sems[1].at[k],
            device_id=(x, y, 1 - c), device_id_type=MESH) for k in range(len(src))]

    def start(self, src, dst, sems):
        for cp in self._copies(src, dst, sems):
            cp.start()

    def wait(self, src, dst, sems):
        for cp in self._copies(src, dst, sems):
            cp.wait()


def _allreduce_small(v, name):
    rows = v.shape[0]
    flips = [(fx, fy, fc) for fx in (0, 1) for fy in (0, 1) for fc in (0, 1) if fx or fy or fc]

    def body(v_ref, o_ref, all_s, ssem, rsem):
        x, y, c = _place()

        def peer(f):
            return (x + f[0] - 2 * x * f[0], y + f[1] - 2 * y * f[1], c + f[2] - 2 * c * f[2])

        def slot(p):
            return all_s.at[4 * p[0] + 2 * p[1] + p[2]]

        def copy(k, f, owner):
            return pltpu.make_async_remote_copy(
                src_ref=v_ref, dst_ref=slot(owner), send_sem=ssem.at[k], recv_sem=rsem.at[k],
                device_id=peer(f), device_id_type=MESH)

        sends = [copy(k, f, (x, y, c)) for k, f in enumerate(flips)]
        for cp in sends:
            cp.start()
        all_s[4 * x + 2 * y + c] = v_ref[...]
        for k, f in enumerate(flips):
            copy(k, f, peer(f)).wait_recv()
        for cp in sends:
            cp.wait_send()
        acc = all_s[0]
        for d in range(1, 8):
            acc = acc + all_s[d]
        o_ref[...] = acc

    return pl.pallas_call(
        body, name=name,
        in_specs=[pl.BlockSpec(memory_space=pltpu.VMEM)],
        out_specs=pl.BlockSpec(memory_space=pltpu.VMEM),
        out_shape=_sds((rows, LANE), F32),
        scratch_shapes=[pltpu.VMEM((8, rows, LANE), F32), pltpu.SemaphoreType.DMA((7,)), pltpu.SemaphoreType.DMA((7,))],
    )(v)


BIG_NAMES = ("ffn1_w_gate", "ffn1_w_up", "ffn1_w_down", "w_in", "w_branch", "w_merge_gate", "w_out",
             "ffn2_w_gate", "ffn2_w_up", "ffn2_w_down")


FFN1 = ("ffn1_w_gate", "ffn1_w_up", "ffn1_w_down")
FFN2 = ("ffn2_w_gate", "ffn2_w_up", "ffn2_w_down")
MIX_IN = ("w_in", "w_merge_gate")
MIX_OUT = ("w_branch", "w_out")


def _keys(names, l):
    return [(n, l) for n in names]


def _local_step(x, tgt, small, convw_full, wx, n_layers):
    T, D = x.shape
    L = n_layers
    ns = N_SHARD
    dq = D // ns
    W = wx.w

    def hosted(call, keys, scatter=False):
        comm = wx.pieces(keys, scatter)
        main, extra = call(comm)
        if comm is not None:
            wx.arrived(keys, extra, scatter)
        return main

    def mixer_views(l):
        return _build_wbig(W[("w_merge_gate", l)], W[("w_in", l)], f"wbig_{l}")

    def out_views(l):
        wb4 = W[("w_branch", l)]
        wb = _copy_blocks(wb4, pl.BlockSpec((None, None, BRANCH_W, dq), lambda s_, i: (s_, i, 0, 0)),
                          _sds((3, BRANCH_W, D), wb4.dtype),
                          pl.BlockSpec((None, BRANCH_W, dq), lambda s_, i: (i, 0, s_)), (ns, 3), f"w_branch_whole_{l}")
        wo = W[("w_out", l)].reshape(D, D)
        return wb, wo

    tb = _ret_tables(T)
    rb_pad = jnp.pad(small["rel_bias"], ((0, 0), (0, 0), (0, RB_PAD - N_REL)))

    saved = []
    h = x
    for l in range(L):
        s = {"x0": h}
        nxt = l + 1
        x1, s["g1"], s["u1"] = hosted(
            lambda c: _ffn_fwd(h, small["ffn1_norm"][l][None], W[("ffn1_w_gate", l)], W[("ffn1_w_up", l)],
                               W[("ffn1_w_down", l)], f"ffn1_fwd_{l}", comm=c), _keys(MIX_IN, l))
        s["x1"] = x1
        s["wbig"] = mixer_views(l)
        big, s["h"] = _inproj_fwd(x1, small["mix_norm"][l][None], s["wbig"], f"inproj_fwd_{l}")
        s["big"] = big
        s["bias"] = jnp.transpose(_relbias_expand(rb_pad[l], f"relbias_expand_{l}"), (1, 0, 2))
        s["yc"] = _conv_fwd(big, convw_full[l], f"conv_fwd_{l}")
        s["yr"], s["o"], s["st"] = hosted(lambda c: _ret_fwd(big, tb, f"ret_fwd_{l}", comm=c), _keys(MIX_OUT, l))
        (s["ya"],) = hosted(lambda c: _att_fwd(big, s["bias"], f"att_fwd_{l}", comm=c), _keys(FFN2, l))
        s["wb"], s["wo"] = out_views(l)
        x2, s["p"], s["mg"] = _merge_fwd(x1, big, (s["yc"], s["yr"], s["ya"]), s["wb"], s["wo"], f"merge_fwd_{l}")
        s["x2"] = x2
        h, s["g2"], s["u2"] = hosted(
            lambda c: _ffn_fwd(x2, small["ffn2_norm"][l][None], W[("ffn2_w_gate", l)], W[("ffn2_w_up", l)],
                               W[("ffn2_w_down", l)], f"ffn2_fwd_{l}", comm=c), _keys(FFN1, nxt) if nxt < L else [])
        saved.append(s)

    loss_p, dx, d_final = _loss_head(h, tgt, small["final_norm"][None], "loss_head")

    gs = {"final_norm": d_final[0]}
    for k in ("ffn1_norm", "mix_norm", "ffn2_norm", "rel_bias", "conv_w"):
        gs[k] = [None] * L
    tk = min(2048, T)
    nk = T // tk

    def ffn_back(pre, l, dxo, x_in, g, u, first_keys, second_keys):
        nw = small[pre + "_norm"][l][None]
        dgv, duv, av, hb, dacc = hosted(
            lambda c: _ffn_bwd_hidden(dxo, x_in, nw, g, u, W[(pre + "_w_down", l)], f"{pre}_bwd_hidden_{l}", comm=c),
            first_keys, scatter=True)
        dxn, dn = hosted(
            lambda c: _ffn_bwd_resid(dgv, duv, W[(pre + "_w_gate", l)], W[(pre + "_w_up", l)], x_in, nw, dxo,
                                     f"{pre}_bwd_resid_{l}", comm=c),
            second_keys, scatter=True)
        gs[pre + "_norm"][l] = dn[0]
        return dxn, (hb, dgv, duv, av, dacc)

    def ffn_grads(pre, l, hb, dgv, duv, av, dacc, chain=False, carry=()):
        fs = dgv.shape[-1]
        hspec = pl.BlockSpec((tk, D), lambda p, q, k: (k, 0))
        sspec = pl.BlockSpec((None, tk, fs), lambda p, q, k: (p, k, 0))
        down_spec = pl.BlockSpec((None, fs, D), lambda p, q, k: (p, 0, 0))
        jobs = [(pre + "_w_gate", dgv, hb, sspec, hspec, (ns, fs, D), down_spec),
                (pre + "_w_up", duv, hb, sspec, hspec, (ns, fs, D), down_spec),
                (pre + "_w_down", av, dacc, sspec, hspec, (ns, fs, D), down_spec)]
        before = None
        for nm, a, b, a_spec, b_spec, shape, o_spec in jobs:
            def product(c):
                r = _tn(a, b, a_spec, b_spec, _sds(shape, BF), o_spec, (ns, 1, nk), f"d{nm}_{l}", comm=c)
                return (r, []) if c is None else r
            if before is None:
                keys = list(carry)
            else:
                keys = [before] if chain else []
            wx.g[(nm, l)] = hosted(product, keys, scatter=True)
            before = (nm, l)

    for l in reversed(range(L)):
        s = saved[l]
        above = _keys(FFN1, l + 1) if l + 1 < L else [None] * 3
        dx, parts = ffn_back("ffn2", l, dx, s["x2"], s["g2"], s["u2"], [k for k in above[:1] if k],
                             [k for k in above[1:2] if k])
        ffn_grads("ffn2", l, *parts, carry=[k for k in above[2:] if k])
        dp, dbig, dyc, dyr, dya, dxb = _merge_bwd(dx, s["big"], s["p"], s["wb"], s["wo"], f"merge_bwd_{l}")
        wx.g[("w_out", l)] = _tn(
            s["mg"], dxb, pl.BlockSpec((tk, dq), lambda p, q, k: (k, p)), pl.BlockSpec((tk, D), lambda p, q, k: (k, 0)),
            _sds((ns, dq, D), BF), pl.BlockSpec((None, dq, D), lambda p, q, k: (p, 0, 0)), (ns, 1, nk), f"dw_out_{l}")
        gb = None
        for i, yv in enumerate((s["yc"], s["yr"], s["ya"])):
            gb = _tn(yv, dp,
                     pl.BlockSpec((tk, BRANCH_W), lambda p, q, k: (k, 0)),
                     pl.BlockSpec((tk, dq), lambda p, q, k, i=i: (k, i * ns + p)),
                     _sds((ns, 3, BRANCH_W, dq), BF),
                     pl.BlockSpec((None, None, BRANCH_W, dq), lambda p, q, k, i=i: (p, i, 0, 0)),
                     (ns, 1, nk), f"dw_branch{i}_{l}", prev=gb)
        wx.g[("w_branch", l)] = gb
        dbig, dcw = _conv_bwd(s["big"], dyc, convw_full[l], dbig, f"conv_bwd_{l}")
        gs["conv_w"][l] = dcw
        dbig = _ret_bwd(s["big"], s["o"], s["st"], dyr, tb, dbig, f"ret_bwd_{l}")
        dbig, dbias = hosted(lambda c: _att_bwd(s["big"], s["bias"], dya, dbig, f"att_bwd_{l}", comm=c),
                             _keys(FFN2, l), scatter=True)
        gs["rel_bias"][l] = _relbias_grad(jnp.transpose(dbias, (1, 0, 2)), f"relbias_grad_{l}")[:, :N_REL]
        n_in = N_SEG * BRANCH_W
        bn = 1024 if (3 * D) % 1024 == 0 else BRANCH_W
        dwp = _tn(s["h"], dbig, pl.BlockSpec((tk, D), lambda p, q, k: (k, 0)),
                  pl.BlockSpec((tk, bn), lambda p, q, k: (k, 3 * D // bn + q)),
                  _sds((D, n_in), BF), pl.BlockSpec((D, bn), lambda p, q, k: (0, q)), (1, n_in // bn, nk), f"dw_in_{l}")
        wx.g[("w_in", l)] = _ungroup_dw_in(dwp, ns, f"dw_in_shards_{l}")
        wx.g[("w_merge_gate", l)] = _tn_gates(s["h"], dbig, ns, tk, f"dw_merge_gate_{l}")
        dx, dn = hosted(
            lambda c: _inproj_bwd(dbig, s["wbig"], s["x1"], small["mix_norm"][l][None], dx, f"inproj_bwd_{l}", comm=c),
            [("w_in", l)], scatter=True)
        gs["mix_norm"][l] = dn[0]
        dx, parts = ffn_back("ffn1", l, dx, s["x0"], s["g1"], s["u1"],
                             [("w_merge_gate", l), ("w_branch", l), ("w_out", l)], [])
        ffn_grads("ffn1", l, *parts, chain=(l == 0))

    for k in ("ffn1_norm", "mix_norm", "ffn2_norm", "rel_bias", "conv_w"):
        gs[k] = jnp.stack(gs[k])
    return loss_p, dx, gs


class _Exchange:
    def __init__(self, shards):
        self.shards = shards
        self.w = {}
        self.g = {}
        self.landed = {}

    def own(self, key):
        return self.shards[key[0]][key[1]].astype(BF)

    def pieces(self, keys, scatter):
        if not keys:
            return None
        if scatter:
            return _Pieces([self.g[k] for k in keys], True)
        return _HalfGather([_halves(self.own(k)) for k in keys])

    def arrived(self, keys, outs, scatter):
        for k, o in zip(keys, outs):
            if scatter:
                self.landed[k] = o
            else:
                self.w[k] = o.reshape((N_SHARD,) + self.shards[k[0]].shape[1:])


def _halves(a):
    return a.reshape(2, -1, a.shape[-1])


TRANSPOSED_GRADS = ("ffn1_w_gate", "ffn1_w_up", "ffn2_w_gate", "ffn2_w_up")
W_NAMES = ("ffn1_norm", "ffn1_w_gate", "ffn1_w_up", "ffn1_w_down", "mix_norm", "w_in", "conv_w", "rel_bias", "w_branch",
           "w_merge_gate", "w_out", "ffn2_norm", "ffn2_w_gate", "ffn2_w_up", "ffn2_w_down", "final_norm")


def _as2d(a):
    return a.reshape(1, -1) if a.ndim == 1 else a.reshape(-1, a.shape[-1])


def kernel(x, ffn1_norm, ffn1_w_gate, ffn1_w_up, ffn1_w_down, mix_norm, w_in, conv_w, rel_bias, w_branch, w_merge_gate, w_out, ffn2_norm, ffn2_w_gate, ffn2_w_up, ffn2_w_down, final_norm, loss_target, m_ffn1_norm, m_ffn1_w_gate, m_ffn1_w_up, m_ffn1_w_down, m_mix_norm, m_w_in, m_conv_w, m_rel_bias, m_w_branch, m_w_merge_gate, m_w_out, m_ffn2_norm, m_ffn2_w_gate, m_ffn2_w_up, m_ffn2_w_down, m_final_norm, v_ffn1_norm, v_ffn1_w_gate, v_ffn1_w_up, v_ffn1_w_down, v_mix_norm, v_w_in, v_conv_w, v_rel_bias, v_w_branch, v_w_merge_gate, v_w_out, v_ffn2_norm, v_ffn2_w_gate, v_ffn2_w_up, v_ffn2_w_down, v_final_norm):
    given = dict(locals())
    w = {n: given[n] for n in W_NAMES}
    m = {n: given["m_" + n] for n in W_NAMES}
    v = {n: given["v_" + n] for n in W_NAMES}
    my_chip = 2 * lax.axis_index("x") + lax.axis_index("y")
    L = w_in.shape[0]

    wx = _Exchange({n: jnp.swapaxes(w[n], 1, 2) if n in TRANSPOSED_GRADS else w[n] for n in BIG_NAMES})
    first = _keys(FFN1, 0)
    got = _comm_alone(_HalfGather([_halves(wx.own(k)) for k in first] + [_halves(conv_w)]), "gather_first")
    wx.arrived(first, got[:-1], False)
    convw_full = jnp.transpose(got[-1].reshape((N_SHARD,) + conv_w.shape), (1, 2, 0, 3)).reshape(
        conv_w.shape[0], conv_w.shape[1], -1)

    small = {n: w[n] for n in ("ffn1_norm", "mix_norm", "ffn2_norm", "final_norm", "rel_bias")}
    loss_p, grad_x, gs = _local_step(x[0], loss_target[0], small, convw_full, wx, L)
    last = [(FFN1[-1], 0)]
    wx.arrived(last, _comm_alone(wx.pieces(last, True), "scatter_last"), True)

    sums, others = [], []
    for n in BIG_NAMES:
        acc = None
        for l in range(L):
            a = wx.landed[(n, l)]
            swap = _CoreSwap(sums[-1:]) if sums and l == L - 1 else None
            acc, came = _sum4(a.reshape(4, -1, a.shape[-1]), l, L, f"sum4_{n}_{l}", prev=acc, comm=swap)
            others += came
        sums.append(acc.reshape(-1, acc.shape[-1]))
    others += _comm_alone(_CoreSwap(sums[-1:]), "swap_last")

    parts = [gs["ffn1_norm"].reshape(-1), gs["mix_norm"].reshape(-1), gs["ffn2_norm"].reshape(-1),
             gs["final_norm"].reshape(-1), gs["rel_bias"].reshape(-1), gs["conv_w"].reshape(-1), loss_p[0]]
    sizes = [p.shape[0] for p in parts]
    flat = jnp.concatenate(parts)
    rows = -(-flat.shape[0] // (8 * LANE)) * 8
    flat = jnp.pad(flat, (0, rows * LANE - flat.shape[0])).reshape(rows, LANE)
    red = _allreduce_small(flat, "allreduce_small").reshape(-1)
    offs = [0]
    for sz in sizes:
        offs.append(offs[-1] + sz)
    sm = {}
    for i, n in enumerate(("ffn1_norm", "mix_norm", "ffn2_norm", "final_norm", "rel_bias", "conv_w")):
        sm[n] = red[offs[i]:offs[i + 1]]
    loss = red[offs[6]]
    sm["conv_w"] = lax.dynamic_slice_in_dim(sm["conv_w"].reshape(conv_w.shape[0], conv_w.shape[1], -1),
                                            my_chip * conv_w.shape[2], conv_w.shape[2], axis=2)

    grads, deltas, new_m, new_v = {}, {}, {}, {}
    big_sum = dict(zip(BIG_NAMES, zip(sums, others)))
    for n in W_NAMES:
        flip = n in TRANSPOSED_GRADS

        def view(a):
            return jnp.swapaxes(a, 1, 2) if flip else a

        shape = view(w[n]).shape
        if n in big_sum:
            ga, gb = big_sum[n]
        else:
            ga, gb = _as2d(sm[n].reshape(shape)), None
        out = _adamw(_as2d(view(w[n])), ga, gb, _as2d(view(m[n])), _as2d(view(v[n])), f"adamw_{n}")
        grads[n], deltas[n], new_m[n], new_v[n] = (view(o.reshape(shape)) for o in out)

    return (loss, grad_x[None], *[grads[n] for n in W_NAMES], *[deltas[n] for n in W_NAMES],
            *[new_m[n] for n in W_NAMES], *[new_v[n] for n in W_NAMES])
```

```python
import functools
import math

import jax
import jax.numpy as jnp
from jax import lax
from jax.experimental import pallas as pl
from jax.experimental.pallas import tpu as pltpu

F32 = jnp.float32
BF = jnp.bfloat16
MESH = pl.DeviceIdType.MESH
ARB = "arbitrary"
PAR = "parallel"

EPS = 1e-6
NEG_INF = -1e30
ROPE_BASE = 10000.0
CHUNK = 64
BRANCH_W = 512
H_RET = 4
DK_RET = 128
H_ATT = 8
DH_ATT = 64
N_PREV = 8
REL_CLIP = 128
N_REL = 2 * REL_CLIP + 1
N_SHARD = 4
LANE = 128
RET_L = 512
ATT_TQ = 128
ATT_SUB = 4
ATT_PAD = N_PREV * CHUNK
ATT_SPAN = ATT_TQ + ATT_PAD
ATT_TOEP = 2 * REL_CLIP
RB_PAD = 264
TM = 512
TM_FFN = 1024

ADAM_LR = 0.001
ADAM_B1 = 0.9
ADAM_B2 = 0.999
ADAM_EPS = 1e-08
ADAM_WD = 0.01
ADAM_STEP = 10

NT_DIMS = (((1,), (1,)), ((), ()))
TN_DIMS = (((0,), (0,)), ((), ()))


def _cp(sem, vmem_mb=48):
    return pltpu.CompilerParams(dimension_semantics=sem, vmem_limit_bytes=vmem_mb << 20)


def _sds(shape, dtype):
    return jax.ShapeDtypeStruct(tuple(shape), dtype)


def _rms_r(x):
    return lax.rsqrt(jnp.mean(x * x, axis=-1, keepdims=True) + EPS)


def _sigmoid(x):
    return 0.5 * jnp.tanh(0.5 * x) + 0.5


def _rms_bwd(dh, xv, nw):
    r = _rms_r(xv)
    xh = xv * r
    dxh = dh * nw
    dx = r * (dxh - xh * jnp.mean(dxh * xh, axis=-1, keepdims=True))
    return dx, jnp.sum(dh * xh, axis=0, keepdims=True)


def _place():
    return lax.axis_index("x"), lax.axis_index("y"), lax.axis_index("c")


def _other_chips(x, y):
    return [(1 - x, y), (x, 1 - y), (1 - x, 1 - y)]


class _Scatter:
    def __init__(self, srcs):
        self.srcs = list(srcs)
        n = len(self.srcs)
        self.out_shape = [_sds((2 * N_SHARD,) + s.shape[1:], s.dtype) for s in self.srcs]
        self.scratch = [pltpu.SemaphoreType.DMA((n,)), pltpu.SemaphoreType.DMA((3, n)), pltpu.SemaphoreType.DMA((3, n)),
                        pltpu.SemaphoreType.DMA((4, n)), pltpu.SemaphoreType.DMA((4, n))]

    def _plan(self, src, dst, sems, want):
        lsem, s1, r1, s2, r2 = sems
        x, y, c = _place()
        mine = 2 * x + y
        n = len(src)
        chips = list(enumerate(_other_chips(x, y)))

        def copy(s_ref, d_ref, ssem, rsem, to):
            return pltpu.make_async_remote_copy(src_ref=s_ref, dst_ref=d_ref, send_sem=ssem, recv_sem=rsem,
                                                device_id=to, device_id_type=MESH)

        local = [pltpu.make_async_copy(src[k].at[mine], dst[k].at[3], lsem.at[k]) for k in range(n)
                 ] if "local" in want else []
        sends = [copy(src[k].at[2 * ch[0] + ch[1]], dst[k].at[j], s1.at[j, k], r1.at[j, k], (ch[0], ch[1], c))
                 for j, ch in chips for k in range(n)] if "sends" in want else []
        passes = [copy(dst[k].at[j], dst[k].at[4 + j], s2.at[j, k], r2.at[j, k], (x, y, 1 - c))
                  for j, ch in chips for k in range(n)] if "passes" in want else []
        own_pass = [copy(src[k].at[mine], dst[k].at[7], s2.at[3, k], r2.at[3, k], (x, y, 1 - c))
                    for k in range(n)] if "own_pass" in want else []
        return local, sends, passes, own_pass

    def start(self, src, dst, sems):
        local, sends, _, own_pass = self._plan(src, dst, sems, ("local", "sends", "own_pass"))
        for cp in local + sends + own_pass:
            cp.start()

    def relay(self, src, dst, sems):
        _, sends, passes, _ = self._plan(src, dst, sems, ("sends", "passes"))
        for land, fwd in zip(sends, passes):
            land.wait_recv()
            fwd.start()

    def finish(self, src, dst, sems):
        local, sends, passes, own_pass = self._plan(src, dst, sems, ("local", "sends", "passes", "own_pass"))
        for cp in passes + own_pass:
            cp.wait_recv()
        for cp in sends + passes + own_pass:
            cp.wait_send()
        for cp in local:
            cp.wait()

    def wait(self, src, dst, sems):
        self.relay(src, dst, sems)
        self.finish(src, dst, sems)


class _HalfGather:
    def __init__(self, srcs):
        self.srcs = list(srcs)
        n = len(self.srcs)
        self.out_shape = [_sds((N_SHARD,) + s.shape, s.dtype) for s in self.srcs]
        self.scratch = [pltpu.SemaphoreType.DMA((n,))] + [pltpu.SemaphoreType.DMA((3, n)) for _ in range(4)]

    def _plan(self, src, dst, sems, want):
        lsem, s1, r1, s2, r2 = sems
        x, y, c = _place()
        mine = 2 * x + y
        n = len(src)
        chips = [(j, ch, 2 * ch[0] + ch[1]) for j, ch in enumerate(_other_chips(x, y))]

        def copy(s_ref, d_ref, ssem, rsem, to):
            return pltpu.make_async_remote_copy(src_ref=s_ref, dst_ref=d_ref, send_sem=ssem, recv_sem=rsem,
                                                device_id=to, device_id_type=MESH)

        def over(kind, make):
            return [make(j, ch, slot, k) for j, ch, slot in chips for k in range(n)] if kind in want else []

        local = [pltpu.make_async_copy(src[k], dst[k].at[mine], lsem.at[k]) for k in range(n)] if "local" in want else []
        sends = over("sends", lambda j, ch, slot, k: copy(src[k].at[c], dst[k].at[mine, c], s1.at[j, k], r1.at[j, k],
                                                          (ch[0], ch[1], c)))
        lands = over("lands", lambda j, ch, slot, k: copy(src[k].at[c], dst[k].at[slot, c], s1.at[j, k], r1.at[j, k],
                                                          (ch[0], ch[1], c)))
        passes = over("passes", lambda j, ch, slot, k: copy(dst[k].at[slot, c], dst[k].at[slot, c], s2.at[j, k],
                                                            r2.at[j, k], (x, y, 1 - c)))
        gets = over("gets", lambda j, ch, slot, k: copy(dst[k].at[slot, 1 - c], dst[k].at[slot, 1 - c], s2.at[j, k],
                                                        r2.at[j, k], (x, y, 1 - c)))
        return local, sends, lands, passes, gets

    def start(self, src, dst, sems):
        lsem, s1, r1, s2, r2 = sems
        x, y, c = _place()
        mine = 2 * x + y
        for k in range(len(src)):
            pltpu.make_async_copy(src[k], dst[k].at[mine], lsem.at[k]).start()
        for j, ch in enumerate(_other_chips(x, y)):
            for k in range(len(src)):
                pltpu.make_async_remote_copy(
                    src_ref=src[k].at[c], dst_ref=dst[k].at[mine, c], send_sem=s1.at[j, k], recv_sem=r1.at[j, k],
                    device_id=(ch[0], ch[1], c), device_id_type=MESH).start()

    def relay(self, src, dst, sems):
        _, _, lands, passes, _ = self._plan(src, dst, sems, ("lands", "passes"))
        for land, fwd in zip(lands, passes):
            land.wait_recv()
            fwd.start()

    def finish(self, src, dst, sems):
        local, sends, _, passes, gets = self._plan(src, dst, sems, ("local", "sends", "passes", "gets"))
        for cp in gets:
            cp.wait_recv()
        for cp in sends + passes:
            cp.wait_send()
        for cp in local:
            cp.wait()

    def wait(self, src, dst, sems):
        self.relay(src, dst, sems)
        self.finish(src, dst, sems)


def _call(body, *, name, args, in_specs, out_specs, out_shape, grid=(), scratch_shapes=(), sem=None, comm=None,
          aliases=None, vmem_mb=48):
    in_specs, out_specs, out_shape = list(in_specs), list(out_specs), list(out_shape)
    scratch, args = list(scratch_shapes), list(args)
    n_in, n_out, n_scr = len(in_specs), len(out_specs), len(scratch)
    if comm is None:
        def kernel_body(*refs):
            body(*refs)
    else:
        c_in, c_out = len(comm.srcs), len(comm.out_shape)

        def kernel_body(*refs):
            o0 = n_in + c_in
            s0 = o0 + n_out + c_out
            cin, cout, sems = refs[n_in:o0], refs[o0 + n_out:s0], refs[s0 + n_scr:]
            main = refs[:n_in] + refs[o0:o0 + n_out] + refs[s0:s0 + n_scr]
            if grid:
                ids = [pl.program_id(a) for a in range(len(grid))]
                first = functools.reduce(lambda p, q: p & q, [i == 0 for i in ids])
                last = functools.reduce(lambda p, q: p & q, [i == g - 1 for i, g in zip(ids, grid)])

                @pl.when(first)
                def _():
                    comm.start(cin, cout, sems)

                body(*main)

                steps = math.prod(grid)
                if hasattr(comm, "relay") and steps >= 4:
                    flat = functools.reduce(lambda p, q: p + q, [i * math.prod(grid[a + 1:]) for a, i in enumerate(ids)])

                    @pl.when(flat == (2 * steps) // 3)
                    def _():
                        comm.relay(cin, cout, sems)

                    @pl.when(last)
                    def _():
                        comm.finish(cin, cout, sems)
                else:
                    @pl.when(last)
                    def _():
                        comm.wait(cin, cout, sems)
            else:
                comm.start(cin, cout, sems)
                body(*main)
                comm.wait(cin, cout, sems)

        hbm = pl.BlockSpec(memory_space=pl.ANY)
        in_specs += [hbm] * c_in
        out_specs += [hbm] * c_out
        out_shape += comm.out_shape
        scratch += comm.scratch
        args += comm.srcs
    params = dict(vmem_limit_bytes=vmem_mb << 20)
    if grid:
        params["dimension_semantics"] = sem
    outs = pl.pallas_call(
        kernel_body, name=name, grid=grid, in_specs=in_specs, out_specs=out_specs, out_shape=out_shape,
        scratch_shapes=scratch, input_output_aliases=aliases or {}, compiler_params=pltpu.CompilerParams(**params),
    )(*args)
    return list(outs[:n_out]), list(outs[n_out:])


def _comm_alone(comm, name):
    return _call(lambda: None, name=name, args=[], in_specs=[], out_specs=[], out_shape=[], comm=comm)[1]


def _ffn_fwd(x, nw, wg, wu, wd, name, comm=None):
    T, D = x.shape
    ns, fs, _ = wg.shape
    tm = min(TM_FFN, T)

    def body(x_ref, nw_ref, wg_ref, wu_ref, wd_ref, xo_ref, g_ref, u_ref, h_s, acc_s):
        j = pl.program_id(1)

        @pl.when(j == 0)
        def _():
            xv = x_ref[...]
            h_s[...] = (xv * _rms_r(xv) * nw_ref[...]).astype(BF)
            acc_s[...] = jnp.zeros_like(acc_s)

        h = h_s[...]
        gb = lax.dot_general(h, wg_ref[...], NT_DIMS, preferred_element_type=F32).astype(BF)
        ub = lax.dot_general(h, wu_ref[...], NT_DIMS, preferred_element_type=F32).astype(BF)
        g_ref[...] = gb
        u_ref[...] = ub
        g = gb.astype(F32)
        a = (g * _sigmoid(g) * ub.astype(F32)).astype(BF)
        acc_s[...] += jnp.dot(a, wd_ref[...], preferred_element_type=F32)

        @pl.when(j == ns - 1)
        def _():
            xo_ref[...] = x_ref[...] + 0.5 * acc_s[...]

    wspec = pl.BlockSpec((None, fs, D), lambda i, j: (j, 0, 0))
    return _call(
        body, name=name, grid=(T // tm, ns), args=(x, nw, wg, wu, wd), comm=comm, vmem_mb=56,
        in_specs=[pl.BlockSpec((tm, D), lambda i, j: (i, 0)),
                  pl.BlockSpec((1, D), lambda i, j: (0, 0)),
                  wspec, wspec,
                  pl.BlockSpec((None, fs, D), lambda i, j: (j, 0, 0))],
        out_specs=[pl.BlockSpec((tm, D), lambda i, j: (i, 0)),
                   pl.BlockSpec((None, tm, fs), lambda i, j: (j, i, 0)),
                   pl.BlockSpec((None, tm, fs), lambda i, j: (j, i, 0))],
        out_shape=[_sds((T, D), F32), _sds((ns, T, fs), BF), _sds((ns, T, fs), BF)],
        scratch_shapes=[pltpu.VMEM((tm, D), BF), pltpu.VMEM((tm, D), F32)],
        sem=(ARB, ARB))


def _ffn_bwd_hidden(dxo, x, nw, g, u, wd, name, comm=None):
    T, D = x.shape
    ns, fs, _ = wd.shape
    tm = min(TM_FFN, T)

    def body(dxo_ref, x_ref, nw_ref, g_ref, u_ref, wd_ref, dg_ref, du_ref, a_ref, h_ref, dacc_ref, dacc_s):
        @pl.when(pl.program_id(1) == 0)
        def _():
            xv = x_ref[...]
            h_ref[...] = (xv * _rms_r(xv) * nw_ref[...]).astype(BF)
            db = (0.5 * dxo_ref[...]).astype(BF)
            dacc_ref[...] = db
            dacc_s[...] = db

        da = lax.dot_general(dacc_s[...], wd_ref[...], NT_DIMS, preferred_element_type=F32)
        gv = g_ref[...].astype(F32)
        uv = u_ref[...].astype(F32)
        s = _sigmoid(gv)
        sg = gv * s
        a_ref[...] = (sg * uv).astype(BF)
        du_ref[...] = (da * sg).astype(BF)
        dg_ref[...] = (da * uv * (s * (1.0 + gv * (1.0 - s)))).astype(BF)

    tok = pl.BlockSpec((tm, D), lambda i, j: (i, 0))
    hid = pl.BlockSpec((None, tm, fs), lambda i, j: (j, i, 0))
    return _call(
        body, name=name, grid=(T // tm, ns), args=(dxo, x, nw, g, u, wd), comm=comm, vmem_mb=56,
        in_specs=[tok, tok, pl.BlockSpec((1, D), lambda i, j: (0, 0)), hid, hid,
                  pl.BlockSpec((None, fs, D), lambda i, j: (j, 0, 0))],
        out_specs=[hid, hid, hid, tok, tok],
        out_shape=[_sds((ns, T, fs), BF)] * 3 + [_sds((T, D), BF)] * 2,
        scratch_shapes=[pltpu.VMEM((tm, D), BF)],
        sem=(ARB, ARB))


def _ffn_bwd_resid(dg, du, wg, wu, x, nw, dxo, name, comm=None):
    T, D = x.shape
    ns, fs, _ = wg.shape
    tm = min(TM_FFN, T)

    def body(dg_ref, du_ref, wg_ref, wu_ref, x_ref, nw_ref, dxo_ref, dx_ref, dnw_ref, acc_s):
        i = pl.program_id(0)
        j = pl.program_id(1)
        prod = (jnp.dot(dg_ref[...], wg_ref[...], preferred_element_type=F32)
                + jnp.dot(du_ref[...], wu_ref[...], preferred_element_type=F32))

        @pl.when((i == 0) & (j == 0))
        def _():
            dnw_ref[...] = jnp.zeros_like(dnw_ref)

        @pl.when(j == 0)
        def _():
            acc_s[...] = prod

        @pl.when(j > 0)
        def _():
            acc_s[...] += prod

        @pl.when(j == ns - 1)
        def _():
            dx, dn = _rms_bwd(acc_s[...], x_ref[...], nw_ref[...])
            dx_ref[...] = dxo_ref[...] + dx
            dnw_ref[...] += dn

    tok = pl.BlockSpec((tm, D), lambda i, j: (i, 0))
    row = pl.BlockSpec((1, D), lambda i, j: (0, 0))
    hid = pl.BlockSpec((None, tm, fs), lambda i, j: (j, i, 0))
    wspec = pl.BlockSpec((None, fs, D), lambda i, j: (j, 0, 0))
    return _call(
        body, name=name, grid=(T // tm, ns), args=(dg, du, wg, wu, x, nw, dxo), comm=comm, vmem_mb=56,
        in_specs=[hid, hid, wspec, wspec, tok, row, tok],
        out_specs=[tok, row],
        out_shape=[_sds((T, D), F32), _sds((1, D), F32)],
        scratch_shapes=[pltpu.VMEM((tm, D), F32)],
        sem=(ARB, ARB))


def _tn(a, b, a_spec, b_spec, out_shape, out_spec, grid, name, prev=None, comm=None):
    nk = grid[-1]
    acc_shape = tuple(d for d in out_spec.block_shape if d is not None)

    def body(*refs):
        a_ref, b_ref = refs[0], refs[1]
        o_ref, acc = refs[-2], refs[-1]
        k = pl.program_id(2)
        prod = lax.dot_general(a_ref[...], b_ref[...], TN_DIMS, preferred_element_type=F32)

        @pl.when(k == 0)
        def _():
            acc[...] = prod

        @pl.when(k > 0)
        def _():
            acc[...] += prod

        @pl.when(k == nk - 1)
        def _():
            o_ref[...] = acc[...].astype(o_ref.dtype)

    in_specs = [a_spec, b_spec]
    args = [a, b]
    aliases = {}
    if prev is not None:
        in_specs.append(pl.BlockSpec(memory_space=pl.ANY))
        args.append(prev)
        aliases = {2: 0}
    main, extra = _call(
        body, name=name, grid=grid, args=args, in_specs=in_specs, out_specs=[out_spec], out_shape=[out_shape],
        scratch_shapes=[pltpu.VMEM(acc_shape, F32)], aliases=aliases, sem=(ARB, ARB, ARB), comm=comm)
    return main[0] if comm is None else (main[0], extra)


def _tn_gates(h, dbig, ns, tk, name):
    T, D = h.shape
    dq = D // ns
    nk = T // tk

    def body(a_ref, b_ref, o_ref, acc):
        k = pl.program_id(1)
        prod = lax.dot_general(a_ref[...], b_ref[...], TN_DIMS, preferred_element_type=F32)

        @pl.when(k == 0)
        def _():
            acc[...] = prod

        @pl.when(k > 0)
        def _():
            acc[...] += prod

        @pl.when(k == nk - 1)
        def _():
            for s in range(ns):
                o_ref[s] = acc[s * dq:(s + 1) * dq, :].astype(o_ref.dtype)

    return pl.pallas_call(
        body, name=name, grid=(3, nk),
        in_specs=[pl.BlockSpec((tk, D), lambda q, k: (k, 0)), pl.BlockSpec((tk, D), lambda q, k: (k, q))],
        out_specs=pl.BlockSpec((ns, None, dq, D), lambda q, k: (0, q, 0, 0)),
        out_shape=_sds((ns, 3, dq, D), BF),
        scratch_shapes=[pltpu.VMEM((D, D), F32)],
        compiler_params=_cp((PAR, ARB)),
    )(h, dbig)


def _inproj_fwd(x, nw, wbig, name):
    T, D = x.shape
    nb = wbig.shape[-1]
    tm = min(2 * TM, T)
    bn = min(2048, nb)

    def body(x_ref, nw_ref, w_ref, o_ref, h_ref, h_s):
        @pl.when(pl.program_id(1) == 0)
        def _():
            xv = x_ref[...]
            hb = (xv * _rms_r(xv) * nw_ref[...]).astype(BF)
            h_s[...] = hb
            h_ref[...] = hb

        o_ref[...] = jnp.dot(h_s[...], w_ref[...], preferred_element_type=F32).astype(BF)

    return pl.pallas_call(
        body, name=name, grid=(T // tm, nb // bn),
        in_specs=[pl.BlockSpec((tm, D), lambda i, n: (i, 0)),
                  pl.BlockSpec((1, D), lambda i, n: (0, 0)),
                  pl.BlockSpec((D, bn), lambda i, n: (0, n))],
        out_specs=[pl.BlockSpec((tm, bn), lambda i, n: (i, n)),
                   pl.BlockSpec((tm, D), lambda i, n: (i, 0))],
        out_shape=[_sds((T, nb), BF), _sds((T, D), BF)],
        scratch_shapes=[pltpu.VMEM((tm, D), BF)],
        compiler_params=_cp((PAR, ARB)),
    )(x, nw, wbig)


def _inproj_bwd(dbig, wbig, x, nw, dxin, name, comm=None):
    T, D = x.shape
    nb = wbig.shape[-1]
    tm = min(TM_FFN, T)
    tk = min(2048, nb)
    nk = nb // tk

    def body(a_ref, w_ref, x_ref, nw_ref, dxin_ref, dx_ref, dnw_ref, acc_s):
        i = pl.program_id(0)
        k = pl.program_id(1)
        prod = lax.dot_general(a_ref[...], w_ref[...], NT_DIMS, preferred_element_type=F32)

        @pl.when((i == 0) & (k == 0))
        def _():
            dnw_ref[...] = jnp.zeros_like(dnw_ref)

        @pl.when(k == 0)
        def _():
            acc_s[...] = prod

        @pl.when(k > 0)
        def _():
            acc_s[...] += prod

        @pl.when(k == nk - 1)
        def _():
            dx, dn = _rms_bwd(acc_s[...], x_ref[...], nw_ref[...])
            dx_ref[...] = dxin_ref[...] + dx
            dnw_ref[...] += dn

    tok = pl.BlockSpec((tm, D), lambda i, k: (i, 0))
    row = pl.BlockSpec((1, D), lambda i, k: (0, 0))
    return _call(
        body, name=name, grid=(T // tm, nk), args=(dbig, wbig, x, nw, dxin), comm=comm, vmem_mb=56,
        in_specs=[pl.BlockSpec((tm, tk), lambda i, k: (i, k)),
                  pl.BlockSpec((D, tk), lambda i, k: (0, k)),
                  tok, row, tok],
        out_specs=[tok, row],
        out_shape=[_sds((T, D), F32), _sds((1, D), F32)],
        scratch_shapes=[pltpu.VMEM((tm, D), F32)],
        sem=(ARB, ARB))


CONV_R = 512
CONV_BASE, CONV_GROUP = 0, 3
ATT_BASE, ATT_GROUP = 12, 3
RET_BASE, RET_GROUP = 24, 4
N_SEG = 10


N_IN_BLOCKS = N_SEG * BRANCH_W // LANE


def _orig_block(p):
    nblk = BRANCH_W // LANE
    qa, qr = p - ATT_BASE, p - RET_BASE
    conv = (p % CONV_GROUP) * nblk + p // CONV_GROUP
    att = (7 + qa % ATT_GROUP) * nblk + qa // ATT_GROUP
    ret = (3 + qr % RET_GROUP) * nblk + qr // RET_GROUP
    return jnp.where(p < ATT_BASE, conv, jnp.where(p < RET_BASE, att, ret))


def _copy_blocks(src, in_spec, out_shape, out_spec, grid, name, prev=None):
    def body(*refs):
        refs[-1][...] = refs[0][...]

    in_specs, args, aliases = [in_spec], [src], {}
    if prev is not None:
        in_specs.append(pl.BlockSpec(memory_space=pl.ANY))
        args.append(prev)
        aliases = {1: 0}
    return pl.pallas_call(
        body, name=name, grid=grid, in_specs=in_specs, out_specs=out_spec, out_shape=out_shape,
        input_output_aliases=aliases, compiler_params=_cp(tuple(PAR for _ in grid)),
    )(*args)


def _build_wbig(gates4, win4, name):
    ns, _, dq, D = gates4.shape
    per = win4.shape[-1] // LANE
    shape = _sds((D, 3 * D + N_IN_BLOCKS * LANE), gates4.dtype)
    out = _copy_blocks(gates4, pl.BlockSpec((None, None, dq, D), lambda s, i: (s, i, 0, 0)), shape,
                       pl.BlockSpec((dq, D), lambda s, i: (s, i)), (ns, 3), name + "_gates")
    return _copy_blocks(
        win4, pl.BlockSpec((None, D, LANE), lambda p: (_orig_block(p) // per, 0, _orig_block(p) % per)), shape,
        pl.BlockSpec((D, LANE), lambda p: (0, 3 * D // LANE + p)), (N_IN_BLOCKS,), name + "_in", prev=out)


def _ungroup_dw_in(dwp, ns, name):
    D = dwp.shape[0]
    per = N_IN_BLOCKS // ns
    return _copy_blocks(
        dwp, pl.BlockSpec((D, LANE), lambda p: (0, p)), _sds((ns, D, per * LANE), dwp.dtype),
        pl.BlockSpec((None, D, LANE), lambda p: (_orig_block(p) // per, 0, _orig_block(p) % per)), (N_IN_BLOCKS,), name)


def _seg0(big):
    return (big.shape[1] - N_SEG * BRANCH_W) // LANE


def _group_spec(big, base, group, rows, where):
    first = (_seg0(big) + base) // group
    assert first * group == _seg0(big) + base

    def index(*ids):
        r, g = where(*ids)
        return r, first + g

    return pl.BlockSpec((rows, group * LANE), index)


CU, CB, CC = (slice(k * LANE, (k + 1) * LANE) for k in range(3))
AQ, AK, AV = CU, CB, CC
RQ, RK, RV, RG = (slice(k * LANE, (k + 1) * LANE) for k in range(4))


def _conv_fwd(big, cw, name):
    T = big.shape[0]
    R = min(CONV_R, T)

    def body(g_ref, w_ref, y_ref, z_s):
        z_s[pl.ds(0, 8), :] = jnp.zeros((8, LANE), F32)

        def fill(t, c):
            sl = pl.ds(pl.multiple_of(t * R, R), R)
            z_s[pl.ds(pl.multiple_of(t * R + 8, 8), R), :] = g_ref[sl, CC].astype(F32) * g_ref[sl, CU].astype(F32)
            return c

        lax.fori_loop(0, T // R, fill, 0)
        w0, w1, w2 = w_ref[0:1, :], w_ref[1:2, :], w_ref[2:3, :]

        def step(t, c):
            zz = z_s[pl.ds(pl.multiple_of(t * R, R), R + 8), :]
            z0 = zz[8:]
            z1 = pltpu.roll(zz, 1, 0)[8:]
            z2 = pltpu.roll(zz, 2, 0)[8:]
            sl = pl.ds(pl.multiple_of(t * R, R), R)
            y_ref[sl, :] = (g_ref[sl, CB].astype(F32) * (w2 * z0 + w1 * z1 + w0 * z2)).astype(BF)
            return c

        lax.fori_loop(0, T // R, step, 0)

    return pl.pallas_call(
        body, name=name, grid=(BRANCH_W // LANE,),
        in_specs=[_group_spec(big, CONV_BASE, CONV_GROUP, T, lambda j: (0, j)),
                  pl.BlockSpec((3, LANE), lambda j: (0, j))],
        out_specs=pl.BlockSpec((T, LANE), lambda j: (0, j)),
        out_shape=_sds((T, BRANCH_W), BF),
        scratch_shapes=[pltpu.VMEM((T + 8, LANE), F32)],
        compiler_params=_cp((PAR,)),
    )(big, cw)


def _conv_bwd(big, dy, cw, dbig, name):
    T = big.shape[0]
    R = min(CONV_R, T)

    def body(g_ref, dy_ref, w_ref, _, o_ref, dw_ref, z_s, d_s):
        z_s[pl.ds(0, 8), :] = jnp.zeros((8, LANE), F32)
        d_s[pl.ds(T, 8), :] = jnp.zeros((8, LANE), F32)

        def fill(t, c):
            sl = pl.ds(pl.multiple_of(t * R, R), R)
            z_s[pl.ds(pl.multiple_of(t * R + 8, 8), R), :] = g_ref[sl, CC].astype(F32) * g_ref[sl, CU].astype(F32)
            d_s[sl, :] = dy_ref[sl, :].astype(F32) * g_ref[sl, CB].astype(F32)
            return c

        lax.fori_loop(0, T // R, fill, 0)
        w0, w1, w2 = w_ref[0:1, :], w_ref[1:2, :], w_ref[2:3, :]

        def step(t, carry):
            a0, a1, a2 = carry
            zz = z_s[pl.ds(pl.multiple_of(t * R, R), R + 8), :]
            z0 = zz[8:]
            z1 = pltpu.roll(zz, 1, 0)[8:]
            z2 = pltpu.roll(zz, 2, 0)[8:]
            sl = pl.ds(pl.multiple_of(t * R, R), R)
            dyv = dy_ref[sl, :].astype(F32)
            o_ref[sl, CB] = (dyv * (w2 * z0 + w1 * z1 + w0 * z2)).astype(BF)
            dd = d_s[pl.ds(pl.multiple_of(t * R, R), R + 8), :]
            d0 = dd[:R]
            d1 = pltpu.roll(dd, R + 7, 0)[:R]
            d2 = pltpu.roll(dd, R + 6, 0)[:R]
            dz = w2 * d0 + w1 * d1 + w0 * d2
            o_ref[sl, CC] = (dz * g_ref[sl, CU].astype(F32)).astype(BF)
            o_ref[sl, CU] = (dz * g_ref[sl, CC].astype(F32)).astype(BF)
            a0 = a0 + jnp.sum(d0 * z2, axis=0, keepdims=True)
            a1 = a1 + jnp.sum(d0 * z1, axis=0, keepdims=True)
            a2 = a2 + jnp.sum(d0 * z0, axis=0, keepdims=True)
            return a0, a1, a2

        zero = jnp.zeros((1, LANE), F32)
        a0, a1, a2 = lax.fori_loop(0, T // R, step, (zero, zero, zero))
        dw_ref[0:1, :] = a0
        dw_ref[1:2, :] = a1
        dw_ref[2:3, :] = a2

    group = _group_spec(big, CONV_BASE, CONV_GROUP, T, lambda j: (0, j))
    w = pl.BlockSpec((3, LANE), lambda j: (0, j))
    return pl.pallas_call(
        body, name=name, grid=(BRANCH_W // LANE,),
        in_specs=[group, pl.BlockSpec((T, LANE), lambda j: (0, j)), w, pl.BlockSpec(memory_space=pl.ANY)],
        out_specs=[group, w],
        out_shape=[_sds(dbig.shape, BF), _sds((3, BRANCH_W), F32)],
        scratch_shapes=[pltpu.VMEM((T + 8, LANE), F32), pltpu.VMEM((T + 8, LANE), F32)],
        input_output_aliases={3: 0}, compiler_params=_cp((PAR,)),
    )(big, dy, cw, dbig)


def _ret_tables(T):
    L = min(RET_L, T)
    hh = jnp.arange(H_RET, dtype=F32)
    lg = jnp.log1p(-jnp.exp2(-5.0 - hh))
    n = jnp.arange(L, dtype=F32)
    a = jnp.exp(lg[:, None] * (n + 1.0))
    b = jnp.exp(lg[:, None] * (L - 1.0 - n))
    gl = jnp.exp(lg * L)
    ch = jnp.arange(L) // CHUNK
    m = jnp.exp(lg[:, None, None] * jnp.abs(n[:, None] - n[None, :])) * (ch[None, :] <= ch[:, None]).astype(F32)
    inv_freq = ROPE_BASE ** (-jnp.linspace(0.0, 1.0, DK_RET // 2, dtype=F32))
    ang = jnp.arange(T, dtype=F32)[:, None] * inv_freq[None, :]
    cos, sin = jnp.cos(ang), jnp.sin(ang)
    return dict(
        L=L, M=m,
        a=jnp.broadcast_to(a[:, :, None], (H_RET, L, DK_RET)),
        b=jnp.broadcast_to(b[:, :, None], (H_RET, L, DK_RET)),
        gl=jnp.broadcast_to(gl[:, None, None], (H_RET, 1, DK_RET)),
        cos=jnp.concatenate([cos, cos], axis=-1), sin=jnp.concatenate([-sin, sin], axis=-1))


def _rot(x, cs, sn):
    return x * cs + pltpu.roll(x, DK_RET // 2, 1) * sn


def _unrot(dy, cs, sn):
    return dy * cs + pltpu.roll(dy * sn, DK_RET // 2, 1)


def _ret_fwd(big, tb, name, comm=None):
    T = big.shape[0]
    L = tb["L"]
    nsc = T // L
    scale = DK_RET ** -0.5

    def body(x_ref, cos_ref, sin_ref, m_ref, a_ref, b_ref, gl_ref, y_ref, o_ref, st_ref, s_s):
        @pl.when(pl.program_id(1) == 0)
        def _():
            s_s[...] = jnp.zeros_like(s_s)

        cs, sn = cos_ref[...], sin_ref[...]
        qt = _rot(x_ref[:, RQ].astype(F32), cs, sn) * scale
        kt = _rot(x_ref[:, RK].astype(F32), cs, sn)
        qb, kb, vb = qt.astype(BF), kt.astype(BF), x_ref[:, RV]
        s_prev = s_s[...]
        st_ref[...] = s_prev
        p = lax.dot_general(qb, kb, NT_DIMS, preferred_element_type=F32) * m_ref[...]
        o = (jnp.dot(p.astype(BF), vb, preferred_element_type=F32)
             + jnp.dot((qt * a_ref[...]).astype(BF), s_prev.astype(BF), preferred_element_type=F32))
        s_s[...] = s_prev * gl_ref[...] + lax.dot_general((kt * b_ref[...]).astype(BF), vb, TN_DIMS,
                                                         preferred_element_type=F32)
        o_ref[...] = o
        gv = x_ref[:, RG].astype(F32)
        y_ref[...] = (gv * _sigmoid(gv) * o * _rms_r(o)).astype(BF)

    tab = pl.BlockSpec((L, DK_RET), lambda h, i: (i, 0))
    per_head = pl.BlockSpec((None, L, DK_RET), lambda h, i: (h, 0, 0))
    out = pl.BlockSpec((L, LANE), lambda h, i: (i, h))
    return _call(
        body, name=name, grid=(H_RET, nsc), comm=comm,
        args=(big, tb["cos"], tb["sin"], tb["M"], tb["a"], tb["b"], tb["gl"]),
        in_specs=[_group_spec(big, RET_BASE, RET_GROUP, L, lambda h, i: (i, h)), tab, tab,
                  pl.BlockSpec((None, L, L), lambda h, i: (h, 0, 0)), per_head, per_head,
                  pl.BlockSpec((None, 1, DK_RET), lambda h, i: (h, 0, 0))],
        out_specs=[out, out, pl.BlockSpec((None, None, DK_RET, DK_RET), lambda h, i: (i, h, 0, 0))],
        out_shape=[_sds((T, BRANCH_W), BF), _sds((T, BRANCH_W), F32), _sds((nsc, H_RET, DK_RET, DK_RET), F32)],
        scratch_shapes=[pltpu.VMEM((DK_RET, DK_RET), F32)],
        sem=(ARB, ARB))


def _ret_bwd(big, o, st, dy, tb, dbig, name):
    T = big.shape[0]
    L = tb["L"]
    nsc = T // L
    scale = DK_RET ** -0.5

    def body(x_ref, cos_ref, sin_ref, m_ref, a_ref, b_ref, gl_ref, o_ref, st_ref, dy_ref, _, d_ref, ds_s):
        @pl.when(pl.program_id(1) == 0)
        def _():
            ds_s[...] = jnp.zeros_like(ds_s)

        cs, sn = cos_ref[...], sin_ref[...]
        mm, av, bv = m_ref[...], a_ref[...], b_ref[...]
        qt = _rot(x_ref[:, RQ].astype(F32), cs, sn) * scale
        kt = _rot(x_ref[:, RK].astype(F32), cs, sn)
        qb, kb, vb = qt.astype(BF), kt.astype(BF), x_ref[:, RV]
        pb = (lax.dot_general(qb, kb, NT_DIMS, preferred_element_type=F32) * mm).astype(BF)
        ov = o_ref[...]
        r = _rms_r(ov)
        oh = ov * r
        gv = x_ref[:, RG].astype(F32)
        sg = _sigmoid(gv)
        dyv = dy_ref[...].astype(F32)
        d_ref[:, RG] = (dyv * oh * (sg * (1.0 + gv * (1.0 - sg)))).astype(BF)
        doh = dyv * gv * sg
        dob = (r * (doh - oh * jnp.mean(doh * oh, axis=-1, keepdims=True))).astype(BF)
        dsb = ds_s[...].astype(BF)
        spb = st_ref[...].astype(BF)
        dpb = (lax.dot_general(dob, vb, NT_DIMS, preferred_element_type=F32) * mm).astype(BF)
        dqt = (jnp.dot(dpb, kb, preferred_element_type=F32)
               + lax.dot_general(dob, spb, NT_DIMS, preferred_element_type=F32) * av)
        dkt = (lax.dot_general(dpb, qb, TN_DIMS, preferred_element_type=F32)
               + lax.dot_general(vb, dsb, NT_DIMS, preferred_element_type=F32) * bv)
        dv = (lax.dot_general(pb, dob, TN_DIMS, preferred_element_type=F32)
              + jnp.dot((kt * bv).astype(BF), dsb, preferred_element_type=F32))
        ds_s[...] = ds_s[...] * gl_ref[...] + lax.dot_general((qt * av).astype(BF), dob, TN_DIMS,
                                                              preferred_element_type=F32)
        d_ref[:, RQ] = (_unrot(dqt, cs, sn) * scale).astype(BF)
        d_ref[:, RK] = _unrot(dkt, cs, sn).astype(BF)
        d_ref[:, RV] = dv.astype(BF)

    def rev(i):
        return nsc - 1 - i

    group = _group_spec(big, RET_BASE, RET_GROUP, L, lambda h, i: (rev(i), h))
    tab = pl.BlockSpec((L, DK_RET), lambda h, i: (rev(i), 0))
    per_head = pl.BlockSpec((None, L, DK_RET), lambda h, i: (h, 0, 0))
    out = pl.BlockSpec((L, LANE), lambda h, i: (rev(i), h))
    return pl.pallas_call(
        body, name=name, grid=(H_RET, nsc),
        in_specs=[group, tab, tab,
                  pl.BlockSpec((None, L, L), lambda h, i: (h, 0, 0)), per_head, per_head,
                  pl.BlockSpec((None, 1, DK_RET), lambda h, i: (h, 0, 0)),
                  out, pl.BlockSpec((None, None, DK_RET, DK_RET), lambda h, i: (rev(i), h, 0, 0)), out,
                  pl.BlockSpec(memory_space=pl.ANY)],
        out_specs=group,
        out_shape=_sds(dbig.shape, BF),
        scratch_shapes=[pltpu.VMEM((DK_RET, DK_RET), F32)],
        input_output_aliases={10: 0}, compiler_params=_cp((PAR, ARB)),
    )(big, tb["cos"], tb["sin"], tb["M"], tb["a"], tb["b"], tb["gl"], o, st, dy, dbig)


def _relbias_onehot(n):
    mm = lax.broadcasted_iota(jnp.int32, (RB_PAD, ATT_TOEP), 1)
    rr = lax.broadcasted_iota(jnp.int32, (RB_PAD, ATT_TOEP), 0)
    idx = jnp.clip(n + ATT_TOEP - mm, 0, 2 * REL_CLIP)
    return (rr == idx).astype(F32)


def _split3(x):
    hi = x.astype(BF).astype(F32)
    mid = (x - hi).astype(BF).astype(F32)
    lo = x - hi - mid
    return jnp.concatenate([hi, mid, lo], axis=0).astype(BF)


def _join3(y):
    k = y.shape[0] // 3
    return (y[:k] + y[k:2 * k]) + y[2 * k:]


def _relbias_expand(rbp, name):
    far = ATT_SPAN - ATT_TOEP

    def body(rb_ref, o_ref):
        rb = rb_ref[...]
        const = jnp.broadcast_to(rb[:, 2 * REL_CLIP:2 * REL_CLIP + 1], (H_ATT, far))

        rb3 = _split3(rb)

        def row(n, c):
            toep = _join3(jnp.dot(rb3, _relbias_onehot(n).astype(BF), preferred_element_type=F32))
            m = lax.broadcasted_iota(jnp.int32, (1, ATT_SPAN), 1)
            d = n // CHUNK + N_PREV - m // CHUNK
            neg = jnp.where((d >= 0) & (d <= N_PREV), 0.0, NEG_INF).astype(F32)
            o_ref[n] = jnp.concatenate([const, toep], axis=1) + neg
            return c

        lax.fori_loop(0, ATT_TQ, row, 0)

    return pl.pallas_call(
        body, name=name,
        in_specs=[pl.BlockSpec(memory_space=pltpu.VMEM)],
        out_specs=pl.BlockSpec(memory_space=pltpu.VMEM),
        out_shape=_sds((ATT_TQ, H_ATT, ATT_SPAN), F32),
    )(rbp)


def _relbias_grad(dbt, name):
    far = ATT_SPAN - ATT_TOEP

    def body(d_ref, o_ref):
        def row(n, carry):
            acc, cs = carry
            dn = d_ref[n]
            acc = acc + _join3(lax.dot_general(_split3(dn[:, far:]), _relbias_onehot(n).astype(BF), NT_DIMS,
                                               preferred_element_type=F32))
            cs = cs + jnp.sum(dn[:, :far], axis=1, keepdims=True)
            return acc, cs

        acc, cs = lax.fori_loop(0, ATT_TQ, row, (jnp.zeros((H_ATT, RB_PAD), F32), jnp.zeros((H_ATT, 1), F32)))
        rr = lax.broadcasted_iota(jnp.int32, (H_ATT, RB_PAD), 1)
        o_ref[...] = acc + jnp.where(rr == 2 * REL_CLIP, cs, 0.0)

    return pl.pallas_call(
        body, name=name,
        in_specs=[pl.BlockSpec(memory_space=pltpu.VMEM)],
        out_specs=pl.BlockSpec(memory_space=pltpu.VMEM),
        out_shape=_sds((H_ATT, RB_PAD), F32),
    )(dbt)


def _att_pad_fill(dst_s, src_ref, cols, T):
    dst_s[pl.ds(0, ATT_PAD), :] = jnp.zeros((ATT_PAD, LANE), dst_s.dtype)
    R = min(512, T)

    def cp(t, c):
        dst_s[pl.ds(pl.multiple_of(ATT_PAD + t * R, LANE), R), :] = src_ref[pl.ds(pl.multiple_of(t * R, R), R), cols]
        return c

    lax.fori_loop(0, T // R, cp, 0)


ATT_WIN = ATT_SUB * ATT_TQ + ATT_PAD


def _att_probs(s_full, sub, bias, t0):
    s = s_full[sub * ATT_TQ:(sub + 1) * ATT_TQ, sub * ATT_TQ:sub * ATT_TQ + ATT_SPAN] * (DH_ATT ** -0.5) + bias
    key_pos = t0 + sub * ATT_TQ - ATT_PAD + lax.broadcasted_iota(jnp.int32, (1, ATT_SPAN), 1)
    s = jnp.where(key_pos >= 0, s, NEG_INF)
    p = jnp.exp(s - jnp.max(s, axis=-1, keepdims=True))
    return p * (1.0 / jnp.sum(p, axis=-1, keepdims=True))


def _att_band(tiles):
    rows = []
    for sub, t in enumerate(tiles):
        parts = []
        if sub:
            parts.append(jnp.zeros((ATT_TQ, sub * ATT_TQ), BF))
        parts.append(t)
        if sub < ATT_SUB - 1:
            parts.append(jnp.zeros((ATT_TQ, (ATT_SUB - 1 - sub) * ATT_TQ), BF))
        rows.append(jnp.concatenate(parts, axis=1))
    return jnp.concatenate(rows, axis=0)


def _att_head_masks(x):
    first = lax.broadcasted_iota(jnp.int32, (1, LANE), 1) < DH_ATT
    zero = jnp.zeros_like(x)
    return first, (jnp.where(first, x, zero), jnp.where(first, zero, x))


def _att_fwd(big, bias, name, comm=None):
    T = big.shape[0]
    rows = ATT_SUB * ATT_TQ
    nt = T // rows

    def body(x_ref, b_ref, y_ref, kp_s, vp_s):
        i = pl.program_id(1)

        @pl.when(i == 0)
        def _():
            _att_pad_fill(kp_s, x_ref, AK, T)
            _att_pad_fill(vp_s, x_ref, AV, T)

        t0 = pl.multiple_of(i * rows, rows)
        kw = kp_s[pl.ds(t0, ATT_WIN), :]
        vw = vp_s[pl.ds(t0, ATT_WIN), :]
        first, qm = _att_head_masks(x_ref[pl.ds(t0, rows), AQ])
        outs = []
        for hh in range(2):
            s_full = lax.dot_general(qm[hh], kw, NT_DIMS, preferred_element_type=F32)
            band = _att_band([_att_probs(s_full, sub, b_ref[hh], t0).astype(BF) for sub in range(ATT_SUB)])
            outs.append(jnp.dot(band, vw, preferred_element_type=F32))
        y_ref[...] = jnp.where(first, outs[0], outs[1]).astype(BF)

    return _call(
        body, name=name, grid=(H_ATT // 2, nt), args=(big, bias), comm=comm,
        in_specs=[_group_spec(big, ATT_BASE, ATT_GROUP, T, lambda p, i: (0, p)),
                  pl.BlockSpec((2, ATT_TQ, ATT_SPAN), lambda p, i: (p, 0, 0))],
        out_specs=[pl.BlockSpec((rows, LANE), lambda p, i: (i, p))],
        out_shape=[_sds((T, BRANCH_W), BF)],
        scratch_shapes=[pltpu.VMEM((T + ATT_PAD, LANE), BF), pltpu.VMEM((T + ATT_PAD, LANE), BF)],
        sem=(ARB, ARB))


def _att_bwd(big, bias, dy, dbig, name, comm=None):
    T = big.shape[0]
    rows = ATT_SUB * ATT_TQ
    nt = T // rows
    scale = DH_ATT ** -0.5

    def body(x_ref, b_ref, dy_ref, _, d_ref, db_ref, kp_s, vp_s, dk_s, dv_s):
        i = pl.program_id(1)

        @pl.when(i == 0)
        def _():
            _att_pad_fill(kp_s, x_ref, AK, T)
            _att_pad_fill(vp_s, x_ref, AV, T)
            dk_s[...] = jnp.zeros_like(dk_s)
            dv_s[...] = jnp.zeros_like(dv_s)
            db_ref[...] = jnp.zeros_like(db_ref)

        t0 = pl.multiple_of(i * rows, rows)
        win = pl.ds(t0, ATT_WIN)
        kw = kp_s[win, :]
        vw = vp_s[win, :]
        first, qm = _att_head_masks(x_ref[pl.ds(t0, rows), AQ])
        _, dom = _att_head_masks(dy_ref[...])
        dqs, dkt, dvt = [], None, None
        for hh in range(2):
            s_full = lax.dot_general(qm[hh], kw, NT_DIMS, preferred_element_type=F32)
            dp_full = lax.dot_general(dom[hh], vw, NT_DIMS, preferred_element_type=F32)
            ps, dss, db = [], [], None
            for sub in range(ATT_SUB):
                pn = _att_probs(s_full, sub, b_ref[hh], t0)
                dp = dp_full[sub * ATT_TQ:(sub + 1) * ATT_TQ, sub * ATT_TQ:sub * ATT_TQ + ATT_SPAN]
                ds = pn * (dp - jnp.sum(dp * pn, axis=-1, keepdims=True))
                db = ds if db is None else db + ds
                ps.append(pn.astype(BF))
                dss.append(ds.astype(BF))
            db_ref[hh] += db
            ds_band, p_band = _att_band(dss), _att_band(ps)
            dqs.append(jnp.dot(ds_band, kw, preferred_element_type=F32))
            qt = jnp.transpose(qm[hh].astype(F32)).astype(BF)
            dot_ = jnp.transpose(dom[hh].astype(F32)).astype(BF)
            dk_h = jnp.dot(qt, ds_band, preferred_element_type=F32)
            dv_h = jnp.dot(dot_, p_band, preferred_element_type=F32)
            dkt = dk_h if dkt is None else dkt + dk_h
            dvt = dv_h if dvt is None else dvt + dv_h
        d_ref[pl.ds(t0, rows), AQ] = (jnp.where(first, dqs[0], dqs[1]) * scale).astype(BF)
        dk_s[win, :] += jnp.transpose(dkt) * scale
        dv_s[win, :] += jnp.transpose(dvt)

        @pl.when(i == nt - 1)
        def _():
            R = min(512, T)

            def cp(t, c):
                src = pl.ds(pl.multiple_of(ATT_PAD + t * R, LANE), R)
                dst = pl.ds(pl.multiple_of(t * R, R), R)
                d_ref[dst, AK] = dk_s[src, :].astype(BF)
                d_ref[dst, AV] = dv_s[src, :].astype(BF)
                return c

            lax.fori_loop(0, T // R, cp, 0)

    group = _group_spec(big, ATT_BASE, ATT_GROUP, T, lambda p, i: (0, p))
    tile = pl.BlockSpec((rows, LANE), lambda p, i: (i, p))
    bspec = pl.BlockSpec((2, ATT_TQ, ATT_SPAN), lambda p, i: (p, 0, 0))
    return _call(
        body, name=name, grid=(H_ATT // 2, nt), args=(big, bias, dy, dbig), comm=comm, aliases={3: 0}, vmem_mb=56,
        in_specs=[group, bspec, tile, pl.BlockSpec(memory_space=pl.ANY)],
        out_specs=[group, bspec],
        out_shape=[_sds(dbig.shape, BF), _sds((H_ATT, ATT_TQ, ATT_SPAN), F32)],
        scratch_shapes=[pltpu.VMEM((T + ATT_PAD, LANE), BF), pltpu.VMEM((T + ATT_PAD, LANE), BF),
                        pltpu.VMEM((T + ATT_PAD, LANE), F32), pltpu.VMEM((T + ATT_PAD, LANE), F32)],
        sem=(ARB, ARB))


def _merge_fwd(x1, big, ys, wb, wo, name):
    T, D = x1.shape
    tm = min(TM, T)

    def body(x_ref, gp_ref, yc_ref, yr_ref, ya_ref, wb_ref, wo_ref, x2_ref, p_ref, mg_ref):
        merged = jnp.zeros((tm, D), F32)
        for i, y_ref in enumerate((yc_ref, yr_ref, ya_ref)):
            cols = slice(i * D, (i + 1) * D)
            pb = jnp.dot(y_ref[...], wb_ref[i], preferred_element_type=F32).astype(BF)
            p_ref[:, cols] = pb
            merged = merged + _sigmoid(gp_ref[:, cols].astype(F32)) * pb.astype(F32)
        mb = merged.astype(BF)
        mg_ref[...] = mb
        x2_ref[...] = x_ref[...] + jnp.dot(mb, wo_ref[...], preferred_element_type=F32)

    tok = pl.BlockSpec((tm, D), lambda i: (i, 0))
    wide = pl.BlockSpec((tm, 3 * D), lambda i: (i, 0))
    yspec = pl.BlockSpec((tm, BRANCH_W), lambda i: (i, 0))
    return pl.pallas_call(
        body, name=name, grid=(T // tm,),
        in_specs=[tok, wide, yspec, yspec, yspec,
                  pl.BlockSpec((3, BRANCH_W, D), lambda i: (0, 0, 0)),
                  pl.BlockSpec((D, D), lambda i: (0, 0))],
        out_specs=[tok, wide, tok],
        out_shape=[_sds((T, D), F32), _sds((T, 3 * D), BF), _sds((T, D), BF)],
        compiler_params=_cp((PAR,)),
    )(x1, big, *ys, wb, wo)


def _merge_bwd(dx2, big, p, wb, wo, name):
    T, D = dx2.shape
    tm = min(TM, T)

    def body(dx_ref, gp_ref, p_ref, wb_ref, wo_ref, dp_ref, dgp_ref, dyc_ref, dyr_ref, dya_ref, dxb_ref):
        dxb = dx_ref[...].astype(BF)
        dxb_ref[...] = dxb
        dm = lax.dot_general(dxb, wo_ref[...], NT_DIMS, preferred_element_type=F32)
        for i, dy_ref in enumerate((dyc_ref, dyr_ref, dya_ref)):
            cols = slice(i * D, (i + 1) * D)
            gt = _sigmoid(gp_ref[:, cols].astype(F32))
            dpb = (dm * gt).astype(BF)
            dp_ref[:, cols] = dpb
            dgp_ref[:, cols] = (dm * p_ref[:, cols].astype(F32) * gt * (1.0 - gt)).astype(BF)
            dy_ref[...] = lax.dot_general(dpb, wb_ref[i], NT_DIMS, preferred_element_type=F32).astype(BF)

    tok = pl.BlockSpec((tm, D), lambda i: (i, 0))
    wide = pl.BlockSpec((tm, 3 * D), lambda i: (i, 0))
    yspec = pl.BlockSpec((tm, BRANCH_W), lambda i: (i, 0))
    return pl.pallas_call(
        body, name=name, grid=(T // tm,),
        in_specs=[tok, wide, wide,
                  pl.BlockSpec((3, BRANCH_W, D), lambda i: (0, 0, 0)),
                  pl.BlockSpec((D, D), lambda i: (0, 0))],
        out_specs=[wide, wide, yspec, yspec, yspec, tok],
        out_shape=[_sds((T, 3 * D), BF), _sds(big.shape, BF)] + [_sds((T, BRANCH_W), BF)] * 3 + [_sds((T, D), BF)],
        compiler_params=_cp((PAR,)),
    )(dx2, big, p, wb, wo)


def _loss_head(x, tgt, fw, name):
    T, D = x.shape
    tm = min(TM, T)

    def body(x_ref, t_ref, w_ref, loss_ref, dx_ref, dw_ref):
        @pl.when(pl.program_id(0) == 0)
        def _():
            loss_ref[...] = jnp.zeros_like(loss_ref)
            dw_ref[...] = jnp.zeros_like(dw_ref)

        xv = x_ref[...]
        wv = w_ref[...]
        e = xv * _rms_r(xv) * wv - t_ref[...]
        loss_ref[...] += 0.5 * jnp.sum(jnp.mean(e * e, axis=-1, keepdims=True))
        dx, dn = _rms_bwd(e * (1.0 / D), xv, wv)
        dx_ref[...] = dx
        dw_ref[...] += dn

    tok = pl.BlockSpec((tm, D), lambda i: (i, 0))
    return pl.pallas_call(
        body, name=name, grid=(T // tm,),
        in_specs=[tok, tok, pl.BlockSpec((1, D), lambda i: (0, 0))],
        out_specs=[pl.BlockSpec((8, LANE), lambda i: (0, 0)), tok, pl.BlockSpec((1, D), lambda i: (0, 0))],
        out_shape=[_sds((8, LANE), F32), _sds((T, D), F32), _sds((1, D), F32)],
        compiler_params=_cp((ARB,)),
    )(x, tgt, fw)


def _block_rows(rows, cols):
    cap = max(8, (1 << 18) // cols)
    best = None
    for r in range(8, rows + 1, 8):
        if rows % r == 0 and r <= cap:
            best = r
    return best if best is not None else rows


def _sum8(land, l, n_layers, name, prev=None, comm=None):
    _, rows, cols = land.shape
    br = _block_rows(rows, cols)

    def body(*refs):
        l_ref, o_ref = refs[0], refs[-1]

        def four(base):
            return ((l_ref[base + 3].astype(F32) + l_ref[base].astype(F32)) + l_ref[base + 1].astype(F32)
                    ) + l_ref[base + 2].astype(F32)

        o_ref[...] = four(0) + four(4)

    in_specs = [pl.BlockSpec((2 * N_SHARD, br, cols), lambda i: (0, i, 0))]
    args = [land]
    aliases = {}
    if prev is not None:
        in_specs.append(pl.BlockSpec(memory_space=pl.ANY))
        args.append(prev)
        aliases = {1: 0}
    main, extra = _call(
        body, name=name, grid=(rows // br,), args=args, in_specs=in_specs,
        out_specs=[pl.BlockSpec((None, br, cols), lambda i: (l, i, 0))],
        out_shape=[_sds((n_layers, rows, cols), F32)], aliases=aliases, sem=(ARB,), comm=comm)
    return main[0], extra


def _adamw_math(w, g, m, v):
    m = ADAM_B1 * m + (1.0 - ADAM_B1) * g
    v = ADAM_B2 * v + (1.0 - ADAM_B2) * (g * g)
    m_hat = m / (1.0 - ADAM_B1 ** ADAM_STEP)
    v_hat = v / (1.0 - ADAM_B2 ** ADAM_STEP)
    delta = -ADAM_LR * (m_hat / (jnp.sqrt(v_hat) + ADAM_EPS) + ADAM_WD * w)
    return delta, m, v


def _adamw(w, g, m, v, name):
    rows, cols = w.shape
    br = _block_rows(rows, cols)

    def body(w_ref, g_ref, m_ref, v_ref, d_ref, nm_ref, nv_ref):
        d, nm, nv = _adamw_math(w_ref[...], g_ref[...], m_ref[...], v_ref[...])
        d_ref[...] = d
        nm_ref[...] = nm
        nv_ref[...] = nv

    blk = pl.BlockSpec((br, cols), lambda i: (i, 0))
    return pl.pallas_call(
        body, name=name, grid=(rows // br,),
        in_specs=[blk] * 4, out_specs=[blk] * 3,
        out_shape=[_sds((rows, cols), F32)] * 3,
        compiler_params=_cp((PAR,)),
    )(w, g, m, v)


def _allreduce_small(v, name):
    rows = v.shape[0]
    flips = [(fx, fy, fc) for fx in (0, 1) for fy in (0, 1) for fc in (0, 1) if fx or fy or fc]

    def body(v_ref, o_ref, all_s, ssem, rsem):
        x, y, c = _place()

        def peer(f):
            return (x + f[0] - 2 * x * f[0], y + f[1] - 2 * y * f[1], c + f[2] - 2 * c * f[2])

        def slot(p):
            return all_s.at[4 * p[0] + 2 * p[1] + p[2]]

        def copy(k, f, owner):
            return pltpu.make_async_remote_copy(
                src_ref=v_ref, dst_ref=slot(owner), send_sem=ssem.at[k], recv_sem=rsem.at[k],
                device_id=peer(f), device_id_type=MESH)

        sends = [copy(k, f, (x, y, c)) for k, f in enumerate(flips)]
        for cp in sends:
            cp.start()
        all_s[4 * x + 2 * y + c] = v_ref[...]
        for k, f in enumerate(flips):
            copy(k, f, peer(f)).wait_recv()
        for cp in sends:
            cp.wait_send()
        acc = all_s[0]
        for d in range(1, 8):
            acc = acc + all_s[d]
        o_ref[...] = acc

    return pl.pallas_call(
        body, name=name,
        in_specs=[pl.BlockSpec(memory_space=pltpu.VMEM)],
        out_specs=pl.BlockSpec(memory_space=pltpu.VMEM),
        out_shape=_sds((rows, LANE), F32),
        scratch_shapes=[pltpu.VMEM((8, rows, LANE), F32), pltpu.SemaphoreType.DMA((7,)), pltpu.SemaphoreType.DMA((7,))],
    )(v)


BIG_NAMES = ("ffn1_w_gate", "ffn1_w_up", "ffn1_w_down", "w_in", "w_branch", "w_merge_gate", "w_out",
             "ffn2_w_gate", "ffn2_w_up", "ffn2_w_down")


FFN1 = ("ffn1_w_gate", "ffn1_w_up", "ffn1_w_down")
FFN2 = ("ffn2_w_gate", "ffn2_w_up", "ffn2_w_down")
MIX_IN = ("w_in", "w_merge_gate")
MIX_OUT = ("w_branch", "w_out")


def _keys(names, l):
    return [(n, l) for n in names]


def _local_step(x, tgt, small, convw_full, wx, n_layers):
    T, D = x.shape
    L = n_layers
    ns = N_SHARD
    dq = D // ns
    W = wx.w

    def hosted(call, keys, scatter=False):
        comm = wx.pieces(keys, scatter)
        main, extra = call(comm)
        if comm is not None:
            wx.arrived(keys, extra, scatter)
        return main

    def mixer_views(l):
        return _build_wbig(W[("w_merge_gate", l)], W[("w_in", l)], f"wbig_{l}")

    def out_views(l):
        wb4 = W[("w_branch", l)]
        wb = _copy_blocks(wb4, pl.BlockSpec((None, None, BRANCH_W, dq), lambda s_, i: (s_, i, 0, 0)),
                          _sds((3, BRANCH_W, D), wb4.dtype),
                          pl.BlockSpec((None, BRANCH_W, dq), lambda s_, i: (i, 0, s_)), (ns, 3), f"w_branch_whole_{l}")
        wo = W[("w_out", l)].reshape(D, D)
        return wb, wo

    tb = _ret_tables(T)
    rb_pad = jnp.pad(small["rel_bias"], ((0, 0), (0, 0), (0, RB_PAD - N_REL)))

    saved = []
    h = x
    for l in range(L):
        s = {"x0": h}
        nxt = l + 1
        x1, s["g1"], s["u1"] = hosted(
            lambda c: _ffn_fwd(h, small["ffn1_norm"][l][None], W[("ffn1_w_gate", l)], W[("ffn1_w_up", l)],
                               W[("ffn1_w_down", l)], f"ffn1_fwd_{l}", comm=c), _keys(MIX_IN, l))
        s["x1"] = x1
        s["wbig"] = mixer_views(l)
        big, s["h"] = _inproj_fwd(x1, small["mix_norm"][l][None], s["wbig"], f"inproj_fwd_{l}")
        s["big"] = big
        s["bias"] = jnp.transpose(_relbias_expand(rb_pad[l], f"relbias_expand_{l}"), (1, 0, 2))
        s["yc"] = _conv_fwd(big, convw_full[l], f"conv_fwd_{l}")
        s["yr"], s["o"], s["st"] = hosted(lambda c: _ret_fwd(big, tb, f"ret_fwd_{l}", comm=c), _keys(MIX_OUT, l))
        (s["ya"],) = hosted(lambda c: _att_fwd(big, s["bias"], f"att_fwd_{l}", comm=c), _keys(FFN2, l))
        s["wb"], s["wo"] = out_views(l)
        x2, s["p"], s["mg"] = _merge_fwd(x1, big, (s["yc"], s["yr"], s["ya"]), s["wb"], s["wo"], f"merge_fwd_{l}")
        s["x2"] = x2
        h, s["g2"], s["u2"] = hosted(
            lambda c: _ffn_fwd(x2, small["ffn2_norm"][l][None], W[("ffn2_w_gate", l)], W[("ffn2_w_up", l)],
                               W[("ffn2_w_down", l)], f"ffn2_fwd_{l}", comm=c), _keys(FFN1, nxt) if nxt < L else [])
        saved.append(s)

    loss_p, dx, d_final = _loss_head(h, tgt, small["final_norm"][None], "loss_head")

    gs = {"final_norm": d_final[0]}
    for k in ("ffn1_norm", "mix_norm", "ffn2_norm", "rel_bias", "conv_w"):
        gs[k] = [None] * L
    tk = min(2048, T)
    nk = T // tk

    def ffn_back(pre, l, dxo, x_in, g, u, first_keys, second_keys):
        nw = small[pre + "_norm"][l][None]
        dgv, duv, av, hb, dacc = hosted(
            lambda c: _ffn_bwd_hidden(dxo, x_in, nw, g, u, W[(pre + "_w_down", l)], f"{pre}_bwd_hidden_{l}", comm=c),
            first_keys, scatter=True)
        dxn, dn = hosted(
            lambda c: _ffn_bwd_resid(dgv, duv, W[(pre + "_w_gate", l)], W[(pre + "_w_up", l)], x_in, nw, dxo,
                                     f"{pre}_bwd_resid_{l}", comm=c),
            second_keys, scatter=True)
        gs[pre + "_norm"][l] = dn[0]
        return dxn, (hb, dgv, duv, av, dacc)

    def ffn_grads(pre, l, hb, dgv, duv, av, dacc, chain=False, carry=()):
        fs = dgv.shape[-1]
        hspec = pl.BlockSpec((tk, D), lambda p, q, k: (k, 0))
        sspec = pl.BlockSpec((None, tk, fs), lambda p, q, k: (p, k, 0))
        down_spec = pl.BlockSpec((None, fs, D), lambda p, q, k: (p, 0, 0))
        jobs = [(pre + "_w_gate", dgv, hb, sspec, hspec, (ns, fs, D), down_spec),
                (pre + "_w_up", duv, hb, sspec, hspec, (ns, fs, D), down_spec),
                (pre + "_w_down", av, dacc, sspec, hspec, (ns, fs, D), down_spec)]
        before = None
        for nm, a, b, a_spec, b_spec, shape, o_spec in jobs:
            def product(c):
                r = _tn(a, b, a_spec, b_spec, _sds(shape, BF), o_spec, (ns, 1, nk), f"d{nm}_{l}", comm=c)
                return (r, []) if c is None else r
            if before is None:
                keys = list(carry)
            else:
                keys = [before] if chain else []
            wx.g[(nm, l)] = hosted(product, keys, scatter=True)
            before = (nm, l)

    for l in reversed(range(L)):
        s = saved[l]
        above = _keys(FFN1, l + 1) if l + 1 < L else [None] * 3
        dx, parts = ffn_back("ffn2", l, dx, s["x2"], s["g2"], s["u2"], [k for k in above[:1] if k],
                             [k for k in above[1:2] if k])
        ffn_grads("ffn2", l, *parts, carry=[k for k in above[2:] if k])
        dp, dbig, dyc, dyr, dya, dxb = _merge_bwd(dx, s["big"], s["p"], s["wb"], s["wo"], f"merge_bwd_{l}")
        wx.g[("w_out", l)] = _tn(
            s["mg"], dxb, pl.BlockSpec((tk, dq), lambda p, q, k: (k, p)), pl.BlockSpec((tk, D), lambda p, q, k: (k, 0)),
            _sds((ns, dq, D), BF), pl.BlockSpec((None, dq, D), lambda p, q, k: (p, 0, 0)), (ns, 1, nk), f"dw_out_{l}")
        gb = None
        for i, yv in enumerate((s["yc"], s["yr"], s["ya"])):
            gb = _tn(yv, dp,
                     pl.BlockSpec((tk, BRANCH_W), lambda p, q, k: (k, 0)),
                     pl.BlockSpec((tk, dq), lambda p, q, k, i=i: (k, i * ns + p)),
                     _sds((ns, 3, BRANCH_W, dq), BF),
                     pl.BlockSpec((None, None, BRANCH_W, dq), lambda p, q, k, i=i: (p, i, 0, 0)),
                     (ns, 1, nk), f"dw_branch{i}_{l}", prev=gb)
        wx.g[("w_branch", l)] = gb
        dbig, dcw = _conv_bwd(s["big"], dyc, convw_full[l], dbig, f"conv_bwd_{l}")
        gs["conv_w"][l] = dcw
        dbig = _ret_bwd(s["big"], s["o"], s["st"], dyr, tb, dbig, f"ret_bwd_{l}")
        dbig, dbias = hosted(lambda c: _att_bwd(s["big"], s["bias"], dya, dbig, f"att_bwd_{l}", comm=c),
                             _keys(FFN2, l), scatter=True)
        gs["rel_bias"][l] = _relbias_grad(jnp.transpose(dbias, (1, 0, 2)), f"relbias_grad_{l}")[:, :N_REL]
        n_in = N_SEG * BRANCH_W
        bn = 1024 if (3 * D) % 1024 == 0 else BRANCH_W
        dwp = _tn(s["h"], dbig, pl.BlockSpec((tk, D), lambda p, q, k: (k, 0)),
                  pl.BlockSpec((tk, bn), lambda p, q, k: (k, 3 * D // bn + q)),
                  _sds((D, n_in), BF), pl.BlockSpec((D, bn), lambda p, q, k: (0, q)), (1, n_in // bn, nk), f"dw_in_{l}")
        wx.g[("w_in", l)] = _ungroup_dw_in(dwp, ns, f"dw_in_shards_{l}")
        wx.g[("w_merge_gate", l)] = _tn_gates(s["h"], dbig, ns, tk, f"dw_merge_gate_{l}")
        dx, dn = hosted(
            lambda c: _inproj_bwd(dbig, s["wbig"], s["x1"], small["mix_norm"][l][None], dx, f"inproj_bwd_{l}", comm=c),
            [("w_in", l)], scatter=True)
        gs["mix_norm"][l] = dn[0]
        dx, parts = ffn_back("ffn1", l, dx, s["x0"], s["g1"], s["u1"],
                             [("w_merge_gate", l), ("w_branch", l), ("w_out", l)], [])
        ffn_grads("ffn1", l, *parts, chain=(l == 0))

    for k in ("ffn1_norm", "mix_norm", "ffn2_norm", "rel_bias", "conv_w"):
        gs[k] = jnp.stack(gs[k])
    return loss_p, dx, gs


class _Exchange:
    def __init__(self, shards):
        self.shards = shards
        self.w = {}
        self.g = {}
        self.landed = {}

    def own(self, key):
        return self.shards[key[0]][key[1]].astype(BF)

    def pieces(self, keys, scatter):
        if not keys:
            return None
        if scatter:
            return _Scatter([self.g[k] for k in keys])
        return _HalfGather([_halves(self.own(k)) for k in keys])

    def arrived(self, keys, outs, scatter):
        for k, o in zip(keys, outs):
            if scatter:
                self.landed[k] = o
            else:
                self.w[k] = o.reshape((N_SHARD,) + self.shards[k[0]].shape[1:])


def _halves(a):
    return a.reshape(2, -1, a.shape[-1])


TRANSPOSED_GRADS = ("ffn1_w_gate", "ffn1_w_up", "ffn2_w_gate", "ffn2_w_up")
W_NAMES = ("ffn1_norm", "ffn1_w_gate", "ffn1_w_up", "ffn1_w_down", "mix_norm", "w_in", "conv_w", "rel_bias", "w_branch",
           "w_merge_gate", "w_out", "ffn2_norm", "ffn2_w_gate", "ffn2_w_up", "ffn2_w_down", "final_norm")


def _as2d(a):
    return a.reshape(1, -1) if a.ndim == 1 else a.reshape(-1, a.shape[-1])


def kernel(x, ffn1_norm, ffn1_w_gate, ffn1_w_up, ffn1_w_down, mix_norm, w_in, conv_w, rel_bias, w_branch, w_merge_gate, w_out, ffn2_norm, ffn2_w_gate, ffn2_w_up, ffn2_w_down, final_norm, loss_target, m_ffn1_norm, m_ffn1_w_gate, m_ffn1_w_up, m_ffn1_w_down, m_mix_norm, m_w_in, m_conv_w, m_rel_bias, m_w_branch, m_w_merge_gate, m_w_out, m_ffn2_norm, m_ffn2_w_gate, m_ffn2_w_up, m_ffn2_w_down, m_final_norm, v_ffn1_norm, v_ffn1_w_gate, v_ffn1_w_up, v_ffn1_w_down, v_mix_norm, v_w_in, v_conv_w, v_rel_bias, v_w_branch, v_w_merge_gate, v_w_out, v_ffn2_norm, v_ffn2_w_gate, v_ffn2_w_up, v_ffn2_w_down, v_final_norm):
    given = dict(locals())
    w = {n: given[n] for n in W_NAMES}
    m = {n: given["m_" + n] for n in W_NAMES}
    v = {n: given["v_" + n] for n in W_NAMES}
    my_chip = 2 * lax.axis_index("x") + lax.axis_index("y")
    L = w_in.shape[0]

    wx = _Exchange({n: jnp.swapaxes(w[n], 1, 2) if n in TRANSPOSED_GRADS else w[n] for n in BIG_NAMES})
    first = _keys(FFN1, 0)
    got = _comm_alone(_HalfGather([_halves(wx.own(k)) for k in first] + [_halves(conv_w)]), "gather_first")
    wx.arrived(first, got[:-1], False)
    convw_full = jnp.transpose(got[-1].reshape((N_SHARD,) + conv_w.shape), (1, 2, 0, 3)).reshape(
        conv_w.shape[0], conv_w.shape[1], -1)

    small = {n: w[n] for n in ("ffn1_norm", "mix_norm", "ffn2_norm", "final_norm", "rel_bias")}
    loss_p, grad_x, gs = _local_step(x[0], loss_target[0], small, convw_full, wx, L)
    last = [(FFN1[-1], 0)]
    wx.arrived(last, _comm_alone(wx.pieces(last, True), "scatter_last"), True)

    sums = []
    for n in BIG_NAMES:
        acc = None
        for l in range(L):
            a = wx.landed[(n, l)]
            acc, _ = _sum8(a.reshape(a.shape[0], -1, a.shape[-1]), l, L, f"sum8_{n}_{l}", prev=acc)
        sums.append(acc.reshape(-1, acc.shape[-1]))

    parts = [gs["ffn1_norm"].reshape(-1), gs["mix_norm"].reshape(-1), gs["ffn2_norm"].reshape(-1),
             gs["final_norm"].reshape(-1), gs["rel_bias"].reshape(-1), gs["conv_w"].reshape(-1), loss_p[0]]
    sizes = [p.shape[0] for p in parts]
    flat = jnp.concatenate(parts)
    rows = -(-flat.shape[0] // (8 * LANE)) * 8
    flat = jnp.pad(flat, (0, rows * LANE - flat.shape[0])).reshape(rows, LANE)
    red = _allreduce_small(flat, "allreduce_small").reshape(-1)
    offs = [0]
    for sz in sizes:
        offs.append(offs[-1] + sz)
    sm = {}
    for i, n in enumerate(("ffn1_norm", "mix_norm", "ffn2_norm", "final_norm", "rel_bias", "conv_w")):
        sm[n] = red[offs[i]:offs[i + 1]]
    loss = red[offs[6]]
    sm["conv_w"] = lax.dynamic_slice_in_dim(sm["conv_w"].reshape(conv_w.shape[0], conv_w.shape[1], -1),
                                            my_chip * conv_w.shape[2], conv_w.shape[2], axis=2)

    grads, deltas, new_m, new_v = {}, {}, {}, {}
    big_sum = dict(zip(BIG_NAMES, sums))
    for n in W_NAMES:
        flip = n in TRANSPOSED_GRADS

        def view(a):
            return jnp.swapaxes(a, 1, 2) if flip else a

        shape = view(w[n]).shape
        g = big_sum[n] if n in big_sum else _as2d(sm[n].reshape(shape))
        out = _adamw(_as2d(view(w[n])), g, _as2d(view(m[n])), _as2d(view(v[n])), f"adamw_{n}")
        grads[n], deltas[n], new_m[n], new_v[n] = (view(o.reshape(shape)) for o in [g] + list(out))

    return (loss, grad_x[None], *[grads[n] for n in W_NAMES], *[deltas[n] for n in W_NAMES],
            *[new_m[n] for n in W_NAMES], *[new_v[n] for n in W_NAMES])
```

```python
import functools
import math

import jax
import jax.numpy as jnp
from jax import lax
from jax.experimental import pallas as pl
from jax.experimental.pallas import tpu as pltpu

F32 = jnp.float32
BF = jnp.bfloat16
MESH = pl.DeviceIdType.MESH
ARB = "arbitrary"
PAR = "parallel"

EPS = 1e-6
NEG_INF = -1e30
ROPE_BASE = 10000.0
CHUNK = 64
BRANCH_W = 512
H_RET = 4
DK_RET = 128
H_ATT = 8
DH_ATT = 64
N_PREV = 8
REL_CLIP = 128
N_REL = 2 * REL_CLIP + 1
N_SHARD = 4
LANE = 128
RET_L = 512
ATT_TQ = 128
ATT_SUB = 4
ATT_PAD = N_PREV * CHUNK
ATT_SPAN = ATT_TQ + ATT_PAD
ATT_TOEP = 2 * REL_CLIP
RB_PAD = 264
TM = 512
TM_FFN = 1024

ADAM_LR = 0.001
ADAM_B1 = 0.9
ADAM_B2 = 0.999
ADAM_EPS = 1e-08
ADAM_WD = 0.01
ADAM_STEP = 10

NT_DIMS = (((1,), (1,)), ((), ()))
TN_DIMS = (((0,), (0,)), ((), ()))


def _cp(sem, vmem_mb=48):
    return pltpu.CompilerParams(dimension_semantics=sem, vmem_limit_bytes=vmem_mb << 20)


def _sds(shape, dtype):
    return jax.ShapeDtypeStruct(tuple(shape), dtype)


def _rms_r(x):
    return lax.rsqrt(jnp.mean(x * x, axis=-1, keepdims=True) + EPS)


def _sigmoid(x):
    return 0.5 * jnp.tanh(0.5 * x) + 0.5


def _rms_bwd(dh, xv, nw):
    r = _rms_r(xv)
    xh = xv * r
    dxh = dh * nw
    dx = r * (dxh - xh * jnp.mean(dxh * xh, axis=-1, keepdims=True))
    return dx, jnp.sum(dh * xh, axis=0, keepdims=True)


def _place():
    return lax.axis_index("x"), lax.axis_index("y"), lax.axis_index("c")


def _other_chips(x, y):
    return [(1 - x, y), (x, 1 - y), (1 - x, 1 - y)]


class _Scatter:
    def __init__(self, srcs):
        self.srcs = list(srcs)
        n = len(self.srcs)
        self.out_shape = [_sds((2 * N_SHARD,) + s.shape[1:], s.dtype) for s in self.srcs]
        self.scratch = [pltpu.SemaphoreType.DMA((n,)), pltpu.SemaphoreType.DMA((3, n)), pltpu.SemaphoreType.DMA((3, n)),
                        pltpu.SemaphoreType.DMA((4, n)), pltpu.SemaphoreType.DMA((4, n))]

    def _plan(self, src, dst, sems, want):
        lsem, s1, r1, s2, r2 = sems
        x, y, c = _place()
        mine = 2 * x + y
        n = len(src)
        chips = list(enumerate(_other_chips(x, y)))

        def copy(s_ref, d_ref, ssem, rsem, to):
            return pltpu.make_async_remote_copy(src_ref=s_ref, dst_ref=d_ref, send_sem=ssem, recv_sem=rsem,
                                                device_id=to, device_id_type=MESH)

        local = [pltpu.make_async_copy(src[k].at[mine], dst[k].at[3], lsem.at[k]) for k in range(n)
                 ] if "local" in want else []
        sends = [copy(src[k].at[2 * ch[0] + ch[1]], dst[k].at[j], s1.at[j, k], r1.at[j, k], (ch[0], ch[1], c))
                 for j, ch in chips for k in range(n)] if "sends" in want else []
        passes = [copy(dst[k].at[j], dst[k].at[4 + j], s2.at[j, k], r2.at[j, k], (x, y, 1 - c))
                  for j, ch in chips for k in range(n)] if "passes" in want else []
        own_pass = [copy(src[k].at[mine], dst[k].at[7], s2.at[3, k], r2.at[3, k], (x, y, 1 - c))
                    for k in range(n)] if "own_pass" in want else []
        return local, sends, passes, own_pass

    def start(self, src, dst, sems):
        local, sends, _, own_pass = self._plan(src, dst, sems, ("local", "sends", "own_pass"))
        for cp in local + sends + own_pass:
            cp.start()

    def relay(self, src, dst, sems):
        _, sends, passes, _ = self._plan(src, dst, sems, ("sends", "passes"))
        for land, fwd in zip(sends, passes):
            land.wait_recv()
            fwd.start()

    def finish(self, src, dst, sems):
        local, sends, passes, own_pass = self._plan(src, dst, sems, ("local", "sends", "passes", "own_pass"))
        for cp in passes + own_pass:
            cp.wait_recv()
        for cp in sends + passes + own_pass:
            cp.wait_send()
        for cp in local:
            cp.wait()

    def wait(self, src, dst, sems):
        self.relay(src, dst, sems)
        self.finish(src, dst, sems)


class _HalfGather:
    def __init__(self, srcs):
        self.srcs = list(srcs)
        n = len(self.srcs)
        self.out_shape = [_sds((N_SHARD,) + s.shape, s.dtype) for s in self.srcs]
        self.scratch = [pltpu.SemaphoreType.DMA((n,))] + [pltpu.SemaphoreType.DMA((3, n)) for _ in range(4)]

    def _plan(self, src, dst, sems, want):
        lsem, s1, r1, s2, r2 = sems
        x, y, c = _place()
        mine = 2 * x + y
        n = len(src)
        chips = [(j, ch, 2 * ch[0] + ch[1]) for j, ch in enumerate(_other_chips(x, y))]

        def copy(s_ref, d_ref, ssem, rsem, to):
            return pltpu.make_async_remote_copy(src_ref=s_ref, dst_ref=d_ref, send_sem=ssem, recv_sem=rsem,
                                                device_id=to, device_id_type=MESH)

        def over(kind, make):
            return [make(j, ch, slot, k) for j, ch, slot in chips for k in range(n)] if kind in want else []

        local = [pltpu.make_async_copy(src[k], dst[k].at[mine], lsem.at[k]) for k in range(n)] if "local" in want else []
        sends = over("sends", lambda j, ch, slot, k: copy(src[k].at[c], dst[k].at[mine, c], s1.at[j, k], r1.at[j, k],
                                                          (ch[0], ch[1], c)))
        lands = over("lands", lambda j, ch, slot, k: copy(src[k].at[c], dst[k].at[slot, c], s1.at[j, k], r1.at[j, k],
                                                          (ch[0], ch[1], c)))
        passes = over("passes", lambda j, ch, slot, k: copy(dst[k].at[slot, c], dst[k].at[slot, c], s2.at[j, k],
                                                            r2.at[j, k], (x, y, 1 - c)))
        gets = over("gets", lambda j, ch, slot, k: copy(dst[k].at[slot, 1 - c], dst[k].at[slot, 1 - c], s2.at[j, k],
                                                        r2.at[j, k], (x, y, 1 - c)))
        return local, sends, lands, passes, gets

    def start(self, src, dst, sems):
        lsem, s1, r1, s2, r2 = sems
        x, y, c = _place()
        mine = 2 * x + y
        for k in range(len(src)):
            pltpu.make_async_copy(src[k], dst[k].at[mine], lsem.at[k]).start()
        for j, ch in enumerate(_other_chips(x, y)):
            for k in range(len(src)):
                pltpu.make_async_remote_copy(
                    src_ref=src[k].at[c], dst_ref=dst[k].at[mine, c], send_sem=s1.at[j, k], recv_sem=r1.at[j, k],
                    device_id=(ch[0], ch[1], c), device_id_type=MESH).start()

    def relay(self, src, dst, sems):
        _, _, lands, passes, _ = self._plan(src, dst, sems, ("lands", "passes"))
        for land, fwd in zip(lands, passes):
            land.wait_recv()
            fwd.start()

    def finish(self, src, dst, sems):
        local, sends, _, passes, gets = self._plan(src, dst, sems, ("local", "sends", "passes", "gets"))
        for cp in gets:
            cp.wait_recv()
        for cp in sends + passes:
            cp.wait_send()
        for cp in local:
            cp.wait()

    def wait(self, src, dst, sems):
        self.relay(src, dst, sems)
        self.finish(src, dst, sems)


def _call(body, *, name, args, in_specs, out_specs, out_shape, grid=(), scratch_shapes=(), sem=None, comm=None,
          aliases=None, vmem_mb=48):
    in_specs, out_specs, out_shape = list(in_specs), list(out_specs), list(out_shape)
    scratch, args = list(scratch_shapes), list(args)
    n_in, n_out, n_scr = len(in_specs), len(out_specs), len(scratch)
    if comm is None:
        def kernel_body(*refs):
            body(*refs)
    else:
        c_in, c_out = len(comm.srcs), len(comm.out_shape)

        def kernel_body(*refs):
            o0 = n_in + c_in
            s0 = o0 + n_out + c_out
            cin, cout, sems = refs[n_in:o0], refs[o0 + n_out:s0], refs[s0 + n_scr:]
            main = refs[:n_in] + refs[o0:o0 + n_out] + refs[s0:s0 + n_scr]
            if grid:
                ids = [pl.program_id(a) for a in range(len(grid))]
                first = functools.reduce(lambda p, q: p & q, [i == 0 for i in ids])
                last = functools.reduce(lambda p, q: p & q, [i == g - 1 for i, g in zip(ids, grid)])

                @pl.when(first)
                def _():
                    comm.start(cin, cout, sems)

                body(*main)

                steps = math.prod(grid)
                if hasattr(comm, "relay") and steps >= 4:
                    flat = functools.reduce(lambda p, q: p + q, [i * math.prod(grid[a + 1:]) for a, i in enumerate(ids)])

                    @pl.when(flat == (5 * steps) // 6)
                    def _():
                        comm.relay(cin, cout, sems)

                    @pl.when(last)
                    def _():
                        comm.finish(cin, cout, sems)
                else:
                    @pl.when(last)
                    def _():
                        comm.wait(cin, cout, sems)
            else:
                comm.start(cin, cout, sems)
                body(*main)
                comm.wait(cin, cout, sems)

        hbm = pl.BlockSpec(memory_space=pl.ANY)
        in_specs += [hbm] * c_in
        out_specs += [hbm] * c_out
        out_shape += comm.out_shape
        scratch += comm.scratch
        args += comm.srcs
    params = dict(vmem_limit_bytes=vmem_mb << 20)
    if grid:
        params["dimension_semantics"] = sem
    outs = pl.pallas_call(
        kernel_body, name=name, grid=grid, in_specs=in_specs, out_specs=out_specs, out_shape=out_shape,
        scratch_shapes=scratch, input_output_aliases=aliases or {}, compiler_params=pltpu.CompilerParams(**params),
    )(*args)
    return list(outs[:n_out]), list(outs[n_out:])


def _comm_alone(comm, name):
    return _call(lambda: None, name=name, args=[], in_specs=[], out_specs=[], out_shape=[], comm=comm)[1]


def _ffn_fwd(x, nw, wg, wu, wd, name, comm=None):
    T, D = x.shape
    ns, fs, _ = wg.shape
    tm = min(TM_FFN, T)

    def body(x_ref, nw_ref, wg_ref, wu_ref, wd_ref, xo_ref, g_ref, u_ref, h_s, acc_s):
        j = pl.program_id(1)

        @pl.when(j == 0)
        def _():
            xv = x_ref[...]
            h_s[...] = (xv * _rms_r(xv) * nw_ref[...]).astype(BF)
            acc_s[...] = jnp.zeros_like(acc_s)

        h = h_s[...]
        gb = lax.dot_general(h, wg_ref[...], NT_DIMS, preferred_element_type=F32).astype(BF)
        ub = lax.dot_general(h, wu_ref[...], NT_DIMS, preferred_element_type=F32).astype(BF)
        g_ref[...] = gb
        u_ref[...] = ub
        g = gb.astype(F32)
        a = (g * _sigmoid(g) * ub.astype(F32)).astype(BF)
        acc_s[...] += jnp.dot(a, wd_ref[...], preferred_element_type=F32)

        @pl.when(j == ns - 1)
        def _():
            xo_ref[...] = x_ref[...] + 0.5 * acc_s[...]

    wspec = pl.BlockSpec((None, fs, D), lambda i, j: (j, 0, 0))
    return _call(
        body, name=name, grid=(T // tm, ns), args=(x, nw, wg, wu, wd), comm=comm, vmem_mb=56,
        in_specs=[pl.BlockSpec((tm, D), lambda i, j: (i, 0)),
                  pl.BlockSpec((1, D), lambda i, j: (0, 0)),
                  wspec, wspec,
                  pl.BlockSpec((None, fs, D), lambda i, j: (j, 0, 0))],
        out_specs=[pl.BlockSpec((tm, D), lambda i, j: (i, 0)),
                   pl.BlockSpec((None, tm, fs), lambda i, j: (j, i, 0)),
                   pl.BlockSpec((None, tm, fs), lambda i, j: (j, i, 0))],
        out_shape=[_sds((T, D), F32), _sds((ns, T, fs), BF), _sds((ns, T, fs), BF)],
        scratch_shapes=[pltpu.VMEM((tm, D), BF), pltpu.VMEM((tm, D), F32)],
        sem=(ARB, ARB))


def _ffn_bwd_hidden(dxo, x, nw, g, u, wd, name, comm=None):
    T, D = x.shape
    ns, fs, _ = wd.shape
    tm = min(TM_FFN, T)

    def body(dxo_ref, x_ref, nw_ref, g_ref, u_ref, wd_ref, dg_ref, du_ref, a_ref, h_ref, dacc_ref, dacc_s):
        @pl.when(pl.program_id(1) == 0)
        def _():
            xv = x_ref[...]
            h_ref[...] = (xv * _rms_r(xv) * nw_ref[...]).astype(BF)
            db = (0.5 * dxo_ref[...]).astype(BF)
            dacc_ref[...] = db
            dacc_s[...] = db

        da = lax.dot_general(dacc_s[...], wd_ref[...], NT_DIMS, preferred_element_type=F32)
        gv = g_ref[...].astype(F32)
        uv = u_ref[...].astype(F32)
        s = _sigmoid(gv)
        sg = gv * s
        a_ref[...] = (sg * uv).astype(BF)
        du_ref[...] = (da * sg).astype(BF)
        dg_ref[...] = (da * uv * (s * (1.0 + gv * (1.0 - s)))).astype(BF)

    tok = pl.BlockSpec((tm, D), lambda i, j: (i, 0))
    hid = pl.BlockSpec((None, tm, fs), lambda i, j: (j, i, 0))
    return _call(
        body, name=name, grid=(T // tm, ns), args=(dxo, x, nw, g, u, wd), comm=comm, vmem_mb=56,
        in_specs=[tok, tok, pl.BlockSpec((1, D), lambda i, j: (0, 0)), hid, hid,
                  pl.BlockSpec((None, fs, D), lambda i, j: (j, 0, 0))],
        out_specs=[hid, hid, hid, tok, tok],
        out_shape=[_sds((ns, T, fs), BF)] * 3 + [_sds((T, D), BF)] * 2,
        scratch_shapes=[pltpu.VMEM((tm, D), BF)],
        sem=(ARB, ARB))


def _ffn_bwd_resid(dg, du, wg, wu, x, nw, dxo, name, comm=None):
    T, D = x.shape
    ns, fs, _ = wg.shape
    tm = min(TM_FFN, T)

    def body(dg_ref, du_ref, wg_ref, wu_ref, x_ref, nw_ref, dxo_ref, dx_ref, dnw_ref, acc_s):
        i = pl.program_id(0)
        j = pl.program_id(1)
        prod = (jnp.dot(dg_ref[...], wg_ref[...], preferred_element_type=F32)
                + jnp.dot(du_ref[...], wu_ref[...], preferred_element_type=F32))

        @pl.when((i == 0) & (j == 0))
        def _():
            dnw_ref[...] = jnp.zeros_like(dnw_ref)

        @pl.when(j == 0)
        def _():
            acc_s[...] = prod

        @pl.when(j > 0)
        def _():
            acc_s[...] += prod

        @pl.when(j == ns - 1)
        def _():
            dx, dn = _rms_bwd(acc_s[...], x_ref[...], nw_ref[...])
            dx_ref[...] = dxo_ref[...] + dx
            dnw_ref[...] += dn

    tok = pl.BlockSpec((tm, D), lambda i, j: (i, 0))
    row = pl.BlockSpec((1, D), lambda i, j: (0, 0))
    hid = pl.BlockSpec((None, tm, fs), lambda i, j: (j, i, 0))
    wspec = pl.BlockSpec((None, fs, D), lambda i, j: (j, 0, 0))
    return _call(
        body, name=name, grid=(T // tm, ns), args=(dg, du, wg, wu, x, nw, dxo), comm=comm, vmem_mb=56,
        in_specs=[hid, hid, wspec, wspec, tok, row, tok],
        out_specs=[tok, row],
        out_shape=[_sds((T, D), F32), _sds((1, D), F32)],
        scratch_shapes=[pltpu.VMEM((tm, D), F32)],
        sem=(ARB, ARB))


def _tn(a, b, a_spec, b_spec, out_shape, out_spec, grid, name, prev=None, comm=None):
    nk = grid[-1]
    acc_shape = tuple(d for d in out_spec.block_shape if d is not None)

    def body(*refs):
        a_ref, b_ref = refs[0], refs[1]
        o_ref, acc = refs[-2], refs[-1]
        k = pl.program_id(2)
        prod = lax.dot_general(a_ref[...], b_ref[...], TN_DIMS, preferred_element_type=F32)

        @pl.when(k == 0)
        def _():
            acc[...] = prod

        @pl.when(k > 0)
        def _():
            acc[...] += prod

        @pl.when(k == nk - 1)
        def _():
            o_ref[...] = acc[...].astype(o_ref.dtype)

    in_specs = [a_spec, b_spec]
    args = [a, b]
    aliases = {}
    if prev is not None:
        in_specs.append(pl.BlockSpec(memory_space=pl.ANY))
        args.append(prev)
        aliases = {2: 0}
    main, extra = _call(
        body, name=name, grid=grid, args=args, in_specs=in_specs, out_specs=[out_spec], out_shape=[out_shape],
        scratch_shapes=[pltpu.VMEM(acc_shape, F32)], aliases=aliases, sem=(ARB, ARB, ARB), comm=comm)
    return main[0] if comm is None else (main[0], extra)


def _tn_gates(h, dbig, ns, tk, name):
    T, D = h.shape
    dq = D // ns
    nk = T // tk

    def body(a_ref, b_ref, o_ref, acc):
        k = pl.program_id(1)
        prod = lax.dot_general(a_ref[...], b_ref[...], TN_DIMS, preferred_element_type=F32)

        @pl.when(k == 0)
        def _():
            acc[...] = prod

        @pl.when(k > 0)
        def _():
            acc[...] += prod

        @pl.when(k == nk - 1)
        def _():
            for s in range(ns):
                o_ref[s] = acc[s * dq:(s + 1) * dq, :].astype(o_ref.dtype)

    return pl.pallas_call(
        body, name=name, grid=(3, nk),
        in_specs=[pl.BlockSpec((tk, D), lambda q, k: (k, 0)), pl.BlockSpec((tk, D), lambda q, k: (k, q))],
        out_specs=pl.BlockSpec((ns, None, dq, D), lambda q, k: (0, q, 0, 0)),
        out_shape=_sds((ns, 3, dq, D), BF),
        scratch_shapes=[pltpu.VMEM((D, D), F32)],
        compiler_params=_cp((PAR, ARB)),
    )(h, dbig)


def _inproj_fwd(x, nw, wbig, name):
    T, D = x.shape
    nb = wbig.shape[-1]
    tm = min(2 * TM, T)
    bn = min(2048, nb)

    def body(x_ref, nw_ref, w_ref, o_ref, h_ref, h_s):
        @pl.when(pl.program_id(1) == 0)
        def _():
            xv = x_ref[...]
            hb = (xv * _rms_r(xv) * nw_ref[...]).astype(BF)
            h_s[...] = hb
            h_ref[...] = hb

        o_ref[...] = jnp.dot(h_s[...], w_ref[...], preferred_element_type=F32).astype(BF)

    return pl.pallas_call(
        body, name=name, grid=(T // tm, nb // bn),
        in_specs=[pl.BlockSpec((tm, D), lambda i, n: (i, 0)),
                  pl.BlockSpec((1, D), lambda i, n: (0, 0)),
                  pl.BlockSpec((D, bn), lambda i, n: (0, n))],
        out_specs=[pl.BlockSpec((tm, bn), lambda i, n: (i, n)),
                   pl.BlockSpec((tm, D), lambda i, n: (i, 0))],
        out_shape=[_sds((T, nb), BF), _sds((T, D), BF)],
        scratch_shapes=[pltpu.VMEM((tm, D), BF)],
        compiler_params=_cp((PAR, ARB)),
    )(x, nw, wbig)


def _inproj_bwd(dbig, wbig, x, nw, dxin, name, comm=None):
    T, D = x.shape
    nb = wbig.shape[-1]
    tm = min(TM_FFN, T)
    tk = min(2048, nb)
    nk = nb // tk

    def body(a_ref, w_ref, x_ref, nw_ref, dxin_ref, dx_ref, dnw_ref, acc_s):
        i = pl.program_id(0)
        k = pl.program_id(1)
        prod = lax.dot_general(a_ref[...], w_ref[...], NT_DIMS, preferred_element_type=F32)

        @pl.when((i == 0) & (k == 0))
        def _():
            dnw_ref[...] = jnp.zeros_like(dnw_ref)

        @pl.when(k == 0)
        def _():
            acc_s[...] = prod

        @pl.when(k > 0)
        def _():
            acc_s[...] += prod

        @pl.when(k == nk - 1)
        def _():
            dx, dn = _rms_bwd(acc_s[...], x_ref[...], nw_ref[...])
            dx_ref[...] = dxin_ref[...] + dx
            dnw_ref[...] += dn

    tok = pl.BlockSpec((tm, D), lambda i, k: (i, 0))
    row = pl.BlockSpec((1, D), lambda i, k: (0, 0))
    return _call(
        body, name=name, grid=(T // tm, nk), args=(dbig, wbig, x, nw, dxin), comm=comm, vmem_mb=56,
        in_specs=[pl.BlockSpec((tm, tk), lambda i, k: (i, k)),
                  pl.BlockSpec((D, tk), lambda i, k: (0, k)),
                  tok, row, tok],
        out_specs=[tok, row],
        out_shape=[_sds((T, D), F32), _sds((1, D), F32)],
        scratch_shapes=[pltpu.VMEM((tm, D), F32)],
        sem=(ARB, ARB))


CONV_R = 512
CONV_BASE, CONV_GROUP = 0, 3
ATT_BASE, ATT_GROUP = 12, 3
RET_BASE, RET_GROUP = 24, 4
N_SEG = 10


N_IN_BLOCKS = N_SEG * BRANCH_W // LANE


def _orig_block(p):
    nblk = BRANCH_W // LANE
    qa, qr = p - ATT_BASE, p - RET_BASE
    conv = (p % CONV_GROUP) * nblk + p // CONV_GROUP
    att = (7 + qa % ATT_GROUP) * nblk + qa // ATT_GROUP
    ret = (3 + qr % RET_GROUP) * nblk + qr // RET_GROUP
    return jnp.where(p < ATT_BASE, conv, jnp.where(p < RET_BASE, att, ret))


def _copy_blocks(src, in_spec, out_shape, out_spec, grid, name, prev=None):
    def body(*refs):
        refs[-1][...] = refs[0][...]

    in_specs, args, aliases = [in_spec], [src], {}
    if prev is not None:
        in_specs.append(pl.BlockSpec(memory_space=pl.ANY))
        args.append(prev)
        aliases = {1: 0}
    return pl.pallas_call(
        body, name=name, grid=grid, in_specs=in_specs, out_specs=out_spec, out_shape=out_shape,
        input_output_aliases=aliases, compiler_params=_cp(tuple(PAR for _ in grid)),
    )(*args)


def _build_wbig(gates4, win4, name):
    ns, _, dq, D = gates4.shape
    per = win4.shape[-1] // LANE
    shape = _sds((D, 3 * D + N_IN_BLOCKS * LANE), gates4.dtype)
    out = _copy_blocks(gates4, pl.BlockSpec((None, None, dq, D), lambda s, i: (s, i, 0, 0)), shape,
                       pl.BlockSpec((dq, D), lambda s, i: (s, i)), (ns, 3), name + "_gates")
    return _copy_blocks(
        win4, pl.BlockSpec((None, D, LANE), lambda p: (_orig_block(p) // per, 0, _orig_block(p) % per)), shape,
        pl.BlockSpec((D, LANE), lambda p: (0, 3 * D // LANE + p)), (N_IN_BLOCKS,), name + "_in", prev=out)


def _ungroup_dw_in(dwp, ns, name):
    D = dwp.shape[0]
    per = N_IN_BLOCKS // ns
    return _copy_blocks(
        dwp, pl.BlockSpec((D, LANE), lambda p: (0, p)), _sds((ns, D, per * LANE), dwp.dtype),
        pl.BlockSpec((None, D, LANE), lambda p: (_orig_block(p) // per, 0, _orig_block(p) % per)), (N_IN_BLOCKS,), name)


def _seg0(big):
    return (big.shape[1] - N_SEG * BRANCH_W) // LANE


def _group_spec(big, base, group, rows, where):
    first = (_seg0(big) + base) // group
    assert first * group == _seg0(big) + base

    def index(*ids):
        r, g = where(*ids)
        return r, first + g

    return pl.BlockSpec((rows, group * LANE), index)


CU, CB, CC = (slice(k * LANE, (k + 1) * LANE) for k in range(3))
AQ, AK, AV = CU, CB, CC
RQ, RK, RV, RG = (slice(k * LANE, (k + 1) * LANE) for k in range(4))


def _conv_fwd(big, cw, name):
    T = big.shape[0]
    R = min(CONV_R, T)

    def body(g_ref, w_ref, y_ref, z_s):
        z_s[pl.ds(0, 8), :] = jnp.zeros((8, LANE), F32)

        def fill(t, c):
            sl = pl.ds(pl.multiple_of(t * R, R), R)
            z_s[pl.ds(pl.multiple_of(t * R + 8, 8), R), :] = g_ref[sl, CC].astype(F32) * g_ref[sl, CU].astype(F32)
            return c

        lax.fori_loop(0, T // R, fill, 0)
        w0, w1, w2 = w_ref[0:1, :], w_ref[1:2, :], w_ref[2:3, :]

        def step(t, c):
            zz = z_s[pl.ds(pl.multiple_of(t * R, R), R + 8), :]
            z0 = zz[8:]
            z1 = pltpu.roll(zz, 1, 0)[8:]
            z2 = pltpu.roll(zz, 2, 0)[8:]
            sl = pl.ds(pl.multiple_of(t * R, R), R)
            y_ref[sl, :] = (g_ref[sl, CB].astype(F32) * (w2 * z0 + w1 * z1 + w0 * z2)).astype(BF)
            return c

        lax.fori_loop(0, T // R, step, 0)

    return pl.pallas_call(
        body, name=name, grid=(BRANCH_W // LANE,),
        in_specs=[_group_spec(big, CONV_BASE, CONV_GROUP, T, lambda j: (0, j)),
                  pl.BlockSpec((3, LANE), lambda j: (0, j))],
        out_specs=pl.BlockSpec((T, LANE), lambda j: (0, j)),
        out_shape=_sds((T, BRANCH_W), BF),
        scratch_shapes=[pltpu.VMEM((T + 8, LANE), F32)],
        compiler_params=_cp((PAR,)),
    )(big, cw)


def _conv_bwd(big, dy, cw, dbig, name):
    T = big.shape[0]
    R = min(CONV_R, T)

    def body(g_ref, dy_ref, w_ref, _, o_ref, dw_ref, z_s, d_s):
        z_s[pl.ds(0, 8), :] = jnp.zeros((8, LANE), F32)
        d_s[pl.ds(T, 8), :] = jnp.zeros((8, LANE), F32)

        def fill(t, c):
            sl = pl.ds(pl.multiple_of(t * R, R), R)
            z_s[pl.ds(pl.multiple_of(t * R + 8, 8), R), :] = g_ref[sl, CC].astype(F32) * g_ref[sl, CU].astype(F32)
            d_s[sl, :] = dy_ref[sl, :].astype(F32) * g_ref[sl, CB].astype(F32)
            return c

        lax.fori_loop(0, T // R, fill, 0)
        w0, w1, w2 = w_ref[0:1, :], w_ref[1:2, :], w_ref[2:3, :]

        def step(t, carry):
            a0, a1, a2 = carry
            zz = z_s[pl.ds(pl.multiple_of(t * R, R), R + 8), :]
            z0 = zz[8:]
            z1 = pltpu.roll(zz, 1, 0)[8:]
            z2 = pltpu.roll(zz, 2, 0)[8:]
            sl = pl.ds(pl.multiple_of(t * R, R), R)
            dyv = dy_ref[sl, :].astype(F32)
            o_ref[sl, CB] = (dyv * (w2 * z0 + w1 * z1 + w0 * z2)).astype(BF)
            dd = d_s[pl.ds(pl.multiple_of(t * R, R), R + 8), :]
            d0 = dd[:R]
            d1 = pltpu.roll(dd, R + 7, 0)[:R]
            d2 = pltpu.roll(dd, R + 6, 0)[:R]
            dz = w2 * d0 + w1 * d1 + w0 * d2
            o_ref[sl, CC] = (dz * g_ref[sl, CU].astype(F32)).astype(BF)
            o_ref[sl, CU] = (dz * g_ref[sl, CC].astype(F32)).astype(BF)
            a0 = a0 + jnp.sum(d0 * z2, axis=0, keepdims=True)
            a1 = a1 + jnp.sum(d0 * z1, axis=0, keepdims=True)
            a2 = a2 + jnp.sum(d0 * z0, axis=0, keepdims=True)
            return a0, a1, a2

        zero = jnp.zeros((1, LANE), F32)
        a0, a1, a2 = lax.fori_loop(0, T // R, step, (zero, zero, zero))
        dw_ref[0:1, :] = a0
        dw_ref[1:2, :] = a1
        dw_ref[2:3, :] = a2

    group = _group_spec(big, CONV_BASE, CONV_GROUP, T, lambda j: (0, j))
    w = pl.BlockSpec((3, LANE), lambda j: (0, j))
    return pl.pallas_call(
        body, name=name, grid=(BRANCH_W // LANE,),
        in_specs=[group, pl.BlockSpec((T, LANE), lambda j: (0, j)), w, pl.BlockSpec(memory_space=pl.ANY)],
        out_specs=[group, w],
        out_shape=[_sds(dbig.shape, BF), _sds((3, BRANCH_W), F32)],
        scratch_shapes=[pltpu.VMEM((T + 8, LANE), F32), pltpu.VMEM((T + 8, LANE), F32)],
        input_output_aliases={3: 0}, compiler_params=_cp((PAR,)),
    )(big, dy, cw, dbig)


def _ret_tables(T):
    L = min(RET_L, T)
    hh = jnp.arange(H_RET, dtype=F32)
    lg = jnp.log1p(-jnp.exp2(-5.0 - hh))
    n = jnp.arange(L, dtype=F32)
    a = jnp.exp(lg[:, None] * (n + 1.0))
    b = jnp.exp(lg[:, None] * (L - 1.0 - n))
    gl = jnp.exp(lg * L)
    ch = jnp.arange(L) // CHUNK
    m = jnp.exp(lg[:, None, None] * jnp.abs(n[:, None] - n[None, :])) * (ch[None, :] <= ch[:, None]).astype(F32)
    inv_freq = ROPE_BASE ** (-jnp.linspace(0.0, 1.0, DK_RET // 2, dtype=F32))
    ang = jnp.arange(T, dtype=F32)[:, None] * inv_freq[None, :]
    cos, sin = jnp.cos(ang), jnp.sin(ang)
    return dict(
        L=L, M=m,
        a=jnp.broadcast_to(a[:, :, None], (H_RET, L, DK_RET)),
        b=jnp.broadcast_to(b[:, :, None], (H_RET, L, DK_RET)),
        gl=jnp.broadcast_to(gl[:, None, None], (H_RET, 1, DK_RET)),
        cos=jnp.concatenate([cos, cos], axis=-1), sin=jnp.concatenate([-sin, sin], axis=-1))


def _rot(x, cs, sn):
    return x * cs + pltpu.roll(x, DK_RET // 2, 1) * sn


def _unrot(dy, cs, sn):
    return dy * cs + pltpu.roll(dy * sn, DK_RET // 2, 1)


def _ret_fwd(big, tb, name, comm=None):
    T = big.shape[0]
    L = tb["L"]
    nsc = T // L
    scale = DK_RET ** -0.5

    def body(x_ref, cos_ref, sin_ref, m_ref, a_ref, b_ref, gl_ref, y_ref, o_ref, st_ref, s_s):
        @pl.when(pl.program_id(1) == 0)
        def _():
            s_s[...] = jnp.zeros_like(s_s)

        cs, sn = cos_ref[...], sin_ref[...]
        qt = _rot(x_ref[:, RQ].astype(F32), cs, sn) * scale
        kt = _rot(x_ref[:, RK].astype(F32), cs, sn)
        qb, kb, vb = qt.astype(BF), kt.astype(BF), x_ref[:, RV]
        s_prev = s_s[...]
        st_ref[...] = s_prev
        p = lax.dot_general(qb, kb, NT_DIMS, preferred_element_type=F32) * m_ref[...]
        o = (jnp.dot(p.astype(BF), vb, preferred_element_type=F32)
             + jnp.dot((qt * a_ref[...]).astype(BF), s_prev.astype(BF), preferred_element_type=F32))
        s_s[...] = s_prev * gl_ref[...] + lax.dot_general((kt * b_ref[...]).astype(BF), vb, TN_DIMS,
                                                         preferred_element_type=F32)
        o_ref[...] = o
        gv = x_ref[:, RG].astype(F32)
        y_ref[...] = (gv * _sigmoid(gv) * o * _rms_r(o)).astype(BF)

    tab = pl.BlockSpec((L, DK_RET), lambda h, i: (i, 0))
    per_head = pl.BlockSpec((None, L, DK_RET), lambda h, i: (h, 0, 0))
    out = pl.BlockSpec((L, LANE), lambda h, i: (i, h))
    return _call(
        body, name=name, grid=(H_RET, nsc), comm=comm,
        args=(big, tb["cos"], tb["sin"], tb["M"], tb["a"], tb["b"], tb["gl"]),
        in_specs=[_group_spec(big, RET_BASE, RET_GROUP, L, lambda h, i: (i, h)), tab, tab,
                  pl.BlockSpec((None, L, L), lambda h, i: (h, 0, 0)), per_head, per_head,
                  pl.BlockSpec((None, 1, DK_RET), lambda h, i: (h, 0, 0))],
        out_specs=[out, out, pl.BlockSpec((None, None, DK_RET, DK_RET), lambda h, i: (i, h, 0, 0))],
        out_shape=[_sds((T, BRANCH_W), BF), _sds((T, BRANCH_W), F32), _sds((nsc, H_RET, DK_RET, DK_RET), F32)],
        scratch_shapes=[pltpu.VMEM((DK_RET, DK_RET), F32)],
        sem=(ARB, ARB))


def _ret_bwd(big, o, st, dy, tb, dbig, name):
    T = big.shape[0]
    L = tb["L"]
    nsc = T // L
    scale = DK_RET ** -0.5

    def body(x_ref, cos_ref, sin_ref, m_ref, a_ref, b_ref, gl_ref, o_ref, st_ref, dy_ref, _, d_ref, ds_s):
        @pl.when(pl.program_id(1) == 0)
        def _():
            ds_s[...] = jnp.zeros_like(ds_s)

        cs, sn = cos_ref[...], sin_ref[...]
        mm, av, bv = m_ref[...], a_ref[...], b_ref[...]
        qt = _rot(x_ref[:, RQ].astype(F32), cs, sn) * scale
        kt = _rot(x_ref[:, RK].astype(F32), cs, sn)
        qb, kb, vb = qt.astype(BF), kt.astype(BF), x_ref[:, RV]
        pb = (lax.dot_general(qb, kb, NT_DIMS, preferred_element_type=F32) * mm).astype(BF)
        ov = o_ref[...]
        r = _rms_r(ov)
        oh = ov * r
        gv = x_ref[:, RG].astype(F32)
        sg = _sigmoid(gv)
        dyv = dy_ref[...].astype(F32)
        d_ref[:, RG] = (dyv * oh * (sg * (1.0 + gv * (1.0 - sg)))).astype(BF)
        doh = dyv * gv * sg
        dob = (r * (doh - oh * jnp.mean(doh * oh, axis=-1, keepdims=True))).astype(BF)
        dsb = ds_s[...].astype(BF)
        spb = st_ref[...].astype(BF)
        dpb = (lax.dot_general(dob, vb, NT_DIMS, preferred_element_type=F32) * mm).astype(BF)
        dqt = (jnp.dot(dpb, kb, preferred_element_type=F32)
               + lax.dot_general(dob, spb, NT_DIMS, preferred_element_type=F32) * av)
        dkt = (lax.dot_general(dpb, qb, TN_DIMS, preferred_element_type=F32)
               + lax.dot_general(vb, dsb, NT_DIMS, preferred_element_type=F32) * bv)
        dv = (lax.dot_general(pb, dob, TN_DIMS, preferred_element_type=F32)
              + jnp.dot((kt * bv).astype(BF), dsb, preferred_element_type=F32))
        ds_s[...] = ds_s[...] * gl_ref[...] + lax.dot_general((qt * av).astype(BF), dob, TN_DIMS,
                                                              preferred_element_type=F32)
        d_ref[:, RQ] = (_unrot(dqt, cs, sn) * scale).astype(BF)
        d_ref[:, RK] = _unrot(dkt, cs, sn).astype(BF)
        d_ref[:, RV] = dv.astype(BF)

    def rev(i):
        return nsc - 1 - i

    group = _group_spec(big, RET_BASE, RET_GROUP, L, lambda h, i: (rev(i), h))
    tab = pl.BlockSpec((L, DK_RET), lambda h, i: (rev(i), 0))
    per_head = pl.BlockSpec((None, L, DK_RET), lambda h, i: (h, 0, 0))
    out = pl.BlockSpec((L, LANE), lambda h, i: (rev(i), h))
    return pl.pallas_call(
        body, name=name, grid=(H_RET, nsc),
        in_specs=[group, tab, tab,
                  pl.BlockSpec((None, L, L), lambda h, i: (h, 0, 0)), per_head, per_head,
                  pl.BlockSpec((None, 1, DK_RET), lambda h, i: (h, 0, 0)),
                  out, pl.BlockSpec((None, None, DK_RET, DK_RET), lambda h, i: (rev(i), h, 0, 0)), out,
                  pl.BlockSpec(memory_space=pl.ANY)],
        out_specs=group,
        out_shape=_sds(dbig.shape, BF),
        scratch_shapes=[pltpu.VMEM((DK_RET, DK_RET), F32)],
        input_output_aliases={10: 0}, compiler_params=_cp((PAR, ARB)),
    )(big, tb["cos"], tb["sin"], tb["M"], tb["a"], tb["b"], tb["gl"], o, st, dy, dbig)


def _relbias_onehot(n):
    mm = lax.broadcasted_iota(jnp.int32, (RB_PAD, ATT_TOEP), 1)
    rr = lax.broadcasted_iota(jnp.int32, (RB_PAD, ATT_TOEP), 0)
    idx = jnp.clip(n + ATT_TOEP - mm, 0, 2 * REL_CLIP)
    return (rr == idx).astype(F32)


def _split3(x):
    hi = x.astype(BF).astype(F32)
    mid = (x - hi).astype(BF).astype(F32)
    lo = x - hi - mid
    return jnp.concatenate([hi, mid, lo], axis=0).astype(BF)


def _join3(y):
    k = y.shape[0] // 3
    return (y[:k] + y[k:2 * k]) + y[2 * k:]


def _relbias_expand(rbp, name):
    far = ATT_SPAN - ATT_TOEP

    def body(rb_ref, o_ref):
        rb = rb_ref[...]
        const = jnp.broadcast_to(rb[:, 2 * REL_CLIP:2 * REL_CLIP + 1], (H_ATT, far))

        rb3 = _split3(rb)

        def row(n, c):
            toep = _join3(jnp.dot(rb3, _relbias_onehot(n).astype(BF), preferred_element_type=F32))
            m = lax.broadcasted_iota(jnp.int32, (1, ATT_SPAN), 1)
            d = n // CHUNK + N_PREV - m // CHUNK
            neg = jnp.where((d >= 0) & (d <= N_PREV), 0.0, NEG_INF).astype(F32)
            o_ref[n] = jnp.concatenate([const, toep], axis=1) + neg
            return c

        lax.fori_loop(0, ATT_TQ, row, 0)

    return pl.pallas_call(
        body, name=name,
        in_specs=[pl.BlockSpec(memory_space=pltpu.VMEM)],
        out_specs=pl.BlockSpec(memory_space=pltpu.VMEM),
        out_shape=_sds((ATT_TQ, H_ATT, ATT_SPAN), F32),
    )(rbp)


def _relbias_grad(dbt, name):
    far = ATT_SPAN - ATT_TOEP

    def body(d_ref, o_ref):
        def row(n, carry):
            acc, cs = carry
            dn = d_ref[n]
            acc = acc + _join3(lax.dot_general(_split3(dn[:, far:]), _relbias_onehot(n).astype(BF), NT_DIMS,
                                               preferred_element_type=F32))
            cs = cs + jnp.sum(dn[:, :far], axis=1, keepdims=True)
            return acc, cs

        acc, cs = lax.fori_loop(0, ATT_TQ, row, (jnp.zeros((H_ATT, RB_PAD), F32), jnp.zeros((H_ATT, 1), F32)))
        rr = lax.broadcasted_iota(jnp.int32, (H_ATT, RB_PAD), 1)
        o_ref[...] = acc + jnp.where(rr == 2 * REL_CLIP, cs, 0.0)

    return pl.pallas_call(
        body, name=name,
        in_specs=[pl.BlockSpec(memory_space=pltpu.VMEM)],
        out_specs=pl.BlockSpec(memory_space=pltpu.VMEM),
        out_shape=_sds((H_ATT, RB_PAD), F32),
    )(dbt)


def _att_pad_fill(dst_s, src_ref, cols, T):
    dst_s[pl.ds(0, ATT_PAD), :] = jnp.zeros((ATT_PAD, LANE), dst_s.dtype)
    R = min(512, T)

    def cp(t, c):
        dst_s[pl.ds(pl.multiple_of(ATT_PAD + t * R, LANE), R), :] = src_ref[pl.ds(pl.multiple_of(t * R, R), R), cols]
        return c

    lax.fori_loop(0, T // R, cp, 0)


ATT_WIN = ATT_SUB * ATT_TQ + ATT_PAD


def _att_probs(s_full, sub, bias, t0):
    s = s_full[sub * ATT_TQ:(sub + 1) * ATT_TQ, sub * ATT_TQ:sub * ATT_TQ + ATT_SPAN] * (DH_ATT ** -0.5) + bias
    key_pos = t0 + sub * ATT_TQ - ATT_PAD + lax.broadcasted_iota(jnp.int32, (1, ATT_SPAN), 1)
    s = jnp.where(key_pos >= 0, s, NEG_INF)
    p = jnp.exp(s - jnp.max(s, axis=-1, keepdims=True))
    return p * (1.0 / jnp.sum(p, axis=-1, keepdims=True))


def _att_band(tiles):
    rows = []
    for sub, t in enumerate(tiles):
        parts = []
        if sub:
            parts.append(jnp.zeros((ATT_TQ, sub * ATT_TQ), BF))
        parts.append(t)
        if sub < ATT_SUB - 1:
            parts.append(jnp.zeros((ATT_TQ, (ATT_SUB - 1 - sub) * ATT_TQ), BF))
        rows.append(jnp.concatenate(parts, axis=1))
    return jnp.concatenate(rows, axis=0)


def _att_head_masks(x):
    first = lax.broadcasted_iota(jnp.int32, (1, LANE), 1) < DH_ATT
    zero = jnp.zeros_like(x)
    return first, (jnp.where(first, x, zero), jnp.where(first, zero, x))


def _att_fwd(big, bias, name, comm=None):
    T = big.shape[0]
    rows = ATT_SUB * ATT_TQ
    nt = T // rows

    def body(x_ref, b_ref, y_ref, kp_s, vp_s):
        i = pl.program_id(1)

        @pl.when(i == 0)
        def _():
            _att_pad_fill(kp_s, x_ref, AK, T)
            _att_pad_fill(vp_s, x_ref, AV, T)

        t0 = pl.multiple_of(i * rows, rows)
        kw = kp_s[pl.ds(t0, ATT_WIN), :]
        vw = vp_s[pl.ds(t0, ATT_WIN), :]
        first, qm = _att_head_masks(x_ref[pl.ds(t0, rows), AQ])
        outs = []
        for hh in range(2):
            s_full = lax.dot_general(qm[hh], kw, NT_DIMS, preferred_element_type=F32)
            band = _att_band([_att_probs(s_full, sub, b_ref[hh], t0).astype(BF) for sub in range(ATT_SUB)])
            outs.append(jnp.dot(band, vw, preferred_element_type=F32))
        y_ref[...] = jnp.where(first, outs[0], outs[1]).astype(BF)

    return _call(
        body, name=name, grid=(H_ATT // 2, nt), args=(big, bias), comm=comm,
        in_specs=[_group_spec(big, ATT_BASE, ATT_GROUP, T, lambda p, i: (0, p)),
                  pl.BlockSpec((2, ATT_TQ, ATT_SPAN), lambda p, i: (p, 0, 0))],
        out_specs=[pl.BlockSpec((rows, LANE), lambda p, i: (i, p))],
        out_shape=[_sds((T, BRANCH_W), BF)],
        scratch_shapes=[pltpu.VMEM((T + ATT_PAD, LANE), BF), pltpu.VMEM((T + ATT_PAD, LANE), BF)],
        sem=(ARB, ARB))


def _att_bwd(big, bias, dy, dbig, name, comm=None):
    T = big.shape[0]
    rows = ATT_SUB * ATT_TQ
    nt = T // rows
    scale = DH_ATT ** -0.5

    def body(x_ref, b_ref, dy_ref, _, d_ref, db_ref, kp_s, vp_s, dk_s, dv_s):
        i = pl.program_id(1)

        @pl.when(i == 0)
        def _():
            _att_pad_fill(kp_s, x_ref, AK, T)
            _att_pad_fill(vp_s, x_ref, AV, T)
            dk_s[...] = jnp.zeros_like(dk_s)
            dv_s[...] = jnp.zeros_like(dv_s)
            db_ref[...] = jnp.zeros_like(db_ref)

        t0 = pl.multiple_of(i * rows, rows)
        win = pl.ds(t0, ATT_WIN)
        kw = kp_s[win, :]
        vw = vp_s[win, :]
        first, qm = _att_head_masks(x_ref[pl.ds(t0, rows), AQ])
        _, dom = _att_head_masks(dy_ref[...])
        dqs, dkt, dvt = [], None, None
        for hh in range(2):
            s_full = lax.dot_general(qm[hh], kw, NT_DIMS, preferred_element_type=F32)
            dp_full = lax.dot_general(dom[hh], vw, NT_DIMS, preferred_element_type=F32)
            ps, dss, db = [], [], None
            for sub in range(ATT_SUB):
                pn = _att_probs(s_full, sub, b_ref[hh], t0)
                dp = dp_full[sub * ATT_TQ:(sub + 1) * ATT_TQ, sub * ATT_TQ:sub * ATT_TQ + ATT_SPAN]
                ds = pn * (dp - jnp.sum(dp * pn, axis=-1, keepdims=True))
                db = ds if db is None else db + ds
                ps.append(pn.astype(BF))
                dss.append(ds.astype(BF))
            db_ref[hh] += db
            ds_band, p_band = _att_band(dss), _att_band(ps)
            dqs.append(jnp.dot(ds_band, kw, preferred_element_type=F32))
            qt = jnp.transpose(qm[hh].astype(F32)).astype(BF)
            dot_ = jnp.transpose(dom[hh].astype(F32)).astype(BF)
            dk_h = jnp.dot(qt, ds_band, preferred_element_type=F32)
            dv_h = jnp.dot(dot_, p_band, preferred_element_type=F32)
            dkt = dk_h if dkt is None else dkt + dk_h
            dvt = dv_h if dvt is None else dvt + dv_h
        d_ref[pl.ds(t0, rows), AQ] = (jnp.where(first, dqs[0], dqs[1]) * scale).astype(BF)
        dk_s[win, :] += jnp.transpose(dkt) * scale
        dv_s[win, :] += jnp.transpose(dvt)

        @pl.when(i == nt - 1)
        def _():
            R = min(512, T)

            def cp(t, c):
                src = pl.ds(pl.multiple_of(ATT_PAD + t * R, LANE), R)
                dst = pl.ds(pl.multiple_of(t * R, R), R)
                d_ref[dst, AK] = dk_s[src, :].astype(BF)
                d_ref[dst, AV] = dv_s[src, :].astype(BF)
                return c

            lax.fori_loop(0, T // R, cp, 0)

    group = _group_spec(big, ATT_BASE, ATT_GROUP, T, lambda p, i: (0, p))
    tile = pl.BlockSpec((rows, LANE), lambda p, i: (i, p))
    bspec = pl.BlockSpec((2, ATT_TQ, ATT_SPAN), lambda p, i: (p, 0, 0))
    return _call(
        body, name=name, grid=(H_ATT // 2, nt), args=(big, bias, dy, dbig), comm=comm, aliases={3: 0}, vmem_mb=56,
        in_specs=[group, bspec, tile, pl.BlockSpec(memory_space=pl.ANY)],
        out_specs=[group, bspec],
        out_shape=[_sds(dbig.shape, BF), _sds((H_ATT, ATT_TQ, ATT_SPAN), F32)],
        scratch_shapes=[pltpu.VMEM((T + ATT_PAD, LANE), BF), pltpu.VMEM((T + ATT_PAD, LANE), BF),
                        pltpu.VMEM((T + ATT_PAD, LANE), F32), pltpu.VMEM((T + ATT_PAD, LANE), F32)],
        sem=(ARB, ARB))


def _merge_fwd(x1, big, ys, wb, wo, name):
    T, D = x1.shape
    tm = min(TM, T)

    def body(x_ref, gp_ref, yc_ref, yr_ref, ya_ref, wb_ref, wo_ref, x2_ref, p_ref, mg_ref):
        merged = jnp.zeros((tm, D), F32)
        for i, y_ref in enumerate((yc_ref, yr_ref, ya_ref)):
            cols = slice(i * D, (i + 1) * D)
            pb = jnp.dot(y_ref[...], wb_ref[i], preferred_element_type=F32).astype(BF)
            p_ref[:, cols] = pb
            merged = merged + _sigmoid(gp_ref[:, cols].astype(F32)) * pb.astype(F32)
        mb = merged.astype(BF)
        mg_ref[...] = mb
        x2_ref[...] = x_ref[...] + jnp.dot(mb, wo_ref[...], preferred_element_type=F32)

    tok = pl.BlockSpec((tm, D), lambda i: (i, 0))
    wide = pl.BlockSpec((tm, 3 * D), lambda i: (i, 0))
    yspec = pl.BlockSpec((tm, BRANCH_W), lambda i: (i, 0))
    return pl.pallas_call(
        body, name=name, grid=(T // tm,),
        in_specs=[tok, wide, yspec, yspec, yspec,
                  pl.BlockSpec((3, BRANCH_W, D), lambda i: (0, 0, 0)),
                  pl.BlockSpec((D, D), lambda i: (0, 0))],
        out_specs=[tok, wide, tok],
        out_shape=[_sds((T, D), F32), _sds((T, 3 * D), BF), _sds((T, D), BF)],
        compiler_params=_cp((PAR,)),
    )(x1, big, *ys, wb, wo)


def _merge_bwd(dx2, big, p, wb, wo, name):
    T, D = dx2.shape
    tm = min(TM, T)

    def body(dx_ref, gp_ref, p_ref, wb_ref, wo_ref, dp_ref, dgp_ref, dyc_ref, dyr_ref, dya_ref, dxb_ref):
        dxb = dx_ref[...].astype(BF)
        dxb_ref[...] = dxb
        dm = lax.dot_general(dxb, wo_ref[...], NT_DIMS, preferred_element_type=F32)
        for i, dy_ref in enumerate((dyc_ref, dyr_ref, dya_ref)):
            cols = slice(i * D, (i + 1) * D)
            gt = _sigmoid(gp_ref[:, cols].astype(F32))
            dpb = (dm * gt).astype(BF)
            dp_ref[:, cols] = dpb
            dgp_ref[:, cols] = (dm * p_ref[:, cols].astype(F32) * gt * (1.0 - gt)).astype(BF)
            dy_ref[...] = lax.dot_general(dpb, wb_ref[i], NT_DIMS, preferred_element_type=F32).astype(BF)

    tok = pl.BlockSpec((tm, D), lambda i: (i, 0))
    wide = pl.BlockSpec((tm, 3 * D), lambda i: (i, 0))
    yspec = pl.BlockSpec((tm, BRANCH_W), lambda i: (i, 0))
    return pl.pallas_call(
        body, name=name, grid=(T // tm,),
        in_specs=[tok, wide, wide,
                  pl.BlockSpec((3, BRANCH_W, D), lambda i: (0, 0, 0)),
                  pl.BlockSpec((D, D), lambda i: (0, 0))],
        out_specs=[wide, wide, yspec, yspec, yspec, tok],
        out_shape=[_sds((T, 3 * D), BF), _sds(big.shape, BF)] + [_sds((T, BRANCH_W), BF)] * 3 + [_sds((T, D), BF)],
        compiler_params=_cp((PAR,)),
    )(dx2, big, p, wb, wo)


def _loss_head(x, tgt, fw, name):
    T, D = x.shape
    tm = min(TM, T)

    def body(x_ref, t_ref, w_ref, loss_ref, dx_ref, dw_ref):
        @pl.when(pl.program_id(0) == 0)
        def _():
            loss_ref[...] = jnp.zeros_like(loss_ref)
            dw_ref[...] = jnp.zeros_like(dw_ref)

        xv = x_ref[...]
        wv = w_ref[...]
        e = xv * _rms_r(xv) * wv - t_ref[...]
        loss_ref[...] += 0.5 * jnp.sum(jnp.mean(e * e, axis=-1, keepdims=True))
        dx, dn = _rms_bwd(e * (1.0 / D), xv, wv)
        dx_ref[...] = dx
        dw_ref[...] += dn

    tok = pl.BlockSpec((tm, D), lambda i: (i, 0))
    return pl.pallas_call(
        body, name=name, grid=(T // tm,),
        in_specs=[tok, tok, pl.BlockSpec((1, D), lambda i: (0, 0))],
        out_specs=[pl.BlockSpec((8, LANE), lambda i: (0, 0)), tok, pl.BlockSpec((1, D), lambda i: (0, 0))],
        out_shape=[_sds((8, LANE), F32), _sds((T, D), F32), _sds((1, D), F32)],
        compiler_params=_cp((ARB,)),
    )(x, tgt, fw)


def _block_rows(rows, cols):
    cap = max(8, (1 << 18) // cols)
    best = None
    for r in range(8, rows + 1, 8):
        if rows % r == 0 and r <= cap:
            best = r
    return best if best is not None else rows


def _sum8(land, l, n_layers, name, prev=None, comm=None):
    _, rows, cols = land.shape
    br = _block_rows(rows, cols)

    def body(*refs):
        l_ref, o_ref = refs[0], refs[-1]

        def four(base):
            return ((l_ref[base + 3].astype(F32) + l_ref[base].astype(F32)) + l_ref[base + 1].astype(F32)
                    ) + l_ref[base + 2].astype(F32)

        o_ref[...] = four(0) + four(4)

    in_specs = [pl.BlockSpec((2 * N_SHARD, br, cols), lambda i: (0, i, 0))]
    args = [land]
    aliases = {}
    if prev is not None:
        in_specs.append(pl.BlockSpec(memory_space=pl.ANY))
        args.append(prev)
        aliases = {1: 0}
    main, extra = _call(
        body, name=name, grid=(rows // br,), args=args, in_specs=in_specs,
        out_specs=[pl.BlockSpec((None, br, cols), lambda i: (l, i, 0))],
        out_shape=[_sds((n_layers, rows, cols), F32)], aliases=aliases, sem=(ARB,), comm=comm)
    return main[0], extra


def _adamw_math(w, g, m, v):
    m = ADAM_B1 * m + (1.0 - ADAM_B1) * g
    v = ADAM_B2 * v + (1.0 - ADAM_B2) * (g * g)
    m_hat = m / (1.0 - ADAM_B1 ** ADAM_STEP)
    v_hat = v / (1.0 - ADAM_B2 ** ADAM_STEP)
    delta = -ADAM_LR * (m_hat / (jnp.sqrt(v_hat) + ADAM_EPS) + ADAM_WD * w)
    return delta, m, v


def _adamw(w, g, m, v, name):
    rows, cols = w.shape
    br = _block_rows(rows, cols)

    def body(w_ref, g_ref, m_ref, v_ref, d_ref, nm_ref, nv_ref):
        d, nm, nv = _adamw_math(w_ref[...], g_ref[...], m_ref[...], v_ref[...])
        d_ref[...] = d
        nm_ref[...] = nm
        nv_ref[...] = nv

    blk = pl.BlockSpec((br, cols), lambda i: (i, 0))
    return pl.pallas_call(
        body, name=name, grid=(rows // br,),
        in_specs=[blk] * 4, out_specs=[blk] * 3,
        out_shape=[_sds((rows, cols), F32)] * 3,
        compiler_params=_cp((PAR,)),
    )(w, g, m, v)


def _allreduce_small(v, name):
    rows = v.shape[0]
    flips = [(fx, fy, fc) for fx in (0, 1) for fy in (0, 1) for fc in (0, 1) if fx or fy or fc]

    def body(v_ref, o_ref, all_s, ssem, rsem):
        x, y, c = _place()

        def peer(f):
            return (x + f[0] - 2 * x * f[0], y + f[1] - 2 * y * f[1], c + f[2] - 2 * c * f[2])

        def slot(p):
            return all_s.at[4 * p[0] + 2 * p[1] + p[2]]

        def copy(k, f, owner):
            return pltpu.make_async_remote_copy(
                src_ref=v_ref, dst_ref=slot(owner), send_sem=ssem.at[k], recv_sem=rsem.at[k],
                device_id=peer(f), device_id_type=MESH)

        sends = [copy(k, f, (x, y, c)) for k, f in enumerate(flips)]
        for cp in sends:
            cp.start()
        all_s[4 * x + 2 * y + c] = v_ref[...]
        for k, f in enumerate(flips):
            copy(k, f, peer(f)).wait_recv()
        for cp in sends:
            cp.wait_send()
        acc = all_s[0]
        for d in range(1, 8):
            acc = acc + all_s[d]
        o_ref[...] = acc

    return pl.pallas_call(
        body, name=name,
        in_specs=[pl.BlockSpec(memory_space=pltpu.VMEM)],
        out_specs=pl.BlockSpec(memory_space=pltpu.VMEM),
        out_shape=_sds((rows, LANE), F32),
        scratch_shapes=[pltpu.VMEM((8, rows, LANE), F32), pltpu.SemaphoreType.DMA((7,)), pltpu.SemaphoreType.DMA((7,))],
    )(v)


BIG_NAMES = ("ffn1_w_gate", "ffn1_w_up", "ffn1_w_down", "w_in", "w_branch", "w_merge_gate", "w_out",
             "ffn2_w_gate", "ffn2_w_up", "ffn2_w_down")


FFN1 = ("ffn1_w_gate", "ffn1_w_up", "ffn1_w_down")
FFN2 = ("ffn2_w_gate", "ffn2_w_up", "ffn2_w_down")
MIX_IN = ("w_in", "w_merge_gate")
MIX_OUT = ("w_branch", "w_out")


def _keys(names, l):
    return [(n, l) for n in names]


def _local_step(x, tgt, small, convw_full, wx, n_layers):
    T, D = x.shape
    L = n_layers
    ns = N_SHARD
    dq = D // ns
    W = wx.w

    def hosted(call, keys, scatter=False):
        comm = wx.pieces(keys, scatter)
        main, extra = call(comm)
        if comm is not None:
            wx.arrived(keys, extra, scatter)
        return main

    def mixer_views(l):
        return _build_wbig(W[("w_merge_gate", l)], W[("w_in", l)], f"wbig_{l}")

    def out_views(l):
        wb4 = W[("w_branch", l)]
        wb = _copy_blocks(wb4, pl.BlockSpec((None, None, BRANCH_W, dq), lambda s_, i: (s_, i, 0, 0)),
                          _sds((3, BRANCH_W, D), wb4.dtype),
                          pl.BlockSpec((None, BRANCH_W, dq), lambda s_, i: (i, 0, s_)), (ns, 3), f"w_branch_whole_{l}")
        wo = W[("w_out", l)].reshape(D, D)
        return wb, wo

    tb = _ret_tables(T)
    rb_pad = jnp.pad(small["rel_bias"], ((0, 0), (0, 0), (0, RB_PAD - N_REL)))

    saved = []
    h = x
    for l in range(L):
        s = {"x0": h}
        nxt = l + 1
        x1, s["g1"], s["u1"] = hosted(
            lambda c: _ffn_fwd(h, small["ffn1_norm"][l][None], W[("ffn1_w_gate", l)], W[("ffn1_w_up", l)],
                               W[("ffn1_w_down", l)], f"ffn1_fwd_{l}", comm=c), _keys(MIX_IN, l))
        s["x1"] = x1
        s["wbig"] = mixer_views(l)
        big, s["h"] = _inproj_fwd(x1, small["mix_norm"][l][None], s["wbig"], f"inproj_fwd_{l}")
        s["big"] = big
        s["bias"] = jnp.transpose(_relbias_expand(rb_pad[l], f"relbias_expand_{l}"), (1, 0, 2))
        s["yc"] = _conv_fwd(big, convw_full[l], f"conv_fwd_{l}")
        s["yr"], s["o"], s["st"] = hosted(lambda c: _ret_fwd(big, tb, f"ret_fwd_{l}", comm=c), _keys(MIX_OUT, l))
        (s["ya"],) = hosted(lambda c: _att_fwd(big, s["bias"], f"att_fwd_{l}", comm=c), _keys(FFN2, l))
        s["wb"], s["wo"] = out_views(l)
        x2, s["p"], s["mg"] = _merge_fwd(x1, big, (s["yc"], s["yr"], s["ya"]), s["wb"], s["wo"], f"merge_fwd_{l}")
        s["x2"] = x2
        h, s["g2"], s["u2"] = hosted(
            lambda c: _ffn_fwd(x2, small["ffn2_norm"][l][None], W[("ffn2_w_gate", l)], W[("ffn2_w_up", l)],
                               W[("ffn2_w_down", l)], f"ffn2_fwd_{l}", comm=c), _keys(FFN1, nxt) if nxt < L else [])
        saved.append(s)

    loss_p, dx, d_final = _loss_head(h, tgt, small["final_norm"][None], "loss_head")

    gs = {"final_norm": d_final[0]}
    for k in ("ffn1_norm", "mix_norm", "ffn2_norm", "rel_bias", "conv_w"):
        gs[k] = [None] * L
    tk = min(2048, T)
    nk = T // tk

    def ffn_back(pre, l, dxo, x_in, g, u, first_keys, second_keys):
        nw = small[pre + "_norm"][l][None]
        dgv, duv, av, hb, dacc = hosted(
            lambda c: _ffn_bwd_hidden(dxo, x_in, nw, g, u, W[(pre + "_w_down", l)], f"{pre}_bwd_hidden_{l}", comm=c),
            first_keys, scatter=True)
        dxn, dn = hosted(
            lambda c: _ffn_bwd_resid(dgv, duv, W[(pre + "_w_gate", l)], W[(pre + "_w_up", l)], x_in, nw, dxo,
                                     f"{pre}_bwd_resid_{l}", comm=c),
            second_keys, scatter=True)
        gs[pre + "_norm"][l] = dn[0]
        return dxn, (hb, dgv, duv, av, dacc)

    def ffn_grads(pre, l, hb, dgv, duv, av, dacc, chain=False, carry=()):
        fs = dgv.shape[-1]
        hspec = pl.BlockSpec((tk, D), lambda p, q, k: (k, 0))
        sspec = pl.BlockSpec((None, tk, fs), lambda p, q, k: (p, k, 0))
        down_spec = pl.BlockSpec((None, fs, D), lambda p, q, k: (p, 0, 0))
        jobs = [(pre + "_w_gate", dgv, hb, sspec, hspec, (ns, fs, D), down_spec),
                (pre + "_w_up", duv, hb, sspec, hspec, (ns, fs, D), down_spec),
                (pre + "_w_down", av, dacc, sspec, hspec, (ns, fs, D), down_spec)]
        before = None
        for nm, a, b, a_spec, b_spec, shape, o_spec in jobs:
            def product(c):
                r = _tn(a, b, a_spec, b_spec, _sds(shape, BF), o_spec, (ns, 1, nk), f"d{nm}_{l}", comm=c)
                return (r, []) if c is None else r
            if before is None:
                keys = list(carry)
            else:
                keys = [before] if chain else []
            wx.g[(nm, l)] = hosted(product, keys, scatter=True)
            before = (nm, l)

    for l in reversed(range(L)):
        s = saved[l]
        above = _keys(FFN1, l + 1) if l + 1 < L else [None] * 3
        dx, parts = ffn_back("ffn2", l, dx, s["x2"], s["g2"], s["u2"], [k for k in above[:1] if k],
                             [k for k in above[1:2] if k])
        ffn_grads("ffn2", l, *parts, carry=[k for k in above[2:] if k])
        dp, dbig, dyc, dyr, dya, dxb = _merge_bwd(dx, s["big"], s["p"], s["wb"], s["wo"], f"merge_bwd_{l}")
        wx.g[("w_out", l)] = _tn(
            s["mg"], dxb, pl.BlockSpec((tk, dq), lambda p, q, k: (k, p)), pl.BlockSpec((tk, D), lambda p, q, k: (k, 0)),
            _sds((ns, dq, D), BF), pl.BlockSpec((None, dq, D), lambda p, q, k: (p, 0, 0)), (ns, 1, nk), f"dw_out_{l}")
        gb = None
        for i, yv in enumerate((s["yc"], s["yr"], s["ya"])):
            gb = _tn(yv, dp,
                     pl.BlockSpec((tk, BRANCH_W), lambda p, q, k: (k, 0)),
                     pl.BlockSpec((tk, dq), lambda p, q, k, i=i: (k, i * ns + p)),
                     _sds((ns, 3, BRANCH_W, dq), BF),
                     pl.BlockSpec((None, None, BRANCH_W, dq), lambda p, q, k, i=i: (p, i, 0, 0)),
                     (ns, 1, nk), f"dw_branch{i}_{l}", prev=gb)
        wx.g[("w_branch", l)] = gb
        dbig, dcw = _conv_bwd(s["big"], dyc, convw_full[l], dbig, f"conv_bwd_{l}")
        gs["conv_w"][l] = dcw
        dbig = _ret_bwd(s["big"], s["o"], s["st"], dyr, tb, dbig, f"ret_bwd_{l}")
        dbig, dbias = hosted(lambda c: _att_bwd(s["big"], s["bias"], dya, dbig, f"att_bwd_{l}", comm=c),
                             _keys(FFN2, l), scatter=True)
        gs["rel_bias"][l] = _relbias_grad(jnp.transpose(dbias, (1, 0, 2)), f"relbias_grad_{l}")[:, :N_REL]
        n_in = N_SEG * BRANCH_W
        bn = 1024 if (3 * D) % 1024 == 0 else BRANCH_W
        dwp = _tn(s["h"], dbig, pl.BlockSpec((tk, D), lambda p, q, k: (k, 0)),
                  pl.BlockSpec((tk, bn), lambda p, q, k: (k, 3 * D // bn + q)),
                  _sds((D, n_in), BF), pl.BlockSpec((D, bn), lambda p, q, k: (0, q)), (1, n_in // bn, nk), f"dw_in_{l}")
        wx.g[("w_in", l)] = _ungroup_dw_in(dwp, ns, f"dw_in_shards_{l}")
        wx.g[("w_merge_gate", l)] = _tn_gates(s["h"], dbig, ns, tk, f"dw_merge_gate_{l}")
        dx, dn = hosted(
            lambda c: _inproj_bwd(dbig, s["wbig"], s["x1"], small["mix_norm"][l][None], dx, f"inproj_bwd_{l}", comm=c),
            [("w_in", l)], scatter=True)
        gs["mix_norm"][l] = dn[0]
        dx, parts = ffn_back("ffn1", l, dx, s["x0"], s["g1"], s["u1"],
                             [("w_merge_gate", l), ("w_branch", l), ("w_out", l)], [])
        ffn_grads("ffn1", l, *parts, chain=(l == 0))

    for k in ("ffn1_norm", "mix_norm", "ffn2_norm", "rel_bias", "conv_w"):
        gs[k] = jnp.stack(gs[k])
    return loss_p, dx, gs


class _Exchange:
    def __init__(self, shards):
        self.shards = shards
        self.w = {}
        self.g = {}
        self.landed = {}

    def own(self, key):
        return self.shards[key[0]][key[1]].astype(BF)

    def pieces(self, keys, scatter):
        if not keys:
            return None
        if scatter:
            return _Scatter([self.g[k] for k in keys])
        return _HalfGather([_halves(self.own(k)) for k in keys])

    def arrived(self, keys, outs, scatter):
        for k, o in zip(keys, outs):
            if scatter:
                self.landed[k] = o
            else:
                self.w[k] = o.reshape((N_SHARD,) + self.shards[k[0]].shape[1:])


def _halves(a):
    return a.reshape(2, -1, a.shape[-1])


TRANSPOSED_GRADS = ("ffn1_w_gate", "ffn1_w_up", "ffn2_w_gate", "ffn2_w_up")
W_NAMES = ("ffn1_norm", "ffn1_w_gate", "ffn1_w_up", "ffn1_w_down", "mix_norm", "w_in", "conv_w", "rel_bias", "w_branch",
           "w_merge_gate", "w_out", "ffn2_norm", "ffn2_w_gate", "ffn2_w_up", "ffn2_w_down", "final_norm")


def _as2d(a):
    return a.reshape(1, -1) if a.ndim == 1 else a.reshape(-1, a.shape[-1])


def kernel(x, ffn1_norm, ffn1_w_gate, ffn1_w_up, ffn1_w_down, mix_norm, w_in, conv_w, rel_bias, w_branch, w_merge_gate, w_out, ffn2_norm, ffn2_w_gate, ffn2_w_up, ffn2_w_down, final_norm, loss_target, m_ffn1_norm, m_ffn1_w_gate, m_ffn1_w_up, m_ffn1_w_down, m_mix_norm, m_w_in, m_conv_w, m_rel_bias, m_w_branch, m_w_merge_gate, m_w_out, m_ffn2_norm, m_ffn2_w_gate, m_ffn2_w_up, m_ffn2_w_down, m_final_norm, v_ffn1_norm, v_ffn1_w_gate, v_ffn1_w_up, v_ffn1_w_down, v_mix_norm, v_w_in, v_conv_w, v_rel_bias, v_w_branch, v_w_merge_gate, v_w_out, v_ffn2_norm, v_ffn2_w_gate, v_ffn2_w_up, v_ffn2_w_down, v_final_norm):
    given = dict(locals())
    w = {n: given[n] for n in W_NAMES}
    m = {n: given["m_" + n] for n in W_NAMES}
    v = {n: given["v_" + n] for n in W_NAMES}
    my_chip = 2 * lax.axis_index("x") + lax.axis_index("y")
    L = w_in.shape[0]

    wx = _Exchange({n: jnp.swapaxes(w[n], 1, 2) if n in TRANSPOSED_GRADS else w[n] for n in BIG_NAMES})
    first = _keys(FFN1, 0)
    got = _comm_alone(_HalfGather([_halves(wx.own(k)) for k in first] + [_halves(conv_w)]), "gather_first")
    wx.arrived(first, got[:-1], False)
    convw_full = jnp.transpose(got[-1].reshape((N_SHARD,) + conv_w.shape), (1, 2, 0, 3)).reshape(
        conv_w.shape[0], conv_w.shape[1], -1)

    small = {n: w[n] for n in ("ffn1_norm", "mix_norm", "ffn2_norm", "final_norm", "rel_bias")}
    loss_p, grad_x, gs = _local_step(x[0], loss_target[0], small, convw_full, wx, L)
    last = [(FFN1[-1], 0)]
    wx.arrived(last, _comm_alone(wx.pieces(last, True), "scatter_last"), True)

    sums = []
    for n in BIG_NAMES:
        acc = None
        for l in range(L):
            a = wx.landed[(n, l)]
            acc, _ = _sum8(a.reshape(a.shape[0], -1, a.shape[-1]), l, L, f"sum8_{n}_{l}", prev=acc)
        sums.append(acc.reshape(-1, acc.shape[-1]))

    parts = [gs["ffn1_norm"].reshape(-1), gs["mix_norm"].reshape(-1), gs["ffn2_norm"].reshape(-1),
             gs["final_norm"].reshape(-1), gs["rel_bias"].reshape(-1), gs["conv_w"].reshape(-1), loss_p[0]]
    sizes = [p.shape[0] for p in parts]
    flat = jnp.concatenate(parts)
    rows = -(-flat.shape[0] // (8 * LANE)) * 8
    flat = jnp.pad(flat, (0, rows * LANE - flat.shape[0])).reshape(rows, LANE)
    red = _allreduce_small(flat, "allreduce_small").reshape(-1)
    offs = [0]
    for sz in sizes:
        offs.append(offs[-1] + sz)
    sm = {}
    for i, n in enumerate(("ffn1_norm", "mix_norm", "ffn2_norm", "final_norm", "rel_bias", "conv_w")):
        sm[n] = red[offs[i]:offs[i + 1]]
    loss = red[offs[6]]
    sm["conv_w"] = lax.dynamic_slice_in_dim(sm["conv_w"].reshape(conv_w.shape[0], conv_w.shape[1], -1),
                                            my_chip * conv_w.shape[2], conv_w.shape[2], axis=2)

    grads, deltas, new_m, new_v = {}, {}, {}, {}
    big_sum = dict(zip(BIG_NAMES, sums))
    for n in W_NAMES:
        flip = n in TRANSPOSED_GRADS

        def view(a):
            return jnp.swapaxes(a, 1, 2) if flip else a

        shape = view(w[n]).shape
        g = big_sum[n] if n in big_sum else _as2d(sm[n].reshape(shape))
        out = _adamw(_as2d(view(w[n])), g, _as2d(view(m[n])), _as2d(view(v[n])), f"adamw_{n}")
        grads[n], deltas[n], new_m[n], new_v[n] = (view(o.reshape(shape)) for o in [g] + list(out))

    return (loss, grad_x[None], *[grads[n] for n in W_NAMES], *[deltas[n] for n in W_NAMES],
            *[new_m[n] for n in W_NAMES], *[new_v[n] for n in W_NAMES])
```

```python
import functools
import math

import jax
import jax.numpy as jnp
from jax import lax
from jax.experimental import pallas as pl
from jax.experimental.pallas import tpu as pltpu

F32 = jnp.float32
BF = jnp.bfloat16
MESH = pl.DeviceIdType.MESH
ARB = "arbitrary"
PAR = "parallel"

EPS = 1e-6
NEG_INF = -1e30
ROPE_BASE = 10000.0
CHUNK = 64
BRANCH_W = 512
H_RET = 4
DK_RET = 128
H_ATT = 8
DH_ATT = 64
N_PREV = 8
REL_CLIP = 128
N_REL = 2 * REL_CLIP + 1
N_SHARD = 4
LANE = 128
RET_L = 512
ATT_TQ = 128
ATT_SUB = 4
ATT_PAD = N_PREV * CHUNK
ATT_SPAN = ATT_TQ + ATT_PAD
ATT_TOEP = 2 * REL_CLIP
RB_PAD = 264
TM = 512
TM_FFN = 1024

ADAM_LR = 0.001
ADAM_B1 = 0.9
ADAM_B2 = 0.999
ADAM_EPS = 1e-08
ADAM_WD = 0.01
ADAM_STEP = 10

NT_DIMS = (((1,), (1,)), ((), ()))
TN_DIMS = (((0,), (0,)), ((), ()))


def _cp(sem, vmem_mb=48):
    return pltpu.CompilerParams(dimension_semantics=sem, vmem_limit_bytes=vmem_mb << 20)


def _sds(shape, dtype):
    return jax.ShapeDtypeStruct(tuple(shape), dtype)


def _rms_r(x):
    return lax.rsqrt(jnp.mean(x * x, axis=-1, keepdims=True) + EPS)


def _sigmoid(x):
    return 0.5 * jnp.tanh(0.5 * x) + 0.5


def _rms_bwd(dh, xv, nw):
    r = _rms_r(xv)
    xh = xv * r
    dxh = dh * nw
    dx = r * (dxh - xh * jnp.mean(dxh * xh, axis=-1, keepdims=True))
    return dx, jnp.sum(dh * xh, axis=0, keepdims=True)


def _place():
    return lax.axis_index("x"), lax.axis_index("y"), lax.axis_index("c")


def _other_chips(x, y):
    return [(1 - x, y), (x, 1 - y), (1 - x, 1 - y)]


class _Scatter:
    def __init__(self, srcs):
        self.srcs = list(srcs)
        n = len(self.srcs)
        self.out_shape = [_sds((2 * N_SHARD,) + s.shape[1:], s.dtype) for s in self.srcs]
        self.scratch = [pltpu.SemaphoreType.DMA((n,)), pltpu.SemaphoreType.DMA((3, n)), pltpu.SemaphoreType.DMA((3, n)),
                        pltpu.SemaphoreType.DMA((4, n)), pltpu.SemaphoreType.DMA((4, n))]

    def _plan(self, src, dst, sems, want):
        lsem, s1, r1, s2, r2 = sems
        x, y, c = _place()
        mine = 2 * x + y
        n = len(src)
        chips = list(enumerate(_other_chips(x, y)))

        def copy(s_ref, d_ref, ssem, rsem, to):
            return pltpu.make_async_remote_copy(src_ref=s_ref, dst_ref=d_ref, send_sem=ssem, recv_sem=rsem,
                                                device_id=to, device_id_type=MESH)

        local = [pltpu.make_async_copy(src[k].at[mine], dst[k].at[3], lsem.at[k]) for k in range(n)
                 ] if "local" in want else []
        sends = [copy(src[k].at[2 * ch[0] + ch[1]], dst[k].at[j], s1.at[j, k], r1.at[j, k], (ch[0], ch[1], c))
                 for j, ch in chips for k in range(n)] if "sends" in want else []
        passes = [copy(dst[k].at[j], dst[k].at[4 + j], s2.at[j, k], r2.at[j, k], (x, y, 1 - c))
                  for j, ch in chips for k in range(n)] if "passes" in want else []
        own_pass = [copy(src[k].at[mine], dst[k].at[7], s2.at[3, k], r2.at[3, k], (x, y, 1 - c))
                    for k in range(n)] if "own_pass" in want else []
        return local, sends, passes, own_pass

    def start(self, src, dst, sems):
        local, sends, _, own_pass = self._plan(src, dst, sems, ("local", "sends", "own_pass"))
        for cp in local + sends + own_pass:
            cp.start()

    def relay(self, src, dst, sems):
        _, sends, passes, _ = self._plan(src, dst, sems, ("sends", "passes"))
        for land, fwd in zip(sends, passes):
            land.wait_recv()
            fwd.start()

    def finish(self, src, dst, sems):
        local, sends, passes, own_pass = self._plan(src, dst, sems, ("local", "sends", "passes", "own_pass"))
        for cp in passes + own_pass:
            cp.wait_recv()
        for cp in sends + passes + own_pass:
            cp.wait_send()
        for cp in local:
            cp.wait()

    def wait(self, src, dst, sems):
        self.relay(src, dst, sems)
        self.finish(src, dst, sems)


class _HalfGather:
    def __init__(self, srcs):
        self.srcs = list(srcs)
        n = len(self.srcs)
        self.out_shape = [_sds((N_SHARD,) + s.shape, s.dtype) for s in self.srcs]
        self.scratch = [pltpu.SemaphoreType.DMA((n,))] + [pltpu.SemaphoreType.DMA((3, n)) for _ in range(4)]

    def _plan(self, src, dst, sems, want):
        lsem, s1, r1, s2, r2 = sems
        x, y, c = _place()
        mine = 2 * x + y
        n = len(src)
        chips = [(j, ch, 2 * ch[0] + ch[1]) for j, ch in enumerate(_other_chips(x, y))]

        def copy(s_ref, d_ref, ssem, rsem, to):
            return pltpu.make_async_remote_copy(src_ref=s_ref, dst_ref=d_ref, send_sem=ssem, recv_sem=rsem,
                                                device_id=to, device_id_type=MESH)

        def over(kind, make):
            return [make(j, ch, slot, k) for j, ch, slot in chips for k in range(n)] if kind in want else []

        local = [pltpu.make_async_copy(src[k], dst[k].at[mine], lsem.at[k]) for k in range(n)] if "local" in want else []
        sends = over("sends", lambda j, ch, slot, k: copy(src[k].at[c], dst[k].at[mine, c], s1.at[j, k], r1.at[j, k],
                                                          (ch[0], ch[1], c)))
        lands = over("lands", lambda j, ch, slot, k: copy(src[k].at[c], dst[k].at[slot, c], s1.at[j, k], r1.at[j, k],
                                                          (ch[0], ch[1], c)))
        passes = over("passes", lambda j, ch, slot, k: copy(dst[k].at[slot, c], dst[k].at[slot, c], s2.at[j, k],
                                                            r2.at[j, k], (x, y, 1 - c)))
        gets = over("gets", lambda j, ch, slot, k: copy(dst[k].at[slot, 1 - c], dst[k].at[slot, 1 - c], s2.at[j, k],
                                                        r2.at[j, k], (x, y, 1 - c)))
        return local, sends, lands, passes, gets

    def start(self, src, dst, sems):
        lsem, s1, r1, s2, r2 = sems
        x, y, c = _place()
        mine = 2 * x + y
        for k in range(len(src)):
            pltpu.make_async_copy(src[k], dst[k].at[mine], lsem.at[k]).start()
        for j, ch in enumerate(_other_chips(x, y)):
            for k in range(len(src)):
                pltpu.make_async_remote_copy(
                    src_ref=src[k].at[c], dst_ref=dst[k].at[mine, c], send_sem=s1.at[j, k], recv_sem=r1.at[j, k],
                    device_id=(ch[0], ch[1], c), device_id_type=MESH).start()

    def relay(self, src, dst, sems):
        _, _, lands, passes, _ = self._plan(src, dst, sems, ("lands", "passes"))
        for land, fwd in zip(lands, passes):
            land.wait_recv()
            fwd.start()

    def finish(self, src, dst, sems):
        local, sends, _, passes, gets = self._plan(src, dst, sems, ("local", "sends", "passes", "gets"))
        for cp in gets:
            cp.wait_recv()
        for cp in sends + passes:
            cp.wait_send()
        for cp in local:
            cp.wait()

    def wait(self, src, dst, sems):
        self.relay(src, dst, sems)
        self.finish(src, dst, sems)


def _call(body, *, name, args, in_specs, out_specs, out_shape, grid=(), scratch_shapes=(), sem=None, comm=None,
          aliases=None, vmem_mb=48):
    in_specs, out_specs, out_shape = list(in_specs), list(out_specs), list(out_shape)
    scratch, args = list(scratch_shapes), list(args)
    n_in, n_out, n_scr = len(in_specs), len(out_specs), len(scratch)
    if comm is None:
        def kernel_body(*refs):
            body(*refs)
    else:
        c_in, c_out = len(comm.srcs), len(comm.out_shape)

        def kernel_body(*refs):
            o0 = n_in + c_in
            s0 = o0 + n_out + c_out
            cin, cout, sems = refs[n_in:o0], refs[o0 + n_out:s0], refs[s0 + n_scr:]
            main = refs[:n_in] + refs[o0:o0 + n_out] + refs[s0:s0 + n_scr]
            if grid:
                ids = [pl.program_id(a) for a in range(len(grid))]
                first = functools.reduce(lambda p, q: p & q, [i == 0 for i in ids])
                last = functools.reduce(lambda p, q: p & q, [i == g - 1 for i, g in zip(ids, grid)])

                @pl.when(first)
                def _():
                    comm.start(cin, cout, sems)

                body(*main)

                steps = math.prod(grid)
                if hasattr(comm, "relay") and steps >= 4:
                    flat = functools.reduce(lambda p, q: p + q, [i * math.prod(grid[a + 1:]) for a, i in enumerate(ids)])

                    @pl.when(flat == (5 * steps) // 6)
                    def _():
                        comm.relay(cin, cout, sems)

                    @pl.when(last)
                    def _():
                        comm.finish(cin, cout, sems)
                else:
                    @pl.when(last)
                    def _():
                        comm.wait(cin, cout, sems)
            else:
                comm.start(cin, cout, sems)
                body(*main)
                comm.wait(cin, cout, sems)

        hbm = pl.BlockSpec(memory_space=pl.ANY)
        in_specs += [hbm] * c_in
        out_specs += [hbm] * c_out
        out_shape += comm.out_shape
        scratch += comm.scratch
        args += comm.srcs
    params = dict(vmem_limit_bytes=vmem_mb << 20)
    if grid:
        params["dimension_semantics"] = sem
    outs = pl.pallas_call(
        kernel_body, name=name, grid=grid, in_specs=in_specs, out_specs=out_specs, out_shape=out_shape,
        scratch_shapes=scratch, input_output_aliases=aliases or {}, compiler_params=pltpu.CompilerParams(**params),
    )(*args)
    return list(outs[:n_out]), list(outs[n_out:])


def _comm_alone(comm, name):
    return _call(lambda: None, name=name, args=[], in_specs=[], out_specs=[], out_shape=[], comm=comm)[1]


def _ffn_fwd(x, nw, wg, wu, wd, name, comm=None):
    T, D = x.shape
    ns, fs, _ = wg.shape
    tm = min(TM_FFN, T)

    def body(x_ref, nw_ref, wg_ref, wu_ref, wd_ref, xo_ref, g_ref, u_ref, h_s, acc_s):
        j = pl.program_id(1)

        @pl.when(j == 0)
        def _():
            xv = x_ref[...]
            h_s[...] = (xv * _rms_r(xv) * nw_ref[...]).astype(BF)
            acc_s[...] = jnp.zeros_like(acc_s)

        h = h_s[...]
        gb = lax.dot_general(h, wg_ref[...], NT_DIMS, preferred_element_type=F32).astype(BF)
        ub = lax.dot_general(h, wu_ref[...], NT_DIMS, preferred_element_type=F32).astype(BF)
        g_ref[...] = gb
        u_ref[...] = ub
        g = gb.astype(F32)
        a = (g * _sigmoid(g) * ub.astype(F32)).astype(BF)
        acc_s[...] += jnp.dot(a, wd_ref[...], preferred_element_type=F32)

        @pl.when(j == ns - 1)
        def _():
            xo_ref[...] = x_ref[...] + 0.5 * acc_s[...]

    wspec = pl.BlockSpec((None, fs, D), lambda i, j: (j, 0, 0))
    return _call(
        body, name=name, grid=(T // tm, ns), args=(x, nw, wg, wu, wd), comm=comm, vmem_mb=56,
        in_specs=[pl.BlockSpec((tm, D), lambda i, j: (i, 0)),
                  pl.BlockSpec((1, D), lambda i, j: (0, 0)),
                  wspec, wspec,
                  pl.BlockSpec((None, fs, D), lambda i, j: (j, 0, 0))],
        out_specs=[pl.BlockSpec((tm, D), lambda i, j: (i, 0)),
                   pl.BlockSpec((None, tm, fs), lambda i, j: (j, i, 0)),
                   pl.BlockSpec((None, tm, fs), lambda i, j: (j, i, 0))],
        out_shape=[_sds((T, D), F32), _sds((ns, T, fs), BF), _sds((ns, T, fs), BF)],
        scratch_shapes=[pltpu.VMEM((tm, D), BF), pltpu.VMEM((tm, D), F32)],
        sem=(ARB, ARB))


def _ffn_bwd_hidden(dxo, x, nw, g, u, wd, name, comm=None):
    T, D = x.shape
    ns, fs, _ = wd.shape
    tm = min(TM_FFN, T)

    def body(dxo_ref, x_ref, nw_ref, g_ref, u_ref, wd_ref, dg_ref, du_ref, a_ref, h_ref, dacc_ref, dacc_s):
        @pl.when(pl.program_id(1) == 0)
        def _():
            xv = x_ref[...]
            h_ref[...] = (xv * _rms_r(xv) * nw_ref[...]).astype(BF)
            db = (0.5 * dxo_ref[...]).astype(BF)
            dacc_ref[...] = db
            dacc_s[...] = db

        da = lax.dot_general(dacc_s[...], wd_ref[...], NT_DIMS, preferred_element_type=F32)
        gv = g_ref[...].astype(F32)
        uv = u_ref[...].astype(F32)
        s = _sigmoid(gv)
        sg = gv * s
        a_ref[...] = (sg * uv).astype(BF)
        du_ref[...] = (da * sg).astype(BF)
        dg_ref[...] = (da * uv * (s * (1.0 + gv * (1.0 - s)))).astype(BF)

    tok = pl.BlockSpec((tm, D), lambda i, j: (i, 0))
    hid = pl.BlockSpec((None, tm, fs), lambda i, j: (j, i, 0))
    return _call(
        body, name=name, grid=(T // tm, ns), args=(dxo, x, nw, g, u, wd), comm=comm, vmem_mb=56,
        in_specs=[tok, tok, pl.BlockSpec((1, D), lambda i, j: (0, 0)), hid, hid,
                  pl.BlockSpec((None, fs, D), lambda i, j: (j, 0, 0))],
        out_specs=[hid, hid, hid, tok, tok],
        out_shape=[_sds((ns, T, fs), BF)] * 3 + [_sds((T, D), BF)] * 2,
        scratch_shapes=[pltpu.VMEM((tm, D), BF)],
        sem=(ARB, ARB))


def _ffn_bwd_resid(dg, du, wg, wu, x, nw, dxo, name, comm=None):
    T, D = x.shape
    ns, fs, _ = wg.shape
    tm = min(TM_FFN, T)

    def body(dg_ref, du_ref, wg_ref, wu_ref, x_ref, nw_ref, dxo_ref, dx_ref, dnw_ref, acc_s):
        i = pl.program_id(0)
        j = pl.program_id(1)
        prod = (jnp.dot(dg_ref[...], wg_ref[...], preferred_element_type=F32)
                + jnp.dot(du_ref[...], wu_ref[...], preferred_element_type=F32))

        @pl.when((i == 0) & (j == 0))
        def _():
            dnw_ref[...] = jnp.zeros_like(dnw_ref)

        @pl.when(j == 0)
        def _():
            acc_s[...] = prod

        @pl.when(j > 0)
        def _():
            acc_s[...] += prod

        @pl.when(j == ns - 1)
        def _():
            dx, dn = _rms_bwd(acc_s[...], x_ref[...], nw_ref[...])
            dx_ref[...] = dxo_ref[...] + dx
            dnw_ref[...] += dn

    tok = pl.BlockSpec((tm, D), lambda i, j: (i, 0))
    row = pl.BlockSpec((1, D), lambda i, j: (0, 0))
    hid = pl.BlockSpec((None, tm, fs), lambda i, j: (j, i, 0))
    wspec = pl.BlockSpec((None, fs, D), lambda i, j: (j, 0, 0))
    return _call(
        body, name=name, grid=(T // tm, ns), args=(dg, du, wg, wu, x, nw, dxo), comm=comm, vmem_mb=56,
        in_specs=[hid, hid, wspec, wspec, tok, row, tok],
        out_specs=[tok, row],
        out_shape=[_sds((T, D), F32), _sds((1, D), F32)],
        scratch_shapes=[pltpu.VMEM((tm, D), F32)],
        sem=(ARB, ARB))


def _tn(a, b, a_spec, b_spec, out_shape, out_spec, grid, name, prev=None, comm=None):
    nk = grid[-1]
    acc_shape = tuple(d for d in out_spec.block_shape if d is not None)

    def body(*refs):
        a_ref, b_ref = refs[0], refs[1]
        o_ref, acc = refs[-2], refs[-1]
        k = pl.program_id(2)
        prod = lax.dot_general(a_ref[...], b_ref[...], TN_DIMS, preferred_element_type=F32)

        @pl.when(k == 0)
        def _():
            acc[...] = prod

        @pl.when(k > 0)
        def _():
            acc[...] += prod

        @pl.when(k == nk - 1)
        def _():
            o_ref[...] = acc[...].astype(o_ref.dtype)

    in_specs = [a_spec, b_spec]
    args = [a, b]
    aliases = {}
    if prev is not None:
        in_specs.append(pl.BlockSpec(memory_space=pl.ANY))
        args.append(prev)
        aliases = {2: 0}
    main, extra = _call(
        body, name=name, grid=grid, args=args, in_specs=in_specs, out_specs=[out_spec], out_shape=[out_shape],
        scratch_shapes=[pltpu.VMEM(acc_shape, F32)], aliases=aliases, sem=(ARB, ARB, ARB), comm=comm)
    return main[0] if comm is None else (main[0], extra)


def _tn_gates(h, dbig, ns, tk, name):
    T, D = h.shape
    dq = D // ns
    nk = T // tk

    def body(a_ref, b_ref, o_ref, acc):
        k = pl.program_id(1)
        prod = lax.dot_general(a_ref[...], b_ref[...], TN_DIMS, preferred_element_type=F32)

        @pl.when(k == 0)
        def _():
            acc[...] = prod

        @pl.when(k > 0)
        def _():
            acc[...] += prod

        @pl.when(k == nk - 1)
        def _():
            for s in range(ns):
                o_ref[s] = acc[s * dq:(s + 1) * dq, :].astype(o_ref.dtype)

    return pl.pallas_call(
        body, name=name, grid=(3, nk),
        in_specs=[pl.BlockSpec((tk, D), lambda q, k: (k, 0)), pl.BlockSpec((tk, D), lambda q, k: (k, q))],
        out_specs=pl.BlockSpec((ns, None, dq, D), lambda q, k: (0, q, 0, 0)),
        out_shape=_sds((ns, 3, dq, D), BF),
        scratch_shapes=[pltpu.VMEM((D, D), F32)],
        compiler_params=_cp((PAR, ARB)),
    )(h, dbig)


def _inproj_fwd(x, nw, wbig, name):
    T, D = x.shape
    nb = wbig.shape[-1]
    tm = min(2 * TM, T)
    bn = min(2048, nb)

    def body(x_ref, nw_ref, w_ref, o_ref, h_ref, h_s):
        @pl.when(pl.program_id(1) == 0)
        def _():
            xv = x_ref[...]
            hb = (xv * _rms_r(xv) * nw_ref[...]).astype(BF)
            h_s[...] = hb
            h_ref[...] = hb

        o_ref[...] = jnp.dot(h_s[...], w_ref[...], preferred_element_type=F32).astype(BF)

    return pl.pallas_call(
        body, name=name, grid=(T // tm, nb // bn),
        in_specs=[pl.BlockSpec((tm, D), lambda i, n: (i, 0)),
                  pl.BlockSpec((1, D), lambda i, n: (0, 0)),
                  pl.BlockSpec((D, bn), lambda i, n: (0, n))],
        out_specs=[pl.BlockSpec((tm, bn), lambda i, n: (i, n)),
                   pl.BlockSpec((tm, D), lambda i, n: (i, 0))],
        out_shape=[_sds((T, nb), BF), _sds((T, D), BF)],
        scratch_shapes=[pltpu.VMEM((tm, D), BF)],
        compiler_params=_cp((PAR, ARB)),
    )(x, nw, wbig)


def _inproj_bwd(dbig, wbig, x, nw, dxin, name, comm=None):
    T, D = x.shape
    nb = wbig.shape[-1]
    tm = min(TM_FFN, T)
    tk = min(2048, nb)
    nk = nb // tk

    def body(a_ref, w_ref, x_ref, nw_ref, dxin_ref, dx_ref, dnw_ref, acc_s):
        i = pl.program_id(0)
        k = pl.program_id(1)
        prod = lax.dot_general(a_ref[...], w_ref[...], NT_DIMS, preferred_element_type=F32)

        @pl.when((i == 0) & (k == 0))
        def _():
            dnw_ref[...] = jnp.zeros_like(dnw_ref)

        @pl.when(k == 0)
        def _():
            acc_s[...] = prod

        @pl.when(k > 0)
        def _():
            acc_s[...] += prod

        @pl.when(k == nk - 1)
        def _():
            dx, dn = _rms_bwd(acc_s[...], x_ref[...], nw_ref[...])
            dx_ref[...] = dxin_ref[...] + dx
            dnw_ref[...] += dn

    tok = pl.BlockSpec((tm, D), lambda i, k: (i, 0))
    row = pl.BlockSpec((1, D), lambda i, k: (0, 0))
    return _call(
        body, name=name, grid=(T // tm, nk), args=(dbig, wbig, x, nw, dxin), comm=comm, vmem_mb=56,
        in_specs=[pl.BlockSpec((tm, tk), lambda i, k: (i, k)),
                  pl.BlockSpec((D, tk), lambda i, k: (0, k)),
                  tok, row, tok],
        out_specs=[tok, row],
        out_shape=[_sds((T, D), F32), _sds((1, D), F32)],
        scratch_shapes=[pltpu.VMEM((tm, D), F32)],
        sem=(ARB, ARB))


CONV_R = 512
CONV_BASE, CONV_GROUP = 0, 3
ATT_BASE, ATT_GROUP = 12, 3
RET_BASE, RET_GROUP = 24, 4
N_SEG = 10


N_IN_BLOCKS = N_SEG * BRANCH_W // LANE


def _orig_block(p):
    nblk = BRANCH_W // LANE
    qa, qr = p - ATT_BASE, p - RET_BASE
    conv = (p % CONV_GROUP) * nblk + p // CONV_GROUP
    att = (7 + qa % ATT_GROUP) * nblk + qa // ATT_GROUP
    ret = (3 + qr % RET_GROUP) * nblk + qr // RET_GROUP
    return jnp.where(p < ATT_BASE, conv, jnp.where(p < RET_BASE, att, ret))


def _copy_blocks(src, in_spec, out_shape, out_spec, grid, name, prev=None):
    def body(*refs):
        refs[-1][...] = refs[0][...]

    in_specs, args, aliases = [in_spec], [src], {}
    if prev is not None:
        in_specs.append(pl.BlockSpec(memory_space=pl.ANY))
        args.append(prev)
        aliases = {1: 0}
    return pl.pallas_call(
        body, name=name, grid=grid, in_specs=in_specs, out_specs=out_spec, out_shape=out_shape,
        input_output_aliases=aliases, compiler_params=_cp(tuple(PAR for _ in grid)),
    )(*args)


def _build_wbig(gates4, win4, name):
    ns, _, dq, D = gates4.shape
    per = win4.shape[-1] // LANE
    shape = _sds((D, 3 * D + N_IN_BLOCKS * LANE), gates4.dtype)
    out = _copy_blocks(gates4, pl.BlockSpec((None, None, dq, D), lambda s, i: (s, i, 0, 0)), shape,
                       pl.BlockSpec((dq, D), lambda s, i: (s, i)), (ns, 3), name + "_gates")
    return _copy_blocks(
        win4, pl.BlockSpec((None, D, LANE), lambda p: (_orig_block(p) // per, 0, _orig_block(p) % per)), shape,
        pl.BlockSpec((D, LANE), lambda p: (0, 3 * D // LANE + p)), (N_IN_BLOCKS,), name + "_in", prev=out)


def _ungroup_dw_in(dwp, ns, name):
    D = dwp.shape[0]
    per = N_IN_BLOCKS // ns
    return _copy_blocks(
        dwp, pl.BlockSpec((D, LANE), lambda p: (0, p)), _sds((ns, D, per * LANE), dwp.dtype),
        pl.BlockSpec((None, D, LANE), lambda p: (_orig_block(p) // per, 0, _orig_block(p) % per)), (N_IN_BLOCKS,), name)


def _seg0(big):
    return (big.shape[1] - N_SEG * BRANCH_W) // LANE


def _group_spec(big, base, group, rows, where):
    first = (_seg0(big) + base) // group
    assert first * group == _seg0(big) + base

    def index(*ids):
        r, g = where(*ids)
        return r, first + g

    return pl.BlockSpec((rows, group * LANE), index)


CU, CB, CC = (slice(k * LANE, (k + 1) * LANE) for k in range(3))
AQ, AK, AV = CU, CB, CC
RQ, RK, RV, RG = (slice(k * LANE, (k + 1) * LANE) for k in range(4))


def _conv_fwd(big, cw, name):
    T = big.shape[0]
    R = min(CONV_R, T)

    def body(g_ref, w_ref, y_ref, z_s):
        z_s[pl.ds(0, 8), :] = jnp.zeros((8, LANE), F32)

        def fill(t, c):
            sl = pl.ds(pl.multiple_of(t * R, R), R)
            z_s[pl.ds(pl.multiple_of(t * R + 8, 8), R), :] = g_ref[sl, CC].astype(F32) * g_ref[sl, CU].astype(F32)
            return c

        lax.fori_loop(0, T // R, fill, 0)
        w0, w1, w2 = w_ref[0:1, :], w_ref[1:2, :], w_ref[2:3, :]

        def step(t, c):
            zz = z_s[pl.ds(pl.multiple_of(t * R, R), R + 8), :]
            z0 = zz[8:]
            z1 = pltpu.roll(zz, 1, 0)[8:]
            z2 = pltpu.roll(zz, 2, 0)[8:]
            sl = pl.ds(pl.multiple_of(t * R, R), R)
            y_ref[sl, :] = (g_ref[sl, CB].astype(F32) * (w2 * z0 + w1 * z1 + w0 * z2)).astype(BF)
            return c

        lax.fori_loop(0, T // R, step, 0)

    return pl.pallas_call(
        body, name=name, grid=(BRANCH_W // LANE,),
        in_specs=[_group_spec(big, CONV_BASE, CONV_GROUP, T, lambda j: (0, j)),
                  pl.BlockSpec((3, LANE), lambda j: (0, j))],
        out_specs=pl.BlockSpec((T, LANE), lambda j: (0, j)),
        out_shape=_sds((T, BRANCH_W), BF),
        scratch_shapes=[pltpu.VMEM((T + 8, LANE), F32)],
        compiler_params=_cp((PAR,)),
    )(big, cw)


def _conv_bwd(big, dy, cw, dbig, name):
    T = big.shape[0]
    R = min(CONV_R, T)

    def body(g_ref, dy_ref, w_ref, _, o_ref, dw_ref, z_s, d_s):
        z_s[pl.ds(0, 8), :] = jnp.zeros((8, LANE), F32)
        d_s[pl.ds(T, 8), :] = jnp.zeros((8, LANE), F32)

        def fill(t, c):
            sl = pl.ds(pl.multiple_of(t * R, R), R)
            z_s[pl.ds(pl.multiple_of(t * R + 8, 8), R), :] = g_ref[sl, CC].astype(F32) * g_ref[sl, CU].astype(F32)
            d_s[sl, :] = dy_ref[sl, :].astype(F32) * g_ref[sl, CB].astype(F32)
            return c

        lax.fori_loop(0, T // R, fill, 0)
        w0, w1, w2 = w_ref[0:1, :], w_ref[1:2, :], w_ref[2:3, :]

        def step(t, carry):
            a0, a1, a2 = carry
            zz = z_s[pl.ds(pl.multiple_of(t * R, R), R + 8), :]
            z0 = zz[8:]
            z1 = pltpu.roll(zz, 1, 0)[8:]
            z2 = pltpu.roll(zz, 2, 0)[8:]
            sl = pl.ds(pl.multiple_of(t * R, R), R)
            dyv = dy_ref[sl, :].astype(F32)
            o_ref[sl, CB] = (dyv * (w2 * z0 + w1 * z1 + w0 * z2)).astype(BF)
            dd = d_s[pl.ds(pl.multiple_of(t * R, R), R + 8), :]
            d0 = dd[:R]
            d1 = pltpu.roll(dd, R + 7, 0)[:R]
            d2 = pltpu.roll(dd, R + 6, 0)[:R]
            dz = w2 * d0 + w1 * d1 + w0 * d2
            o_ref[sl, CC] = (dz * g_ref[sl, CU].astype(F32)).astype(BF)
            o_ref[sl, CU] = (dz * g_ref[sl, CC].astype(F32)).astype(BF)
            a0 = a0 + jnp.sum(d0 * z2, axis=0, keepdims=True)
            a1 = a1 + jnp.sum(d0 * z1, axis=0, keepdims=True)
            a2 = a2 + jnp.sum(d0 * z0, axis=0, keepdims=True)
            return a0, a1, a2

        zero = jnp.zeros((1, LANE), F32)
        a0, a1, a2 = lax.fori_loop(0, T // R, step, (zero, zero, zero))
        dw_ref[0:1, :] = a0
        dw_ref[1:2, :] = a1
        dw_ref[2:3, :] = a2

    group = _group_spec(big, CONV_BASE, CONV_GROUP, T, lambda j: (0, j))
    w = pl.BlockSpec((3, LANE), lambda j: (0, j))
    return pl.pallas_call(
        body, name=name, grid=(BRANCH_W // LANE,),
        in_specs=[group, pl.BlockSpec((T, LANE), lambda j: (0, j)), w, pl.BlockSpec(memory_space=pl.ANY)],
        out_specs=[group, w],
        out_shape=[_sds(dbig.shape, BF), _sds((3, BRANCH_W), F32)],
        scratch_shapes=[pltpu.VMEM((T + 8, LANE), F32), pltpu.VMEM((T + 8, LANE), F32)],
        input_output_aliases={3: 0}, compiler_params=_cp((PAR,)),
    )(big, dy, cw, dbig)


def _ret_tables(T):
    L = min(RET_L, T)
    hh = jnp.arange(H_RET, dtype=F32)
    lg = jnp.log1p(-jnp.exp2(-5.0 - hh))
    n = jnp.arange(L, dtype=F32)
    a = jnp.exp(lg[:, None] * (n + 1.0))
    b = jnp.exp(lg[:, None] * (L - 1.0 - n))
    gl = jnp.exp(lg * L)
    ch = jnp.arange(L) // CHUNK
    m = jnp.exp(lg[:, None, None] * jnp.abs(n[:, None] - n[None, :])) * (ch[None, :] <= ch[:, None]).astype(F32)
    inv_freq = ROPE_BASE ** (-jnp.linspace(0.0, 1.0, DK_RET // 2, dtype=F32))
    ang = jnp.arange(T, dtype=F32)[:, None] * inv_freq[None, :]
    cos, sin = jnp.cos(ang), jnp.sin(ang)
    return dict(
        L=L, M=m,
        a=jnp.broadcast_to(a[:, :, None], (H_RET, L, DK_RET)),
        b=jnp.broadcast_to(b[:, :, None], (H_RET, L, DK_RET)),
        gl=jnp.broadcast_to(gl[:, None, None], (H_RET, 1, DK_RET)),
        cos=jnp.concatenate([cos, cos], axis=-1), sin=jnp.concatenate([-sin, sin], axis=-1))


def _rot(x, cs, sn):
    return x * cs + pltpu.roll(x, DK_RET // 2, 1) * sn


def _unrot(dy, cs, sn):
    return dy * cs + pltpu.roll(dy * sn, DK_RET // 2, 1)


def _ret_fwd(big, tb, name, comm=None):
    T = big.shape[0]
    L = tb["L"]
    nsc = T // L
    scale = DK_RET ** -0.5

    def body(x_ref, cos_ref, sin_ref, m_ref, a_ref, b_ref, gl_ref, y_ref, o_ref, st_ref, s_s):
        @pl.when(pl.program_id(1) == 0)
        def _():
            s_s[...] = jnp.zeros_like(s_s)

        cs, sn = cos_ref[...], sin_ref[...]
        qt = _rot(x_ref[:, RQ].astype(F32), cs, sn) * scale
        kt = _rot(x_ref[:, RK].astype(F32), cs, sn)
        qb, kb, vb = qt.astype(BF), kt.astype(BF), x_ref[:, RV]
        s_prev = s_s[...]
        st_ref[...] = s_prev
        p = lax.dot_general(qb, kb, NT_DIMS, preferred_element_type=F32) * m_ref[...]
        o = (jnp.dot(p.astype(BF), vb, preferred_element_type=F32)
             + jnp.dot((qt * a_ref[...]).astype(BF), s_prev.astype(BF), preferred_element_type=F32))
        s_s[...] = s_prev * gl_ref[...] + lax.dot_general((kt * b_ref[...]).astype(BF), vb, TN_DIMS,
                                                         preferred_element_type=F32)
        o_ref[...] = o
        gv = x_ref[:, RG].astype(F32)
        y_ref[...] = (gv * _sigmoid(gv) * o * _rms_r(o)).astype(BF)

    tab = pl.BlockSpec((L, DK_RET), lambda h, i: (i, 0))
    per_head = pl.BlockSpec((None, L, DK_RET), lambda h, i: (h, 0, 0))
    out = pl.BlockSpec((L, LANE), lambda h, i: (i, h))
    return _call(
        body, name=name, grid=(H_RET, nsc), comm=comm,
        args=(big, tb["cos"], tb["sin"], tb["M"], tb["a"], tb["b"], tb["gl"]),
        in_specs=[_group_spec(big, RET_BASE, RET_GROUP, L, lambda h, i: (i, h)), tab, tab,
                  pl.BlockSpec((None, L, L), lambda h, i: (h, 0, 0)), per_head, per_head,
                  pl.BlockSpec((None, 1, DK_RET), lambda h, i: (h, 0, 0))],
        out_specs=[out, out, pl.BlockSpec((None, None, DK_RET, DK_RET), lambda h, i: (i, h, 0, 0))],
        out_shape=[_sds((T, BRANCH_W), BF), _sds((T, BRANCH_W), F32), _sds((nsc, H_RET, DK_RET, DK_RET), F32)],
        scratch_shapes=[pltpu.VMEM((DK_RET, DK_RET), F32)],
        sem=(ARB, ARB))


def _ret_bwd(big, o, st, dy, tb, dbig, name):
    T = big.shape[0]
    L = tb["L"]
    nsc = T // L
    scale = DK_RET ** -0.5

    def body(x_ref, cos_ref, sin_ref, m_ref, a_ref, b_ref, gl_ref, o_ref, st_ref, dy_ref, _, d_ref, ds_s):
        @pl.when(pl.program_id(1) == 0)
        def _():
            ds_s[...] = jnp.zeros_like(ds_s)

        cs, sn = cos_ref[...], sin_ref[...]
        mm, av, bv = m_ref[...], a_ref[...], b_ref[...]
        qt = _rot(x_ref[:, RQ].astype(F32), cs, sn) * scale
        kt = _rot(x_ref[:, RK].astype(F32), cs, sn)
        qb, kb, vb = qt.astype(BF), kt.astype(BF), x_ref[:, RV]
        pb = (lax.dot_general(qb, kb, NT_DIMS, preferred_element_type=F32) * mm).astype(BF)
        ov = o_ref[...]
        r = _rms_r(ov)
        oh = ov * r
        gv = x_ref[:, RG].astype(F32)
        sg = _sigmoid(gv)
        dyv = dy_ref[...].astype(F32)
        d_ref[:, RG] = (dyv * oh * (sg * (1.0 + gv * (1.0 - sg)))).astype(BF)
        doh = dyv * gv * sg
        dob = (r * (doh - oh * jnp.mean(doh * oh, axis=-1, keepdims=True))).astype(BF)
        dsb = ds_s[...].astype(BF)
        spb = st_ref[...].astype(BF)
        dpb = (lax.dot_general(dob, vb, NT_DIMS, preferred_element_type=F32) * mm).astype(BF)
        dqt = (jnp.dot(dpb, kb, preferred_element_type=F32)
               + lax.dot_general(dob, spb, NT_DIMS, preferred_element_type=F32) * av)
        dkt = (lax.dot_general(dpb, qb, TN_DIMS, preferred_element_type=F32)
               + lax.dot_general(vb, dsb, NT_DIMS, preferred_element_type=F32) * bv)
        dv = (lax.dot_general(pb, dob, TN_DIMS, preferred_element_type=F32)
              + jnp.dot((kt * bv).astype(BF), dsb, preferred_element_type=F32))
        ds_s[...] = ds_s[...] * gl_ref[...] + lax.dot_general((qt * av).astype(BF), dob, TN_DIMS,
                                                              preferred_element_type=F32)
        d_ref[:, RQ] = (_unrot(dqt, cs, sn) * scale).astype(BF)
        d_ref[:, RK] = _unrot(dkt, cs, sn).astype(BF)
        d_ref[:, RV] = dv.astype(BF)

    def rev(i):
        return nsc - 1 - i

    group = _group_spec(big, RET_BASE, RET_GROUP, L, lambda h, i: (rev(i), h))
    tab = pl.BlockSpec((L, DK_RET), lambda h, i: (rev(i), 0))
    per_head = pl.BlockSpec((None, L, DK_RET), lambda h, i: (h, 0, 0))
    out = pl.BlockSpec((L, LANE), lambda h, i: (rev(i), h))
    return pl.pallas_call(
        body, name=name, grid=(H_RET, nsc),
        in_specs=[group, tab, tab,
                  pl.BlockSpec((None, L, L), lambda h, i: (h, 0, 0)), per_head, per_head,
                  pl.BlockSpec((None, 1, DK_RET), lambda h, i: (h, 0, 0)),
                  out, pl.BlockSpec((None, None, DK_RET, DK_RET), lambda h, i: (rev(i), h, 0, 0)), out,
                  pl.BlockSpec(memory_space=pl.ANY)],
        out_specs=group,
        out_shape=_sds(dbig.shape, BF),
        scratch_shapes=[pltpu.VMEM((DK_RET, DK_RET), F32)],
        input_output_aliases={10: 0}, compiler_params=_cp((PAR, ARB)),
    )(big, tb["cos"], tb["sin"], tb["M"], tb["a"], tb["b"], tb["gl"], o, st, dy, dbig)


def _relbias_onehot(n):
    mm = lax.broadcasted_iota(jnp.int32, (RB_PAD, ATT_TOEP), 1)
    rr = lax.broadcasted_iota(jnp.int32, (RB_PAD, ATT_TOEP), 0)
    idx = jnp.clip(n + ATT_TOEP - mm, 0, 2 * REL_CLIP)
    return (rr == idx).astype(F32)


def _split3(x):
    hi = x.astype(BF).astype(F32)
    mid = (x - hi).astype(BF).astype(F32)
    lo = x - hi - mid
    return jnp.concatenate([hi, mid, lo], axis=0).astype(BF)


def _join3(y):
    k = y.shape[0] // 3
    return (y[:k] + y[k:2 * k]) + y[2 * k:]


def _relbias_expand(rbp, name):
    far = ATT_SPAN - ATT_TOEP

    def body(rb_ref, o_ref):
        rb = rb_ref[...]
        const = jnp.broadcast_to(rb[:, 2 * REL_CLIP:2 * REL_CLIP + 1], (H_ATT, far))

        rb3 = _split3(rb)

        def row(n, c):
            toep = _join3(jnp.dot(rb3, _relbias_onehot(n).astype(BF), preferred_element_type=F32))
            m = lax.broadcasted_iota(jnp.int32, (1, ATT_SPAN), 1)
            d = n // CHUNK + N_PREV - m // CHUNK
            neg = jnp.where((d >= 0) & (d <= N_PREV), 0.0, NEG_INF).astype(F32)
            o_ref[n] = jnp.concatenate([const, toep], axis=1) + neg
            return c

        lax.fori_loop(0, ATT_TQ, row, 0)

    return pl.pallas_call(
        body, name=name,
        in_specs=[pl.BlockSpec(memory_space=pltpu.VMEM)],
        out_specs=pl.BlockSpec(memory_space=pltpu.VMEM),
        out_shape=_sds((ATT_TQ, H_ATT, ATT_SPAN), F32),
    )(rbp)


def _relbias_grad(dbt, name):
    far = ATT_SPAN - ATT_TOEP

    def body(d_ref, o_ref):
        def row(n, carry):
            acc, cs = carry
            dn = d_ref[n]
            acc = acc + _join3(lax.dot_general(_split3(dn[:, far:]), _relbias_onehot(n).astype(BF), NT_DIMS,
                                               preferred_element_type=F32))
            cs = cs + jnp.sum(dn[:, :far], axis=1, keepdims=True)
            return acc, cs

        acc, cs = lax.fori_loop(0, ATT_TQ, row, (jnp.zeros((H_ATT, RB_PAD), F32), jnp.zeros((H_ATT, 1), F32)))
        rr = lax.broadcasted_iota(jnp.int32, (H_ATT, RB_PAD), 1)
        o_ref[...] = acc + jnp.where(rr == 2 * REL_CLIP, cs, 0.0)

    return pl.pallas_call(
        body, name=name,
        in_specs=[pl.BlockSpec(memory_space=pltpu.VMEM)],
        out_specs=pl.BlockSpec(memory_space=pltpu.VMEM),
        out_shape=_sds((H_ATT, RB_PAD), F32),
    )(dbt)


def _att_pad_fill(dst_s, src_ref, cols, T):
    dst_s[pl.ds(0, ATT_PAD), :] = jnp.zeros((ATT_PAD, LANE), dst_s.dtype)
    R = min(512, T)

    def cp(t, c):
        dst_s[pl.ds(pl.multiple_of(ATT_PAD + t * R, LANE), R), :] = src_ref[pl.ds(pl.multiple_of(t * R, R), R), cols]
        return c

    lax.fori_loop(0, T // R, cp, 0)


ATT_WIN = ATT_SUB * ATT_TQ + ATT_PAD


def _att_probs(s_full, sub, bias, t0):
    s = s_full[sub * ATT_TQ:(sub + 1) * ATT_TQ, sub * ATT_TQ:sub * ATT_TQ + ATT_SPAN] * (DH_ATT ** -0.5) + bias
    key_pos = t0 + sub * ATT_TQ - ATT_PAD + lax.broadcasted_iota(jnp.int32, (1, ATT_SPAN), 1)
    s = jnp.where(key_pos >= 0, s, NEG_INF)
    p = jnp.exp(s - jnp.max(s, axis=-1, keepdims=True))
    return p * (1.0 / jnp.sum(p, axis=-1, keepdims=True))


def _att_band(tiles):
    rows = []
    for sub, t in enumerate(tiles):
        parts = []
        if sub:
            parts.append(jnp.zeros((ATT_TQ, sub * ATT_TQ), BF))
        parts.append(t)
        if sub < ATT_SUB - 1:
            parts.append(jnp.zeros((ATT_TQ, (ATT_SUB - 1 - sub) * ATT_TQ), BF))
        rows.append(jnp.concatenate(parts, axis=1))
    return jnp.concatenate(rows, axis=0)


def _att_head_masks(x):
    first = lax.broadcasted_iota(jnp.int32, (1, LANE), 1) < DH_ATT
    zero = jnp.zeros_like(x)
    return first, (jnp.where(first, x, zero), jnp.where(first, zero, x))


def _att_fwd(big, bias, name, comm=None):
    T = big.shape[0]
    rows = ATT_SUB * ATT_TQ
    nt = T // rows

    def body(x_ref, b_ref, y_ref, kp_s, vp_s):
        i = pl.program_id(1)

        @pl.when(i == 0)
        def _():
            _att_pad_fill(kp_s, x_ref, AK, T)
            _att_pad_fill(vp_s, x_ref, AV, T)

        t0 = pl.multiple_of(i * rows, rows)
        kw = kp_s[pl.ds(t0, ATT_WIN), :]
        vw = vp_s[pl.ds(t0, ATT_WIN), :]
        first, qm = _att_head_masks(x_ref[pl.ds(t0, rows), AQ])
        outs = []
        for hh in range(2):
            s_full = lax.dot_general(qm[hh], kw, NT_DIMS, preferred_element_type=F32)
            band = _att_band([_att_probs(s_full, sub, b_ref[hh], t0).astype(BF) for sub in range(ATT_SUB)])
            outs.append(jnp.dot(band, vw, preferred_element_type=F32))
        y_ref[...] = jnp.where(first, outs[0], outs[1]).astype(BF)

    return _call(
        body, name=name, grid=(H_ATT // 2, nt), args=(big, bias), comm=comm,
        in_specs=[_group_spec(big, ATT_BASE, ATT_GROUP, T, lambda p, i: (0, p)),
                  pl.BlockSpec((2, ATT_TQ, ATT_SPAN), lambda p, i: (p, 0, 0))],
        out_specs=[pl.BlockSpec((rows, LANE), lambda p, i: (i, p))],
        out_shape=[_sds((T, BRANCH_W), BF)],
        scratch_shapes=[pltpu.VMEM((T + ATT_PAD, LANE), BF), pltpu.VMEM((T + ATT_PAD, LANE), BF)],
        sem=(ARB, ARB))


def _att_bwd(big, bias, dy, dbig, name, comm=None):
    T = big.shape[0]
    rows = ATT_SUB * ATT_TQ
    nt = T // rows
    scale = DH_ATT ** -0.5

    def body(x_ref, b_ref, dy_ref, _, d_ref, db_ref, kp_s, vp_s, dk_s, dv_s):
        i = pl.program_id(1)

        @pl.when(i == 0)
        def _():
            _att_pad_fill(kp_s, x_ref, AK, T)
            _att_pad_fill(vp_s, x_ref, AV, T)
            dk_s[...] = jnp.zeros_like(dk_s)
            dv_s[...] = jnp.zeros_like(dv_s)
            db_ref[...] = jnp.zeros_like(db_ref)

        t0 = pl.multiple_of(i * rows, rows)
        win = pl.ds(t0, ATT_WIN)
        kw = kp_s[win, :]
        vw = vp_s[win, :]
        first, qm = _att_head_masks(x_ref[pl.ds(t0, rows), AQ])
        _, dom = _att_head_masks(dy_ref[...])
        dqs, dkt, dvt = [], None, None
        for hh in range(2):
            s_full = lax.dot_general(qm[hh], kw, NT_DIMS, preferred_element_type=F32)
            dp_full = lax.dot_general(dom[hh], vw, NT_DIMS, preferred_element_type=F32)
            ps, dss, db = [], [], None
            for sub in range(ATT_SUB):
                pn = _att_probs(s_full, sub, b_ref[hh], t0)
                dp = dp_full[sub * ATT_TQ:(sub + 1) * ATT_TQ, sub * ATT_TQ:sub * ATT_TQ + ATT_SPAN]
                ds = pn * (dp - jnp.sum(dp * pn, axis=-1, keepdims=True))
                db = ds if db is None else db + ds
                ps.append(pn.astype(BF))
                dss.append(ds.astype(BF))
            db_ref[hh] += db
            ds_band, p_band = _att_band(dss), _att_band(ps)
            dqs.append(jnp.dot(ds_band, kw, preferred_element_type=F32))
            qt = jnp.transpose(qm[hh].astype(F32)).astype(BF)
            dot_ = jnp.transpose(dom[hh].astype(F32)).astype(BF)
            dk_h = jnp.dot(qt, ds_band, preferred_element_type=F32)
            dv_h = jnp.dot(dot_, p_band, preferred_element_type=F32)
            dkt = dk_h if dkt is None else dkt + dk_h
            dvt = dv_h if dvt is None else dvt + dv_h
        d_ref[pl.ds(t0, rows), AQ] = (jnp.where(first, dqs[0], dqs[1]) * scale).astype(BF)
        dk_s[win, :] += jnp.transpose(dkt) * scale
        dv_s[win, :] += jnp.transpose(dvt)

        @pl.when(i == nt - 1)
        def _():
            R = min(512, T)

            def cp(t, c):
                src = pl.ds(pl.multiple_of(ATT_PAD + t * R, LANE), R)
                dst = pl.ds(pl.multiple_of(t * R, R), R)
                d_ref[dst, AK] = dk_s[src, :].astype(BF)
                d_ref[dst, AV] = dv_s[src, :].astype(BF)
                return c

            lax.fori_loop(0, T // R, cp, 0)

    group = _group_spec(big, ATT_BASE, ATT_GROUP, T, lambda p, i: (0, p))
    tile = pl.BlockSpec((rows, LANE), lambda p, i: (i, p))
    bspec = pl.BlockSpec((2, ATT_TQ, ATT_SPAN), lambda p, i: (p, 0, 0))
    return _call(
        body, name=name, grid=(H_ATT // 2, nt), args=(big, bias, dy, dbig), comm=comm, aliases={3: 0}, vmem_mb=56,
        in_specs=[group, bspec, tile, pl.BlockSpec(memory_space=pl.ANY)],
        out_specs=[group, bspec],
        out_shape=[_sds(dbig.shape, BF), _sds((H_ATT, ATT_TQ, ATT_SPAN), F32)],
        scratch_shapes=[pltpu.VMEM((T + ATT_PAD, LANE), BF), pltpu.VMEM((T + ATT_PAD, LANE), BF),
                        pltpu.VMEM((T + ATT_PAD, LANE), F32), pltpu.VMEM((T + ATT_PAD, LANE), F32)],
        sem=(ARB, ARB))


def _merge_fwd(x1, big, ys, wb, wo, name):
    T, D = x1.shape
    tm = min(TM, T)

    def body(x_ref, gp_ref, yc_ref, yr_ref, ya_ref, wb_ref, wo_ref, x2_ref, p_ref, mg_ref):
        merged = jnp.zeros((tm, D), F32)
        for i, y_ref in enumerate((yc_ref, yr_ref, ya_ref)):
            cols = slice(i * D, (i + 1) * D)
            pb = jnp.dot(y_ref[...], wb_ref[i], preferred_element_type=F32).astype(BF)
            p_ref[:, cols] = pb
            merged = merged + _sigmoid(gp_ref[:, cols].astype(F32)) * pb.astype(F32)
        mb = merged.astype(BF)
        mg_ref[...] = mb
        x2_ref[...] = x_ref[...] + jnp.dot(mb, wo_ref[...], preferred_element_type=F32)

    tok = pl.BlockSpec((tm, D), lambda i: (i, 0))
    wide = pl.BlockSpec((tm, 3 * D), lambda i: (i, 0))
    yspec = pl.BlockSpec((tm, BRANCH_W), lambda i: (i, 0))
    return pl.pallas_call(
        body, name=name, grid=(T // tm,),
        in_specs=[tok, wide, yspec, yspec, yspec,
                  pl.BlockSpec((3, BRANCH_W, D), lambda i: (0, 0, 0)),
                  pl.BlockSpec((D, D), lambda i: (0, 0))],
        out_specs=[tok, wide, tok],
        out_shape=[_sds((T, D), F32), _sds((T, 3 * D), BF), _sds((T, D), BF)],
        compiler_params=_cp((PAR,)),
    )(x1, big, *ys, wb, wo)


def _merge_bwd(dx2, big, p, wb, wo, name):
    T, D = dx2.shape
    tm = min(TM, T)

    def body(dx_ref, gp_ref, p_ref, wb_ref, wo_ref, dp_ref, dgp_ref, dyc_ref, dyr_ref, dya_ref, dxb_ref):
        dxb = dx_ref[...].astype(BF)
        dxb_ref[...] = dxb
        dm = lax.dot_general(dxb, wo_ref[...], NT_DIMS, preferred_element_type=F32)
        for i, dy_ref in enumerate((dyc_ref, dyr_ref, dya_ref)):
            cols = slice(i * D, (i + 1) * D)
            gt = _sigmoid(gp_ref[:, cols].astype(F32))
            dpb = (dm * gt).astype(BF)
            dp_ref[:, cols] = dpb
            dgp_ref[:, cols] = (dm * p_ref[:, cols].astype(F32) * gt * (1.0 - gt)).astype(BF)
            dy_ref[...] = lax.dot_general(dpb, wb_ref[i], NT_DIMS, preferred_element_type=F32).astype(BF)

    tok = pl.BlockSpec((tm, D), lambda i: (i, 0))
    wide = pl.BlockSpec((tm, 3 * D), lambda i: (i, 0))
    yspec = pl.BlockSpec((tm, BRANCH_W), lambda i: (i, 0))
    return pl.pallas_call(
        body, name=name, grid=(T // tm,),
        in_specs=[tok, wide, wide,
                  pl.BlockSpec((3, BRANCH_W, D), lambda i: (0, 0, 0)),
                  pl.BlockSpec((D, D), lambda i: (0, 0))],
        out_specs=[wide, wide, yspec, yspec, yspec, tok],
        out_shape=[_sds((T, 3 * D), BF), _sds(big.shape, BF)] + [_sds((T, BRANCH_W), BF)] * 3 + [_sds((T, D), BF)],
        compiler_params=_cp((PAR,)),
    )(dx2, big, p, wb, wo)


def _loss_head(x, tgt, fw, name):
    T, D = x.shape
    tm = min(TM, T)

    def body(x_ref, t_ref, w_ref, loss_ref, dx_ref, dw_ref):
        @pl.when(pl.program_id(0) == 0)
        def _():
            loss_ref[...] = jnp.zeros_like(loss_ref)
            dw_ref[...] = jnp.zeros_like(dw_ref)

        xv = x_ref[...]
        wv = w_ref[...]
        e = xv * _rms_r(xv) * wv - t_ref[...]
        loss_ref[...] += 0.5 * jnp.sum(jnp.mean(e * e, axis=-1, keepdims=True))
        dx, dn = _rms_bwd(e * (1.0 / D), xv, wv)
        dx_ref[...] = dx
        dw_ref[...] += dn

    tok = pl.BlockSpec((tm, D), lambda i: (i, 0))
    return pl.pallas_call(
        body, name=name, grid=(T // tm,),
        in_specs=[tok, tok, pl.BlockSpec((1, D), lambda i: (0, 0))],
        out_specs=[pl.BlockSpec((8, LANE), lambda i: (0, 0)), tok, pl.BlockSpec((1, D), lambda i: (0, 0))],
        out_shape=[_sds((8, LANE), F32), _sds((T, D), F32), _sds((1, D), F32)],
        compiler_params=_cp((ARB,)),
    )(x, tgt, fw)


def _block_rows(rows, cols):
    cap = max(8, (1 << 18) // cols)
    best = None
    for r in range(8, rows + 1, 8):
        if rows % r == 0 and r <= cap:
            best = r
    return best if best is not None else rows


def _sum8(land, l, n_layers, name, prev=None, comm=None):
    _, rows, cols = land.shape
    br = _block_rows(rows, cols)

    def body(*refs):
        l_ref, o_ref = refs[0], refs[-1]

        def four(base):
            return ((l_ref[base + 3].astype(F32) + l_ref[base].astype(F32)) + l_ref[base + 1].astype(F32)
                    ) + l_ref[base + 2].astype(F32)

        o_ref[...] = four(0) + four(4)

    in_specs = [pl.BlockSpec((2 * N_SHARD, br, cols), lambda i: (0, i, 0))]
    args = [land]
    aliases = {}
    if prev is not None:
        in_specs.append(pl.BlockSpec(memory_space=pl.ANY))
        args.append(prev)
        aliases = {1: 0}
    main, extra = _call(
        body, name=name, grid=(rows // br,), args=args, in_specs=in_specs,
        out_specs=[pl.BlockSpec((None, br, cols), lambda i: (l, i, 0))],
        out_shape=[_sds((n_layers, rows, cols), F32)], aliases=aliases, sem=(ARB,), comm=comm)
    return main[0], extra


def _adamw_math(w, g, m, v):
    m = ADAM_B1 * m + (1.0 - ADAM_B1) * g
    v = ADAM_B2 * v + (1.0 - ADAM_B2) * (g * g)
    m_hat = m / (1.0 - ADAM_B1 ** ADAM_STEP)
    v_hat = v / (1.0 - ADAM_B2 ** ADAM_STEP)
    delta = -ADAM_LR * (m_hat / (jnp.sqrt(v_hat) + ADAM_EPS) + ADAM_WD * w)
    return delta, m, v


def _adamw(w, g, m, v, name):
    rows, cols = w.shape
    br = _block_rows(rows, cols)

    def body(w_ref, g_ref, m_ref, v_ref, d_ref, nm_ref, nv_ref):
        d, nm, nv = _adamw_math(w_ref[...], g_ref[...], m_ref[...], v_ref[...])
        d_ref[...] = d
        nm_ref[...] = nm
        nv_ref[...] = nv

    blk = pl.BlockSpec((br, cols), lambda i: (i, 0))
    return pl.pallas_call(
        body, name=name, grid=(rows // br,),
        in_specs=[blk] * 4, out_specs=[blk] * 3,
        out_shape=[_sds((rows, cols), F32)] * 3,
        compiler_params=_cp((PAR,)),
    )(w, g, m, v)


def _allreduce_small(v, name):
    rows = v.shape[0]
    flips = [(fx, fy, fc) for fx in (0, 1) for fy in (0, 1) for fc in (0, 1) if fx or fy or fc]

    def body(v_ref, o_ref, all_s, ssem, rsem):
        x, y, c = _place()

        def peer(f):
            return (x + f[0] - 2 * x * f[0], y + f[1] - 2 * y * f[1], c + f[2] - 2 * c * f[2])

        def slot(p):
            return all_s.at[4 * p[0] + 2 * p[1] + p[2]]

        def copy(k, f, owner):
            return pltpu.make_async_remote_copy(
                src_ref=v_ref, dst_ref=slot(owner), send_sem=ssem.at[k], recv_sem=rsem.at[k],
                device_id=peer(f), device_id_type=MESH)

        sends = [copy(k, f, (x, y, c)) for k, f in enumerate(flips)]
        for cp in sends:
            cp.start()
        all_s[4 * x + 2 * y + c] = v_ref[...]
        for k, f in enumerate(flips):
            copy(k, f, peer(f)).wait_recv()
        for cp in sends:
            cp.wait_send()
        acc = all_s[0]
        for d in range(1, 8):
            acc = acc + all_s[d]
        o_ref[...] = acc

    return pl.pallas_call(
        body, name=name,
        in_specs=[pl.BlockSpec(memory_space=pltpu.VMEM)],
        out_specs=pl.BlockSpec(memory_space=pltpu.VMEM),
        out_shape=_sds((rows, LANE), F32),
        scratch_shapes=[pltpu.VMEM((8, rows, LANE), F32), pltpu.SemaphoreType.DMA((7,)), pltpu.SemaphoreType.DMA((7,))],
    )(v)


BIG_NAMES = ("ffn1_w_gate", "ffn1_w_up", "ffn1_w_down", "w_in", "w_branch", "w_merge_gate", "w_out",
             "ffn2_w_gate", "ffn2_w_up", "ffn2_w_down")


FFN1 = ("ffn1_w_gate", "ffn1_w_up", "ffn1_w_down")
FFN2 = ("ffn2_w_gate", "ffn2_w_up", "ffn2_w_down")
MIX_IN = ("w_in", "w_merge_gate")
MIX_OUT = ("w_branch", "w_out")


def _keys(names, l):
    return [(n, l) for n in names]


def _local_step(x, tgt, small, convw_full, wx, n_layers):
    T, D = x.shape
    L = n_layers
    ns = N_SHARD
    dq = D // ns
    W = wx.w

    def hosted(call, keys, scatter=False):
        comm = wx.pieces(keys, scatter)
        main, extra = call(comm)
        if comm is not None:
            wx.arrived(keys, extra, scatter)
        return main

    def mixer_views(l):
        return _build_wbig(W[("w_merge_gate", l)], W[("w_in", l)], f"wbig_{l}")

    def out_views(l):
        wb4 = W[("w_branch", l)]
        wb = _copy_blocks(wb4, pl.BlockSpec((None, None, BRANCH_W, dq), lambda s_, i: (s_, i, 0, 0)),
                          _sds((3, BRANCH_W, D), wb4.dtype),
                          pl.BlockSpec((None, BRANCH_W, dq), lambda s_, i: (i, 0, s_)), (ns, 3), f"w_branch_whole_{l}")
        wo = W[("w_out", l)].reshape(D, D)
        return wb, wo

    tb = _ret_tables(T)
    rb_pad = jnp.pad(small["rel_bias"], ((0, 0), (0, 0), (0, RB_PAD - N_REL)))

    saved = []
    h = x
    for l in range(L):
        s = {"x0": h}
        nxt = l + 1
        x1, s["g1"], s["u1"] = hosted(
            lambda c: _ffn_fwd(h, small["ffn1_norm"][l][None], W[("ffn1_w_gate", l)], W[("ffn1_w_up", l)],
                               W[("ffn1_w_down", l)], f"ffn1_fwd_{l}", comm=c), _keys(MIX_IN, l))
        s["x1"] = x1
        s["wbig"] = mixer_views(l)
        big, s["h"] = _inproj_fwd(x1, small["mix_norm"][l][None], s["wbig"], f"inproj_fwd_{l}")
        s["big"] = big
        s["bias"] = jnp.transpose(_relbias_expand(rb_pad[l], f"relbias_expand_{l}"), (1, 0, 2))
        s["yc"] = _conv_fwd(big, convw_full[l], f"conv_fwd_{l}")
        s["yr"], s["o"], s["st"] = hosted(lambda c: _ret_fwd(big, tb, f"ret_fwd_{l}", comm=c), _keys(MIX_OUT, l))
        (s["ya"],) = hosted(lambda c: _att_fwd(big, s["bias"], f"att_fwd_{l}", comm=c), _keys(FFN2, l))
        s["wb"], s["wo"] = out_views(l)
        x2, s["p"], s["mg"] = _merge_fwd(x1, big, (s["yc"], s["yr"], s["ya"]), s["wb"], s["wo"], f"merge_fwd_{l}")
        s["x2"] = x2
        h, s["g2"], s["u2"] = hosted(
            lambda c: _ffn_fwd(x2, small["ffn2_norm"][l][None], W[("ffn2_w_gate", l)], W[("ffn2_w_up", l)],
                               W[("ffn2_w_down", l)], f"ffn2_fwd_{l}", comm=c), _keys(FFN1, nxt) if nxt < L else [])
        saved.append(s)

    loss_p, dx, d_final = _loss_head(h, tgt, small["final_norm"][None], "loss_head")

    gs = {"final_norm": d_final[0]}
    for k in ("ffn1_norm", "mix_norm", "ffn2_norm", "rel_bias", "conv_w"):
        gs[k] = [None] * L
    tk = min(2048, T)
    nk = T // tk

    def ffn_back(pre, l, dxo, x_in, g, u, first_keys, second_keys, between=None):
        nw = small[pre + "_norm"][l][None]
        dgv, duv, av, hb, dacc = hosted(
            lambda c: _ffn_bwd_hidden(dxo, x_in, nw, g, u, W[(pre + "_w_down", l)], f"{pre}_bwd_hidden_{l}", comm=c),
            first_keys, scatter=True)
        parts = (hb, dgv, duv, av, dacc)
        if between is not None:
            second_keys = between(parts)
        dxn, dn = hosted(
            lambda c: _ffn_bwd_resid(dgv, duv, W[(pre + "_w_gate", l)], W[(pre + "_w_up", l)], x_in, nw, dxo,
                                     f"{pre}_bwd_resid_{l}", comm=c),
            second_keys, scatter=True)
        gs[pre + "_norm"][l] = dn[0]
        return dxn, parts

    def ffn_grads(pre, l, hb, dgv, duv, av, dacc, chain=False, carry=()):
        fs = dgv.shape[-1]
        hspec = pl.BlockSpec((tk, D), lambda p, q, k: (k, 0))
        sspec = pl.BlockSpec((None, tk, fs), lambda p, q, k: (p, k, 0))
        down_spec = pl.BlockSpec((None, fs, D), lambda p, q, k: (p, 0, 0))
        jobs = [(pre + "_w_gate", dgv, hb, sspec, hspec, (ns, fs, D), down_spec),
                (pre + "_w_up", duv, hb, sspec, hspec, (ns, fs, D), down_spec),
                (pre + "_w_down", av, dacc, sspec, hspec, (ns, fs, D), down_spec)]
        for idx, (nm, a, b, a_spec, b_spec, shape, o_spec) in enumerate(jobs):
            def product(c):
                r = _tn(a, b, a_spec, b_spec, _sds(shape, BF), o_spec, (ns, 1, nk), f"d{nm}_{l}", comm=c)
                return (r, []) if c is None else r
            keys = list(carry) if idx == 0 else ([(jobs[0][0], l)] if chain and idx == 2 else [])
            wx.g[(nm, l)] = hosted(product, keys, scatter=True)
        return [(jobs[1][0], l), (jobs[2][0], l)] if chain else []

    for l in reversed(range(L)):
        s = saved[l]
        above = _keys(FFN1, l + 1) if l + 1 < L else [None] * 3
        dx, parts = ffn_back("ffn2", l, dx, s["x2"], s["g2"], s["u2"], [k for k in above[:1] if k],
                             [k for k in above[1:2] if k])
        ffn_grads("ffn2", l, *parts, carry=[k for k in above[2:] if k])
        dp, dbig, dyc, dyr, dya, dxb = _merge_bwd(dx, s["big"], s["p"], s["wb"], s["wo"], f"merge_bwd_{l}")
        wx.g[("w_out", l)] = _tn(
            s["mg"], dxb, pl.BlockSpec((tk, dq), lambda p, q, k: (k, p)), pl.BlockSpec((tk, D), lambda p, q, k: (k, 0)),
            _sds((ns, dq, D), BF), pl.BlockSpec((None, dq, D), lambda p, q, k: (p, 0, 0)), (ns, 1, nk), f"dw_out_{l}")
        gb = None
        for i, yv in enumerate((s["yc"], s["yr"], s["ya"])):
            gb = _tn(yv, dp,
                     pl.BlockSpec((tk, BRANCH_W), lambda p, q, k: (k, 0)),
                     pl.BlockSpec((tk, dq), lambda p, q, k, i=i: (k, i * ns + p)),
                     _sds((ns, 3, BRANCH_W, dq), BF),
                     pl.BlockSpec((None, None, BRANCH_W, dq), lambda p, q, k, i=i: (p, i, 0, 0)),
                     (ns, 1, nk), f"dw_branch{i}_{l}", prev=gb)
        wx.g[("w_branch", l)] = gb
        dbig, dcw = _conv_bwd(s["big"], dyc, convw_full[l], dbig, f"conv_bwd_{l}")
        gs["conv_w"][l] = dcw
        dbig = _ret_bwd(s["big"], s["o"], s["st"], dyr, tb, dbig, f"ret_bwd_{l}")
        dbig, dbias = hosted(lambda c: _att_bwd(s["big"], s["bias"], dya, dbig, f"att_bwd_{l}", comm=c),
                             _keys(FFN2, l), scatter=True)
        gs["rel_bias"][l] = _relbias_grad(jnp.transpose(dbias, (1, 0, 2)), f"relbias_grad_{l}")[:, :N_REL]
        n_in = N_SEG * BRANCH_W
        bn = 1024 if (3 * D) % 1024 == 0 else BRANCH_W
        dwp = _tn(s["h"], dbig, pl.BlockSpec((tk, D), lambda p, q, k: (k, 0)),
                  pl.BlockSpec((tk, bn), lambda p, q, k: (k, 3 * D // bn + q)),
                  _sds((D, n_in), BF), pl.BlockSpec((D, bn), lambda p, q, k: (0, q)), (1, n_in // bn, nk), f"dw_in_{l}")
        wx.g[("w_in", l)] = _ungroup_dw_in(dwp, ns, f"dw_in_shards_{l}")
        wx.g[("w_merge_gate", l)] = _tn_gates(s["h"], dbig, ns, tk, f"dw_merge_gate_{l}")
        dx, dn = hosted(
            lambda c: _inproj_bwd(dbig, s["wbig"], s["x1"], small["mix_norm"][l][None], dx, f"inproj_bwd_{l}", comm=c),
            [("w_in", l)], scatter=True)
        gs["mix_norm"][l] = dn[0]
        rest = [("w_merge_gate", l), ("w_branch", l), ("w_out", l)]
        if l == 0:
            dx, _ = ffn_back("ffn1", l, dx, s["x0"], s["g1"], s["u1"], rest, [],
                             between=lambda parts: ffn_grads("ffn1", 0, *parts, chain=True))
        else:
            dx, parts = ffn_back("ffn1", l, dx, s["x0"], s["g1"], s["u1"], rest, [])
            ffn_grads("ffn1", l, *parts)

    for k in ("ffn1_norm", "mix_norm", "ffn2_norm", "rel_bias", "conv_w"):
        gs[k] = jnp.stack(gs[k])
    return loss_p, dx, gs


class _Exchange:
    def __init__(self, shards):
        self.shards = shards
        self.w = {}
        self.g = {}
        self.landed = {}

    def own(self, key):
        return self.shards[key[0]][key[1]].astype(BF)

    def pieces(self, keys, scatter):
        if not keys:
            return None
        if scatter:
            return _Scatter([self.g[k] for k in keys])
        return _HalfGather([_halves(self.own(k)) for k in keys])

    def arrived(self, keys, outs, scatter):
        for k, o in zip(keys, outs):
            if scatter:
                self.landed[k] = o
            else:
                self.w[k] = o.reshape((N_SHARD,) + self.shards[k[0]].shape[1:])


def _halves(a):
    return a.reshape(2, -1, a.shape[-1])


TRANSPOSED_GRADS = ("ffn1_w_gate", "ffn1_w_up", "ffn2_w_gate", "ffn2_w_up")
W_NAMES = ("ffn1_norm", "ffn1_w_gate", "ffn1_w_up", "ffn1_w_down", "mix_norm", "w_in", "conv_w", "rel_bias", "w_branch",
           "w_merge_gate", "w_out", "ffn2_norm", "ffn2_w_gate", "ffn2_w_up", "ffn2_w_down", "final_norm")


def _as2d(a):
    return a.reshape(1, -1) if a.ndim == 1 else a.reshape(-1, a.shape[-1])


def kernel(x, ffn1_norm, ffn1_w_gate, ffn1_w_up, ffn1_w_down, mix_norm, w_in, conv_w, rel_bias, w_branch, w_merge_gate, w_out, ffn2_norm, ffn2_w_gate, ffn2_w_up, ffn2_w_down, final_norm, loss_target, m_ffn1_norm, m_ffn1_w_gate, m_ffn1_w_up, m_ffn1_w_down, m_mix_norm, m_w_in, m_conv_w, m_rel_bias, m_w_branch, m_w_merge_gate, m_w_out, m_ffn2_norm, m_ffn2_w_gate, m_ffn2_w_up, m_ffn2_w_down, m_final_norm, v_ffn1_norm, v_ffn1_w_gate, v_ffn1_w_up, v_ffn1_w_down, v_mix_norm, v_w_in, v_conv_w, v_rel_bias, v_w_branch, v_w_merge_gate, v_w_out, v_ffn2_norm, v_ffn2_w_gate, v_ffn2_w_up, v_ffn2_w_down, v_final_norm):
    given = dict(locals())
    w = {n: given[n] for n in W_NAMES}
    m = {n: given["m_" + n] for n in W_NAMES}
    v = {n: given["v_" + n] for n in W_NAMES}
    my_chip = 2 * lax.axis_index("x") + lax.axis_index("y")
    L = w_in.shape[0]

    wx = _Exchange({n: jnp.swapaxes(w[n], 1, 2) if n in TRANSPOSED_GRADS else w[n] for n in BIG_NAMES})
    first = _keys(FFN1, 0)
    got = _comm_alone(_HalfGather([_halves(wx.own(k)) for k in first] + [_halves(conv_w)]), "gather_first")
    wx.arrived(first, got[:-1], False)
    convw_full = jnp.transpose(got[-1].reshape((N_SHARD,) + conv_w.shape), (1, 2, 0, 3)).reshape(
        conv_w.shape[0], conv_w.shape[1], -1)

    small = {n: w[n] for n in ("ffn1_norm", "mix_norm", "ffn2_norm", "final_norm", "rel_bias")}
    loss_p, grad_x, gs = _local_step(x[0], loss_target[0], small, convw_full, wx, L)

    sums = []
    for n in BIG_NAMES:
        acc = None
        for l in range(L):
            a = wx.landed[(n, l)]
            acc, _ = _sum8(a.reshape(a.shape[0], -1, a.shape[-1]), l, L, f"sum8_{n}_{l}", prev=acc)
        sums.append(acc.reshape(-1, acc.shape[-1]))

    parts = [gs["ffn1_norm"].reshape(-1), gs["mix_norm"].reshape(-1), gs["ffn2_norm"].reshape(-1),
             gs["final_norm"].reshape(-1), gs["rel_bias"].reshape(-1), gs["conv_w"].reshape(-1), loss_p[0]]
    sizes = [p.shape[0] for p in parts]
    flat = jnp.concatenate(parts)
    rows = -(-flat.shape[0] // (8 * LANE)) * 8
    flat = jnp.pad(flat, (0, rows * LANE - flat.shape[0])).reshape(rows, LANE)
    red = _allreduce_small(flat, "allreduce_small").reshape(-1)
    offs = [0]
    for sz in sizes:
        offs.append(offs[-1] + sz)
    sm = {}
    for i, n in enumerate(("ffn1_norm", "mix_norm", "ffn2_norm", "final_norm", "rel_bias", "conv_w")):
        sm[n] = red[offs[i]:offs[i + 1]]
    loss = red[offs[6]]
    sm["conv_w"] = lax.dynamic_slice_in_dim(sm["conv_w"].reshape(conv_w.shape[0], conv_w.shape[1], -1),
                                            my_chip * conv_w.shape[2], conv_w.shape[2], axis=2)

    grads, deltas, new_m, new_v = {}, {}, {}, {}
    big_sum = dict(zip(BIG_NAMES, sums))
    for n in W_NAMES:
        flip = n in TRANSPOSED_GRADS

        def view(a):
            return jnp.swapaxes(a, 1, 2) if flip else a

        shape = view(w[n]).shape
        g = big_sum[n] if n in big_sum else _as2d(sm[n].reshape(shape))
        out = _adamw(_as2d(view(w[n])), g, _as2d(view(m[n])), _as2d(view(v[n])), f"adamw_{n}")
        grads[n], deltas[n], new_m[n], new_v[n] = (view(o.reshape(shape)) for o in [g] + list(out))

    return (loss, grad_x[None], *[grads[n] for n in W_NAMES], *[deltas[n] for n in W_NAMES],
            *[new_m[n] for n in W_NAMES], *[new_v[n] for n in W_NAMES])
```

```python
import functools
import math

import jax
import jax.numpy as jnp
from jax import lax
from jax.experimental import pallas as pl
from jax.experimental.pallas import tpu as pltpu

F32 = jnp.float32
BF = jnp.bfloat16
MESH = pl.DeviceIdType.MESH
ARB = "arbitrary"
PAR = "parallel"

EPS = 1e-6
NEG_INF = -1e30
ROPE_BASE = 10000.0
CHUNK = 64
BRANCH_W = 512
H_RET = 4
DK_RET = 128
H_ATT = 8
DH_ATT = 64
N_PREV = 8
REL_CLIP = 128
N_REL = 2 * REL_CLIP + 1
N_SHARD = 4
LANE = 128
RET_L = 512
ATT_TQ = 128
ATT_SUB = 4
ATT_PAD = N_PREV * CHUNK
ATT_SPAN = ATT_TQ + ATT_PAD
ATT_TOEP = 2 * REL_CLIP
RB_PAD = 264
TM = 512
TM_FFN = 1024

ADAM_LR = 0.001
ADAM_B1 = 0.9
ADAM_B2 = 0.999
ADAM_EPS = 1e-08
ADAM_WD = 0.01
ADAM_STEP = 10

NT_DIMS = (((1,), (1,)), ((), ()))
TN_DIMS = (((0,), (0,)), ((), ()))


def _cp(sem, vmem_mb=48):
    return pltpu.CompilerParams(dimension_semantics=sem, vmem_limit_bytes=vmem_mb << 20)


def _sds(shape, dtype):
    return jax.ShapeDtypeStruct(tuple(shape), dtype)


def _rms_r(x):
    return lax.rsqrt(jnp.mean(x * x, axis=-1, keepdims=True) + EPS)


def _sigmoid(x):
    return 0.5 * jnp.tanh(0.5 * x) + 0.5


def _rms_bwd(dh, xv, nw):
    r = _rms_r(xv)
    xh = xv * r
    dxh = dh * nw
    dx = r * (dxh - xh * jnp.mean(dxh * xh, axis=-1, keepdims=True))
    return dx, jnp.sum(dh * xh, axis=0, keepdims=True)


def _place():
    return lax.axis_index("x"), lax.axis_index("y"), lax.axis_index("c")


def _other_chips(x, y):
    return [(1 - x, y), (x, 1 - y), (1 - x, 1 - y)]


class _Scatter:
    def __init__(self, srcs):
        self.srcs = list(srcs)
        n = len(self.srcs)
        self.out_shape = [_sds((2 * N_SHARD,) + s.shape[1:], s.dtype) for s in self.srcs]
        self.scratch = [pltpu.SemaphoreType.DMA((n,)), pltpu.SemaphoreType.DMA((3, n)), pltpu.SemaphoreType.DMA((3, n)),
                        pltpu.SemaphoreType.DMA((4, n)), pltpu.SemaphoreType.DMA((4, n))]

    def _plan(self, src, dst, sems, want):
        lsem, s1, r1, s2, r2 = sems
        x, y, c = _place()
        mine = 2 * x + y
        n = len(src)
        chips = list(enumerate(_other_chips(x, y)))

        def copy(s_ref, d_ref, ssem, rsem, to):
            return pltpu.make_async_remote_copy(src_ref=s_ref, dst_ref=d_ref, send_sem=ssem, recv_sem=rsem,
                                                device_id=to, device_id_type=MESH)

        local = [pltpu.make_async_copy(src[k].at[mine], dst[k].at[3], lsem.at[k]) for k in range(n)
                 ] if "local" in want else []
        sends = [copy(src[k].at[2 * ch[0] + ch[1]], dst[k].at[j], s1.at[j, k], r1.at[j, k], (ch[0], ch[1], c))
                 for j, ch in chips for k in range(n)] if "sends" in want else []
        passes = [copy(dst[k].at[j], dst[k].at[4 + j], s2.at[j, k], r2.at[j, k], (x, y, 1 - c))
                  for j, ch in chips for k in range(n)] if "passes" in want else []
        own_pass = [copy(src[k].at[mine], dst[k].at[7], s2.at[3, k], r2.at[3, k], (x, y, 1 - c))
                    for k in range(n)] if "own_pass" in want else []
        return local, sends, passes, own_pass

    def start(self, src, dst, sems):
        local, sends, _, own_pass = self._plan(src, dst, sems, ("local", "sends", "own_pass"))
        for cp in local + sends + own_pass:
            cp.start()

    def relay(self, src, dst, sems):
        _, sends, passes, _ = self._plan(src, dst, sems, ("sends", "passes"))
        for land, fwd in zip(sends, passes):
            land.wait_recv()
            fwd.start()

    def finish(self, src, dst, sems):
        local, sends, passes, own_pass = self._plan(src, dst, sems, ("local", "sends", "passes", "own_pass"))
        for cp in passes + own_pass:
            cp.wait_recv()
        for cp in sends + passes + own_pass:
            cp.wait_send()
        for cp in local:
            cp.wait()

    def wait(self, src, dst, sems):
        self.relay(src, dst, sems)
        self.finish(src, dst, sems)


class _HalfGather:
    def __init__(self, srcs):
        self.srcs = list(srcs)
        n = len(self.srcs)
        self.out_shape = [_sds((N_SHARD,) + s.shape, s.dtype) for s in self.srcs]
        self.scratch = [pltpu.SemaphoreType.DMA((n,))] + [pltpu.SemaphoreType.DMA((3, n)) for _ in range(4)]

    def _plan(self, src, dst, sems, want):
        lsem, s1, r1, s2, r2 = sems
        x, y, c = _place()
        mine = 2 * x + y
        n = len(src)
        chips = [(j, ch, 2 * ch[0] + ch[1]) for j, ch in enumerate(_other_chips(x, y))]

        def copy(s_ref, d_ref, ssem, rsem, to):
            return pltpu.make_async_remote_copy(src_ref=s_ref, dst_ref=d_ref, send_sem=ssem, recv_sem=rsem,
                                                device_id=to, device_id_type=MESH)

        def over(kind, make):
            return [make(j, ch, slot, k) for j, ch, slot in chips for k in range(n)] if kind in want else []

        local = [pltpu.make_async_copy(src[k], dst[k].at[mine], lsem.at[k]) for k in range(n)] if "local" in want else []
        sends = over("sends", lambda j, ch, slot, k: copy(src[k].at[c], dst[k].at[mine, c], s1.at[j, k], r1.at[j, k],
                                                          (ch[0], ch[1], c)))
        lands = over("lands", lambda j, ch, slot, k: copy(src[k].at[c], dst[k].at[slot, c], s1.at[j, k], r1.at[j, k],
                                                          (ch[0], ch[1], c)))
        passes = over("passes", lambda j, ch, slot, k: copy(dst[k].at[slot, c], dst[k].at[slot, c], s2.at[j, k],
                                                            r2.at[j, k], (x, y, 1 - c)))
        gets = over("gets", lambda j, ch, slot, k: copy(dst[k].at[slot, 1 - c], dst[k].at[slot, 1 - c], s2.at[j, k],
                                                        r2.at[j, k], (x, y, 1 - c)))
        return local, sends, lands, passes, gets

    def start(self, src, dst, sems):
        lsem, s1, r1, s2, r2 = sems
        x, y, c = _place()
        mine = 2 * x + y
        for k in range(len(src)):
            pltpu.make_async_copy(src[k], dst[k].at[mine], lsem.at[k]).start()
        for j, ch in enumerate(_other_chips(x, y)):
            for k in range(len(src)):
                pltpu.make_async_remote_copy(
                    src_ref=src[k].at[c], dst_ref=dst[k].at[mine, c], send_sem=s1.at[j, k], recv_sem=r1.at[j, k],
                    device_id=(ch[0], ch[1], c), device_id_type=MESH).start()

    def relay(self, src, dst, sems):
        _, _, lands, passes, _ = self._plan(src, dst, sems, ("lands", "passes"))
        for land, fwd in zip(lands, passes):
            land.wait_recv()
            fwd.start()

    def finish(self, src, dst, sems):
        local, sends, _, passes, gets = self._plan(src, dst, sems, ("local", "sends", "passes", "gets"))
        for cp in gets:
            cp.wait_recv()
        for cp in sends + passes:
            cp.wait_send()
        for cp in local:
            cp.wait()

    def wait(self, src, dst, sems):
        self.relay(src, dst, sems)
        self.finish(src, dst, sems)


def _call(body, *, name, args, in_specs, out_specs, out_shape, grid=(), scratch_shapes=(), sem=None, comm=None,
          aliases=None, vmem_mb=48):
    in_specs, out_specs, out_shape = list(in_specs), list(out_specs), list(out_shape)
    scratch, args = list(scratch_shapes), list(args)
    n_in, n_out, n_scr = len(in_specs), len(out_specs), len(scratch)
    if comm is None:
        def kernel_body(*refs):
            body(*refs)
    else:
        c_in, c_out = len(comm.srcs), len(comm.out_shape)

        def kernel_body(*refs):
            o0 = n_in + c_in
            s0 = o0 + n_out + c_out
            cin, cout, sems = refs[n_in:o0], refs[o0 + n_out:s0], refs[s0 + n_scr:]
            main = refs[:n_in] + refs[o0:o0 + n_out] + refs[s0:s0 + n_scr]
            if grid:
                ids = [pl.program_id(a) for a in range(len(grid))]
                first = functools.reduce(lambda p, q: p & q, [i == 0 for i in ids])
                last = functools.reduce(lambda p, q: p & q, [i == g - 1 for i, g in zip(ids, grid)])

                @pl.when(first)
                def _():
                    comm.start(cin, cout, sems)

                body(*main)

                steps = math.prod(grid)
                if hasattr(comm, "relay") and steps >= 4:
                    flat = functools.reduce(lambda p, q: p + q, [i * math.prod(grid[a + 1:]) for a, i in enumerate(ids)])

                    @pl.when(flat == (5 * steps) // 6)
                    def _():
                        comm.relay(cin, cout, sems)

                    @pl.when(last)
                    def _():
                        comm.finish(cin, cout, sems)
                else:
                    @pl.when(last)
                    def _():
                        comm.wait(cin, cout, sems)
            else:
                comm.start(cin, cout, sems)
                body(*main)
                comm.wait(cin, cout, sems)

        hbm = pl.BlockSpec(memory_space=pl.ANY)
        in_specs += [hbm] * c_in
        out_specs += [hbm] * c_out
        out_shape += comm.out_shape
        scratch += comm.scratch
        args += comm.srcs
    params = dict(vmem_limit_bytes=vmem_mb << 20)
    if grid:
        params["dimension_semantics"] = sem
    outs = pl.pallas_call(
        kernel_body, name=name, grid=grid, in_specs=in_specs, out_specs=out_specs, out_shape=out_shape,
        scratch_shapes=scratch, input_output_aliases=aliases or {}, compiler_params=pltpu.CompilerParams(**params),
    )(*args)
    return list(outs[:n_out]), list(outs[n_out:])


def _ffn_fwd(x, nw, wg, wu, wd, name, comm=None):
    T, D = x.shape
    ns, fs, _ = wg.shape
    tm = min(TM_FFN, T)

    def body(x_ref, nw_ref, wg_ref, wu_ref, wd_ref, xo_ref, g_ref, u_ref, h_s, acc_s):
        j = pl.program_id(1)

        @pl.when(j == 0)
        def _():
            xv = x_ref[...]
            h_s[...] = (xv * _rms_r(xv) * nw_ref[...]).astype(BF)
            acc_s[...] = jnp.zeros_like(acc_s)

        h = h_s[...]
        gb = lax.dot_general(h, wg_ref[...], NT_DIMS, preferred_element_type=F32).astype(BF)
        ub = lax.dot_general(h, wu_ref[...], NT_DIMS, preferred_element_type=F32).astype(BF)
        g_ref[...] = gb
        u_ref[...] = ub
        g = gb.astype(F32)
        a = (g * _sigmoid(g) * ub.astype(F32)).astype(BF)
        acc_s[...] += jnp.dot(a, wd_ref[...], preferred_element_type=F32)

        @pl.when(j == ns - 1)
        def _():
            xo_ref[...] = x_ref[...] + 0.5 * acc_s[...]

    wspec = pl.BlockSpec((None, fs, D), lambda i, j: (j, 0, 0))
    return _call(
        body, name=name, grid=(T // tm, ns), args=(x, nw, wg, wu, wd), comm=comm, vmem_mb=56,
        in_specs=[pl.BlockSpec((tm, D), lambda i, j: (i, 0)),
                  pl.BlockSpec((1, D), lambda i, j: (0, 0)),
                  wspec, wspec,
                  pl.BlockSpec((None, fs, D), lambda i, j: (j, 0, 0))],
        out_specs=[pl.BlockSpec((tm, D), lambda i, j: (i, 0)),
                   pl.BlockSpec((None, tm, fs), lambda i, j: (j, i, 0)),
                   pl.BlockSpec((None, tm, fs), lambda i, j: (j, i, 0))],
        out_shape=[_sds((T, D), F32), _sds((ns, T, fs), BF), _sds((ns, T, fs), BF)],
        scratch_shapes=[pltpu.VMEM((tm, D), BF), pltpu.VMEM((tm, D), F32)],
        sem=(ARB, ARB))


def _ffn_bwd_hidden(dxo, x, nw, g, u, wd, name, comm=None):
    T, D = x.shape
    ns, fs, _ = wd.shape
    tm = min(TM_FFN, T)

    def body(dxo_ref, x_ref, nw_ref, g_ref, u_ref, wd_ref, dg_ref, du_ref, a_ref, h_ref, dacc_ref, dacc_s):
        @pl.when(pl.program_id(1) == 0)
        def _():
            xv = x_ref[...]
            h_ref[...] = (xv * _rms_r(xv) * nw_ref[...]).astype(BF)
            db = (0.5 * dxo_ref[...]).astype(BF)
            dacc_ref[...] = db
            dacc_s[...] = db

        da = lax.dot_general(dacc_s[...], wd_ref[...], NT_DIMS, preferred_element_type=F32)
        gv = g_ref[...].astype(F32)
        uv = u_ref[...].astype(F32)
        s = _sigmoid(gv)
        sg = gv * s
        a_ref[...] = (sg * uv).astype(BF)
        du_ref[...] = (da * sg).astype(BF)
        dg_ref[...] = (da * uv * (s * (1.0 + gv * (1.0 - s)))).astype(BF)

    tok = pl.BlockSpec((tm, D), lambda i, j: (i, 0))
    hid = pl.BlockSpec((None, tm, fs), lambda i, j: (j, i, 0))
    return _call(
        body, name=name, grid=(T // tm, ns), args=(dxo, x, nw, g, u, wd), comm=comm, vmem_mb=56,
        in_specs=[tok, tok, pl.BlockSpec((1, D), lambda i, j: (0, 0)), hid, hid,
                  pl.BlockSpec((None, fs, D), lambda i, j: (j, 0, 0))],
        out_specs=[hid, hid, hid, tok, tok],
        out_shape=[_sds((ns, T, fs), BF)] * 3 + [_sds((T, D), BF)] * 2,
        scratch_shapes=[pltpu.VMEM((tm, D), BF)],
        sem=(ARB, ARB))


def _ffn_bwd_resid(dg, du, wg, wu, x, nw, dxo, name, comm=None):
    T, D = x.shape
    ns, fs, _ = wg.shape
    tm = min(TM_FFN, T)

    def body(dg_ref, du_ref, wg_ref, wu_ref, x_ref, nw_ref, dxo_ref, dx_ref, dnw_ref, acc_s):
        i = pl.program_id(0)
        j = pl.program_id(1)
        prod = (jnp.dot(dg_ref[...], wg_ref[...], preferred_element_type=F32)
                + jnp.dot(du_ref[...], wu_ref[...], preferred_element_type=F32))

        @pl.when((i == 0) & (j == 0))
        def _():
            dnw_ref[...] = jnp.zeros_like(dnw_ref)

        @pl.when(j == 0)
        def _():
            acc_s[...] = prod

        @pl.when(j > 0)
        def _():
            acc_s[...] += prod

        @pl.when(j == ns - 1)
        def _():
            dx, dn = _rms_bwd(acc_s[...], x_ref[...], nw_ref[...])
            dx_ref[...] = dxo_ref[...] + dx
            dnw_ref[...] += dn

    tok = pl.BlockSpec((tm, D), lambda i, j: (i, 0))
    row = pl.BlockSpec((1, D), lambda i, j: (0, 0))
    hid = pl.BlockSpec((None, tm, fs), lambda i, j: (j, i, 0))
    wspec = pl.BlockSpec((None, fs, D), lambda i, j: (j, 0, 0))
    return _call(
        body, name=name, grid=(T // tm, ns), args=(dg, du, wg, wu, x, nw, dxo), comm=comm, vmem_mb=56,
        in_specs=[hid, hid, wspec, wspec, tok, row, tok],
        out_specs=[tok, row],
        out_shape=[_sds((T, D), F32), _sds((1, D), F32)],
        scratch_shapes=[pltpu.VMEM((tm, D), F32)],
        sem=(ARB, ARB))


def _tn(a, b, a_spec, b_spec, out_shape, out_spec, grid, name, prev=None, comm=None):
    nk = grid[-1]
    acc_shape = tuple(d for d in out_spec.block_shape if d is not None)

    def body(*refs):
        a_ref, b_ref = refs[0], refs[1]
        o_ref, acc = refs[-2], refs[-1]
        k = pl.program_id(2)
        prod = lax.dot_general(a_ref[...], b_ref[...], TN_DIMS, preferred_element_type=F32)

        @pl.when(k == 0)
        def _():
            acc[...] = prod

        @pl.when(k > 0)
        def _():
            acc[...] += prod

        @pl.when(k == nk - 1)
        def _():
            o_ref[...] = acc[...].astype(o_ref.dtype)

    in_specs = [a_spec, b_spec]
    args = [a, b]
    aliases = {}
    if prev is not None:
        in_specs.append(pl.BlockSpec(memory_space=pl.ANY))
        args.append(prev)
        aliases = {2: 0}
    main, extra = _call(
        body, name=name, grid=grid, args=args, in_specs=in_specs, out_specs=[out_spec], out_shape=[out_shape],
        scratch_shapes=[pltpu.VMEM(acc_shape, F32)], aliases=aliases, sem=(ARB, ARB, ARB), comm=comm)
    return main[0] if comm is None else (main[0], extra)


def _tn_gates(h, dbig, ns, tk, name):
    T, D = h.shape
    dq = D // ns
    nk = T // tk

    def body(a_ref, b_ref, o_ref, acc):
        k = pl.program_id(1)
        prod = lax.dot_general(a_ref[...], b_ref[...], TN_DIMS, preferred_element_type=F32)

        @pl.when(k == 0)
        def _():
            acc[...] = prod

        @pl.when(k > 0)
        def _():
            acc[...] += prod

        @pl.when(k == nk - 1)
        def _():
            for s in range(ns):
                o_ref[s] = acc[s * dq:(s + 1) * dq, :].astype(o_ref.dtype)

    return pl.pallas_call(
        body, name=name, grid=(3, nk),
        in_specs=[pl.BlockSpec((tk, D), lambda q, k: (k, 0)), pl.BlockSpec((tk, D), lambda q, k: (k, q))],
        out_specs=pl.BlockSpec((ns, None, dq, D), lambda q, k: (0, q, 0, 0)),
        out_shape=_sds((ns, 3, dq, D), BF),
        scratch_shapes=[pltpu.VMEM((D, D), F32)],
        compiler_params=_cp((PAR, ARB)),
    )(h, dbig)


def _inproj_fwd(x, nw, wbig, name):
    T, D = x.shape
    nb = wbig.shape[-1]
    tm = min(2 * TM, T)
    bn = min(2048, nb)

    def body(x_ref, nw_ref, w_ref, o_ref, h_ref, h_s):
        @pl.when(pl.program_id(1) == 0)
        def _():
            xv = x_ref[...]
            hb = (xv * _rms_r(xv) * nw_ref[...]).astype(BF)
            h_s[...] = hb
            h_ref[...] = hb

        o_ref[...] = jnp.dot(h_s[...], w_ref[...], preferred_element_type=F32).astype(BF)

    return pl.pallas_call(
        body, name=name, grid=(T // tm, nb // bn),
        in_specs=[pl.BlockSpec((tm, D), lambda i, n: (i, 0)),
                  pl.BlockSpec((1, D), lambda i, n: (0, 0)),
                  pl.BlockSpec((D, bn), lambda i, n: (0, n))],
        out_specs=[pl.BlockSpec((tm, bn), lambda i, n: (i, n)),
                   pl.BlockSpec((tm, D), lambda i, n: (i, 0))],
        out_shape=[_sds((T, nb), BF), _sds((T, D), BF)],
        scratch_shapes=[pltpu.VMEM((tm, D), BF)],
        compiler_params=_cp((PAR, ARB)),
    )(x, nw, wbig)


def _inproj_bwd(dbig, wbig, x, nw, dxin, name, comm=None):
    T, D = x.shape
    nb = wbig.shape[-1]
    tm = min(TM_FFN, T)
    tk = min(2048, nb)
    nk = nb // tk

    def body(a_ref, w_ref, x_ref, nw_ref, dxin_ref, dx_ref, dnw_ref, acc_s):
        i = pl.program_id(0)
        k = pl.program_id(1)
        prod = lax.dot_general(a_ref[...], w_ref[...], NT_DIMS, preferred_element_type=F32)

        @pl.when((i == 0) & (k == 0))
        def _():
            dnw_ref[...] = jnp.zeros_like(dnw_ref)

        @pl.when(k == 0)
        def _():
            acc_s[...] = prod

        @pl.when(k > 0)
        def _():
            acc_s[...] += prod

        @pl.when(k == nk - 1)
        def _():
            dx, dn = _rms_bwd(acc_s[...], x_ref[...], nw_ref[...])
            dx_ref[...] = dxin_ref[...] + dx
            dnw_ref[...] += dn

    tok = pl.BlockSpec((tm, D), lambda i, k: (i, 0))
    row = pl.BlockSpec((1, D), lambda i, k: (0, 0))
    return _call(
        body, name=name, grid=(T // tm, nk), args=(dbig, wbig, x, nw, dxin), comm=comm, vmem_mb=56,
        in_specs=[pl.BlockSpec((tm, tk), lambda i, k: (i, k)),
                  pl.BlockSpec((D, tk), lambda i, k: (0, k)),
                  tok, row, tok],
        out_specs=[tok, row],
        out_shape=[_sds((T, D), F32), _sds((1, D), F32)],
        scratch_shapes=[pltpu.VMEM((tm, D), F32)],
        sem=(ARB, ARB))


CONV_R = 512
CONV_BASE, CONV_GROUP = 0, 3
ATT_BASE, ATT_GROUP = 12, 3
RET_BASE, RET_GROUP = 24, 4
N_SEG = 10


N_IN_BLOCKS = N_SEG * BRANCH_W // LANE


def _orig_block(p):
    nblk = BRANCH_W // LANE
    qa, qr = p - ATT_BASE, p - RET_BASE
    conv = (p % CONV_GROUP) * nblk + p // CONV_GROUP
    att = (7 + qa % ATT_GROUP) * nblk + qa // ATT_GROUP
    ret = (3 + qr % RET_GROUP) * nblk + qr // RET_GROUP
    return jnp.where(p < ATT_BASE, conv, jnp.where(p < RET_BASE, att, ret))


def _copy_blocks(src, in_spec, out_shape, out_spec, grid, name, prev=None):
    def body(*refs):
        refs[-1][...] = refs[0][...]

    in_specs, args, aliases = [in_spec], [src], {}
    if prev is not None:
        in_specs.append(pl.BlockSpec(memory_space=pl.ANY))
        args.append(prev)
        aliases = {1: 0}
    return pl.pallas_call(
        body, name=name, grid=grid, in_specs=in_specs, out_specs=out_spec, out_shape=out_shape,
        input_output_aliases=aliases, compiler_params=_cp(tuple(PAR for _ in grid)),
    )(*args)


def _build_wbig(gates4, win4, name):
    ns, _, dq, D = gates4.shape
    per = win4.shape[-1] // LANE
    shape = _sds((D, 3 * D + N_IN_BLOCKS * LANE), gates4.dtype)
    out = _copy_blocks(gates4, pl.BlockSpec((None, None, dq, D), lambda s, i: (s, i, 0, 0)), shape,
                       pl.BlockSpec((dq, D), lambda s, i: (s, i)), (ns, 3), name + "_gates")
    return _copy_blocks(
        win4, pl.BlockSpec((None, D, LANE), lambda p: (_orig_block(p) // per, 0, _orig_block(p) % per)), shape,
        pl.BlockSpec((D, LANE), lambda p: (0, 3 * D // LANE + p)), (N_IN_BLOCKS,), name + "_in", prev=out)


def _ungroup_dw_in(dwp, ns, name):
    D = dwp.shape[0]
    per = N_IN_BLOCKS // ns
    return _copy_blocks(
        dwp, pl.BlockSpec((D, LANE), lambda p: (0, p)), _sds((ns, D, per * LANE), dwp.dtype),
        pl.BlockSpec((None, D, LANE), lambda p: (_orig_block(p) // per, 0, _orig_block(p) % per)), (N_IN_BLOCKS,), name)


def _seg0(big):
    return (big.shape[1] - N_SEG * BRANCH_W) // LANE


def _group_spec(big, base, group, rows, where):
    first = (_seg0(big) + base) // group
    assert first * group == _seg0(big) + base

    def index(*ids):
        r, g = where(*ids)
        return r, first + g

    return pl.BlockSpec((rows, group * LANE), index)


CU, CB, CC = (slice(k * LANE, (k + 1) * LANE) for k in range(3))
AQ, AK, AV = CU, CB, CC
RQ, RK, RV, RG = (slice(k * LANE, (k + 1) * LANE) for k in range(4))


def _conv_fwd(big, cw, name):
    T = big.shape[0]
    R = min(CONV_R, T)

    def body(g_ref, w_ref, y_ref, z_s):
        z_s[pl.ds(0, 8), :] = jnp.zeros((8, LANE), F32)

        def fill(t, c):
            sl = pl.ds(pl.multiple_of(t * R, R), R)
            z_s[pl.ds(pl.multiple_of(t * R + 8, 8), R), :] = g_ref[sl, CC].astype(F32) * g_ref[sl, CU].astype(F32)
            return c

        lax.fori_loop(0, T // R, fill, 0)
        w0, w1, w2 = w_ref[0:1, :], w_ref[1:2, :], w_ref[2:3, :]

        def step(t, c):
            zz = z_s[pl.ds(pl.multiple_of(t * R, R), R + 8), :]
            z0 = zz[8:]
            z1 = pltpu.roll(zz, 1, 0)[8:]
            z2 = pltpu.roll(zz, 2, 0)[8:]
            sl = pl.ds(pl.multiple_of(t * R, R), R)
            y_ref[sl, :] = (g_ref[sl, CB].astype(F32) * (w2 * z0 + w1 * z1 + w0 * z2)).astype(BF)
            return c

        lax.fori_loop(0, T // R, step, 0)

    return pl.pallas_call(
        body, name=name, grid=(BRANCH_W // LANE,),
        in_specs=[_group_spec(big, CONV_BASE, CONV_GROUP, T, lambda j: (0, j)),
                  pl.BlockSpec((3, LANE), lambda j: (0, j))],
        out_specs=pl.BlockSpec((T, LANE), lambda j: (0, j)),
        out_shape=_sds((T, BRANCH_W), BF),
        scratch_shapes=[pltpu.VMEM((T + 8, LANE), F32)],
        compiler_params=_cp((PAR,)),
    )(big, cw)


def _conv_bwd(big, dy, cw, dbig, name):
    T = big.shape[0]
    R = min(CONV_R, T)

    def body(g_ref, dy_ref, w_ref, _, o_ref, dw_ref, z_s, d_s):
        z_s[pl.ds(0, 8), :] = jnp.zeros((8, LANE), F32)
        d_s[pl.ds(T, 8), :] = jnp.zeros((8, LANE), F32)

        def fill(t, c):
            sl = pl.ds(pl.multiple_of(t * R, R), R)
            z_s[pl.ds(pl.multiple_of(t * R + 8, 8), R), :] = g_ref[sl, CC].astype(F32) * g_ref[sl, CU].astype(F32)
            d_s[sl, :] = dy_ref[sl, :].astype(F32) * g_ref[sl, CB].astype(F32)
            return c

        lax.fori_loop(0, T // R, fill, 0)
        w0, w1, w2 = w_ref[0:1, :], w_ref[1:2, :], w_ref[2:3, :]

        def step(t, carry):
            a0, a1, a2 = carry
            zz = z_s[pl.ds(pl.multiple_of(t * R, R), R + 8), :]
            z0 = zz[8:]
            z1 = pltpu.roll(zz, 1, 0)[8:]
            z2 = pltpu.roll(zz, 2, 0)[8:]
            sl = pl.ds(pl.multiple_of(t * R, R), R)
            dyv = dy_ref[sl, :].astype(F32)
            o_ref[sl, CB] = (dyv * (w2 * z0 + w1 * z1 + w0 * z2)).astype(BF)
            dd = d_s[pl.ds(pl.multiple_of(t * R, R), R + 8), :]
            d0 = dd[:R]
            d1 = pltpu.roll(dd, R + 7, 0)[:R]
            d2 = pltpu.roll(dd, R + 6, 0)[:R]
            dz = w2 * d0 + w1 * d1 + w0 * d2
            o_ref[sl, CC] = (dz * g_ref[sl, CU].astype(F32)).astype(BF)
            o_ref[sl, CU] = (dz * g_ref[sl, CC].astype(F32)).astype(BF)
            a0 = a0 + jnp.sum(d0 * z2, axis=0, keepdims=True)
            a1 = a1 + jnp.sum(d0 * z1, axis=0, keepdims=True)
            a2 = a2 + jnp.sum(d0 * z0, axis=0, keepdims=True)
            return a0, a1, a2

        zero = jnp.zeros((1, LANE), F32)
        a0, a1, a2 = lax.fori_loop(0, T // R, step, (zero, zero, zero))
        dw_ref[0:1, :] = a0
        dw_ref[1:2, :] = a1
        dw_ref[2:3, :] = a2

    group = _group_spec(big, CONV_BASE, CONV_GROUP, T, lambda j: (0, j))
    w = pl.BlockSpec((3, LANE), lambda j: (0, j))
    return pl.pallas_call(
        body, name=name, grid=(BRANCH_W // LANE,),
        in_specs=[group, pl.BlockSpec((T, LANE), lambda j: (0, j)), w, pl.BlockSpec(memory_space=pl.ANY)],
        out_specs=[group, w],
        out_shape=[_sds(dbig.shape, BF), _sds((3, BRANCH_W), F32)],
        scratch_shapes=[pltpu.VMEM((T + 8, LANE), F32), pltpu.VMEM((T + 8, LANE), F32)],
        input_output_aliases={3: 0}, compiler_params=_cp((PAR,)),
    )(big, dy, cw, dbig)


def _ret_tables(T):
    L = min(RET_L, T)
    hh = jnp.arange(H_RET, dtype=F32)
    lg = jnp.log1p(-jnp.exp2(-5.0 - hh))
    n = jnp.arange(L, dtype=F32)
    a = jnp.exp(lg[:, None] * (n + 1.0))
    b = jnp.exp(lg[:, None] * (L - 1.0 - n))
    gl = jnp.exp(lg * L)
    ch = jnp.arange(L) // CHUNK
    m = jnp.exp(lg[:, None, None] * jnp.abs(n[:, None] - n[None, :])) * (ch[None, :] <= ch[:, None]).astype(F32)
    inv_freq = ROPE_BASE ** (-jnp.linspace(0.0, 1.0, DK_RET // 2, dtype=F32))
    ang = jnp.arange(T, dtype=F32)[:, None] * inv_freq[None, :]
    cos, sin = jnp.cos(ang), jnp.sin(ang)
    return dict(
        L=L, M=m,
        a=jnp.broadcast_to(a[:, :, None], (H_RET, L, DK_RET)),
        b=jnp.broadcast_to(b[:, :, None], (H_RET, L, DK_RET)),
        gl=jnp.broadcast_to(gl[:, None, None], (H_RET, 1, DK_RET)),
        cos=jnp.concatenate([cos, cos], axis=-1), sin=jnp.concatenate([-sin, sin], axis=-1))


def _rot(x, cs, sn):
    return x * cs + pltpu.roll(x, DK_RET // 2, 1) * sn


def _unrot(dy, cs, sn):
    return dy * cs + pltpu.roll(dy * sn, DK_RET // 2, 1)


def _ret_fwd(big, tb, name, comm=None):
    T = big.shape[0]
    L = tb["L"]
    nsc = T // L
    scale = DK_RET ** -0.5

    def body(x_ref, cos_ref, sin_ref, m_ref, a_ref, b_ref, gl_ref, y_ref, o_ref, st_ref, s_s):
        @pl.when(pl.program_id(1) == 0)
        def _():
            s_s[...] = jnp.zeros_like(s_s)

        cs, sn = cos_ref[...], sin_ref[...]
        qt = _rot(x_ref[:, RQ].astype(F32), cs, sn) * scale
        kt = _rot(x_ref[:, RK].astype(F32), cs, sn)
        qb, kb, vb = qt.astype(BF), kt.astype(BF), x_ref[:, RV]
        s_prev = s_s[...]
        st_ref[...] = s_prev
        p = lax.dot_general(qb, kb, NT_DIMS, preferred_element_type=F32) * m_ref[...]
        o = (jnp.dot(p.astype(BF), vb, preferred_element_type=F32)
             + jnp.dot((qt * a_ref[...]).astype(BF), s_prev.astype(BF), preferred_element_type=F32))
        s_s[...] = s_prev * gl_ref[...] + lax.dot_general((kt * b_ref[...]).astype(BF), vb, TN_DIMS,
                                                         preferred_element_type=F32)
        o_ref[...] = o
        gv = x_ref[:, RG].astype(F32)
        y_ref[...] = (gv * _sigmoid(gv) * o * _rms_r(o)).astype(BF)

    tab = pl.BlockSpec((L, DK_RET), lambda h, i: (i, 0))
    per_head = pl.BlockSpec((None, L, DK_RET), lambda h, i: (h, 0, 0))
    out = pl.BlockSpec((L, LANE), lambda h, i: (i, h))
    return _call(
        body, name=name, grid=(H_RET, nsc), comm=comm,
        args=(big, tb["cos"], tb["sin"], tb["M"], tb["a"], tb["b"], tb["gl"]),
        in_specs=[_group_spec(big, RET_BASE, RET_GROUP, L, lambda h, i: (i, h)), tab, tab,
                  pl.BlockSpec((None, L, L), lambda h, i: (h, 0, 0)), per_head, per_head,
                  pl.BlockSpec((None, 1, DK_RET), lambda h, i: (h, 0, 0))],
        out_specs=[out, out, pl.BlockSpec((None, None, DK_RET, DK_RET), lambda h, i: (i, h, 0, 0))],
        out_shape=[_sds((T, BRANCH_W), BF), _sds((T, BRANCH_W), F32), _sds((nsc, H_RET, DK_RET, DK_RET), F32)],
        scratch_shapes=[pltpu.VMEM((DK_RET, DK_RET), F32)],
        sem=(ARB, ARB))


def _ret_bwd(big, o, st, dy, tb, dbig, name):
    T = big.shape[0]
    L = tb["L"]
    nsc = T // L
    scale = DK_RET ** -0.5

    def body(x_ref, cos_ref, sin_ref, m_ref, a_ref, b_ref, gl_ref, o_ref, st_ref, dy_ref, _, d_ref, ds_s):
        @pl.when(pl.program_id(1) == 0)
        def _():
            ds_s[...] = jnp.zeros_like(ds_s)

        cs, sn = cos_ref[...], sin_ref[...]
        mm, av, bv = m_ref[...], a_ref[...], b_ref[...]
        qt = _rot(x_ref[:, RQ].astype(F32), cs, sn) * scale
        kt = _rot(x_ref[:, RK].astype(F32), cs, sn)
        qb, kb, vb = qt.astype(BF), kt.astype(BF), x_ref[:, RV]
        pb = (lax.dot_general(qb, kb, NT_DIMS, preferred_element_type=F32) * mm).astype(BF)
        ov = o_ref[...]
        r = _rms_r(ov)
        oh = ov * r
        gv = x_ref[:, RG].astype(F32)
        sg = _sigmoid(gv)
        dyv = dy_ref[...].astype(F32)
        d_ref[:, RG] = (dyv * oh * (sg * (1.0 + gv * (1.0 - sg)))).astype(BF)
        doh = dyv * gv * sg
        dob = (r * (doh - oh * jnp.mean(doh * oh, axis=-1, keepdims=True))).astype(BF)
        dsb = ds_s[...].astype(BF)
        spb = st_ref[...].astype(BF)
        dpb = (lax.dot_general(dob, vb, NT_DIMS, preferred_element_type=F32) * mm).astype(BF)
        dqt = (jnp.dot(dpb, kb, preferred_element_type=F32)
               + lax.dot_general(dob, spb, NT_DIMS, preferred_element_type=F32) * av)
        dkt = (lax.dot_general(dpb, qb, TN_DIMS, preferred_element_type=F32)
               + lax.dot_general(vb, dsb, NT_DIMS, preferred_element_type=F32) * bv)
        dv = (lax.dot_general(pb, dob, TN_DIMS, preferred_element_type=F32)
              + jnp.dot((kt * bv).astype(BF), dsb, preferred_element_type=F32))
        ds_s[...] = ds_s[...] * gl_ref[...] + lax.dot_general((qt * av).astype(BF), dob, TN_DIMS,
                                                              preferred_element_type=F32)
        d_ref[:, RQ] = (_unrot(dqt, cs, sn) * scale).astype(BF)
        d_ref[:, RK] = _unrot(dkt, cs, sn).astype(BF)
        d_ref[:, RV] = dv.astype(BF)

    def rev(i):
        return nsc - 1 - i

    group = _group_spec(big, RET_BASE, RET_GROUP, L, lambda h, i: (rev(i), h))
    tab = pl.BlockSpec((L, DK_RET), lambda h, i: (rev(i), 0))
    per_head = pl.BlockSpec((None, L, DK_RET), lambda h, i: (h, 0, 0))
    out = pl.BlockSpec((L, LANE), lambda h, i: (rev(i), h))
    return pl.pallas_call(
        body, name=name, grid=(H_RET, nsc),
        in_specs=[group, tab, tab,
                  pl.BlockSpec((None, L, L), lambda h, i: (h, 0, 0)), per_head, per_head,
                  pl.BlockSpec((None, 1, DK_RET), lambda h, i: (h, 0, 0)),
                  out, pl.BlockSpec((None, None, DK_RET, DK_RET), lambda h, i: (rev(i), h, 0, 0)), out,
                  pl.BlockSpec(memory_space=pl.ANY)],
        out_specs=group,
        out_shape=_sds(dbig.shape, BF),
        scratch_shapes=[pltpu.VMEM((DK_RET, DK_RET), F32)],
        input_output_aliases={10: 0}, compiler_params=_cp((PAR, ARB)),
    )(big, tb["cos"], tb["sin"], tb["M"], tb["a"], tb["b"], tb["gl"], o, st, dy, dbig)


def _relbias_onehot(n):
    mm = lax.broadcasted_iota(jnp.int32, (RB_PAD, ATT_TOEP), 1)
    rr = lax.broadcasted_iota(jnp.int32, (RB_PAD, ATT_TOEP), 0)
    idx = jnp.clip(n + ATT_TOEP - mm, 0, 2 * REL_CLIP)
    return (rr == idx).astype(F32)


def _split3(x):
    hi = x.astype(BF).astype(F32)
    mid = (x - hi).astype(BF).astype(F32)
    lo = x - hi - mid
    return jnp.concatenate([hi, mid, lo], axis=0).astype(BF)


def _join3(y):
    k = y.shape[0] // 3
    return (y[:k] + y[k:2 * k]) + y[2 * k:]


def _relbias_expand(rel_bias, name, comm=None):
    far = ATT_SPAN - ATT_TOEP
    n_layers = rel_bias.shape[0]
    rbp = jnp.pad(rel_bias, ((0, 0), (0, 0), (0, RB_PAD - N_REL)))

    def body(rb_ref, o_ref):
        for l in range(n_layers):
            rb = rb_ref[l]
            const = jnp.broadcast_to(rb[:, 2 * REL_CLIP:2 * REL_CLIP + 1], (H_ATT, far))
            rb3 = _split3(rb)

            def row(n, c):
                toep = _join3(jnp.dot(rb3, _relbias_onehot(n).astype(BF), preferred_element_type=F32))
                m = lax.broadcasted_iota(jnp.int32, (1, ATT_SPAN), 1)
                d = n // CHUNK + N_PREV - m // CHUNK
                neg = jnp.where((d >= 0) & (d <= N_PREV), 0.0, NEG_INF).astype(F32)
                o_ref[l, n] = jnp.concatenate([const, toep], axis=1) + neg
                return c

            lax.fori_loop(0, ATT_TQ, row, 0)

    (out,), extra = _call(
        body, name=name, args=(rbp,), comm=comm,
        in_specs=[pl.BlockSpec(memory_space=pltpu.VMEM)],
        out_specs=[pl.BlockSpec(memory_space=pltpu.VMEM)],
        out_shape=[_sds((n_layers, ATT_TQ, H_ATT, ATT_SPAN), F32)])
    return jnp.transpose(out, (0, 2, 1, 3)), extra


def _relbias_grad(dbt, name):
    far = ATT_SPAN - ATT_TOEP

    def body(d_ref, o_ref):
        def row(n, carry):
            acc, cs = carry
            dn = d_ref[n]
            acc = acc + _join3(lax.dot_general(_split3(dn[:, far:]), _relbias_onehot(n).astype(BF), NT_DIMS,
                                               preferred_element_type=F32))
            cs = cs + jnp.sum(dn[:, :far], axis=1, keepdims=True)
            return acc, cs

        acc, cs = lax.fori_loop(0, ATT_TQ, row, (jnp.zeros((H_ATT, RB_PAD), F32), jnp.zeros((H_ATT, 1), F32)))
        rr = lax.broadcasted_iota(jnp.int32, (H_ATT, RB_PAD), 1)
        o_ref[...] = acc + jnp.where(rr == 2 * REL_CLIP, cs, 0.0)

    return pl.pallas_call(
        body, name=name,
        in_specs=[pl.BlockSpec(memory_space=pltpu.VMEM)],
        out_specs=pl.BlockSpec(memory_space=pltpu.VMEM),
        out_shape=_sds((H_ATT, RB_PAD), F32),
    )(dbt)


def _att_pad_fill(dst_s, src_ref, cols, T):
    dst_s[pl.ds(0, ATT_PAD), :] = jnp.zeros((ATT_PAD, LANE), dst_s.dtype)
    R = min(512, T)

    def cp(t, c):
        dst_s[pl.ds(pl.multiple_of(ATT_PAD + t * R, LANE), R), :] = src_ref[pl.ds(pl.multiple_of(t * R, R), R), cols]
        return c

    lax.fori_loop(0, T // R, cp, 0)


ATT_WIN = ATT_SUB * ATT_TQ + ATT_PAD


def _att_probs(s_full, sub, bias, t0):
    s = s_full[sub * ATT_TQ:(sub + 1) * ATT_TQ, sub * ATT_TQ:sub * ATT_TQ + ATT_SPAN] * (DH_ATT ** -0.5) + bias
    key_pos = t0 + sub * ATT_TQ - ATT_PAD + lax.broadcasted_iota(jnp.int32, (1, ATT_SPAN), 1)
    s = jnp.where(key_pos >= 0, s, NEG_INF)
    p = jnp.exp(s - jnp.max(s, axis=-1, keepdims=True))
    return p * (1.0 / jnp.sum(p, axis=-1, keepdims=True))


def _att_band(tiles):
    rows = []
    for sub, t in enumerate(tiles):
        parts = []
        if sub:
            parts.append(jnp.zeros((ATT_TQ, sub * ATT_TQ), BF))
        parts.append(t)
        if sub < ATT_SUB - 1:
            parts.append(jnp.zeros((ATT_TQ, (ATT_SUB - 1 - sub) * ATT_TQ), BF))
        rows.append(jnp.concatenate(parts, axis=1))
    return jnp.concatenate(rows, axis=0)


def _att_head_masks(x):
    first = lax.broadcasted_iota(jnp.int32, (1, LANE), 1) < DH_ATT
    zero = jnp.zeros_like(x)
    return first, (jnp.where(first, x, zero), jnp.where(first, zero, x))


def _att_fwd(big, bias, name, comm=None):
    T = big.shape[0]
    rows = ATT_SUB * ATT_TQ
    nt = T // rows

    def body(x_ref, b_ref, y_ref, kp_s, vp_s):
        i = pl.program_id(1)

        @pl.when(i == 0)
        def _():
            _att_pad_fill(kp_s, x_ref, AK, T)
            _att_pad_fill(vp_s, x_ref, AV, T)

        t0 = pl.multiple_of(i * rows, rows)
        kw = kp_s[pl.ds(t0, ATT_WIN), :]
        vw = vp_s[pl.ds(t0, ATT_WIN), :]
        first, qm = _att_head_masks(x_ref[pl.ds(t0, rows), AQ])
        outs = []
        for hh in range(2):
            s_full = lax.dot_general(qm[hh], kw, NT_DIMS, preferred_element_type=F32)
            band = _att_band([_att_probs(s_full, sub, b_ref[hh], t0).astype(BF) for sub in range(ATT_SUB)])
            outs.append(jnp.dot(band, vw, preferred_element_type=F32))
        y_ref[...] = jnp.where(first, outs[0], outs[1]).astype(BF)

    return _call(
        body, name=name, grid=(H_ATT // 2, nt), args=(big, bias), comm=comm,
        in_specs=[_group_spec(big, ATT_BASE, ATT_GROUP, T, lambda p, i: (0, p)),
                  pl.BlockSpec((2, ATT_TQ, ATT_SPAN), lambda p, i: (p, 0, 0))],
        out_specs=[pl.BlockSpec((rows, LANE), lambda p, i: (i, p))],
        out_shape=[_sds((T, BRANCH_W), BF)],
        scratch_shapes=[pltpu.VMEM((T + ATT_PAD, LANE), BF), pltpu.VMEM((T + ATT_PAD, LANE), BF)],
        sem=(ARB, ARB))


def _att_bwd(big, bias, dy, dbig, name, comm=None):
    T = big.shape[0]
    rows = ATT_SUB * ATT_TQ
    nt = T // rows
    scale = DH_ATT ** -0.5

    def body(x_ref, b_ref, dy_ref, _, d_ref, db_ref, kp_s, vp_s, dk_s, dv_s):
        i = pl.program_id(1)

        @pl.when(i == 0)
        def _():
            _att_pad_fill(kp_s, x_ref, AK, T)
            _att_pad_fill(vp_s, x_ref, AV, T)
            dk_s[...] = jnp.zeros_like(dk_s)
            dv_s[...] = jnp.zeros_like(dv_s)
            db_ref[...] = jnp.zeros_like(db_ref)

        t0 = pl.multiple_of(i * rows, rows)
        win = pl.ds(t0, ATT_WIN)
        kw = kp_s[win, :]
        vw = vp_s[win, :]
        first, qm = _att_head_masks(x_ref[pl.ds(t0, rows), AQ])
        _, dom = _att_head_masks(dy_ref[...])
        dqs, dkt, dvt = [], None, None
        for hh in range(2):
            s_full = lax.dot_general(qm[hh], kw, NT_DIMS, preferred_element_type=F32)
            dp_full = lax.dot_general(dom[hh], vw, NT_DIMS, preferred_element_type=F32)
            ps, dss, db = [], [], None
            for sub in range(ATT_SUB):
                pn = _att_probs(s_full, sub, b_ref[hh], t0)
                dp = dp_full[sub * ATT_TQ:(sub + 1) * ATT_TQ, sub * ATT_TQ:sub * ATT_TQ + ATT_SPAN]
                ds = pn * (dp - jnp.sum(dp * pn, axis=-1, keepdims=True))
                db = ds if db is None else db + ds
                ps.append(pn.astype(BF))
                dss.append(ds.astype(BF))
            db_ref[hh] += db
            ds_band, p_band = _att_band(dss), _att_band(ps)
            dqs.append(jnp.dot(ds_band, kw, preferred_element_type=F32))
            qt = jnp.transpose(qm[hh].astype(F32)).astype(BF)
            dot_ = jnp.transpose(dom[hh].astype(F32)).astype(BF)
            dk_h = jnp.dot(qt, ds_band, preferred_element_type=F32)
            dv_h = jnp.dot(dot_, p_band, preferred_element_type=F32)
            dkt = dk_h if dkt is None else dkt + dk_h
            dvt = dv_h if dvt is None else dvt + dv_h
        d_ref[pl.ds(t0, rows), AQ] = (jnp.where(first, dqs[0], dqs[1]) * scale).astype(BF)
        dk_s[win, :] += jnp.transpose(dkt) * scale
        dv_s[win, :] += jnp.transpose(dvt)

        @pl.when(i == nt - 1)
        def _():
            R = min(512, T)

            def cp(t, c):
                src = pl.ds(pl.multiple_of(ATT_PAD + t * R, LANE), R)
                dst = pl.ds(pl.multiple_of(t * R, R), R)
                d_ref[dst, AK] = dk_s[src, :].astype(BF)
                d_ref[dst, AV] = dv_s[src, :].astype(BF)
                return c

            lax.fori_loop(0, T // R, cp, 0)

    group = _group_spec(big, ATT_BASE, ATT_GROUP, T, lambda p, i: (0, p))
    tile = pl.BlockSpec((rows, LANE), lambda p, i: (i, p))
    bspec = pl.BlockSpec((2, ATT_TQ, ATT_SPAN), lambda p, i: (p, 0, 0))
    return _call(
        body, name=name, grid=(H_ATT // 2, nt), args=(big, bias, dy, dbig), comm=comm, aliases={3: 0}, vmem_mb=56,
        in_specs=[group, bspec, tile, pl.BlockSpec(memory_space=pl.ANY)],
        out_specs=[group, bspec],
        out_shape=[_sds(dbig.shape, BF), _sds((H_ATT, ATT_TQ, ATT_SPAN), F32)],
        scratch_shapes=[pltpu.VMEM((T + ATT_PAD, LANE), BF), pltpu.VMEM((T + ATT_PAD, LANE), BF),
                        pltpu.VMEM((T + ATT_PAD, LANE), F32), pltpu.VMEM((T + ATT_PAD, LANE), F32)],
        sem=(ARB, ARB))


def _merge_fwd(x1, big, ys, wb, wo, name):
    T, D = x1.shape
    tm = min(TM, T)

    def body(x_ref, gp_ref, yc_ref, yr_ref, ya_ref, wb_ref, wo_ref, x2_ref, p_ref, mg_ref):
        merged = jnp.zeros((tm, D), F32)
        for i, y_ref in enumerate((yc_ref, yr_ref, ya_ref)):
            cols = slice(i * D, (i + 1) * D)
            pb = jnp.dot(y_ref[...], wb_ref[i], preferred_element_type=F32).astype(BF)
            p_ref[:, cols] = pb
            merged = merged + _sigmoid(gp_ref[:, cols].astype(F32)) * pb.astype(F32)
        mb = merged.astype(BF)
        mg_ref[...] = mb
        x2_ref[...] = x_ref[...] + jnp.dot(mb, wo_ref[...], preferred_element_type=F32)

    tok = pl.BlockSpec((tm, D), lambda i: (i, 0))
    wide = pl.BlockSpec((tm, 3 * D), lambda i: (i, 0))
    yspec = pl.BlockSpec((tm, BRANCH_W), lambda i: (i, 0))
    return pl.pallas_call(
        body, name=name, grid=(T // tm,),
        in_specs=[tok, wide, yspec, yspec, yspec,
                  pl.BlockSpec((3, BRANCH_W, D), lambda i: (0, 0, 0)),
                  pl.BlockSpec((D, D), lambda i: (0, 0))],
        out_specs=[tok, wide, tok],
        out_shape=[_sds((T, D), F32), _sds((T, 3 * D), BF), _sds((T, D), BF)],
        compiler_params=_cp((PAR,)),
    )(x1, big, *ys, wb, wo)


def _merge_bwd(dx2, big, p, wb, wo, name):
    T, D = dx2.shape
    tm = min(TM, T)

    def body(dx_ref, gp_ref, p_ref, wb_ref, wo_ref, dp_ref, dgp_ref, dyc_ref, dyr_ref, dya_ref, dxb_ref):
        dxb = dx_ref[...].astype(BF)
        dxb_ref[...] = dxb
        dm = lax.dot_general(dxb, wo_ref[...], NT_DIMS, preferred_element_type=F32)
        for i, dy_ref in enumerate((dyc_ref, dyr_ref, dya_ref)):
            cols = slice(i * D, (i + 1) * D)
            gt = _sigmoid(gp_ref[:, cols].astype(F32))
            dpb = (dm * gt).astype(BF)
            dp_ref[:, cols] = dpb
            dgp_ref[:, cols] = (dm * p_ref[:, cols].astype(F32) * gt * (1.0 - gt)).astype(BF)
            dy_ref[...] = lax.dot_general(dpb, wb_ref[i], NT_DIMS, preferred_element_type=F32).astype(BF)

    tok = pl.BlockSpec((tm, D), lambda i: (i, 0))
    wide = pl.BlockSpec((tm, 3 * D), lambda i: (i, 0))
    yspec = pl.BlockSpec((tm, BRANCH_W), lambda i: (i, 0))
    return pl.pallas_call(
        body, name=name, grid=(T // tm,),
        in_specs=[tok, wide, wide,
                  pl.BlockSpec((3, BRANCH_W, D), lambda i: (0, 0, 0)),
                  pl.BlockSpec((D, D), lambda i: (0, 0))],
        out_specs=[wide, wide, yspec, yspec, yspec, tok],
        out_shape=[_sds((T, 3 * D), BF), _sds(big.shape, BF)] + [_sds((T, BRANCH_W), BF)] * 3 + [_sds((T, D), BF)],
        compiler_params=_cp((PAR,)),
    )(dx2, big, p, wb, wo)


def _loss_head(x, tgt, fw, name):
    T, D = x.shape
    tm = min(TM, T)

    def body(x_ref, t_ref, w_ref, loss_ref, dx_ref, dw_ref):
        @pl.when(pl.program_id(0) == 0)
        def _():
            loss_ref[...] = jnp.zeros_like(loss_ref)
            dw_ref[...] = jnp.zeros_like(dw_ref)

        xv = x_ref[...]
        wv = w_ref[...]
        e = xv * _rms_r(xv) * wv - t_ref[...]
        loss_ref[...] += 0.5 * jnp.sum(jnp.mean(e * e, axis=-1, keepdims=True))
        dx, dn = _rms_bwd(e * (1.0 / D), xv, wv)
        dx_ref[...] = dx
        dw_ref[...] += dn

    tok = pl.BlockSpec((tm, D), lambda i: (i, 0))
    return pl.pallas_call(
        body, name=name, grid=(T // tm,),
        in_specs=[tok, tok, pl.BlockSpec((1, D), lambda i: (0, 0))],
        out_specs=[pl.BlockSpec((8, LANE), lambda i: (0, 0)), tok, pl.BlockSpec((1, D), lambda i: (0, 0))],
        out_shape=[_sds((8, LANE), F32), _sds((T, D), F32), _sds((1, D), F32)],
        compiler_params=_cp((ARB,)),
    )(x, tgt, fw)


def _block_rows(rows, cols):
    cap = max(8, (1 << 18) // cols)
    best = None
    for r in range(8, rows + 1, 8):
        if rows % r == 0 and r <= cap:
            best = r
    return best if best is not None else rows


def _sum8(land, l, n_layers, name, prev=None, comm=None):
    _, rows, cols = land.shape
    br = _block_rows(rows, cols)

    def body(*refs):
        l_ref, o_ref = refs[0], refs[-1]

        def four(base):
            return ((l_ref[base + 3].astype(F32) + l_ref[base].astype(F32)) + l_ref[base + 1].astype(F32)
                    ) + l_ref[base + 2].astype(F32)

        o_ref[...] = four(0) + four(4)

    in_specs = [pl.BlockSpec((2 * N_SHARD, br, cols), lambda i: (0, i, 0))]
    args = [land]
    aliases = {}
    if prev is not None:
        in_specs.append(pl.BlockSpec(memory_space=pl.ANY))
        args.append(prev)
        aliases = {1: 0}
    main, extra = _call(
        body, name=name, grid=(rows // br,), args=args, in_specs=in_specs,
        out_specs=[pl.BlockSpec((None, br, cols), lambda i: (l, i, 0))],
        out_shape=[_sds((n_layers, rows, cols), F32)], aliases=aliases, sem=(ARB,), comm=comm)
    return main[0], extra


def _adamw_math(w, g, m, v):
    m = ADAM_B1 * m + (1.0 - ADAM_B1) * g
    v = ADAM_B2 * v + (1.0 - ADAM_B2) * (g * g)
    m_hat = m / (1.0 - ADAM_B1 ** ADAM_STEP)
    v_hat = v / (1.0 - ADAM_B2 ** ADAM_STEP)
    delta = -ADAM_LR * (m_hat / (jnp.sqrt(v_hat) + ADAM_EPS) + ADAM_WD * w)
    return delta, m, v


def _adamw(w, g, m, v, name):
    rows, cols = w.shape
    br = _block_rows(rows, cols)

    def body(w_ref, g_ref, m_ref, v_ref, d_ref, nm_ref, nv_ref):
        d, nm, nv = _adamw_math(w_ref[...], g_ref[...], m_ref[...], v_ref[...])
        d_ref[...] = d
        nm_ref[...] = nm
        nv_ref[...] = nv

    blk = pl.BlockSpec((br, cols), lambda i: (i, 0))
    return pl.pallas_call(
        body, name=name, grid=(rows // br,),
        in_specs=[blk] * 4, out_specs=[blk] * 3,
        out_shape=[_sds((rows, cols), F32)] * 3,
        compiler_params=_cp((PAR,)),
    )(w, g, m, v)


def _allreduce_small(v, name):
    rows = v.shape[0]
    flips = [(fx, fy, fc) for fx in (0, 1) for fy in (0, 1) for fc in (0, 1) if fx or fy or fc]

    def body(v_ref, o_ref, all_s, ssem, rsem):
        x, y, c = _place()

        def peer(f):
            return (x + f[0] - 2 * x * f[0], y + f[1] - 2 * y * f[1], c + f[2] - 2 * c * f[2])

        def slot(p):
            return all_s.at[4 * p[0] + 2 * p[1] + p[2]]

        def copy(k, f, owner):
            return pltpu.make_async_remote_copy(
                src_ref=v_ref, dst_ref=slot(owner), send_sem=ssem.at[k], recv_sem=rsem.at[k],
                device_id=peer(f), device_id_type=MESH)

        sends = [copy(k, f, (x, y, c)) for k, f in enumerate(flips)]
        for cp in sends:
            cp.start()
        all_s[4 * x + 2 * y + c] = v_ref[...]
        for k, f in enumerate(flips):
            copy(k, f, peer(f)).wait_recv()
        for cp in sends:
            cp.wait_send()
        acc = all_s[0]
        for d in range(1, 8):
            acc = acc + all_s[d]
        o_ref[...] = acc

    return pl.pallas_call(
        body, name=name,
        in_specs=[pl.BlockSpec(memory_space=pltpu.VMEM)],
        out_specs=pl.BlockSpec(memory_space=pltpu.VMEM),
        out_shape=_sds((rows, LANE), F32),
        scratch_shapes=[pltpu.VMEM((8, rows, LANE), F32), pltpu.SemaphoreType.DMA((7,)), pltpu.SemaphoreType.DMA((7,))],
    )(v)


BIG_NAMES = ("ffn1_w_gate", "ffn1_w_up", "ffn1_w_down", "w_in", "w_branch", "w_merge_gate", "w_out",
             "ffn2_w_gate", "ffn2_w_up", "ffn2_w_down")


FFN1 = ("ffn1_w_gate", "ffn1_w_up", "ffn1_w_down")
FFN2 = ("ffn2_w_gate", "ffn2_w_up", "ffn2_w_down")
MIX_IN = ("w_in", "w_merge_gate")
MIX_OUT = ("w_branch", "w_out")


def _keys(names, l):
    return [(n, l) for n in names]


def _local_step(x, tgt, small, convw_full, biases, wx, n_layers):
    T, D = x.shape
    L = n_layers
    ns = N_SHARD
    dq = D // ns
    W = wx.w

    def hosted(call, keys, scatter=False):
        comm = wx.pieces(keys, scatter)
        main, extra = call(comm)
        if comm is not None:
            wx.arrived(keys, extra, scatter)
        return main

    def mixer_views(l):
        return _build_wbig(W[("w_merge_gate", l)], W[("w_in", l)], f"wbig_{l}")

    def out_views(l):
        wb4 = W[("w_branch", l)]
        wb = _copy_blocks(wb4, pl.BlockSpec((None, None, BRANCH_W, dq), lambda s_, i: (s_, i, 0, 0)),
                          _sds((3, BRANCH_W, D), wb4.dtype),
                          pl.BlockSpec((None, BRANCH_W, dq), lambda s_, i: (i, 0, s_)), (ns, 3), f"w_branch_whole_{l}")
        wo = W[("w_out", l)].reshape(D, D)
        return wb, wo

    tb = _ret_tables(T)

    saved = []
    h = x
    for l in range(L):
        s = {"x0": h}
        nxt = l + 1
        x1, s["g1"], s["u1"] = hosted(
            lambda c: _ffn_fwd(h, small["ffn1_norm"][l][None], W[("ffn1_w_gate", l)], W[("ffn1_w_up", l)],
                               W[("ffn1_w_down", l)], f"ffn1_fwd_{l}", comm=c), _keys(MIX_IN, l))
        s["x1"] = x1
        s["wbig"] = mixer_views(l)
        big, s["h"] = _inproj_fwd(x1, small["mix_norm"][l][None], s["wbig"], f"inproj_fwd_{l}")
        s["big"] = big
        s["bias"] = biases[l]
        s["yc"] = _conv_fwd(big, convw_full[l], f"conv_fwd_{l}")
        s["yr"], s["o"], s["st"] = hosted(lambda c: _ret_fwd(big, tb, f"ret_fwd_{l}", comm=c), _keys(MIX_OUT, l))
        (s["ya"],) = hosted(lambda c: _att_fwd(big, s["bias"], f"att_fwd_{l}", comm=c), _keys(FFN2, l))
        s["wb"], s["wo"] = out_views(l)
        x2, s["p"], s["mg"] = _merge_fwd(x1, big, (s["yc"], s["yr"], s["ya"]), s["wb"], s["wo"], f"merge_fwd_{l}")
        s["x2"] = x2
        h, s["g2"], s["u2"] = hosted(
            lambda c: _ffn_fwd(x2, small["ffn2_norm"][l][None], W[("ffn2_w_gate", l)], W[("ffn2_w_up", l)],
                               W[("ffn2_w_down", l)], f"ffn2_fwd_{l}", comm=c), _keys(FFN1, nxt) if nxt < L else [])
        saved.append(s)

    loss_p, dx, d_final = _loss_head(h, tgt, small["final_norm"][None], "loss_head")

    gs = {"final_norm": d_final[0]}
    for k in ("ffn1_norm", "mix_norm", "ffn2_norm", "rel_bias", "conv_w"):
        gs[k] = [None] * L
    tk = min(2048, T)
    nk = T // tk

    def ffn_back(pre, l, dxo, x_in, g, u, first_keys, second_keys, between=None):
        nw = small[pre + "_norm"][l][None]
        dgv, duv, av, hb, dacc = hosted(
            lambda c: _ffn_bwd_hidden(dxo, x_in, nw, g, u, W[(pre + "_w_down", l)], f"{pre}_bwd_hidden_{l}", comm=c),
            first_keys, scatter=True)
        parts = (hb, dgv, duv, av, dacc)
        if between is not None:
            second_keys = between(parts)
        dxn, dn = hosted(
            lambda c: _ffn_bwd_resid(dgv, duv, W[(pre + "_w_gate", l)], W[(pre + "_w_up", l)], x_in, nw, dxo,
                                     f"{pre}_bwd_resid_{l}", comm=c),
            second_keys, scatter=True)
        gs[pre + "_norm"][l] = dn[0]
        return dxn, parts

    def ffn_grads(pre, l, hb, dgv, duv, av, dacc, chain=False):
        fs = dgv.shape[-1]
        hspec = pl.BlockSpec((tk, D), lambda p, q, k: (k, 0))
        sspec = pl.BlockSpec((None, tk, fs), lambda p, q, k: (p, k, 0))
        down_spec = pl.BlockSpec((None, fs, D), lambda p, q, k: (p, 0, 0))
        jobs = [(pre + "_w_gate", dgv, hb, sspec, hspec, (ns, fs, D), down_spec),
                (pre + "_w_up", duv, hb, sspec, hspec, (ns, fs, D), down_spec),
                (pre + "_w_down", av, dacc, sspec, hspec, (ns, fs, D), down_spec)]
        for idx, (nm, a, b, a_spec, b_spec, shape, o_spec) in enumerate(jobs):
            def product(c):
                r = _tn(a, b, a_spec, b_spec, _sds(shape, BF), o_spec, (ns, 1, nk), f"d{nm}_{l}", comm=c)
                return (r, []) if c is None else r
            keys = [(jobs[0][0], l)] if chain and idx == 2 else []
            wx.g[(nm, l)] = hosted(product, keys, scatter=True)
        return [(jobs[1][0], l), (jobs[2][0], l)] if chain else []

    for l in reversed(range(L)):
        s = saved[l]
        above = _keys(FFN1, l + 1) if l + 1 < L else []
        dx, parts = ffn_back("ffn2", l, dx, s["x2"], s["g2"], s["u2"], above[:1], above[1:])
        ffn_grads("ffn2", l, *parts)
        dp, dbig, dyc, dyr, dya, dxb = _merge_bwd(dx, s["big"], s["p"], s["wb"], s["wo"], f"merge_bwd_{l}")
        wx.g[("w_out", l)] = _tn(
            s["mg"], dxb, pl.BlockSpec((tk, dq), lambda p, q, k: (k, p)), pl.BlockSpec((tk, D), lambda p, q, k: (k, 0)),
            _sds((ns, dq, D), BF), pl.BlockSpec((None, dq, D), lambda p, q, k: (p, 0, 0)), (ns, 1, nk), f"dw_out_{l}")
        gb = None
        for i, yv in enumerate((s["yc"], s["yr"], s["ya"])):
            gb = _tn(yv, dp,
                     pl.BlockSpec((tk, BRANCH_W), lambda p, q, k: (k, 0)),
                     pl.BlockSpec((tk, dq), lambda p, q, k, i=i: (k, i * ns + p)),
                     _sds((ns, 3, BRANCH_W, dq), BF),
                     pl.BlockSpec((None, None, BRANCH_W, dq), lambda p, q, k, i=i: (p, i, 0, 0)),
                     (ns, 1, nk), f"dw_branch{i}_{l}", prev=gb)
        wx.g[("w_branch", l)] = gb
        dbig, dcw = _conv_bwd(s["big"], dyc, convw_full[l], dbig, f"conv_bwd_{l}")
        gs["conv_w"][l] = dcw
        dbig = _ret_bwd(s["big"], s["o"], s["st"], dyr, tb, dbig, f"ret_bwd_{l}")
        dbig, dbias = hosted(lambda c: _att_bwd(s["big"], s["bias"], dya, dbig, f"att_bwd_{l}", comm=c),
                             _keys(FFN2, l), scatter=True)
        gs["rel_bias"][l] = _relbias_grad(jnp.transpose(dbias, (1, 0, 2)), f"relbias_grad_{l}")[:, :N_REL]
        n_in = N_SEG * BRANCH_W
        bn = 1024 if (3 * D) % 1024 == 0 else BRANCH_W
        dwp = _tn(s["h"], dbig, pl.BlockSpec((tk, D), lambda p, q, k: (k, 0)),
                  pl.BlockSpec((tk, bn), lambda p, q, k: (k, 3 * D // bn + q)),
                  _sds((D, n_in), BF), pl.BlockSpec((D, bn), lambda p, q, k: (0, q)), (1, n_in // bn, nk), f"dw_in_{l}")
        wx.g[("w_in", l)] = _ungroup_dw_in(dwp, ns, f"dw_in_shards_{l}")
        wx.g[("w_merge_gate", l)] = _tn_gates(s["h"], dbig, ns, tk, f"dw_merge_gate_{l}")
        dx, dn = hosted(
            lambda c: _inproj_bwd(dbig, s["wbig"], s["x1"], small["mix_norm"][l][None], dx, f"inproj_bwd_{l}", comm=c),
            [("w_in", l)], scatter=True)
        gs["mix_norm"][l] = dn[0]
        rest = [("w_merge_gate", l), ("w_branch", l), ("w_out", l)]
        if l == 0:
            dx, _ = ffn_back("ffn1", l, dx, s["x0"], s["g1"], s["u1"], rest, [],
                             between=lambda parts: ffn_grads("ffn1", 0, *parts, chain=True))
        else:
            dx, parts = ffn_back("ffn1", l, dx, s["x0"], s["g1"], s["u1"], rest, [])
            ffn_grads("ffn1", l, *parts)

    for k in ("ffn1_norm", "mix_norm", "ffn2_norm", "rel_bias", "conv_w"):
        gs[k] = jnp.stack(gs[k])
    return loss_p, dx, gs


class _Exchange:
    def __init__(self, shards):
        self.shards = shards
        self.w = {}
        self.g = {}
        self.landed = {}

    def own(self, key):
        return self.shards[key[0]][key[1]].astype(BF)

    def pieces(self, keys, scatter):
        if not keys:
            return None
        if scatter:
            return _Scatter([self.g[k] for k in keys])
        return _HalfGather([_halves(self.own(k)) for k in keys])

    def arrived(self, keys, outs, scatter):
        for k, o in zip(keys, outs):
            if scatter:
                self.landed[k] = o
            else:
                self.w[k] = o.reshape((N_SHARD,) + self.shards[k[0]].shape[1:])


def _halves(a):
    return a.reshape(2, -1, a.shape[-1])


TRANSPOSED_GRADS = ("ffn1_w_gate", "ffn1_w_up", "ffn2_w_gate", "ffn2_w_up")
W_NAMES = ("ffn1_norm", "ffn1_w_gate", "ffn1_w_up", "ffn1_w_down", "mix_norm", "w_in", "conv_w", "rel_bias", "w_branch",
           "w_merge_gate", "w_out", "ffn2_norm", "ffn2_w_gate", "ffn2_w_up", "ffn2_w_down", "final_norm")


def _as2d(a):
    return a.reshape(1, -1) if a.ndim == 1 else a.reshape(-1, a.shape[-1])


def kernel(x, ffn1_norm, ffn1_w_gate, ffn1_w_up, ffn1_w_down, mix_norm, w_in, conv_w, rel_bias, w_branch, w_merge_gate, w_out, ffn2_norm, ffn2_w_gate, ffn2_w_up, ffn2_w_down, final_norm, loss_target, m_ffn1_norm, m_ffn1_w_gate, m_ffn1_w_up, m_ffn1_w_down, m_mix_norm, m_w_in, m_conv_w, m_rel_bias, m_w_branch, m_w_merge_gate, m_w_out, m_ffn2_norm, m_ffn2_w_gate, m_ffn2_w_up, m_ffn2_w_down, m_final_norm, v_ffn1_norm, v_ffn1_w_gate, v_ffn1_w_up, v_ffn1_w_down, v_mix_norm, v_w_in, v_conv_w, v_rel_bias, v_w_branch, v_w_merge_gate, v_w_out, v_ffn2_norm, v_ffn2_w_gate, v_ffn2_w_up, v_ffn2_w_down, v_final_norm):
    given = dict(locals())
    w = {n: given[n] for n in W_NAMES}
    m = {n: given["m_" + n] for n in W_NAMES}
    v = {n: given["v_" + n] for n in W_NAMES}
    my_chip = 2 * lax.axis_index("x") + lax.axis_index("y")
    L = w_in.shape[0]

    wx = _Exchange({n: jnp.swapaxes(w[n], 1, 2) if n in TRANSPOSED_GRADS else w[n] for n in BIG_NAMES})
    first = _keys(FFN1, 0)
    biases, got = _relbias_expand(
        rel_bias, "relbias_expand", comm=_HalfGather([_halves(wx.own(k)) for k in first] + [_halves(conv_w)]))
    wx.arrived(first, got[:-1], False)
    convw_full = jnp.transpose(got[-1].reshape((N_SHARD,) + conv_w.shape), (1, 2, 0, 3)).reshape(
        conv_w.shape[0], conv_w.shape[1], -1)

    small = {n: w[n] for n in ("ffn1_norm", "mix_norm", "ffn2_norm", "final_norm")}
    loss_p, grad_x, gs = _local_step(x[0], loss_target[0], small, convw_full, biases, wx, L)

    sums = []
    for n in BIG_NAMES:
        acc = None
        for l in range(L):
            a = wx.landed[(n, l)]
            acc, _ = _sum8(a.reshape(a.shape[0], -1, a.shape[-1]), l, L, f"sum8_{n}_{l}", prev=acc)
        sums.append(acc.reshape(-1, acc.shape[-1]))

    parts = [gs["ffn1_norm"].reshape(-1), gs["mix_norm"].reshape(-1), gs["ffn2_norm"].reshape(-1),
             gs["final_norm"].reshape(-1), gs["rel_bias"].reshape(-1), gs["conv_w"].reshape(-1), loss_p[0]]
    sizes = [p.shape[0] for p in parts]
    flat = jnp.concatenate(parts)
    rows = -(-flat.shape[0] // (8 * LANE)) * 8
    flat = jnp.pad(flat, (0, rows * LANE - flat.shape[0])).reshape(rows, LANE)
    red = _allreduce_small(flat, "allreduce_small").reshape(-1)
    offs = [0]
    for sz in sizes:
        offs.append(offs[-1] + sz)
    sm = {}
    for i, n in enumerate(("ffn1_norm", "mix_norm", "ffn2_norm", "final_norm", "rel_bias", "conv_w")):
        sm[n] = red[offs[i]:offs[i + 1]]
    loss = red[offs[6]]
    sm["conv_w"] = lax.dynamic_slice_in_dim(sm["conv_w"].reshape(conv_w.shape[0], conv_w.shape[1], -1),
                                            my_chip * conv_w.shape[2], conv_w.shape[2], axis=2)

    grads, deltas, new_m, new_v = {}, {}, {}, {}
    big_sum = dict(zip(BIG_NAMES, sums))
    for n in W_NAMES:
        flip = n in TRANSPOSED_GRADS

        def view(a):
            return jnp.swapaxes(a, 1, 2) if flip else a

        shape = view(w[n]).shape
        g = big_sum[n] if n in big_sum else _as2d(sm[n].reshape(shape))
        out = _adamw(_as2d(view(w[n])), g, _as2d(view(m[n])), _as2d(view(v[n])), f"adamw_{n}")
        grads[n], deltas[n], new_m[n], new_v[n] = (view(o.reshape(shape)) for o in [g] + list(out))

    return (loss, grad_x[None], *[grads[n] for n in W_NAMES], *[deltas[n] for n in W_NAMES],
            *[new_m[n] for n in W_NAMES], *[new_v[n] for n in W_NAMES])
```

```python
import functools
import math

import jax
import jax.numpy as jnp
from jax import lax
from jax.experimental import pallas as pl
from jax.experimental.pallas import tpu as pltpu

F32 = jnp.float32
BF = jnp.bfloat16
MESH = pl.DeviceIdType.MESH
ARB = "arbitrary"
PAR = "parallel"

EPS = 1e-6
NEG_INF = -1e30
ROPE_BASE = 10000.0
CHUNK = 64
BRANCH_W = 512
H_RET = 4
DK_RET = 128
H_ATT = 8
DH_ATT = 64
N_PREV = 8
REL_CLIP = 128
N_REL = 2 * REL_CLIP + 1
N_SHARD = 4
LANE = 128
RET_L = 512
ATT_TQ = 128
ATT_SUB = 4
ATT_PAD = N_PREV * CHUNK
ATT_SPAN = ATT_TQ + ATT_PAD
ATT_TOEP = 2 * REL_CLIP
RB_PAD = 264
TM = 512
TM_FFN = 1024

ADAM_LR = 0.001
ADAM_B1 = 0.9
ADAM_B2 = 0.999
ADAM_EPS = 1e-08
ADAM_WD = 0.01
ADAM_STEP = 10

NT_DIMS = (((1,), (1,)), ((), ()))
TN_DIMS = (((0,), (0,)), ((), ()))


def _cp(sem, vmem_mb=48):
    return pltpu.CompilerParams(dimension_semantics=sem, vmem_limit_bytes=vmem_mb << 20)


def _sds(shape, dtype):
    return jax.ShapeDtypeStruct(tuple(shape), dtype)


def _rms_r(x):
    return lax.rsqrt(jnp.mean(x * x, axis=-1, keepdims=True) + EPS)


def _sigmoid(x):
    return 0.5 * jnp.tanh(0.5 * x) + 0.5


def _rms_bwd(dh, xv, nw):
    r = _rms_r(xv)
    xh = xv * r
    dxh = dh * nw
    dx = r * (dxh - xh * jnp.mean(dxh * xh, axis=-1, keepdims=True))
    return dx, jnp.sum(dh * xh, axis=0, keepdims=True)


def _place():
    return lax.axis_index("x"), lax.axis_index("y"), lax.axis_index("c")


def _other_chips(x, y):
    return [(1 - x, y), (x, 1 - y), (1 - x, 1 - y)]


class _Scatter:
    def __init__(self, srcs):
        self.srcs = list(srcs)
        n = len(self.srcs)
        self.out_shape = [_sds((2 * N_SHARD,) + s.shape[1:], s.dtype) for s in self.srcs]
        self.scratch = [pltpu.SemaphoreType.DMA((n,)), pltpu.SemaphoreType.DMA((3, n)), pltpu.SemaphoreType.DMA((3, n)),
                        pltpu.SemaphoreType.DMA((4, n)), pltpu.SemaphoreType.DMA((4, n))]

    def _plan(self, src, dst, sems, want):
        lsem, s1, r1, s2, r2 = sems
        x, y, c = _place()
        mine = 2 * x + y
        n = len(src)
        chips = list(enumerate(_other_chips(x, y)))

        def copy(s_ref, d_ref, ssem, rsem, to):
            return pltpu.make_async_remote_copy(src_ref=s_ref, dst_ref=d_ref, send_sem=ssem, recv_sem=rsem,
                                                device_id=to, device_id_type=MESH)

        local = [pltpu.make_async_copy(src[k].at[mine], dst[k].at[3], lsem.at[k]) for k in range(n)
                 ] if "local" in want else []
        sends = [copy(src[k].at[2 * ch[0] + ch[1]], dst[k].at[j], s1.at[j, k], r1.at[j, k], (ch[0], ch[1], c))
                 for j, ch in chips for k in range(n)] if "sends" in want else []
        passes = [copy(dst[k].at[j], dst[k].at[4 + j], s2.at[j, k], r2.at[j, k], (x, y, 1 - c))
                  for j, ch in chips for k in range(n)] if "passes" in want else []
        own_pass = [copy(src[k].at[mine], dst[k].at[7], s2.at[3, k], r2.at[3, k], (x, y, 1 - c))
                    for k in range(n)] if "own_pass" in want else []
        return local, sends, passes, own_pass

    def start(self, src, dst, sems):
        local, sends, _, own_pass = self._plan(src, dst, sems, ("local", "sends", "own_pass"))
        for cp in local + sends + own_pass:
            cp.start()

    def relay(self, src, dst, sems):
        _, sends, passes, _ = self._plan(src, dst, sems, ("sends", "passes"))
        for land, fwd in zip(sends, passes):
            land.wait_recv()
            fwd.start()

    def finish(self, src, dst, sems):
        local, sends, passes, own_pass = self._plan(src, dst, sems, ("local", "sends", "passes", "own_pass"))
        for cp in passes + own_pass:
            cp.wait_recv()
        for cp in sends + passes + own_pass:
            cp.wait_send()
        for cp in local:
            cp.wait()

    def wait(self, src, dst, sems):
        self.relay(src, dst, sems)
        self.finish(src, dst, sems)


class _HalfGather:
    def __init__(self, srcs):
        self.srcs = list(srcs)
        n = len(self.srcs)
        self.out_shape = [_sds((N_SHARD,) + s.shape, s.dtype) for s in self.srcs]
        self.scratch = [pltpu.SemaphoreType.DMA((n,))] + [pltpu.SemaphoreType.DMA((3, n)) for _ in range(4)]

    def _plan(self, src, dst, sems, want):
        lsem, s1, r1, s2, r2 = sems
        x, y, c = _place()
        mine = 2 * x + y
        n = len(src)
        chips = [(j, ch, 2 * ch[0] + ch[1]) for j, ch in enumerate(_other_chips(x, y))]

        def copy(s_ref, d_ref, ssem, rsem, to):
            return pltpu.make_async_remote_copy(src_ref=s_ref, dst_ref=d_ref, send_sem=ssem, recv_sem=rsem,
                                                device_id=to, device_id_type=MESH)

        def over(kind, make):
            return [make(j, ch, slot, k) for j, ch, slot in chips for k in range(n)] if kind in want else []

        local = [pltpu.make_async_copy(src[k], dst[k].at[mine], lsem.at[k]) for k in range(n)] if "local" in want else []
        sends = over("sends", lambda j, ch, slot, k: copy(src[k].at[c], dst[k].at[mine, c], s1.at[j, k], r1.at[j, k],
                                                          (ch[0], ch[1], c)))
        lands = over("lands", lambda j, ch, slot, k: copy(src[k].at[c], dst[k].at[slot, c], s1.at[j, k], r1.at[j, k],
                                                          (ch[0], ch[1], c)))
        passes = over("passes", lambda j, ch, slot, k: copy(dst[k].at[slot, c], dst[k].at[slot, c], s2.at[j, k],
                                                            r2.at[j, k], (x, y, 1 - c)))
        gets = over("gets", lambda j, ch, slot, k: copy(dst[k].at[slot, 1 - c], dst[k].at[slot, 1 - c], s2.at[j, k],
                                                        r2.at[j, k], (x, y, 1 - c)))
        return local, sends, lands, passes, gets

    def start(self, src, dst, sems):
        lsem, s1, r1, s2, r2 = sems
        x, y, c = _place()
        mine = 2 * x + y
        for k in range(len(src)):
            pltpu.make_async_copy(src[k], dst[k].at[mine], lsem.at[k]).start()
        for j, ch in enumerate(_other_chips(x, y)):
            for k in range(len(src)):
                pltpu.make_async_remote_copy(
                    src_ref=src[k].at[c], dst_ref=dst[k].at[mine, c], send_sem=s1.at[j, k], recv_sem=r1.at[j, k],
                    device_id=(ch[0], ch[1], c), device_id_type=MESH).start()

    def relay(self, src, dst, sems):
        _, _, lands, passes, _ = self._plan(src, dst, sems, ("lands", "passes"))
        for land, fwd in zip(lands, passes):
            land.wait_recv()
            fwd.start()

    def finish(self, src, dst, sems):
        local, sends, _, passes, gets = self._plan(src, dst, sems, ("local", "sends", "passes", "gets"))
        for cp in gets:
            cp.wait_recv()
        for cp in sends + passes:
            cp.wait_send()
        for cp in local:
            cp.wait()

    def wait(self, src, dst, sems):
        self.relay(src, dst, sems)
        self.finish(src, dst, sems)


def _call(body, *, name, args, in_specs, out_specs, out_shape, grid=(), scratch_shapes=(), sem=None, comm=None,
          aliases=None, vmem_mb=48):
    in_specs, out_specs, out_shape = list(in_specs), list(out_specs), list(out_shape)
    scratch, args = list(scratch_shapes), list(args)
    n_in, n_out, n_scr = len(in_specs), len(out_specs), len(scratch)
    if comm is None:
        def kernel_body(*refs):
            body(*refs)
    else:
        c_in, c_out = len(comm.srcs), len(comm.out_shape)

        def kernel_body(*refs):
            o0 = n_in + c_in
            s0 = o0 + n_out + c_out
            cin, cout, sems = refs[n_in:o0], refs[o0 + n_out:s0], refs[s0 + n_scr:]
            main = refs[:n_in] + refs[o0:o0 + n_out] + refs[s0:s0 + n_scr]
            if grid:
                ids = [pl.program_id(a) for a in range(len(grid))]
                first = functools.reduce(lambda p, q: p & q, [i == 0 for i in ids])
                last = functools.reduce(lambda p, q: p & q, [i == g - 1 for i, g in zip(ids, grid)])

                @pl.when(first)
                def _():
                    comm.start(cin, cout, sems)

                body(*main)

                steps = math.prod(grid)
                if hasattr(comm, "relay") and steps >= 4:
                    flat = functools.reduce(lambda p, q: p + q, [i * math.prod(grid[a + 1:]) for a, i in enumerate(ids)])

                    @pl.when(flat == (5 * steps) // 6)
                    def _():
                        comm.relay(cin, cout, sems)

                    @pl.when(last)
                    def _():
                        comm.finish(cin, cout, sems)
                else:
                    @pl.when(last)
                    def _():
                        comm.wait(cin, cout, sems)
            else:
                comm.start(cin, cout, sems)
                body(*main)
                comm.wait(cin, cout, sems)

        hbm = pl.BlockSpec(memory_space=pl.ANY)
        in_specs += [hbm] * c_in
        out_specs += [hbm] * c_out
        out_shape += comm.out_shape
        scratch += comm.scratch
        args += comm.srcs
    params = dict(vmem_limit_bytes=vmem_mb << 20)
    if grid:
        params["dimension_semantics"] = sem
    outs = pl.pallas_call(
        kernel_body, name=name, grid=grid, in_specs=in_specs, out_specs=out_specs, out_shape=out_shape,
        scratch_shapes=scratch, input_output_aliases=aliases or {}, compiler_params=pltpu.CompilerParams(**params),
    )(*args)
    return list(outs[:n_out]), list(outs[n_out:])


def _ffn_fwd(x, nw, wg, wu, wd, name, comm=None):
    T, D = x.shape
    ns, fs, _ = wg.shape
    tm = min(TM_FFN, T)

    def body(x_ref, nw_ref, wg_ref, wu_ref, wd_ref, xo_ref, g_ref, u_ref, h_s, acc_s):
        j = pl.program_id(1)

        @pl.when(j == 0)
        def _():
            xv = x_ref[...]
            h_s[...] = (xv * _rms_r(xv) * nw_ref[...]).astype(BF)
            acc_s[...] = jnp.zeros_like(acc_s)

        h = h_s[...]
        gb = lax.dot_general(h, wg_ref[...], NT_DIMS, preferred_element_type=F32).astype(BF)
        ub = lax.dot_general(h, wu_ref[...], NT_DIMS, preferred_element_type=F32).astype(BF)
        g_ref[...] = gb
        u_ref[...] = ub
        g = gb.astype(F32)
        a = (g * _sigmoid(g) * ub.astype(F32)).astype(BF)
        acc_s[...] += jnp.dot(a, wd_ref[...], preferred_element_type=F32)

        @pl.when(j == ns - 1)
        def _():
            xo_ref[...] = x_ref[...] + 0.5 * acc_s[...]

    wspec = pl.BlockSpec((None, fs, D), lambda i, j: (j, 0, 0))
    return _call(
        body, name=name, grid=(T // tm, ns), args=(x, nw, wg, wu, wd), comm=comm, vmem_mb=56,
        in_specs=[pl.BlockSpec((tm, D), lambda i, j: (i, 0)),
                  pl.BlockSpec((1, D), lambda i, j: (0, 0)),
                  wspec, wspec,
                  pl.BlockSpec((None, fs, D), lambda i, j: (j, 0, 0))],
        out_specs=[pl.BlockSpec((tm, D), lambda i, j: (i, 0)),
                   pl.BlockSpec((None, tm, fs), lambda i, j: (j, i, 0)),
                   pl.BlockSpec((None, tm, fs), lambda i, j: (j, i, 0))],
        out_shape=[_sds((T, D), F32), _sds((ns, T, fs), BF), _sds((ns, T, fs), BF)],
        scratch_shapes=[pltpu.VMEM((tm, D), BF), pltpu.VMEM((tm, D), F32)],
        sem=(ARB, ARB))


def _ffn_bwd_hidden(dxo, x, nw, g, u, wd, name, comm=None):
    T, D = x.shape
    ns, fs, _ = wd.shape
    tm = min(TM_FFN, T)

    def body(dxo_ref, x_ref, nw_ref, g_ref, u_ref, wd_ref, dg_ref, du_ref, a_ref, h_ref, dacc_ref, dacc_s):
        @pl.when(pl.program_id(1) == 0)
        def _():
            xv = x_ref[...]
            h_ref[...] = (xv * _rms_r(xv) * nw_ref[...]).astype(BF)
            db = (0.5 * dxo_ref[...]).astype(BF)
            dacc_ref[...] = db
            dacc_s[...] = db

        da = lax.dot_general(dacc_s[...], wd_ref[...], NT_DIMS, preferred_element_type=F32)
        gv = g_ref[...].astype(F32)
        uv = u_ref[...].astype(F32)
        s = _sigmoid(gv)
        sg = gv * s
        a_ref[...] = (sg * uv).astype(BF)
        du_ref[...] = (da * sg).astype(BF)
        dg_ref[...] = (da * uv * (s * (1.0 + gv * (1.0 - s)))).astype(BF)

    tok = pl.BlockSpec((tm, D), lambda i, j: (i, 0))
    hid = pl.BlockSpec((None, tm, fs), lambda i, j: (j, i, 0))
    return _call(
        body, name=name, grid=(T // tm, ns), args=(dxo, x, nw, g, u, wd), comm=comm, vmem_mb=56,
        in_specs=[tok, tok, pl.BlockSpec((1, D), lambda i, j: (0, 0)), hid, hid,
                  pl.BlockSpec((None, fs, D), lambda i, j: (j, 0, 0))],
        out_specs=[hid, hid, hid, tok, tok],
        out_shape=[_sds((ns, T, fs), BF)] * 3 + [_sds((T, D), BF)] * 2,
        scratch_shapes=[pltpu.VMEM((tm, D), BF)],
        sem=(ARB, ARB))


def _ffn_bwd_resid(dg, du, wg, wu, x, nw, dxo, name, comm=None):
    T, D = x.shape
    ns, fs, _ = wg.shape
    tm = min(TM_FFN, T)

    def body(dg_ref, du_ref, wg_ref, wu_ref, x_ref, nw_ref, dxo_ref, dx_ref, dnw_ref, acc_s):
        i = pl.program_id(0)
        j = pl.program_id(1)
        prod = (jnp.dot(dg_ref[...], wg_ref[...], preferred_element_type=F32)
                + jnp.dot(du_ref[...], wu_ref[...], preferred_element_type=F32))

        @pl.when((i == 0) & (j == 0))
        def _():
            dnw_ref[...] = jnp.zeros_like(dnw_ref)

        @pl.when(j == 0)
        def _():
            acc_s[...] = prod

        @pl.when(j > 0)
        def _():
            acc_s[...] += prod

        @pl.when(j == ns - 1)
        def _():
            dx, dn = _rms_bwd(acc_s[...], x_ref[...], nw_ref[...])
            dx_ref[...] = dxo_ref[...] + dx
            dnw_ref[...] += dn

    tok = pl.BlockSpec((tm, D), lambda i, j: (i, 0))
    row = pl.BlockSpec((1, D), lambda i, j: (0, 0))
    hid = pl.BlockSpec((None, tm, fs), lambda i, j: (j, i, 0))
    wspec = pl.BlockSpec((None, fs, D), lambda i, j: (j, 0, 0))
    return _call(
        body, name=name, grid=(T // tm, ns), args=(dg, du, wg, wu, x, nw, dxo), comm=comm, vmem_mb=56,
        in_specs=[hid, hid, wspec, wspec, tok, row, tok],
        out_specs=[tok, row],
        out_shape=[_sds((T, D), F32), _sds((1, D), F32)],
        scratch_shapes=[pltpu.VMEM((tm, D), F32)],
        sem=(ARB, ARB))


def _tn(a, b, a_spec, b_spec, out_shape, out_spec, grid, name, prev=None, comm=None):
    nk = grid[-1]
    acc_shape = tuple(d for d in out_spec.block_shape if d is not None)

    def body(*refs):
        a_ref, b_ref = refs[0], refs[1]
        o_ref, acc = refs[-2], refs[-1]
        k = pl.program_id(2)
        prod = lax.dot_general(a_ref[...], b_ref[...], TN_DIMS, preferred_element_type=F32)

        @pl.when(k == 0)
        def _():
            acc[...] = prod

        @pl.when(k > 0)
        def _():
            acc[...] += prod

        @pl.when(k == nk - 1)
        def _():
            o_ref[...] = acc[...].astype(o_ref.dtype)

    in_specs = [a_spec, b_spec]
    args = [a, b]
    aliases = {}
    if prev is not None:
        in_specs.append(pl.BlockSpec(memory_space=pl.ANY))
        args.append(prev)
        aliases = {2: 0}
    main, extra = _call(
        body, name=name, grid=grid, args=args, in_specs=in_specs, out_specs=[out_spec], out_shape=[out_shape],
        scratch_shapes=[pltpu.VMEM(acc_shape, F32)], aliases=aliases, sem=(ARB, ARB, ARB), comm=comm)
    return main[0] if comm is None else (main[0], extra)


def _tn_gates(h, dbig, ns, tk, name):
    T, D = h.shape
    dq = D // ns
    nk = T // tk

    def body(a_ref, b_ref, o_ref, acc):
        k = pl.program_id(1)
        prod = lax.dot_general(a_ref[...], b_ref[...], TN_DIMS, preferred_element_type=F32)

        @pl.when(k == 0)
        def _():
            acc[...] = prod

        @pl.when(k > 0)
        def _():
            acc[...] += prod

        @pl.when(k == nk - 1)
        def _():
            for s in range(ns):
                o_ref[s] = acc[s * dq:(s + 1) * dq, :].astype(o_ref.dtype)

    return pl.pallas_call(
        body, name=name, grid=(3, nk),
        in_specs=[pl.BlockSpec((tk, D), lambda q, k: (k, 0)), pl.BlockSpec((tk, D), lambda q, k: (k, q))],
        out_specs=pl.BlockSpec((ns, None, dq, D), lambda q, k: (0, q, 0, 0)),
        out_shape=_sds((ns, 3, dq, D), BF),
        scratch_shapes=[pltpu.VMEM((D, D), F32)],
        compiler_params=_cp((PAR, ARB)),
    )(h, dbig)


def _inproj_fwd(x, nw, wbig, name):
    T, D = x.shape
    nb = wbig.shape[-1]
    tm = min(2 * TM, T)
    bn = min(2048, nb)

    def body(x_ref, nw_ref, w_ref, o_ref, h_ref, h_s):
        @pl.when(pl.program_id(1) == 0)
        def _():
            xv = x_ref[...]
            hb = (xv * _rms_r(xv) * nw_ref[...]).astype(BF)
            h_s[...] = hb
            h_ref[...] = hb

        o_ref[...] = jnp.dot(h_s[...], w_ref[...], preferred_element_type=F32).astype(BF)

    return pl.pallas_call(
        body, name=name, grid=(T // tm, nb // bn),
        in_specs=[pl.BlockSpec((tm, D), lambda i, n: (i, 0)),
                  pl.BlockSpec((1, D), lambda i, n: (0, 0)),
                  pl.BlockSpec((D, bn), lambda i, n: (0, n))],
        out_specs=[pl.BlockSpec((tm, bn), lambda i, n: (i, n)),
                   pl.BlockSpec((tm, D), lambda i, n: (i, 0))],
        out_shape=[_sds((T, nb), BF), _sds((T, D), BF)],
        scratch_shapes=[pltpu.VMEM((tm, D), BF)],
        compiler_params=_cp((PAR, ARB)),
    )(x, nw, wbig)


def _inproj_bwd(dbig, wbig, x, nw, dxin, name, comm=None):
    T, D = x.shape
    nb = wbig.shape[-1]
    tm = min(TM_FFN, T)
    tk = min(2048, nb)
    nk = nb // tk

    def body(a_ref, w_ref, x_ref, nw_ref, dxin_ref, dx_ref, dnw_ref, acc_s):
        i = pl.program_id(0)
        k = pl.program_id(1)
        prod = lax.dot_general(a_ref[...], w_ref[...], NT_DIMS, preferred_element_type=F32)

        @pl.when((i == 0) & (k == 0))
        def _():
            dnw_ref[...] = jnp.zeros_like(dnw_ref)

        @pl.when(k == 0)
        def _():
            acc_s[...] = prod

        @pl.when(k > 0)
        def _():
            acc_s[...] += prod

        @pl.when(k == nk - 1)
        def _():
            dx, dn = _rms_bwd(acc_s[...], x_ref[...], nw_ref[...])
            dx_ref[...] = dxin_ref[...] + dx
            dnw_ref[...] += dn

    tok = pl.BlockSpec((tm, D), lambda i, k: (i, 0))
    row = pl.BlockSpec((1, D), lambda i, k: (0, 0))
    return _call(
        body, name=name, grid=(T // tm, nk), args=(dbig, wbig, x, nw, dxin), comm=comm, vmem_mb=56,
        in_specs=[pl.BlockSpec((tm, tk), lambda i, k: (i, k)),
                  pl.BlockSpec((D, tk), lambda i, k: (0, k)),
                  tok, row, tok],
        out_specs=[tok, row],
        out_shape=[_sds((T, D), F32), _sds((1, D), F32)],
        scratch_shapes=[pltpu.VMEM((tm, D), F32)],
        sem=(ARB, ARB))


CONV_R = 512
CONV_BASE, CONV_GROUP = 0, 3
ATT_BASE, ATT_GROUP = 12, 3
RET_BASE, RET_GROUP = 24, 4
N_SEG = 10


N_IN_BLOCKS = N_SEG * BRANCH_W // LANE


def _orig_block(p):
    nblk = BRANCH_W // LANE
    qa, qr = p - ATT_BASE, p - RET_BASE
    conv = (p % CONV_GROUP) * nblk + p // CONV_GROUP
    att = (7 + qa % ATT_GROUP) * nblk + qa // ATT_GROUP
    ret = (3 + qr % RET_GROUP) * nblk + qr // RET_GROUP
    return jnp.where(p < ATT_BASE, conv, jnp.where(p < RET_BASE, att, ret))


def _copy_blocks(src, in_spec, out_shape, out_spec, grid, name, prev=None):
    def body(*refs):
        refs[-1][...] = refs[0][...]

    in_specs, args, aliases = [in_spec], [src], {}
    if prev is not None:
        in_specs.append(pl.BlockSpec(memory_space=pl.ANY))
        args.append(prev)
        aliases = {1: 0}
    return pl.pallas_call(
        body, name=name, grid=grid, in_specs=in_specs, out_specs=out_spec, out_shape=out_shape,
        input_output_aliases=aliases, compiler_params=_cp(tuple(PAR for _ in grid)),
    )(*args)


def _build_wbig(gates4, win4, name):
    ns, _, dq, D = gates4.shape
    per = win4.shape[-1] // LANE
    shape = _sds((D, 3 * D + N_IN_BLOCKS * LANE), gates4.dtype)
    out = _copy_blocks(gates4, pl.BlockSpec((None, None, dq, D), lambda s, i: (s, i, 0, 0)), shape,
                       pl.BlockSpec((dq, D), lambda s, i: (s, i)), (ns, 3), name + "_gates")
    return _copy_blocks(
        win4, pl.BlockSpec((None, D, LANE), lambda p: (_orig_block(p) // per, 0, _orig_block(p) % per)), shape,
        pl.BlockSpec((D, LANE), lambda p: (0, 3 * D // LANE + p)), (N_IN_BLOCKS,), name + "_in", prev=out)


def _ungroup_dw_in(dwp, ns, name):
    D = dwp.shape[0]
    per = N_IN_BLOCKS // ns
    return _copy_blocks(
        dwp, pl.BlockSpec((D, LANE), lambda p: (0, p)), _sds((ns, D, per * LANE), dwp.dtype),
        pl.BlockSpec((None, D, LANE), lambda p: (_orig_block(p) // per, 0, _orig_block(p) % per)), (N_IN_BLOCKS,), name)


def _seg0(big):
    return (big.shape[1] - N_SEG * BRANCH_W) // LANE


def _group_spec(big, base, group, rows, where):
    first = (_seg0(big) + base) // group
    assert first * group == _seg0(big) + base

    def index(*ids):
        r, g = where(*ids)
        return r, first + g

    return pl.BlockSpec((rows, group * LANE), index)


CU, CB, CC = (slice(k * LANE, (k + 1) * LANE) for k in range(3))
AQ, AK, AV = CU, CB, CC
RQ, RK, RV, RG = (slice(k * LANE, (k + 1) * LANE) for k in range(4))


def _conv_fwd(big, cw, name):
    T = big.shape[0]
    R = min(CONV_R, T)

    def body(g_ref, w_ref, y_ref, z_s):
        z_s[pl.ds(0, 8), :] = jnp.zeros((8, LANE), F32)

        def fill(t, c):
            sl = pl.ds(pl.multiple_of(t * R, R), R)
            z_s[pl.ds(pl.multiple_of(t * R + 8, 8), R), :] = g_ref[sl, CC].astype(F32) * g_ref[sl, CU].astype(F32)
            return c

        lax.fori_loop(0, T // R, fill, 0)
        w0, w1, w2 = w_ref[0:1, :], w_ref[1:2, :], w_ref[2:3, :]

        def step(t, c):
            zz = z_s[pl.ds(pl.multiple_of(t * R, R), R + 8), :]
            z0 = zz[8:]
            z1 = pltpu.roll(zz, 1, 0)[8:]
            z2 = pltpu.roll(zz, 2, 0)[8:]
            sl = pl.ds(pl.multiple_of(t * R, R), R)
            y_ref[sl, :] = (g_ref[sl, CB].astype(F32) * (w2 * z0 + w1 * z1 + w0 * z2)).astype(BF)
            return c

        lax.fori_loop(0, T // R, step, 0)

    return pl.pallas_call(
        body, name=name, grid=(BRANCH_W // LANE,),
        in_specs=[_group_spec(big, CONV_BASE, CONV_GROUP, T, lambda j: (0, j)),
                  pl.BlockSpec((3, LANE), lambda j: (0, j))],
        out_specs=pl.BlockSpec((T, LANE), lambda j: (0, j)),
        out_shape=_sds((T, BRANCH_W), BF),
        scratch_shapes=[pltpu.VMEM((T + 8, LANE), F32)],
        compiler_params=_cp((PAR,)),
    )(big, cw)


def _conv_bwd(big, dy, cw, dbig, name):
    T = big.shape[0]
    R = min(CONV_R, T)

    def body(g_ref, dy_ref, w_ref, _, o_ref, dw_ref, z_s, d_s):
        z_s[pl.ds(0, 8), :] = jnp.zeros((8, LANE), F32)
        d_s[pl.ds(T, 8), :] = jnp.zeros((8, LANE), F32)

        def fill(t, c):
            sl = pl.ds(pl.multiple_of(t * R, R), R)
            z_s[pl.ds(pl.multiple_of(t * R + 8, 8), R), :] = g_ref[sl, CC].astype(F32) * g_ref[sl, CU].astype(F32)
            d_s[sl, :] = dy_ref[sl, :].astype(F32) * g_ref[sl, CB].astype(F32)
            return c

        lax.fori_loop(0, T // R, fill, 0)
        w0, w1, w2 = w_ref[0:1, :], w_ref[1:2, :], w_ref[2:3, :]

        def step(t, carry):
            a0, a1, a2 = carry
            zz = z_s[pl.ds(pl.multiple_of(t * R, R), R + 8), :]
            z0 = zz[8:]
            z1 = pltpu.roll(zz, 1, 0)[8:]
            z2 = pltpu.roll(zz, 2, 0)[8:]
            sl = pl.ds(pl.multiple_of(t * R, R), R)
            dyv = dy_ref[sl, :].astype(F32)
            o_ref[sl, CB] = (dyv * (w2 * z0 + w1 * z1 + w0 * z2)).astype(BF)
            dd = d_s[pl.ds(pl.multiple_of(t * R, R), R + 8), :]
            d0 = dd[:R]
            d1 = pltpu.roll(dd, R + 7, 0)[:R]
            d2 = pltpu.roll(dd, R + 6, 0)[:R]
            dz = w2 * d0 + w1 * d1 + w0 * d2
            o_ref[sl, CC] = (dz * g_ref[sl, CU].astype(F32)).astype(BF)
            o_ref[sl, CU] = (dz * g_ref[sl, CC].astype(F32)).astype(BF)
            a0 = a0 + jnp.sum(d0 * z2, axis=0, keepdims=True)
            a1 = a1 + jnp.sum(d0 * z1, axis=0, keepdims=True)
            a2 = a2 + jnp.sum(d0 * z0, axis=0, keepdims=True)
            return a0, a1, a2

        zero = jnp.zeros((1, LANE), F32)
        a0, a1, a2 = lax.fori_loop(0, T // R, step, (zero, zero, zero))
        dw_ref[0:1, :] = a0
        dw_ref[1:2, :] = a1
        dw_ref[2:3, :] = a2

    group = _group_spec(big, CONV_BASE, CONV_GROUP, T, lambda j: (0, j))
    w = pl.BlockSpec((3, LANE), lambda j: (0, j))
    return pl.pallas_call(
        body, name=name, grid=(BRANCH_W // LANE,),
        in_specs=[group, pl.BlockSpec((T, LANE), lambda j: (0, j)), w, pl.BlockSpec(memory_space=pl.ANY)],
        out_specs=[group, w],
        out_shape=[_sds(dbig.shape, BF), _sds((3, BRANCH_W), F32)],
        scratch_shapes=[pltpu.VMEM((T + 8, LANE), F32), pltpu.VMEM((T + 8, LANE), F32)],
        input_output_aliases={3: 0}, compiler_params=_cp((PAR,)),
    )(big, dy, cw, dbig)


def _ret_tables(T):
    L = min(RET_L, T)
    hh = jnp.arange(H_RET, dtype=F32)
    lg = jnp.log1p(-jnp.exp2(-5.0 - hh))
    n = jnp.arange(L, dtype=F32)
    a = jnp.exp(lg[:, None] * (n + 1.0))
    b = jnp.exp(lg[:, None] * (L - 1.0 - n))
    gl = jnp.exp(lg * L)
    ch = jnp.arange(L) // CHUNK
    m = jnp.exp(lg[:, None, None] * jnp.abs(n[:, None] - n[None, :])) * (ch[None, :] <= ch[:, None]).astype(F32)
    inv_freq = ROPE_BASE ** (-jnp.linspace(0.0, 1.0, DK_RET // 2, dtype=F32))
    ang = jnp.arange(T, dtype=F32)[:, None] * inv_freq[None, :]
    cos, sin = jnp.cos(ang), jnp.sin(ang)
    return dict(
        L=L, M=m,
        a=jnp.broadcast_to(a[:, :, None], (H_RET, L, DK_RET)),
        b=jnp.broadcast_to(b[:, :, None], (H_RET, L, DK_RET)),
        gl=jnp.broadcast_to(gl[:, None, None], (H_RET, 1, DK_RET)),
        cos=jnp.concatenate([cos, cos], axis=-1), sin=jnp.concatenate([-sin, sin], axis=-1))


def _rot(x, cs, sn):
    return x * cs + pltpu.roll(x, DK_RET // 2, 1) * sn


def _unrot(dy, cs, sn):
    return dy * cs + pltpu.roll(dy * sn, DK_RET // 2, 1)


def _ret_fwd(big, tb, name, comm=None):
    T = big.shape[0]
    L = tb["L"]
    nsc = T // L
    scale = DK_RET ** -0.5

    def body(x_ref, cos_ref, sin_ref, m_ref, a_ref, b_ref, gl_ref, y_ref, o_ref, st_ref, s_s):
        @pl.when(pl.program_id(1) == 0)
        def _():
            s_s[...] = jnp.zeros_like(s_s)

        cs, sn = cos_ref[...], sin_ref[...]
        qt = _rot(x_ref[:, RQ].astype(F32), cs, sn) * scale
        kt = _rot(x_ref[:, RK].astype(F32), cs, sn)
        qb, kb, vb = qt.astype(BF), kt.astype(BF), x_ref[:, RV]
        s_prev = s_s[...]
        st_ref[...] = s_prev
        p = lax.dot_general(qb, kb, NT_DIMS, preferred_element_type=F32) * m_ref[...]
        o = (jnp.dot(p.astype(BF), vb, preferred_element_type=F32)
             + jnp.dot((qt * a_ref[...]).astype(BF), s_prev.astype(BF), preferred_element_type=F32))
        s_s[...] = s_prev * gl_ref[...] + lax.dot_general((kt * b_ref[...]).astype(BF), vb, TN_DIMS,
                                                         preferred_element_type=F32)
        o_ref[...] = o
        gv = x_ref[:, RG].astype(F32)
        y_ref[...] = (gv * _sigmoid(gv) * o * _rms_r(o)).astype(BF)

    tab = pl.BlockSpec((L, DK_RET), lambda h, i: (i, 0))
    per_head = pl.BlockSpec((None, L, DK_RET), lambda h, i: (h, 0, 0))
    out = pl.BlockSpec((L, LANE), lambda h, i: (i, h))
    return _call(
        body, name=name, grid=(H_RET, nsc), comm=comm,
        args=(big, tb["cos"], tb["sin"], tb["M"], tb["a"], tb["b"], tb["gl"]),
        in_specs=[_group_spec(big, RET_BASE, RET_GROUP, L, lambda h, i: (i, h)), tab, tab,
                  pl.BlockSpec((None, L, L), lambda h, i: (h, 0, 0)), per_head, per_head,
                  pl.BlockSpec((None, 1, DK_RET), lambda h, i: (h, 0, 0))],
        out_specs=[out, out, pl.BlockSpec((None, None, DK_RET, DK_RET), lambda h, i: (i, h, 0, 0))],
        out_shape=[_sds((T, BRANCH_W), BF), _sds((T, BRANCH_W), F32), _sds((nsc, H_RET, DK_RET, DK_RET), F32)],
        scratch_shapes=[pltpu.VMEM((DK_RET, DK_RET), F32)],
        sem=(ARB, ARB))


def _ret_bwd(big, o, st, dy, tb, dbig, name):
    T = big.shape[0]
    L = tb["L"]
    nsc = T // L
    scale = DK_RET ** -0.5

    def body(x_ref, cos_ref, sin_ref, m_ref, a_ref, b_ref, gl_ref, o_ref, st_ref, dy_ref, _, d_ref, ds_s):
        @pl.when(pl.program_id(1) == 0)
        def _():
            ds_s[...] = jnp.zeros_like(ds_s)

        cs, sn = cos_ref[...], sin_ref[...]
        mm, av, bv = m_ref[...], a_ref[...], b_ref[...]
        qt = _rot(x_ref[:, RQ].astype(F32), cs, sn) * scale
        kt = _rot(x_ref[:, RK].astype(F32), cs, sn)
        qb, kb, vb = qt.astype(BF), kt.astype(BF), x_ref[:, RV]
        pb = (lax.dot_general(qb, kb, NT_DIMS, preferred_element_type=F32) * mm).astype(BF)
        ov = o_ref[...]
        r = _rms_r(ov)
        oh = ov * r
        gv = x_ref[:, RG].astype(F32)
        sg = _sigmoid(gv)
        dyv = dy_ref[...].astype(F32)
        d_ref[:, RG] = (dyv * oh * (sg * (1.0 + gv * (1.0 - sg)))).astype(BF)
        doh = dyv * gv * sg
        dob = (r * (doh - oh * jnp.mean(doh * oh, axis=-1, keepdims=True))).astype(BF)
        dsb = ds_s[...].astype(BF)
        spb = st_ref[...].astype(BF)
        dpb = (lax.dot_general(dob, vb, NT_DIMS, preferred_element_type=F32) * mm).astype(BF)
        dqt = (jnp.dot(dpb, kb, preferred_element_type=F32)
               + lax.dot_general(dob, spb, NT_DIMS, preferred_element_type=F32) * av)
        dkt = (lax.dot_general(dpb, qb, TN_DIMS, preferred_element_type=F32)
               + lax.dot_general(vb, dsb, NT_DIMS, preferred_element_type=F32) * bv)
        dv = (lax.dot_general(pb, dob, TN_DIMS, preferred_element_type=F32)
              + jnp.dot((kt * bv).astype(BF), dsb, preferred_element_type=F32))
        ds_s[...] = ds_s[...] * gl_ref[...] + lax.dot_general((qt * av).astype(BF), dob, TN_DIMS,
                                                              preferred_element_type=F32)
        d_ref[:, RQ] = (_unrot(dqt, cs, sn) * scale).astype(BF)
        d_ref[:, RK] = _unrot(dkt, cs, sn).astype(BF)
        d_ref[:, RV] = dv.astype(BF)

    def rev(i):
        return nsc - 1 - i

    group = _group_spec(big, RET_BASE, RET_GROUP, L, lambda h, i: (rev(i), h))
    tab = pl.BlockSpec((L, DK_RET), lambda h, i: (rev(i), 0))
    per_head = pl.BlockSpec((None, L, DK_RET), lambda h, i: (h, 0, 0))
    out = pl.BlockSpec((L, LANE), lambda h, i: (rev(i), h))
    return pl.pallas_call(
        body, name=name, grid=(H_RET, nsc),
        in_specs=[group, tab, tab,
                  pl.BlockSpec((None, L, L), lambda h, i: (h, 0, 0)), per_head, per_head,
                  pl.BlockSpec((None, 1, DK_RET), lambda h, i: (h, 0, 0)),
                  out, pl.BlockSpec((None, None, DK_RET, DK_RET), lambda h, i: (rev(i), h, 0, 0)), out,
                  pl.BlockSpec(memory_space=pl.ANY)],
        out_specs=group,
        out_shape=_sds(dbig.shape, BF),
        scratch_shapes=[pltpu.VMEM((DK_RET, DK_RET), F32)],
        input_output_aliases={10: 0}, compiler_params=_cp((PAR, ARB)),
    )(big, tb["cos"], tb["sin"], tb["M"], tb["a"], tb["b"], tb["gl"], o, st, dy, dbig)


def _relbias_onehot(n):
    mm = lax.broadcasted_iota(jnp.int32, (RB_PAD, ATT_TOEP), 1)
    rr = lax.broadcasted_iota(jnp.int32, (RB_PAD, ATT_TOEP), 0)
    idx = jnp.clip(n + ATT_TOEP - mm, 0, 2 * REL_CLIP)
    return (rr == idx).astype(F32)


def _split3(x):
    hi = x.astype(BF).astype(F32)
    mid = (x - hi).astype(BF).astype(F32)
    lo = x - hi - mid
    return jnp.concatenate([hi, mid, lo], axis=0).astype(BF)


def _join3(y):
    k = y.shape[0] // 3
    return (y[:k] + y[k:2 * k]) + y[2 * k:]


def _relbias_expand(rel_bias, name, comm=None):
    far = ATT_SPAN - ATT_TOEP
    n_layers = rel_bias.shape[0]
    rbp = jnp.pad(rel_bias, ((0, 0), (0, 0), (0, RB_PAD - N_REL)))

    def body(rb_ref, o_ref):
        for l in range(n_layers):
            rb = rb_ref[l]
            const = jnp.broadcast_to(rb[:, 2 * REL_CLIP:2 * REL_CLIP + 1], (H_ATT, far))
            rb3 = _split3(rb)

            def row(n, c):
                toep = _join3(jnp.dot(rb3, _relbias_onehot(n).astype(BF), preferred_element_type=F32))
                m = lax.broadcasted_iota(jnp.int32, (1, ATT_SPAN), 1)
                d = n // CHUNK + N_PREV - m // CHUNK
                neg = jnp.where((d >= 0) & (d <= N_PREV), 0.0, NEG_INF).astype(F32)
                o_ref[l, n] = jnp.concatenate([const, toep], axis=1) + neg
                return c

            lax.fori_loop(0, ATT_TQ, row, 0)

    (out,), extra = _call(
        body, name=name, args=(rbp,), comm=comm,
        in_specs=[pl.BlockSpec(memory_space=pltpu.VMEM)],
        out_specs=[pl.BlockSpec(memory_space=pltpu.VMEM)],
        out_shape=[_sds((n_layers, ATT_TQ, H_ATT, ATT_SPAN), F32)])
    return jnp.transpose(out, (0, 2, 1, 3)), extra


def _relbias_grad(dbt, name):
    far = ATT_SPAN - ATT_TOEP

    def body(d_ref, o_ref):
        def row(n, carry):
            acc, cs = carry
            dn = d_ref[n]
            acc = acc + _join3(lax.dot_general(_split3(dn[:, far:]), _relbias_onehot(n).astype(BF), NT_DIMS,
                                               preferred_element_type=F32))
            cs = cs + jnp.sum(dn[:, :far], axis=1, keepdims=True)
            return acc, cs

        acc, cs = lax.fori_loop(0, ATT_TQ, row, (jnp.zeros((H_ATT, RB_PAD), F32), jnp.zeros((H_ATT, 1), F32)))
        rr = lax.broadcasted_iota(jnp.int32, (H_ATT, RB_PAD), 1)
        o_ref[...] = acc + jnp.where(rr == 2 * REL_CLIP, cs, 0.0)

    return pl.pallas_call(
        body, name=name,
        in_specs=[pl.BlockSpec(memory_space=pltpu.VMEM)],
        out_specs=pl.BlockSpec(memory_space=pltpu.VMEM),
        out_shape=_sds((H_ATT, RB_PAD), F32),
    )(dbt)


def _att_pad_fill(dst_s, src_ref, cols, T):
    dst_s[pl.ds(0, ATT_PAD), :] = jnp.zeros((ATT_PAD, LANE), dst_s.dtype)
    R = min(512, T)

    def cp(t, c):
        dst_s[pl.ds(pl.multiple_of(ATT_PAD + t * R, LANE), R), :] = src_ref[pl.ds(pl.multiple_of(t * R, R), R), cols]
        return c

    lax.fori_loop(0, T // R, cp, 0)


ATT_WIN = ATT_SUB * ATT_TQ + ATT_PAD


def _att_probs(s_full, sub, bias, t0):
    s = s_full[sub * ATT_TQ:(sub + 1) * ATT_TQ, sub * ATT_TQ:sub * ATT_TQ + ATT_SPAN] * (DH_ATT ** -0.5) + bias
    key_pos = t0 + sub * ATT_TQ - ATT_PAD + lax.broadcasted_iota(jnp.int32, (1, ATT_SPAN), 1)
    s = jnp.where(key_pos >= 0, s, NEG_INF)
    p = jnp.exp(s - jnp.max(s, axis=-1, keepdims=True))
    return p * (1.0 / jnp.sum(p, axis=-1, keepdims=True))


def _att_band(tiles):
    rows = []
    for sub, t in enumerate(tiles):
        parts = []
        if sub:
            parts.append(jnp.zeros((ATT_TQ, sub * ATT_TQ), BF))
        parts.append(t)
        if sub < ATT_SUB - 1:
            parts.append(jnp.zeros((ATT_TQ, (ATT_SUB - 1 - sub) * ATT_TQ), BF))
        rows.append(jnp.concatenate(parts, axis=1))
    return jnp.concatenate(rows, axis=0)


def _att_head_masks(x):
    first = lax.broadcasted_iota(jnp.int32, (1, LANE), 1) < DH_ATT
    zero = jnp.zeros_like(x)
    return first, (jnp.where(first, x, zero), jnp.where(first, zero, x))


def _att_fwd(big, bias, name, comm=None):
    T = big.shape[0]
    rows = ATT_SUB * ATT_TQ
    nt = T // rows

    def body(x_ref, b_ref, y_ref, kp_s, vp_s):
        i = pl.program_id(1)

        @pl.when(i == 0)
        def _():
            _att_pad_fill(kp_s, x_ref, AK, T)
            _att_pad_fill(vp_s, x_ref, AV, T)

        t0 = pl.multiple_of(i * rows, rows)
        kw = kp_s[pl.ds(t0, ATT_WIN), :]
        vw = vp_s[pl.ds(t0, ATT_WIN), :]
        first, qm = _att_head_masks(x_ref[pl.ds(t0, rows), AQ])
        outs = []
        for hh in range(2):
            s_full = lax.dot_general(qm[hh], kw, NT_DIMS, preferred_element_type=F32)
            band = _att_band([_att_probs(s_full, sub, b_ref[hh], t0).astype(BF) for sub in range(ATT_SUB)])
            outs.append(jnp.dot(band, vw, preferred_element_type=F32))
        y_ref[...] = jnp.where(first, outs[0], outs[1]).astype(BF)

    return _call(
        body, name=name, grid=(H_ATT // 2, nt), args=(big, bias), comm=comm,
        in_specs=[_group_spec(big, ATT_BASE, ATT_GROUP, T, lambda p, i: (0, p)),
                  pl.BlockSpec((2, ATT_TQ, ATT_SPAN), lambda p, i: (p, 0, 0))],
        out_specs=[pl.BlockSpec((rows, LANE), lambda p, i: (i, p))],
        out_shape=[_sds((T, BRANCH_W), BF)],
        scratch_shapes=[pltpu.VMEM((T + ATT_PAD, LANE), BF), pltpu.VMEM((T + ATT_PAD, LANE), BF)],
        sem=(ARB, ARB))


def _att_bwd(big, bias, dy, dbig, name, comm=None):
    T = big.shape[0]
    rows = ATT_SUB * ATT_TQ
    nt = T // rows
    scale = DH_ATT ** -0.5

    def body(x_ref, b_ref, dy_ref, _, d_ref, db_ref, kp_s, vp_s, dk_s, dv_s):
        i = pl.program_id(1)

        @pl.when(i == 0)
        def _():
            _att_pad_fill(kp_s, x_ref, AK, T)
            _att_pad_fill(vp_s, x_ref, AV, T)
            dk_s[...] = jnp.zeros_like(dk_s)
            dv_s[...] = jnp.zeros_like(dv_s)
            db_ref[...] = jnp.zeros_like(db_ref)

        t0 = pl.multiple_of(i * rows, rows)
        win = pl.ds(t0, ATT_WIN)
        kw = kp_s[win, :]
        vw = vp_s[win, :]
        first, qm = _att_head_masks(x_ref[pl.ds(t0, rows), AQ])
        _, dom = _att_head_masks(dy_ref[...])
        dqs, dkt, dvt = [], None, None
        for hh in range(2):
            s_full = lax.dot_general(qm[hh], kw, NT_DIMS, preferred_element_type=F32)
            dp_full = lax.dot_general(dom[hh], vw, NT_DIMS, preferred_element_type=F32)
            ps, dss, db = [], [], None
            for sub in range(ATT_SUB):
                pn = _att_probs(s_full, sub, b_ref[hh], t0)
                dp = dp_full[sub * ATT_TQ:(sub + 1) * ATT_TQ, sub * ATT_TQ:sub * ATT_TQ + ATT_SPAN]
                ds = pn * (dp - jnp.sum(dp * pn, axis=-1, keepdims=True))
                db = ds if db is None else db + ds
                ps.append(pn.astype(BF))
                dss.append(ds.astype(BF))
            db_ref[hh] += db
            ds_band, p_band = _att_band(dss), _att_band(ps)
            dqs.append(jnp.dot(ds_band, kw, preferred_element_type=F32))
            qt = jnp.transpose(qm[hh].astype(F32)).astype(BF)
            dot_ = jnp.transpose(dom[hh].astype(F32)).astype(BF)
            dk_h = jnp.dot(qt, ds_band, preferred_element_type=F32)
            dv_h = jnp.dot(dot_, p_band, preferred_element_type=F32)
            dkt = dk_h if dkt is None else dkt + dk_h
            dvt = dv_h if dvt is None else dvt + dv_h
        d_ref[pl.ds(t0, rows), AQ] = (jnp.where(first, dqs[0], dqs[1]) * scale).astype(BF)
        dk_s[win, :] += jnp.transpose(dkt) * scale
        dv_s[win, :] += jnp.transpose(dvt)

        @pl.when(i == nt - 1)
        def _():
            R = min(512, T)

            def cp(t, c):
                src = pl.ds(pl.multiple_of(ATT_PAD + t * R, LANE), R)
                dst = pl.ds(pl.multiple_of(t * R, R), R)
                d_ref[dst, AK] = dk_s[src, :].astype(BF)
                d_ref[dst, AV] = dv_s[src, :].astype(BF)
                return c

            lax.fori_loop(0, T // R, cp, 0)

    group = _group_spec(big, ATT_BASE, ATT_GROUP, T, lambda p, i: (0, p))
    tile = pl.BlockSpec((rows, LANE), lambda p, i: (i, p))
    bspec = pl.BlockSpec((2, ATT_TQ, ATT_SPAN), lambda p, i: (p, 0, 0))
    return _call(
        body, name=name, grid=(H_ATT // 2, nt), args=(big, bias, dy, dbig), comm=comm, aliases={3: 0}, vmem_mb=56,
        in_specs=[group, bspec, tile, pl.BlockSpec(memory_space=pl.ANY)],
        out_specs=[group, bspec],
        out_shape=[_sds(dbig.shape, BF), _sds((H_ATT, ATT_TQ, ATT_SPAN), F32)],
        scratch_shapes=[pltpu.VMEM((T + ATT_PAD, LANE), BF), pltpu.VMEM((T + ATT_PAD, LANE), BF),
                        pltpu.VMEM((T + ATT_PAD, LANE), F32), pltpu.VMEM((T + ATT_PAD, LANE), F32)],
        sem=(ARB, ARB))


def _merge_fwd(x1, big, ys, wb, wo, name):
    T, D = x1.shape
    tm = min(TM, T)

    def body(x_ref, gp_ref, yc_ref, yr_ref, ya_ref, wb_ref, wo_ref, x2_ref, p_ref, mg_ref):
        merged = jnp.zeros((tm, D), F32)
        for i, y_ref in enumerate((yc_ref, yr_ref, ya_ref)):
            cols = slice(i * D, (i + 1) * D)
            pb = jnp.dot(y_ref[...], wb_ref[i], preferred_element_type=F32).astype(BF)
            p_ref[:, cols] = pb
            merged = merged + _sigmoid(gp_ref[:, cols].astype(F32)) * pb.astype(F32)
        mb = merged.astype(BF)
        mg_ref[...] = mb
        x2_ref[...] = x_ref[...] + jnp.dot(mb, wo_ref[...], preferred_element_type=F32)

    tok = pl.BlockSpec((tm, D), lambda i: (i, 0))
    wide = pl.BlockSpec((tm, 3 * D), lambda i: (i, 0))
    yspec = pl.BlockSpec((tm, BRANCH_W), lambda i: (i, 0))
    return pl.pallas_call(
        body, name=name, grid=(T // tm,),
        in_specs=[tok, wide, yspec, yspec, yspec,
                  pl.BlockSpec((3, BRANCH_W, D), lambda i: (0, 0, 0)),
                  pl.BlockSpec((D, D), lambda i: (0, 0))],
        out_specs=[tok, wide, tok],
        out_shape=[_sds((T, D), F32), _sds((T, 3 * D), BF), _sds((T, D), BF)],
        compiler_params=_cp((PAR,)),
    )(x1, big, *ys, wb, wo)


def _merge_bwd(dx2, big, p, wb, wo, name):
    T, D = dx2.shape
    tm = min(TM, T)

    def body(dx_ref, gp_ref, p_ref, wb_ref, wo_ref, dp_ref, dgp_ref, dyc_ref, dyr_ref, dya_ref, dxb_ref):
        dxb = dx_ref[...].astype(BF)
        dxb_ref[...] = dxb
        dm = lax.dot_general(dxb, wo_ref[...], NT_DIMS, preferred_element_type=F32)
        for i, dy_ref in enumerate((dyc_ref, dyr_ref, dya_ref)):
            cols = slice(i * D, (i + 1) * D)
            gt = _sigmoid(gp_ref[:, cols].astype(F32))
            dpb = (dm * gt).astype(BF)
            dp_ref[:, cols] = dpb
            dgp_ref[:, cols] = (dm * p_ref[:, cols].astype(F32) * gt * (1.0 - gt)).astype(BF)
            dy_ref[...] = lax.dot_general(dpb, wb_ref[i], NT_DIMS, preferred_element_type=F32).astype(BF)

    tok = pl.BlockSpec((tm, D), lambda i: (i, 0))
    wide = pl.BlockSpec((tm, 3 * D), lambda i: (i, 0))
    yspec = pl.BlockSpec((tm, BRANCH_W), lambda i: (i, 0))
    return pl.pallas_call(
        body, name=name, grid=(T // tm,),
        in_specs=[tok, wide, wide,
                  pl.BlockSpec((3, BRANCH_W, D), lambda i: (0, 0, 0)),
                  pl.BlockSpec((D, D), lambda i: (0, 0))],
        out_specs=[wide, wide, yspec, yspec, yspec, tok],
        out_shape=[_sds((T, 3 * D), BF), _sds(big.shape, BF)] + [_sds((T, BRANCH_W), BF)] * 3 + [_sds((T, D), BF)],
        compiler_params=_cp((PAR,)),
    )(dx2, big, p, wb, wo)


def _loss_head(x, tgt, fw, name):
    T, D = x.shape
    tm = min(TM, T)

    def body(x_ref, t_ref, w_ref, loss_ref, dx_ref, dw_ref):
        @pl.when(pl.program_id(0) == 0)
        def _():
            loss_ref[...] = jnp.zeros_like(loss_ref)
            dw_ref[...] = jnp.zeros_like(dw_ref)

        xv = x_ref[...]
        wv = w_ref[...]
        e = xv * _rms_r(xv) * wv - t_ref[...]
        loss_ref[...] += 0.5 * jnp.sum(jnp.mean(e * e, axis=-1, keepdims=True))
        dx, dn = _rms_bwd(e * (1.0 / D), xv, wv)
        dx_ref[...] = dx
        dw_ref[...] += dn

    tok = pl.BlockSpec((tm, D), lambda i: (i, 0))
    return pl.pallas_call(
        body, name=name, grid=(T // tm,),
        in_specs=[tok, tok, pl.BlockSpec((1, D), lambda i: (0, 0))],
        out_specs=[pl.BlockSpec((8, LANE), lambda i: (0, 0)), tok, pl.BlockSpec((1, D), lambda i: (0, 0))],
        out_shape=[_sds((8, LANE), F32), _sds((T, D), F32), _sds((1, D), F32)],
        compiler_params=_cp((ARB,)),
    )(x, tgt, fw)


def _block_rows(rows, cols):
    cap = max(8, (1 << 18) // cols)
    best = None
    for r in range(8, rows + 1, 8):
        if rows % r == 0 and r <= cap:
            best = r
    return best if best is not None else rows


def _sum8(land, l, n_layers, name, prev=None, comm=None):
    _, rows, cols = land.shape
    br = _block_rows(rows, cols)

    def body(*refs):
        l_ref, o_ref = refs[0], refs[-1]

        def four(base):
            return ((l_ref[base + 3].astype(F32) + l_ref[base].astype(F32)) + l_ref[base + 1].astype(F32)
                    ) + l_ref[base + 2].astype(F32)

        o_ref[...] = four(0) + four(4)

    in_specs = [pl.BlockSpec((2 * N_SHARD, br, cols), lambda i: (0, i, 0))]
    args = [land]
    aliases = {}
    if prev is not None:
        in_specs.append(pl.BlockSpec(memory_space=pl.ANY))
        args.append(prev)
        aliases = {1: 0}
    main, extra = _call(
        body, name=name, grid=(rows // br,), args=args, in_specs=in_specs,
        out_specs=[pl.BlockSpec((None, br, cols), lambda i: (l, i, 0))],
        out_shape=[_sds((n_layers, rows, cols), F32)], aliases=aliases, sem=(ARB,), comm=comm)
    return main[0], extra


def _adamw_math(w, g, m, v):
    m = ADAM_B1 * m + (1.0 - ADAM_B1) * g
    v = ADAM_B2 * v + (1.0 - ADAM_B2) * (g * g)
    m_hat = m / (1.0 - ADAM_B1 ** ADAM_STEP)
    v_hat = v / (1.0 - ADAM_B2 ** ADAM_STEP)
    delta = -ADAM_LR * (m_hat / (jnp.sqrt(v_hat) + ADAM_EPS) + ADAM_WD * w)
    return delta, m, v


def _adamw(w, g, m, v, name):
    rows, cols = w.shape
    br = _block_rows(rows, cols)

    def body(w_ref, g_ref, m_ref, v_ref, d_ref, nm_ref, nv_ref):
        d, nm, nv = _adamw_math(w_ref[...], g_ref[...], m_ref[...], v_ref[...])
        d_ref[...] = d
        nm_ref[...] = nm
        nv_ref[...] = nv

    blk = pl.BlockSpec((br, cols), lambda i: (i, 0))
    return pl.pallas_call(
        body, name=name, grid=(rows // br,),
        in_specs=[blk] * 4, out_specs=[blk] * 3,
        out_shape=[_sds((rows, cols), F32)] * 3,
        compiler_params=_cp((PAR,)),
    )(w, g, m, v)


def _allreduce_small(v, name):
    rows = v.shape[0]
    flips = [(fx, fy, fc) for fx in (0, 1) for fy in (0, 1) for fc in (0, 1) if fx or fy or fc]

    def body(v_ref, o_ref, all_s, ssem, rsem):
        x, y, c = _place()

        def peer(f):
            return (x + f[0] - 2 * x * f[0], y + f[1] - 2 * y * f[1], c + f[2] - 2 * c * f[2])

        def slot(p):
            return all_s.at[4 * p[0] + 2 * p[1] + p[2]]

        def copy(k, f, owner):
            return pltpu.make_async_remote_copy(
                src_ref=v_ref, dst_ref=slot(owner), send_sem=ssem.at[k], recv_sem=rsem.at[k],
                device_id=peer(f), device_id_type=MESH)

        sends = [copy(k, f, (x, y, c)) for k, f in enumerate(flips)]
        for cp in sends:
            cp.start()
        all_s[4 * x + 2 * y + c] = v_ref[...]
        for k, f in enumerate(flips):
            copy(k, f, peer(f)).wait_recv()
        for cp in sends:
            cp.wait_send()
        acc = all_s[0]
        for d in range(1, 8):
            acc = acc + all_s[d]
        o_ref[...] = acc

    return pl.pallas_call(
        body, name=name,
        in_specs=[pl.BlockSpec(memory_space=pltpu.VMEM)],
        out_specs=pl.BlockSpec(memory_space=pltpu.VMEM),
        out_shape=_sds((rows, LANE), F32),
        scratch_shapes=[pltpu.VMEM((8, rows, LANE), F32), pltpu.SemaphoreType.DMA((7,)), pltpu.SemaphoreType.DMA((7,))],
    )(v)


BIG_NAMES = ("ffn1_w_gate", "ffn1_w_up", "ffn1_w_down", "w_in", "w_branch", "w_merge_gate", "w_out",
             "ffn2_w_gate", "ffn2_w_up", "ffn2_w_down")


FFN1 = ("ffn1_w_gate", "ffn1_w_up", "ffn1_w_down")
FFN2 = ("ffn2_w_gate", "ffn2_w_up", "ffn2_w_down")
MIX_IN = ("w_in", "w_merge_gate")
MIX_OUT = ("w_branch", "w_out")


def _keys(names, l):
    return [(n, l) for n in names]


def _local_step(x, tgt, small, convw_full, biases, wx, n_layers):
    T, D = x.shape
    L = n_layers
    ns = N_SHARD
    dq = D // ns
    W = wx.w

    def hosted(call, keys, scatter=False):
        comm = wx.pieces(keys, scatter)
        main, extra = call(comm)
        if comm is not None:
            wx.arrived(keys, extra, scatter)
        return main

    def mixer_views(l):
        return _build_wbig(W[("w_merge_gate", l)], W[("w_in", l)], f"wbig_{l}")

    def out_views(l):
        wb4 = W[("w_branch", l)]
        wb = _copy_blocks(wb4, pl.BlockSpec((None, None, BRANCH_W, dq), lambda s_, i: (s_, i, 0, 0)),
                          _sds((3, BRANCH_W, D), wb4.dtype),
                          pl.BlockSpec((None, BRANCH_W, dq), lambda s_, i: (i, 0, s_)), (ns, 3), f"w_branch_whole_{l}")
        wo = W[("w_out", l)].reshape(D, D)
        return wb, wo

    tb = _ret_tables(T)

    saved = []
    h = x
    for l in range(L):
        s = {"x0": h}
        nxt = l + 1
        x1, s["g1"], s["u1"] = hosted(
            lambda c: _ffn_fwd(h, small["ffn1_norm"][l][None], W[("ffn1_w_gate", l)], W[("ffn1_w_up", l)],
                               W[("ffn1_w_down", l)], f"ffn1_fwd_{l}", comm=c), _keys(MIX_IN, l))
        s["x1"] = x1
        s["wbig"] = mixer_views(l)
        big, s["h"] = _inproj_fwd(x1, small["mix_norm"][l][None], s["wbig"], f"inproj_fwd_{l}")
        s["big"] = big
        s["bias"] = biases[l]
        s["yc"] = _conv_fwd(big, convw_full[l], f"conv_fwd_{l}")
        s["yr"], s["o"], s["st"] = hosted(lambda c: _ret_fwd(big, tb, f"ret_fwd_{l}", comm=c), _keys(MIX_OUT, l))
        (s["ya"],) = hosted(lambda c: _att_fwd(big, s["bias"], f"att_fwd_{l}", comm=c), _keys(FFN2, l))
        s["wb"], s["wo"] = out_views(l)
        x2, s["p"], s["mg"] = _merge_fwd(x1, big, (s["yc"], s["yr"], s["ya"]), s["wb"], s["wo"], f"merge_fwd_{l}")
        s["x2"] = x2
        h, s["g2"], s["u2"] = hosted(
            lambda c: _ffn_fwd(x2, small["ffn2_norm"][l][None], W[("ffn2_w_gate", l)], W[("ffn2_w_up", l)],
                               W[("ffn2_w_down", l)], f"ffn2_fwd_{l}", comm=c), _keys(FFN1, nxt) if nxt < L else [])
        saved.append(s)

    loss_p, dx, d_final = _loss_head(h, tgt, small["final_norm"][None], "loss_head")

    gs = {"final_norm": d_final[0]}
    for k in ("ffn1_norm", "mix_norm", "ffn2_norm", "rel_bias", "conv_w"):
        gs[k] = [None] * L
    tk = min(2048, T)
    nk = T // tk

    def ffn_back(pre, l, dxo, x_in, g, u, first_keys, second_keys, between=None):
        nw = small[pre + "_norm"][l][None]
        dgv, duv, av, hb, dacc = hosted(
            lambda c: _ffn_bwd_hidden(dxo, x_in, nw, g, u, W[(pre + "_w_down", l)], f"{pre}_bwd_hidden_{l}", comm=c),
            first_keys, scatter=True)
        parts = (hb, dgv, duv, av, dacc)
        if between is not None:
            second_keys = between(parts)
        dxn, dn = hosted(
            lambda c: _ffn_bwd_resid(dgv, duv, W[(pre + "_w_gate", l)], W[(pre + "_w_up", l)], x_in, nw, dxo,
                                     f"{pre}_bwd_resid_{l}", comm=c),
            second_keys, scatter=True)
        gs[pre + "_norm"][l] = dn[0]
        return dxn, parts

    def ffn_grads(pre, l, hb, dgv, duv, av, dacc, chain=False):
        fs = dgv.shape[-1]
        tkf = min(2 * tk, T)
        hspec = pl.BlockSpec((tkf, D), lambda p, q, k: (k, 0))
        sspec = pl.BlockSpec((None, tkf, fs), lambda p, q, k: (p, k, 0))
        down_spec = pl.BlockSpec((None, fs, D), lambda p, q, k: (p, 0, 0))
        jobs = [(pre + "_w_gate", dgv, hb, sspec, hspec, (ns, fs, D), down_spec),
                (pre + "_w_up", duv, hb, sspec, hspec, (ns, fs, D), down_spec),
                (pre + "_w_down", av, dacc, sspec, hspec, (ns, fs, D), down_spec)]
        for idx, (nm, a, b, a_spec, b_spec, shape, o_spec) in enumerate(jobs):
            def product(c):
                r = _tn(a, b, a_spec, b_spec, _sds(shape, BF), o_spec, (ns, 1, T // tkf), f"d{nm}_{l}", comm=c)
                return (r, []) if c is None else r
            keys = [(jobs[0][0], l)] if chain and idx == 2 else []
            wx.g[(nm, l)] = hosted(product, keys, scatter=True)
        return [(jobs[1][0], l), (jobs[2][0], l)] if chain else []

    for l in reversed(range(L)):
        s = saved[l]
        above = _keys(FFN1, l + 1) if l + 1 < L else []
        dx, parts = ffn_back("ffn2", l, dx, s["x2"], s["g2"], s["u2"], above[:1], above[1:])
        ffn_grads("ffn2", l, *parts)
        dp, dbig, dyc, dyr, dya, dxb = _merge_bwd(dx, s["big"], s["p"], s["wb"], s["wo"], f"merge_bwd_{l}")
        wx.g[("w_out", l)] = _tn(
            s["mg"], dxb, pl.BlockSpec((tk, dq), lambda p, q, k: (k, p)), pl.BlockSpec((tk, D), lambda p, q, k: (k, 0)),
            _sds((ns, dq, D), BF), pl.BlockSpec((None, dq, D), lambda p, q, k: (p, 0, 0)), (ns, 1, nk), f"dw_out_{l}")
        gb = None
        for i, yv in enumerate((s["yc"], s["yr"], s["ya"])):
            gb = _tn(yv, dp,
                     pl.BlockSpec((tk, BRANCH_W), lambda p, q, k: (k, 0)),
                     pl.BlockSpec((tk, dq), lambda p, q, k, i=i: (k, i * ns + p)),
                     _sds((ns, 3, BRANCH_W, dq), BF),
                     pl.BlockSpec((None, None, BRANCH_W, dq), lambda p, q, k, i=i: (p, i, 0, 0)),
                     (ns, 1, nk), f"dw_branch{i}_{l}", prev=gb)
        wx.g[("w_branch", l)] = gb
        dbig, dcw = _conv_bwd(s["big"], dyc, convw_full[l], dbig, f"conv_bwd_{l}")
        gs["conv_w"][l] = dcw
        dbig = _ret_bwd(s["big"], s["o"], s["st"], dyr, tb, dbig, f"ret_bwd_{l}")
        dbig, dbias = hosted(lambda c: _att_bwd(s["big"], s["bias"], dya, dbig, f"att_bwd_{l}", comm=c),
                             _keys(FFN2, l), scatter=True)
        gs["rel_bias"][l] = _relbias_grad(jnp.transpose(dbias, (1, 0, 2)), f"relbias_grad_{l}")[:, :N_REL]
        n_in = N_SEG * BRANCH_W
        bn = 1024 if (3 * D) % 1024 == 0 else BRANCH_W
        dwp = _tn(s["h"], dbig, pl.BlockSpec((tk, D), lambda p, q, k: (k, 0)),
                  pl.BlockSpec((tk, bn), lambda p, q, k: (k, 3 * D // bn + q)),
                  _sds((D, n_in), BF), pl.BlockSpec((D, bn), lambda p, q, k: (0, q)), (1, n_in // bn, nk), f"dw_in_{l}")
        wx.g[("w_in", l)] = _ungroup_dw_in(dwp, ns, f"dw_in_shards_{l}")
        wx.g[("w_merge_gate", l)] = _tn_gates(s["h"], dbig, ns, tk, f"dw_merge_gate_{l}")
        dx, dn = hosted(
            lambda c: _inproj_bwd(dbig, s["wbig"], s["x1"], small["mix_norm"][l][None], dx, f"inproj_bwd_{l}", comm=c),
            [("w_in", l)], scatter=True)
        gs["mix_norm"][l] = dn[0]
        rest = [("w_merge_gate", l), ("w_branch", l), ("w_out", l)]
        if l == 0:
            dx, _ = ffn_back("ffn1", l, dx, s["x0"], s["g1"], s["u1"], rest, [],
                             between=lambda parts: ffn_grads("ffn1", 0, *parts, chain=True))
        else:
            dx, parts = ffn_back("ffn1", l, dx, s["x0"], s["g1"], s["u1"], rest, [])
            ffn_grads("ffn1", l, *parts)

    for k in ("ffn1_norm", "mix_norm", "ffn2_norm", "rel_bias", "conv_w"):
        gs[k] = jnp.stack(gs[k])
    return loss_p, dx, gs


class _Exchange:
    def __init__(self, shards):
        self.shards = shards
        self.w = {}
        self.g = {}
        self.landed = {}

    def own(self, key):
        return self.shards[key[0]][key[1]].astype(BF)

    def pieces(self, keys, scatter):
        if not keys:
            return None
        if scatter:
            return _Scatter([self.g[k] for k in keys])
        return _HalfGather([_halves(self.own(k)) for k in keys])

    def arrived(self, keys, outs, scatter):
        for k, o in zip(keys, outs):
            if scatter:
                self.landed[k] = o
            else:
                self.w[k] = o.reshape((N_SHARD,) + self.shards[k[0]].shape[1:])


def _halves(a):
    return a.reshape(2, -1, a.shape[-1])


TRANSPOSED_GRADS = ("ffn1_w_gate", "ffn1_w_up", "ffn2_w_gate", "ffn2_w_up")
W_NAMES = ("ffn1_norm", "ffn1_w_gate", "ffn1_w_up", "ffn1_w_down", "mix_norm", "w_in", "conv_w", "rel_bias", "w_branch",
           "w_merge_gate", "w_out", "ffn2_norm", "ffn2_w_gate", "ffn2_w_up", "ffn2_w_down", "final_norm")


def _as2d(a):
    return a.reshape(1, -1) if a.ndim == 1 else a.reshape(-1, a.shape[-1])


def kernel(x, ffn1_norm, ffn1_w_gate, ffn1_w_up, ffn1_w_down, mix_norm, w_in, conv_w, rel_bias, w_branch, w_merge_gate, w_out, ffn2_norm, ffn2_w_gate, ffn2_w_up, ffn2_w_down, final_norm, loss_target, m_ffn1_norm, m_ffn1_w_gate, m_ffn1_w_up, m_ffn1_w_down, m_mix_norm, m_w_in, m_conv_w, m_rel_bias, m_w_branch, m_w_merge_gate, m_w_out, m_ffn2_norm, m_ffn2_w_gate, m_ffn2_w_up, m_ffn2_w_down, m_final_norm, v_ffn1_norm, v_ffn1_w_gate, v_ffn1_w_up, v_ffn1_w_down, v_mix_norm, v_w_in, v_conv_w, v_rel_bias, v_w_branch, v_w_merge_gate, v_w_out, v_ffn2_norm, v_ffn2_w_gate, v_ffn2_w_up, v_ffn2_w_down, v_final_norm):
    given = dict(locals())
    w = {n: given[n] for n in W_NAMES}
    m = {n: given["m_" + n] for n in W_NAMES}
    v = {n: given["v_" + n] for n in W_NAMES}
    my_chip = 2 * lax.axis_index("x") + lax.axis_index("y")
    L = w_in.shape[0]

    wx = _Exchange({n: jnp.swapaxes(w[n], 1, 2) if n in TRANSPOSED_GRADS else w[n] for n in BIG_NAMES})
    first = _keys(FFN1, 0)
    biases, got = _relbias_expand(
        rel_bias, "relbias_expand", comm=_HalfGather([_halves(wx.own(k)) for k in first] + [_halves(conv_w)]))
    wx.arrived(first, got[:-1], False)
    convw_full = jnp.transpose(got[-1].reshape((N_SHARD,) + conv_w.shape), (1, 2, 0, 3)).reshape(
        conv_w.shape[0], conv_w.shape[1], -1)

    small = {n: w[n] for n in ("ffn1_norm", "mix_norm", "ffn2_norm", "final_norm")}
    loss_p, grad_x, gs = _local_step(x[0], loss_target[0], small, convw_full, biases, wx, L)

    sums = []
    for n in BIG_NAMES:
        acc = None
        for l in range(L):
            a = wx.landed[(n, l)]
            acc, _ = _sum8(a.reshape(a.shape[0], -1, a.shape[-1]), l, L, f"sum8_{n}_{l}", prev=acc)
        sums.append(acc.reshape(-1, acc.shape[-1]))

    parts = [gs["ffn1_norm"].reshape(-1), gs["mix_norm"].reshape(-1), gs["ffn2_norm"].reshape(-1),
             gs["final_norm"].reshape(-1), gs["rel_bias"].reshape(-1), gs["conv_w"].reshape(-1), loss_p[0]]
    sizes = [p.shape[0] for p in parts]
    flat = jnp.concatenate(parts)
    rows = -(-flat.shape[0] // (8 * LANE)) * 8
    flat = jnp.pad(flat, (0, rows * LANE - flat.shape[0])).reshape(rows, LANE)
    red = _allreduce_small(flat, "allreduce_small").reshape(-1)
    offs = [0]
    for sz in sizes:
        offs.append(offs[-1] + sz)
    sm = {}
    for i, n in enumerate(("ffn1_norm", "mix_norm", "ffn2_norm", "final_norm", "rel_bias", "conv_w")):
        sm[n] = red[offs[i]:offs[i + 1]]
    loss = red[offs[6]]
    sm["conv_w"] = lax.dynamic_slice_in_dim(sm["conv_w"].reshape(conv_w.shape[0], conv_w.shape[1], -1),
                                            my_chip * conv_w.shape[2], conv_w.shape[2], axis=2)

    grads, deltas, new_m, new_v = {}, {}, {}, {}
    big_sum = dict(zip(BIG_NAMES, sums))
    for n in W_NAMES:
        flip = n in TRANSPOSED_GRADS

        def view(a):
            return jnp.swapaxes(a, 1, 2) if flip else a

        shape = view(w[n]).shape
        g = big_sum[n] if n in big_sum else _as2d(sm[n].reshape(shape))
        out = _adamw(_as2d(view(w[n])), g, _as2d(view(m[n])), _as2d(view(v[n])), f"adamw_{n}")
        grads[n], deltas[n], new_m[n], new_v[n] = (view(o.reshape(shape)) for o in [g] + list(out))

    return (loss, grad_x[None], *[grads[n] for n in W_NAMES], *[deltas[n] for n in W_NAMES],
            *[new_m[n] for n in W_NAMES], *[new_v[n] for n in W_NAMES])
```

```python
import functools
import math

import jax
import jax.numpy as jnp
from jax import lax
from jax.experimental import pallas as pl
from jax.experimental.pallas import tpu as pltpu

F32 = jnp.float32
BF = jnp.bfloat16
MESH = pl.DeviceIdType.MESH
ARB = "arbitrary"
PAR = "parallel"

EPS = 1e-6
NEG_INF = -1e30
ROPE_BASE = 10000.0
CHUNK = 64
BRANCH_W = 512
H_RET = 4
DK_RET = 128
H_ATT = 8
DH_ATT = 64
N_PREV = 8
REL_CLIP = 128
N_REL = 2 * REL_CLIP + 1
N_SHARD = 4
LANE = 128
RET_L = 512
ATT_TQ = 128
ATT_SUB = 4
ATT_PAD = N_PREV * CHUNK
ATT_SPAN = ATT_TQ + ATT_PAD
ATT_TOEP = 2 * REL_CLIP
RB_PAD = 264
TM = 512
TM_FFN = 1024

ADAM_LR = 0.001
ADAM_B1 = 0.9
ADAM_B2 = 0.999
ADAM_EPS = 1e-08
ADAM_WD = 0.01
ADAM_STEP = 10

NT_DIMS = (((1,), (1,)), ((), ()))
TN_DIMS = (((0,), (0,)), ((), ()))


def _cp(sem, vmem_mb=48):
    return pltpu.CompilerParams(dimension_semantics=sem, vmem_limit_bytes=vmem_mb << 20)


def _sds(shape, dtype):
    return jax.ShapeDtypeStruct(tuple(shape), dtype)


def _rms_r(x):
    return lax.rsqrt(jnp.mean(x * x, axis=-1, keepdims=True) + EPS)


def _sigmoid(x):
    return 0.5 * jnp.tanh(0.5 * x) + 0.5


def _rms_bwd(dh, xv, nw):
    r = _rms_r(xv)
    xh = xv * r
    dxh = dh * nw
    dx = r * (dxh - xh * jnp.mean(dxh * xh, axis=-1, keepdims=True))
    return dx, jnp.sum(dh * xh, axis=0, keepdims=True)


def _place():
    return lax.axis_index("x"), lax.axis_index("y"), lax.axis_index("c")


def _other_chips(x, y):
    return [(1 - x, y), (x, 1 - y), (1 - x, 1 - y)]


class _Scatter:
    def __init__(self, srcs):
        self.srcs = list(srcs)
        n = len(self.srcs)
        self.out_shape = [_sds((2 * N_SHARD,) + s.shape[1:], s.dtype) for s in self.srcs]
        self.scratch = [pltpu.SemaphoreType.DMA((n,)), pltpu.SemaphoreType.DMA((3, n)), pltpu.SemaphoreType.DMA((3, n)),
                        pltpu.SemaphoreType.DMA((4, n)), pltpu.SemaphoreType.DMA((4, n))]

    def _plan(self, src, dst, sems, want):
        lsem, s1, r1, s2, r2 = sems
        x, y, c = _place()
        mine = 2 * x + y
        n = len(src)
        chips = list(enumerate(_other_chips(x, y)))

        def copy(s_ref, d_ref, ssem, rsem, to):
            return pltpu.make_async_remote_copy(src_ref=s_ref, dst_ref=d_ref, send_sem=ssem, recv_sem=rsem,
                                                device_id=to, device_id_type=MESH)

        local = [pltpu.make_async_copy(src[k].at[mine], dst[k].at[3], lsem.at[k]) for k in range(n)
                 ] if "local" in want else []
        sends = [copy(src[k].at[2 * ch[0] + ch[1]], dst[k].at[j], s1.at[j, k], r1.at[j, k], (ch[0], ch[1], c))
                 for j, ch in chips for k in range(n)] if "sends" in want else []
        passes = [copy(dst[k].at[j], dst[k].at[4 + j], s2.at[j, k], r2.at[j, k], (x, y, 1 - c))
                  for j, ch in chips for k in range(n)] if "passes" in want else []
        own_pass = [copy(src[k].at[mine], dst[k].at[7], s2.at[3, k], r2.at[3, k], (x, y, 1 - c))
                    for k in range(n)] if "own_pass" in want else []
        return local, sends, passes, own_pass

    def start(self, src, dst, sems):
        local, sends, _, own_pass = self._plan(src, dst, sems, ("local", "sends", "own_pass"))
        for cp in local + sends + own_pass:
            cp.start()

    def relay(self, src, dst, sems):
        _, sends, passes, _ = self._plan(src, dst, sems, ("sends", "passes"))
        for land, fwd in zip(sends, passes):
            land.wait_recv()
            fwd.start()

    def finish(self, src, dst, sems):
        local, sends, passes, own_pass = self._plan(src, dst, sems, ("local", "sends", "passes", "own_pass"))
        for cp in passes + own_pass:
            cp.wait_recv()
        for cp in sends + passes + own_pass:
            cp.wait_send()
        for cp in local:
            cp.wait()

    def wait(self, src, dst, sems):
        self.relay(src, dst, sems)
        self.finish(src, dst, sems)


class _HalfGather:
    def __init__(self, srcs):
        self.srcs = list(srcs)
        n = len(self.srcs)
        self.out_shape = [_sds((N_SHARD,) + s.shape, s.dtype) for s in self.srcs]
        self.scratch = [pltpu.SemaphoreType.DMA((n,))] + [pltpu.SemaphoreType.DMA((3, n)) for _ in range(4)]

    def _plan(self, src, dst, sems, want):
        lsem, s1, r1, s2, r2 = sems
        x, y, c = _place()
        mine = 2 * x + y
        n = len(src)
        chips = [(j, ch, 2 * ch[0] + ch[1]) for j, ch in enumerate(_other_chips(x, y))]

        def copy(s_ref, d_ref, ssem, rsem, to):
            return pltpu.make_async_remote_copy(src_ref=s_ref, dst_ref=d_ref, send_sem=ssem, recv_sem=rsem,
                                                device_id=to, device_id_type=MESH)

        def over(kind, make):
            return [make(j, ch, slot, k) for j, ch, slot in chips for k in range(n)] if kind in want else []

        local = [pltpu.make_async_copy(src[k], dst[k].at[mine], lsem.at[k]) for k in range(n)] if "local" in want else []
        sends = over("sends", lambda j, ch, slot, k: copy(src[k].at[c], dst[k].at[mine, c], s1.at[j, k], r1.at[j, k],
                                                          (ch[0], ch[1], c)))
        lands = over("lands", lambda j, ch, slot, k: copy(src[k].at[c], dst[k].at[slot, c], s1.at[j, k], r1.at[j, k],
                                                          (ch[0], ch[1], c)))
        passes = over("passes", lambda j, ch, slot, k: copy(dst[k].at[slot, c], dst[k].at[slot, c], s2.at[j, k],
                                                            r2.at[j, k], (x, y, 1 - c)))
        gets = over("gets", lambda j, ch, slot, k: copy(dst[k].at[slot, 1 - c], dst[k].at[slot, 1 - c], s2.at[j, k],
                                                        r2.at[j, k], (x, y, 1 - c)))
        return local, sends, lands, passes, gets

    def start(self, src, dst, sems):
        lsem, s1, r1, s2, r2 = sems
        x, y, c = _place()
        mine = 2 * x + y
        for k in range(len(src)):
            pltpu.make_async_copy(src[k], dst[k].at[mine], lsem.at[k]).start()
        for j, ch in enumerate(_other_chips(x, y)):
            for k in range(len(src)):
                pltpu.make_async_remote_copy(
                    src_ref=src[k].at[c], dst_ref=dst[k].at[mine, c], send_sem=s1.at[j, k], recv_sem=r1.at[j, k],
                    device_id=(ch[0], ch[1], c), device_id_type=MESH).start()

    def relay(self, src, dst, sems):
        _, _, lands, passes, _ = self._plan(src, dst, sems, ("lands", "passes"))
        for land, fwd in zip(lands, passes):
            land.wait_recv()
            fwd.start()

    def finish(self, src, dst, sems):
        local, sends, _, passes, gets = self._plan(src, dst, sems, ("local", "sends", "passes", "gets"))
        for cp in gets:
            cp.wait_recv()
        for cp in sends + passes:
            cp.wait_send()
        for cp in local:
            cp.wait()

    def wait(self, src, dst, sems):
        self.relay(src, dst, sems)
        self.finish(src, dst, sems)


def _call(body, *, name, args, in_specs, out_specs, out_shape, grid=(), scratch_shapes=(), sem=None, comm=None,
          aliases=None, vmem_mb=48):
    in_specs, out_specs, out_shape = list(in_specs), list(out_specs), list(out_shape)
    scratch, args = list(scratch_shapes), list(args)
    n_in, n_out, n_scr = len(in_specs), len(out_specs), len(scratch)
    if comm is None:
        def kernel_body(*refs):
            body(*refs)
    else:
        c_in, c_out = len(comm.srcs), len(comm.out_shape)

        def kernel_body(*refs):
            o0 = n_in + c_in
            s0 = o0 + n_out + c_out
            cin, cout, sems = refs[n_in:o0], refs[o0 + n_out:s0], refs[s0 + n_scr:]
            main = refs[:n_in] + refs[o0:o0 + n_out] + refs[s0:s0 + n_scr]
            if grid:
                ids = [pl.program_id(a) for a in range(len(grid))]
                first = functools.reduce(lambda p, q: p & q, [i == 0 for i in ids])
                last = functools.reduce(lambda p, q: p & q, [i == g - 1 for i, g in zip(ids, grid)])

                @pl.when(first)
                def _():
                    comm.start(cin, cout, sems)

                body(*main)

                steps = math.prod(grid)
                if hasattr(comm, "relay") and steps >= 4:
                    flat = functools.reduce(lambda p, q: p + q, [i * math.prod(grid[a + 1:]) for a, i in enumerate(ids)])

                    @pl.when(flat == (5 * steps) // 6)
                    def _():
                        comm.relay(cin, cout, sems)

                    @pl.when(last)
                    def _():
                        comm.finish(cin, cout, sems)
                else:
                    @pl.when(last)
                    def _():
                        comm.wait(cin, cout, sems)
            else:
                comm.start(cin, cout, sems)
                body(*main)
                comm.wait(cin, cout, sems)

        hbm = pl.BlockSpec(memory_space=pl.ANY)
        in_specs += [hbm] * c_in
        out_specs += [hbm] * c_out
        out_shape += comm.out_shape
        scratch += comm.scratch
        args += comm.srcs
    params = dict(vmem_limit_bytes=vmem_mb << 20)
    if grid:
        params["dimension_semantics"] = sem
    outs = pl.pallas_call(
        kernel_body, name=name, grid=grid, in_specs=in_specs, out_specs=out_specs, out_shape=out_shape,
        scratch_shapes=scratch, input_output_aliases=aliases or {}, compiler_params=pltpu.CompilerParams(**params),
    )(*args)
    return list(outs[:n_out]), list(outs[n_out:])


def _ffn_fwd(x, nw, wg, wu, wd, name, comm=None):
    T, D = x.shape
    ns, fs, _ = wg.shape
    tm = min(TM_FFN, T)

    def body(x_ref, nw_ref, wg_ref, wu_ref, wd_ref, xo_ref, g_ref, u_ref, h_s, acc_s):
        j = pl.program_id(1)

        @pl.when(j == 0)
        def _():
            xv = x_ref[...]
            h_s[...] = (xv * _rms_r(xv) * nw_ref[...]).astype(BF)
            acc_s[...] = jnp.zeros_like(acc_s)

        h = h_s[...]
        gb = lax.dot_general(h, wg_ref[...], NT_DIMS, preferred_element_type=F32).astype(BF)
        ub = lax.dot_general(h, wu_ref[...], NT_DIMS, preferred_element_type=F32).astype(BF)
        g_ref[...] = gb
        u_ref[...] = ub
        g = gb.astype(F32)
        a = (g * _sigmoid(g) * ub.astype(F32)).astype(BF)
        acc_s[...] += jnp.dot(a, wd_ref[...], preferred_element_type=F32)

        @pl.when(j == ns - 1)
        def _():
            xo_ref[...] = x_ref[...] + 0.5 * acc_s[...]

    wspec = pl.BlockSpec((None, fs, D), lambda i, j: (j, 0, 0))
    return _call(
        body, name=name, grid=(T // tm, ns), args=(x, nw, wg, wu, wd), comm=comm, vmem_mb=56,
        in_specs=[pl.BlockSpec((tm, D), lambda i, j: (i, 0)),
                  pl.BlockSpec((1, D), lambda i, j: (0, 0)),
                  wspec, wspec,
                  pl.BlockSpec((None, fs, D), lambda i, j: (j, 0, 0))],
        out_specs=[pl.BlockSpec((tm, D), lambda i, j: (i, 0)),
                   pl.BlockSpec((None, tm, fs), lambda i, j: (j, i, 0)),
                   pl.BlockSpec((None, tm, fs), lambda i, j: (j, i, 0))],
        out_shape=[_sds((T, D), F32), _sds((ns, T, fs), BF), _sds((ns, T, fs), BF)],
        scratch_shapes=[pltpu.VMEM((tm, D), BF), pltpu.VMEM((tm, D), F32)],
        sem=(ARB, ARB))


def _ffn_bwd_hidden(dxo, x, nw, g, u, wd, name, comm=None):
    T, D = x.shape
    ns, fs, _ = wd.shape
    tm = min(TM_FFN, T)

    def body(dxo_ref, x_ref, nw_ref, g_ref, u_ref, wd_ref, dg_ref, du_ref, a_ref, h_ref, dacc_ref, dacc_s):
        @pl.when(pl.program_id(1) == 0)
        def _():
            xv = x_ref[...]
            h_ref[...] = (xv * _rms_r(xv) * nw_ref[...]).astype(BF)
            db = (0.5 * dxo_ref[...]).astype(BF)
            dacc_ref[...] = db
            dacc_s[...] = db

        da = lax.dot_general(dacc_s[...], wd_ref[...], NT_DIMS, preferred_element_type=F32)
        gv = g_ref[...].astype(F32)
        uv = u_ref[...].astype(F32)
        s = _sigmoid(gv)
        sg = gv * s
        a_ref[...] = (sg * uv).astype(BF)
        du_ref[...] = (da * sg).astype(BF)
        dg_ref[...] = (da * uv * (s * (1.0 + gv * (1.0 - s)))).astype(BF)

    tok = pl.BlockSpec((tm, D), lambda i, j: (i, 0))
    hid = pl.BlockSpec((None, tm, fs), lambda i, j: (j, i, 0))
    return _call(
        body, name=name, grid=(T // tm, ns), args=(dxo, x, nw, g, u, wd), comm=comm, vmem_mb=56,
        in_specs=[tok, tok, pl.BlockSpec((1, D), lambda i, j: (0, 0)), hid, hid,
                  pl.BlockSpec((None, fs, D), lambda i, j: (j, 0, 0))],
        out_specs=[hid, hid, hid, tok, tok],
        out_shape=[_sds((ns, T, fs), BF)] * 3 + [_sds((T, D), BF)] * 2,
        scratch_shapes=[pltpu.VMEM((tm, D), BF)],
        sem=(ARB, ARB))


def _ffn_bwd_resid(dg, du, wg, wu, x, nw, dxo, name, comm=None):
    T, D = x.shape
    ns, fs, _ = wg.shape
    tm = min(TM_FFN, T)

    def body(dg_ref, du_ref, wg_ref, wu_ref, x_ref, nw_ref, dxo_ref, dx_ref, dnw_ref, acc_s):
        i = pl.program_id(0)
        j = pl.program_id(1)
        prod = (jnp.dot(dg_ref[...], wg_ref[...], preferred_element_type=F32)
                + jnp.dot(du_ref[...], wu_ref[...], preferred_element_type=F32))

        @pl.when((i == 0) & (j == 0))
        def _():
            dnw_ref[...] = jnp.zeros_like(dnw_ref)

        @pl.when(j == 0)
        def _():
            acc_s[...] = prod

        @pl.when(j > 0)
        def _():
            acc_s[...] += prod

        @pl.when(j == ns - 1)
        def _():
            dx, dn = _rms_bwd(acc_s[...], x_ref[...], nw_ref[...])
            dx_ref[...] = dxo_ref[...] + dx
            dnw_ref[...] += dn

    tok = pl.BlockSpec((tm, D), lambda i, j: (i, 0))
    row = pl.BlockSpec((1, D), lambda i, j: (0, 0))
    hid = pl.BlockSpec((None, tm, fs), lambda i, j: (j, i, 0))
    wspec = pl.BlockSpec((None, fs, D), lambda i, j: (j, 0, 0))
    return _call(
        body, name=name, grid=(T // tm, ns), args=(dg, du, wg, wu, x, nw, dxo), comm=comm, vmem_mb=56,
        in_specs=[hid, hid, wspec, wspec, tok, row, tok],
        out_specs=[tok, row],
        out_shape=[_sds((T, D), F32), _sds((1, D), F32)],
        scratch_shapes=[pltpu.VMEM((tm, D), F32)],
        sem=(ARB, ARB))


def _tn(a, b, a_spec, b_spec, out_shape, out_spec, grid, name, prev=None, comm=None):
    nk = grid[-1]
    acc_shape = tuple(d for d in out_spec.block_shape if d is not None)

    def body(*refs):
        a_ref, b_ref = refs[0], refs[1]
        o_ref, acc = refs[-2], refs[-1]
        k = pl.program_id(2)
        prod = lax.dot_general(a_ref[...], b_ref[...], TN_DIMS, preferred_element_type=F32)

        @pl.when(k == 0)
        def _():
            acc[...] = prod

        @pl.when(k > 0)
        def _():
            acc[...] += prod

        @pl.when(k == nk - 1)
        def _():
            o_ref[...] = acc[...].astype(o_ref.dtype)

    in_specs = [a_spec, b_spec]
    args = [a, b]
    aliases = {}
    if prev is not None:
        in_specs.append(pl.BlockSpec(memory_space=pl.ANY))
        args.append(prev)
        aliases = {2: 0}
    main, extra = _call(
        body, name=name, grid=grid, args=args, in_specs=in_specs, out_specs=[out_spec], out_shape=[out_shape],
        scratch_shapes=[pltpu.VMEM(acc_shape, F32)], aliases=aliases, sem=(ARB, ARB, ARB), comm=comm)
    return main[0] if comm is None else (main[0], extra)


def _tn_gates(h, dbig, ns, tk, name):
    T, D = h.shape
    dq = D // ns
    nk = T // tk

    def body(a_ref, b_ref, o_ref, acc):
        k = pl.program_id(1)
        prod = lax.dot_general(a_ref[...], b_ref[...], TN_DIMS, preferred_element_type=F32)

        @pl.when(k == 0)
        def _():
            acc[...] = prod

        @pl.when(k > 0)
        def _():
            acc[...] += prod

        @pl.when(k == nk - 1)
        def _():
            for s in range(ns):
                o_ref[s] = acc[s * dq:(s + 1) * dq, :].astype(o_ref.dtype)

    return pl.pallas_call(
        body, name=name, grid=(3, nk),
        in_specs=[pl.BlockSpec((tk, D), lambda q, k: (k, 0)), pl.BlockSpec((tk, D), lambda q, k: (k, q))],
        out_specs=pl.BlockSpec((ns, None, dq, D), lambda q, k: (0, q, 0, 0)),
        out_shape=_sds((ns, 3, dq, D), BF),
        scratch_shapes=[pltpu.VMEM((D, D), F32)],
        compiler_params=_cp((PAR, ARB)),
    )(h, dbig)


def _inproj_fwd(x, nw, wbig, name):
    T, D = x.shape
    nb = wbig.shape[-1]
    tm = min(2 * TM, T)
    bn = min(2048, nb)

    def body(x_ref, nw_ref, w_ref, o_ref, h_ref, h_s):
        @pl.when(pl.program_id(1) == 0)
        def _():
            xv = x_ref[...]
            hb = (xv * _rms_r(xv) * nw_ref[...]).astype(BF)
            h_s[...] = hb
            h_ref[...] = hb

        o_ref[...] = jnp.dot(h_s[...], w_ref[...], preferred_element_type=F32).astype(BF)

    return pl.pallas_call(
        body, name=name, grid=(T // tm, nb // bn),
        in_specs=[pl.BlockSpec((tm, D), lambda i, n: (i, 0)),
                  pl.BlockSpec((1, D), lambda i, n: (0, 0)),
                  pl.BlockSpec((D, bn), lambda i, n: (0, n))],
        out_specs=[pl.BlockSpec((tm, bn), lambda i, n: (i, n)),
                   pl.BlockSpec((tm, D), lambda i, n: (i, 0))],
        out_shape=[_sds((T, nb), BF), _sds((T, D), BF)],
        scratch_shapes=[pltpu.VMEM((tm, D), BF)],
        compiler_params=_cp((PAR, ARB)),
    )(x, nw, wbig)


def _inproj_bwd(dbig, wbig, x, nw, dxin, name, comm=None):
    T, D = x.shape
    nb = wbig.shape[-1]
    tm = min(TM_FFN, T)
    tk = min(2048, nb)
    nk = nb // tk

    def body(a_ref, w_ref, x_ref, nw_ref, dxin_ref, dx_ref, dnw_ref, acc_s):
        i = pl.program_id(0)
        k = pl.program_id(1)
        prod = lax.dot_general(a_ref[...], w_ref[...], NT_DIMS, preferred_element_type=F32)

        @pl.when((i == 0) & (k == 0))
        def _():
            dnw_ref[...] = jnp.zeros_like(dnw_ref)

        @pl.when(k == 0)
        def _():
            acc_s[...] = prod

        @pl.when(k > 0)
        def _():
            acc_s[...] += prod

        @pl.when(k == nk - 1)
        def _():
            dx, dn = _rms_bwd(acc_s[...], x_ref[...], nw_ref[...])
            dx_ref[...] = dxin_ref[...] + dx
            dnw_ref[...] += dn

    tok = pl.BlockSpec((tm, D), lambda i, k: (i, 0))
    row = pl.BlockSpec((1, D), lambda i, k: (0, 0))
    return _call(
        body, name=name, grid=(T // tm, nk), args=(dbig, wbig, x, nw, dxin), comm=comm, vmem_mb=56,
        in_specs=[pl.BlockSpec((tm, tk), lambda i, k: (i, k)),
                  pl.BlockSpec((D, tk), lambda i, k: (0, k)),
                  tok, row, tok],
        out_specs=[tok, row],
        out_shape=[_sds((T, D), F32), _sds((1, D), F32)],
        scratch_shapes=[pltpu.VMEM((tm, D), F32)],
        sem=(ARB, ARB))


CONV_R = 512
CONV_BASE, CONV_GROUP = 0, 3
ATT_BASE, ATT_GROUP = 12, 3
RET_BASE, RET_GROUP = 24, 4
N_SEG = 10


N_IN_BLOCKS = N_SEG * BRANCH_W // LANE


def _orig_block(p):
    nblk = BRANCH_W // LANE
    qa, qr = p - ATT_BASE, p - RET_BASE
    conv = (p % CONV_GROUP) * nblk + p // CONV_GROUP
    att = (7 + qa % ATT_GROUP) * nblk + qa // ATT_GROUP
    ret = (3 + qr % RET_GROUP) * nblk + qr // RET_GROUP
    return jnp.where(p < ATT_BASE, conv, jnp.where(p < RET_BASE, att, ret))


def _copy_blocks(src, in_spec, out_shape, out_spec, grid, name, prev=None):
    def body(*refs):
        refs[-1][...] = refs[0][...]

    in_specs, args, aliases = [in_spec], [src], {}
    if prev is not None:
        in_specs.append(pl.BlockSpec(memory_space=pl.ANY))
        args.append(prev)
        aliases = {1: 0}
    return pl.pallas_call(
        body, name=name, grid=grid, in_specs=in_specs, out_specs=out_spec, out_shape=out_shape,
        input_output_aliases=aliases, compiler_params=_cp(tuple(PAR for _ in grid)),
    )(*args)


def _build_wbig(gates4, win4, name):
    ns, _, dq, D = gates4.shape
    per = win4.shape[-1] // LANE
    shape = _sds((D, 3 * D + N_IN_BLOCKS * LANE), gates4.dtype)
    out = _copy_blocks(gates4, pl.BlockSpec((None, None, dq, D), lambda s, i: (s, i, 0, 0)), shape,
                       pl.BlockSpec((dq, D), lambda s, i: (s, i)), (ns, 3), name + "_gates")
    return _copy_blocks(
        win4, pl.BlockSpec((None, D, LANE), lambda p: (_orig_block(p) // per, 0, _orig_block(p) % per)), shape,
        pl.BlockSpec((D, LANE), lambda p: (0, 3 * D // LANE + p)), (N_IN_BLOCKS,), name + "_in", prev=out)


def _ungroup_dw_in(dwp, ns, name):
    D = dwp.shape[0]
    per = N_IN_BLOCKS // ns
    return _copy_blocks(
        dwp, pl.BlockSpec((D, LANE), lambda p: (0, p)), _sds((ns, D, per * LANE), dwp.dtype),
        pl.BlockSpec((None, D, LANE), lambda p: (_orig_block(p) // per, 0, _orig_block(p) % per)), (N_IN_BLOCKS,), name)


def _seg0(big):
    return (big.shape[1] - N_SEG * BRANCH_W) // LANE


def _group_spec(big, base, group, rows, where):
    first = (_seg0(big) + base) // group
    assert first * group == _seg0(big) + base

    def index(*ids):
        r, g = where(*ids)
        return r, first + g

    return pl.BlockSpec((rows, group * LANE), index)


CU, CB, CC = (slice(k * LANE, (k + 1) * LANE) for k in range(3))
AQ, AK, AV = CU, CB, CC
RQ, RK, RV, RG = (slice(k * LANE, (k + 1) * LANE) for k in range(4))


def _conv_fwd(big, cw, name):
    T = big.shape[0]
    R = min(CONV_R, T)

    def body(g_ref, w_ref, y_ref, z_s):
        z_s[pl.ds(0, 8), :] = jnp.zeros((8, LANE), F32)

        def fill(t, c):
            sl = pl.ds(pl.multiple_of(t * R, R), R)
            z_s[pl.ds(pl.multiple_of(t * R + 8, 8), R), :] = g_ref[sl, CC].astype(F32) * g_ref[sl, CU].astype(F32)
            return c

        lax.fori_loop(0, T // R, fill, 0)
        w0, w1, w2 = w_ref[0:1, :], w_ref[1:2, :], w_ref[2:3, :]

        def step(t, c):
            zz = z_s[pl.ds(pl.multiple_of(t * R, R), R + 8), :]
            z0 = zz[8:]
            z1 = pltpu.roll(zz, 1, 0)[8:]
            z2 = pltpu.roll(zz, 2, 0)[8:]
            sl = pl.ds(pl.multiple_of(t * R, R), R)
            y_ref[sl, :] = (g_ref[sl, CB].astype(F32) * (w2 * z0 + w1 * z1 + w0 * z2)).astype(BF)
            return c

        lax.fori_loop(0, T // R, step, 0)

    return pl.pallas_call(
        body, name=name, grid=(BRANCH_W // LANE,),
        in_specs=[_group_spec(big, CONV_BASE, CONV_GROUP, T, lambda j: (0, j)),
                  pl.BlockSpec((3, LANE), lambda j: (0, j))],
        out_specs=pl.BlockSpec((T, LANE), lambda j: (0, j)),
        out_shape=_sds((T, BRANCH_W), BF),
        scratch_shapes=[pltpu.VMEM((T + 8, LANE), F32)],
        compiler_params=_cp((PAR,)),
    )(big, cw)


def _conv_bwd(big, dy, cw, dbig, name):
    T = big.shape[0]
    R = min(CONV_R, T)

    def body(g_ref, dy_ref, w_ref, _, o_ref, dw_ref, z_s, d_s):
        z_s[pl.ds(0, 8), :] = jnp.zeros((8, LANE), F32)
        d_s[pl.ds(T, 8), :] = jnp.zeros((8, LANE), F32)

        def fill(t, c):
            sl = pl.ds(pl.multiple_of(t * R, R), R)
            z_s[pl.ds(pl.multiple_of(t * R + 8, 8), R), :] = g_ref[sl, CC].astype(F32) * g_ref[sl, CU].astype(F32)
            d_s[sl, :] = dy_ref[sl, :].astype(F32) * g_ref[sl, CB].astype(F32)
            return c

        lax.fori_loop(0, T // R, fill, 0)
        w0, w1, w2 = w_ref[0:1, :], w_ref[1:2, :], w_ref[2:3, :]

        def step(t, carry):
            a0, a1, a2 = carry
            zz = z_s[pl.ds(pl.multiple_of(t * R, R), R + 8), :]
            z0 = zz[8:]
            z1 = pltpu.roll(zz, 1, 0)[8:]
            z2 = pltpu.roll(zz, 2, 0)[8:]
            sl = pl.ds(pl.multiple_of(t * R, R), R)
            dyv = dy_ref[sl, :].astype(F32)
            o_ref[sl, CB] = (dyv * (w2 * z0 + w1 * z1 + w0 * z2)).astype(BF)
            dd = d_s[pl.ds(pl.multiple_of(t * R, R), R + 8), :]
            d0 = dd[:R]
            d1 = pltpu.roll(dd, R + 7, 0)[:R]
            d2 = pltpu.roll(dd, R + 6, 0)[:R]
            dz = w2 * d0 + w1 * d1 + w0 * d2
            o_ref[sl, CC] = (dz * g_ref[sl, CU].astype(F32)).astype(BF)
            o_ref[sl, CU] = (dz * g_ref[sl, CC].astype(F32)).astype(BF)
            a0 = a0 + jnp.sum(d0 * z2, axis=0, keepdims=True)
            a1 = a1 + jnp.sum(d0 * z1, axis=0, keepdims=True)
            a2 = a2 + jnp.sum(d0 * z0, axis=0, keepdims=True)
            return a0, a1, a2

        zero = jnp.zeros((1, LANE), F32)
        a0, a1, a2 = lax.fori_loop(0, T // R, step, (zero, zero, zero))
        dw_ref[0:1, :] = a0
        dw_ref[1:2, :] = a1
        dw_ref[2:3, :] = a2

    group = _group_spec(big, CONV_BASE, CONV_GROUP, T, lambda j: (0, j))
    w = pl.BlockSpec((3, LANE), lambda j: (0, j))
    return pl.pallas_call(
        body, name=name, grid=(BRANCH_W // LANE,),
        in_specs=[group, pl.BlockSpec((T, LANE), lambda j: (0, j)), w, pl.BlockSpec(memory_space=pl.ANY)],
        out_specs=[group, w],
        out_shape=[_sds(dbig.shape, BF), _sds((3, BRANCH_W), F32)],
        scratch_shapes=[pltpu.VMEM((T + 8, LANE), F32), pltpu.VMEM((T + 8, LANE), F32)],
        input_output_aliases={3: 0}, compiler_params=_cp((PAR,)),
    )(big, dy, cw, dbig)


def _ret_tables(T):
    L = min(RET_L, T)
    hh = jnp.arange(H_RET, dtype=F32)
    lg = jnp.log1p(-jnp.exp2(-5.0 - hh))
    n = jnp.arange(L, dtype=F32)
    a = jnp.exp(lg[:, None] * (n + 1.0))
    b = jnp.exp(lg[:, None] * (L - 1.0 - n))
    gl = jnp.exp(lg * L)
    ch = jnp.arange(L) // CHUNK
    m = jnp.exp(lg[:, None, None] * jnp.abs(n[:, None] - n[None, :])) * (ch[None, :] <= ch[:, None]).astype(F32)
    inv_freq = ROPE_BASE ** (-jnp.linspace(0.0, 1.0, DK_RET // 2, dtype=F32))
    ang = jnp.arange(T, dtype=F32)[:, None] * inv_freq[None, :]
    cos, sin = jnp.cos(ang), jnp.sin(ang)
    return dict(
        L=L, M=m,
        a=jnp.broadcast_to(a[:, :, None], (H_RET, L, DK_RET)),
        b=jnp.broadcast_to(b[:, :, None], (H_RET, L, DK_RET)),
        gl=jnp.broadcast_to(gl[:, None, None], (H_RET, 1, DK_RET)),
        cos=jnp.concatenate([cos, cos], axis=-1), sin=jnp.concatenate([-sin, sin], axis=-1))


def _rot(x, cs, sn):
    return x * cs + pltpu.roll(x, DK_RET // 2, 1) * sn


def _unrot(dy, cs, sn):
    return dy * cs + pltpu.roll(dy * sn, DK_RET // 2, 1)


def _ret_fwd(big, tb, name, comm=None):
    T = big.shape[0]
    L = tb["L"]
    nsc = T // L
    scale = DK_RET ** -0.5

    def body(x_ref, cos_ref, sin_ref, m_ref, a_ref, b_ref, gl_ref, y_ref, o_ref, st_ref, s_s):
        @pl.when(pl.program_id(1) == 0)
        def _():
            s_s[...] = jnp.zeros_like(s_s)

        cs, sn = cos_ref[...], sin_ref[...]
        qt = _rot(x_ref[:, RQ].astype(F32), cs, sn) * scale
        kt = _rot(x_ref[:, RK].astype(F32), cs, sn)
        qb, kb, vb = qt.astype(BF), kt.astype(BF), x_ref[:, RV]
        s_prev = s_s[...]
        st_ref[...] = s_prev
        p = lax.dot_general(qb, kb, NT_DIMS, preferred_element_type=F32) * m_ref[...]
        o = (jnp.dot(p.astype(BF), vb, preferred_element_type=F32)
             + jnp.dot((qt * a_ref[...]).astype(BF), s_prev.astype(BF), preferred_element_type=F32))
        s_s[...] = s_prev * gl_ref[...] + lax.dot_general((kt * b_ref[...]).astype(BF), vb, TN_DIMS,
                                                         preferred_element_type=F32)
        o_ref[...] = o
        gv = x_ref[:, RG].astype(F32)
        y_ref[...] = (gv * _sigmoid(gv) * o * _rms_r(o)).astype(BF)

    tab = pl.BlockSpec((L, DK_RET), lambda h, i: (i, 0))
    per_head = pl.BlockSpec((None, L, DK_RET), lambda h, i: (h, 0, 0))
    out = pl.BlockSpec((L, LANE), lambda h, i: (i, h))
    return _call(
        body, name=name, grid=(H_RET, nsc), comm=comm,
        args=(big, tb["cos"], tb["sin"], tb["M"], tb["a"], tb["b"], tb["gl"]),
        in_specs=[_group_spec(big, RET_BASE, RET_GROUP, L, lambda h, i: (i, h)), tab, tab,
                  pl.BlockSpec((None, L, L), lambda h, i: (h, 0, 0)), per_head, per_head,
                  pl.BlockSpec((None, 1, DK_RET), lambda h, i: (h, 0, 0))],
        out_specs=[out, out, pl.BlockSpec((None, None, DK_RET, DK_RET), lambda h, i: (i, h, 0, 0))],
        out_shape=[_sds((T, BRANCH_W), BF), _sds((T, BRANCH_W), F32), _sds((nsc, H_RET, DK_RET, DK_RET), F32)],
        scratch_shapes=[pltpu.VMEM((DK_RET, DK_RET), F32)],
        sem=(ARB, ARB))


def _ret_bwd(big, o, st, dy, tb, dbig, name):
    T = big.shape[0]
    L = tb["L"]
    nsc = T // L
    scale = DK_RET ** -0.5

    def body(x_ref, cos_ref, sin_ref, m_ref, a_ref, b_ref, gl_ref, o_ref, st_ref, dy_ref, _, d_ref, ds_s):
        @pl.when(pl.program_id(1) == 0)
        def _():
            ds_s[...] = jnp.zeros_like(ds_s)

        cs, sn = cos_ref[...], sin_ref[...]
        mm, av, bv = m_ref[...], a_ref[...], b_ref[...]
        qt = _rot(x_ref[:, RQ].astype(F32), cs, sn) * scale
        kt = _rot(x_ref[:, RK].astype(F32), cs, sn)
        qb, kb, vb = qt.astype(BF), kt.astype(BF), x_ref[:, RV]
        pb = (lax.dot_general(qb, kb, NT_DIMS, preferred_element_type=F32) * mm).astype(BF)
        ov = o_ref[...]
        r = _rms_r(ov)
        oh = ov * r
        gv = x_ref[:, RG].astype(F32)
        sg = _sigmoid(gv)
        dyv = dy_ref[...].astype(F32)
        d_ref[:, RG] = (dyv * oh * (sg * (1.0 + gv * (1.0 - sg)))).astype(BF)
        doh = dyv * gv * sg
        dob = (r * (doh - oh * jnp.mean(doh * oh, axis=-1, keepdims=True))).astype(BF)
        dsb = ds_s[...].astype(BF)
        spb = st_ref[...].astype(BF)
        dpb = (lax.dot_general(dob, vb, NT_DIMS, preferred_element_type=F32) * mm).astype(BF)
        dqt = (jnp.dot(dpb, kb, preferred_element_type=F32)
               + lax.dot_general(dob, spb, NT_DIMS, preferred_element_type=F32) * av)
        dkt = (lax.dot_general(dpb, qb, TN_DIMS, preferred_element_type=F32)
               + lax.dot_general(vb, dsb, NT_DIMS, preferred_element_type=F32) * bv)
        dv = (lax.dot_general(pb, dob, TN_DIMS, preferred_element_type=F32)
              + jnp.dot((kt * bv).astype(BF), dsb, preferred_element_type=F32))
        ds_s[...] = ds_s[...] * gl_ref[...] + lax.dot_general((qt * av).astype(BF), dob, TN_DIMS,
                                                              preferred_element_type=F32)
        d_ref[:, RQ] = (_unrot(dqt, cs, sn) * scale).astype(BF)
        d_ref[:, RK] = _unrot(dkt, cs, sn).astype(BF)
        d_ref[:, RV] = dv.astype(BF)

    def rev(i):
        return nsc - 1 - i

    group = _group_spec(big, RET_BASE, RET_GROUP, L, lambda h, i: (rev(i), h))
    tab = pl.BlockSpec((L, DK_RET), lambda h, i: (rev(i), 0))
    per_head = pl.BlockSpec((None, L, DK_RET), lambda h, i: (h, 0, 0))
    out = pl.BlockSpec((L, LANE), lambda h, i: (rev(i), h))
    return pl.pallas_call(
        body, name=name, grid=(H_RET, nsc),
        in_specs=[group, tab, tab,
                  pl.BlockSpec((None, L, L), lambda h, i: (h, 0, 0)), per_head, per_head,
                  pl.BlockSpec((None, 1, DK_RET), lambda h, i: (h, 0, 0)),
                  out, pl.BlockSpec((None, None, DK_RET, DK_RET), lambda h, i: (rev(i), h, 0, 0)), out,
                  pl.BlockSpec(memory_space=pl.ANY)],
        out_specs=group,
        out_shape=_sds(dbig.shape, BF),
        scratch_shapes=[pltpu.VMEM((DK_RET, DK_RET), F32)],
        input_output_aliases={10: 0}, compiler_params=_cp((PAR, ARB)),
    )(big, tb["cos"], tb["sin"], tb["M"], tb["a"], tb["b"], tb["gl"], o, st, dy, dbig)


def _relbias_onehot(n):
    mm = lax.broadcasted_iota(jnp.int32, (RB_PAD, ATT_TOEP), 1)
    rr = lax.broadcasted_iota(jnp.int32, (RB_PAD, ATT_TOEP), 0)
    idx = jnp.clip(n + ATT_TOEP - mm, 0, 2 * REL_CLIP)
    return (rr == idx).astype(F32)


def _split3(x):
    hi = x.astype(BF).astype(F32)
    mid = (x - hi).astype(BF).astype(F32)
    lo = x - hi - mid
    return jnp.concatenate([hi, mid, lo], axis=0).astype(BF)


def _join3(y):
    k = y.shape[0] // 3
    return (y[:k] + y[k:2 * k]) + y[2 * k:]


def _relbias_expand(rel_bias, name, comm=None):
    far = ATT_SPAN - ATT_TOEP
    n_layers = rel_bias.shape[0]
    rbp = jnp.pad(rel_bias, ((0, 0), (0, 0), (0, RB_PAD - N_REL)))

    def body(rb_ref, o_ref):
        for l in range(n_layers):
            rb = rb_ref[l]
            const = jnp.broadcast_to(rb[:, 2 * REL_CLIP:2 * REL_CLIP + 1], (H_ATT, far))
            rb3 = _split3(rb)

            def row(n, c):
                toep = _join3(jnp.dot(rb3, _relbias_onehot(n).astype(BF), preferred_element_type=F32))
                m = lax.broadcasted_iota(jnp.int32, (1, ATT_SPAN), 1)
                d = n // CHUNK + N_PREV - m // CHUNK
                neg = jnp.where((d >= 0) & (d <= N_PREV), 0.0, NEG_INF).astype(F32)
                o_ref[l, n] = jnp.concatenate([const, toep], axis=1) + neg
                return c

            lax.fori_loop(0, ATT_TQ, row, 0)

    (out,), extra = _call(
        body, name=name, args=(rbp,), comm=comm,
        in_specs=[pl.BlockSpec(memory_space=pltpu.VMEM)],
        out_specs=[pl.BlockSpec(memory_space=pltpu.VMEM)],
        out_shape=[_sds((n_layers, ATT_TQ, H_ATT, ATT_SPAN), F32)])
    return jnp.transpose(out, (0, 2, 1, 3)), extra


def _relbias_grad(dbt, name):
    far = ATT_SPAN - ATT_TOEP

    def body(d_ref, o_ref):
        def row(n, carry):
            acc, cs = carry
            dn = d_ref[n]
            acc = acc + _join3(lax.dot_general(_split3(dn[:, far:]), _relbias_onehot(n).astype(BF), NT_DIMS,
                                               preferred_element_type=F32))
            cs = cs + jnp.sum(dn[:, :far], axis=1, keepdims=True)
            return acc, cs

        acc, cs = lax.fori_loop(0, ATT_TQ, row, (jnp.zeros((H_ATT, RB_PAD), F32), jnp.zeros((H_ATT, 1), F32)))
        rr = lax.broadcasted_iota(jnp.int32, (H_ATT, RB_PAD), 1)
        o_ref[...] = acc + jnp.where(rr == 2 * REL_CLIP, cs, 0.0)

    return pl.pallas_call(
        body, name=name,
        in_specs=[pl.BlockSpec(memory_space=pltpu.VMEM)],
        out_specs=pl.BlockSpec(memory_space=pltpu.VMEM),
        out_shape=_sds((H_ATT, RB_PAD), F32),
    )(dbt)


def _att_pad_fill(dst_s, src_ref, cols, T):
    dst_s[pl.ds(0, ATT_PAD), :] = jnp.zeros((ATT_PAD, LANE), dst_s.dtype)
    R = min(512, T)

    def cp(t, c):
        dst_s[pl.ds(pl.multiple_of(ATT_PAD + t * R, LANE), R), :] = src_ref[pl.ds(pl.multiple_of(t * R, R), R), cols]
        return c

    lax.fori_loop(0, T // R, cp, 0)


ATT_WIN = ATT_SUB * ATT_TQ + ATT_PAD


def _att_probs(s_full, sub, bias, t0):
    s = s_full[sub * ATT_TQ:(sub + 1) * ATT_TQ, sub * ATT_TQ:sub * ATT_TQ + ATT_SPAN] * (DH_ATT ** -0.5) + bias
    key_pos = t0 + sub * ATT_TQ - ATT_PAD + lax.broadcasted_iota(jnp.int32, (1, ATT_SPAN), 1)
    s = jnp.where(key_pos >= 0, s, NEG_INF)
    p = jnp.exp(s - jnp.max(s, axis=-1, keepdims=True))
    return p * (1.0 / jnp.sum(p, axis=-1, keepdims=True))


def _att_band(tiles):
    rows = []
    for sub, t in enumerate(tiles):
        parts = []
        if sub:
            parts.append(jnp.zeros((ATT_TQ, sub * ATT_TQ), BF))
        parts.append(t)
        if sub < ATT_SUB - 1:
            parts.append(jnp.zeros((ATT_TQ, (ATT_SUB - 1 - sub) * ATT_TQ), BF))
        rows.append(jnp.concatenate(parts, axis=1))
    return jnp.concatenate(rows, axis=0)


def _att_head_masks(x):
    first = lax.broadcasted_iota(jnp.int32, (1, LANE), 1) < DH_ATT
    zero = jnp.zeros_like(x)
    return first, (jnp.where(first, x, zero), jnp.where(first, zero, x))


def _att_fwd(big, bias, name, comm=None):
    T = big.shape[0]
    rows = ATT_SUB * ATT_TQ
    nt = T // rows

    def body(x_ref, b_ref, y_ref, kp_s, vp_s):
        i = pl.program_id(1)

        @pl.when(i == 0)
        def _():
            _att_pad_fill(kp_s, x_ref, AK, T)
            _att_pad_fill(vp_s, x_ref, AV, T)

        t0 = pl.multiple_of(i * rows, rows)
        kw = kp_s[pl.ds(t0, ATT_WIN), :]
        vw = vp_s[pl.ds(t0, ATT_WIN), :]
        first, qm = _att_head_masks(x_ref[pl.ds(t0, rows), AQ])
        outs = []
        for hh in range(2):
            s_full = lax.dot_general(qm[hh], kw, NT_DIMS, preferred_element_type=F32)
            band = _att_band([_att_probs(s_full, sub, b_ref[hh], t0).astype(BF) for sub in range(ATT_SUB)])
            outs.append(jnp.dot(band, vw, preferred_element_type=F32))
        y_ref[...] = jnp.where(first, outs[0], outs[1]).astype(BF)

    return _call(
        body, name=name, grid=(H_ATT // 2, nt), args=(big, bias), comm=comm,
        in_specs=[_group_spec(big, ATT_BASE, ATT_GROUP, T, lambda p, i: (0, p)),
                  pl.BlockSpec((2, ATT_TQ, ATT_SPAN), lambda p, i: (p, 0, 0))],
        out_specs=[pl.BlockSpec((rows, LANE), lambda p, i: (i, p))],
        out_shape=[_sds((T, BRANCH_W), BF)],
        scratch_shapes=[pltpu.VMEM((T + ATT_PAD, LANE), BF), pltpu.VMEM((T + ATT_PAD, LANE), BF)],
        sem=(ARB, ARB))


def _att_bwd(big, bias, dy, dbig, name, comm=None):
    T = big.shape[0]
    rows = ATT_SUB * ATT_TQ
    nt = T // rows
    scale = DH_ATT ** -0.5

    def body(x_ref, b_ref, dy_ref, _, d_ref, db_ref, kp_s, vp_s, dk_s, dv_s):
        i = pl.program_id(1)

        @pl.when(i == 0)
        def _():
            _att_pad_fill(kp_s, x_ref, AK, T)
            _att_pad_fill(vp_s, x_ref, AV, T)
            dk_s[...] = jnp.zeros_like(dk_s)
            dv_s[...] = jnp.zeros_like(dv_s)
            db_ref[...] = jnp.zeros_like(db_ref)

        t0 = pl.multiple_of(i * rows, rows)
        win = pl.ds(t0, ATT_WIN)
        kw = kp_s[win, :]
        vw = vp_s[win, :]
        first, qm = _att_head_masks(x_ref[pl.ds(t0, rows), AQ])
        _, dom = _att_head_masks(dy_ref[...])
        dqs, dkt, dvt = [], None, None
        for hh in range(2):
            s_full = lax.dot_general(qm[hh], kw, NT_DIMS, preferred_element_type=F32)
            dp_full = lax.dot_general(dom[hh], vw, NT_DIMS, preferred_element_type=F32)
            ps, dss, db = [], [], None
            for sub in range(ATT_SUB):
                pn = _att_probs(s_full, sub, b_ref[hh], t0)
                dp = dp_full[sub * ATT_TQ:(sub + 1) * ATT_TQ, sub * ATT_TQ:sub * ATT_TQ + ATT_SPAN]
                ds = pn * (dp - jnp.sum(dp * pn, axis=-1, keepdims=True))
                db = ds if db is None else db + ds
                ps.append(pn.astype(BF))
                dss.append(ds.astype(BF))
            db_ref[hh] += db
            ds_band, p_band = _att_band(dss), _att_band(ps)
            dqs.append(jnp.dot(ds_band, kw, preferred_element_type=F32))
            qt = jnp.transpose(qm[hh].astype(F32)).astype(BF)
            dot_ = jnp.transpose(dom[hh].astype(F32)).astype(BF)
            dk_h = jnp.dot(qt, ds_band, preferred_element_type=F32)
            dv_h = jnp.dot(dot_, p_band, preferred_element_type=F32)
            dkt = dk_h if dkt is None else dkt + dk_h
            dvt = dv_h if dvt is None else dvt + dv_h
        d_ref[pl.ds(t0, rows), AQ] = (jnp.where(first, dqs[0], dqs[1]) * scale).astype(BF)
        dk_s[win, :] += jnp.transpose(dkt) * scale
        dv_s[win, :] += jnp.transpose(dvt)

        @pl.when(i == nt - 1)
        def _():
            R = min(512, T)

            def cp(t, c):
                src = pl.ds(pl.multiple_of(ATT_PAD + t * R, LANE), R)
                dst = pl.ds(pl.multiple_of(t * R, R), R)
                d_ref[dst, AK] = dk_s[src, :].astype(BF)
                d_ref[dst, AV] = dv_s[src, :].astype(BF)
                return c

            lax.fori_loop(0, T // R, cp, 0)

    group = _group_spec(big, ATT_BASE, ATT_GROUP, T, lambda p, i: (0, p))
    tile = pl.BlockSpec((rows, LANE), lambda p, i: (i, p))
    bspec = pl.BlockSpec((2, ATT_TQ, ATT_SPAN), lambda p, i: (p, 0, 0))
    return _call(
        body, name=name, grid=(H_ATT // 2, nt), args=(big, bias, dy, dbig), comm=comm, aliases={3: 0}, vmem_mb=56,
        in_specs=[group, bspec, tile, pl.BlockSpec(memory_space=pl.ANY)],
        out_specs=[group, bspec],
        out_shape=[_sds(dbig.shape, BF), _sds((H_ATT, ATT_TQ, ATT_SPAN), F32)],
        scratch_shapes=[pltpu.VMEM((T + ATT_PAD, LANE), BF), pltpu.VMEM((T + ATT_PAD, LANE), BF),
                        pltpu.VMEM((T + ATT_PAD, LANE), F32), pltpu.VMEM((T + ATT_PAD, LANE), F32)],
        sem=(ARB, ARB))


def _merge_fwd(x1, big, ys, wb, wo, name):
    T, D = x1.shape
    tm = min(TM, T)

    def body(x_ref, gp_ref, yc_ref, yr_ref, ya_ref, wb_ref, wo_ref, x2_ref, p_ref, mg_ref):
        merged = jnp.zeros((tm, D), F32)
        for i, y_ref in enumerate((yc_ref, yr_ref, ya_ref)):
            cols = slice(i * D, (i + 1) * D)
            pb = jnp.dot(y_ref[...], wb_ref[i], preferred_element_type=F32).astype(BF)
            p_ref[:, cols] = pb
            merged = merged + _sigmoid(gp_ref[:, cols].astype(F32)) * pb.astype(F32)
        mb = merged.astype(BF)
        mg_ref[...] = mb
        x2_ref[...] = x_ref[...] + jnp.dot(mb, wo_ref[...], preferred_element_type=F32)

    tok = pl.BlockSpec((tm, D), lambda i: (i, 0))
    wide = pl.BlockSpec((tm, 3 * D), lambda i: (i, 0))
    yspec = pl.BlockSpec((tm, BRANCH_W), lambda i: (i, 0))
    return pl.pallas_call(
        body, name=name, grid=(T // tm,),
        in_specs=[tok, wide, yspec, yspec, yspec,
                  pl.BlockSpec((3, BRANCH_W, D), lambda i: (0, 0, 0)),
                  pl.BlockSpec((D, D), lambda i: (0, 0))],
        out_specs=[tok, wide, tok],
        out_shape=[_sds((T, D), F32), _sds((T, 3 * D), BF), _sds((T, D), BF)],
        compiler_params=_cp((PAR,)),
    )(x1, big, *ys, wb, wo)


def _merge_bwd(dx2, big, p, wb, wo, name):
    T, D = dx2.shape
    tm = min(TM, T)

    def body(dx_ref, gp_ref, p_ref, wb_ref, wo_ref, dp_ref, dgp_ref, dyc_ref, dyr_ref, dya_ref, dxb_ref):
        dxb = dx_ref[...].astype(BF)
        dxb_ref[...] = dxb
        dm = lax.dot_general(dxb, wo_ref[...], NT_DIMS, preferred_element_type=F32)
        for i, dy_ref in enumerate((dyc_ref, dyr_ref, dya_ref)):
            cols = slice(i * D, (i + 1) * D)
            gt = _sigmoid(gp_ref[:, cols].astype(F32))
            dpb = (dm * gt).astype(BF)
            dp_ref[:, cols] = dpb
            dgp_ref[:, cols] = (dm * p_ref[:, cols].astype(F32) * gt * (1.0 - gt)).astype(BF)
            dy_ref[...] = lax.dot_general(dpb, wb_ref[i], NT_DIMS, preferred_element_type=F32).astype(BF)

    tok = pl.BlockSpec((tm, D), lambda i: (i, 0))
    wide = pl.BlockSpec((tm, 3 * D), lambda i: (i, 0))
    yspec = pl.BlockSpec((tm, BRANCH_W), lambda i: (i, 0))
    return pl.pallas_call(
        body, name=name, grid=(T // tm,),
        in_specs=[tok, wide, wide,
                  pl.BlockSpec((3, BRANCH_W, D), lambda i: (0, 0, 0)),
                  pl.BlockSpec((D, D), lambda i: (0, 0))],
        out_specs=[wide, wide, yspec, yspec, yspec, tok],
        out_shape=[_sds((T, 3 * D), BF), _sds(big.shape, BF)] + [_sds((T, BRANCH_W), BF)] * 3 + [_sds((T, D), BF)],
        compiler_params=_cp((PAR,)),
    )(dx2, big, p, wb, wo)


def _loss_head(x, tgt, fw, name):
    T, D = x.shape
    tm = min(TM, T)

    def body(x_ref, t_ref, w_ref, loss_ref, dx_ref, dw_ref):
        @pl.when(pl.program_id(0) == 0)
        def _():
            loss_ref[...] = jnp.zeros_like(loss_ref)
            dw_ref[...] = jnp.zeros_like(dw_ref)

        xv = x_ref[...]
        wv = w_ref[...]
        e = xv * _rms_r(xv) * wv - t_ref[...]
        loss_ref[...] += 0.5 * jnp.sum(jnp.mean(e * e, axis=-1, keepdims=True))
        dx, dn = _rms_bwd(e * (1.0 / D), xv, wv)
        dx_ref[...] = dx
        dw_ref[...] += dn

    tok = pl.BlockSpec((tm, D), lambda i: (i, 0))
    return pl.pallas_call(
        body, name=name, grid=(T // tm,),
        in_specs=[tok, tok, pl.BlockSpec((1, D), lambda i: (0, 0))],
        out_specs=[pl.BlockSpec((8, LANE), lambda i: (0, 0)), tok, pl.BlockSpec((1, D), lambda i: (0, 0))],
        out_shape=[_sds((8, LANE), F32), _sds((T, D), F32), _sds((1, D), F32)],
        compiler_params=_cp((ARB,)),
    )(x, tgt, fw)


def _block_rows(rows, cols):
    cap = max(8, (1 << 18) // cols)
    best = None
    for r in range(8, rows + 1, 8):
        if rows % r == 0 and r <= cap:
            best = r
    return best if best is not None else rows


def _sum8(land, l, n_layers, name, prev=None, comm=None):
    _, rows, cols = land.shape
    br = _block_rows(rows, cols)

    def body(*refs):
        l_ref, o_ref = refs[0], refs[-1]

        def four(base):
            return ((l_ref[base + 3].astype(F32) + l_ref[base].astype(F32)) + l_ref[base + 1].astype(F32)
                    ) + l_ref[base + 2].astype(F32)

        o_ref[...] = four(0) + four(4)

    in_specs = [pl.BlockSpec((2 * N_SHARD, br, cols), lambda i: (0, i, 0))]
    args = [land]
    aliases = {}
    if prev is not None:
        in_specs.append(pl.BlockSpec(memory_space=pl.ANY))
        args.append(prev)
        aliases = {1: 0}
    main, extra = _call(
        body, name=name, grid=(rows // br,), args=args, in_specs=in_specs,
        out_specs=[pl.BlockSpec((None, br, cols), lambda i: (l, i, 0))],
        out_shape=[_sds((n_layers, rows, cols), F32)], aliases=aliases, sem=(ARB,), comm=comm)
    return main[0], extra


def _adamw_math(w, g, m, v):
    m = ADAM_B1 * m + (1.0 - ADAM_B1) * g
    v = ADAM_B2 * v + (1.0 - ADAM_B2) * (g * g)
    m_hat = m / (1.0 - ADAM_B1 ** ADAM_STEP)
    v_hat = v / (1.0 - ADAM_B2 ** ADAM_STEP)
    delta = -ADAM_LR * (m_hat / (jnp.sqrt(v_hat) + ADAM_EPS) + ADAM_WD * w)
    return delta, m, v


def _adamw(w, g, m, v, name):
    rows, cols = w.shape
    br = _block_rows(rows, cols)

    def body(w_ref, g_ref, m_ref, v_ref, d_ref, nm_ref, nv_ref):
        d, nm, nv = _adamw_math(w_ref[...], g_ref[...], m_ref[...], v_ref[...])
        d_ref[...] = d
        nm_ref[...] = nm
        nv_ref[...] = nv

    blk = pl.BlockSpec((br, cols), lambda i: (i, 0))
    return pl.pallas_call(
        body, name=name, grid=(rows // br,),
        in_specs=[blk] * 4, out_specs=[blk] * 3,
        out_shape=[_sds((rows, cols), F32)] * 3,
        compiler_params=_cp((PAR,)),
    )(w, g, m, v)


def _allreduce_small(v, name):
    rows = v.shape[0]
    flips = [(fx, fy, fc) for fx in (0, 1) for fy in (0, 1) for fc in (0, 1) if fx or fy or fc]

    def body(v_ref, o_ref, all_s, ssem, rsem):
        x, y, c = _place()

        def peer(f):
            return (x + f[0] - 2 * x * f[0], y + f[1] - 2 * y * f[1], c + f[2] - 2 * c * f[2])

        def slot(p):
            return all_s.at[4 * p[0] + 2 * p[1] + p[2]]

        def copy(k, f, owner):
            return pltpu.make_async_remote_copy(
                src_ref=v_ref, dst_ref=slot(owner), send_sem=ssem.at[k], recv_sem=rsem.at[k],
                device_id=peer(f), device_id_type=MESH)

        sends = [copy(k, f, (x, y, c)) for k, f in enumerate(flips)]
        for cp in sends:
            cp.start()
        all_s[4 * x + 2 * y + c] = v_ref[...]
        for k, f in enumerate(flips):
            copy(k, f, peer(f)).wait_recv()
        for cp in sends:
            cp.wait_send()
        acc = all_s[0]
        for d in range(1, 8):
            acc = acc + all_s[d]
        o_ref[...] = acc

    return pl.pallas_call(
        body, name=name,
        in_specs=[pl.BlockSpec(memory_space=pltpu.VMEM)],
        out_specs=pl.BlockSpec(memory_space=pltpu.VMEM),
        out_shape=_sds((rows, LANE), F32),
        scratch_shapes=[pltpu.VMEM((8, rows, LANE), F32), pltpu.SemaphoreType.DMA((7,)), pltpu.SemaphoreType.DMA((7,))],
    )(v)


BIG_NAMES = ("ffn1_w_gate", "ffn1_w_up", "ffn1_w_down", "w_in", "w_branch", "w_merge_gate", "w_out",
             "ffn2_w_gate", "ffn2_w_up", "ffn2_w_down")


FFN1 = ("ffn1_w_gate", "ffn1_w_up", "ffn1_w_down")
FFN2 = ("ffn2_w_gate", "ffn2_w_up", "ffn2_w_down")
MIX_IN = ("w_in", "w_merge_gate")
MIX_OUT = ("w_branch", "w_out")


def _keys(names, l):
    return [(n, l) for n in names]


def _local_step(x, tgt, small, convw_full, biases, wx, n_layers):
    T, D = x.shape
    L = n_layers
    ns = N_SHARD
    dq = D // ns
    W = wx.w

    def hosted(call, keys, scatter=False):
        comm = wx.pieces(keys, scatter)
        main, extra = call(comm)
        if comm is not None:
            wx.arrived(keys, extra, scatter)
        return main

    def mixer_views(l):
        return _build_wbig(W[("w_merge_gate", l)], W[("w_in", l)], f"wbig_{l}")

    def out_views(l):
        wb4 = W[("w_branch", l)]
        wb = _copy_blocks(wb4, pl.BlockSpec((None, None, BRANCH_W, dq), lambda s_, i: (s_, i, 0, 0)),
                          _sds((3, BRANCH_W, D), wb4.dtype),
                          pl.BlockSpec((None, BRANCH_W, dq), lambda s_, i: (i, 0, s_)), (ns, 3), f"w_branch_whole_{l}")
        wo = W[("w_out", l)].reshape(D, D)
        return wb, wo

    tb = _ret_tables(T)

    saved = []
    h = x
    for l in range(L):
        s = {"x0": h}
        nxt = l + 1
        x1, s["g1"], s["u1"] = hosted(
            lambda c: _ffn_fwd(h, small["ffn1_norm"][l][None], W[("ffn1_w_gate", l)], W[("ffn1_w_up", l)],
                               W[("ffn1_w_down", l)], f"ffn1_fwd_{l}", comm=c), _keys(MIX_IN + MIX_OUT, l))
        s["x1"] = x1
        s["wbig"] = mixer_views(l)
        big, s["h"] = _inproj_fwd(x1, small["mix_norm"][l][None], s["wbig"], f"inproj_fwd_{l}")
        s["big"] = big
        s["bias"] = biases[l]
        s["yc"] = _conv_fwd(big, convw_full[l], f"conv_fwd_{l}")
        s["yr"], s["o"], s["st"] = hosted(lambda c: _ret_fwd(big, tb, f"ret_fwd_{l}", comm=c), [])
        (s["ya"],) = hosted(lambda c: _att_fwd(big, s["bias"], f"att_fwd_{l}", comm=c), _keys(FFN2, l))
        s["wb"], s["wo"] = out_views(l)
        x2, s["p"], s["mg"] = _merge_fwd(x1, big, (s["yc"], s["yr"], s["ya"]), s["wb"], s["wo"], f"merge_fwd_{l}")
        s["x2"] = x2
        h, s["g2"], s["u2"] = hosted(
            lambda c: _ffn_fwd(x2, small["ffn2_norm"][l][None], W[("ffn2_w_gate", l)], W[("ffn2_w_up", l)],
                               W[("ffn2_w_down", l)], f"ffn2_fwd_{l}", comm=c), _keys(FFN1, nxt) if nxt < L else [])
        saved.append(s)

    loss_p, dx, d_final = _loss_head(h, tgt, small["final_norm"][None], "loss_head")

    gs = {"final_norm": d_final[0]}
    for k in ("ffn1_norm", "mix_norm", "ffn2_norm", "rel_bias", "conv_w"):
        gs[k] = [None] * L
    tk = min(2048, T)
    nk = T // tk

    def ffn_back(pre, l, dxo, x_in, g, u, first_keys, second_keys, between=None):
        nw = small[pre + "_norm"][l][None]
        dgv, duv, av, hb, dacc = hosted(
            lambda c: _ffn_bwd_hidden(dxo, x_in, nw, g, u, W[(pre + "_w_down", l)], f"{pre}_bwd_hidden_{l}", comm=c),
            first_keys, scatter=True)
        parts = (hb, dgv, duv, av, dacc)
        if between is not None:
            second_keys = between(parts)
        dxn, dn = hosted(
            lambda c: _ffn_bwd_resid(dgv, duv, W[(pre + "_w_gate", l)], W[(pre + "_w_up", l)], x_in, nw, dxo,
                                     f"{pre}_bwd_resid_{l}", comm=c),
            second_keys, scatter=True)
        gs[pre + "_norm"][l] = dn[0]
        return dxn, parts

    def ffn_grads(pre, l, hb, dgv, duv, av, dacc, chain=False):
        fs = dgv.shape[-1]
        tkf = min(2 * tk, T)
        hspec = pl.BlockSpec((tkf, D), lambda p, q, k: (k, 0))
        sspec = pl.BlockSpec((None, tkf, fs), lambda p, q, k: (p, k, 0))
        down_spec = pl.BlockSpec((None, fs, D), lambda p, q, k: (p, 0, 0))
        jobs = [(pre + "_w_gate", dgv, hb, sspec, hspec, (ns, fs, D), down_spec),
                (pre + "_w_up", duv, hb, sspec, hspec, (ns, fs, D), down_spec),
                (pre + "_w_down", av, dacc, sspec, hspec, (ns, fs, D), down_spec)]
        for idx, (nm, a, b, a_spec, b_spec, shape, o_spec) in enumerate(jobs):
            def product(c):
                r = _tn(a, b, a_spec, b_spec, _sds(shape, BF), o_spec, (ns, 1, T // tkf), f"d{nm}_{l}", comm=c)
                return (r, []) if c is None else r
            keys = [(jobs[0][0], l)] if chain and idx == 2 else []
            wx.g[(nm, l)] = hosted(product, keys, scatter=True)
        return [(jobs[1][0], l), (jobs[2][0], l)] if chain else []

    for l in reversed(range(L)):
        s = saved[l]
        above = _keys(FFN1, l + 1) if l + 1 < L else []
        dx, parts = ffn_back("ffn2", l, dx, s["x2"], s["g2"], s["u2"], above[:1], above[1:])
        ffn_grads("ffn2", l, *parts)
        dp, dbig, dyc, dyr, dya, dxb = _merge_bwd(dx, s["big"], s["p"], s["wb"], s["wo"], f"merge_bwd_{l}")
        wx.g[("w_out", l)] = _tn(
            s["mg"], dxb, pl.BlockSpec((tk, dq), lambda p, q, k: (k, p)), pl.BlockSpec((tk, D), lambda p, q, k: (k, 0)),
            _sds((ns, dq, D), BF), pl.BlockSpec((None, dq, D), lambda p, q, k: (p, 0, 0)), (ns, 1, nk), f"dw_out_{l}")
        gb = None
        for i, yv in enumerate((s["yc"], s["yr"], s["ya"])):
            gb = _tn(yv, dp,
                     pl.BlockSpec((tk, BRANCH_W), lambda p, q, k: (k, 0)),
                     pl.BlockSpec((tk, dq), lambda p, q, k, i=i: (k, i * ns + p)),
                     _sds((ns, 3, BRANCH_W, dq), BF),
                     pl.BlockSpec((None, None, BRANCH_W, dq), lambda p, q, k, i=i: (p, i, 0, 0)),
                     (ns, 1, nk), f"dw_branch{i}_{l}", prev=gb)
        wx.g[("w_branch", l)] = gb
        dbig, dcw = _conv_bwd(s["big"], dyc, convw_full[l], dbig, f"conv_bwd_{l}")
        gs["conv_w"][l] = dcw
        dbig = _ret_bwd(s["big"], s["o"], s["st"], dyr, tb, dbig, f"ret_bwd_{l}")
        dbig, dbias = hosted(lambda c: _att_bwd(s["big"], s["bias"], dya, dbig, f"att_bwd_{l}", comm=c),
                             _keys(FFN2, l), scatter=True)
        gs["rel_bias"][l] = _relbias_grad(jnp.transpose(dbias, (1, 0, 2)), f"relbias_grad_{l}")[:, :N_REL]
        n_in = N_SEG * BRANCH_W
        bn = 1024 if (3 * D) % 1024 == 0 else BRANCH_W
        dwp = _tn(s["h"], dbig, pl.BlockSpec((tk, D), lambda p, q, k: (k, 0)),
                  pl.BlockSpec((tk, bn), lambda p, q, k: (k, 3 * D // bn + q)),
                  _sds((D, n_in), BF), pl.BlockSpec((D, bn), lambda p, q, k: (0, q)), (1, n_in // bn, nk), f"dw_in_{l}")
        wx.g[("w_in", l)] = _ungroup_dw_in(dwp, ns, f"dw_in_shards_{l}")
        wx.g[("w_merge_gate", l)] = _tn_gates(s["h"], dbig, ns, tk, f"dw_merge_gate_{l}")
        dx, dn = hosted(
            lambda c: _inproj_bwd(dbig, s["wbig"], s["x1"], small["mix_norm"][l][None], dx, f"inproj_bwd_{l}", comm=c),
            [("w_in", l)], scatter=True)
        gs["mix_norm"][l] = dn[0]
        rest = [("w_merge_gate", l), ("w_branch", l), ("w_out", l)]
        if l == 0:
            dx, _ = ffn_back("ffn1", l, dx, s["x0"], s["g1"], s["u1"], rest, [],
                             between=lambda parts: ffn_grads("ffn1", 0, *parts, chain=True))
        else:
            dx, parts = ffn_back("ffn1", l, dx, s["x0"], s["g1"], s["u1"], rest, [])
            ffn_grads("ffn1", l, *parts)

    for k in ("ffn1_norm", "mix_norm", "ffn2_norm", "rel_bias", "conv_w"):
        gs[k] = jnp.stack(gs[k])
    return loss_p, dx, gs


class _Exchange:
    def __init__(self, shards):
        self.shards = shards
        self.w = {}
        self.g = {}
        self.landed = {}

    def own(self, key):
        return self.shards[key[0]][key[1]].astype(BF)

    def pieces(self, keys, scatter):
        if not keys:
            return None
        if scatter:
            return _Scatter([self.g[k] for k in keys])
        return _HalfGather([_halves(self.own(k)) for k in keys])

    def arrived(self, keys, outs, scatter):
        for k, o in zip(keys, outs):
            if scatter:
                self.landed[k] = o
            else:
                self.w[k] = o.reshape((N_SHARD,) + self.shards[k[0]].shape[1:])


def _halves(a):
    return a.reshape(2, -1, a.shape[-1])


TRANSPOSED_GRADS = ("ffn1_w_gate", "ffn1_w_up", "ffn2_w_gate", "ffn2_w_up")
W_NAMES = ("ffn1_norm", "ffn1_w_gate", "ffn1_w_up", "ffn1_w_down", "mix_norm", "w_in", "conv_w", "rel_bias", "w_branch",
           "w_merge_gate", "w_out", "ffn2_norm", "ffn2_w_gate", "ffn2_w_up", "ffn2_w_down", "final_norm")


def _as2d(a):
    return a.reshape(1, -1) if a.ndim == 1 else a.reshape(-1, a.shape[-1])


def kernel(x, ffn1_norm, ffn1_w_gate, ffn1_w_up, ffn1_w_down, mix_norm, w_in, conv_w, rel_bias, w_branch, w_merge_gate, w_out, ffn2_norm, ffn2_w_gate, ffn2_w_up, ffn2_w_down, final_norm, loss_target, m_ffn1_norm, m_ffn1_w_gate, m_ffn1_w_up, m_ffn1_w_down, m_mix_norm, m_w_in, m_conv_w, m_rel_bias, m_w_branch, m_w_merge_gate, m_w_out, m_ffn2_norm, m_ffn2_w_gate, m_ffn2_w_up, m_ffn2_w_down, m_final_norm, v_ffn1_norm, v_ffn1_w_gate, v_ffn1_w_up, v_ffn1_w_down, v_mix_norm, v_w_in, v_conv_w, v_rel_bias, v_w_branch, v_w_merge_gate, v_w_out, v_ffn2_norm, v_ffn2_w_gate, v_ffn2_w_up, v_ffn2_w_down, v_final_norm):
    given = dict(locals())
    w = {n: given[n] for n in W_NAMES}
    m = {n: given["m_" + n] for n in W_NAMES}
    v = {n: given["v_" + n] for n in W_NAMES}
    my_chip = 2 * lax.axis_index("x") + lax.axis_index("y")
    L = w_in.shape[0]

    wx = _Exchange({n: jnp.swapaxes(w[n], 1, 2) if n in TRANSPOSED_GRADS else w[n] for n in BIG_NAMES})
    first = _keys(FFN1, 0)
    biases, got = _relbias_expand(
        rel_bias, "relbias_expand", comm=_HalfGather([_halves(wx.own(k)) for k in first] + [_halves(conv_w)]))
    wx.arrived(first, got[:-1], False)
    convw_full = jnp.transpose(got[-1].reshape((N_SHARD,) + conv_w.shape), (1, 2, 0, 3)).reshape(
        conv_w.shape[0], conv_w.shape[1], -1)

    small = {n: w[n] for n in ("ffn1_norm", "mix_norm", "ffn2_norm", "final_norm")}
    loss_p, grad_x, gs = _local_step(x[0], loss_target[0], small, convw_full, biases, wx, L)

    sums = []
    for n in BIG_NAMES:
        acc = None
        for l in range(L):
            a = wx.landed[(n, l)]
            acc, _ = _sum8(a.reshape(a.shape[0], -1, a.shape[-1]), l, L, f"sum8_{n}_{l}", prev=acc)
        sums.append(acc.reshape(-1, acc.shape[-1]))

    parts = [gs["ffn1_norm"].reshape(-1), gs["mix_norm"].reshape(-1), gs["ffn2_norm"].reshape(-1),
             gs["final_norm"].reshape(-1), gs["rel_bias"].reshape(-1), gs["conv_w"].reshape(-1), loss_p[0]]
    sizes = [p.shape[0] for p in parts]
    flat = jnp.concatenate(parts)
    rows = -(-flat.shape[0] // (8 * LANE)) * 8
    flat = jnp.pad(flat, (0, rows * LANE - flat.shape[0])).reshape(rows, LANE)
    red = _allreduce_small(flat, "allreduce_small").reshape(-1)
    offs = [0]
    for sz in sizes:
        offs.append(offs[-1] + sz)
    sm = {}
    for i, n in enumerate(("ffn1_norm", "mix_norm", "ffn2_norm", "final_norm", "rel_bias", "conv_w")):
        sm[n] = red[offs[i]:offs[i + 1]]
    loss = red[offs[6]]
    sm["conv_w"] = lax.dynamic_slice_in_dim(sm["conv_w"].reshape(conv_w.shape[0], conv_w.shape[1], -1),
                                            my_chip * conv_w.shape[2], conv_w.shape[2], axis=2)

    grads, deltas, new_m, new_v = {}, {}, {}, {}
    big_sum = dict(zip(BIG_NAMES, sums))
    for n in W_NAMES:
        flip = n in TRANSPOSED_GRADS

        def view(a):
            return jnp.swapaxes(a, 1, 2) if flip else a

        shape = view(w[n]).shape
        g = big_sum[n] if n in big_sum else _as2d(sm[n].reshape(shape))
        out = _adamw(_as2d(view(w[n])), g, _as2d(view(m[n])), _as2d(view(v[n])), f"adamw_{n}")
        grads[n], deltas[n], new_m[n], new_v[n] = (view(o.reshape(shape)) for o in [g] + list(out))

    return (loss, grad_x[None], *[grads[n] for n in W_NAMES], *[deltas[n] for n in W_NAMES],
            *[new_m[n] for n in W_NAMES], *[new_v[n] for n in W_NAMES])
```

```python
import functools
import math

import jax
import jax.numpy as jnp
from jax import lax
from jax.experimental import pallas as pl
from jax.experimental.pallas import tpu as pltpu

F32 = jnp.float32
BF = jnp.bfloat16
MESH = pl.DeviceIdType.MESH
ARB = "arbitrary"
PAR = "parallel"

EPS = 1e-6
NEG_INF = -1e30
ROPE_BASE = 10000.0
CHUNK = 64
BRANCH_W = 512
H_RET = 4
DK_RET = 128
H_ATT = 8
DH_ATT = 64
N_PREV = 8
REL_CLIP = 128
N_REL = 2 * REL_CLIP + 1
N_SHARD = 4
LANE = 128
RET_L = 512
ATT_TQ = 128
ATT_SUB = 4
ATT_PAD = N_PREV * CHUNK
ATT_SPAN = ATT_TQ + ATT_PAD
ATT_TOEP = 2 * REL_CLIP
RB_PAD = 264
TM = 512
TM_FFN = 1024

ADAM_LR = 0.001
ADAM_B1 = 0.9
ADAM_B2 = 0.999
ADAM_EPS = 1e-08
ADAM_WD = 0.01
ADAM_STEP = 10

NT_DIMS = (((1,), (1,)), ((), ()))
TN_DIMS = (((0,), (0,)), ((), ()))


def _cp(sem, vmem_mb=48):
    return pltpu.CompilerParams(dimension_semantics=sem, vmem_limit_bytes=vmem_mb << 20)


def _sds(shape, dtype):
    return jax.ShapeDtypeStruct(tuple(shape), dtype)


def _rms_r(x):
    return lax.rsqrt(jnp.mean(x * x, axis=-1, keepdims=True) + EPS)


def _sigmoid(x):
    return 0.5 * jnp.tanh(0.5 * x) + 0.5


def _rms_bwd(dh, xv, nw):
    r = _rms_r(xv)
    xh = xv * r
    dxh = dh * nw
    dx = r * (dxh - xh * jnp.mean(dxh * xh, axis=-1, keepdims=True))
    return dx, jnp.sum(dh * xh, axis=0, keepdims=True)


def _place():
    return lax.axis_index("x"), lax.axis_index("y"), lax.axis_index("c")


def _other_chips(x, y):
    return [(1 - x, y), (x, 1 - y), (1 - x, 1 - y)]


class _Scatter:
    def __init__(self, srcs):
        self.srcs = list(srcs)
        n = len(self.srcs)
        self.out_shape = [_sds((2 * N_SHARD,) + s.shape[1:], s.dtype) for s in self.srcs]
        self.scratch = [pltpu.SemaphoreType.DMA((n,)), pltpu.SemaphoreType.DMA((3, n)), pltpu.SemaphoreType.DMA((3, n)),
                        pltpu.SemaphoreType.DMA((4, n)), pltpu.SemaphoreType.DMA((4, n))]

    def _plan(self, src, dst, sems, want):
        lsem, s1, r1, s2, r2 = sems
        x, y, c = _place()
        mine = 2 * x + y
        n = len(src)
        chips = list(enumerate(_other_chips(x, y)))

        def copy(s_ref, d_ref, ssem, rsem, to):
            return pltpu.make_async_remote_copy(src_ref=s_ref, dst_ref=d_ref, send_sem=ssem, recv_sem=rsem,
                                                device_id=to, device_id_type=MESH)

        local = [pltpu.make_async_copy(src[k].at[mine], dst[k].at[3], lsem.at[k]) for k in range(n)
                 ] if "local" in want else []
        sends = [copy(src[k].at[2 * ch[0] + ch[1]], dst[k].at[j], s1.at[j, k], r1.at[j, k], (ch[0], ch[1], c))
                 for j, ch in chips for k in range(n)] if "sends" in want else []
        passes = [copy(dst[k].at[j], dst[k].at[4 + j], s2.at[j, k], r2.at[j, k], (x, y, 1 - c))
                  for j, ch in chips for k in range(n)] if "passes" in want else []
        own_pass = [copy(src[k].at[mine], dst[k].at[7], s2.at[3, k], r2.at[3, k], (x, y, 1 - c))
                    for k in range(n)] if "own_pass" in want else []
        return local, sends, passes, own_pass

    def start(self, src, dst, sems):
        local, sends, _, own_pass = self._plan(src, dst, sems, ("local", "sends", "own_pass"))
        for cp in local + sends + own_pass:
            cp.start()

    def relay(self, src, dst, sems):
        _, sends, passes, _ = self._plan(src, dst, sems, ("sends", "passes"))
        for land, fwd in zip(sends, passes):
            land.wait_recv()
            fwd.start()

    def finish(self, src, dst, sems):
        local, sends, passes, own_pass = self._plan(src, dst, sems, ("local", "sends", "passes", "own_pass"))
        for cp in passes + own_pass:
            cp.wait_recv()
        for cp in sends + passes + own_pass:
            cp.wait_send()
        for cp in local:
            cp.wait()

    def wait(self, src, dst, sems):
        self.relay(src, dst, sems)
        self.finish(src, dst, sems)


class _HalfGather:
    def __init__(self, srcs):
        self.srcs = list(srcs)
        n = len(self.srcs)
        self.out_shape = [_sds((N_SHARD,) + s.shape, s.dtype) for s in self.srcs]
        self.scratch = [pltpu.SemaphoreType.DMA((n,))] + [pltpu.SemaphoreType.DMA((3, n)) for _ in range(4)]

    def _plan(self, src, dst, sems, want):
        lsem, s1, r1, s2, r2 = sems
        x, y, c = _place()
        mine = 2 * x + y
        n = len(src)
        chips = [(j, ch, 2 * ch[0] + ch[1]) for j, ch in enumerate(_other_chips(x, y))]

        def copy(s_ref, d_ref, ssem, rsem, to):
            return pltpu.make_async_remote_copy(src_ref=s_ref, dst_ref=d_ref, send_sem=ssem, recv_sem=rsem,
                                                device_id=to, device_id_type=MESH)

        def over(kind, make):
            return [make(j, ch, slot, k) for j, ch, slot in chips for k in range(n)] if kind in want else []

        local = [pltpu.make_async_copy(src[k], dst[k].at[mine], lsem.at[k]) for k in range(n)] if "local" in want else []
        sends = over("sends", lambda j, ch, slot, k: copy(src[k].at[c], dst[k].at[mine, c], s1.at[j, k], r1.at[j, k],
                                                          (ch[0], ch[1], c)))
        lands = over("lands", lambda j, ch, slot, k: copy(src[k].at[c], dst[k].at[slot, c], s1.at[j, k], r1.at[j, k],
                                                          (ch[0], ch[1], c)))
        passes = over("passes", lambda j, ch, slot, k: copy(dst[k].at[slot, c], dst[k].at[slot, c], s2.at[j, k],
                                                            r2.at[j, k], (x, y, 1 - c)))
        gets = over("gets", lambda j, ch, slot, k: copy(dst[k].at[slot, 1 - c], dst[k].at[slot, 1 - c], s2.at[j, k],
                                                        r2.at[j, k], (x, y, 1 - c)))
        return local, sends, lands, passes, gets

    def start(self, src, dst, sems):
        lsem, s1, r1, s2, r2 = sems
        x, y, c = _place()
        mine = 2 * x + y
        for k in range(len(src)):
            pltpu.make_async_copy(src[k], dst[k].at[mine], lsem.at[k]).start()
        for j, ch in enumerate(_other_chips(x, y)):
            for k in range(len(src)):
                pltpu.make_async_remote_copy(
                    src_ref=src[k].at[c], dst_ref=dst[k].at[mine, c], send_sem=s1.at[j, k], recv_sem=r1.at[j, k],
                    device_id=(ch[0], ch[1], c), device_id_type=MESH).start()

    def relay(self, src, dst, sems):
        _, _, lands, passes, _ = self._plan(src, dst, sems, ("lands", "passes"))
        for land, fwd in zip(lands, passes):
            land.wait_recv()
            fwd.start()

    def finish(self, src, dst, sems):
        local, sends, _, passes, gets = self._plan(src, dst, sems, ("local", "sends", "passes", "gets"))
        for cp in gets:
            cp.wait_recv()
        for cp in sends + passes:
            cp.wait_send()
        for cp in local:
            cp.wait()

    def wait(self, src, dst, sems):
        self.relay(src, dst, sems)
        self.finish(src, dst, sems)


def _call(body, *, name, args, in_specs, out_specs, out_shape, grid=(), scratch_shapes=(), sem=None, comm=None,
          aliases=None, vmem_mb=48):
    in_specs, out_specs, out_shape = list(in_specs), list(out_specs), list(out_shape)
    scratch, args = list(scratch_shapes), list(args)
    n_in, n_out, n_scr = len(in_specs), len(out_specs), len(scratch)
    if comm is None:
        def kernel_body(*refs):
            body(*refs)
    else:
        c_in, c_out = len(comm.srcs), len(comm.out_shape)

        def kernel_body(*refs):
            o0 = n_in + c_in
            s0 = o0 + n_out + c_out
            cin, cout, sems = refs[n_in:o0], refs[o0 + n_out:s0], refs[s0 + n_scr:]
            main = refs[:n_in] + refs[o0:o0 + n_out] + refs[s0:s0 + n_scr]
            if grid:
                ids = [pl.program_id(a) for a in range(len(grid))]
                first = functools.reduce(lambda p, q: p & q, [i == 0 for i in ids])
                last = functools.reduce(lambda p, q: p & q, [i == g - 1 for i, g in zip(ids, grid)])

                @pl.when(first)
                def _():
                    comm.start(cin, cout, sems)

                body(*main)

                steps = math.prod(grid)
                if hasattr(comm, "relay") and steps >= 4:
                    flat = functools.reduce(lambda p, q: p + q, [i * math.prod(grid[a + 1:]) for a, i in enumerate(ids)])

                    @pl.when(flat == (5 * steps) // 6)
                    def _():
                        comm.relay(cin, cout, sems)

                    @pl.when(last)
                    def _():
                        comm.finish(cin, cout, sems)
                else:
                    @pl.when(last)
                    def _():
                        comm.wait(cin, cout, sems)
            else:
                comm.start(cin, cout, sems)
                body(*main)
                comm.wait(cin, cout, sems)

        hbm = pl.BlockSpec(memory_space=pl.ANY)
        in_specs += [hbm] * c_in
        out_specs += [hbm] * c_out
        out_shape += comm.out_shape
        scratch += comm.scratch
        args += comm.srcs
    params = dict(vmem_limit_bytes=vmem_mb << 20)
    if grid:
        params["dimension_semantics"] = sem
    outs = pl.pallas_call(
        kernel_body, name=name, grid=grid, in_specs=in_specs, out_specs=out_specs, out_shape=out_shape,
        scratch_shapes=scratch, input_output_aliases=aliases or {}, compiler_params=pltpu.CompilerParams(**params),
    )(*args)
    return list(outs[:n_out]), list(outs[n_out:])


def _ffn_fwd(x, nw, wg, wu, wd, name, comm=None):
    T, D = x.shape
    ns, fs, _ = wg.shape
    tm = min(TM_FFN, T)

    def body(x_ref, nw_ref, wg_ref, wu_ref, wd_ref, xo_ref, g_ref, u_ref, h_s, acc_s):
        j = pl.program_id(1)

        @pl.when(j == 0)
        def _():
            xv = x_ref[...]
            h_s[...] = (xv * _rms_r(xv) * nw_ref[...]).astype(BF)
            acc_s[...] = jnp.zeros_like(acc_s)

        h = h_s[...]
        gb = lax.dot_general(h, wg_ref[...], NT_DIMS, preferred_element_type=F32).astype(BF)
        ub = lax.dot_general(h, wu_ref[...], NT_DIMS, preferred_element_type=F32).astype(BF)
        g_ref[...] = gb
        u_ref[...] = ub
        g = gb.astype(F32)
        a = (g * _sigmoid(g) * ub.astype(F32)).astype(BF)
        acc_s[...] += jnp.dot(a, wd_ref[...], preferred_element_type=F32)

        @pl.when(j == ns - 1)
        def _():
            xo_ref[...] = x_ref[...] + 0.5 * acc_s[...]

    wspec = pl.BlockSpec((None, fs, D), lambda i, j: (j, 0, 0))
    return _call(
        body, name=name, grid=(T // tm, ns), args=(x, nw, wg, wu, wd), comm=comm, vmem_mb=56,
        in_specs=[pl.BlockSpec((tm, D), lambda i, j: (i, 0)),
                  pl.BlockSpec((1, D), lambda i, j: (0, 0)),
                  wspec, wspec,
                  pl.BlockSpec((None, fs, D), lambda i, j: (j, 0, 0))],
        out_specs=[pl.BlockSpec((tm, D), lambda i, j: (i, 0)),
                   pl.BlockSpec((None, tm, fs), lambda i, j: (j, i, 0)),
                   pl.BlockSpec((None, tm, fs), lambda i, j: (j, i, 0))],
        out_shape=[_sds((T, D), F32), _sds((ns, T, fs), BF), _sds((ns, T, fs), BF)],
        scratch_shapes=[pltpu.VMEM((tm, D), BF), pltpu.VMEM((tm, D), F32)],
        sem=(ARB, ARB))


def _ffn_bwd_hidden(dxo, x, nw, g, u, wd, name, comm=None):
    T, D = x.shape
    ns, fs, _ = wd.shape
    tm = min(TM_FFN, T)

    def body(dxo_ref, x_ref, nw_ref, g_ref, u_ref, wd_ref, dg_ref, du_ref, a_ref, h_ref, dacc_ref, dacc_s):
        @pl.when(pl.program_id(1) == 0)
        def _():
            xv = x_ref[...]
            h_ref[...] = (xv * _rms_r(xv) * nw_ref[...]).astype(BF)
            db = (0.5 * dxo_ref[...]).astype(BF)
            dacc_ref[...] = db
            dacc_s[...] = db

        da = lax.dot_general(dacc_s[...], wd_ref[...], NT_DIMS, preferred_element_type=F32)
        gv = g_ref[...].astype(F32)
        uv = u_ref[...].astype(F32)
        s = _sigmoid(gv)
        sg = gv * s
        a_ref[...] = (sg * uv).astype(BF)
        du_ref[...] = (da * sg).astype(BF)
        dg_ref[...] = (da * uv * (s * (1.0 + gv * (1.0 - s)))).astype(BF)

    tok = pl.BlockSpec((tm, D), lambda i, j: (i, 0))
    hid = pl.BlockSpec((None, tm, fs), lambda i, j: (j, i, 0))
    return _call(
        body, name=name, grid=(T // tm, ns), args=(dxo, x, nw, g, u, wd), comm=comm, vmem_mb=56,
        in_specs=[tok, tok, pl.BlockSpec((1, D), lambda i, j: (0, 0)), hid, hid,
                  pl.BlockSpec((None, fs, D), lambda i, j: (j, 0, 0))],
        out_specs=[hid, hid, hid, tok, tok],
        out_shape=[_sds((ns, T, fs), BF)] * 3 + [_sds((T, D), BF)] * 2,
        scratch_shapes=[pltpu.VMEM((tm, D), BF)],
        sem=(ARB, ARB))


def _ffn_bwd_resid(dg, du, wg, wu, x, nw, dxo, name, comm=None):
    T, D = x.shape
    ns, fs, _ = wg.shape
    tm = min(TM_FFN, T)

    def body(dg_ref, du_ref, wg_ref, wu_ref, x_ref, nw_ref, dxo_ref, dx_ref, dnw_ref, acc_s):
        i = pl.program_id(0)
        j = pl.program_id(1)
        prod = (jnp.dot(dg_ref[...], wg_ref[...], preferred_element_type=F32)
                + jnp.dot(du_ref[...], wu_ref[...], preferred_element_type=F32))

        @pl.when((i == 0) & (j == 0))
        def _():
            dnw_ref[...] = jnp.zeros_like(dnw_ref)

        @pl.when(j == 0)
        def _():
            acc_s[...] = prod

        @pl.when(j > 0)
        def _():
            acc_s[...] += prod

        @pl.when(j == ns - 1)
        def _():
            dx, dn = _rms_bwd(acc_s[...], x_ref[...], nw_ref[...])
            dx_ref[...] = dxo_ref[...] + dx
            dnw_ref[...] += dn

    tok = pl.BlockSpec((tm, D), lambda i, j: (i, 0))
    row = pl.BlockSpec((1, D), lambda i, j: (0, 0))
    hid = pl.BlockSpec((None, tm, fs), lambda i, j: (j, i, 0))
    wspec = pl.BlockSpec((None, fs, D), lambda i, j: (j, 0, 0))
    return _call(
        body, name=name, grid=(T // tm, ns), args=(dg, du, wg, wu, x, nw, dxo), comm=comm, vmem_mb=56,
        in_specs=[hid, hid, wspec, wspec, tok, row, tok],
        out_specs=[tok, row],
        out_shape=[_sds((T, D), F32), _sds((1, D), F32)],
        scratch_shapes=[pltpu.VMEM((tm, D), F32)],
        sem=(ARB, ARB))


def _tn(a, b, a_spec, b_spec, out_shape, out_spec, grid, name, prev=None, comm=None):
    nk = grid[-1]
    acc_shape = tuple(d for d in out_spec.block_shape if d is not None)

    def body(*refs):
        a_ref, b_ref = refs[0], refs[1]
        o_ref, acc = refs[-2], refs[-1]
        k = pl.program_id(2)
        prod = lax.dot_general(a_ref[...], b_ref[...], TN_DIMS, preferred_element_type=F32)

        @pl.when(k == 0)
        def _():
            acc[...] = prod

        @pl.when(k > 0)
        def _():
            acc[...] += prod

        @pl.when(k == nk - 1)
        def _():
            o_ref[...] = acc[...].astype(o_ref.dtype)

    in_specs = [a_spec, b_spec]
    args = [a, b]
    aliases = {}
    if prev is not None:
        in_specs.append(pl.BlockSpec(memory_space=pl.ANY))
        args.append(prev)
        aliases = {2: 0}
    main, extra = _call(
        body, name=name, grid=grid, args=args, in_specs=in_specs, out_specs=[out_spec], out_shape=[out_shape],
        scratch_shapes=[pltpu.VMEM(acc_shape, F32)], aliases=aliases, sem=(ARB, ARB, ARB), comm=comm)
    return main[0] if comm is None else (main[0], extra)


def _tn_gates(h, dbig, ns, tk, name):
    T, D = h.shape
    dq = D // ns
    nk = T // tk

    def body(a_ref, b_ref, o_ref, acc):
        k = pl.program_id(1)
        prod = lax.dot_general(a_ref[...], b_ref[...], TN_DIMS, preferred_element_type=F32)

        @pl.when(k == 0)
        def _():
            acc[...] = prod

        @pl.when(k > 0)
        def _():
            acc[...] += prod

        @pl.when(k == nk - 1)
        def _():
            for s in range(ns):
                o_ref[s] = acc[s * dq:(s + 1) * dq, :].astype(o_ref.dtype)

    return pl.pallas_call(
        body, name=name, grid=(3, nk),
        in_specs=[pl.BlockSpec((tk, D), lambda q, k: (k, 0)), pl.BlockSpec((tk, D), lambda q, k: (k, q))],
        out_specs=pl.BlockSpec((ns, None, dq, D), lambda q, k: (0, q, 0, 0)),
        out_shape=_sds((ns, 3, dq, D), BF),
        scratch_shapes=[pltpu.VMEM((D, D), F32)],
        compiler_params=_cp((PAR, ARB)),
    )(h, dbig)


def _inproj_fwd(x, nw, wbig, name):
    T, D = x.shape
    nb = wbig.shape[-1]
    tm = min(2 * TM, T)
    bn = min(2048, nb)

    def body(x_ref, nw_ref, w_ref, o_ref, h_ref, h_s):
        @pl.when(pl.program_id(1) == 0)
        def _():
            xv = x_ref[...]
            hb = (xv * _rms_r(xv) * nw_ref[...]).astype(BF)
            h_s[...] = hb
            h_ref[...] = hb

        o_ref[...] = jnp.dot(h_s[...], w_ref[...], preferred_element_type=F32).astype(BF)

    return pl.pallas_call(
        body, name=name, grid=(T // tm, nb // bn),
        in_specs=[pl.BlockSpec((tm, D), lambda i, n: (i, 0)),
                  pl.BlockSpec((1, D), lambda i, n: (0, 0)),
                  pl.BlockSpec((D, bn), lambda i, n: (0, n))],
        out_specs=[pl.BlockSpec((tm, bn), lambda i, n: (i, n)),
                   pl.BlockSpec((tm, D), lambda i, n: (i, 0))],
        out_shape=[_sds((T, nb), BF), _sds((T, D), BF)],
        scratch_shapes=[pltpu.VMEM((tm, D), BF)],
        compiler_params=_cp((PAR, ARB)),
    )(x, nw, wbig)


def _inproj_bwd(dbig, wbig, x, nw, dxin, name, comm=None):
    T, D = x.shape
    nb = wbig.shape[-1]
    tm = min(TM_FFN, T)
    tk = min(2048, nb)
    nk = nb // tk

    def body(a_ref, w_ref, x_ref, nw_ref, dxin_ref, dx_ref, dnw_ref, acc_s):
        i = pl.program_id(0)
        k = pl.program_id(1)
        prod = lax.dot_general(a_ref[...], w_ref[...], NT_DIMS, preferred_element_type=F32)

        @pl.when((i == 0) & (k == 0))
        def _():
            dnw_ref[...] = jnp.zeros_like(dnw_ref)

        @pl.when(k == 0)
        def _():
            acc_s[...] = prod

        @pl.when(k > 0)
        def _():
            acc_s[...] += prod

        @pl.when(k == nk - 1)
        def _():
            dx, dn = _rms_bwd(acc_s[...], x_ref[...], nw_ref[...])
            dx_ref[...] = dxin_ref[...] + dx
            dnw_ref[...] += dn

    tok = pl.BlockSpec((tm, D), lambda i, k: (i, 0))
    row = pl.BlockSpec((1, D), lambda i, k: (0, 0))
    return _call(
        body, name=name, grid=(T // tm, nk), args=(dbig, wbig, x, nw, dxin), comm=comm, vmem_mb=56,
        in_specs=[pl.BlockSpec((tm, tk), lambda i, k: (i, k)),
                  pl.BlockSpec((D, tk), lambda i, k: (0, k)),
                  tok, row, tok],
        out_specs=[tok, row],
        out_shape=[_sds((T, D), F32), _sds((1, D), F32)],
        scratch_shapes=[pltpu.VMEM((tm, D), F32)],
        sem=(ARB, ARB))


CONV_R = 512
CONV_BASE, CONV_GROUP = 0, 3
ATT_BASE, ATT_GROUP = 12, 3
RET_BASE, RET_GROUP = 24, 4
N_SEG = 10


N_IN_BLOCKS = N_SEG * BRANCH_W // LANE


def _orig_block(p):
    nblk = BRANCH_W // LANE
    qa, qr = p - ATT_BASE, p - RET_BASE
    conv = (p % CONV_GROUP) * nblk + p // CONV_GROUP
    att = (7 + qa % ATT_GROUP) * nblk + qa // ATT_GROUP
    ret = (3 + qr % RET_GROUP) * nblk + qr // RET_GROUP
    return jnp.where(p < ATT_BASE, conv, jnp.where(p < RET_BASE, att, ret))


def _copy_blocks(src, in_spec, out_shape, out_spec, grid, name, prev=None):
    def body(*refs):
        refs[-1][...] = refs[0][...]

    in_specs, args, aliases = [in_spec], [src], {}
    if prev is not None:
        in_specs.append(pl.BlockSpec(memory_space=pl.ANY))
        args.append(prev)
        aliases = {1: 0}
    return pl.pallas_call(
        body, name=name, grid=grid, in_specs=in_specs, out_specs=out_spec, out_shape=out_shape,
        input_output_aliases=aliases, compiler_params=_cp(tuple(PAR for _ in grid)),
    )(*args)


def _build_wbig(gates4, win4, name):
    ns, _, dq, D = gates4.shape
    per = win4.shape[-1] // LANE
    shape = _sds((D, 3 * D + N_IN_BLOCKS * LANE), gates4.dtype)
    out = _copy_blocks(gates4, pl.BlockSpec((None, None, dq, D), lambda s, i: (s, i, 0, 0)), shape,
                       pl.BlockSpec((dq, D), lambda s, i: (s, i)), (ns, 3), name + "_gates")
    return _copy_blocks(
        win4, pl.BlockSpec((None, D, LANE), lambda p: (_orig_block(p) // per, 0, _orig_block(p) % per)), shape,
        pl.BlockSpec((D, LANE), lambda p: (0, 3 * D // LANE + p)), (N_IN_BLOCKS,), name + "_in", prev=out)


def _ungroup_dw_in(dwp, ns, name):
    D = dwp.shape[0]
    per = N_IN_BLOCKS // ns
    return _copy_blocks(
        dwp, pl.BlockSpec((D, LANE), lambda p: (0, p)), _sds((ns, D, per * LANE), dwp.dtype),
        pl.BlockSpec((None, D, LANE), lambda p: (_orig_block(p) // per, 0, _orig_block(p) % per)), (N_IN_BLOCKS,), name)


def _seg0(big):
    return (big.shape[1] - N_SEG * BRANCH_W) // LANE


def _group_spec(big, base, group, rows, where):
    first = (_seg0(big) + base) // group
    assert first * group == _seg0(big) + base

    def index(*ids):
        r, g = where(*ids)
        return r, first + g

    return pl.BlockSpec((rows, group * LANE), index)


CU, CB, CC = (slice(k * LANE, (k + 1) * LANE) for k in range(3))
AQ, AK, AV = CU, CB, CC
RQ, RK, RV, RG = (slice(k * LANE, (k + 1) * LANE) for k in range(4))


def _conv_fwd(big, cw, name):
    T = big.shape[0]
    R = min(CONV_R, T)

    def body(g_ref, w_ref, y_ref, z_s):
        z_s[pl.ds(0, 8), :] = jnp.zeros((8, LANE), F32)

        def fill(t, c):
            sl = pl.ds(pl.multiple_of(t * R, R), R)
            z_s[pl.ds(pl.multiple_of(t * R + 8, 8), R), :] = g_ref[sl, CC].astype(F32) * g_ref[sl, CU].astype(F32)
            return c

        lax.fori_loop(0, T // R, fill, 0)
        w0, w1, w2 = w_ref[0:1, :], w_ref[1:2, :], w_ref[2:3, :]

        def step(t, c):
            zz = z_s[pl.ds(pl.multiple_of(t * R, R), R + 8), :]
            z0 = zz[8:]
            z1 = pltpu.roll(zz, 1, 0)[8:]
            z2 = pltpu.roll(zz, 2, 0)[8:]
            sl = pl.ds(pl.multiple_of(t * R, R), R)
            y_ref[sl, :] = (g_ref[sl, CB].astype(F32) * (w2 * z0 + w1 * z1 + w0 * z2)).astype(BF)
            return c

        lax.fori_loop(0, T // R, step, 0)

    return pl.pallas_call(
        body, name=name, grid=(BRANCH_W // LANE,),
        in_specs=[_group_spec(big, CONV_BASE, CONV_GROUP, T, lambda j: (0, j)),
                  pl.BlockSpec((3, LANE), lambda j: (0, j))],
        out_specs=pl.BlockSpec((T, LANE), lambda j: (0, j)),
        out_shape=_sds((T, BRANCH_W), BF),
        scratch_shapes=[pltpu.VMEM((T + 8, LANE), F32)],
        compiler_params=_cp((PAR,)),
    )(big, cw)


def _conv_bwd(big, dy, cw, dbig, name):
    T = big.shape[0]
    R = min(CONV_R, T)

    def body(g_ref, dy_ref, w_ref, _, o_ref, dw_ref, z_s, d_s):
        z_s[pl.ds(0, 8), :] = jnp.zeros((8, LANE), F32)
        d_s[pl.ds(T, 8), :] = jnp.zeros((8, LANE), F32)

        def fill(t, c):
            sl = pl.ds(pl.multiple_of(t * R, R), R)
            z_s[pl.ds(pl.multiple_of(t * R + 8, 8), R), :] = g_ref[sl, CC].astype(F32) * g_ref[sl, CU].astype(F32)
            d_s[sl, :] = dy_ref[sl, :].astype(F32) * g_ref[sl, CB].astype(F32)
            return c

        lax.fori_loop(0, T // R, fill, 0)
        w0, w1, w2 = w_ref[0:1, :], w_ref[1:2, :], w_ref[2:3, :]

        def step(t, carry):
            a0, a1, a2 = carry
            zz = z_s[pl.ds(pl.multiple_of(t * R, R), R + 8), :]
            z0 = zz[8:]
            z1 = pltpu.roll(zz, 1, 0)[8:]
            z2 = pltpu.roll(zz, 2, 0)[8:]
            sl = pl.ds(pl.multiple_of(t * R, R), R)
            dyv = dy_ref[sl, :].astype(F32)
            o_ref[sl, CB] = (dyv * (w2 * z0 + w1 * z1 + w0 * z2)).astype(BF)
            dd = d_s[pl.ds(pl.multiple_of(t * R, R), R + 8), :]
            d0 = dd[:R]
            d1 = pltpu.roll(dd, R + 7, 0)[:R]
            d2 = pltpu.roll(dd, R + 6, 0)[:R]
            dz = w2 * d0 + w1 * d1 + w0 * d2
            o_ref[sl, CC] = (dz * g_ref[sl, CU].astype(F32)).astype(BF)
            o_ref[sl, CU] = (dz * g_ref[sl, CC].astype(F32)).astype(BF)
            a0 = a0 + jnp.sum(d0 * z2, axis=0, keepdims=True)
            a1 = a1 + jnp.sum(d0 * z1, axis=0, keepdims=True)
            a2 = a2 + jnp.sum(d0 * z0, axis=0, keepdims=True)
            return a0, a1, a2

        zero = jnp.zeros((1, LANE), F32)
        a0, a1, a2 = lax.fori_loop(0, T // R, step, (zero, zero, zero))
        dw_ref[0:1, :] = a0
        dw_ref[1:2, :] = a1
        dw_ref[2:3, :] = a2

    group = _group_spec(big, CONV_BASE, CONV_GROUP, T, lambda j: (0, j))
    w = pl.BlockSpec((3, LANE), lambda j: (0, j))
    return pl.pallas_call(
        body, name=name, grid=(BRANCH_W // LANE,),
        in_specs=[group, pl.BlockSpec((T, LANE), lambda j: (0, j)), w, pl.BlockSpec(memory_space=pl.ANY)],
        out_specs=[group, w],
        out_shape=[_sds(dbig.shape, BF), _sds((3, BRANCH_W), F32)],
        scratch_shapes=[pltpu.VMEM((T + 8, LANE), F32), pltpu.VMEM((T + 8, LANE), F32)],
        input_output_aliases={3: 0}, compiler_params=_cp((PAR,)),
    )(big, dy, cw, dbig)


def _ret_tables(T):
    L = min(RET_L, T)
    hh = jnp.arange(H_RET, dtype=F32)
    lg = jnp.log1p(-jnp.exp2(-5.0 - hh))
    n = jnp.arange(L, dtype=F32)
    a = jnp.exp(lg[:, None] * (n + 1.0))
    b = jnp.exp(lg[:, None] * (L - 1.0 - n))
    gl = jnp.exp(lg * L)
    ch = jnp.arange(L) // CHUNK
    m = jnp.exp(lg[:, None, None] * jnp.abs(n[:, None] - n[None, :])) * (ch[None, :] <= ch[:, None]).astype(F32)
    inv_freq = ROPE_BASE ** (-jnp.linspace(0.0, 1.0, DK_RET // 2, dtype=F32))
    ang = jnp.arange(T, dtype=F32)[:, None] * inv_freq[None, :]
    cos, sin = jnp.cos(ang), jnp.sin(ang)
    return dict(
        L=L, M=m,
        a=jnp.broadcast_to(a[:, :, None], (H_RET, L, DK_RET)),
        b=jnp.broadcast_to(b[:, :, None], (H_RET, L, DK_RET)),
        gl=jnp.broadcast_to(gl[:, None, None], (H_RET, 1, DK_RET)),
        cos=jnp.concatenate([cos, cos], axis=-1), sin=jnp.concatenate([-sin, sin], axis=-1))


def _rot(x, cs, sn):
    return x * cs + pltpu.roll(x, DK_RET // 2, 1) * sn


def _unrot(dy, cs, sn):
    return dy * cs + pltpu.roll(dy * sn, DK_RET // 2, 1)


def _ret_fwd(big, tb, name, comm=None):
    T = big.shape[0]
    L = tb["L"]
    nsc = T // L
    scale = DK_RET ** -0.5

    def body(x_ref, cos_ref, sin_ref, m_ref, a_ref, b_ref, gl_ref, y_ref, o_ref, st_ref, s_s):
        @pl.when(pl.program_id(1) == 0)
        def _():
            s_s[...] = jnp.zeros_like(s_s)

        cs, sn = cos_ref[...], sin_ref[...]
        qt = _rot(x_ref[:, RQ].astype(F32), cs, sn) * scale
        kt = _rot(x_ref[:, RK].astype(F32), cs, sn)
        qb, kb, vb = qt.astype(BF), kt.astype(BF), x_ref[:, RV]
        s_prev = s_s[...]
        st_ref[...] = s_prev
        p = lax.dot_general(qb, kb, NT_DIMS, preferred_element_type=F32) * m_ref[...]
        o = (jnp.dot(p.astype(BF), vb, preferred_element_type=F32)
             + jnp.dot((qt * a_ref[...]).astype(BF), s_prev.astype(BF), preferred_element_type=F32))
        s_s[...] = s_prev * gl_ref[...] + lax.dot_general((kt * b_ref[...]).astype(BF), vb, TN_DIMS,
                                                         preferred_element_type=F32)
        o_ref[...] = o
        gv = x_ref[:, RG].astype(F32)
        y_ref[...] = (gv * _sigmoid(gv) * o * _rms_r(o)).astype(BF)

    tab = pl.BlockSpec((L, DK_RET), lambda h, i: (i, 0))
    per_head = pl.BlockSpec((None, L, DK_RET), lambda h, i: (h, 0, 0))
    out = pl.BlockSpec((L, LANE), lambda h, i: (i, h))
    return _call(
        body, name=name, grid=(H_RET, nsc), comm=comm,
        args=(big, tb["cos"], tb["sin"], tb["M"], tb["a"], tb["b"], tb["gl"]),
        in_specs=[_group_spec(big, RET_BASE, RET_GROUP, L, lambda h, i: (i, h)), tab, tab,
                  pl.BlockSpec((None, L, L), lambda h, i: (h, 0, 0)), per_head, per_head,
                  pl.BlockSpec((None, 1, DK_RET), lambda h, i: (h, 0, 0))],
        out_specs=[out, out, pl.BlockSpec((None, None, DK_RET, DK_RET), lambda h, i: (i, h, 0, 0))],
        out_shape=[_sds((T, BRANCH_W), BF), _sds((T, BRANCH_W), F32), _sds((nsc, H_RET, DK_RET, DK_RET), F32)],
        scratch_shapes=[pltpu.VMEM((DK_RET, DK_RET), F32)],
        sem=(ARB, ARB))


def _ret_bwd(big, o, st, dy, tb, dbig, name):
    T = big.shape[0]
    L = tb["L"]
    nsc = T // L
    scale = DK_RET ** -0.5

    def body(x_ref, cos_ref, sin_ref, m_ref, a_ref, b_ref, gl_ref, o_ref, st_ref, dy_ref, _, d_ref, ds_s):
        @pl.when(pl.program_id(1) == 0)
        def _():
            ds_s[...] = jnp.zeros_like(ds_s)

        cs, sn = cos_ref[...], sin_ref[...]
        mm, av, bv = m_ref[...], a_ref[...], b_ref[...]
        qt = _rot(x_ref[:, RQ].astype(F32), cs, sn) * scale
        kt = _rot(x_ref[:, RK].astype(F32), cs, sn)
        qb, kb, vb = qt.astype(BF), kt.astype(BF), x_ref[:, RV]
        pb = (lax.dot_general(qb, kb, NT_DIMS, preferred_element_type=F32) * mm).astype(BF)
        ov = o_ref[...]
        r = _rms_r(ov)
        oh = ov * r
        gv = x_ref[:, RG].astype(F32)
        sg = _sigmoid(gv)
        dyv = dy_ref[...].astype(F32)
        d_ref[:, RG] = (dyv * oh * (sg * (1.0 + gv * (1.0 - sg)))).astype(BF)
        doh = dyv * gv * sg
        dob = (r * (doh - oh * jnp.mean(doh * oh, axis=-1, keepdims=True))).astype(BF)
        dsb = ds_s[...].astype(BF)
        spb = st_ref[...].astype(BF)
        dpb = (lax.dot_general(dob, vb, NT_DIMS, preferred_element_type=F32) * mm).astype(BF)
        dqt = (jnp.dot(dpb, kb, preferred_element_type=F32)
               + lax.dot_general(dob, spb, NT_DIMS, preferred_element_type=F32) * av)
        dkt = (lax.dot_general(dpb, qb, TN_DIMS, preferred_element_type=F32)
               + lax.dot_general(vb, dsb, NT_DIMS, preferred_element_type=F32) * bv)
        dv = (lax.dot_general(pb, dob, TN_DIMS, preferred_element_type=F32)
              + jnp.dot((kt * bv).astype(BF), dsb, preferred_element_type=F32))
        ds_s[...] = ds_s[...] * gl_ref[...] + lax.dot_general((qt * av).astype(BF), dob, TN_DIMS,
                                                              preferred_element_type=F32)
        d_ref[:, RQ] = (_unrot(dqt, cs, sn) * scale).astype(BF)
        d_ref[:, RK] = _unrot(dkt, cs, sn).astype(BF)
        d_ref[:, RV] = dv.astype(BF)

    def rev(i):
        return nsc - 1 - i

    group = _group_spec(big, RET_BASE, RET_GROUP, L, lambda h, i: (rev(i), h))
    tab = pl.BlockSpec((L, DK_RET), lambda h, i: (rev(i), 0))
    per_head = pl.BlockSpec((None, L, DK_RET), lambda h, i: (h, 0, 0))
    out = pl.BlockSpec((L, LANE), lambda h, i: (rev(i), h))
    return pl.pallas_call(
        body, name=name, grid=(H_RET, nsc),
        in_specs=[group, tab, tab,
                  pl.BlockSpec((None, L, L), lambda h, i: (h, 0, 0)), per_head, per_head,
                  pl.BlockSpec((None, 1, DK_RET), lambda h, i: (h, 0, 0)),
                  out, pl.BlockSpec((None, None, DK_RET, DK_RET), lambda h, i: (rev(i), h, 0, 0)), out,
                  pl.BlockSpec(memory_space=pl.ANY)],
        out_specs=group,
        out_shape=_sds(dbig.shape, BF),
        scratch_shapes=[pltpu.VMEM((DK_RET, DK_RET), F32)],
        input_output_aliases={10: 0}, compiler_params=_cp((PAR, ARB)),
    )(big, tb["cos"], tb["sin"], tb["M"], tb["a"], tb["b"], tb["gl"], o, st, dy, dbig)


def _relbias_onehot(n):
    mm = lax.broadcasted_iota(jnp.int32, (RB_PAD, ATT_TOEP), 1)
    rr = lax.broadcasted_iota(jnp.int32, (RB_PAD, ATT_TOEP), 0)
    idx = jnp.clip(n + ATT_TOEP - mm, 0, 2 * REL_CLIP)
    return (rr == idx).astype(F32)


def _split3(x):
    hi = x.astype(BF).astype(F32)
    mid = (x - hi).astype(BF).astype(F32)
    lo = x - hi - mid
    return jnp.concatenate([hi, mid, lo], axis=0).astype(BF)


def _join3(y):
    k = y.shape[0] // 3
    return (y[:k] + y[k:2 * k]) + y[2 * k:]


def _relbias_expand(rel_bias, name, comm=None):
    far = ATT_SPAN - ATT_TOEP
    n_layers = rel_bias.shape[0]
    rbp = jnp.pad(rel_bias, ((0, 0), (0, 0), (0, RB_PAD - N_REL)))

    def body(rb_ref, o_ref):
        for l in range(n_layers):
            rb = rb_ref[l]
            const = jnp.broadcast_to(rb[:, 2 * REL_CLIP:2 * REL_CLIP + 1], (H_ATT, far))
            rb3 = _split3(rb)

            def row(n, c):
                toep = _join3(jnp.dot(rb3, _relbias_onehot(n).astype(BF), preferred_element_type=F32))
                m = lax.broadcasted_iota(jnp.int32, (1, ATT_SPAN), 1)
                d = n // CHUNK + N_PREV - m // CHUNK
                neg = jnp.where((d >= 0) & (d <= N_PREV), 0.0, NEG_INF).astype(F32)
                o_ref[l, n] = jnp.concatenate([const, toep], axis=1) + neg
                return c

            lax.fori_loop(0, ATT_TQ, row, 0)

    (out,), extra = _call(
        body, name=name, args=(rbp,), comm=comm,
        in_specs=[pl.BlockSpec(memory_space=pltpu.VMEM)],
        out_specs=[pl.BlockSpec(memory_space=pltpu.VMEM)],
        out_shape=[_sds((n_layers, ATT_TQ, H_ATT, ATT_SPAN), F32)])
    return jnp.transpose(out, (0, 2, 1, 3)), extra


def _relbias_grad(dbt, name):
    far = ATT_SPAN - ATT_TOEP

    def body(d_ref, o_ref):
        def row(n, carry):
            acc, cs = carry
            dn = d_ref[n]
            acc = acc + _join3(lax.dot_general(_split3(dn[:, far:]), _relbias_onehot(n).astype(BF), NT_DIMS,
                                               preferred_element_type=F32))
            cs = cs + jnp.sum(dn[:, :far], axis=1, keepdims=True)
            return acc, cs

        acc, cs = lax.fori_loop(0, ATT_TQ, row, (jnp.zeros((H_ATT, RB_PAD), F32), jnp.zeros((H_ATT, 1), F32)))
        rr = lax.broadcasted_iota(jnp.int32, (H_ATT, RB_PAD), 1)
        o_ref[...] = acc + jnp.where(rr == 2 * REL_CLIP, cs, 0.0)

    return pl.pallas_call(
        body, name=name,
        in_specs=[pl.BlockSpec(memory_space=pltpu.VMEM)],
        out_specs=pl.BlockSpec(memory_space=pltpu.VMEM),
        out_shape=_sds((H_ATT, RB_PAD), F32),
    )(dbt)


def _att_pad_fill(dst_s, src_ref, cols, T):
    dst_s[pl.ds(0, ATT_PAD), :] = jnp.zeros((ATT_PAD, LANE), dst_s.dtype)
    R = min(512, T)

    def cp(t, c):
        dst_s[pl.ds(pl.multiple_of(ATT_PAD + t * R, LANE), R), :] = src_ref[pl.ds(pl.multiple_of(t * R, R), R), cols]
        return c

    lax.fori_loop(0, T // R, cp, 0)


ATT_WIN = ATT_SUB * ATT_TQ + ATT_PAD


def _att_probs(s_full, sub, bias, t0):
    s = s_full[sub * ATT_TQ:(sub + 1) * ATT_TQ, sub * ATT_TQ:sub * ATT_TQ + ATT_SPAN] * (DH_ATT ** -0.5) + bias
    key_pos = t0 + sub * ATT_TQ - ATT_PAD + lax.broadcasted_iota(jnp.int32, (1, ATT_SPAN), 1)
    s = jnp.where(key_pos >= 0, s, NEG_INF)
    p = jnp.exp(s - jnp.max(s, axis=-1, keepdims=True))
    return p * (1.0 / jnp.sum(p, axis=-1, keepdims=True))


def _att_band(tiles):
    rows = []
    for sub, t in enumerate(tiles):
        parts = []
        if sub:
            parts.append(jnp.zeros((ATT_TQ, sub * ATT_TQ), BF))
        parts.append(t)
        if sub < ATT_SUB - 1:
            parts.append(jnp.zeros((ATT_TQ, (ATT_SUB - 1 - sub) * ATT_TQ), BF))
        rows.append(jnp.concatenate(parts, axis=1))
    return jnp.concatenate(rows, axis=0)


def _att_head_masks(x):
    first = lax.broadcasted_iota(jnp.int32, (1, LANE), 1) < DH_ATT
    zero = jnp.zeros_like(x)
    return first, (jnp.where(first, x, zero), jnp.where(first, zero, x))


def _att_fwd(big, bias, name, comm=None):
    T = big.shape[0]
    rows = ATT_SUB * ATT_TQ
    nt = T // rows

    def body(x_ref, b_ref, y_ref, kp_s, vp_s):
        i = pl.program_id(1)

        @pl.when(i == 0)
        def _():
            _att_pad_fill(kp_s, x_ref, AK, T)
            _att_pad_fill(vp_s, x_ref, AV, T)

        t0 = pl.multiple_of(i * rows, rows)
        kw = kp_s[pl.ds(t0, ATT_WIN), :]
        vw = vp_s[pl.ds(t0, ATT_WIN), :]
        first, qm = _att_head_masks(x_ref[pl.ds(t0, rows), AQ])
        outs = []
        for hh in range(2):
            s_full = lax.dot_general(qm[hh], kw, NT_DIMS, preferred_element_type=F32)
            band = _att_band([_att_probs(s_full, sub, b_ref[hh], t0).astype(BF) for sub in range(ATT_SUB)])
            outs.append(jnp.dot(band, vw, preferred_element_type=F32))
        y_ref[...] = jnp.where(first, outs[0], outs[1]).astype(BF)

    return _call(
        body, name=name, grid=(H_ATT // 2, nt), args=(big, bias), comm=comm,
        in_specs=[_group_spec(big, ATT_BASE, ATT_GROUP, T, lambda p, i: (0, p)),
                  pl.BlockSpec((2, ATT_TQ, ATT_SPAN), lambda p, i: (p, 0, 0))],
        out_specs=[pl.BlockSpec((rows, LANE), lambda p, i: (i, p))],
        out_shape=[_sds((T, BRANCH_W), BF)],
        scratch_shapes=[pltpu.VMEM((T + ATT_PAD, LANE), BF), pltpu.VMEM((T + ATT_PAD, LANE), BF)],
        sem=(ARB, ARB))


def _att_bwd(big, bias, dy, dbig, name, comm=None):
    T = big.shape[0]
    rows = ATT_SUB * ATT_TQ
    nt = T // rows
    scale = DH_ATT ** -0.5

    def body(x_ref, b_ref, dy_ref, _, d_ref, db_ref, kp_s, vp_s, dk_s, dv_s):
        i = pl.program_id(1)

        @pl.when(i == 0)
        def _():
            _att_pad_fill(kp_s, x_ref, AK, T)
            _att_pad_fill(vp_s, x_ref, AV, T)
            dk_s[...] = jnp.zeros_like(dk_s)
            dv_s[...] = jnp.zeros_like(dv_s)
            db_ref[...] = jnp.zeros_like(db_ref)

        t0 = pl.multiple_of(i * rows, rows)
        win = pl.ds(t0, ATT_WIN)
        kw = kp_s[win, :]
        vw = vp_s[win, :]
        first, qm = _att_head_masks(x_ref[pl.ds(t0, rows), AQ])
        _, dom = _att_head_masks(dy_ref[...])
        dqs, dkt, dvt = [], None, None
        for hh in range(2):
            s_full = lax.dot_general(qm[hh], kw, NT_DIMS, preferred_element_type=F32)
            dp_full = lax.dot_general(dom[hh], vw, NT_DIMS, preferred_element_type=F32)
            ps, dss, db = [], [], None
            for sub in range(ATT_SUB):
                pn = _att_probs(s_full, sub, b_ref[hh], t0)
                dp = dp_full[sub * ATT_TQ:(sub + 1) * ATT_TQ, sub * ATT_TQ:sub * ATT_TQ + ATT_SPAN]
                ds = pn * (dp - jnp.sum(dp * pn, axis=-1, keepdims=True))
                db = ds if db is None else db + ds
                ps.append(pn.astype(BF))
                dss.append(ds.astype(BF))
            db_ref[hh] += db
            ds_band, p_band = _att_band(dss), _att_band(ps)
            dqs.append(jnp.dot(ds_band, kw, preferred_element_type=F32))
            qt = jnp.transpose(qm[hh].astype(F32)).astype(BF)
            dot_ = jnp.transpose(dom[hh].astype(F32)).astype(BF)
            dk_h = jnp.dot(qt, ds_band, preferred_element_type=F32)
            dv_h = jnp.dot(dot_, p_band, preferred_element_type=F32)
            dkt = dk_h if dkt is None else dkt + dk_h
            dvt = dv_h if dvt is None else dvt + dv_h
        d_ref[pl.ds(t0, rows), AQ] = (jnp.where(first, dqs[0], dqs[1]) * scale).astype(BF)
        dk_s[win, :] += jnp.transpose(dkt) * scale
        dv_s[win, :] += jnp.transpose(dvt)

        @pl.when(i == nt - 1)
        def _():
            R = min(512, T)

            def cp(t, c):
                src = pl.ds(pl.multiple_of(ATT_PAD + t * R, LANE), R)
                dst = pl.ds(pl.multiple_of(t * R, R), R)
                d_ref[dst, AK] = dk_s[src, :].astype(BF)
                d_ref[dst, AV] = dv_s[src, :].astype(BF)
                return c

            lax.fori_loop(0, T // R, cp, 0)

    group = _group_spec(big, ATT_BASE, ATT_GROUP, T, lambda p, i: (0, p))
    tile = pl.BlockSpec((rows, LANE), lambda p, i: (i, p))
    bspec = pl.BlockSpec((2, ATT_TQ, ATT_SPAN), lambda p, i: (p, 0, 0))
    return _call(
        body, name=name, grid=(H_ATT // 2, nt), args=(big, bias, dy, dbig), comm=comm, aliases={3: 0}, vmem_mb=56,
        in_specs=[group, bspec, tile, pl.BlockSpec(memory_space=pl.ANY)],
        out_specs=[group, bspec],
        out_shape=[_sds(dbig.shape, BF), _sds((H_ATT, ATT_TQ, ATT_SPAN), F32)],
        scratch_shapes=[pltpu.VMEM((T + ATT_PAD, LANE), BF), pltpu.VMEM((T + ATT_PAD, LANE), BF),
                        pltpu.VMEM((T + ATT_PAD, LANE), F32), pltpu.VMEM((T + ATT_PAD, LANE), F32)],
        sem=(ARB, ARB))


def _merge_fwd(x1, big, ys, wb, wo, name):
    T, D = x1.shape
    tm = min(TM, T)

    def body(x_ref, gp_ref, yc_ref, yr_ref, ya_ref, wb_ref, wo_ref, x2_ref, p_ref, mg_ref):
        merged = jnp.zeros((tm, D), F32)
        for i, y_ref in enumerate((yc_ref, yr_ref, ya_ref)):
            cols = slice(i * D, (i + 1) * D)
            pb = jnp.dot(y_ref[...], wb_ref[i], preferred_element_type=F32).astype(BF)
            p_ref[:, cols] = pb
            merged = merged + _sigmoid(gp_ref[:, cols].astype(F32)) * pb.astype(F32)
        mb = merged.astype(BF)
        mg_ref[...] = mb
        x2_ref[...] = x_ref[...] + jnp.dot(mb, wo_ref[...], preferred_element_type=F32)

    tok = pl.BlockSpec((tm, D), lambda i: (i, 0))
    wide = pl.BlockSpec((tm, 3 * D), lambda i: (i, 0))
    yspec = pl.BlockSpec((tm, BRANCH_W), lambda i: (i, 0))
    return pl.pallas_call(
        body, name=name, grid=(T // tm,),
        in_specs=[tok, wide, yspec, yspec, yspec,
                  pl.BlockSpec((3, BRANCH_W, D), lambda i: (0, 0, 0)),
                  pl.BlockSpec((D, D), lambda i: (0, 0))],
        out_specs=[tok, wide, tok],
        out_shape=[_sds((T, D), F32), _sds((T, 3 * D), BF), _sds((T, D), BF)],
        compiler_params=_cp((PAR,)),
    )(x1, big, *ys, wb, wo)


def _merge_bwd(dx2, big, p, wb, wo, name):
    T, D = dx2.shape
    tm = min(TM, T)

    def body(dx_ref, gp_ref, p_ref, wb_ref, wo_ref, dp_ref, dgp_ref, dyc_ref, dyr_ref, dya_ref, dxb_ref):
        dxb = dx_ref[...].astype(BF)
        dxb_ref[...] = dxb
        dm = lax.dot_general(dxb, wo_ref[...], NT_DIMS, preferred_element_type=F32)
        for i, dy_ref in enumerate((dyc_ref, dyr_ref, dya_ref)):
            cols = slice(i * D, (i + 1) * D)
            gt = _sigmoid(gp_ref[:, cols].astype(F32))
            dpb = (dm * gt).astype(BF)
            dp_ref[:, cols] = dpb
            dgp_ref[:, cols] = (dm * p_ref[:, cols].astype(F32) * gt * (1.0 - gt)).astype(BF)
            dy_ref[...] = lax.dot_general(dpb, wb_ref[i], NT_DIMS, preferred_element_type=F32).astype(BF)

    tok = pl.BlockSpec((tm, D), lambda i: (i, 0))
    wide = pl.BlockSpec((tm, 3 * D), lambda i: (i, 0))
    yspec = pl.BlockSpec((tm, BRANCH_W), lambda i: (i, 0))
    return pl.pallas_call(
        body, name=name, grid=(T // tm,),
        in_specs=[tok, wide, wide,
                  pl.BlockSpec((3, BRANCH_W, D), lambda i: (0, 0, 0)),
                  pl.BlockSpec((D, D), lambda i: (0, 0))],
        out_specs=[wide, wide, yspec, yspec, yspec, tok],
        out_shape=[_sds((T, 3 * D), BF), _sds(big.shape, BF)] + [_sds((T, BRANCH_W), BF)] * 3 + [_sds((T, D), BF)],
        compiler_params=_cp((PAR,)),
    )(dx2, big, p, wb, wo)


def _loss_head(x, tgt, fw, name):
    T, D = x.shape
    tm = min(TM, T)

    def body(x_ref, t_ref, w_ref, loss_ref, dx_ref, dw_ref):
        @pl.when(pl.program_id(0) == 0)
        def _():
            loss_ref[...] = jnp.zeros_like(loss_ref)
            dw_ref[...] = jnp.zeros_like(dw_ref)

        xv = x_ref[...]
        wv = w_ref[...]
        e = xv * _rms_r(xv) * wv - t_ref[...]
        loss_ref[...] += 0.5 * jnp.sum(jnp.mean(e * e, axis=-1, keepdims=True))
        dx, dn = _rms_bwd(e * (1.0 / D), xv, wv)
        dx_ref[...] = dx
        dw_ref[...] += dn

    tok = pl.BlockSpec((tm, D), lambda i: (i, 0))
    return pl.pallas_call(
        body, name=name, grid=(T // tm,),
        in_specs=[tok, tok, pl.BlockSpec((1, D), lambda i: (0, 0))],
        out_specs=[pl.BlockSpec((8, LANE), lambda i: (0, 0)), tok, pl.BlockSpec((1, D), lambda i: (0, 0))],
        out_shape=[_sds((8, LANE), F32), _sds((T, D), F32), _sds((1, D), F32)],
        compiler_params=_cp((ARB,)),
    )(x, tgt, fw)


def _block_rows(rows, cols):
    cap = max(8, (1 << 18) // cols)
    best = None
    for r in range(8, rows + 1, 8):
        if rows % r == 0 and r <= cap:
            best = r
    return best if best is not None else rows


def _sum8(l_ref):
    def four(base):
        return ((l_ref[base + 3].astype(F32) + l_ref[base].astype(F32)) + l_ref[base + 1].astype(F32)
                ) + l_ref[base + 2].astype(F32)

    return four(0) + four(4)


def _sum_adamw(lands, w, m, v, name):
    _, rows, cols = lands[0].shape
    br = _block_rows(rows, cols)
    nb = rows // br
    n_layers = len(lands)

    def body(*refs):
        l_refs = refs[:n_layers]
        w_ref, m_ref, v_ref, g_ref, d_ref, nm_ref, nv_ref = refs[n_layers:]
        i = pl.program_id(0)
        for l, l_ref in enumerate(l_refs):
            @pl.when((i >= l * nb) & (i < (l + 1) * nb))
            def _():
                g = _sum8(l_ref)
                d, nm, nv = _adamw_math(w_ref[...], g, m_ref[...], v_ref[...])
                g_ref[...] = g
                d_ref[...] = d
                nm_ref[...] = nm
                nv_ref[...] = nv

    def land_spec(l):
        return pl.BlockSpec((2 * N_SHARD, br, cols), lambda i: (0, jnp.clip(i - l * nb, 0, nb - 1), 0))

    blk = pl.BlockSpec((br, cols), lambda i: (i, 0))
    return pl.pallas_call(
        body, name=name, grid=(n_layers * nb,),
        in_specs=[land_spec(l) for l in range(n_layers)] + [blk] * 3, out_specs=[blk] * 4,
        out_shape=[_sds((n_layers * rows, cols), F32)] * 4,
        compiler_params=_cp((PAR,)),
    )(*lands, w, m, v)


def _adamw_math(w, g, m, v):
    m = ADAM_B1 * m + (1.0 - ADAM_B1) * g
    v = ADAM_B2 * v + (1.0 - ADAM_B2) * (g * g)
    m_hat = m / (1.0 - ADAM_B1 ** ADAM_STEP)
    v_hat = v / (1.0 - ADAM_B2 ** ADAM_STEP)
    delta = -ADAM_LR * (m_hat / (jnp.sqrt(v_hat) + ADAM_EPS) + ADAM_WD * w)
    return delta, m, v


def _adamw(w, g, m, v, name):
    rows, cols = w.shape
    br = _block_rows(rows, cols)

    def body(w_ref, g_ref, m_ref, v_ref, d_ref, nm_ref, nv_ref):
        d, nm, nv = _adamw_math(w_ref[...], g_ref[...], m_ref[...], v_ref[...])
        d_ref[...] = d
        nm_ref[...] = nm
        nv_ref[...] = nv

    blk = pl.BlockSpec((br, cols), lambda i: (i, 0))
    return pl.pallas_call(
        body, name=name, grid=(rows // br,),
        in_specs=[blk] * 4, out_specs=[blk] * 3,
        out_shape=[_sds((rows, cols), F32)] * 3,
        compiler_params=_cp((PAR,)),
    )(w, g, m, v)


def _allreduce_small(v, name):
    rows = v.shape[0]
    flips = [(fx, fy, fc) for fx in (0, 1) for fy in (0, 1) for fc in (0, 1) if fx or fy or fc]

    def body(v_ref, o_ref, all_s, ssem, rsem):
        x, y, c = _place()

        def peer(f):
            return (x + f[0] - 2 * x * f[0], y + f[1] - 2 * y * f[1], c + f[2] - 2 * c * f[2])

        def slot(p):
            return all_s.at[4 * p[0] + 2 * p[1] + p[2]]

        def copy(k, f, owner):
            return pltpu.make_async_remote_copy(
                src_ref=v_ref, dst_ref=slot(owner), send_sem=ssem.at[k], recv_sem=rsem.at[k],
                device_id=peer(f), device_id_type=MESH)

        sends = [copy(k, f, (x, y, c)) for k, f in enumerate(flips)]
        for cp in sends:
            cp.start()
        all_s[4 * x + 2 * y + c] = v_ref[...]
        for k, f in enumerate(flips):
            copy(k, f, peer(f)).wait_recv()
        for cp in sends:
            cp.wait_send()
        acc = all_s[0]
        for d in range(1, 8):
            acc = acc + all_s[d]
        o_ref[...] = acc

    return pl.pallas_call(
        body, name=name,
        in_specs=[pl.BlockSpec(memory_space=pltpu.VMEM)],
        out_specs=pl.BlockSpec(memory_space=pltpu.VMEM),
        out_shape=_sds((rows, LANE), F32),
        scratch_shapes=[pltpu.VMEM((8, rows, LANE), F32), pltpu.SemaphoreType.DMA((7,)), pltpu.SemaphoreType.DMA((7,))],
    )(v)


BIG_NAMES = ("ffn1_w_gate", "ffn1_w_up", "ffn1_w_down", "w_in", "w_branch", "w_merge_gate", "w_out",
             "ffn2_w_gate", "ffn2_w_up", "ffn2_w_down")


FFN1 = ("ffn1_w_gate", "ffn1_w_up", "ffn1_w_down")
FFN2 = ("ffn2_w_gate", "ffn2_w_up", "ffn2_w_down")
MIX_IN = ("w_in", "w_merge_gate")
MIX_OUT = ("w_branch", "w_out")


def _keys(names, l):
    return [(n, l) for n in names]


def _local_step(x, tgt, small, convw_full, biases, wx, n_layers):
    T, D = x.shape
    L = n_layers
    ns = N_SHARD
    dq = D // ns
    W = wx.w

    def hosted(call, keys, scatter=False):
        comm = wx.pieces(keys, scatter)
        main, extra = call(comm)
        if comm is not None:
            wx.arrived(keys, extra, scatter)
        return main

    def mixer_views(l):
        return _build_wbig(W[("w_merge_gate", l)], W[("w_in", l)], f"wbig_{l}")

    def out_views(l):
        wb4 = W[("w_branch", l)]
        wb = _copy_blocks(wb4, pl.BlockSpec((None, None, BRANCH_W, dq), lambda s_, i: (s_, i, 0, 0)),
                          _sds((3, BRANCH_W, D), wb4.dtype),
                          pl.BlockSpec((None, BRANCH_W, dq), lambda s_, i: (i, 0, s_)), (ns, 3), f"w_branch_whole_{l}")
        wo = W[("w_out", l)].reshape(D, D)
        return wb, wo

    tb = _ret_tables(T)

    saved = []
    h = x
    for l in range(L):
        s = {"x0": h}
        nxt = l + 1
        x1, s["g1"], s["u1"] = hosted(
            lambda c: _ffn_fwd(h, small["ffn1_norm"][l][None], W[("ffn1_w_gate", l)], W[("ffn1_w_up", l)],
                               W[("ffn1_w_down", l)], f"ffn1_fwd_{l}", comm=c), _keys(MIX_IN + MIX_OUT, l))
        s["x1"] = x1
        s["wbig"] = mixer_views(l)
        big, s["h"] = _inproj_fwd(x1, small["mix_norm"][l][None], s["wbig"], f"inproj_fwd_{l}")
        s["big"] = big
        s["bias"] = biases[l]
        s["yc"] = _conv_fwd(big, convw_full[l], f"conv_fwd_{l}")
        s["yr"], s["o"], s["st"] = hosted(lambda c: _ret_fwd(big, tb, f"ret_fwd_{l}", comm=c), [])
        (s["ya"],) = hosted(lambda c: _att_fwd(big, s["bias"], f"att_fwd_{l}", comm=c), _keys(FFN2, l))
        s["wb"], s["wo"] = out_views(l)
        x2, s["p"], s["mg"] = _merge_fwd(x1, big, (s["yc"], s["yr"], s["ya"]), s["wb"], s["wo"], f"merge_fwd_{l}")
        s["x2"] = x2
        h, s["g2"], s["u2"] = hosted(
            lambda c: _ffn_fwd(x2, small["ffn2_norm"][l][None], W[("ffn2_w_gate", l)], W[("ffn2_w_up", l)],
                               W[("ffn2_w_down", l)], f"ffn2_fwd_{l}", comm=c), _keys(FFN1, nxt) if nxt < L else [])
        saved.append(s)

    loss_p, dx, d_final = _loss_head(h, tgt, small["final_norm"][None], "loss_head")

    gs = {"final_norm": d_final[0]}
    for k in ("ffn1_norm", "mix_norm", "ffn2_norm", "rel_bias", "conv_w"):
        gs[k] = [None] * L
    tk = min(2048, T)
    nk = T // tk

    def ffn_back(pre, l, dxo, x_in, g, u, first_keys, second_keys, between=None):
        nw = small[pre + "_norm"][l][None]
        dgv, duv, av, hb, dacc = hosted(
            lambda c: _ffn_bwd_hidden(dxo, x_in, nw, g, u, W[(pre + "_w_down", l)], f"{pre}_bwd_hidden_{l}", comm=c),
            first_keys, scatter=True)
        parts = (hb, dgv, duv, av, dacc)
        if between is not None:
            second_keys = between(parts)
        dxn, dn = hosted(
            lambda c: _ffn_bwd_resid(dgv, duv, W[(pre + "_w_gate", l)], W[(pre + "_w_up", l)], x_in, nw, dxo,
                                     f"{pre}_bwd_resid_{l}", comm=c),
            second_keys, scatter=True)
        gs[pre + "_norm"][l] = dn[0]
        return dxn, parts

    def ffn_grads(pre, l, hb, dgv, duv, av, dacc, chain=False):
        fs = dgv.shape[-1]
        tkf = min(2 * tk, T)
        hspec = pl.BlockSpec((tkf, D), lambda p, q, k: (k, 0))
        sspec = pl.BlockSpec((None, tkf, fs), lambda p, q, k: (p, k, 0))
        down_spec = pl.BlockSpec((None, fs, D), lambda p, q, k: (p, 0, 0))
        jobs = [(pre + "_w_gate", dgv, hb, sspec, hspec, (ns, fs, D), down_spec),
                (pre + "_w_up", duv, hb, sspec, hspec, (ns, fs, D), down_spec),
                (pre + "_w_down", av, dacc, sspec, hspec, (ns, fs, D), down_spec)]
        for idx, (nm, a, b, a_spec, b_spec, shape, o_spec) in enumerate(jobs):
            def product(c):
                r = _tn(a, b, a_spec, b_spec, _sds(shape, BF), o_spec, (ns, 1, T // tkf), f"d{nm}_{l}", comm=c)
                return (r, []) if c is None else r
            keys = [(jobs[0][0], l)] if chain and idx == 2 else []
            wx.g[(nm, l)] = hosted(product, keys, scatter=True)
        return [(jobs[1][0], l), (jobs[2][0], l)] if chain else []

    for l in reversed(range(L)):
        s = saved[l]
        above = _keys(FFN1, l + 1) if l + 1 < L else []
        dx, parts = ffn_back("ffn2", l, dx, s["x2"], s["g2"], s["u2"], above[:1], above[1:])
        ffn_grads("ffn2", l, *parts)
        dp, dbig, dyc, dyr, dya, dxb = _merge_bwd(dx, s["big"], s["p"], s["wb"], s["wo"], f"merge_bwd_{l}")
        wx.g[("w_out", l)] = _tn(
            s["mg"], dxb, pl.BlockSpec((tk, dq), lambda p, q, k: (k, p)), pl.BlockSpec((tk, D), lambda p, q, k: (k, 0)),
            _sds((ns, dq, D), BF), pl.BlockSpec((None, dq, D), lambda p, q, k: (p, 0, 0)), (ns, 1, nk), f"dw_out_{l}")
        gb = None
        for i, yv in enumerate((s["yc"], s["yr"], s["ya"])):
            gb = _tn(yv, dp,
                     pl.BlockSpec((tk, BRANCH_W), lambda p, q, k: (k, 0)),
                     pl.BlockSpec((tk, dq), lambda p, q, k, i=i: (k, i * ns + p)),
                     _sds((ns, 3, BRANCH_W, dq), BF),
                     pl.BlockSpec((None, None, BRANCH_W, dq), lambda p, q, k, i=i: (p, i, 0, 0)),
                     (ns, 1, nk), f"dw_branch{i}_{l}", prev=gb)
        wx.g[("w_branch", l)] = gb
        dbig, dcw = _conv_bwd(s["big"], dyc, convw_full[l], dbig, f"conv_bwd_{l}")
        gs["conv_w"][l] = dcw
        dbig = _ret_bwd(s["big"], s["o"], s["st"], dyr, tb, dbig, f"ret_bwd_{l}")
        dbig, dbias = hosted(lambda c: _att_bwd(s["big"], s["bias"], dya, dbig, f"att_bwd_{l}", comm=c),
                             _keys(FFN2, l), scatter=True)
        gs["rel_bias"][l] = _relbias_grad(jnp.transpose(dbias, (1, 0, 2)), f"relbias_grad_{l}")[:, :N_REL]
        n_in = N_SEG * BRANCH_W
        bn = 1024 if (3 * D) % 1024 == 0 else BRANCH_W
        dwp = _tn(s["h"], dbig, pl.BlockSpec((tk, D), lambda p, q, k: (k, 0)),
                  pl.BlockSpec((tk, bn), lambda p, q, k: (k, 3 * D // bn + q)),
                  _sds((D, n_in), BF), pl.BlockSpec((D, bn), lambda p, q, k: (0, q)), (1, n_in // bn, nk), f"dw_in_{l}")
        wx.g[("w_in", l)] = _ungroup_dw_in(dwp, ns, f"dw_in_shards_{l}")
        wx.g[("w_merge_gate", l)] = _tn_gates(s["h"], dbig, ns, tk, f"dw_merge_gate_{l}")
        dx, dn = hosted(
            lambda c: _inproj_bwd(dbig, s["wbig"], s["x1"], small["mix_norm"][l][None], dx, f"inproj_bwd_{l}", comm=c),
            [("w_in", l)], scatter=True)
        gs["mix_norm"][l] = dn[0]
        rest = [("w_merge_gate", l), ("w_branch", l), ("w_out", l)]
        if l == 0:
            dx, _ = ffn_back("ffn1", l, dx, s["x0"], s["g1"], s["u1"], rest, [],
                             between=lambda parts: ffn_grads("ffn1", 0, *parts, chain=True))
        else:
            dx, parts = ffn_back("ffn1", l, dx, s["x0"], s["g1"], s["u1"], rest, [])
            ffn_grads("ffn1", l, *parts)

    for k in ("ffn1_norm", "mix_norm", "ffn2_norm", "rel_bias", "conv_w"):
        gs[k] = jnp.stack(gs[k])
    return loss_p, dx, gs


class _Exchange:
    def __init__(self, shards):
        self.shards = shards
        self.w = {}
        self.g = {}
        self.landed = {}

    def own(self, key):
        return self.shards[key[0]][key[1]].astype(BF)

    def pieces(self, keys, scatter):
        if not keys:
            return None
        if scatter:
            return _Scatter([self.g[k] for k in keys])
        return _HalfGather([_halves(self.own(k)) for k in keys])

    def arrived(self, keys, outs, scatter):
        for k, o in zip(keys, outs):
            if scatter:
                self.landed[k] = o
            else:
                self.w[k] = o.reshape((N_SHARD,) + self.shards[k[0]].shape[1:])


def _halves(a):
    return a.reshape(2, -1, a.shape[-1])


TRANSPOSED_GRADS = ("ffn1_w_gate", "ffn1_w_up", "ffn2_w_gate", "ffn2_w_up")
W_NAMES = ("ffn1_norm", "ffn1_w_gate", "ffn1_w_up", "ffn1_w_down", "mix_norm", "w_in", "conv_w", "rel_bias", "w_branch",
           "w_merge_gate", "w_out", "ffn2_norm", "ffn2_w_gate", "ffn2_w_up", "ffn2_w_down", "final_norm")


def _as2d(a):
    return a.reshape(1, -1) if a.ndim == 1 else a.reshape(-1, a.shape[-1])


def kernel(x, ffn1_norm, ffn1_w_gate, ffn1_w_up, ffn1_w_down, mix_norm, w_in, conv_w, rel_bias, w_branch, w_merge_gate, w_out, ffn2_norm, ffn2_w_gate, ffn2_w_up, ffn2_w_down, final_norm, loss_target, m_ffn1_norm, m_ffn1_w_gate, m_ffn1_w_up, m_ffn1_w_down, m_mix_norm, m_w_in, m_conv_w, m_rel_bias, m_w_branch, m_w_merge_gate, m_w_out, m_ffn2_norm, m_ffn2_w_gate, m_ffn2_w_up, m_ffn2_w_down, m_final_norm, v_ffn1_norm, v_ffn1_w_gate, v_ffn1_w_up, v_ffn1_w_down, v_mix_norm, v_w_in, v_conv_w, v_rel_bias, v_w_branch, v_w_merge_gate, v_w_out, v_ffn2_norm, v_ffn2_w_gate, v_ffn2_w_up, v_ffn2_w_down, v_final_norm):
    given = dict(locals())
    w = {n: given[n] for n in W_NAMES}
    m = {n: given["m_" + n] for n in W_NAMES}
    v = {n: given["v_" + n] for n in W_NAMES}
    my_chip = 2 * lax.axis_index("x") + lax.axis_index("y")
    L = w_in.shape[0]

    wx = _Exchange({n: jnp.swapaxes(w[n], 1, 2) if n in TRANSPOSED_GRADS else w[n] for n in BIG_NAMES})
    first = _keys(FFN1, 0)
    biases, got = _relbias_expand(
        rel_bias, "relbias_expand", comm=_HalfGather([_halves(wx.own(k)) for k in first] + [_halves(conv_w)]))
    wx.arrived(first, got[:-1], False)
    convw_full = jnp.transpose(got[-1].reshape((N_SHARD,) + conv_w.shape), (1, 2, 0, 3)).reshape(
        conv_w.shape[0], conv_w.shape[1], -1)

    small = {n: w[n] for n in ("ffn1_norm", "mix_norm", "ffn2_norm", "final_norm")}
    loss_p, grad_x, gs = _local_step(x[0], loss_target[0], small, convw_full, biases, wx, L)

    parts = [gs["ffn1_norm"].reshape(-1), gs["mix_norm"].reshape(-1), gs["ffn2_norm"].reshape(-1),
             gs["final_norm"].reshape(-1), gs["rel_bias"].reshape(-1), gs["conv_w"].reshape(-1), loss_p[0]]
    sizes = [p.shape[0] for p in parts]
    flat = jnp.concatenate(parts)
    rows = -(-flat.shape[0] // (8 * LANE)) * 8
    flat = jnp.pad(flat, (0, rows * LANE - flat.shape[0])).reshape(rows, LANE)
    red = _allreduce_small(flat, "allreduce_small").reshape(-1)
    offs = [0]
    for sz in sizes:
        offs.append(offs[-1] + sz)
    sm = {}
    for i, n in enumerate(("ffn1_norm", "mix_norm", "ffn2_norm", "final_norm", "rel_bias", "conv_w")):
        sm[n] = red[offs[i]:offs[i + 1]]
    loss = red[offs[6]]
    sm["conv_w"] = lax.dynamic_slice_in_dim(sm["conv_w"].reshape(conv_w.shape[0], conv_w.shape[1], -1),
                                            my_chip * conv_w.shape[2], conv_w.shape[2], axis=2)

    grads, deltas, new_m, new_v = {}, {}, {}, {}
    for n in W_NAMES:
        flip = n in TRANSPOSED_GRADS

        def view(a):
            return jnp.swapaxes(a, 1, 2) if flip else a

        shape = view(w[n]).shape
        wmv = [_as2d(view(a[n])) for a in (w, m, v)]
        if n in BIG_NAMES:
            lands = [wx.landed[(n, l)] for l in range(L)]
            out = _sum_adamw([a.reshape(a.shape[0], -1, a.shape[-1]) for a in lands], *wmv, f"adamw_{n}")
        else:
            g = _as2d(sm[n].reshape(shape))
            out = [g] + list(_adamw(wmv[0], g, wmv[1], wmv[2], f"adamw_{n}"))
        grads[n], deltas[n], new_m[n], new_v[n] = (view(o.reshape(shape)) for o in out)

    return (loss, grad_x[None], *[grads[n] for n in W_NAMES], *[deltas[n] for n in W_NAMES],
            *[new_m[n] for n in W_NAMES], *[new_v[n] for n in W_NAMES])
```

```python
import functools
import math

import jax
import jax.numpy as jnp
from jax import lax
from jax.experimental import pallas as pl
from jax.experimental.pallas import tpu as pltpu

F32 = jnp.float32
BF = jnp.bfloat16
MESH = pl.DeviceIdType.MESH
ARB = "arbitrary"
PAR = "parallel"

EPS = 1e-6
NEG_INF = -1e30
ROPE_BASE = 10000.0
CHUNK = 64
BRANCH_W = 512
H_RET = 4
DK_RET = 128
H_ATT = 8
DH_ATT = 64
N_PREV = 8
REL_CLIP = 128
N_REL = 2 * REL_CLIP + 1
N_SHARD = 4
LANE = 128
RET_L = 512
ATT_TQ = 128
ATT_SUB = 4
ATT_PAD = N_PREV * CHUNK
ATT_SPAN = ATT_TQ + ATT_PAD
ATT_TOEP = 2 * REL_CLIP
RB_PAD = 264
TM = 512
TM_FFN = 1024

ADAM_LR = 0.001
ADAM_B1 = 0.9
ADAM_B2 = 0.999
ADAM_EPS = 1e-08
ADAM_WD = 0.01
ADAM_STEP = 10

NT_DIMS = (((1,), (1,)), ((), ()))
TN_DIMS = (((0,), (0,)), ((), ()))


def _cp(sem, vmem_mb=48):
    return pltpu.CompilerParams(dimension_semantics=sem, vmem_limit_bytes=vmem_mb << 20)


def _sds(shape, dtype):
    return jax.ShapeDtypeStruct(tuple(shape), dtype)


def _rms_r(x):
    return lax.rsqrt(jnp.mean(x * x, axis=-1, keepdims=True) + EPS)


def _sigmoid(x):
    return 0.5 * jnp.tanh(0.5 * x) + 0.5


def _rms_bwd(dh, xv, nw):
    r = _rms_r(xv)
    xh = xv * r
    dxh = dh * nw
    dx = r * (dxh - xh * jnp.mean(dxh * xh, axis=-1, keepdims=True))
    return dx, jnp.sum(dh * xh, axis=0, keepdims=True)


def _place():
    return lax.axis_index("x"), lax.axis_index("y"), lax.axis_index("c")


def _other_chips(x, y):
    return [(1 - x, y), (x, 1 - y), (1 - x, 1 - y)]


class _Scatter:
    def __init__(self, srcs):
        self.srcs = list(srcs)
        n = len(self.srcs)
        self.out_shape = [_sds((2 * N_SHARD,) + s.shape[1:], s.dtype) for s in self.srcs]
        self.scratch = [pltpu.SemaphoreType.DMA((n,)), pltpu.SemaphoreType.DMA((3, n)), pltpu.SemaphoreType.DMA((3, n)),
                        pltpu.SemaphoreType.DMA((4, n)), pltpu.SemaphoreType.DMA((4, n))]

    def _plan(self, src, dst, sems, want):
        lsem, s1, r1, s2, r2 = sems
        x, y, c = _place()
        mine = 2 * x + y
        n = len(src)
        chips = list(enumerate(_other_chips(x, y)))

        def copy(s_ref, d_ref, ssem, rsem, to):
            return pltpu.make_async_remote_copy(src_ref=s_ref, dst_ref=d_ref, send_sem=ssem, recv_sem=rsem,
                                                device_id=to, device_id_type=MESH)

        local = [pltpu.make_async_copy(src[k].at[mine], dst[k].at[3], lsem.at[k]) for k in range(n)
                 ] if "local" in want else []
        sends = [copy(src[k].at[2 * ch[0] + ch[1]], dst[k].at[j], s1.at[j, k], r1.at[j, k], (ch[0], ch[1], c))
                 for j, ch in chips for k in range(n)] if "sends" in want else []
        passes = [copy(dst[k].at[j], dst[k].at[4 + j], s2.at[j, k], r2.at[j, k], (x, y, 1 - c))
                  for j, ch in chips for k in range(n)] if "passes" in want else []
        own_pass = [copy(src[k].at[mine], dst[k].at[7], s2.at[3, k], r2.at[3, k], (x, y, 1 - c))
                    for k in range(n)] if "own_pass" in want else []
        return local, sends, passes, own_pass

    def start(self, src, dst, sems):
        local, sends, _, own_pass = self._plan(src, dst, sems, ("local", "sends", "own_pass"))
        for cp in local + sends + own_pass:
            cp.start()

    def relay(self, src, dst, sems):
        _, sends, passes, _ = self._plan(src, dst, sems, ("sends", "passes"))
        for land, fwd in zip(sends, passes):
            land.wait_recv()
            fwd.start()

    def finish(self, src, dst, sems):
        local, sends, passes, own_pass = self._plan(src, dst, sems, ("local", "sends", "passes", "own_pass"))
        for cp in passes + own_pass:
            cp.wait_recv()
        for cp in sends + passes + own_pass:
            cp.wait_send()
        for cp in local:
            cp.wait()

    def wait(self, src, dst, sems):
        self.relay(src, dst, sems)
        self.finish(src, dst, sems)


class _HalfGather:
    def __init__(self, srcs):
        self.srcs = list(srcs)
        n = len(self.srcs)
        self.out_shape = [_sds((N_SHARD,) + s.shape, s.dtype) for s in self.srcs]
        self.scratch = [pltpu.SemaphoreType.DMA((n,))] + [pltpu.SemaphoreType.DMA((3, n)) for _ in range(4)]

    def _plan(self, src, dst, sems, want):
        lsem, s1, r1, s2, r2 = sems
        x, y, c = _place()
        mine = 2 * x + y
        n = len(src)
        chips = [(j, ch, 2 * ch[0] + ch[1]) for j, ch in enumerate(_other_chips(x, y))]

        def copy(s_ref, d_ref, ssem, rsem, to):
            return pltpu.make_async_remote_copy(src_ref=s_ref, dst_ref=d_ref, send_sem=ssem, recv_sem=rsem,
                                                device_id=to, device_id_type=MESH)

        def over(kind, make):
            return [make(j, ch, slot, k) for j, ch, slot in chips for k in range(n)] if kind in want else []

        local = [pltpu.make_async_copy(src[k], dst[k].at[mine], lsem.at[k]) for k in range(n)] if "local" in want else []
        sends = over("sends", lambda j, ch, slot, k: copy(src[k].at[c], dst[k].at[mine, c], s1.at[j, k], r1.at[j, k],
                                                          (ch[0], ch[1], c)))
        lands = over("lands", lambda j, ch, slot, k: copy(src[k].at[c], dst[k].at[slot, c], s1.at[j, k], r1.at[j, k],
                                                          (ch[0], ch[1], c)))
        passes = over("passes", lambda j, ch, slot, k: copy(dst[k].at[slot, c], dst[k].at[slot, c], s2.at[j, k],
                                                            r2.at[j, k], (x, y, 1 - c)))
        gets = over("gets", lambda j, ch, slot, k: copy(dst[k].at[slot, 1 - c], dst[k].at[slot, 1 - c], s2.at[j, k],
                                                        r2.at[j, k], (x, y, 1 - c)))
        return local, sends, lands, passes, gets

    def start(self, src, dst, sems):
        lsem, s1, r1, s2, r2 = sems
        x, y, c = _place()
        mine = 2 * x + y
        for k in range(len(src)):
            pltpu.make_async_copy(src[k], dst[k].at[mine], lsem.at[k]).start()
        for j, ch in enumerate(_other_chips(x, y)):
            for k in range(len(src)):
                pltpu.make_async_remote_copy(
                    src_ref=src[k].at[c], dst_ref=dst[k].at[mine, c], send_sem=s1.at[j, k], recv_sem=r1.at[j, k],
                    device_id=(ch[0], ch[1], c), device_id_type=MESH).start()

    def relay(self, src, dst, sems):
        _, _, lands, passes, _ = self._plan(src, dst, sems, ("lands", "passes"))
        for land, fwd in zip(lands, passes):
            land.wait_recv()
            fwd.start()

    def finish(self, src, dst, sems):
        local, sends, _, passes, gets = self._plan(src, dst, sems, ("local", "sends", "passes", "gets"))
        for cp in gets:
            cp.wait_recv()
        for cp in sends + passes:
            cp.wait_send()
        for cp in local:
            cp.wait()

    def wait(self, src, dst, sems):
        self.relay(src, dst, sems)
        self.finish(src, dst, sems)


def _call(body, *, name, args, in_specs, out_specs, out_shape, grid=(), scratch_shapes=(), sem=None, comm=None,
          aliases=None, vmem_mb=48):
    in_specs, out_specs, out_shape = list(in_specs), list(out_specs), list(out_shape)
    scratch, args = list(scratch_shapes), list(args)
    n_in, n_out, n_scr = len(in_specs), len(out_specs), len(scratch)
    if comm is None:
        def kernel_body(*refs):
            body(*refs)
    else:
        c_in, c_out = len(comm.srcs), len(comm.out_shape)

        def kernel_body(*refs):
            o0 = n_in + c_in
            s0 = o0 + n_out + c_out
            cin, cout, sems = refs[n_in:o0], refs[o0 + n_out:s0], refs[s0 + n_scr:]
            main = refs[:n_in] + refs[o0:o0 + n_out] + refs[s0:s0 + n_scr]
            if grid:
                ids = [pl.program_id(a) for a in range(len(grid))]
                first = functools.reduce(lambda p, q: p & q, [i == 0 for i in ids])
                last = functools.reduce(lambda p, q: p & q, [i == g - 1 for i, g in zip(ids, grid)])

                @pl.when(first)
                def _():
                    comm.start(cin, cout, sems)

                body(*main)

                steps = math.prod(grid)
                if hasattr(comm, "relay") and steps >= 4:
                    flat = functools.reduce(lambda p, q: p + q, [i * math.prod(grid[a + 1:]) for a, i in enumerate(ids)])

                    @pl.when(flat == (5 * steps) // 6)
                    def _():
                        comm.relay(cin, cout, sems)

                    @pl.when(last)
                    def _():
                        comm.finish(cin, cout, sems)
                else:
                    @pl.when(last)
                    def _():
                        comm.wait(cin, cout, sems)
            else:
                comm.start(cin, cout, sems)
                body(*main)
                comm.wait(cin, cout, sems)

        hbm = pl.BlockSpec(memory_space=pl.ANY)
        in_specs += [hbm] * c_in
        out_specs += [hbm] * c_out
        out_shape += comm.out_shape
        scratch += comm.scratch
        args += comm.srcs
    params = dict(vmem_limit_bytes=vmem_mb << 20)
    if grid:
        params["dimension_semantics"] = sem
    outs = pl.pallas_call(
        kernel_body, name=name, grid=grid, in_specs=in_specs, out_specs=out_specs, out_shape=out_shape,
        scratch_shapes=scratch, input_output_aliases=aliases or {}, compiler_params=pltpu.CompilerParams(**params),
    )(*args)
    return list(outs[:n_out]), list(outs[n_out:])


def _ffn_fwd(x, nw, wg, wu, wd, name, comm=None):
    T, D = x.shape
    ns, fs, _ = wg.shape
    tm = min(TM_FFN, T)

    def body(x_ref, nw_ref, wg_ref, wu_ref, wd_ref, xo_ref, g_ref, u_ref, h_s, acc_s):
        j = pl.program_id(1)

        @pl.when(j == 0)
        def _():
            xv = x_ref[...]
            h_s[...] = (xv * _rms_r(xv) * nw_ref[...]).astype(BF)
            acc_s[...] = jnp.zeros_like(acc_s)

        h = h_s[...]
        gb = lax.dot_general(h, wg_ref[...], NT_DIMS, preferred_element_type=F32).astype(BF)
        ub = lax.dot_general(h, wu_ref[...], NT_DIMS, preferred_element_type=F32).astype(BF)
        g_ref[...] = gb
        u_ref[...] = ub
        g = gb.astype(F32)
        a = (g * _sigmoid(g) * ub.astype(F32)).astype(BF)
        acc_s[...] += jnp.dot(a, wd_ref[...], preferred_element_type=F32)

        @pl.when(j == ns - 1)
        def _():
            xo_ref[...] = x_ref[...] + 0.5 * acc_s[...]

    wspec = pl.BlockSpec((None, fs, D), lambda i, j: (j, 0, 0))
    return _call(
        body, name=name, grid=(T // tm, ns), args=(x, nw, wg, wu, wd), comm=comm, vmem_mb=56,
        in_specs=[pl.BlockSpec((tm, D), lambda i, j: (i, 0)),
                  pl.BlockSpec((1, D), lambda i, j: (0, 0)),
                  wspec, wspec,
                  pl.BlockSpec((None, fs, D), lambda i, j: (j, 0, 0))],
        out_specs=[pl.BlockSpec((tm, D), lambda i, j: (i, 0)),
                   pl.BlockSpec((None, tm, fs), lambda i, j: (j, i, 0)),
                   pl.BlockSpec((None, tm, fs), lambda i, j: (j, i, 0))],
        out_shape=[_sds((T, D), F32), _sds((ns, T, fs), BF), _sds((ns, T, fs), BF)],
        scratch_shapes=[pltpu.VMEM((tm, D), BF), pltpu.VMEM((tm, D), F32)],
        sem=(ARB, ARB))


def _ffn_bwd_hidden(dxo, x, nw, g, u, wd, name, comm=None):
    T, D = x.shape
    ns, fs, _ = wd.shape
    tm = min(TM_FFN, T)

    def body(dxo_ref, x_ref, nw_ref, g_ref, u_ref, wd_ref, dg_ref, du_ref, a_ref, h_ref, dacc_ref, dacc_s):
        @pl.when(pl.program_id(1) == 0)
        def _():
            xv = x_ref[...]
            h_ref[...] = (xv * _rms_r(xv) * nw_ref[...]).astype(BF)
            db = (0.5 * dxo_ref[...]).astype(BF)
            dacc_ref[...] = db
            dacc_s[...] = db

        da = lax.dot_general(dacc_s[...], wd_ref[...], NT_DIMS, preferred_element_type=F32)
        gv = g_ref[...].astype(F32)
        uv = u_ref[...].astype(F32)
        s = _sigmoid(gv)
        sg = gv * s
        a_ref[...] = (sg * uv).astype(BF)
        du_ref[...] = (da * sg).astype(BF)
        dg_ref[...] = (da * uv * (s * (1.0 + gv * (1.0 - s)))).astype(BF)

    tok = pl.BlockSpec((tm, D), lambda i, j: (i, 0))
    hid = pl.BlockSpec((None, tm, fs), lambda i, j: (j, i, 0))
    return _call(
        body, name=name, grid=(T // tm, ns), args=(dxo, x, nw, g, u, wd), comm=comm, vmem_mb=56,
        in_specs=[tok, tok, pl.BlockSpec((1, D), lambda i, j: (0, 0)), hid, hid,
                  pl.BlockSpec((None, fs, D), lambda i, j: (j, 0, 0))],
        out_specs=[hid, hid, hid, tok, tok],
        out_shape=[_sds((ns, T, fs), BF)] * 3 + [_sds((T, D), BF)] * 2,
        scratch_shapes=[pltpu.VMEM((tm, D), BF)],
        sem=(ARB, ARB))


def _ffn_bwd_resid(dg, du, wg, wu, x, nw, dxo, name, comm=None):
    T, D = x.shape
    ns, fs, _ = wg.shape
    tm = min(TM_FFN, T)

    def body(dg_ref, du_ref, wg_ref, wu_ref, x_ref, nw_ref, dxo_ref, dx_ref, dnw_ref, acc_s):
        i = pl.program_id(0)
        j = pl.program_id(1)
        prod = (jnp.dot(dg_ref[...], wg_ref[...], preferred_element_type=F32)
                + jnp.dot(du_ref[...], wu_ref[...], preferred_element_type=F32))

        @pl.when((i == 0) & (j == 0))
        def _():
            dnw_ref[...] = jnp.zeros_like(dnw_ref)

        @pl.when(j == 0)
        def _():
            acc_s[...] = prod

        @pl.when(j > 0)
        def _():
            acc_s[...] += prod

        @pl.when(j == ns - 1)
        def _():
            dx, dn = _rms_bwd(acc_s[...], x_ref[...], nw_ref[...])
            dx_ref[...] = dxo_ref[...] + dx
            dnw_ref[...] += dn

    tok = pl.BlockSpec((tm, D), lambda i, j: (i, 0))
    row = pl.BlockSpec((1, D), lambda i, j: (0, 0))
    hid = pl.BlockSpec((None, tm, fs), lambda i, j: (j, i, 0))
    wspec = pl.BlockSpec((None, fs, D), lambda i, j: (j, 0, 0))
    return _call(
        body, name=name, grid=(T // tm, ns), args=(dg, du, wg, wu, x, nw, dxo), comm=comm, vmem_mb=56,
        in_specs=[hid, hid, wspec, wspec, tok, row, tok],
        out_specs=[tok, row],
        out_shape=[_sds((T, D), F32), _sds((1, D), F32)],
        scratch_shapes=[pltpu.VMEM((tm, D), F32)],
        sem=(ARB, ARB))


def _tn(a, b, a_spec, b_spec, out_shape, out_spec, grid, name, prev=None, comm=None):
    nk = grid[-1]
    acc_shape = tuple(d for d in out_spec.block_shape if d is not None)

    def body(*refs):
        a_ref, b_ref = refs[0], refs[1]
        o_ref, acc = refs[-2], refs[-1]
        k = pl.program_id(2)
        prod = lax.dot_general(a_ref[...], b_ref[...], TN_DIMS, preferred_element_type=F32)

        @pl.when(k == 0)
        def _():
            acc[...] = prod

        @pl.when(k > 0)
        def _():
            acc[...] += prod

        @pl.when(k == nk - 1)
        def _():
            o_ref[...] = acc[...].astype(o_ref.dtype)

    in_specs = [a_spec, b_spec]
    args = [a, b]
    aliases = {}
    if prev is not None:
        in_specs.append(pl.BlockSpec(memory_space=pl.ANY))
        args.append(prev)
        aliases = {2: 0}
    main, extra = _call(
        body, name=name, grid=grid, args=args, in_specs=in_specs, out_specs=[out_spec], out_shape=[out_shape],
        scratch_shapes=[pltpu.VMEM(acc_shape, F32)], aliases=aliases, sem=(ARB, ARB, ARB), comm=comm)
    return main[0] if comm is None else (main[0], extra)


def _tn_branches(ys, dp, ns, tk, name):
    T, w = ys[0].shape
    D = dp.shape[1] // 3
    dq = D // ns
    nk = T // tk

    def body(yc_ref, yr_ref, ya_ref, b_ref, o_ref, acc):
        i = pl.program_id(0)
        k = pl.program_id(1)

        @pl.when(k == 0)
        def _():
            acc[...] = jnp.zeros_like(acc)

        for branch, y_ref in enumerate((yc_ref, yr_ref, ya_ref)):
            @pl.when(i == branch)
            def _():
                acc[...] += lax.dot_general(y_ref[...], b_ref[...], TN_DIMS, preferred_element_type=F32)

        @pl.when(k == nk - 1)
        def _():
            for s in range(ns):
                o_ref[s] = acc[:, s * dq:(s + 1) * dq].astype(o_ref.dtype)

    yspec = pl.BlockSpec((tk, w), lambda i, k: (k, 0))
    return pl.pallas_call(
        body, name=name, grid=(3, nk),
        in_specs=[yspec, yspec, yspec, pl.BlockSpec((tk, D), lambda i, k: (k, i))],
        out_specs=pl.BlockSpec((ns, None, w, dq), lambda i, k: (0, i, 0, 0)),
        out_shape=_sds((ns, 3, w, dq), BF),
        scratch_shapes=[pltpu.VMEM((w, D), F32)],
        compiler_params=_cp((PAR, ARB)),
    )(*ys, dp)


def _tn_gates(h, dbig, ns, tk, name):
    T, D = h.shape
    dq = D // ns
    nk = T // tk

    def body(a_ref, b_ref, o_ref, acc):
        k = pl.program_id(1)
        prod = lax.dot_general(a_ref[...], b_ref[...], TN_DIMS, preferred_element_type=F32)

        @pl.when(k == 0)
        def _():
            acc[...] = prod

        @pl.when(k > 0)
        def _():
            acc[...] += prod

        @pl.when(k == nk - 1)
        def _():
            for s in range(ns):
                o_ref[s] = acc[s * dq:(s + 1) * dq, :].astype(o_ref.dtype)

    return pl.pallas_call(
        body, name=name, grid=(3, nk),
        in_specs=[pl.BlockSpec((tk, D), lambda q, k: (k, 0)), pl.BlockSpec((tk, D), lambda q, k: (k, q))],
        out_specs=pl.BlockSpec((ns, None, dq, D), lambda q, k: (0, q, 0, 0)),
        out_shape=_sds((ns, 3, dq, D), BF),
        scratch_shapes=[pltpu.VMEM((D, D), F32)],
        compiler_params=_cp((PAR, ARB)),
    )(h, dbig)


def _inproj_fwd(x, nw, wbig, name):
    T, D = x.shape
    nb = wbig.shape[-1]
    tm = min(2 * TM, T)
    bn = min(2048, nb)

    def body(x_ref, nw_ref, w_ref, o_ref, h_ref, h_s):
        @pl.when(pl.program_id(1) == 0)
        def _():
            xv = x_ref[...]
            hb = (xv * _rms_r(xv) * nw_ref[...]).astype(BF)
            h_s[...] = hb
            h_ref[...] = hb

        o_ref[...] = jnp.dot(h_s[...], w_ref[...], preferred_element_type=F32).astype(BF)

    return pl.pallas_call(
        body, name=name, grid=(T // tm, nb // bn),
        in_specs=[pl.BlockSpec((tm, D), lambda i, n: (i, 0)),
                  pl.BlockSpec((1, D), lambda i, n: (0, 0)),
                  pl.BlockSpec((D, bn), lambda i, n: (0, n))],
        out_specs=[pl.BlockSpec((tm, bn), lambda i, n: (i, n)),
                   pl.BlockSpec((tm, D), lambda i, n: (i, 0))],
        out_shape=[_sds((T, nb), BF), _sds((T, D), BF)],
        scratch_shapes=[pltpu.VMEM((tm, D), BF)],
        compiler_params=_cp((PAR, ARB)),
    )(x, nw, wbig)


def _inproj_bwd(dbig, wbig, x, nw, dxin, name, comm=None):
    T, D = x.shape
    nb = wbig.shape[-1]
    tm = min(TM_FFN, T)
    tk = min(2048, nb)
    nk = nb // tk

    def body(a_ref, w_ref, x_ref, nw_ref, dxin_ref, dx_ref, dnw_ref, acc_s):
        i = pl.program_id(0)
        k = pl.program_id(1)
        prod = lax.dot_general(a_ref[...], w_ref[...], NT_DIMS, preferred_element_type=F32)

        @pl.when((i == 0) & (k == 0))
        def _():
            dnw_ref[...] = jnp.zeros_like(dnw_ref)

        @pl.when(k == 0)
        def _():
            acc_s[...] = prod

        @pl.when(k > 0)
        def _():
            acc_s[...] += prod

        @pl.when(k == nk - 1)
        def _():
            dx, dn = _rms_bwd(acc_s[...], x_ref[...], nw_ref[...])
            dx_ref[...] = dxin_ref[...] + dx
            dnw_ref[...] += dn

    tok = pl.BlockSpec((tm, D), lambda i, k: (i, 0))
    row = pl.BlockSpec((1, D), lambda i, k: (0, 0))
    return _call(
        body, name=name, grid=(T // tm, nk), args=(dbig, wbig, x, nw, dxin), comm=comm, vmem_mb=56,
        in_specs=[pl.BlockSpec((tm, tk), lambda i, k: (i, k)),
                  pl.BlockSpec((D, tk), lambda i, k: (0, k)),
                  tok, row, tok],
        out_specs=[tok, row],
        out_shape=[_sds((T, D), F32), _sds((1, D), F32)],
        scratch_shapes=[pltpu.VMEM((tm, D), F32)],
        sem=(ARB, ARB))


CONV_R = 512
CONV_BASE, CONV_GROUP = 0, 3
ATT_BASE, ATT_GROUP = 12, 3
RET_BASE, RET_GROUP = 24, 4
N_SEG = 10


N_IN_BLOCKS = N_SEG * BRANCH_W // LANE


def _orig_block(p):
    nblk = BRANCH_W // LANE
    qa, qr = p - ATT_BASE, p - RET_BASE
    conv = (p % CONV_GROUP) * nblk + p // CONV_GROUP
    att = (7 + qa % ATT_GROUP) * nblk + qa // ATT_GROUP
    ret = (3 + qr % RET_GROUP) * nblk + qr // RET_GROUP
    return jnp.where(p < ATT_BASE, conv, jnp.where(p < RET_BASE, att, ret))


def _copy_blocks(src, in_spec, out_shape, out_spec, grid, name, prev=None):
    def body(*refs):
        refs[-1][...] = refs[0][...]

    in_specs, args, aliases = [in_spec], [src], {}
    if prev is not None:
        in_specs.append(pl.BlockSpec(memory_space=pl.ANY))
        args.append(prev)
        aliases = {1: 0}
    return pl.pallas_call(
        body, name=name, grid=grid, in_specs=in_specs, out_specs=out_spec, out_shape=out_shape,
        input_output_aliases=aliases, compiler_params=_cp(tuple(PAR for _ in grid)),
    )(*args)


def _build_wbig(gates4, win4, name):
    ns, _, dq, D = gates4.shape
    per = win4.shape[-1] // LANE
    shape = _sds((D, 3 * D + N_IN_BLOCKS * LANE), gates4.dtype)
    out = _copy_blocks(gates4, pl.BlockSpec((None, None, dq, D), lambda s, i: (s, i, 0, 0)), shape,
                       pl.BlockSpec((dq, D), lambda s, i: (s, i)), (ns, 3), name + "_gates")
    return _copy_blocks(
        win4, pl.BlockSpec((None, D, LANE), lambda p: (_orig_block(p) // per, 0, _orig_block(p) % per)), shape,
        pl.BlockSpec((D, LANE), lambda p: (0, 3 * D // LANE + p)), (N_IN_BLOCKS,), name + "_in", prev=out)


def _ungroup_dw_in(dwp, ns, name):
    D = dwp.shape[0]
    per = N_IN_BLOCKS // ns
    return _copy_blocks(
        dwp, pl.BlockSpec((D, LANE), lambda p: (0, p)), _sds((ns, D, per * LANE), dwp.dtype),
        pl.BlockSpec((None, D, LANE), lambda p: (_orig_block(p) // per, 0, _orig_block(p) % per)), (N_IN_BLOCKS,), name)


def _seg0(big):
    return (big.shape[1] - N_SEG * BRANCH_W) // LANE


def _group_spec(big, base, group, rows, where):
    first = (_seg0(big) + base) // group
    assert first * group == _seg0(big) + base

    def index(*ids):
        r, g = where(*ids)
        return r, first + g

    return pl.BlockSpec((rows, group * LANE), index)


CU, CB, CC = (slice(k * LANE, (k + 1) * LANE) for k in range(3))
AQ, AK, AV = CU, CB, CC
RQ, RK, RV, RG = (slice(k * LANE, (k + 1) * LANE) for k in range(4))


def _conv_fwd(big, cw, name):
    T = big.shape[0]
    R = min(CONV_R, T)

    def body(g_ref, w_ref, y_ref, z_s):
        z_s[pl.ds(0, 8), :] = jnp.zeros((8, LANE), F32)

        def fill(t, c):
            sl = pl.ds(pl.multiple_of(t * R, R), R)
            z_s[pl.ds(pl.multiple_of(t * R + 8, 8), R), :] = g_ref[sl, CC].astype(F32) * g_ref[sl, CU].astype(F32)
            return c

        lax.fori_loop(0, T // R, fill, 0)
        w0, w1, w2 = w_ref[0:1, :], w_ref[1:2, :], w_ref[2:3, :]

        def step(t, c):
            zz = z_s[pl.ds(pl.multiple_of(t * R, R), R + 8), :]
            z0 = zz[8:]
            z1 = pltpu.roll(zz, 1, 0)[8:]
            z2 = pltpu.roll(zz, 2, 0)[8:]
            sl = pl.ds(pl.multiple_of(t * R, R), R)
            y_ref[sl, :] = (g_ref[sl, CB].astype(F32) * (w2 * z0 + w1 * z1 + w0 * z2)).astype(BF)
            return c

        lax.fori_loop(0, T // R, step, 0)

    return pl.pallas_call(
        body, name=name, grid=(BRANCH_W // LANE,),
        in_specs=[_group_spec(big, CONV_BASE, CONV_GROUP, T, lambda j: (0, j)),
                  pl.BlockSpec((3, LANE), lambda j: (0, j))],
        out_specs=pl.BlockSpec((T, LANE), lambda j: (0, j)),
        out_shape=_sds((T, BRANCH_W), BF),
        scratch_shapes=[pltpu.VMEM((T + 8, LANE), F32)],
        compiler_params=_cp((PAR,)),
    )(big, cw)


def _conv_bwd(big, dy, cw, dbig, name):
    T = big.shape[0]
    R = min(CONV_R, T)

    def body(g_ref, dy_ref, w_ref, _, o_ref, dw_ref, z_s, d_s):
        z_s[pl.ds(0, 8), :] = jnp.zeros((8, LANE), F32)
        d_s[pl.ds(T, 8), :] = jnp.zeros((8, LANE), F32)

        def fill(t, c):
            sl = pl.ds(pl.multiple_of(t * R, R), R)
            z_s[pl.ds(pl.multiple_of(t * R + 8, 8), R), :] = g_ref[sl, CC].astype(F32) * g_ref[sl, CU].astype(F32)
            d_s[sl, :] = dy_ref[sl, :].astype(F32) * g_ref[sl, CB].astype(F32)
            return c

        lax.fori_loop(0, T // R, fill, 0)
        w0, w1, w2 = w_ref[0:1, :], w_ref[1:2, :], w_ref[2:3, :]

        def step(t, carry):
            a0, a1, a2 = carry
            zz = z_s[pl.ds(pl.multiple_of(t * R, R), R + 8), :]
            z0 = zz[8:]
            z1 = pltpu.roll(zz, 1, 0)[8:]
            z2 = pltpu.roll(zz, 2, 0)[8:]
            sl = pl.ds(pl.multiple_of(t * R, R), R)
            dyv = dy_ref[sl, :].astype(F32)
            o_ref[sl, CB] = (dyv * (w2 * z0 + w1 * z1 + w0 * z2)).astype(BF)
            dd = d_s[pl.ds(pl.multiple_of(t * R, R), R + 8), :]
            d0 = dd[:R]
            d1 = pltpu.roll(dd, R + 7, 0)[:R]
            d2 = pltpu.roll(dd, R + 6, 0)[:R]
            dz = w2 * d0 + w1 * d1 + w0 * d2
            o_ref[sl, CC] = (dz * g_ref[sl, CU].astype(F32)).astype(BF)
            o_ref[sl, CU] = (dz * g_ref[sl, CC].astype(F32)).astype(BF)
            a0 = a0 + jnp.sum(d0 * z2, axis=0, keepdims=True)
            a1 = a1 + jnp.sum(d0 * z1, axis=0, keepdims=True)
            a2 = a2 + jnp.sum(d0 * z0, axis=0, keepdims=True)
            return a0, a1, a2

        zero = jnp.zeros((1, LANE), F32)
        a0, a1, a2 = lax.fori_loop(0, T // R, step, (zero, zero, zero))
        dw_ref[0:1, :] = a0
        dw_ref[1:2, :] = a1
        dw_ref[2:3, :] = a2

    group = _group_spec(big, CONV_BASE, CONV_GROUP, T, lambda j: (0, j))
    w = pl.BlockSpec((3, LANE), lambda j: (0, j))
    return pl.pallas_call(
        body, name=name, grid=(BRANCH_W // LANE,),
        in_specs=[group, pl.BlockSpec((T, LANE), lambda j: (0, j)), w, pl.BlockSpec(memory_space=pl.ANY)],
        out_specs=[group, w],
        out_shape=[_sds(dbig.shape, BF), _sds((3, BRANCH_W), F32)],
        scratch_shapes=[pltpu.VMEM((T + 8, LANE), F32), pltpu.VMEM((T + 8, LANE), F32)],
        input_output_aliases={3: 0}, compiler_params=_cp((PAR,)),
    )(big, dy, cw, dbig)


def _ret_tables(T):
    L = min(RET_L, T)
    hh = jnp.arange(H_RET, dtype=F32)
    lg = jnp.log1p(-jnp.exp2(-5.0 - hh))
    n = jnp.arange(L, dtype=F32)
    a = jnp.exp(lg[:, None] * (n + 1.0))
    b = jnp.exp(lg[:, None] * (L - 1.0 - n))
    gl = jnp.exp(lg * L)
    ch = jnp.arange(L) // CHUNK
    m = jnp.exp(lg[:, None, None] * jnp.abs(n[:, None] - n[None, :])) * (ch[None, :] <= ch[:, None]).astype(F32)
    inv_freq = ROPE_BASE ** (-jnp.linspace(0.0, 1.0, DK_RET // 2, dtype=F32))
    ang = jnp.arange(T, dtype=F32)[:, None] * inv_freq[None, :]
    cos, sin = jnp.cos(ang), jnp.sin(ang)
    return dict(
        L=L, M=m,
        a=jnp.broadcast_to(a[:, :, None], (H_RET, L, DK_RET)),
        b=jnp.broadcast_to(b[:, :, None], (H_RET, L, DK_RET)),
        gl=jnp.broadcast_to(gl[:, None, None], (H_RET, 1, DK_RET)),
        cos=jnp.concatenate([cos, cos], axis=-1), sin=jnp.concatenate([-sin, sin], axis=-1))


def _rot(x, cs, sn):
    return x * cs + pltpu.roll(x, DK_RET // 2, 1) * sn


def _unrot(dy, cs, sn):
    return dy * cs + pltpu.roll(dy * sn, DK_RET // 2, 1)


def _ret_fwd(big, tb, name, comm=None):
    T = big.shape[0]
    L = tb["L"]
    nsc = T // L
    scale = DK_RET ** -0.5

    def body(x_ref, cos_ref, sin_ref, m_ref, a_ref, b_ref, gl_ref, y_ref, o_ref, st_ref, s_s):
        @pl.when(pl.program_id(1) == 0)
        def _():
            s_s[...] = jnp.zeros_like(s_s)

        cs, sn = cos_ref[...], sin_ref[...]
        qt = _rot(x_ref[:, RQ].astype(F32), cs, sn) * scale
        kt = _rot(x_ref[:, RK].astype(F32), cs, sn)
        qb, kb, vb = qt.astype(BF), kt.astype(BF), x_ref[:, RV]
        s_prev = s_s[...]
        st_ref[...] = s_prev
        p = lax.dot_general(qb, kb, NT_DIMS, preferred_element_type=F32) * m_ref[...]
        o = (jnp.dot(p.astype(BF), vb, preferred_element_type=F32)
             + jnp.dot((qt * a_ref[...]).astype(BF), s_prev.astype(BF), preferred_element_type=F32))
        s_s[...] = s_prev * gl_ref[...] + lax.dot_general((kt * b_ref[...]).astype(BF), vb, TN_DIMS,
                                                         preferred_element_type=F32)
        o_ref[...] = o
        gv = x_ref[:, RG].astype(F32)
        y_ref[...] = (gv * _sigmoid(gv) * o * _rms_r(o)).astype(BF)

    tab = pl.BlockSpec((L, DK_RET), lambda h, i: (i, 0))
    per_head = pl.BlockSpec((None, L, DK_RET), lambda h, i: (h, 0, 0))
    out = pl.BlockSpec((L, LANE), lambda h, i: (i, h))
    return _call(
        body, name=name, grid=(H_RET, nsc), comm=comm,
        args=(big, tb["cos"], tb["sin"], tb["M"], tb["a"], tb["b"], tb["gl"]),
        in_specs=[_group_spec(big, RET_BASE, RET_GROUP, L, lambda h, i: (i, h)), tab, tab,
                  pl.BlockSpec((None, L, L), lambda h, i: (h, 0, 0)), per_head, per_head,
                  pl.BlockSpec((None, 1, DK_RET), lambda h, i: (h, 0, 0))],
        out_specs=[out, out, pl.BlockSpec((None, None, DK_RET, DK_RET), lambda h, i: (i, h, 0, 0))],
        out_shape=[_sds((T, BRANCH_W), BF), _sds((T, BRANCH_W), F32), _sds((nsc, H_RET, DK_RET, DK_RET), F32)],
        scratch_shapes=[pltpu.VMEM((DK_RET, DK_RET), F32)],
        sem=(ARB, ARB))


def _ret_bwd(big, o, st, dy, tb, dbig, name):
    T = big.shape[0]
    L = tb["L"]
    nsc = T // L
    scale = DK_RET ** -0.5

    def body(x_ref, cos_ref, sin_ref, m_ref, a_ref, b_ref, gl_ref, o_ref, st_ref, dy_ref, _, d_ref, ds_s):
        @pl.when(pl.program_id(1) == 0)
        def _():
            ds_s[...] = jnp.zeros_like(ds_s)

        cs, sn = cos_ref[...], sin_ref[...]
        mm, av, bv = m_ref[...], a_ref[...], b_ref[...]
        qt = _rot(x_ref[:, RQ].astype(F32), cs, sn) * scale
        kt = _rot(x_ref[:, RK].astype(F32), cs, sn)
        qb, kb, vb = qt.astype(BF), kt.astype(BF), x_ref[:, RV]
        pb = (lax.dot_general(qb, kb, NT_DIMS, preferred_element_type=F32) * mm).astype(BF)
        ov = o_ref[...]
        r = _rms_r(ov)
        oh = ov * r
        gv = x_ref[:, RG].astype(F32)
        sg = _sigmoid(gv)
        dyv = dy_ref[...].astype(F32)
        d_ref[:, RG] = (dyv * oh * (sg * (1.0 + gv * (1.0 - sg)))).astype(BF)
        doh = dyv * gv * sg
        dob = (r * (doh - oh * jnp.mean(doh * oh, axis=-1, keepdims=True))).astype(BF)
        dsb = ds_s[...].astype(BF)
        spb = st_ref[...].astype(BF)
        dpb = (lax.dot_general(dob, vb, NT_DIMS, preferred_element_type=F32) * mm).astype(BF)
        dqt = (jnp.dot(dpb, kb, preferred_element_type=F32)
               + lax.dot_general(dob, spb, NT_DIMS, preferred_element_type=F32) * av)
        dkt = (lax.dot_general(dpb, qb, TN_DIMS, preferred_element_type=F32)
               + lax.dot_general(vb, dsb, NT_DIMS, preferred_element_type=F32) * bv)
        dv = (lax.dot_general(pb, dob, TN_DIMS, preferred_element_type=F32)
              + jnp.dot((kt * bv).astype(BF), dsb, preferred_element_type=F32))
        ds_s[...] = ds_s[...] * gl_ref[...] + lax.dot_general((qt * av).astype(BF), dob, TN_DIMS,
                                                              preferred_element_type=F32)
        d_ref[:, RQ] = (_unrot(dqt, cs, sn) * scale).astype(BF)
        d_ref[:, RK] = _unrot(dkt, cs, sn).astype(BF)
        d_ref[:, RV] = dv.astype(BF)

    def rev(i):
        return nsc - 1 - i

    group = _group_spec(big, RET_BASE, RET_GROUP, L, lambda h, i: (rev(i), h))
    tab = pl.BlockSpec((L, DK_RET), lambda h, i: (rev(i), 0))
    per_head = pl.BlockSpec((None, L, DK_RET), lambda h, i: (h, 0, 0))
    out = pl.BlockSpec((L, LANE), lambda h, i: (rev(i), h))
    return pl.pallas_call(
        body, name=name, grid=(H_RET, nsc),
        in_specs=[group, tab, tab,
                  pl.BlockSpec((None, L, L), lambda h, i: (h, 0, 0)), per_head, per_head,
                  pl.BlockSpec((None, 1, DK_RET), lambda h, i: (h, 0, 0)),
                  out, pl.BlockSpec((None, None, DK_RET, DK_RET), lambda h, i: (rev(i), h, 0, 0)), out,
                  pl.BlockSpec(memory_space=pl.ANY)],
        out_specs=group,
        out_shape=_sds(dbig.shape, BF),
        scratch_shapes=[pltpu.VMEM((DK_RET, DK_RET), F32)],
        input_output_aliases={10: 0}, compiler_params=_cp((PAR, ARB)),
    )(big, tb["cos"], tb["sin"], tb["M"], tb["a"], tb["b"], tb["gl"], o, st, dy, dbig)


def _relbias_onehot(n):
    mm = lax.broadcasted_iota(jnp.int32, (RB_PAD, ATT_TOEP), 1)
    rr = lax.broadcasted_iota(jnp.int32, (RB_PAD, ATT_TOEP), 0)
    idx = jnp.clip(n + ATT_TOEP - mm, 0, 2 * REL_CLIP)
    return (rr == idx).astype(F32)


def _split3(x):
    hi = x.astype(BF).astype(F32)
    mid = (x - hi).astype(BF).astype(F32)
    lo = x - hi - mid
    return jnp.concatenate([hi, mid, lo], axis=0).astype(BF)


def _join3(y):
    k = y.shape[0] // 3
    return (y[:k] + y[k:2 * k]) + y[2 * k:]


def _relbias_expand(rel_bias, name, comm=None):
    far = ATT_SPAN - ATT_TOEP
    n_layers = rel_bias.shape[0]
    rbp = jnp.pad(rel_bias, ((0, 0), (0, 0), (0, RB_PAD - N_REL)))

    def body(rb_ref, o_ref):
        for l in range(n_layers):
            rb = rb_ref[l]
            const = jnp.broadcast_to(rb[:, 2 * REL_CLIP:2 * REL_CLIP + 1], (H_ATT, far))
            rb3 = _split3(rb)

            def row(n, c):
                toep = _join3(jnp.dot(rb3, _relbias_onehot(n).astype(BF), preferred_element_type=F32))
                m = lax.broadcasted_iota(jnp.int32, (1, ATT_SPAN), 1)
                d = n // CHUNK + N_PREV - m // CHUNK
                neg = jnp.where((d >= 0) & (d <= N_PREV), 0.0, NEG_INF).astype(F32)
                o_ref[l, n] = jnp.concatenate([const, toep], axis=1) + neg
                return c

            lax.fori_loop(0, ATT_TQ, row, 0)

    (out,), extra = _call(
        body, name=name, args=(rbp,), comm=comm,
        in_specs=[pl.BlockSpec(memory_space=pltpu.VMEM)],
        out_specs=[pl.BlockSpec(memory_space=pltpu.VMEM)],
        out_shape=[_sds((n_layers, ATT_TQ, H_ATT, ATT_SPAN), F32)])
    return jnp.transpose(out, (0, 2, 1, 3)), extra


def _relbias_grad(dbt, name):
    far = ATT_SPAN - ATT_TOEP

    def body(d_ref, o_ref):
        def row(n, carry):
            acc, cs = carry
            dn = d_ref[n]
            acc = acc + _join3(lax.dot_general(_split3(dn[:, far:]), _relbias_onehot(n).astype(BF), NT_DIMS,
                                               preferred_element_type=F32))
            cs = cs + jnp.sum(dn[:, :far], axis=1, keepdims=True)
            return acc, cs

        acc, cs = lax.fori_loop(0, ATT_TQ, row, (jnp.zeros((H_ATT, RB_PAD), F32), jnp.zeros((H_ATT, 1), F32)))
        rr = lax.broadcasted_iota(jnp.int32, (H_ATT, RB_PAD), 1)
        o_ref[...] = acc + jnp.where(rr == 2 * REL_CLIP, cs, 0.0)

    return pl.pallas_call(
        body, name=name,
        in_specs=[pl.BlockSpec(memory_space=pltpu.VMEM)],
        out_specs=pl.BlockSpec(memory_space=pltpu.VMEM),
        out_shape=_sds((H_ATT, RB_PAD), F32),
    )(dbt)


def _att_pad_fill(dst_s, src_ref, cols, T):
    dst_s[pl.ds(0, ATT_PAD), :] = jnp.zeros((ATT_PAD, LANE), dst_s.dtype)
    R = min(512, T)

    def cp(t, c):
        dst_s[pl.ds(pl.multiple_of(ATT_PAD + t * R, LANE), R), :] = src_ref[pl.ds(pl.multiple_of(t * R, R), R), cols]
        return c

    lax.fori_loop(0, T // R, cp, 0)


ATT_WIN = ATT_SUB * ATT_TQ + ATT_PAD


def _att_probs(s_full, sub, bias, t0):
    s = s_full[sub * ATT_TQ:(sub + 1) * ATT_TQ, sub * ATT_TQ:sub * ATT_TQ + ATT_SPAN] * (DH_ATT ** -0.5) + bias
    key_pos = t0 + sub * ATT_TQ - ATT_PAD + lax.broadcasted_iota(jnp.int32, (1, ATT_SPAN), 1)
    s = jnp.where(key_pos >= 0, s, NEG_INF)
    p = jnp.exp(s - jnp.max(s, axis=-1, keepdims=True))
    return p * (1.0 / jnp.sum(p, axis=-1, keepdims=True))


def _att_band(tiles):
    rows = []
    for sub, t in enumerate(tiles):
        parts = []
        if sub:
            parts.append(jnp.zeros((ATT_TQ, sub * ATT_TQ), BF))
        parts.append(t)
        if sub < ATT_SUB - 1:
            parts.append(jnp.zeros((ATT_TQ, (ATT_SUB - 1 - sub) * ATT_TQ), BF))
        rows.append(jnp.concatenate(parts, axis=1))
    return jnp.concatenate(rows, axis=0)


def _att_head_masks(x):
    first = lax.broadcasted_iota(jnp.int32, (1, LANE), 1) < DH_ATT
    zero = jnp.zeros_like(x)
    return first, (jnp.where(first, x, zero), jnp.where(first, zero, x))


def _att_fwd(big, bias, name, comm=None):
    T = big.shape[0]
    rows = ATT_SUB * ATT_TQ
    nt = T // rows

    def body(x_ref, b_ref, y_ref, kp_s, vp_s):
        i = pl.program_id(1)

        @pl.when(i == 0)
        def _():
            _att_pad_fill(kp_s, x_ref, AK, T)
            _att_pad_fill(vp_s, x_ref, AV, T)

        t0 = pl.multiple_of(i * rows, rows)
        kw = kp_s[pl.ds(t0, ATT_WIN), :]
        vw = vp_s[pl.ds(t0, ATT_WIN), :]
        first, qm = _att_head_masks(x_ref[pl.ds(t0, rows), AQ])
        outs = []
        for hh in range(2):
            s_full = lax.dot_general(qm[hh], kw, NT_DIMS, preferred_element_type=F32)
            band = _att_band([_att_probs(s_full, sub, b_ref[hh], t0).astype(BF) for sub in range(ATT_SUB)])
            outs.append(jnp.dot(band, vw, preferred_element_type=F32))
        y_ref[...] = jnp.where(first, outs[0], outs[1]).astype(BF)

    return _call(
        body, name=name, grid=(H_ATT // 2, nt), args=(big, bias), comm=comm,
        in_specs=[_group_spec(big, ATT_BASE, ATT_GROUP, T, lambda p, i: (0, p)),
                  pl.BlockSpec((2, ATT_TQ, ATT_SPAN), lambda p, i: (p, 0, 0))],
        out_specs=[pl.BlockSpec((rows, LANE), lambda p, i: (i, p))],
        out_shape=[_sds((T, BRANCH_W), BF)],
        scratch_shapes=[pltpu.VMEM((T + ATT_PAD, LANE), BF), pltpu.VMEM((T + ATT_PAD, LANE), BF)],
        sem=(ARB, ARB))


def _att_bwd(big, bias, dy, dbig, name, comm=None):
    T = big.shape[0]
    rows = ATT_SUB * ATT_TQ
    nt = T // rows
    scale = DH_ATT ** -0.5

    def body(x_ref, b_ref, dy_ref, _, d_ref, db_ref, kp_s, vp_s, dk_s, dv_s):
        i = pl.program_id(1)

        @pl.when(i == 0)
        def _():
            _att_pad_fill(kp_s, x_ref, AK, T)
            _att_pad_fill(vp_s, x_ref, AV, T)
            dk_s[...] = jnp.zeros_like(dk_s)
            dv_s[...] = jnp.zeros_like(dv_s)
            db_ref[...] = jnp.zeros_like(db_ref)

        t0 = pl.multiple_of(i * rows, rows)
        win = pl.ds(t0, ATT_WIN)
        kw = kp_s[win, :]
        vw = vp_s[win, :]
        first, qm = _att_head_masks(x_ref[pl.ds(t0, rows), AQ])
        _, dom = _att_head_masks(dy_ref[...])
        dqs, dkt, dvt = [], None, None
        for hh in range(2):
            s_full = lax.dot_general(qm[hh], kw, NT_DIMS, preferred_element_type=F32)
            dp_full = lax.dot_general(dom[hh], vw, NT_DIMS, preferred_element_type=F32)
            ps, dss, db = [], [], None
            for sub in range(ATT_SUB):
                pn = _att_probs(s_full, sub, b_ref[hh], t0)
                dp = dp_full[sub * ATT_TQ:(sub + 1) * ATT_TQ, sub * ATT_TQ:sub * ATT_TQ + ATT_SPAN]
                ds = pn * (dp - jnp.sum(dp * pn, axis=-1, keepdims=True))
                db = ds if db is None else db + ds
                ps.append(pn.astype(BF))
                dss.append(ds.astype(BF))
            db_ref[hh] += db
            ds_band, p_band = _att_band(dss), _att_band(ps)
            dqs.append(jnp.dot(ds_band, kw, preferred_element_type=F32))
            qt = jnp.transpose(qm[hh].astype(F32)).astype(BF)
            dot_ = jnp.transpose(dom[hh].astype(F32)).astype(BF)
            dk_h = jnp.dot(qt, ds_band, preferred_element_type=F32)
            dv_h = jnp.dot(dot_, p_band, preferred_element_type=F32)
            dkt = dk_h if dkt is None else dkt + dk_h
            dvt = dv_h if dvt is None else dvt + dv_h
        d_ref[pl.ds(t0, rows), AQ] = (jnp.where(first, dqs[0], dqs[1]) * scale).astype(BF)
        dk_s[win, :] += jnp.transpose(dkt) * scale
        dv_s[win, :] += jnp.transpose(dvt)

        @pl.when(i == nt - 1)
        def _():
            R = min(512, T)

            def cp(t, c):
                src = pl.ds(pl.multiple_of(ATT_PAD + t * R, LANE), R)
                dst = pl.ds(pl.multiple_of(t * R, R), R)
                d_ref[dst, AK] = dk_s[src, :].astype(BF)
                d_ref[dst, AV] = dv_s[src, :].astype(BF)
                return c

            lax.fori_loop(0, T // R, cp, 0)

    group = _group_spec(big, ATT_BASE, ATT_GROUP, T, lambda p, i: (0, p))
    tile = pl.BlockSpec((rows, LANE), lambda p, i: (i, p))
    bspec = pl.BlockSpec((2, ATT_TQ, ATT_SPAN), lambda p, i: (p, 0, 0))
    return _call(
        body, name=name, grid=(H_ATT // 2, nt), args=(big, bias, dy, dbig), comm=comm, aliases={3: 0}, vmem_mb=56,
        in_specs=[group, bspec, tile, pl.BlockSpec(memory_space=pl.ANY)],
        out_specs=[group, bspec],
        out_shape=[_sds(dbig.shape, BF), _sds((H_ATT, ATT_TQ, ATT_SPAN), F32)],
        scratch_shapes=[pltpu.VMEM((T + ATT_PAD, LANE), BF), pltpu.VMEM((T + ATT_PAD, LANE), BF),
                        pltpu.VMEM((T + ATT_PAD, LANE), F32), pltpu.VMEM((T + ATT_PAD, LANE), F32)],
        sem=(ARB, ARB))


def _merge_fwd(x1, big, ys, wb, wo, name):
    T, D = x1.shape
    tm = min(TM, T)

    def body(x_ref, gp_ref, yc_ref, yr_ref, ya_ref, wb_ref, wo_ref, x2_ref, p_ref, mg_ref):
        merged = jnp.zeros((tm, D), F32)
        for i, y_ref in enumerate((yc_ref, yr_ref, ya_ref)):
            cols = slice(i * D, (i + 1) * D)
            pb = jnp.dot(y_ref[...], wb_ref[i], preferred_element_type=F32).astype(BF)
            p_ref[:, cols] = pb
            merged = merged + _sigmoid(gp_ref[:, cols].astype(F32)) * pb.astype(F32)
        mb = merged.astype(BF)
        mg_ref[...] = mb
        x2_ref[...] = x_ref[...] + jnp.dot(mb, wo_ref[...], preferred_element_type=F32)

    tok = pl.BlockSpec((tm, D), lambda i: (i, 0))
    wide = pl.BlockSpec((tm, 3 * D), lambda i: (i, 0))
    yspec = pl.BlockSpec((tm, BRANCH_W), lambda i: (i, 0))
    return pl.pallas_call(
        body, name=name, grid=(T // tm,),
        in_specs=[tok, wide, yspec, yspec, yspec,
                  pl.BlockSpec((3, BRANCH_W, D), lambda i: (0, 0, 0)),
                  pl.BlockSpec((D, D), lambda i: (0, 0))],
        out_specs=[tok, wide, tok],
        out_shape=[_sds((T, D), F32), _sds((T, 3 * D), BF), _sds((T, D), BF)],
        compiler_params=_cp((PAR,)),
    )(x1, big, *ys, wb, wo)


def _merge_bwd(dx2, big, p, wb, wo, name):
    T, D = dx2.shape
    tm = min(TM, T)

    def body(dx_ref, gp_ref, p_ref, wb_ref, wo_ref, dp_ref, dgp_ref, dyc_ref, dyr_ref, dya_ref, dxb_ref):
        dxb = dx_ref[...].astype(BF)
        dxb_ref[...] = dxb
        dm = lax.dot_general(dxb, wo_ref[...], NT_DIMS, preferred_element_type=F32)
        for i, dy_ref in enumerate((dyc_ref, dyr_ref, dya_ref)):
            cols = slice(i * D, (i + 1) * D)
            gt = _sigmoid(gp_ref[:, cols].astype(F32))
            dpb = (dm * gt).astype(BF)
            dp_ref[:, cols] = dpb
            dgp_ref[:, cols] = (dm * p_ref[:, cols].astype(F32) * gt * (1.0 - gt)).astype(BF)
            dy_ref[...] = lax.dot_general(dpb, wb_ref[i], NT_DIMS, preferred_element_type=F32).astype(BF)

    tok = pl.BlockSpec((tm, D), lambda i: (i, 0))
    wide = pl.BlockSpec((tm, 3 * D), lambda i: (i, 0))
    yspec = pl.BlockSpec((tm, BRANCH_W), lambda i: (i, 0))
    return pl.pallas_call(
        body, name=name, grid=(T // tm,),
        in_specs=[tok, wide, wide,
                  pl.BlockSpec((3, BRANCH_W, D), lambda i: (0, 0, 0)),
                  pl.BlockSpec((D, D), lambda i: (0, 0))],
        out_specs=[wide, wide, yspec, yspec, yspec, tok],
        out_shape=[_sds((T, 3 * D), BF), _sds(big.shape, BF)] + [_sds((T, BRANCH_W), BF)] * 3 + [_sds((T, D), BF)],
        compiler_params=_cp((PAR,)),
    )(dx2, big, p, wb, wo)


def _loss_head(x, tgt, fw, name):
    T, D = x.shape
    tm = min(TM, T)

    def body(x_ref, t_ref, w_ref, loss_ref, dx_ref, dw_ref):
        @pl.when(pl.program_id(0) == 0)
        def _():
            loss_ref[...] = jnp.zeros_like(loss_ref)
            dw_ref[...] = jnp.zeros_like(dw_ref)

        xv = x_ref[...]
        wv = w_ref[...]
        e = xv * _rms_r(xv) * wv - t_ref[...]
        loss_ref[...] += 0.5 * jnp.sum(jnp.mean(e * e, axis=-1, keepdims=True))
        dx, dn = _rms_bwd(e * (1.0 / D), xv, wv)
        dx_ref[...] = dx
        dw_ref[...] += dn

    tok = pl.BlockSpec((tm, D), lambda i: (i, 0))
    return pl.pallas_call(
        body, name=name, grid=(T // tm,),
        in_specs=[tok, tok, pl.BlockSpec((1, D), lambda i: (0, 0))],
        out_specs=[pl.BlockSpec((8, LANE), lambda i: (0, 0)), tok, pl.BlockSpec((1, D), lambda i: (0, 0))],
        out_shape=[_sds((8, LANE), F32), _sds((T, D), F32), _sds((1, D), F32)],
        compiler_params=_cp((ARB,)),
    )(x, tgt, fw)


def _block_rows(rows, cols):
    cap = max(8, (1 << 18) // cols)
    best = None
    for r in range(8, rows + 1, 8):
        if rows % r == 0 and r <= cap:
            best = r
    return best if best is not None else rows


def _sum8(l_ref):
    def four(base):
        return ((l_ref[base + 3].astype(F32) + l_ref[base].astype(F32)) + l_ref[base + 1].astype(F32)
                ) + l_ref[base + 2].astype(F32)

    return four(0) + four(4)


def _sum_adamw(lands, w, m, v, name):
    _, rows, cols = lands[0].shape
    br = _block_rows(rows, cols)
    nb = rows // br
    n_layers = len(lands)

    def body(*refs):
        l_refs = refs[:n_layers]
        w_ref, m_ref, v_ref, g_ref, d_ref, nm_ref, nv_ref = refs[n_layers:]
        i = pl.program_id(0)
        for l, l_ref in enumerate(l_refs):
            @pl.when((i >= l * nb) & (i < (l + 1) * nb))
            def _():
                g = _sum8(l_ref)
                d, nm, nv = _adamw_math(w_ref[...], g, m_ref[...], v_ref[...])
                g_ref[...] = g
                d_ref[...] = d
                nm_ref[...] = nm
                nv_ref[...] = nv

    def land_spec(l):
        return pl.BlockSpec((2 * N_SHARD, br, cols), lambda i: (0, jnp.clip(i - l * nb, 0, nb - 1), 0))

    blk = pl.BlockSpec((br, cols), lambda i: (i, 0))
    return pl.pallas_call(
        body, name=name, grid=(n_layers * nb,),
        in_specs=[land_spec(l) for l in range(n_layers)] + [blk] * 3, out_specs=[blk] * 4,
        out_shape=[_sds((n_layers * rows, cols), F32)] * 4,
        compiler_params=_cp((PAR,)),
    )(*lands, w, m, v)


def _adamw_math(w, g, m, v):
    m = ADAM_B1 * m + (1.0 - ADAM_B1) * g
    v = ADAM_B2 * v + (1.0 - ADAM_B2) * (g * g)
    m_hat = m / (1.0 - ADAM_B1 ** ADAM_STEP)
    v_hat = v / (1.0 - ADAM_B2 ** ADAM_STEP)
    delta = -ADAM_LR * (m_hat / (jnp.sqrt(v_hat) + ADAM_EPS) + ADAM_WD * w)
    return delta, m, v


def _adamw(w, g, m, v, name):
    rows, cols = w.shape
    br = _block_rows(rows, cols)

    def body(w_ref, g_ref, m_ref, v_ref, d_ref, nm_ref, nv_ref):
        d, nm, nv = _adamw_math(w_ref[...], g_ref[...], m_ref[...], v_ref[...])
        d_ref[...] = d
        nm_ref[...] = nm
        nv_ref[...] = nv

    blk = pl.BlockSpec((br, cols), lambda i: (i, 0))
    return pl.pallas_call(
        body, name=name, grid=(rows // br,),
        in_specs=[blk] * 4, out_specs=[blk] * 3,
        out_shape=[_sds((rows, cols), F32)] * 3,
        compiler_params=_cp((PAR,)),
    )(w, g, m, v)


def _allreduce_small(v, name):
    rows = v.shape[0]
    flips = [(fx, fy, fc) for fx in (0, 1) for fy in (0, 1) for fc in (0, 1) if fx or fy or fc]

    def body(v_ref, o_ref, all_s, ssem, rsem):
        x, y, c = _place()

        def peer(f):
            return (x + f[0] - 2 * x * f[0], y + f[1] - 2 * y * f[1], c + f[2] - 2 * c * f[2])

        def slot(p):
            return all_s.at[4 * p[0] + 2 * p[1] + p[2]]

        def copy(k, f, owner):
            return pltpu.make_async_remote_copy(
                src_ref=v_ref, dst_ref=slot(owner), send_sem=ssem.at[k], recv_sem=rsem.at[k],
                device_id=peer(f), device_id_type=MESH)

        sends = [copy(k, f, (x, y, c)) for k, f in enumerate(flips)]
        for cp in sends:
            cp.start()
        all_s[4 * x + 2 * y + c] = v_ref[...]
        for k, f in enumerate(flips):
            copy(k, f, peer(f)).wait_recv()
        for cp in sends:
            cp.wait_send()
        acc = all_s[0]
        for d in range(1, 8):
            acc = acc + all_s[d]
        o_ref[...] = acc

    return pl.pallas_call(
        body, name=name,
        in_specs=[pl.BlockSpec(memory_space=pltpu.VMEM)],
        out_specs=pl.BlockSpec(memory_space=pltpu.VMEM),
        out_shape=_sds((rows, LANE), F32),
        scratch_shapes=[pltpu.VMEM((8, rows, LANE), F32), pltpu.SemaphoreType.DMA((7,)), pltpu.SemaphoreType.DMA((7,))],
    )(v)


BIG_NAMES = ("ffn1_w_gate", "ffn1_w_up", "ffn1_w_down", "w_in", "w_branch", "w_merge_gate", "w_out",
             "ffn2_w_gate", "ffn2_w_up", "ffn2_w_down")


FFN1 = ("ffn1_w_gate", "ffn1_w_up", "ffn1_w_down")
FFN2 = ("ffn2_w_gate", "ffn2_w_up", "ffn2_w_down")
MIX_IN = ("w_in", "w_merge_gate")
MIX_OUT = ("w_branch", "w_out")


def _keys(names, l):
    return [(n, l) for n in names]


def _local_step(x, tgt, small, convw_full, biases, wx, n_layers):
    T, D = x.shape
    L = n_layers
    ns = N_SHARD
    dq = D // ns
    W = wx.w

    def hosted(call, keys, scatter=False):
        comm = wx.pieces(keys, scatter)
        main, extra = call(comm)
        if comm is not None:
            wx.arrived(keys, extra, scatter)
        return main

    def mixer_views(l):
        return _build_wbig(W[("w_merge_gate", l)], W[("w_in", l)], f"wbig_{l}")

    def out_views(l):
        wb4 = W[("w_branch", l)]
        wb = _copy_blocks(wb4, pl.BlockSpec((None, None, BRANCH_W, dq), lambda s_, i: (s_, i, 0, 0)),
                          _sds((3, BRANCH_W, D), wb4.dtype),
                          pl.BlockSpec((None, BRANCH_W, dq), lambda s_, i: (i, 0, s_)), (ns, 3), f"w_branch_whole_{l}")
        wo = W[("w_out", l)].reshape(D, D)
        return wb, wo

    tb = _ret_tables(T)

    saved = []
    h = x
    for l in range(L):
        s = {"x0": h}
        nxt = l + 1
        x1, s["g1"], s["u1"] = hosted(
            lambda c: _ffn_fwd(h, small["ffn1_norm"][l][None], W[("ffn1_w_gate", l)], W[("ffn1_w_up", l)],
                               W[("ffn1_w_down", l)], f"ffn1_fwd_{l}", comm=c), _keys(MIX_IN + MIX_OUT, l))
        s["x1"] = x1
        s["wbig"] = mixer_views(l)
        big, s["h"] = _inproj_fwd(x1, small["mix_norm"][l][None], s["wbig"], f"inproj_fwd_{l}")
        s["big"] = big
        s["bias"] = biases[l]
        s["yc"] = _conv_fwd(big, convw_full[l], f"conv_fwd_{l}")
        s["yr"], s["o"], s["st"] = hosted(lambda c: _ret_fwd(big, tb, f"ret_fwd_{l}", comm=c), [])
        (s["ya"],) = hosted(lambda c: _att_fwd(big, s["bias"], f"att_fwd_{l}", comm=c), _keys(FFN2, l))
        s["wb"], s["wo"] = out_views(l)
        x2, s["p"], s["mg"] = _merge_fwd(x1, big, (s["yc"], s["yr"], s["ya"]), s["wb"], s["wo"], f"merge_fwd_{l}")
        s["x2"] = x2
        h, s["g2"], s["u2"] = hosted(
            lambda c: _ffn_fwd(x2, small["ffn2_norm"][l][None], W[("ffn2_w_gate", l)], W[("ffn2_w_up", l)],
                               W[("ffn2_w_down", l)], f"ffn2_fwd_{l}", comm=c), _keys(FFN1, nxt) if nxt < L else [])
        saved.append(s)

    loss_p, dx, d_final = _loss_head(h, tgt, small["final_norm"][None], "loss_head")

    gs = {"final_norm": d_final[0]}
    for k in ("ffn1_norm", "mix_norm", "ffn2_norm", "rel_bias", "conv_w"):
        gs[k] = [None] * L
    tk = min(2048, T)
    nk = T // tk

    def ffn_back(pre, l, dxo, x_in, g, u, first_keys, second_keys, between=None):
        nw = small[pre + "_norm"][l][None]
        dgv, duv, av, hb, dacc = hosted(
            lambda c: _ffn_bwd_hidden(dxo, x_in, nw, g, u, W[(pre + "_w_down", l)], f"{pre}_bwd_hidden_{l}", comm=c),
            first_keys, scatter=True)
        parts = (hb, dgv, duv, av, dacc)
        if between is not None:
            second_keys = between(parts)
        dxn, dn = hosted(
            lambda c: _ffn_bwd_resid(dgv, duv, W[(pre + "_w_gate", l)], W[(pre + "_w_up", l)], x_in, nw, dxo,
                                     f"{pre}_bwd_resid_{l}", comm=c),
            second_keys, scatter=True)
        gs[pre + "_norm"][l] = dn[0]
        return dxn, parts

    def ffn_grads(pre, l, hb, dgv, duv, av, dacc, chain=False):
        fs = dgv.shape[-1]
        tkf = min(2 * tk, T)
        hspec = pl.BlockSpec((tkf, D), lambda p, q, k: (k, 0))
        sspec = pl.BlockSpec((None, tkf, fs), lambda p, q, k: (p, k, 0))
        down_spec = pl.BlockSpec((None, fs, D), lambda p, q, k: (p, 0, 0))
        jobs = [(pre + "_w_gate", dgv, hb, sspec, hspec, (ns, fs, D), down_spec),
                (pre + "_w_up", duv, hb, sspec, hspec, (ns, fs, D), down_spec),
                (pre + "_w_down", av, dacc, sspec, hspec, (ns, fs, D), down_spec)]
        for idx, (nm, a, b, a_spec, b_spec, shape, o_spec) in enumerate(jobs):
            def product(c):
                r = _tn(a, b, a_spec, b_spec, _sds(shape, BF), o_spec, (ns, 1, T // tkf), f"d{nm}_{l}", comm=c)
                return (r, []) if c is None else r
            keys = [(jobs[0][0], l)] if chain and idx == 2 else []
            wx.g[(nm, l)] = hosted(product, keys, scatter=True)
        return [(jobs[1][0], l), (jobs[2][0], l)] if chain else []

    for l in reversed(range(L)):
        s = saved[l]
        above = _keys(FFN1, l + 1) if l + 1 < L else []
        dx, parts = ffn_back("ffn2", l, dx, s["x2"], s["g2"], s["u2"], above[:1], above[1:])
        ffn_grads("ffn2", l, *parts)
        dp, dbig, dyc, dyr, dya, dxb = _merge_bwd(dx, s["big"], s["p"], s["wb"], s["wo"], f"merge_bwd_{l}")
        wx.g[("w_out", l)] = _tn(
            s["mg"], dxb, pl.BlockSpec((tk, dq), lambda p, q, k: (k, p)), pl.BlockSpec((tk, D), lambda p, q, k: (k, 0)),
            _sds((ns, dq, D), BF), pl.BlockSpec((None, dq, D), lambda p, q, k: (p, 0, 0)), (ns, 1, nk), f"dw_out_{l}")
        wx.g[("w_branch", l)] = _tn_branches((s["yc"], s["yr"], s["ya"]), dp, ns, tk, f"dw_branch_{l}")
        dbig, dcw = _conv_bwd(s["big"], dyc, convw_full[l], dbig, f"conv_bwd_{l}")
        gs["conv_w"][l] = dcw
        dbig = _ret_bwd(s["big"], s["o"], s["st"], dyr, tb, dbig, f"ret_bwd_{l}")
        dbig, dbias = hosted(lambda c: _att_bwd(s["big"], s["bias"], dya, dbig, f"att_bwd_{l}", comm=c),
                             _keys(FFN2, l), scatter=True)
        gs["rel_bias"][l] = _relbias_grad(jnp.transpose(dbias, (1, 0, 2)), f"relbias_grad_{l}")[:, :N_REL]
        n_in = N_SEG * BRANCH_W
        bn = 1024 if (3 * D) % 1024 == 0 else BRANCH_W
        dwp = _tn(s["h"], dbig, pl.BlockSpec((tk, D), lambda p, q, k: (k, 0)),
                  pl.BlockSpec((tk, bn), lambda p, q, k: (k, 3 * D // bn + q)),
                  _sds((D, n_in), BF), pl.BlockSpec((D, bn), lambda p, q, k: (0, q)), (1, n_in // bn, nk), f"dw_in_{l}")
        wx.g[("w_in", l)] = _ungroup_dw_in(dwp, ns, f"dw_in_shards_{l}")
        wx.g[("w_merge_gate", l)] = _tn_gates(s["h"], dbig, ns, tk, f"dw_merge_gate_{l}")
        dx, dn = hosted(
            lambda c: _inproj_bwd(dbig, s["wbig"], s["x1"], small["mix_norm"][l][None], dx, f"inproj_bwd_{l}", comm=c),
            [("w_in", l)], scatter=True)
        gs["mix_norm"][l] = dn[0]
        rest = [("w_merge_gate", l), ("w_branch", l), ("w_out", l)]
        if l == 0:
            dx, _ = ffn_back("ffn1", l, dx, s["x0"], s["g1"], s["u1"], rest, [],
                             between=lambda parts: ffn_grads("ffn1", 0, *parts, chain=True))
        else:
            dx, parts = ffn_back("ffn1", l, dx, s["x0"], s["g1"], s["u1"], rest, [])
            ffn_grads("ffn1", l, *parts)

    for k in ("ffn1_norm", "mix_norm", "ffn2_norm", "rel_bias", "conv_w"):
        gs[k] = jnp.stack(gs[k])
    return loss_p, dx, gs


class _Exchange:
    def __init__(self, shards):
        self.shards = shards
        self.w = {}
        self.g = {}
        self.landed = {}

    def own(self, key):
        return self.shards[key[0]][key[1]].astype(BF)

    def pieces(self, keys, scatter):
        if not keys:
            return None
        if scatter:
            return _Scatter([self.g[k] for k in keys])
        return _HalfGather([_halves(self.own(k)) for k in keys])

    def arrived(self, keys, outs, scatter):
        for k, o in zip(keys, outs):
            if scatter:
                self.landed[k] = o
            else:
                self.w[k] = o.reshape((N_SHARD,) + self.shards[k[0]].shape[1:])


def _halves(a):
    return a.reshape(2, -1, a.shape[-1])


TRANSPOSED_GRADS = ("ffn1_w_gate", "ffn1_w_up", "ffn2_w_gate", "ffn2_w_up")
W_NAMES = ("ffn1_norm", "ffn1_w_gate", "ffn1_w_up", "ffn1_w_down", "mix_norm", "w_in", "conv_w", "rel_bias", "w_branch",
           "w_merge_gate", "w_out", "ffn2_norm", "ffn2_w_gate", "ffn2_w_up", "ffn2_w_down", "final_norm")


def _as2d(a):
    return a.reshape(1, -1) if a.ndim == 1 else a.reshape(-1, a.shape[-1])


def kernel(x, ffn1_norm, ffn1_w_gate, ffn1_w_up, ffn1_w_down, mix_norm, w_in, conv_w, rel_bias, w_branch, w_merge_gate, w_out, ffn2_norm, ffn2_w_gate, ffn2_w_up, ffn2_w_down, final_norm, loss_target, m_ffn1_norm, m_ffn1_w_gate, m_ffn1_w_up, m_ffn1_w_down, m_mix_norm, m_w_in, m_conv_w, m_rel_bias, m_w_branch, m_w_merge_gate, m_w_out, m_ffn2_norm, m_ffn2_w_gate, m_ffn2_w_up, m_ffn2_w_down, m_final_norm, v_ffn1_norm, v_ffn1_w_gate, v_ffn1_w_up, v_ffn1_w_down, v_mix_norm, v_w_in, v_conv_w, v_rel_bias, v_w_branch, v_w_merge_gate, v_w_out, v_ffn2_norm, v_ffn2_w_gate, v_ffn2_w_up, v_ffn2_w_down, v_final_norm):
    given = dict(locals())
    w = {n: given[n] for n in W_NAMES}
    m = {n: given["m_" + n] for n in W_NAMES}
    v = {n: given["v_" + n] for n in W_NAMES}
    my_chip = 2 * lax.axis_index("x") + lax.axis_index("y")
    L = w_in.shape[0]

    wx = _Exchange({n: jnp.swapaxes(w[n], 1, 2) if n in TRANSPOSED_GRADS else w[n] for n in BIG_NAMES})
    first = _keys(FFN1, 0)
    biases, got = _relbias_expand(
        rel_bias, "relbias_expand", comm=_HalfGather([_halves(wx.own(k)) for k in first] + [_halves(conv_w)]))
    wx.arrived(first, got[:-1], False)
    convw_full = jnp.transpose(got[-1].reshape((N_SHARD,) + conv_w.shape), (1, 2, 0, 3)).reshape(
        conv_w.shape[0], conv_w.shape[1], -1)

    small = {n: w[n] for n in ("ffn1_norm", "mix_norm", "ffn2_norm", "final_norm")}
    loss_p, grad_x, gs = _local_step(x[0], loss_target[0], small, convw_full, biases, wx, L)

    parts = [gs["ffn1_norm"].reshape(-1), gs["mix_norm"].reshape(-1), gs["ffn2_norm"].reshape(-1),
             gs["final_norm"].reshape(-1), gs["rel_bias"].reshape(-1), gs["conv_w"].reshape(-1), loss_p[0]]
    sizes = [p.shape[0] for p in parts]
    flat = jnp.concatenate(parts)
    rows = -(-flat.shape[0] // (8 * LANE)) * 8
    flat = jnp.pad(flat, (0, rows * LANE - flat.shape[0])).reshape(rows, LANE)
    red = _allreduce_small(flat, "allreduce_small").reshape(-1)
    offs = [0]
    for sz in sizes:
        offs.append(offs[-1] + sz)
    sm = {}
    for i, n in enumerate(("ffn1_norm", "mix_norm", "ffn2_norm", "final_norm", "rel_bias", "conv_w")):
        sm[n] = red[offs[i]:offs[i + 1]]
    loss = red[offs[6]]
    sm["conv_w"] = lax.dynamic_slice_in_dim(sm["conv_w"].reshape(conv_w.shape[0], conv_w.shape[1], -1),
                                            my_chip * conv_w.shape[2], conv_w.shape[2], axis=2)

    grads, deltas, new_m, new_v = {}, {}, {}, {}
    for n in W_NAMES:
        flip = n in TRANSPOSED_GRADS

        def view(a):
            return jnp.swapaxes(a, 1, 2) if flip else a

        shape = view(w[n]).shape
        wmv = [_as2d(view(a[n])) for a in (w, m, v)]
        if n in BIG_NAMES:
            lands = [wx.landed[(n, l)] for l in range(L)]
            out = _sum_adamw([a.reshape(a.shape[0], -1, a.shape[-1]) for a in lands], *wmv, f"adamw_{n}")
        else:
            g = _as2d(sm[n].reshape(shape))
            out = [g] + list(_adamw(wmv[0], g, wmv[1], wmv[2], f"adamw_{n}"))
        grads[n], deltas[n], new_m[n], new_v[n] = (view(o.reshape(shape)) for o in out)

    return (loss, grad_x[None], *[grads[n] for n in W_NAMES], *[deltas[n] for n in W_NAMES],
            *[new_m[n] for n in W_NAMES], *[new_v[n] for n in W_NAMES])
```

```python
import functools
import math

import jax
import jax.numpy as jnp
from jax import lax
from jax.experimental import pallas as pl
from jax.experimental.pallas import tpu as pltpu

F32 = jnp.float32
BF = jnp.bfloat16
MESH = pl.DeviceIdType.MESH
ARB = "arbitrary"
PAR = "parallel"

EPS = 1e-6
NEG_INF = -1e30
ROPE_BASE = 10000.0
CHUNK = 64
BRANCH_W = 512
H_RET = 4
DK_RET = 128
H_ATT = 8
DH_ATT = 64
N_PREV = 8
REL_CLIP = 128
N_REL = 2 * REL_CLIP + 1
N_SHARD = 4
LANE = 128
RET_L = 512
ATT_TQ = 128
ATT_SUB = 4
ATT_PAD = N_PREV * CHUNK
ATT_SPAN = ATT_TQ + ATT_PAD
ATT_TOEP = 2 * REL_CLIP
RB_PAD = 264
TM = 512
TM_FFN = 1024

ADAM_LR = 0.001
ADAM_B1 = 0.9
ADAM_B2 = 0.999
ADAM_EPS = 1e-08
ADAM_WD = 0.01
ADAM_STEP = 10

NT_DIMS = (((1,), (1,)), ((), ()))
TN_DIMS = (((0,), (0,)), ((), ()))


def _cp(sem, vmem_mb=48):
    return pltpu.CompilerParams(dimension_semantics=sem, vmem_limit_bytes=vmem_mb << 20)


def _sds(shape, dtype):
    return jax.ShapeDtypeStruct(tuple(shape), dtype)


def _rms_r(x):
    return lax.rsqrt(jnp.mean(x * x, axis=-1, keepdims=True) + EPS)


def _sigmoid(x):
    return 0.5 * jnp.tanh(0.5 * x) + 0.5


def _rms_bwd(dh, xv, nw):
    r = _rms_r(xv)
    xh = xv * r
    dxh = dh * nw
    dx = r * (dxh - xh * jnp.mean(dxh * xh, axis=-1, keepdims=True))
    return dx, jnp.sum(dh * xh, axis=0, keepdims=True)


def _place():
    return lax.axis_index("x"), lax.axis_index("y"), lax.axis_index("c")


def _other_chips(x, y):
    return [(1 - x, y), (x, 1 - y), (1 - x, 1 - y)]


class _Scatter:
    def __init__(self, srcs):
        self.srcs = list(srcs)
        n = len(self.srcs)
        self.out_shape = [_sds((2 * N_SHARD,) + s.shape[1:], s.dtype) for s in self.srcs]
        self.scratch = [pltpu.SemaphoreType.DMA((n,)), pltpu.SemaphoreType.DMA((3, n)), pltpu.SemaphoreType.DMA((3, n)),
                        pltpu.SemaphoreType.DMA((4, n)), pltpu.SemaphoreType.DMA((4, n))]

    def _plan(self, src, dst, sems, want):
        lsem, s1, r1, s2, r2 = sems
        x, y, c = _place()
        mine = 2 * x + y
        n = len(src)
        chips = list(enumerate(_other_chips(x, y)))

        def copy(s_ref, d_ref, ssem, rsem, to):
            return pltpu.make_async_remote_copy(src_ref=s_ref, dst_ref=d_ref, send_sem=ssem, recv_sem=rsem,
                                                device_id=to, device_id_type=MESH)

        local = [pltpu.make_async_copy(src[k].at[mine], dst[k].at[3], lsem.at[k]) for k in range(n)
                 ] if "local" in want else []
        sends = [copy(src[k].at[2 * ch[0] + ch[1]], dst[k].at[j], s1.at[j, k], r1.at[j, k], (ch[0], ch[1], c))
                 for j, ch in chips for k in range(n)] if "sends" in want else []
        passes = [copy(dst[k].at[j], dst[k].at[4 + j], s2.at[j, k], r2.at[j, k], (x, y, 1 - c))
                  for j, ch in chips for k in range(n)] if "passes" in want else []
        own_pass = [copy(src[k].at[mine], dst[k].at[7], s2.at[3, k], r2.at[3, k], (x, y, 1 - c))
                    for k in range(n)] if "own_pass" in want else []
        return local, sends, passes, own_pass

    def start(self, src, dst, sems):
        local, sends, _, own_pass = self._plan(src, dst, sems, ("local", "sends", "own_pass"))
        for cp in local + sends + own_pass:
            cp.start()

    def relay(self, src, dst, sems):
        _, sends, passes, _ = self._plan(src, dst, sems, ("sends", "passes"))
        for land, fwd in zip(sends, passes):
            land.wait_recv()
            fwd.start()

    def finish(self, src, dst, sems):
        local, sends, passes, own_pass = self._plan(src, dst, sems, ("local", "sends", "passes", "own_pass"))
        for cp in passes + own_pass:
            cp.wait_recv()
        for cp in sends + passes + own_pass:
            cp.wait_send()
        for cp in local:
            cp.wait()

    def wait(self, src, dst, sems):
        self.relay(src, dst, sems)
        self.finish(src, dst, sems)


class _HalfGather:
    def __init__(self, srcs):
        self.srcs = list(srcs)
        n = len(self.srcs)
        self.out_shape = [_sds((N_SHARD,) + s.shape, s.dtype) for s in self.srcs]
        self.scratch = [pltpu.SemaphoreType.DMA((n,))] + [pltpu.SemaphoreType.DMA((3, n)) for _ in range(4)]

    def _plan(self, src, dst, sems, want):
        lsem, s1, r1, s2, r2 = sems
        x, y, c = _place()
        mine = 2 * x + y
        n = len(src)
        chips = [(j, ch, 2 * ch[0] + ch[1]) for j, ch in enumerate(_other_chips(x, y))]

        def copy(s_ref, d_ref, ssem, rsem, to):
            return pltpu.make_async_remote_copy(src_ref=s_ref, dst_ref=d_ref, send_sem=ssem, recv_sem=rsem,
                                                device_id=to, device_id_type=MESH)

        def over(kind, make):
            return [make(j, ch, slot, k) for j, ch, slot in chips for k in range(n)] if kind in want else []

        local = [pltpu.make_async_copy(src[k], dst[k].at[mine], lsem.at[k]) for k in range(n)] if "local" in want else []
        sends = over("sends", lambda j, ch, slot, k: copy(src[k].at[c], dst[k].at[mine, c], s1.at[j, k], r1.at[j, k],
                                                          (ch[0], ch[1], c)))
        lands = over("lands", lambda j, ch, slot, k: copy(src[k].at[c], dst[k].at[slot, c], s1.at[j, k], r1.at[j, k],
                                                          (ch[0], ch[1], c)))
        passes = over("passes", lambda j, ch, slot, k: copy(dst[k].at[slot, c], dst[k].at[slot, c], s2.at[j, k],
                                                            r2.at[j, k], (x, y, 1 - c)))
        gets = over("gets", lambda j, ch, slot, k: copy(dst[k].at[slot, 1 - c], dst[k].at[slot, 1 - c], s2.at[j, k],
                                                        r2.at[j, k], (x, y, 1 - c)))
        return local, sends, lands, passes, gets

    def start(self, src, dst, sems):
        lsem, s1, r1, s2, r2 = sems
        x, y, c = _place()
        mine = 2 * x + y
        for k in range(len(src)):
            pltpu.make_async_copy(src[k], dst[k].at[mine], lsem.at[k]).start()
        for j, ch in enumerate(_other_chips(x, y)):
            for k in range(len(src)):
                pltpu.make_async_remote_copy(
                    src_ref=src[k].at[c], dst_ref=dst[k].at[mine, c], send_sem=s1.at[j, k], recv_sem=r1.at[j, k],
                    device_id=(ch[0], ch[1], c), device_id_type=MESH).start()

    def relay(self, src, dst, sems):
        _, _, lands, passes, _ = self._plan(src, dst, sems, ("lands", "passes"))
        for land, fwd in zip(lands, passes):
            land.wait_recv()
            fwd.start()

    def finish(self, src, dst, sems):
        local, sends, _, passes, gets = self._plan(src, dst, sems, ("local", "sends", "passes", "gets"))
        for cp in gets:
            cp.wait_recv()
        for cp in sends + passes:
            cp.wait_send()
        for cp in local:
            cp.wait()

    def wait(self, src, dst, sems):
        self.relay(src, dst, sems)
        self.finish(src, dst, sems)


def _call(body, *, name, args, in_specs, out_specs, out_shape, grid=(), scratch_shapes=(), sem=None, comm=None,
          aliases=None, vmem_mb=48):
    in_specs, out_specs, out_shape = list(in_specs), list(out_specs), list(out_shape)
    scratch, args = list(scratch_shapes), list(args)
    n_in, n_out, n_scr = len(in_specs), len(out_specs), len(scratch)
    if comm is None:
        def kernel_body(*refs):
            body(*refs)
    else:
        c_in, c_out = len(comm.srcs), len(comm.out_shape)

        def kernel_body(*refs):
            o0 = n_in + c_in
            s0 = o0 + n_out + c_out
            cin, cout, sems = refs[n_in:o0], refs[o0 + n_out:s0], refs[s0 + n_scr:]
            main = refs[:n_in] + refs[o0:o0 + n_out] + refs[s0:s0 + n_scr]
            if grid:
                ids = [pl.program_id(a) for a in range(len(grid))]
                first = functools.reduce(lambda p, q: p & q, [i == 0 for i in ids])
                last = functools.reduce(lambda p, q: p & q, [i == g - 1 for i, g in zip(ids, grid)])

                @pl.when(first)
                def _():
                    comm.start(cin, cout, sems)

                body(*main)

                steps = math.prod(grid)
                if hasattr(comm, "relay") and steps >= 4:
                    flat = functools.reduce(lambda p, q: p + q, [i * math.prod(grid[a + 1:]) for a, i in enumerate(ids)])

                    @pl.when(flat == (5 * steps) // 6)
                    def _():
                        comm.relay(cin, cout, sems)

                    @pl.when(last)
                    def _():
                        comm.finish(cin, cout, sems)
                else:
                    @pl.when(last)
                    def _():
                        comm.wait(cin, cout, sems)
            else:
                comm.start(cin, cout, sems)
                body(*main)
                comm.wait(cin, cout, sems)

        hbm = pl.BlockSpec(memory_space=pl.ANY)
        in_specs += [hbm] * c_in
        out_specs += [hbm] * c_out
        out_shape += comm.out_shape
        scratch += comm.scratch
        args += comm.srcs
    params = dict(vmem_limit_bytes=vmem_mb << 20)
    if grid:
        params["dimension_semantics"] = sem
    outs = pl.pallas_call(
        kernel_body, name=name, grid=grid, in_specs=in_specs, out_specs=out_specs, out_shape=out_shape,
        scratch_shapes=scratch, input_output_aliases=aliases or {}, compiler_params=pltpu.CompilerParams(**params),
    )(*args)
    return list(outs[:n_out]), list(outs[n_out:])


def _ffn_fwd(x, nw, wg, wu, wd, name, comm=None):
    T, D = x.shape
    ns, fs, _ = wg.shape
    tm = min(TM_FFN, T)

    def body(x_ref, nw_ref, wg_ref, wu_ref, wd_ref, xo_ref, g_ref, u_ref, h_s, acc_s):
        j = pl.program_id(1)

        @pl.when(j == 0)
        def _():
            xv = x_ref[...]
            h_s[...] = (xv * _rms_r(xv) * nw_ref[...]).astype(BF)
            acc_s[...] = jnp.zeros_like(acc_s)

        h = h_s[...]
        gb = lax.dot_general(h, wg_ref[...], NT_DIMS, preferred_element_type=F32).astype(BF)
        ub = lax.dot_general(h, wu_ref[...], NT_DIMS, preferred_element_type=F32).astype(BF)
        g_ref[...] = gb
        u_ref[...] = ub
        g = gb.astype(F32)
        a = (g * _sigmoid(g) * ub.astype(F32)).astype(BF)
        acc_s[...] += jnp.dot(a, wd_ref[...], preferred_element_type=F32)

        @pl.when(j == ns - 1)
        def _():
            xo_ref[...] = x_ref[...] + 0.5 * acc_s[...]

    wspec = pl.BlockSpec((None, fs, D), lambda i, j: (j, 0, 0))
    return _call(
        body, name=name, grid=(T // tm, ns), args=(x, nw, wg, wu, wd), comm=comm, vmem_mb=56,
        in_specs=[pl.BlockSpec((tm, D), lambda i, j: (i, 0)),
                  pl.BlockSpec((1, D), lambda i, j: (0, 0)),
                  wspec, wspec,
                  pl.BlockSpec((None, fs, D), lambda i, j: (j, 0, 0))],
        out_specs=[pl.BlockSpec((tm, D), lambda i, j: (i, 0)),
                   pl.BlockSpec((None, tm, fs), lambda i, j: (j, i, 0)),
                   pl.BlockSpec((None, tm, fs), lambda i, j: (j, i, 0))],
        out_shape=[_sds((T, D), F32), _sds((ns, T, fs), BF), _sds((ns, T, fs), BF)],
        scratch_shapes=[pltpu.VMEM((tm, D), BF), pltpu.VMEM((tm, D), F32)],
        sem=(ARB, ARB))


def _ffn_bwd_hidden(dxo, x, nw, g, u, wd, name, comm=None):
    T, D = x.shape
    ns, fs, _ = wd.shape
    tm = min(TM_FFN, T)

    def body(dxo_ref, x_ref, nw_ref, g_ref, u_ref, wd_ref, dg_ref, du_ref, a_ref, h_ref, dacc_ref, dacc_s):
        @pl.when(pl.program_id(1) == 0)
        def _():
            xv = x_ref[...]
            h_ref[...] = (xv * _rms_r(xv) * nw_ref[...]).astype(BF)
            db = (0.5 * dxo_ref[...]).astype(BF)
            dacc_ref[...] = db
            dacc_s[...] = db

        da = lax.dot_general(dacc_s[...], wd_ref[...], NT_DIMS, preferred_element_type=F32)
        gv = g_ref[...].astype(F32)
        uv = u_ref[...].astype(F32)
        s = _sigmoid(gv)
        sg = gv * s
        a_ref[...] = (sg * uv).astype(BF)
        du_ref[...] = (da * sg).astype(BF)
        dg_ref[...] = (da * uv * (s * (1.0 + gv * (1.0 - s)))).astype(BF)

    tok = pl.BlockSpec((tm, D), lambda i, j: (i, 0))
    hid = pl.BlockSpec((None, tm, fs), lambda i, j: (j, i, 0))
    return _call(
        body, name=name, grid=(T // tm, ns), args=(dxo, x, nw, g, u, wd), comm=comm, vmem_mb=56,
        in_specs=[tok, tok, pl.BlockSpec((1, D), lambda i, j: (0, 0)), hid, hid,
                  pl.BlockSpec((None, fs, D), lambda i, j: (j, 0, 0))],
        out_specs=[hid, hid, hid, tok, tok],
        out_shape=[_sds((ns, T, fs), BF)] * 3 + [_sds((T, D), BF)] * 2,
        scratch_shapes=[pltpu.VMEM((tm, D), BF)],
        sem=(ARB, ARB))


def _ffn_bwd_resid(dg, du, wg, wu, x, nw, dxo, name, comm=None):
    T, D = x.shape
    ns, fs, _ = wg.shape
    tm = min(TM_FFN, T)

    def body(dg_ref, du_ref, wg_ref, wu_ref, x_ref, nw_ref, dxo_ref, dx_ref, dnw_ref, acc_s):
        i = pl.program_id(0)
        j = pl.program_id(1)
        prod = (jnp.dot(dg_ref[...], wg_ref[...], preferred_element_type=F32)
                + jnp.dot(du_ref[...], wu_ref[...], preferred_element_type=F32))

        @pl.when((i == 0) & (j == 0))
        def _():
            dnw_ref[...] = jnp.zeros_like(dnw_ref)

        @pl.when(j == 0)
        def _():
            acc_s[...] = prod

        @pl.when(j > 0)
        def _():
            acc_s[...] += prod

        @pl.when(j == ns - 1)
        def _():
            dx, dn = _rms_bwd(acc_s[...], x_ref[...], nw_ref[...])
            dx_ref[...] = dxo_ref[...] + dx
            dnw_ref[...] += dn

    tok = pl.BlockSpec((tm, D), lambda i, j: (i, 0))
    row = pl.BlockSpec((1, D), lambda i, j: (0, 0))
    hid = pl.BlockSpec((None, tm, fs), lambda i, j: (j, i, 0))
    wspec = pl.BlockSpec((None, fs, D), lambda i, j: (j, 0, 0))
    return _call(
        body, name=name, grid=(T // tm, ns), args=(dg, du, wg, wu, x, nw, dxo), comm=comm, vmem_mb=56,
        in_specs=[hid, hid, wspec, wspec, tok, row, tok],
        out_specs=[tok, row],
        out_shape=[_sds((T, D), F32), _sds((1, D), F32)],
        scratch_shapes=[pltpu.VMEM((tm, D), F32)],
        sem=(ARB, ARB))


def _tn(a, b, a_spec, b_spec, out_shape, out_spec, grid, name, prev=None, comm=None):
    nk = grid[-1]
    acc_shape = tuple(d for d in out_spec.block_shape if d is not None)

    def body(*refs):
        a_ref, b_ref = refs[0], refs[1]
        o_ref, acc = refs[-2], refs[-1]
        k = pl.program_id(2)
        prod = lax.dot_general(a_ref[...], b_ref[...], TN_DIMS, preferred_element_type=F32)

        @pl.when(k == 0)
        def _():
            acc[...] = prod

        @pl.when(k > 0)
        def _():
            acc[...] += prod

        @pl.when(k == nk - 1)
        def _():
            o_ref[...] = acc[...].astype(o_ref.dtype)

    in_specs = [a_spec, b_spec]
    args = [a, b]
    aliases = {}
    if prev is not None:
        in_specs.append(pl.BlockSpec(memory_space=pl.ANY))
        args.append(prev)
        aliases = {2: 0}
    main, extra = _call(
        body, name=name, grid=grid, args=args, in_specs=in_specs, out_specs=[out_spec], out_shape=[out_shape],
        scratch_shapes=[pltpu.VMEM(acc_shape, F32)], aliases=aliases, sem=(ARB, ARB, ARB), comm=comm)
    return main[0] if comm is None else (main[0], extra)


def _tn_branches(ys, dp, ns, tk, name):
    T, w = ys[0].shape
    D = dp.shape[1] // 3
    dq = D // ns
    nk = T // tk

    def body(yc_ref, yr_ref, ya_ref, b_ref, o_ref, acc):
        i = pl.program_id(0)
        k = pl.program_id(1)

        @pl.when(k == 0)
        def _():
            acc[...] = jnp.zeros_like(acc)

        for branch, y_ref in enumerate((yc_ref, yr_ref, ya_ref)):
            @pl.when(i == branch)
            def _():
                acc[...] += lax.dot_general(y_ref[...], b_ref[...], TN_DIMS, preferred_element_type=F32)

        @pl.when(k == nk - 1)
        def _():
            for s in range(ns):
                o_ref[s] = acc[:, s * dq:(s + 1) * dq].astype(o_ref.dtype)

    yspec = pl.BlockSpec((tk, w), lambda i, k: (k, 0))
    return pl.pallas_call(
        body, name=name, grid=(3, nk),
        in_specs=[yspec, yspec, yspec, pl.BlockSpec((tk, D), lambda i, k: (k, i))],
        out_specs=pl.BlockSpec((ns, None, w, dq), lambda i, k: (0, i, 0, 0)),
        out_shape=_sds((ns, 3, w, dq), BF),
        scratch_shapes=[pltpu.VMEM((w, D), F32)],
        compiler_params=_cp((PAR, ARB)),
    )(*ys, dp)


def _tn_rows(h, dbig, ns, tk, n_mats, name):
    T, D = h.shape
    dq = D // ns
    nk = T // tk

    def body(a_ref, b_ref, o_ref, acc):
        k = pl.program_id(1)
        prod = lax.dot_general(a_ref[...], b_ref[...], TN_DIMS, preferred_element_type=F32)

        @pl.when(k == 0)
        def _():
            acc[...] = prod

        @pl.when(k > 0)
        def _():
            acc[...] += prod

        @pl.when(k == nk - 1)
        def _():
            for s in range(ns):
                o_ref[s] = acc[s * dq:(s + 1) * dq, :].astype(o_ref.dtype)

    return pl.pallas_call(
        body, name=name, grid=(n_mats, nk),
        in_specs=[pl.BlockSpec((tk, D), lambda q, k: (k, 0)), pl.BlockSpec((tk, D), lambda q, k: (k, q))],
        out_specs=pl.BlockSpec((ns, None, dq, D), lambda q, k: (0, q, 0, 0)),
        out_shape=_sds((ns, n_mats, dq, D), BF),
        scratch_shapes=[pltpu.VMEM((D, D), F32)],
        compiler_params=_cp((PAR, ARB)),
    )(h, dbig)


def _inproj_fwd(x, nw, wbig, name):
    T, D = x.shape
    nb = wbig.shape[-1]
    tm = min(2 * TM, T)
    bn = min(2048, nb)

    def body(x_ref, nw_ref, w_ref, o_ref, h_ref, h_s):
        @pl.when(pl.program_id(1) == 0)
        def _():
            xv = x_ref[...]
            hb = (xv * _rms_r(xv) * nw_ref[...]).astype(BF)
            h_s[...] = hb
            h_ref[...] = hb

        o_ref[...] = jnp.dot(h_s[...], w_ref[...], preferred_element_type=F32).astype(BF)

    return pl.pallas_call(
        body, name=name, grid=(T // tm, nb // bn),
        in_specs=[pl.BlockSpec((tm, D), lambda i, n: (i, 0)),
                  pl.BlockSpec((1, D), lambda i, n: (0, 0)),
                  pl.BlockSpec((D, bn), lambda i, n: (0, n))],
        out_specs=[pl.BlockSpec((tm, bn), lambda i, n: (i, n)),
                   pl.BlockSpec((tm, D), lambda i, n: (i, 0))],
        out_shape=[_sds((T, nb), BF), _sds((T, D), BF)],
        scratch_shapes=[pltpu.VMEM((tm, D), BF)],
        compiler_params=_cp((PAR, ARB)),
    )(x, nw, wbig)


def _inproj_bwd(dbig, wbig, x, nw, dxin, name, comm=None):
    T, D = x.shape
    nb = wbig.shape[-1]
    tm = min(TM_FFN, T)
    tk = min(2048, nb)
    nk = nb // tk

    def body(a_ref, w_ref, x_ref, nw_ref, dxin_ref, dx_ref, dnw_ref, acc_s):
        i = pl.program_id(0)
        k = pl.program_id(1)
        prod = lax.dot_general(a_ref[...], w_ref[...], NT_DIMS, preferred_element_type=F32)

        @pl.when((i == 0) & (k == 0))
        def _():
            dnw_ref[...] = jnp.zeros_like(dnw_ref)

        @pl.when(k == 0)
        def _():
            acc_s[...] = prod

        @pl.when(k > 0)
        def _():
            acc_s[...] += prod

        @pl.when(k == nk - 1)
        def _():
            dx, dn = _rms_bwd(acc_s[...], x_ref[...], nw_ref[...])
            dx_ref[...] = dxin_ref[...] + dx
            dnw_ref[...] += dn

    tok = pl.BlockSpec((tm, D), lambda i, k: (i, 0))
    row = pl.BlockSpec((1, D), lambda i, k: (0, 0))
    return _call(
        body, name=name, grid=(T // tm, nk), args=(dbig, wbig, x, nw, dxin), comm=comm, vmem_mb=56,
        in_specs=[pl.BlockSpec((tm, tk), lambda i, k: (i, k)),
                  pl.BlockSpec((D, tk), lambda i, k: (0, k)),
                  tok, row, tok],
        out_specs=[tok, row],
        out_shape=[_sds((T, D), F32), _sds((1, D), F32)],
        scratch_shapes=[pltpu.VMEM((tm, D), F32)],
        sem=(ARB, ARB))


CONV_R = 512
CONV_BASE, CONV_GROUP = 0, 3
ATT_BASE, ATT_GROUP = 12, 3
RET_BASE, RET_GROUP = 24, 4
N_SEG = 10


N_IN_BLOCKS = N_SEG * BRANCH_W // LANE


def _orig_block(p):
    nblk = BRANCH_W // LANE
    qa, qr = p - ATT_BASE, p - RET_BASE
    conv = (p % CONV_GROUP) * nblk + p // CONV_GROUP
    att = (7 + qa % ATT_GROUP) * nblk + qa // ATT_GROUP
    ret = (3 + qr % RET_GROUP) * nblk + qr // RET_GROUP
    return jnp.where(p < ATT_BASE, conv, jnp.where(p < RET_BASE, att, ret))


def _copy_blocks(src, in_spec, out_shape, out_spec, grid, name, prev=None):
    def body(*refs):
        refs[-1][...] = refs[0][...]

    in_specs, args, aliases = [in_spec], [src], {}
    if prev is not None:
        in_specs.append(pl.BlockSpec(memory_space=pl.ANY))
        args.append(prev)
        aliases = {1: 0}
    return pl.pallas_call(
        body, name=name, grid=grid, in_specs=in_specs, out_specs=out_spec, out_shape=out_shape,
        input_output_aliases=aliases, compiler_params=_cp(tuple(PAR for _ in grid)),
    )(*args)


def _build_wbig(gates4, win4, name):
    ns, _, dq, D = gates4.shape
    per = win4.shape[-1] // LANE
    shape = _sds((D, 3 * D + N_IN_BLOCKS * LANE), gates4.dtype)
    out = _copy_blocks(gates4, pl.BlockSpec((None, None, dq, D), lambda s, i: (s, i, 0, 0)), shape,
                       pl.BlockSpec((dq, D), lambda s, i: (s, i)), (ns, 3), name + "_gates")
    return _copy_blocks(
        win4, pl.BlockSpec((None, D, LANE), lambda p: (_orig_block(p) // per, 0, _orig_block(p) % per)), shape,
        pl.BlockSpec((D, LANE), lambda p: (0, 3 * D // LANE + p)), (N_IN_BLOCKS,), name + "_in", prev=out)


def _ungroup_dw_in(dwp, ns, name):
    D = dwp.shape[0]
    per = N_IN_BLOCKS // ns
    return _copy_blocks(
        dwp, pl.BlockSpec((D, LANE), lambda p: (0, p)), _sds((ns, D, per * LANE), dwp.dtype),
        pl.BlockSpec((None, D, LANE), lambda p: (_orig_block(p) // per, 0, _orig_block(p) % per)), (N_IN_BLOCKS,), name)


def _seg0(big):
    return (big.shape[1] - N_SEG * BRANCH_W) // LANE


def _group_spec(big, base, group, rows, where):
    first = (_seg0(big) + base) // group
    assert first * group == _seg0(big) + base

    def index(*ids):
        r, g = where(*ids)
        return r, first + g

    return pl.BlockSpec((rows, group * LANE), index)


CU, CB, CC = (slice(k * LANE, (k + 1) * LANE) for k in range(3))
AQ, AK, AV = CU, CB, CC
RQ, RK, RV, RG = (slice(k * LANE, (k + 1) * LANE) for k in range(4))


def _conv_fwd(big, cw, name):
    T = big.shape[0]
    R = min(CONV_R, T)

    def body(g_ref, w_ref, y_ref, z_s):
        z_s[pl.ds(0, 8), :] = jnp.zeros((8, LANE), F32)

        def fill(t, c):
            sl = pl.ds(pl.multiple_of(t * R, R), R)
            z_s[pl.ds(pl.multiple_of(t * R + 8, 8), R), :] = g_ref[sl, CC].astype(F32) * g_ref[sl, CU].astype(F32)
            return c

        lax.fori_loop(0, T // R, fill, 0)
        w0, w1, w2 = w_ref[0:1, :], w_ref[1:2, :], w_ref[2:3, :]

        def step(t, c):
            zz = z_s[pl.ds(pl.multiple_of(t * R, R), R + 8), :]
            z0 = zz[8:]
            z1 = pltpu.roll(zz, 1, 0)[8:]
            z2 = pltpu.roll(zz, 2, 0)[8:]
            sl = pl.ds(pl.multiple_of(t * R, R), R)
            y_ref[sl, :] = (g_ref[sl, CB].astype(F32) * (w2 * z0 + w1 * z1 + w0 * z2)).astype(BF)
            return c

        lax.fori_loop(0, T // R, step, 0)

    return pl.pallas_call(
        body, name=name, grid=(BRANCH_W // LANE,),
        in_specs=[_group_spec(big, CONV_BASE, CONV_GROUP, T, lambda j: (0, j)),
                  pl.BlockSpec((3, LANE), lambda j: (0, j))],
        out_specs=pl.BlockSpec((T, LANE), lambda j: (0, j)),
        out_shape=_sds((T, BRANCH_W), BF),
        scratch_shapes=[pltpu.VMEM((T + 8, LANE), F32)],
        compiler_params=_cp((PAR,)),
    )(big, cw)


def _conv_bwd(big, dy, cw, dbig, name):
    T = big.shape[0]
    R = min(CONV_R, T)

    def body(g_ref, dy_ref, w_ref, _, o_ref, dw_ref, z_s, d_s):
        z_s[pl.ds(0, 8), :] = jnp.zeros((8, LANE), F32)
        d_s[pl.ds(T, 8), :] = jnp.zeros((8, LANE), F32)

        def fill(t, c):
            sl = pl.ds(pl.multiple_of(t * R, R), R)
            z_s[pl.ds(pl.multiple_of(t * R + 8, 8), R), :] = g_ref[sl, CC].astype(F32) * g_ref[sl, CU].astype(F32)
            d_s[sl, :] = dy_ref[sl, :].astype(F32) * g_ref[sl, CB].astype(F32)
            return c

        lax.fori_loop(0, T // R, fill, 0)
        w0, w1, w2 = w_ref[0:1, :], w_ref[1:2, :], w_ref[2:3, :]

        def step(t, carry):
            a0, a1, a2 = carry
            zz = z_s[pl.ds(pl.multiple_of(t * R, R), R + 8), :]
            z0 = zz[8:]
            z1 = pltpu.roll(zz, 1, 0)[8:]
            z2 = pltpu.roll(zz, 2, 0)[8:]
            sl = pl.ds(pl.multiple_of(t * R, R), R)
            dyv = dy_ref[sl, :].astype(F32)
            o_ref[sl, CB] = (dyv * (w2 * z0 + w1 * z1 + w0 * z2)).astype(BF)
            dd = d_s[pl.ds(pl.multiple_of(t * R, R), R + 8), :]
            d0 = dd[:R]
            d1 = pltpu.roll(dd, R + 7, 0)[:R]
            d2 = pltpu.roll(dd, R + 6, 0)[:R]
            dz = w2 * d0 + w1 * d1 + w0 * d2
            o_ref[sl, CC] = (dz * g_ref[sl, CU].astype(F32)).astype(BF)
            o_ref[sl, CU] = (dz * g_ref[sl, CC].astype(F32)).astype(BF)
            a0 = a0 + jnp.sum(d0 * z2, axis=0, keepdims=True)
            a1 = a1 + jnp.sum(d0 * z1, axis=0, keepdims=True)
            a2 = a2 + jnp.sum(d0 * z0, axis=0, keepdims=True)
            return a0, a1, a2

        zero = jnp.zeros((1, LANE), F32)
        a0, a1, a2 = lax.fori_loop(0, T // R, step, (zero, zero, zero))
        dw_ref[0:1, :] = a0
        dw_ref[1:2, :] = a1
        dw_ref[2:3, :] = a2

    group = _group_spec(big, CONV_BASE, CONV_GROUP, T, lambda j: (0, j))
    w = pl.BlockSpec((3, LANE), lambda j: (0, j))
    return pl.pallas_call(
        body, name=name, grid=(BRANCH_W // LANE,),
        in_specs=[group, pl.BlockSpec((T, LANE), lambda j: (0, j)), w, pl.BlockSpec(memory_space=pl.ANY)],
        out_specs=[group, w],
        out_shape=[_sds(dbig.shape, BF), _sds((3, BRANCH_W), F32)],
        scratch_shapes=[pltpu.VMEM((T + 8, LANE), F32), pltpu.VMEM((T + 8, LANE), F32)],
        input_output_aliases={3: 0}, compiler_params=_cp((PAR,)),
    )(big, dy, cw, dbig)


def _ret_tables(T):
    L = min(RET_L, T)
    hh = jnp.arange(H_RET, dtype=F32)
    lg = jnp.log1p(-jnp.exp2(-5.0 - hh))
    n = jnp.arange(L, dtype=F32)
    a = jnp.exp(lg[:, None] * (n + 1.0))
    b = jnp.exp(lg[:, None] * (L - 1.0 - n))
    gl = jnp.exp(lg * L)
    ch = jnp.arange(L) // CHUNK
    m = jnp.exp(lg[:, None, None] * jnp.abs(n[:, None] - n[None, :])) * (ch[None, :] <= ch[:, None]).astype(F32)
    inv_freq = ROPE_BASE ** (-jnp.linspace(0.0, 1.0, DK_RET // 2, dtype=F32))
    ang = jnp.arange(T, dtype=F32)[:, None] * inv_freq[None, :]
    cos, sin = jnp.cos(ang), jnp.sin(ang)
    return dict(
        L=L, M=m,
        a=jnp.broadcast_to(a[:, :, None], (H_RET, L, DK_RET)),
        b=jnp.broadcast_to(b[:, :, None], (H_RET, L, DK_RET)),
        gl=jnp.broadcast_to(gl[:, None, None], (H_RET, 1, DK_RET)),
        cos=jnp.concatenate([cos, cos], axis=-1), sin=jnp.concatenate([-sin, sin], axis=-1))


def _rot(x, cs, sn):
    return x * cs + pltpu.roll(x, DK_RET // 2, 1) * sn


def _unrot(dy, cs, sn):
    return dy * cs + pltpu.roll(dy * sn, DK_RET // 2, 1)


def _ret_fwd(big, tb, name, comm=None):
    T = big.shape[0]
    L = tb["L"]
    nsc = T // L
    scale = DK_RET ** -0.5

    def body(x_ref, cos_ref, sin_ref, m_ref, a_ref, b_ref, gl_ref, y_ref, o_ref, st_ref, s_s):
        @pl.when(pl.program_id(1) == 0)
        def _():
            s_s[...] = jnp.zeros_like(s_s)

        cs, sn = cos_ref[...], sin_ref[...]
        qt = _rot(x_ref[:, RQ].astype(F32), cs, sn) * scale
        kt = _rot(x_ref[:, RK].astype(F32), cs, sn)
        qb, kb, vb = qt.astype(BF), kt.astype(BF), x_ref[:, RV]
        s_prev = s_s[...]
        st_ref[...] = s_prev
        p = lax.dot_general(qb, kb, NT_DIMS, preferred_element_type=F32) * m_ref[...]
        o = (jnp.dot(p.astype(BF), vb, preferred_element_type=F32)
             + jnp.dot((qt * a_ref[...]).astype(BF), s_prev.astype(BF), preferred_element_type=F32))
        s_s[...] = s_prev * gl_ref[...] + lax.dot_general((kt * b_ref[...]).astype(BF), vb, TN_DIMS,
                                                         preferred_element_type=F32)
        o_ref[...] = o
        gv = x_ref[:, RG].astype(F32)
        y_ref[...] = (gv * _sigmoid(gv) * o * _rms_r(o)).astype(BF)

    tab = pl.BlockSpec((L, DK_RET), lambda h, i: (i, 0))
    per_head = pl.BlockSpec((None, L, DK_RET), lambda h, i: (h, 0, 0))
    out = pl.BlockSpec((L, LANE), lambda h, i: (i, h))
    return _call(
        body, name=name, grid=(H_RET, nsc), comm=comm,
        args=(big, tb["cos"], tb["sin"], tb["M"], tb["a"], tb["b"], tb["gl"]),
        in_specs=[_group_spec(big, RET_BASE, RET_GROUP, L, lambda h, i: (i, h)), tab, tab,
                  pl.BlockSpec((None, L, L), lambda h, i: (h, 0, 0)), per_head, per_head,
                  pl.BlockSpec((None, 1, DK_RET), lambda h, i: (h, 0, 0))],
        out_specs=[out, out, pl.BlockSpec((None, None, DK_RET, DK_RET), lambda h, i: (i, h, 0, 0))],
        out_shape=[_sds((T, BRANCH_W), BF), _sds((T, BRANCH_W), F32), _sds((nsc, H_RET, DK_RET, DK_RET), F32)],
        scratch_shapes=[pltpu.VMEM((DK_RET, DK_RET), F32)],
        sem=(ARB, ARB))


def _ret_bwd(big, o, st, dy, tb, dbig, name):
    T = big.shape[0]
    L = tb["L"]
    nsc = T // L
    scale = DK_RET ** -0.5

    def body(x_ref, cos_ref, sin_ref, m_ref, a_ref, b_ref, gl_ref, o_ref, st_ref, dy_ref, _, d_ref, ds_s):
        @pl.when(pl.program_id(1) == 0)
        def _():
            ds_s[...] = jnp.zeros_like(ds_s)

        cs, sn = cos_ref[...], sin_ref[...]
        mm, av, bv = m_ref[...], a_ref[...], b_ref[...]
        qt = _rot(x_ref[:, RQ].astype(F32), cs, sn) * scale
        kt = _rot(x_ref[:, RK].astype(F32), cs, sn)
        qb, kb, vb = qt.astype(BF), kt.astype(BF), x_ref[:, RV]
        pb = (lax.dot_general(qb, kb, NT_DIMS, preferred_element_type=F32) * mm).astype(BF)
        ov = o_ref[...]
        r = _rms_r(ov)
        oh = ov * r
        gv = x_ref[:, RG].astype(F32)
        sg = _sigmoid(gv)
        dyv = dy_ref[...].astype(F32)
        d_ref[:, RG] = (dyv * oh * (sg * (1.0 + gv * (1.0 - sg)))).astype(BF)
        doh = dyv * gv * sg
        dob = (r * (doh - oh * jnp.mean(doh * oh, axis=-1, keepdims=True))).astype(BF)
        dsb = ds_s[...].astype(BF)
        spb = st_ref[...].astype(BF)
        dpb = (lax.dot_general(dob, vb, NT_DIMS, preferred_element_type=F32) * mm).astype(BF)
        dqt = (jnp.dot(dpb, kb, preferred_element_type=F32)
               + lax.dot_general(dob, spb, NT_DIMS, preferred_element_type=F32) * av)
        dkt = (lax.dot_general(dpb, qb, TN_DIMS, preferred_element_type=F32)
               + lax.dot_general(vb, dsb, NT_DIMS, preferred_element_type=F32) * bv)
        dv = (lax.dot_general(pb, dob, TN_DIMS, preferred_element_type=F32)
              + jnp.dot((kt * bv).astype(BF), dsb, preferred_element_type=F32))
        ds_s[...] = ds_s[...] * gl_ref[...] + lax.dot_general((qt * av).astype(BF), dob, TN_DIMS,
                                                              preferred_element_type=F32)
        d_ref[:, RQ] = (_unrot(dqt, cs, sn) * scale).astype(BF)
        d_ref[:, RK] = _unrot(dkt, cs, sn).astype(BF)
        d_ref[:, RV] = dv.astype(BF)

    def rev(i):
        return nsc - 1 - i

    group = _group_spec(big, RET_BASE, RET_GROUP, L, lambda h, i: (rev(i), h))
    tab = pl.BlockSpec((L, DK_RET), lambda h, i: (rev(i), 0))
    per_head = pl.BlockSpec((None, L, DK_RET), lambda h, i: (h, 0, 0))
    out = pl.BlockSpec((L, LANE), lambda h, i: (rev(i), h))
    return pl.pallas_call(
        body, name=name, grid=(H_RET, nsc),
        in_specs=[group, tab, tab,
                  pl.BlockSpec((None, L, L), lambda h, i: (h, 0, 0)), per_head, per_head,
                  pl.BlockSpec((None, 1, DK_RET), lambda h, i: (h, 0, 0)),
                  out, pl.BlockSpec((None, None, DK_RET, DK_RET), lambda h, i: (rev(i), h, 0, 0)), out,
                  pl.BlockSpec(memory_space=pl.ANY)],
        out_specs=group,
        out_shape=_sds(dbig.shape, BF),
        scratch_shapes=[pltpu.VMEM((DK_RET, DK_RET), F32)],
        input_output_aliases={10: 0}, compiler_params=_cp((PAR, ARB)),
    )(big, tb["cos"], tb["sin"], tb["M"], tb["a"], tb["b"], tb["gl"], o, st, dy, dbig)


def _relbias_onehot(n):
    mm = lax.broadcasted_iota(jnp.int32, (RB_PAD, ATT_TOEP), 1)
    rr = lax.broadcasted_iota(jnp.int32, (RB_PAD, ATT_TOEP), 0)
    idx = jnp.clip(n + ATT_TOEP - mm, 0, 2 * REL_CLIP)
    return (rr == idx).astype(F32)


def _split3(x):
    hi = x.astype(BF).astype(F32)
    mid = (x - hi).astype(BF).astype(F32)
    lo = x - hi - mid
    return jnp.concatenate([hi, mid, lo], axis=0).astype(BF)


def _join3(y):
    k = y.shape[0] // 3
    return (y[:k] + y[k:2 * k]) + y[2 * k:]


def _relbias_expand(rel_bias, name, comm=None):
    far = ATT_SPAN - ATT_TOEP
    n_layers = rel_bias.shape[0]
    rbp = jnp.pad(rel_bias, ((0, 0), (0, 0), (0, RB_PAD - N_REL)))

    def body(rb_ref, o_ref):
        for l in range(n_layers):
            rb = rb_ref[l]
            const = jnp.broadcast_to(rb[:, 2 * REL_CLIP:2 * REL_CLIP + 1], (H_ATT, far))
            rb3 = _split3(rb)

            def row(n, c):
                toep = _join3(jnp.dot(rb3, _relbias_onehot(n).astype(BF), preferred_element_type=F32))
                m = lax.broadcasted_iota(jnp.int32, (1, ATT_SPAN), 1)
                d = n // CHUNK + N_PREV - m // CHUNK
                neg = jnp.where((d >= 0) & (d <= N_PREV), 0.0, NEG_INF).astype(F32)
                o_ref[l, n] = jnp.concatenate([const, toep], axis=1) + neg
                return c

            lax.fori_loop(0, ATT_TQ, row, 0)

    (out,), extra = _call(
        body, name=name, args=(rbp,), comm=comm,
        in_specs=[pl.BlockSpec(memory_space=pltpu.VMEM)],
        out_specs=[pl.BlockSpec(memory_space=pltpu.VMEM)],
        out_shape=[_sds((n_layers, ATT_TQ, H_ATT, ATT_SPAN), F32)])
    return jnp.transpose(out, (0, 2, 1, 3)), extra


def _relbias_grad(dbt, name):
    far = ATT_SPAN - ATT_TOEP

    def body(d_ref, o_ref):
        def row(n, carry):
            acc, cs = carry
            dn = d_ref[n]
            acc = acc + _join3(lax.dot_general(_split3(dn[:, far:]), _relbias_onehot(n).astype(BF), NT_DIMS,
                                               preferred_element_type=F32))
            cs = cs + jnp.sum(dn[:, :far], axis=1, keepdims=True)
            return acc, cs

        acc, cs = lax.fori_loop(0, ATT_TQ, row, (jnp.zeros((H_ATT, RB_PAD), F32), jnp.zeros((H_ATT, 1), F32)))
        rr = lax.broadcasted_iota(jnp.int32, (H_ATT, RB_PAD), 1)
        o_ref[...] = acc + jnp.where(rr == 2 * REL_CLIP, cs, 0.0)

    return pl.pallas_call(
        body, name=name,
        in_specs=[pl.BlockSpec(memory_space=pltpu.VMEM)],
        out_specs=pl.BlockSpec(memory_space=pltpu.VMEM),
        out_shape=_sds((H_ATT, RB_PAD), F32),
    )(dbt)


def _att_pad_fill(dst_s, src_ref, cols, T):
    dst_s[pl.ds(0, ATT_PAD), :] = jnp.zeros((ATT_PAD, LANE), dst_s.dtype)
    R = min(512, T)

    def cp(t, c):
        dst_s[pl.ds(pl.multiple_of(ATT_PAD + t * R, LANE), R), :] = src_ref[pl.ds(pl.multiple_of(t * R, R), R), cols]
        return c

    lax.fori_loop(0, T // R, cp, 0)


ATT_WIN = ATT_SUB * ATT_TQ + ATT_PAD


def _att_probs(s_full, sub, bias, t0):
    s = s_full[sub * ATT_TQ:(sub + 1) * ATT_TQ, sub * ATT_TQ:sub * ATT_TQ + ATT_SPAN] * (DH_ATT ** -0.5) + bias
    key_pos = t0 + sub * ATT_TQ - ATT_PAD + lax.broadcasted_iota(jnp.int32, (1, ATT_SPAN), 1)
    s = jnp.where(key_pos >= 0, s, NEG_INF)
    p = jnp.exp(s - jnp.max(s, axis=-1, keepdims=True))
    return p * (1.0 / jnp.sum(p, axis=-1, keepdims=True))


def _att_band(tiles):
    rows = []
    for sub, t in enumerate(tiles):
        parts = []
        if sub:
            parts.append(jnp.zeros((ATT_TQ, sub * ATT_TQ), BF))
        parts.append(t)
        if sub < ATT_SUB - 1:
            parts.append(jnp.zeros((ATT_TQ, (ATT_SUB - 1 - sub) * ATT_TQ), BF))
        rows.append(jnp.concatenate(parts, axis=1))
    return jnp.concatenate(rows, axis=0)


def _att_head_masks(x):
    first = lax.broadcasted_iota(jnp.int32, (1, LANE), 1) < DH_ATT
    zero = jnp.zeros_like(x)
    return first, (jnp.where(first, x, zero), jnp.where(first, zero, x))


def _att_fwd(big, bias, name, comm=None):
    T = big.shape[0]
    rows = ATT_SUB * ATT_TQ
    nt = T // rows

    def body(x_ref, b_ref, y_ref, kp_s, vp_s):
        i = pl.program_id(1)

        @pl.when(i == 0)
        def _():
            _att_pad_fill(kp_s, x_ref, AK, T)
            _att_pad_fill(vp_s, x_ref, AV, T)

        t0 = pl.multiple_of(i * rows, rows)
        kw = kp_s[pl.ds(t0, ATT_WIN), :]
        vw = vp_s[pl.ds(t0, ATT_WIN), :]
        first, qm = _att_head_masks(x_ref[pl.ds(t0, rows), AQ])
        outs = []
        for hh in range(2):
            s_full = lax.dot_general(qm[hh], kw, NT_DIMS, preferred_element_type=F32)
            band = _att_band([_att_probs(s_full, sub, b_ref[hh], t0).astype(BF) for sub in range(ATT_SUB)])
            outs.append(jnp.dot(band, vw, preferred_element_type=F32))
        y_ref[...] = jnp.where(first, outs[0], outs[1]).astype(BF)

    return _call(
        body, name=name, grid=(H_ATT // 2, nt), args=(big, bias), comm=comm,
        in_specs=[_group_spec(big, ATT_BASE, ATT_GROUP, T, lambda p, i: (0, p)),
                  pl.BlockSpec((2, ATT_TQ, ATT_SPAN), lambda p, i: (p, 0, 0))],
        out_specs=[pl.BlockSpec((rows, LANE), lambda p, i: (i, p))],
        out_shape=[_sds((T, BRANCH_W), BF)],
        scratch_shapes=[pltpu.VMEM((T + ATT_PAD, LANE), BF), pltpu.VMEM((T + ATT_PAD, LANE), BF)],
        sem=(ARB, ARB))


def _att_bwd(big, bias, dy, dbig, name, comm=None):
    T = big.shape[0]
    rows = ATT_SUB * ATT_TQ
    nt = T // rows
    scale = DH_ATT ** -0.5

    def body(x_ref, b_ref, dy_ref, _, d_ref, db_ref, kp_s, vp_s, dk_s, dv_s):
        i = pl.program_id(1)

        @pl.when(i == 0)
        def _():
            _att_pad_fill(kp_s, x_ref, AK, T)
            _att_pad_fill(vp_s, x_ref, AV, T)
            dk_s[...] = jnp.zeros_like(dk_s)
            dv_s[...] = jnp.zeros_like(dv_s)
            db_ref[...] = jnp.zeros_like(db_ref)

        t0 = pl.multiple_of(i * rows, rows)
        win = pl.ds(t0, ATT_WIN)
        kw = kp_s[win, :]
        vw = vp_s[win, :]
        first, qm = _att_head_masks(x_ref[pl.ds(t0, rows), AQ])
        _, dom = _att_head_masks(dy_ref[...])
        dqs, dkt, dvt = [], None, None
        for hh in range(2):
            s_full = lax.dot_general(qm[hh], kw, NT_DIMS, preferred_element_type=F32)
            dp_full = lax.dot_general(dom[hh], vw, NT_DIMS, preferred_element_type=F32)
            ps, dss, db = [], [], None
            for sub in range(ATT_SUB):
                pn = _att_probs(s_full, sub, b_ref[hh], t0)
                dp = dp_full[sub * ATT_TQ:(sub + 1) * ATT_TQ, sub * ATT_TQ:sub * ATT_TQ + ATT_SPAN]
                ds = pn * (dp - jnp.sum(dp * pn, axis=-1, keepdims=True))
                db = ds if db is None else db + ds
                ps.append(pn.astype(BF))
                dss.append(ds.astype(BF))
            db_ref[hh] += db
            ds_band, p_band = _att_band(dss), _att_band(ps)
            dqs.append(jnp.dot(ds_band, kw, preferred_element_type=F32))
            qt = jnp.transpose(qm[hh].astype(F32)).astype(BF)
            dot_ = jnp.transpose(dom[hh].astype(F32)).astype(BF)
            dk_h = jnp.dot(qt, ds_band, preferred_element_type=F32)
            dv_h = jnp.dot(dot_, p_band, preferred_element_type=F32)
            dkt = dk_h if dkt is None else dkt + dk_h
            dvt = dv_h if dvt is None else dvt + dv_h
        d_ref[pl.ds(t0, rows), AQ] = (jnp.where(first, dqs[0], dqs[1]) * scale).astype(BF)
        dk_s[win, :] += jnp.transpose(dkt) * scale
        dv_s[win, :] += jnp.transpose(dvt)

        @pl.when(i == nt - 1)
        def _():
            R = min(512, T)

            def cp(t, c):
                src = pl.ds(pl.multiple_of(ATT_PAD + t * R, LANE), R)
                dst = pl.ds(pl.multiple_of(t * R, R), R)
                d_ref[dst, AK] = dk_s[src, :].astype(BF)
                d_ref[dst, AV] = dv_s[src, :].astype(BF)
                return c

            lax.fori_loop(0, T // R, cp, 0)

    group = _group_spec(big, ATT_BASE, ATT_GROUP, T, lambda p, i: (0, p))
    tile = pl.BlockSpec((rows, LANE), lambda p, i: (i, p))
    bspec = pl.BlockSpec((2, ATT_TQ, ATT_SPAN), lambda p, i: (p, 0, 0))
    return _call(
        body, name=name, grid=(H_ATT // 2, nt), args=(big, bias, dy, dbig), comm=comm, aliases={3: 0}, vmem_mb=56,
        in_specs=[group, bspec, tile, pl.BlockSpec(memory_space=pl.ANY)],
        out_specs=[group, bspec],
        out_shape=[_sds(dbig.shape, BF), _sds((H_ATT, ATT_TQ, ATT_SPAN), F32)],
        scratch_shapes=[pltpu.VMEM((T + ATT_PAD, LANE), BF), pltpu.VMEM((T + ATT_PAD, LANE), BF),
                        pltpu.VMEM((T + ATT_PAD, LANE), F32), pltpu.VMEM((T + ATT_PAD, LANE), F32)],
        sem=(ARB, ARB))


def _merge_fwd(x1, big, ys, wb, wo, name):
    T, D = x1.shape
    tm = min(TM, T)

    def body(x_ref, gp_ref, yc_ref, yr_ref, ya_ref, wb_ref, wo_ref, x2_ref, p_ref, mg_ref):
        merged = jnp.zeros((tm, D), F32)
        for i, y_ref in enumerate((yc_ref, yr_ref, ya_ref)):
            cols = slice(i * D, (i + 1) * D)
            pb = jnp.dot(y_ref[...], wb_ref[i], preferred_element_type=F32).astype(BF)
            p_ref[:, cols] = pb
            merged = merged + _sigmoid(gp_ref[:, cols].astype(F32)) * pb.astype(F32)
        mb = merged.astype(BF)
        mg_ref[...] = mb
        x2_ref[...] = x_ref[...] + jnp.dot(mb, wo_ref[...], preferred_element_type=F32)

    tok = pl.BlockSpec((tm, D), lambda i: (i, 0))
    wide = pl.BlockSpec((tm, 3 * D), lambda i: (i, 0))
    yspec = pl.BlockSpec((tm, BRANCH_W), lambda i: (i, 0))
    return pl.pallas_call(
        body, name=name, grid=(T // tm,),
        in_specs=[tok, wide, yspec, yspec, yspec,
                  pl.BlockSpec((3, BRANCH_W, D), lambda i: (0, 0, 0)),
                  pl.BlockSpec((D, D), lambda i: (0, 0))],
        out_specs=[tok, wide, tok],
        out_shape=[_sds((T, D), F32), _sds((T, 3 * D), BF), _sds((T, D), BF)],
        compiler_params=_cp((PAR,)),
    )(x1, big, *ys, wb, wo)


def _merge_bwd(dx2, big, p, wb, wo, name):
    T, D = dx2.shape
    tm = min(TM, T)

    def body(dx_ref, gp_ref, p_ref, wb_ref, wo_ref, dp_ref, dgp_ref, dyc_ref, dyr_ref, dya_ref, dxb_ref):
        dxb = dx_ref[...].astype(BF)
        dxb_ref[...] = dxb
        dm = lax.dot_general(dxb, wo_ref[...], NT_DIMS, preferred_element_type=F32)
        for i, dy_ref in enumerate((dyc_ref, dyr_ref, dya_ref)):
            cols = slice(i * D, (i + 1) * D)
            gt = _sigmoid(gp_ref[:, cols].astype(F32))
            dpb = (dm * gt).astype(BF)
            dp_ref[:, cols] = dpb
            dgp_ref[:, cols] = (dm * p_ref[:, cols].astype(F32) * gt * (1.0 - gt)).astype(BF)
            dy_ref[...] = lax.dot_general(dpb, wb_ref[i], NT_DIMS, preferred_element_type=F32).astype(BF)

    tok = pl.BlockSpec((tm, D), lambda i: (i, 0))
    wide = pl.BlockSpec((tm, 3 * D), lambda i: (i, 0))
    yspec = pl.BlockSpec((tm, BRANCH_W), lambda i: (i, 0))
    return pl.pallas_call(
        body, name=name, grid=(T // tm,),
        in_specs=[tok, wide, wide,
                  pl.BlockSpec((3, BRANCH_W, D), lambda i: (0, 0, 0)),
                  pl.BlockSpec((D, D), lambda i: (0, 0))],
        out_specs=[wide, wide, yspec, yspec, yspec, tok],
        out_shape=[_sds((T, 3 * D), BF), _sds(big.shape, BF)] + [_sds((T, BRANCH_W), BF)] * 3 + [_sds((T, D), BF)],
        compiler_params=_cp((PAR,)),
    )(dx2, big, p, wb, wo)


def _loss_head(x, tgt, fw, name):
    T, D = x.shape
    tm = min(TM, T)

    def body(x_ref, t_ref, w_ref, loss_ref, dx_ref, dw_ref):
        @pl.when(pl.program_id(0) == 0)
        def _():
            loss_ref[...] = jnp.zeros_like(loss_ref)
            dw_ref[...] = jnp.zeros_like(dw_ref)

        xv = x_ref[...]
        wv = w_ref[...]
        e = xv * _rms_r(xv) * wv - t_ref[...]
        loss_ref[...] += 0.5 * jnp.sum(jnp.mean(e * e, axis=-1, keepdims=True))
        dx, dn = _rms_bwd(e * (1.0 / D), xv, wv)
        dx_ref[...] = dx
        dw_ref[...] += dn

    tok = pl.BlockSpec((tm, D), lambda i: (i, 0))
    return pl.pallas_call(
        body, name=name, grid=(T // tm,),
        in_specs=[tok, tok, pl.BlockSpec((1, D), lambda i: (0, 0))],
        out_specs=[pl.BlockSpec((8, LANE), lambda i: (0, 0)), tok, pl.BlockSpec((1, D), lambda i: (0, 0))],
        out_shape=[_sds((8, LANE), F32), _sds((T, D), F32), _sds((1, D), F32)],
        compiler_params=_cp((ARB,)),
    )(x, tgt, fw)


def _block_rows(rows, cols):
    cap = max(8, (1 << 18) // cols)
    best = None
    for r in range(8, rows + 1, 8):
        if rows % r == 0 and r <= cap:
            best = r
    return best if best is not None else rows


def _sum8(l_ref):
    def four(base):
        return ((l_ref[base + 3].astype(F32) + l_ref[base].astype(F32)) + l_ref[base + 1].astype(F32)
                ) + l_ref[base + 2].astype(F32)

    return four(0) + four(4)


def _sum_adamw(lands, w, m, v, name):
    _, rows, cols = lands[0].shape
    br = _block_rows(rows, cols)
    nb = rows // br
    n_layers = len(lands)

    def body(*refs):
        l_refs = refs[:n_layers]
        w_ref, m_ref, v_ref, g_ref, d_ref, nm_ref, nv_ref = refs[n_layers:]
        i = pl.program_id(0)
        for l, l_ref in enumerate(l_refs):
            @pl.when((i >= l * nb) & (i < (l + 1) * nb))
            def _():
                g = _sum8(l_ref)
                d, nm, nv = _adamw_math(w_ref[...], g, m_ref[...], v_ref[...])
                g_ref[...] = g
                d_ref[...] = d
                nm_ref[...] = nm
                nv_ref[...] = nv

    def land_spec(l):
        return pl.BlockSpec((2 * N_SHARD, br, cols), lambda i: (0, jnp.clip(i - l * nb, 0, nb - 1), 0))

    blk = pl.BlockSpec((br, cols), lambda i: (i, 0))
    return pl.pallas_call(
        body, name=name, grid=(n_layers * nb,),
        in_specs=[land_spec(l) for l in range(n_layers)] + [blk] * 3, out_specs=[blk] * 4,
        out_shape=[_sds((n_layers * rows, cols), F32)] * 4,
        compiler_params=_cp((PAR,)),
    )(*lands, w, m, v)


def _adamw_math(w, g, m, v):
    m = ADAM_B1 * m + (1.0 - ADAM_B1) * g
    v = ADAM_B2 * v + (1.0 - ADAM_B2) * (g * g)
    m_hat = m / (1.0 - ADAM_B1 ** ADAM_STEP)
    v_hat = v / (1.0 - ADAM_B2 ** ADAM_STEP)
    delta = -ADAM_LR * (m_hat / (jnp.sqrt(v_hat) + ADAM_EPS) + ADAM_WD * w)
    return delta, m, v


def _adamw(w, g, m, v, name):
    rows, cols = w.shape
    br = _block_rows(rows, cols)

    def body(w_ref, g_ref, m_ref, v_ref, d_ref, nm_ref, nv_ref):
        d, nm, nv = _adamw_math(w_ref[...], g_ref[...], m_ref[...], v_ref[...])
        d_ref[...] = d
        nm_ref[...] = nm
        nv_ref[...] = nv

    blk = pl.BlockSpec((br, cols), lambda i: (i, 0))
    return pl.pallas_call(
        body, name=name, grid=(rows // br,),
        in_specs=[blk] * 4, out_specs=[blk] * 3,
        out_shape=[_sds((rows, cols), F32)] * 3,
        compiler_params=_cp((PAR,)),
    )(w, g, m, v)


def _allreduce_small(v, name):
    rows = v.shape[0]
    flips = [(fx, fy, fc) for fx in (0, 1) for fy in (0, 1) for fc in (0, 1) if fx or fy or fc]

    def body(v_ref, o_ref, all_s, ssem, rsem):
        x, y, c = _place()

        def peer(f):
            return (x + f[0] - 2 * x * f[0], y + f[1] - 2 * y * f[1], c + f[2] - 2 * c * f[2])

        def slot(p):
            return all_s.at[4 * p[0] + 2 * p[1] + p[2]]

        def copy(k, f, owner):
            return pltpu.make_async_remote_copy(
                src_ref=v_ref, dst_ref=slot(owner), send_sem=ssem.at[k], recv_sem=rsem.at[k],
                device_id=peer(f), device_id_type=MESH)

        sends = [copy(k, f, (x, y, c)) for k, f in enumerate(flips)]
        for cp in sends:
            cp.start()
        all_s[4 * x + 2 * y + c] = v_ref[...]
        for k, f in enumerate(flips):
            copy(k, f, peer(f)).wait_recv()
        for cp in sends:
            cp.wait_send()
        acc = all_s[0]
        for d in range(1, 8):
            acc = acc + all_s[d]
        o_ref[...] = acc

    return pl.pallas_call(
        body, name=name,
        in_specs=[pl.BlockSpec(memory_space=pltpu.VMEM)],
        out_specs=pl.BlockSpec(memory_space=pltpu.VMEM),
        out_shape=_sds((rows, LANE), F32),
        scratch_shapes=[pltpu.VMEM((8, rows, LANE), F32), pltpu.SemaphoreType.DMA((7,)), pltpu.SemaphoreType.DMA((7,))],
    )(v)


BIG_NAMES = ("ffn1_w_gate", "ffn1_w_up", "ffn1_w_down", "w_in", "w_branch", "w_merge_gate", "w_out",
             "ffn2_w_gate", "ffn2_w_up", "ffn2_w_down")


FFN1 = ("ffn1_w_gate", "ffn1_w_up", "ffn1_w_down")
FFN2 = ("ffn2_w_gate", "ffn2_w_up", "ffn2_w_down")
MIX_IN = ("w_in", "w_merge_gate")
MIX_OUT = ("w_branch", "w_out")


def _keys(names, l):
    return [(n, l) for n in names]


def _local_step(x, tgt, small, convw_full, biases, wx, n_layers):
    T, D = x.shape
    L = n_layers
    ns = N_SHARD
    dq = D // ns
    W = wx.w

    def hosted(call, keys, scatter=False):
        comm = wx.pieces(keys, scatter)
        main, extra = call(comm)
        if comm is not None:
            wx.arrived(keys, extra, scatter)
        return main

    def mixer_views(l):
        return _build_wbig(W[("w_merge_gate", l)], W[("w_in", l)], f"wbig_{l}")

    def out_views(l):
        wb4 = W[("w_branch", l)]
        wb = _copy_blocks(wb4, pl.BlockSpec((None, None, BRANCH_W, dq), lambda s_, i: (s_, i, 0, 0)),
                          _sds((3, BRANCH_W, D), wb4.dtype),
                          pl.BlockSpec((None, BRANCH_W, dq), lambda s_, i: (i, 0, s_)), (ns, 3), f"w_branch_whole_{l}")
        wo = W[("w_out", l)].reshape(D, D)
        return wb, wo

    tb = _ret_tables(T)

    saved = []
    h = x
    for l in range(L):
        s = {"x0": h}
        nxt = l + 1
        x1, s["g1"], s["u1"] = hosted(
            lambda c: _ffn_fwd(h, small["ffn1_norm"][l][None], W[("ffn1_w_gate", l)], W[("ffn1_w_up", l)],
                               W[("ffn1_w_down", l)], f"ffn1_fwd_{l}", comm=c), _keys(MIX_IN + MIX_OUT, l))
        s["x1"] = x1
        s["wbig"] = mixer_views(l)
        big, s["h"] = _inproj_fwd(x1, small["mix_norm"][l][None], s["wbig"], f"inproj_fwd_{l}")
        s["big"] = big
        s["bias"] = biases[l]
        s["yc"] = _conv_fwd(big, convw_full[l], f"conv_fwd_{l}")
        s["yr"], s["o"], s["st"] = hosted(lambda c: _ret_fwd(big, tb, f"ret_fwd_{l}", comm=c), [])
        (s["ya"],) = hosted(lambda c: _att_fwd(big, s["bias"], f"att_fwd_{l}", comm=c), _keys(FFN2, l))
        s["wb"], s["wo"] = out_views(l)
        x2, s["p"], s["mg"] = _merge_fwd(x1, big, (s["yc"], s["yr"], s["ya"]), s["wb"], s["wo"], f"merge_fwd_{l}")
        s["x2"] = x2
        h, s["g2"], s["u2"] = hosted(
            lambda c: _ffn_fwd(x2, small["ffn2_norm"][l][None], W[("ffn2_w_gate", l)], W[("ffn2_w_up", l)],
                               W[("ffn2_w_down", l)], f"ffn2_fwd_{l}", comm=c), _keys(FFN1, nxt) if nxt < L else [])
        saved.append(s)

    loss_p, dx, d_final = _loss_head(h, tgt, small["final_norm"][None], "loss_head")

    gs = {"final_norm": d_final[0]}
    for k in ("ffn1_norm", "mix_norm", "ffn2_norm", "rel_bias", "conv_w"):
        gs[k] = [None] * L
    tk = min(2048, T)
    nk = T // tk

    def ffn_back(pre, l, dxo, x_in, g, u, first_keys, second_keys, between=None):
        nw = small[pre + "_norm"][l][None]
        dgv, duv, av, hb, dacc = hosted(
            lambda c: _ffn_bwd_hidden(dxo, x_in, nw, g, u, W[(pre + "_w_down", l)], f"{pre}_bwd_hidden_{l}", comm=c),
            first_keys, scatter=True)
        parts = (hb, dgv, duv, av, dacc)
        if between is not None:
            second_keys = between(parts)
        dxn, dn = hosted(
            lambda c: _ffn_bwd_resid(dgv, duv, W[(pre + "_w_gate", l)], W[(pre + "_w_up", l)], x_in, nw, dxo,
                                     f"{pre}_bwd_resid_{l}", comm=c),
            second_keys, scatter=True)
        gs[pre + "_norm"][l] = dn[0]
        return dxn, parts

    def ffn_grads(pre, l, hb, dgv, duv, av, dacc, chain=False):
        fs = dgv.shape[-1]
        tkf = min(2 * tk, T)
        hspec = pl.BlockSpec((tkf, D), lambda p, q, k: (k, 0))
        sspec = pl.BlockSpec((None, tkf, fs), lambda p, q, k: (p, k, 0))
        down_spec = pl.BlockSpec((None, fs, D), lambda p, q, k: (p, 0, 0))
        jobs = [(pre + "_w_gate", dgv, hb, sspec, hspec, (ns, fs, D), down_spec),
                (pre + "_w_up", duv, hb, sspec, hspec, (ns, fs, D), down_spec),
                (pre + "_w_down", av, dacc, sspec, hspec, (ns, fs, D), down_spec)]
        for idx, (nm, a, b, a_spec, b_spec, shape, o_spec) in enumerate(jobs):
            def product(c):
                r = _tn(a, b, a_spec, b_spec, _sds(shape, BF), o_spec, (ns, 1, T // tkf), f"d{nm}_{l}", comm=c)
                return (r, []) if c is None else r
            keys = [(jobs[0][0], l)] if chain and idx == 2 else []
            wx.g[(nm, l)] = hosted(product, keys, scatter=True)
        return [(jobs[1][0], l), (jobs[2][0], l)] if chain else []

    for l in reversed(range(L)):
        s = saved[l]
        above = _keys(FFN1, l + 1) if l + 1 < L else []
        dx, parts = ffn_back("ffn2", l, dx, s["x2"], s["g2"], s["u2"], above[:1], above[1:])
        ffn_grads("ffn2", l, *parts)
        dp, dbig, dyc, dyr, dya, dxb = _merge_bwd(dx, s["big"], s["p"], s["wb"], s["wo"], f"merge_bwd_{l}")
        wx.g[("w_out", l)] = _tn_rows(s["mg"], dxb, ns, tk, 1, f"dw_out_{l}").reshape(ns, dq, D)
        wx.g[("w_branch", l)] = _tn_branches((s["yc"], s["yr"], s["ya"]), dp, ns, tk, f"dw_branch_{l}")
        dbig, dcw = _conv_bwd(s["big"], dyc, convw_full[l], dbig, f"conv_bwd_{l}")
        gs["conv_w"][l] = dcw
        dbig = _ret_bwd(s["big"], s["o"], s["st"], dyr, tb, dbig, f"ret_bwd_{l}")
        dbig, dbias = hosted(lambda c: _att_bwd(s["big"], s["bias"], dya, dbig, f"att_bwd_{l}", comm=c),
                             _keys(FFN2, l), scatter=True)
        gs["rel_bias"][l] = _relbias_grad(jnp.transpose(dbias, (1, 0, 2)), f"relbias_grad_{l}")[:, :N_REL]
        n_in = N_SEG * BRANCH_W
        bn = 1024 if (3 * D) % 1024 == 0 else BRANCH_W
        dwp = _tn(s["h"], dbig, pl.BlockSpec((tk, D), lambda p, q, k: (k, 0)),
                  pl.BlockSpec((tk, bn), lambda p, q, k: (k, 3 * D // bn + q)),
                  _sds((D, n_in), BF), pl.BlockSpec((D, bn), lambda p, q, k: (0, q)), (1, n_in // bn, nk), f"dw_in_{l}")
        wx.g[("w_in", l)] = _ungroup_dw_in(dwp, ns, f"dw_in_shards_{l}")
        wx.g[("w_merge_gate", l)] = _tn_rows(s["h"], dbig, ns, tk, 3, f"dw_merge_gate_{l}")
        dx, dn = hosted(
            lambda c: _inproj_bwd(dbig, s["wbig"], s["x1"], small["mix_norm"][l][None], dx, f"inproj_bwd_{l}", comm=c),
            [("w_in", l)], scatter=True)
        gs["mix_norm"][l] = dn[0]
        rest = [("w_merge_gate", l), ("w_branch", l), ("w_out", l)]
        if l == 0:
            dx, _ = ffn_back("ffn1", l, dx, s["x0"], s["g1"], s["u1"], rest, [],
                             between=lambda parts: ffn_grads("ffn1", 0, *parts, chain=True))
        else:
            dx, parts = ffn_back("ffn1", l, dx, s["x0"], s["g1"], s["u1"], rest, [])
            ffn_grads("ffn1", l, *parts)

    for k in ("ffn1_norm", "mix_norm", "ffn2_norm", "rel_bias", "conv_w"):
        gs[k] = jnp.stack(gs[k])
    return loss_p, dx, gs


class _Exchange:
    def __init__(self, shards):
        self.shards = shards
        self.w = {}
        self.g = {}
        self.landed = {}

    def own(self, key):
        return self.shards[key[0]][key[1]].astype(BF)

    def pieces(self, keys, scatter):
        if not keys:
            return None
        if scatter:
            return _Scatter([self.g[k] for k in keys])
        return _HalfGather([_halves(self.own(k)) for k in keys])

    def arrived(self, keys, outs, scatter):
        for k, o in zip(keys, outs):
            if scatter:
                self.landed[k] = o
            else:
                self.w[k] = o.reshape((N_SHARD,) + self.shards[k[0]].shape[1:])


def _halves(a):
    return a.reshape(2, -1, a.shape[-1])


TRANSPOSED_GRADS = ("ffn1_w_gate", "ffn1_w_up", "ffn2_w_gate", "ffn2_w_up")
W_NAMES = ("ffn1_norm", "ffn1_w_gate", "ffn1_w_up", "ffn1_w_down", "mix_norm", "w_in", "conv_w", "rel_bias", "w_branch",
           "w_merge_gate", "w_out", "ffn2_norm", "ffn2_w_gate", "ffn2_w_up", "ffn2_w_down", "final_norm")


def _as2d(a):
    return a.reshape(1, -1) if a.ndim == 1 else a.reshape(-1, a.shape[-1])


def kernel(x, ffn1_norm, ffn1_w_gate, ffn1_w_up, ffn1_w_down, mix_norm, w_in, conv_w, rel_bias, w_branch, w_merge_gate, w_out, ffn2_norm, ffn2_w_gate, ffn2_w_up, ffn2_w_down, final_norm, loss_target, m_ffn1_norm, m_ffn1_w_gate, m_ffn1_w_up, m_ffn1_w_down, m_mix_norm, m_w_in, m_conv_w, m_rel_bias, m_w_branch, m_w_merge_gate, m_w_out, m_ffn2_norm, m_ffn2_w_gate, m_ffn2_w_up, m_ffn2_w_down, m_final_norm, v_ffn1_norm, v_ffn1_w_gate, v_ffn1_w_up, v_ffn1_w_down, v_mix_norm, v_w_in, v_conv_w, v_rel_bias, v_w_branch, v_w_merge_gate, v_w_out, v_ffn2_norm, v_ffn2_w_gate, v_ffn2_w_up, v_ffn2_w_down, v_final_norm):
    given = dict(locals())
    w = {n: given[n] for n in W_NAMES}
    m = {n: given["m_" + n] for n in W_NAMES}
    v = {n: given["v_" + n] for n in W_NAMES}
    my_chip = 2 * lax.axis_index("x") + lax.axis_index("y")
    L = w_in.shape[0]

    wx = _Exchange({n: jnp.swapaxes(w[n], 1, 2) if n in TRANSPOSED_GRADS else w[n] for n in BIG_NAMES})
    first = _keys(FFN1, 0)
    biases, got = _relbias_expand(
        rel_bias, "relbias_expand", comm=_HalfGather([_halves(wx.own(k)) for k in first] + [_halves(conv_w)]))
    wx.arrived(first, got[:-1], False)
    convw_full = jnp.transpose(got[-1].reshape((N_SHARD,) + conv_w.shape), (1, 2, 0, 3)).reshape(
        conv_w.shape[0], conv_w.shape[1], -1)

    small = {n: w[n] for n in ("ffn1_norm", "mix_norm", "ffn2_norm", "final_norm")}
    loss_p, grad_x, gs = _local_step(x[0], loss_target[0], small, convw_full, biases, wx, L)

    parts = [gs["ffn1_norm"].reshape(-1), gs["mix_norm"].reshape(-1), gs["ffn2_norm"].reshape(-1),
             gs["final_norm"].reshape(-1), gs["rel_bias"].reshape(-1), gs["conv_w"].reshape(-1), loss_p[0]]
    sizes = [p.shape[0] for p in parts]
    flat = jnp.concatenate(parts)
    rows = -(-flat.shape[0] // (8 * LANE)) * 8
    flat = jnp.pad(flat, (0, rows * LANE - flat.shape[0])).reshape(rows, LANE)
    red = _allreduce_small(flat, "allreduce_small").reshape(-1)
    offs = [0]
    for sz in sizes:
        offs.append(offs[-1] + sz)
    sm = {}
    for i, n in enumerate(("ffn1_norm", "mix_norm", "ffn2_norm", "final_norm", "rel_bias", "conv_w")):
        sm[n] = red[offs[i]:offs[i + 1]]
    loss = red[offs[6]]
    sm["conv_w"] = lax.dynamic_slice_in_dim(sm["conv_w"].reshape(conv_w.shape[0], conv_w.shape[1], -1),
                                            my_chip * conv_w.shape[2], conv_w.shape[2], axis=2)

    grads, deltas, new_m, new_v = {}, {}, {}, {}
    for n in W_NAMES:
        flip = n in TRANSPOSED_GRADS

        def view(a):
            return jnp.swapaxes(a, 1, 2) if flip else a

        shape = view(w[n]).shape
        wmv = [_as2d(view(a[n])) for a in (w, m, v)]
        if n in BIG_NAMES:
            lands = [wx.landed[(n, l)] for l in range(L)]
            out = _sum_adamw([a.reshape(a.shape[0], -1, a.shape[-1]) for a in lands], *wmv, f"adamw_{n}")
        else:
            g = _as2d(sm[n].reshape(shape))
            out = [g] + list(_adamw(wmv[0], g, wmv[1], wmv[2], f"adamw_{n}"))
        grads[n], deltas[n], new_m[n], new_v[n] = (view(o.reshape(shape)) for o in out)

    return (loss, grad_x[None], *[grads[n] for n in W_NAMES], *[deltas[n] for n in W_NAMES],
            *[new_m[n] for n in W_NAMES], *[new_v[n] for n in W_NAMES])
```

```python
import functools
import math

import jax
import jax.numpy as jnp
from jax import lax
from jax.experimental import pallas as pl
from jax.experimental.pallas import tpu as pltpu

F32 = jnp.float32
BF = jnp.bfloat16
MESH = pl.DeviceIdType.MESH
ARB = "arbitrary"
PAR = "parallel"

EPS = 1e-6
NEG_INF = -1e30
ROPE_BASE = 10000.0
CHUNK = 64
BRANCH_W = 512
H_RET = 4
DK_RET = 128
H_ATT = 8
DH_ATT = 64
N_PREV = 8
REL_CLIP = 128
N_REL = 2 * REL_CLIP + 1
N_SHARD = 4
LANE = 128
RET_L = 512
ATT_TQ = 128
ATT_SUB = 4
ATT_PAD = N_PREV * CHUNK
ATT_SPAN = ATT_TQ + ATT_PAD
ATT_TOEP = 2 * REL_CLIP
RB_PAD = 264
TM = 512
TM_FFN = 1024

ADAM_LR = 0.001
ADAM_B1 = 0.9
ADAM_B2 = 0.999
ADAM_EPS = 1e-08
ADAM_WD = 0.01
ADAM_STEP = 10

NT_DIMS = (((1,), (1,)), ((), ()))
TN_DIMS = (((0,), (0,)), ((), ()))


def _cp(sem, vmem_mb=48):
    return pltpu.CompilerParams(dimension_semantics=sem, vmem_limit_bytes=vmem_mb << 20)


def _sds(shape, dtype):
    return jax.ShapeDtypeStruct(tuple(shape), dtype)


def _rms_r(x):
    return lax.rsqrt(jnp.mean(x * x, axis=-1, keepdims=True) + EPS)


def _sigmoid(x):
    return 0.5 * jnp.tanh(0.5 * x) + 0.5


def _rms_bwd(dh, xv, nw):
    r = _rms_r(xv)
    xh = xv * r
    dxh = dh * nw
    dx = r * (dxh - xh * jnp.mean(dxh * xh, axis=-1, keepdims=True))
    return dx, jnp.sum(dh * xh, axis=0, keepdims=True)


def _place():
    return lax.axis_index("x"), lax.axis_index("y"), lax.axis_index("c")


def _other_chips(x, y):
    return [(1 - x, y), (x, 1 - y), (1 - x, 1 - y)]


class _Scatter:
    def __init__(self, srcs):
        self.srcs = list(srcs)
        n = len(self.srcs)
        self.out_shape = [_sds((2 * N_SHARD,) + s.shape[1:], s.dtype) for s in self.srcs]
        self.scratch = [pltpu.SemaphoreType.DMA((n,)), pltpu.SemaphoreType.DMA((3, n)), pltpu.SemaphoreType.DMA((3, n)),
                        pltpu.SemaphoreType.DMA((4, n)), pltpu.SemaphoreType.DMA((4, n))]

    def _plan(self, src, dst, sems, want):
        lsem, s1, r1, s2, r2 = sems
        x, y, c = _place()
        mine = 2 * x + y
        n = len(src)
        chips = list(enumerate(_other_chips(x, y)))

        def copy(s_ref, d_ref, ssem, rsem, to):
            return pltpu.make_async_remote_copy(src_ref=s_ref, dst_ref=d_ref, send_sem=ssem, recv_sem=rsem,
                                                device_id=to, device_id_type=MESH)

        local = [pltpu.make_async_copy(src[k].at[mine], dst[k].at[3], lsem.at[k]) for k in range(n)
                 ] if "local" in want else []
        sends = [copy(src[k].at[2 * ch[0] + ch[1]], dst[k].at[j], s1.at[j, k], r1.at[j, k], (ch[0], ch[1], c))
                 for j, ch in chips for k in range(n)] if "sends" in want else []
        passes = [copy(dst[k].at[j], dst[k].at[4 + j], s2.at[j, k], r2.at[j, k], (x, y, 1 - c))
                  for j, ch in chips for k in range(n)] if "passes" in want else []
        own_pass = [copy(src[k].at[mine], dst[k].at[7], s2.at[3, k], r2.at[3, k], (x, y, 1 - c))
                    for k in range(n)] if "own_pass" in want else []
        return local, sends, passes, own_pass

    def start(self, src, dst, sems):
        local, sends, _, own_pass = self._plan(src, dst, sems, ("local", "sends", "own_pass"))
        for cp in local + sends + own_pass:
            cp.start()

    def relay(self, src, dst, sems):
        _, sends, passes, _ = self._plan(src, dst, sems, ("sends", "passes"))
        for land, fwd in zip(sends, passes):
            land.wait_recv()
            fwd.start()

    def finish(self, src, dst, sems):
        local, sends, passes, own_pass = self._plan(src, dst, sems, ("local", "sends", "passes", "own_pass"))
        for cp in passes + own_pass:
            cp.wait_recv()
        for cp in sends + passes + own_pass:
            cp.wait_send()
        for cp in local:
            cp.wait()

    def wait(self, src, dst, sems):
        self.relay(src, dst, sems)
        self.finish(src, dst, sems)


class _HalfGather:
    def __init__(self, srcs):
        self.srcs = list(srcs)
        n = len(self.srcs)
        self.out_shape = [_sds((N_SHARD,) + s.shape, s.dtype) for s in self.srcs]
        self.scratch = [pltpu.SemaphoreType.DMA((n,))] + [pltpu.SemaphoreType.DMA((3, n)) for _ in range(4)]

    def _plan(self, src, dst, sems, want):
        lsem, s1, r1, s2, r2 = sems
        x, y, c = _place()
        mine = 2 * x + y
        n = len(src)
        chips = [(j, ch, 2 * ch[0] + ch[1]) for j, ch in enumerate(_other_chips(x, y))]

        def copy(s_ref, d_ref, ssem, rsem, to):
            return pltpu.make_async_remote_copy(src_ref=s_ref, dst_ref=d_ref, send_sem=ssem, recv_sem=rsem,
                                                device_id=to, device_id_type=MESH)

        def over(kind, make):
            return [make(j, ch, slot, k) for j, ch, slot in chips for k in range(n)] if kind in want else []

        local = [pltpu.make_async_copy(src[k], dst[k].at[mine], lsem.at[k]) for k in range(n)] if "local" in want else []
        sends = over("sends", lambda j, ch, slot, k: copy(src[k].at[c], dst[k].at[mine, c], s1.at[j, k], r1.at[j, k],
                                                          (ch[0], ch[1], c)))
        lands = over("lands", lambda j, ch, slot, k: copy(src[k].at[c], dst[k].at[slot, c], s1.at[j, k], r1.at[j, k],
                                                          (ch[0], ch[1], c)))
        passes = over("passes", lambda j, ch, slot, k: copy(dst[k].at[slot, c], dst[k].at[slot, c], s2.at[j, k],
                                                            r2.at[j, k], (x, y, 1 - c)))
        gets = over("gets", lambda j, ch, slot, k: copy(dst[k].at[slot, 1 - c], dst[k].at[slot, 1 - c], s2.at[j, k],
                                                        r2.at[j, k], (x, y, 1 - c)))
        return local, sends, lands, passes, gets

    def start(self, src, dst, sems):
        lsem, s1, r1, s2, r2 = sems
        x, y, c = _place()
        mine = 2 * x + y
        for k in range(len(src)):
            pltpu.make_async_copy(src[k], dst[k].at[mine], lsem.at[k]).start()
        for j, ch in enumerate(_other_chips(x, y)):
            for k in range(len(src)):
                pltpu.make_async_remote_copy(
                    src_ref=src[k].at[c], dst_ref=dst[k].at[mine, c], send_sem=s1.at[j, k], recv_sem=r1.at[j, k],
                    device_id=(ch[0], ch[1], c), device_id_type=MESH).start()

    def relay(self, src, dst, sems):
        _, _, lands, passes, _ = self._plan(src, dst, sems, ("lands", "passes"))
        for land, fwd in zip(lands, passes):
            land.wait_recv()
            fwd.start()

    def finish(self, src, dst, sems):
        local, sends, _, passes, gets = self._plan(src, dst, sems, ("local", "sends", "passes", "gets"))
        for cp in gets:
            cp.wait_recv()
        for cp in sends + passes:
            cp.wait_send()
        for cp in local:
            cp.wait()

    def wait(self, src, dst, sems):
        self.relay(src, dst, sems)
        self.finish(src, dst, sems)


def _call(body, *, name, args, in_specs, out_specs, out_shape, grid=(), scratch_shapes=(), sem=None, comm=None,
          aliases=None, vmem_mb=48):
    in_specs, out_specs, out_shape = list(in_specs), list(out_specs), list(out_shape)
    scratch, args = list(scratch_shapes), list(args)
    n_in, n_out, n_scr = len(in_specs), len(out_specs), len(scratch)
    if comm is None:
        def kernel_body(*refs):
            body(*refs)
    else:
        c_in, c_out = len(comm.srcs), len(comm.out_shape)

        def kernel_body(*refs):
            o0 = n_in + c_in
            s0 = o0 + n_out + c_out
            cin, cout, sems = refs[n_in:o0], refs[o0 + n_out:s0], refs[s0 + n_scr:]
            main = refs[:n_in] + refs[o0:o0 + n_out] + refs[s0:s0 + n_scr]
            if grid:
                ids = [pl.program_id(a) for a in range(len(grid))]
                first = functools.reduce(lambda p, q: p & q, [i == 0 for i in ids])
                last = functools.reduce(lambda p, q: p & q, [i == g - 1 for i, g in zip(ids, grid)])

                @pl.when(first)
                def _():
                    comm.start(cin, cout, sems)

                body(*main)

                steps = math.prod(grid)
                if hasattr(comm, "relay") and steps >= 4:
                    flat = functools.reduce(lambda p, q: p + q, [i * math.prod(grid[a + 1:]) for a, i in enumerate(ids)])

                    @pl.when(flat == (5 * steps) // 6)
                    def _():
                        comm.relay(cin, cout, sems)

                    @pl.when(last)
                    def _():
                        comm.finish(cin, cout, sems)
                else:
                    @pl.when(last)
                    def _():
                        comm.wait(cin, cout, sems)
            else:
                comm.start(cin, cout, sems)
                body(*main)
                comm.wait(cin, cout, sems)

        hbm = pl.BlockSpec(memory_space=pl.ANY)
        in_specs += [hbm] * c_in
        out_specs += [hbm] * c_out
        out_shape += comm.out_shape
        scratch += comm.scratch
        args += comm.srcs
    params = dict(vmem_limit_bytes=vmem_mb << 20)
    if grid:
        params["dimension_semantics"] = sem
    outs = pl.pallas_call(
        kernel_body, name=name, grid=grid, in_specs=in_specs, out_specs=out_specs, out_shape=out_shape,
        scratch_shapes=scratch, input_output_aliases=aliases or {}, compiler_params=pltpu.CompilerParams(**params),
    )(*args)
    return list(outs[:n_out]), list(outs[n_out:])


def _ffn_fwd(x, nw, wg, wu, wd, name, comm=None):
    T, D = x.shape
    ns, fs, _ = wg.shape
    tm = min(TM_FFN, T)

    def body(x_ref, nw_ref, wg_ref, wu_ref, wd_ref, xo_ref, g_ref, u_ref, h_s, acc_s):
        j = pl.program_id(1)

        @pl.when(j == 0)
        def _():
            xv = x_ref[...]
            h_s[...] = (xv * _rms_r(xv) * nw_ref[...]).astype(BF)
            acc_s[...] = jnp.zeros_like(acc_s)

        h = h_s[...]
        gb = lax.dot_general(h, wg_ref[...], NT_DIMS, preferred_element_type=F32).astype(BF)
        ub = lax.dot_general(h, wu_ref[...], NT_DIMS, preferred_element_type=F32).astype(BF)
        g_ref[...] = gb
        u_ref[...] = ub
        g = gb.astype(F32)
        a = (g * _sigmoid(g) * ub.astype(F32)).astype(BF)
        acc_s[...] += jnp.dot(a, wd_ref[...], preferred_element_type=F32)

        @pl.when(j == ns - 1)
        def _():
            xo_ref[...] = x_ref[...] + 0.5 * acc_s[...]

    wspec = pl.BlockSpec((None, fs, D), lambda i, j: (j, 0, 0))
    return _call(
        body, name=name, grid=(T // tm, ns), args=(x, nw, wg, wu, wd), comm=comm, vmem_mb=56,
        in_specs=[pl.BlockSpec((tm, D), lambda i, j: (i, 0)),
                  pl.BlockSpec((1, D), lambda i, j: (0, 0)),
                  wspec, wspec,
                  pl.BlockSpec((None, fs, D), lambda i, j: (j, 0, 0))],
        out_specs=[pl.BlockSpec((tm, D), lambda i, j: (i, 0)),
                   pl.BlockSpec((None, tm, fs), lambda i, j: (j, i, 0)),
                   pl.BlockSpec((None, tm, fs), lambda i, j: (j, i, 0))],
        out_shape=[_sds((T, D), F32), _sds((ns, T, fs), BF), _sds((ns, T, fs), BF)],
        scratch_shapes=[pltpu.VMEM((tm, D), BF), pltpu.VMEM((tm, D), F32)],
        sem=(ARB, ARB))


def _ffn_bwd_hidden(dxo, x, nw, g, u, wd, name, comm=None):
    T, D = x.shape
    ns, fs, _ = wd.shape
    tm = min(TM_FFN, T)

    def body(dxo_ref, x_ref, nw_ref, g_ref, u_ref, wd_ref, dg_ref, du_ref, a_ref, h_ref, dacc_ref, dacc_s):
        @pl.when(pl.program_id(1) == 0)
        def _():
            xv = x_ref[...]
            h_ref[...] = (xv * _rms_r(xv) * nw_ref[...]).astype(BF)
            db = (0.5 * dxo_ref[...]).astype(BF)
            dacc_ref[...] = db
            dacc_s[...] = db

        da = lax.dot_general(dacc_s[...], wd_ref[...], NT_DIMS, preferred_element_type=F32).astype(BF)
        gb = g_ref[...]
        ub = u_ref[...]
        s = _sigmoid(gb.astype(F32)).astype(BF)
        sg = gb * s
        a_ref[...] = sg * ub
        du_ref[...] = da * sg
        dg_ref[...] = (da * ub) * (s + sg * (1.0 - s))

    tok = pl.BlockSpec((tm, D), lambda i, j: (i, 0))
    hid = pl.BlockSpec((None, tm, fs), lambda i, j: (j, i, 0))
    return _call(
        body, name=name, grid=(T // tm, ns), args=(dxo, x, nw, g, u, wd), comm=comm, vmem_mb=56,
        in_specs=[tok, tok, pl.BlockSpec((1, D), lambda i, j: (0, 0)), hid, hid,
                  pl.BlockSpec((None, fs, D), lambda i, j: (j, 0, 0))],
        out_specs=[hid, hid, hid, tok, tok],
        out_shape=[_sds((ns, T, fs), BF)] * 3 + [_sds((T, D), BF)] * 2,
        scratch_shapes=[pltpu.VMEM((tm, D), BF)],
        sem=(ARB, ARB))


def _ffn_bwd_resid(dg, du, wg, wu, x, nw, dxo, name, comm=None):
    T, D = x.shape
    ns, fs, _ = wg.shape
    tm = min(TM_FFN, T)

    def body(dg_ref, du_ref, wg_ref, wu_ref, x_ref, nw_ref, dxo_ref, dx_ref, dnw_ref, acc_s):
        i = pl.program_id(0)
        j = pl.program_id(1)
        prod = (jnp.dot(dg_ref[...], wg_ref[...], preferred_element_type=F32)
                + jnp.dot(du_ref[...], wu_ref[...], preferred_element_type=F32))

        @pl.when((i == 0) & (j == 0))
        def _():
            dnw_ref[...] = jnp.zeros_like(dnw_ref)

        @pl.when(j == 0)
        def _():
            acc_s[...] = prod

        @pl.when(j > 0)
        def _():
            acc_s[...] += prod

        @pl.when(j == ns - 1)
        def _():
            dx, dn = _rms_bwd(acc_s[...], x_ref[...], nw_ref[...])
            dx_ref[...] = dxo_ref[...] + dx
            dnw_ref[...] += dn

    tok = pl.BlockSpec((tm, D), lambda i, j: (i, 0))
    row = pl.BlockSpec((1, D), lambda i, j: (0, 0))
    hid = pl.BlockSpec((None, tm, fs), lambda i, j: (j, i, 0))
    wspec = pl.BlockSpec((None, fs, D), lambda i, j: (j, 0, 0))
    return _call(
        body, name=name, grid=(T // tm, ns), args=(dg, du, wg, wu, x, nw, dxo), comm=comm, vmem_mb=56,
        in_specs=[hid, hid, wspec, wspec, tok, row, tok],
        out_specs=[tok, row],
        out_shape=[_sds((T, D), F32), _sds((1, D), F32)],
        scratch_shapes=[pltpu.VMEM((tm, D), F32)],
        sem=(ARB, ARB))


def _tn(a, b, a_spec, b_spec, out_shape, out_spec, grid, name, prev=None, comm=None):
    nk = grid[-1]
    acc_shape = tuple(d for d in out_spec.block_shape if d is not None)

    def body(*refs):
        a_ref, b_ref = refs[0], refs[1]
        o_ref, acc = refs[-2], refs[-1]
        k = pl.program_id(2)
        prod = lax.dot_general(a_ref[...], b_ref[...], TN_DIMS, preferred_element_type=F32)

        @pl.when(k == 0)
        def _():
            acc[...] = prod

        @pl.when(k > 0)
        def _():
            acc[...] += prod

        @pl.when(k == nk - 1)
        def _():
            o_ref[...] = acc[...].astype(o_ref.dtype)

    in_specs = [a_spec, b_spec]
    args = [a, b]
    aliases = {}
    if prev is not None:
        in_specs.append(pl.BlockSpec(memory_space=pl.ANY))
        args.append(prev)
        aliases = {2: 0}
    main, extra = _call(
        body, name=name, grid=grid, args=args, in_specs=in_specs, out_specs=[out_spec], out_shape=[out_shape],
        scratch_shapes=[pltpu.VMEM(acc_shape, F32)], aliases=aliases, sem=(ARB, ARB, ARB), comm=comm)
    return main[0] if comm is None else (main[0], extra)


def _tn_branches(ys, dp, ns, tk, name):
    T, w = ys[0].shape
    D = dp.shape[1] // 3
    dq = D // ns
    nk = T // tk

    def body(yc_ref, yr_ref, ya_ref, b_ref, o_ref, acc):
        i = pl.program_id(0)
        k = pl.program_id(1)

        @pl.when(k == 0)
        def _():
            acc[...] = jnp.zeros_like(acc)

        for branch, y_ref in enumerate((yc_ref, yr_ref, ya_ref)):
            @pl.when(i == branch)
            def _():
                acc[...] += lax.dot_general(y_ref[...], b_ref[...], TN_DIMS, preferred_element_type=F32)

        @pl.when(k == nk - 1)
        def _():
            for s in range(ns):
                o_ref[s] = acc[:, s * dq:(s + 1) * dq].astype(o_ref.dtype)

    yspec = pl.BlockSpec((tk, w), lambda i, k: (k, 0))
    return pl.pallas_call(
        body, name=name, grid=(3, nk),
        in_specs=[yspec, yspec, yspec, pl.BlockSpec((tk, D), lambda i, k: (k, i))],
        out_specs=pl.BlockSpec((ns, None, w, dq), lambda i, k: (0, i, 0, 0)),
        out_shape=_sds((ns, 3, w, dq), BF),
        scratch_shapes=[pltpu.VMEM((w, D), F32)],
        compiler_params=_cp((PAR, ARB)),
    )(*ys, dp)


def _tn_rows(h, dbig, ns, tk, n_mats, name):
    T, D = h.shape
    dq = D // ns
    nk = T // tk

    def body(a_ref, b_ref, o_ref, acc):
        k = pl.program_id(1)
        prod = lax.dot_general(a_ref[...], b_ref[...], TN_DIMS, preferred_element_type=F32)

        @pl.when(k == 0)
        def _():
            acc[...] = prod

        @pl.when(k > 0)
        def _():
            acc[...] += prod

        @pl.when(k == nk - 1)
        def _():
            for s in range(ns):
                o_ref[s] = acc[s * dq:(s + 1) * dq, :].astype(o_ref.dtype)

    return pl.pallas_call(
        body, name=name, grid=(n_mats, nk),
        in_specs=[pl.BlockSpec((tk, D), lambda q, k: (k, 0)), pl.BlockSpec((tk, D), lambda q, k: (k, q))],
        out_specs=pl.BlockSpec((ns, None, dq, D), lambda q, k: (0, q, 0, 0)),
        out_shape=_sds((ns, n_mats, dq, D), BF),
        scratch_shapes=[pltpu.VMEM((D, D), F32)],
        compiler_params=_cp((PAR, ARB)),
    )(h, dbig)


def _inproj_fwd(x, nw, wbig, name):
    T, D = x.shape
    nb = wbig.shape[-1]
    tm = min(2 * TM, T)
    bn = min(2048, nb)

    def body(x_ref, nw_ref, w_ref, o_ref, h_ref, h_s):
        @pl.when(pl.program_id(1) == 0)
        def _():
            xv = x_ref[...]
            hb = (xv * _rms_r(xv) * nw_ref[...]).astype(BF)
            h_s[...] = hb
            h_ref[...] = hb

        o_ref[...] = jnp.dot(h_s[...], w_ref[...], preferred_element_type=F32).astype(BF)

    return pl.pallas_call(
        body, name=name, grid=(T // tm, nb // bn),
        in_specs=[pl.BlockSpec((tm, D), lambda i, n: (i, 0)),
                  pl.BlockSpec((1, D), lambda i, n: (0, 0)),
                  pl.BlockSpec((D, bn), lambda i, n: (0, n))],
        out_specs=[pl.BlockSpec((tm, bn), lambda i, n: (i, n)),
                   pl.BlockSpec((tm, D), lambda i, n: (i, 0))],
        out_shape=[_sds((T, nb), BF), _sds((T, D), BF)],
        scratch_shapes=[pltpu.VMEM((tm, D), BF)],
        compiler_params=_cp((PAR, ARB)),
    )(x, nw, wbig)


def _inproj_bwd(dbig, wbig, x, nw, dxin, name, comm=None):
    T, D = x.shape
    nb = wbig.shape[-1]
    tm = min(TM_FFN, T)
    tk = min(2048, nb)
    nk = nb // tk

    def body(a_ref, w_ref, x_ref, nw_ref, dxin_ref, dx_ref, dnw_ref, acc_s):
        i = pl.program_id(0)
        k = pl.program_id(1)
        prod = lax.dot_general(a_ref[...], w_ref[...], NT_DIMS, preferred_element_type=F32)

        @pl.when((i == 0) & (k == 0))
        def _():
            dnw_ref[...] = jnp.zeros_like(dnw_ref)

        @pl.when(k == 0)
        def _():
            acc_s[...] = prod

        @pl.when(k > 0)
        def _():
            acc_s[...] += prod

        @pl.when(k == nk - 1)
        def _():
            dx, dn = _rms_bwd(acc_s[...], x_ref[...], nw_ref[...])
            dx_ref[...] = dxin_ref[...] + dx
            dnw_ref[...] += dn

    tok = pl.BlockSpec((tm, D), lambda i, k: (i, 0))
    row = pl.BlockSpec((1, D), lambda i, k: (0, 0))
    return _call(
        body, name=name, grid=(T // tm, nk), args=(dbig, wbig, x, nw, dxin), comm=comm, vmem_mb=56,
        in_specs=[pl.BlockSpec((tm, tk), lambda i, k: (i, k)),
                  pl.BlockSpec((D, tk), lambda i, k: (0, k)),
                  tok, row, tok],
        out_specs=[tok, row],
        out_shape=[_sds((T, D), F32), _sds((1, D), F32)],
        scratch_shapes=[pltpu.VMEM((tm, D), F32)],
        sem=(ARB, ARB))


CONV_R = 512
CONV_BASE, CONV_GROUP = 0, 3
ATT_BASE, ATT_GROUP = 12, 3
RET_BASE, RET_GROUP = 24, 4
N_SEG = 10


N_IN_BLOCKS = N_SEG * BRANCH_W // LANE


def _orig_block(p):
    nblk = BRANCH_W // LANE
    qa, qr = p - ATT_BASE, p - RET_BASE
    conv = (p % CONV_GROUP) * nblk + p // CONV_GROUP
    att = (7 + qa % ATT_GROUP) * nblk + qa // ATT_GROUP
    ret = (3 + qr % RET_GROUP) * nblk + qr // RET_GROUP
    return jnp.where(p < ATT_BASE, conv, jnp.where(p < RET_BASE, att, ret))


def _copy_blocks(src, in_spec, out_shape, out_spec, grid, name, prev=None):
    def body(*refs):
        refs[-1][...] = refs[0][...]

    in_specs, args, aliases = [in_spec], [src], {}
    if prev is not None:
        in_specs.append(pl.BlockSpec(memory_space=pl.ANY))
        args.append(prev)
        aliases = {1: 0}
    return pl.pallas_call(
        body, name=name, grid=grid, in_specs=in_specs, out_specs=out_spec, out_shape=out_shape,
        input_output_aliases=aliases, compiler_params=_cp(tuple(PAR for _ in grid)),
    )(*args)


def _build_wbig(gates4, win4, name):
    ns, _, dq, D = gates4.shape
    per = win4.shape[-1] // LANE
    shape = _sds((D, 3 * D + N_IN_BLOCKS * LANE), gates4.dtype)
    out = _copy_blocks(gates4, pl.BlockSpec((None, None, dq, D), lambda s, i: (s, i, 0, 0)), shape,
                       pl.BlockSpec((dq, D), lambda s, i: (s, i)), (ns, 3), name + "_gates")
    return _copy_blocks(
        win4, pl.BlockSpec((None, D, LANE), lambda p: (_orig_block(p) // per, 0, _orig_block(p) % per)), shape,
        pl.BlockSpec((D, LANE), lambda p: (0, 3 * D // LANE + p)), (N_IN_BLOCKS,), name + "_in", prev=out)


def _ungroup_dw_in(dwp, ns, name):
    D = dwp.shape[0]
    per = N_IN_BLOCKS // ns
    return _copy_blocks(
        dwp, pl.BlockSpec((D, LANE), lambda p: (0, p)), _sds((ns, D, per * LANE), dwp.dtype),
        pl.BlockSpec((None, D, LANE), lambda p: (_orig_block(p) // per, 0, _orig_block(p) % per)), (N_IN_BLOCKS,), name)


def _seg0(big):
    return (big.shape[1] - N_SEG * BRANCH_W) // LANE


def _group_spec(big, base, group, rows, where):
    first = (_seg0(big) + base) // group
    assert first * group == _seg0(big) + base

    def index(*ids):
        r, g = where(*ids)
        return r, first + g

    return pl.BlockSpec((rows, group * LANE), index)


CU, CB, CC = (slice(k * LANE, (k + 1) * LANE) for k in range(3))
AQ, AK, AV = CU, CB, CC
RQ, RK, RV, RG = (slice(k * LANE, (k + 1) * LANE) for k in range(4))


def _conv_fwd(big, cw, name):
    T = big.shape[0]
    R = min(CONV_R, T)

    def body(g_ref, w_ref, y_ref, z_s):
        z_s[pl.ds(0, 8), :] = jnp.zeros((8, LANE), F32)

        def fill(t, c):
            sl = pl.ds(pl.multiple_of(t * R, R), R)
            z_s[pl.ds(pl.multiple_of(t * R + 8, 8), R), :] = g_ref[sl, CC].astype(F32) * g_ref[sl, CU].astype(F32)
            return c

        lax.fori_loop(0, T // R, fill, 0)
        w0, w1, w2 = w_ref[0:1, :], w_ref[1:2, :], w_ref[2:3, :]

        def step(t, c):
            zz = z_s[pl.ds(pl.multiple_of(t * R, R), R + 8), :]
            z0 = zz[8:]
            z1 = pltpu.roll(zz, 1, 0)[8:]
            z2 = pltpu.roll(zz, 2, 0)[8:]
            sl = pl.ds(pl.multiple_of(t * R, R), R)
            y_ref[sl, :] = (g_ref[sl, CB].astype(F32) * (w2 * z0 + w1 * z1 + w0 * z2)).astype(BF)
            return c

        lax.fori_loop(0, T // R, step, 0)

    return pl.pallas_call(
        body, name=name, grid=(BRANCH_W // LANE,),
        in_specs=[_group_spec(big, CONV_BASE, CONV_GROUP, T, lambda j: (0, j)),
                  pl.BlockSpec((3, LANE), lambda j: (0, j))],
        out_specs=pl.BlockSpec((T, LANE), lambda j: (0, j)),
        out_shape=_sds((T, BRANCH_W), BF),
        scratch_shapes=[pltpu.VMEM((T + 8, LANE), F32)],
        compiler_params=_cp((PAR,)),
    )(big, cw)


def _conv_bwd(big, dy, cw, dbig, name):
    T = big.shape[0]
    R = min(CONV_R, T)

    def body(g_ref, dy_ref, w_ref, _, o_ref, dw_ref, z_s, d_s):
        z_s[pl.ds(0, 8), :] = jnp.zeros((8, LANE), F32)
        d_s[pl.ds(T, 8), :] = jnp.zeros((8, LANE), F32)

        def fill(t, c):
            sl = pl.ds(pl.multiple_of(t * R, R), R)
            z_s[pl.ds(pl.multiple_of(t * R + 8, 8), R), :] = g_ref[sl, CC].astype(F32) * g_ref[sl, CU].astype(F32)
            d_s[sl, :] = dy_ref[sl, :].astype(F32) * g_ref[sl, CB].astype(F32)
            return c

        lax.fori_loop(0, T // R, fill, 0)
        w0, w1, w2 = w_ref[0:1, :], w_ref[1:2, :], w_ref[2:3, :]

        def step(t, carry):
            a0, a1, a2 = carry
            zz = z_s[pl.ds(pl.multiple_of(t * R, R), R + 8), :]
            z0 = zz[8:]
            z1 = pltpu.roll(zz, 1, 0)[8:]
            z2 = pltpu.roll(zz, 2, 0)[8:]
            sl = pl.ds(pl.multiple_of(t * R, R), R)
            dyv = dy_ref[sl, :].astype(F32)
            o_ref[sl, CB] = (dyv * (w2 * z0 + w1 * z1 + w0 * z2)).astype(BF)
            dd = d_s[pl.ds(pl.multiple_of(t * R, R), R + 8), :]
            d0 = dd[:R]
            d1 = pltpu.roll(dd, R + 7, 0)[:R]
            d2 = pltpu.roll(dd, R + 6, 0)[:R]
            dz = w2 * d0 + w1 * d1 + w0 * d2
            o_ref[sl, CC] = (dz * g_ref[sl, CU].astype(F32)).astype(BF)
            o_ref[sl, CU] = (dz * g_ref[sl, CC].astype(F32)).astype(BF)
            a0 = a0 + jnp.sum(d0 * z2, axis=0, keepdims=True)
            a1 = a1 + jnp.sum(d0 * z1, axis=0, keepdims=True)
            a2 = a2 + jnp.sum(d0 * z0, axis=0, keepdims=True)
            return a0, a1, a2

        zero = jnp.zeros((1, LANE), F32)
        a0, a1, a2 = lax.fori_loop(0, T // R, step, (zero, zero, zero))
        dw_ref[0:1, :] = a0
        dw_ref[1:2, :] = a1
        dw_ref[2:3, :] = a2

    group = _group_spec(big, CONV_BASE, CONV_GROUP, T, lambda j: (0, j))
    w = pl.BlockSpec((3, LANE), lambda j: (0, j))
    return pl.pallas_call(
        body, name=name, grid=(BRANCH_W // LANE,),
        in_specs=[group, pl.BlockSpec((T, LANE), lambda j: (0, j)), w, pl.BlockSpec(memory_space=pl.ANY)],
        out_specs=[group, w],
        out_shape=[_sds(dbig.shape, BF), _sds((3, BRANCH_W), F32)],
        scratch_shapes=[pltpu.VMEM((T + 8, LANE), F32), pltpu.VMEM((T + 8, LANE), F32)],
        input_output_aliases={3: 0}, compiler_params=_cp((PAR,)),
    )(big, dy, cw, dbig)


def _ret_tables(T):
    L = min(RET_L, T)
    hh = jnp.arange(H_RET, dtype=F32)
    lg = jnp.log1p(-jnp.exp2(-5.0 - hh))
    n = jnp.arange(L, dtype=F32)
    a = jnp.exp(lg[:, None] * (n + 1.0))
    b = jnp.exp(lg[:, None] * (L - 1.0 - n))
    gl = jnp.exp(lg * L)
    ch = jnp.arange(L) // CHUNK
    m = jnp.exp(lg[:, None, None] * jnp.abs(n[:, None] - n[None, :])) * (ch[None, :] <= ch[:, None]).astype(F32)
    inv_freq = ROPE_BASE ** (-jnp.linspace(0.0, 1.0, DK_RET // 2, dtype=F32))
    ang = jnp.arange(T, dtype=F32)[:, None] * inv_freq[None, :]
    cos, sin = jnp.cos(ang), jnp.sin(ang)
    return dict(
        L=L, M=m,
        a=jnp.broadcast_to(a[:, :, None], (H_RET, L, DK_RET)),
        b=jnp.broadcast_to(b[:, :, None], (H_RET, L, DK_RET)),
        gl=jnp.broadcast_to(gl[:, None, None], (H_RET, 1, DK_RET)),
        cos=jnp.concatenate([cos, cos], axis=-1), sin=jnp.concatenate([-sin, sin], axis=-1))


def _rot(x, cs, sn):
    return x * cs + pltpu.roll(x, DK_RET // 2, 1) * sn


def _unrot(dy, cs, sn):
    return dy * cs + pltpu.roll(dy * sn, DK_RET // 2, 1)


def _ret_fwd(big, tb, name, comm=None):
    T = big.shape[0]
    L = tb["L"]
    nsc = T // L
    scale = DK_RET ** -0.5

    def body(x_ref, cos_ref, sin_ref, m_ref, a_ref, b_ref, gl_ref, y_ref, o_ref, st_ref, s_s):
        @pl.when(pl.program_id(1) == 0)
        def _():
            s_s[...] = jnp.zeros_like(s_s)

        cs, sn = cos_ref[...], sin_ref[...]
        qt = _rot(x_ref[:, RQ].astype(F32), cs, sn) * scale
        kt = _rot(x_ref[:, RK].astype(F32), cs, sn)
        qb, kb, vb = qt.astype(BF), kt.astype(BF), x_ref[:, RV]
        s_prev = s_s[...]
        st_ref[...] = s_prev
        p = lax.dot_general(qb, kb, NT_DIMS, preferred_element_type=F32) * m_ref[...]
        o = (jnp.dot(p.astype(BF), vb, preferred_element_type=F32)
             + jnp.dot((qt * a_ref[...]).astype(BF), s_prev.astype(BF), preferred_element_type=F32))
        s_s[...] = s_prev * gl_ref[...] + lax.dot_general((kt * b_ref[...]).astype(BF), vb, TN_DIMS,
                                                         preferred_element_type=F32)
        o_ref[...] = o
        gv = x_ref[:, RG].astype(F32)
        y_ref[...] = (gv * _sigmoid(gv) * o * _rms_r(o)).astype(BF)

    tab = pl.BlockSpec((L, DK_RET), lambda h, i: (i, 0))
    per_head = pl.BlockSpec((None, L, DK_RET), lambda h, i: (h, 0, 0))
    out = pl.BlockSpec((L, LANE), lambda h, i: (i, h))
    return _call(
        body, name=name, grid=(H_RET, nsc), comm=comm,
        args=(big, tb["cos"], tb["sin"], tb["M"], tb["a"], tb["b"], tb["gl"]),
        in_specs=[_group_spec(big, RET_BASE, RET_GROUP, L, lambda h, i: (i, h)), tab, tab,
                  pl.BlockSpec((None, L, L), lambda h, i: (h, 0, 0)), per_head, per_head,
                  pl.BlockSpec((None, 1, DK_RET), lambda h, i: (h, 0, 0))],
        out_specs=[out, out, pl.BlockSpec((None, None, DK_RET, DK_RET), lambda h, i: (i, h, 0, 0))],
        out_shape=[_sds((T, BRANCH_W), BF), _sds((T, BRANCH_W), F32), _sds((nsc, H_RET, DK_RET, DK_RET), F32)],
        scratch_shapes=[pltpu.VMEM((DK_RET, DK_RET), F32)],
        sem=(ARB, ARB))


def _ret_bwd(big, o, st, dy, tb, dbig, name):
    T = big.shape[0]
    L = tb["L"]
    nsc = T // L
    scale = DK_RET ** -0.5

    def body(x_ref, cos_ref, sin_ref, m_ref, a_ref, b_ref, gl_ref, o_ref, st_ref, dy_ref, _, d_ref, ds_s):
        @pl.when(pl.program_id(1) == 0)
        def _():
            ds_s[...] = jnp.zeros_like(ds_s)

        cs, sn = cos_ref[...], sin_ref[...]
        mm, av, bv = m_ref[...], a_ref[...], b_ref[...]
        qt = _rot(x_ref[:, RQ].astype(F32), cs, sn) * scale
        kt = _rot(x_ref[:, RK].astype(F32), cs, sn)
        qb, kb, vb = qt.astype(BF), kt.astype(BF), x_ref[:, RV]
        pb = (lax.dot_general(qb, kb, NT_DIMS, preferred_element_type=F32) * mm).astype(BF)
        ov = o_ref[...]
        r = _rms_r(ov)
        oh = ov * r
        gv = x_ref[:, RG].astype(F32)
        sg = _sigmoid(gv)
        dyv = dy_ref[...].astype(F32)
        d_ref[:, RG] = (dyv * oh * (sg * (1.0 + gv * (1.0 - sg)))).astype(BF)
        doh = dyv * gv * sg
        dob = (r * (doh - oh * jnp.mean(doh * oh, axis=-1, keepdims=True))).astype(BF)
        dsb = ds_s[...].astype(BF)
        spb = st_ref[...].astype(BF)
        dpb = (lax.dot_general(dob, vb, NT_DIMS, preferred_element_type=F32) * mm).astype(BF)
        dqt = (jnp.dot(dpb, kb, preferred_element_type=F32)
               + lax.dot_general(dob, spb, NT_DIMS, preferred_element_type=F32) * av)
        dkt = (lax.dot_general(dpb, qb, TN_DIMS, preferred_element_type=F32)
               + lax.dot_general(vb, dsb, NT_DIMS, preferred_element_type=F32) * bv)
        dv = (lax.dot_general(pb, dob, TN_DIMS, preferred_element_type=F32)
              + jnp.dot((kt * bv).astype(BF), dsb, preferred_element_type=F32))
        ds_s[...] = ds_s[...] * gl_ref[...] + lax.dot_general((qt * av).astype(BF), dob, TN_DIMS,
                                                              preferred_element_type=F32)
        d_ref[:, RQ] = (_unrot(dqt, cs, sn) * scale).astype(BF)
        d_ref[:, RK] = _unrot(dkt, cs, sn).astype(BF)
        d_ref[:, RV] = dv.astype(BF)

    def rev(i):
        return nsc - 1 - i

    group = _group_spec(big, RET_BASE, RET_GROUP, L, lambda h, i: (rev(i), h))
    tab = pl.BlockSpec((L, DK_RET), lambda h, i: (rev(i), 0))
    per_head = pl.BlockSpec((None, L, DK_RET), lambda h, i: (h, 0, 0))
    out = pl.BlockSpec((L, LANE), lambda h, i: (rev(i), h))
    return pl.pallas_call(
        body, name=name, grid=(H_RET, nsc),
        in_specs=[group, tab, tab,
                  pl.BlockSpec((None, L, L), lambda h, i: (h, 0, 0)), per_head, per_head,
                  pl.BlockSpec((None, 1, DK_RET), lambda h, i: (h, 0, 0)),
                  out, pl.BlockSpec((None, None, DK_RET, DK_RET), lambda h, i: (rev(i), h, 0, 0)), out,
                  pl.BlockSpec(memory_space=pl.ANY)],
        out_specs=group,
        out_shape=_sds(dbig.shape, BF),
        scratch_shapes=[pltpu.VMEM((DK_RET, DK_RET), F32)],
        input_output_aliases={10: 0}, compiler_params=_cp((PAR, ARB)),
    )(big, tb["cos"], tb["sin"], tb["M"], tb["a"], tb["b"], tb["gl"], o, st, dy, dbig)


def _relbias_onehot(n):
    mm = lax.broadcasted_iota(jnp.int32, (RB_PAD, ATT_TOEP), 1)
    rr = lax.broadcasted_iota(jnp.int32, (RB_PAD, ATT_TOEP), 0)
    idx = jnp.clip(n + ATT_TOEP - mm, 0, 2 * REL_CLIP)
    return (rr == idx).astype(F32)


def _split3(x):
    hi = x.astype(BF).astype(F32)
    mid = (x - hi).astype(BF).astype(F32)
    lo = x - hi - mid
    return jnp.concatenate([hi, mid, lo], axis=0).astype(BF)


def _join3(y):
    k = y.shape[0] // 3
    return (y[:k] + y[k:2 * k]) + y[2 * k:]


def _relbias_expand(rel_bias, name, comm=None):
    far = ATT_SPAN - ATT_TOEP
    n_layers = rel_bias.shape[0]
    rbp = jnp.pad(rel_bias, ((0, 0), (0, 0), (0, RB_PAD - N_REL)))

    def body(rb_ref, o_ref):
        for l in range(n_layers):
            rb = rb_ref[l]
            const = jnp.broadcast_to(rb[:, 2 * REL_CLIP:2 * REL_CLIP + 1], (H_ATT, far))
            rb3 = _split3(rb)

            def row(n, c):
                toep = _join3(jnp.dot(rb3, _relbias_onehot(n).astype(BF), preferred_element_type=F32))
                m = lax.broadcasted_iota(jnp.int32, (1, ATT_SPAN), 1)
                d = n // CHUNK + N_PREV - m // CHUNK
                neg = jnp.where((d >= 0) & (d <= N_PREV), 0.0, NEG_INF).astype(F32)
                o_ref[l, n] = jnp.concatenate([const, toep], axis=1) + neg
                return c

            lax.fori_loop(0, ATT_TQ, row, 0)

    (out,), extra = _call(
        body, name=name, args=(rbp,), comm=comm,
        in_specs=[pl.BlockSpec(memory_space=pltpu.VMEM)],
        out_specs=[pl.BlockSpec(memory_space=pltpu.VMEM)],
        out_shape=[_sds((n_layers, ATT_TQ, H_ATT, ATT_SPAN), F32)])
    return jnp.transpose(out, (0, 2, 1, 3)), extra


def _relbias_grad(dbt, name):
    far = ATT_SPAN - ATT_TOEP

    def body(d_ref, o_ref):
        def row(n, carry):
            acc, cs = carry
            dn = d_ref[n]
            acc = acc + _join3(lax.dot_general(_split3(dn[:, far:]), _relbias_onehot(n).astype(BF), NT_DIMS,
                                               preferred_element_type=F32))
            cs = cs + jnp.sum(dn[:, :far], axis=1, keepdims=True)
            return acc, cs

        acc, cs = lax.fori_loop(0, ATT_TQ, row, (jnp.zeros((H_ATT, RB_PAD), F32), jnp.zeros((H_ATT, 1), F32)))
        rr = lax.broadcasted_iota(jnp.int32, (H_ATT, RB_PAD), 1)
        o_ref[...] = acc + jnp.where(rr == 2 * REL_CLIP, cs, 0.0)

    return pl.pallas_call(
        body, name=name,
        in_specs=[pl.BlockSpec(memory_space=pltpu.VMEM)],
        out_specs=pl.BlockSpec(memory_space=pltpu.VMEM),
        out_shape=_sds((H_ATT, RB_PAD), F32),
    )(dbt)


def _att_pad_fill(dst_s, src_ref, cols, T):
    dst_s[pl.ds(0, ATT_PAD), :] = jnp.zeros((ATT_PAD, LANE), dst_s.dtype)
    R = min(512, T)

    def cp(t, c):
        dst_s[pl.ds(pl.multiple_of(ATT_PAD + t * R, LANE), R), :] = src_ref[pl.ds(pl.multiple_of(t * R, R), R), cols]
        return c

    lax.fori_loop(0, T // R, cp, 0)


ATT_WIN = ATT_SUB * ATT_TQ + ATT_PAD


def _att_probs(s_full, sub, bias, t0):
    s = s_full[sub * ATT_TQ:(sub + 1) * ATT_TQ, sub * ATT_TQ:sub * ATT_TQ + ATT_SPAN] * (DH_ATT ** -0.5) + bias
    key_pos = t0 + sub * ATT_TQ - ATT_PAD + lax.broadcasted_iota(jnp.int32, (1, ATT_SPAN), 1)
    s = jnp.where(key_pos >= 0, s, NEG_INF)
    p = jnp.exp(s - jnp.max(s, axis=-1, keepdims=True))
    return p * (1.0 / jnp.sum(p, axis=-1, keepdims=True))


def _att_band(tiles):
    rows = []
    for sub, t in enumerate(tiles):
        parts = []
        if sub:
            parts.append(jnp.zeros((ATT_TQ, sub * ATT_TQ), BF))
        parts.append(t)
        if sub < ATT_SUB - 1:
            parts.append(jnp.zeros((ATT_TQ, (ATT_SUB - 1 - sub) * ATT_TQ), BF))
        rows.append(jnp.concatenate(parts, axis=1))
    return jnp.concatenate(rows, axis=0)


def _att_head_masks(x):
    first = lax.broadcasted_iota(jnp.int32, (1, LANE), 1) < DH_ATT
    zero = jnp.zeros_like(x)
    return first, (jnp.where(first, x, zero), jnp.where(first, zero, x))


def _att_fwd(big, bias, name, comm=None):
    T = big.shape[0]
    rows = ATT_SUB * ATT_TQ
    nt = T // rows

    def body(x_ref, b_ref, y_ref, kp_s, vp_s):
        i = pl.program_id(1)

        @pl.when(i == 0)
        def _():
            _att_pad_fill(kp_s, x_ref, AK, T)
            _att_pad_fill(vp_s, x_ref, AV, T)

        t0 = pl.multiple_of(i * rows, rows)
        kw = kp_s[pl.ds(t0, ATT_WIN), :]
        vw = vp_s[pl.ds(t0, ATT_WIN), :]
        first, qm = _att_head_masks(x_ref[pl.ds(t0, rows), AQ])
        outs = []
        for hh in range(2):
            s_full = lax.dot_general(qm[hh], kw, NT_DIMS, preferred_element_type=F32)
            band = _att_band([_att_probs(s_full, sub, b_ref[hh], t0).astype(BF) for sub in range(ATT_SUB)])
            outs.append(jnp.dot(band, vw, preferred_element_type=F32))
        y_ref[...] = jnp.where(first, outs[0], outs[1]).astype(BF)

    return _call(
        body, name=name, grid=(H_ATT // 2, nt), args=(big, bias), comm=comm,
        in_specs=[_group_spec(big, ATT_BASE, ATT_GROUP, T, lambda p, i: (0, p)),
                  pl.BlockSpec((2, ATT_TQ, ATT_SPAN), lambda p, i: (p, 0, 0))],
        out_specs=[pl.BlockSpec((rows, LANE), lambda p, i: (i, p))],
        out_shape=[_sds((T, BRANCH_W), BF)],
        scratch_shapes=[pltpu.VMEM((T + ATT_PAD, LANE), BF), pltpu.VMEM((T + ATT_PAD, LANE), BF)],
        sem=(ARB, ARB))


def _att_bwd(big, bias, dy, dbig, name, comm=None):
    T = big.shape[0]
    rows = ATT_SUB * ATT_TQ
    nt = T // rows
    scale = DH_ATT ** -0.5

    def body(x_ref, b_ref, dy_ref, _, d_ref, db_ref, kp_s, vp_s, dk_s, dv_s):
        i = pl.program_id(1)

        @pl.when(i == 0)
        def _():
            _att_pad_fill(kp_s, x_ref, AK, T)
            _att_pad_fill(vp_s, x_ref, AV, T)
            dk_s[...] = jnp.zeros_like(dk_s)
            dv_s[...] = jnp.zeros_like(dv_s)
            db_ref[...] = jnp.zeros_like(db_ref)

        t0 = pl.multiple_of(i * rows, rows)
        win = pl.ds(t0, ATT_WIN)
        kw = kp_s[win, :]
        vw = vp_s[win, :]
        first, qm = _att_head_masks(x_ref[pl.ds(t0, rows), AQ])
        _, dom = _att_head_masks(dy_ref[...])
        dqs, dkt, dvt = [], None, None
        for hh in range(2):
            s_full = lax.dot_general(qm[hh], kw, NT_DIMS, preferred_element_type=F32)
            dp_full = lax.dot_general(dom[hh], vw, NT_DIMS, preferred_element_type=F32)
            ps, dss, db = [], [], None
            for sub in range(ATT_SUB):
                pn = _att_probs(s_full, sub, b_ref[hh], t0)
                dp = dp_full[sub * ATT_TQ:(sub + 1) * ATT_TQ, sub * ATT_TQ:sub * ATT_TQ + ATT_SPAN]
                ds = pn * (dp - jnp.sum(dp * pn, axis=-1, keepdims=True))
                db = ds if db is None else db + ds
                ps.append(pn.astype(BF))
                dss.append(ds.astype(BF))
            db_ref[hh] += db
            ds_band, p_band = _att_band(dss), _att_band(ps)
            dqs.append(jnp.dot(ds_band, kw, preferred_element_type=F32))
            qt = jnp.transpose(qm[hh].astype(F32)).astype(BF)
            dot_ = jnp.transpose(dom[hh].astype(F32)).astype(BF)
            dk_h = jnp.dot(qt, ds_band, preferred_element_type=F32)
            dv_h = jnp.dot(dot_, p_band, preferred_element_type=F32)
            dkt = dk_h if dkt is None else dkt + dk_h
            dvt = dv_h if dvt is None else dvt + dv_h
        d_ref[pl.ds(t0, rows), AQ] = (jnp.where(first, dqs[0], dqs[1]) * scale).astype(BF)
        dk_s[win, :] += jnp.transpose(dkt) * scale
        dv_s[win, :] += jnp.transpose(dvt)

        @pl.when(i == nt - 1)
        def _():
            R = min(512, T)

            def cp(t, c):
                src = pl.ds(pl.multiple_of(ATT_PAD + t * R, LANE), R)
                dst = pl.ds(pl.multiple_of(t * R, R), R)
                d_ref[dst, AK] = dk_s[src, :].astype(BF)
                d_ref[dst, AV] = dv_s[src, :].astype(BF)
                return c

            lax.fori_loop(0, T // R, cp, 0)

    group = _group_spec(big, ATT_BASE, ATT_GROUP, T, lambda p, i: (0, p))
    tile = pl.BlockSpec((rows, LANE), lambda p, i: (i, p))
    bspec = pl.BlockSpec((2, ATT_TQ, ATT_SPAN), lambda p, i: (p, 0, 0))
    return _call(
        body, name=name, grid=(H_ATT // 2, nt), args=(big, bias, dy, dbig), comm=comm, aliases={3: 0}, vmem_mb=56,
        in_specs=[group, bspec, tile, pl.BlockSpec(memory_space=pl.ANY)],
        out_specs=[group, bspec],
        out_shape=[_sds(dbig.shape, BF), _sds((H_ATT, ATT_TQ, ATT_SPAN), F32)],
        scratch_shapes=[pltpu.VMEM((T + ATT_PAD, LANE), BF), pltpu.VMEM((T + ATT_PAD, LANE), BF),
                        pltpu.VMEM((T + ATT_PAD, LANE), F32), pltpu.VMEM((T + ATT_PAD, LANE), F32)],
        sem=(ARB, ARB))


def _merge_fwd(x1, big, ys, wb, wo, name):
    T, D = x1.shape
    tm = min(TM, T)

    def body(x_ref, gp_ref, yc_ref, yr_ref, ya_ref, wb_ref, wo_ref, x2_ref, p_ref, mg_ref):
        merged = jnp.zeros((tm, D), F32)
        for i, y_ref in enumerate((yc_ref, yr_ref, ya_ref)):
            cols = slice(i * D, (i + 1) * D)
            pb = jnp.dot(y_ref[...], wb_ref[i], preferred_element_type=F32).astype(BF)
            p_ref[:, cols] = pb
            merged = merged + _sigmoid(gp_ref[:, cols].astype(F32)) * pb.astype(F32)
        mb = merged.astype(BF)
        mg_ref[...] = mb
        x2_ref[...] = x_ref[...] + jnp.dot(mb, wo_ref[...], preferred_element_type=F32)

    tok = pl.BlockSpec((tm, D), lambda i: (i, 0))
    wide = pl.BlockSpec((tm, 3 * D), lambda i: (i, 0))
    yspec = pl.BlockSpec((tm, BRANCH_W), lambda i: (i, 0))
    return pl.pallas_call(
        body, name=name, grid=(T // tm,),
        in_specs=[tok, wide, yspec, yspec, yspec,
                  pl.BlockSpec((3, BRANCH_W, D), lambda i: (0, 0, 0)),
                  pl.BlockSpec((D, D), lambda i: (0, 0))],
        out_specs=[tok, wide, tok],
        out_shape=[_sds((T, D), F32), _sds((T, 3 * D), BF), _sds((T, D), BF)],
        compiler_params=_cp((PAR,)),
    )(x1, big, *ys, wb, wo)


def _merge_bwd(dx2, big, p, wb, wo, name):
    T, D = dx2.shape
    tm = min(TM, T)

    def body(dx_ref, gp_ref, p_ref, wb_ref, wo_ref, dp_ref, dgp_ref, dyc_ref, dyr_ref, dya_ref, dxb_ref):
        dxb = dx_ref[...].astype(BF)
        dxb_ref[...] = dxb
        dm = lax.dot_general(dxb, wo_ref[...], NT_DIMS, preferred_element_type=F32)
        for i, dy_ref in enumerate((dyc_ref, dyr_ref, dya_ref)):
            cols = slice(i * D, (i + 1) * D)
            gt = _sigmoid(gp_ref[:, cols].astype(F32))
            dpb = (dm * gt).astype(BF)
            dp_ref[:, cols] = dpb
            dgp_ref[:, cols] = (dm * p_ref[:, cols].astype(F32) * gt * (1.0 - gt)).astype(BF)
            dy_ref[...] = lax.dot_general(dpb, wb_ref[i], NT_DIMS, preferred_element_type=F32).astype(BF)

    tok = pl.BlockSpec((tm, D), lambda i: (i, 0))
    wide = pl.BlockSpec((tm, 3 * D), lambda i: (i, 0))
    yspec = pl.BlockSpec((tm, BRANCH_W), lambda i: (i, 0))
    return pl.pallas_call(
        body, name=name, grid=(T // tm,),
        in_specs=[tok, wide, wide,
                  pl.BlockSpec((3, BRANCH_W, D), lambda i: (0, 0, 0)),
                  pl.BlockSpec((D, D), lambda i: (0, 0))],
        out_specs=[wide, wide, yspec, yspec, yspec, tok],
        out_shape=[_sds((T, 3 * D), BF), _sds(big.shape, BF)] + [_sds((T, BRANCH_W), BF)] * 3 + [_sds((T, D), BF)],
        compiler_params=_cp((PAR,)),
    )(dx2, big, p, wb, wo)


def _loss_head(x, tgt, fw, name):
    T, D = x.shape
    tm = min(TM, T)

    def body(x_ref, t_ref, w_ref, loss_ref, dx_ref, dw_ref):
        @pl.when(pl.program_id(0) == 0)
        def _():
            loss_ref[...] = jnp.zeros_like(loss_ref)
            dw_ref[...] = jnp.zeros_like(dw_ref)

        xv = x_ref[...]
        wv = w_ref[...]
        e = xv * _rms_r(xv) * wv - t_ref[...]
        loss_ref[...] += 0.5 * jnp.sum(jnp.mean(e * e, axis=-1, keepdims=True))
        dx, dn = _rms_bwd(e * (1.0 / D), xv, wv)
        dx_ref[...] = dx
        dw_ref[...] += dn

    tok = pl.BlockSpec((tm, D), lambda i: (i, 0))
    return pl.pallas_call(
        body, name=name, grid=(T // tm,),
        in_specs=[tok, tok, pl.BlockSpec((1, D), lambda i: (0, 0))],
        out_specs=[pl.BlockSpec((8, LANE), lambda i: (0, 0)), tok, pl.BlockSpec((1, D), lambda i: (0, 0))],
        out_shape=[_sds((8, LANE), F32), _sds((T, D), F32), _sds((1, D), F32)],
        compiler_params=_cp((ARB,)),
    )(x, tgt, fw)


def _block_rows(rows, cols):
    cap = max(8, (1 << 18) // cols)
    best = None
    for r in range(8, rows + 1, 8):
        if rows % r == 0 and r <= cap:
            best = r
    return best if best is not None else rows


def _sum8(l_ref):
    def four(base):
        return ((l_ref[base + 3].astype(F32) + l_ref[base].astype(F32)) + l_ref[base + 1].astype(F32)
                ) + l_ref[base + 2].astype(F32)

    return four(0) + four(4)


def _sum_adamw(lands, w, m, v, name):
    _, rows, cols = lands[0].shape
    br = _block_rows(rows, cols)
    nb = rows // br
    n_layers = len(lands)

    def body(*refs):
        l_refs = refs[:n_layers]
        w_ref, m_ref, v_ref, g_ref, d_ref, nm_ref, nv_ref = refs[n_layers:]
        i = pl.program_id(0)
        for l, l_ref in enumerate(l_refs):
            @pl.when((i >= l * nb) & (i < (l + 1) * nb))
            def _():
                g = _sum8(l_ref)
                d, nm, nv = _adamw_math(w_ref[...], g, m_ref[...], v_ref[...])
                g_ref[...] = g
                d_ref[...] = d
                nm_ref[...] = nm
                nv_ref[...] = nv

    def land_spec(l):
        return pl.BlockSpec((2 * N_SHARD, br, cols), lambda i: (0, jnp.clip(i - l * nb, 0, nb - 1), 0))

    blk = pl.BlockSpec((br, cols), lambda i: (i, 0))
    return pl.pallas_call(
        body, name=name, grid=(n_layers * nb,),
        in_specs=[land_spec(l) for l in range(n_layers)] + [blk] * 3, out_specs=[blk] * 4,
        out_shape=[_sds((n_layers * rows, cols), F32)] * 4,
        compiler_params=_cp((PAR,)),
    )(*lands, w, m, v)


def _adamw_math(w, g, m, v):
    m = ADAM_B1 * m + (1.0 - ADAM_B1) * g
    v = ADAM_B2 * v + (1.0 - ADAM_B2) * (g * g)
    m_hat = m / (1.0 - ADAM_B1 ** ADAM_STEP)
    v_hat = v / (1.0 - ADAM_B2 ** ADAM_STEP)
    delta = -ADAM_LR * (m_hat / (jnp.sqrt(v_hat) + ADAM_EPS) + ADAM_WD * w)
    return delta, m, v


def _adamw(w, g, m, v, name):
    rows, cols = w.shape
    br = _block_rows(rows, cols)

    def body(w_ref, g_ref, m_ref, v_ref, d_ref, nm_ref, nv_ref):
        d, nm, nv = _adamw_math(w_ref[...], g_ref[...], m_ref[...], v_ref[...])
        d_ref[...] = d
        nm_ref[...] = nm
        nv_ref[...] = nv

    blk = pl.BlockSpec((br, cols), lambda i: (i, 0))
    return pl.pallas_call(
        body, name=name, grid=(rows // br,),
        in_specs=[blk] * 4, out_specs=[blk] * 3,
        out_shape=[_sds((rows, cols), F32)] * 3,
        compiler_params=_cp((PAR,)),
    )(w, g, m, v)


def _allreduce_small(v, name):
    rows = v.shape[0]
    flips = [(fx, fy, fc) for fx in (0, 1) for fy in (0, 1) for fc in (0, 1) if fx or fy or fc]

    def body(v_ref, o_ref, all_s, ssem, rsem):
        x, y, c = _place()

        def peer(f):
            return (x + f[0] - 2 * x * f[0], y + f[1] - 2 * y * f[1], c + f[2] - 2 * c * f[2])

        def slot(p):
            return all_s.at[4 * p[0] + 2 * p[1] + p[2]]

        def copy(k, f, owner):
            return pltpu.make_async_remote_copy(
                src_ref=v_ref, dst_ref=slot(owner), send_sem=ssem.at[k], recv_sem=rsem.at[k],
                device_id=peer(f), device_id_type=MESH)

        sends = [copy(k, f, (x, y, c)) for k, f in enumerate(flips)]
        for cp in sends:
            cp.start()
        all_s[4 * x + 2 * y + c] = v_ref[...]
        for k, f in enumerate(flips):
            copy(k, f, peer(f)).wait_recv()
        for cp in sends:
            cp.wait_send()
        acc = all_s[0]
        for d in range(1, 8):
            acc = acc + all_s[d]
        o_ref[...] = acc

    return pl.pallas_call(
        body, name=name,
        in_specs=[pl.BlockSpec(memory_space=pltpu.VMEM)],
        out_specs=pl.BlockSpec(memory_space=pltpu.VMEM),
        out_shape=_sds((rows, LANE), F32),
        scratch_shapes=[pltpu.VMEM((8, rows, LANE), F32), pltpu.SemaphoreType.DMA((7,)), pltpu.SemaphoreType.DMA((7,))],
    )(v)


BIG_NAMES = ("ffn1_w_gate", "ffn1_w_up", "ffn1_w_down", "w_in", "w_branch", "w_merge_gate", "w_out",
             "ffn2_w_gate", "ffn2_w_up", "ffn2_w_down")


FFN1 = ("ffn1_w_gate", "ffn1_w_up", "ffn1_w_down")
FFN2 = ("ffn2_w_gate", "ffn2_w_up", "ffn2_w_down")
MIX_IN = ("w_in", "w_merge_gate")
MIX_OUT = ("w_branch", "w_out")


def _keys(names, l):
    return [(n, l) for n in names]


def _local_step(x, tgt, small, convw_full, biases, wx, n_layers):
    T, D = x.shape
    L = n_layers
    ns = N_SHARD
    dq = D // ns
    W = wx.w

    def hosted(call, keys, scatter=False):
        comm = wx.pieces(keys, scatter)
        main, extra = call(comm)
        if comm is not None:
            wx.arrived(keys, extra, scatter)
        return main

    def mixer_views(l):
        return _build_wbig(W[("w_merge_gate", l)], W[("w_in", l)], f"wbig_{l}")

    def out_views(l):
        wb4 = W[("w_branch", l)]
        wb = _copy_blocks(wb4, pl.BlockSpec((None, None, BRANCH_W, dq), lambda s_, i: (s_, i, 0, 0)),
                          _sds((3, BRANCH_W, D), wb4.dtype),
                          pl.BlockSpec((None, BRANCH_W, dq), lambda s_, i: (i, 0, s_)), (ns, 3), f"w_branch_whole_{l}")
        wo = W[("w_out", l)].reshape(D, D)
        return wb, wo

    tb = _ret_tables(T)

    saved = []
    h = x
    for l in range(L):
        s = {"x0": h}
        nxt = l + 1
        x1, s["g1"], s["u1"] = hosted(
            lambda c: _ffn_fwd(h, small["ffn1_norm"][l][None], W[("ffn1_w_gate", l)], W[("ffn1_w_up", l)],
                               W[("ffn1_w_down", l)], f"ffn1_fwd_{l}", comm=c), _keys(MIX_IN + MIX_OUT, l))
        s["x1"] = x1
        s["wbig"] = mixer_views(l)
        big, s["h"] = _inproj_fwd(x1, small["mix_norm"][l][None], s["wbig"], f"inproj_fwd_{l}")
        s["big"] = big
        s["bias"] = biases[l]
        s["yc"] = _conv_fwd(big, convw_full[l], f"conv_fwd_{l}")
        s["yr"], s["o"], s["st"] = hosted(lambda c: _ret_fwd(big, tb, f"ret_fwd_{l}", comm=c), [])
        (s["ya"],) = hosted(lambda c: _att_fwd(big, s["bias"], f"att_fwd_{l}", comm=c), _keys(FFN2, l))
        s["wb"], s["wo"] = out_views(l)
        x2, s["p"], s["mg"] = _merge_fwd(x1, big, (s["yc"], s["yr"], s["ya"]), s["wb"], s["wo"], f"merge_fwd_{l}")
        s["x2"] = x2
        h, s["g2"], s["u2"] = hosted(
            lambda c: _ffn_fwd(x2, small["ffn2_norm"][l][None], W[("ffn2_w_gate", l)], W[("ffn2_w_up", l)],
                               W[("ffn2_w_down", l)], f"ffn2_fwd_{l}", comm=c), _keys(FFN1, nxt) if nxt < L else [])
        saved.append(s)

    loss_p, dx, d_final = _loss_head(h, tgt, small["final_norm"][None], "loss_head")

    gs = {"final_norm": d_final[0]}
    for k in ("ffn1_norm", "mix_norm", "ffn2_norm", "rel_bias", "conv_w"):
        gs[k] = [None] * L
    tk = min(2048, T)
    nk = T // tk

    def ffn_back(pre, l, dxo, x_in, g, u, first_keys, second_keys, between=None):
        nw = small[pre + "_norm"][l][None]
        dgv, duv, av, hb, dacc = hosted(
            lambda c: _ffn_bwd_hidden(dxo, x_in, nw, g, u, W[(pre + "_w_down", l)], f"{pre}_bwd_hidden_{l}", comm=c),
            first_keys, scatter=True)
        parts = (hb, dgv, duv, av, dacc)
        if between is not None:
            second_keys = between(parts)
        dxn, dn = hosted(
            lambda c: _ffn_bwd_resid(dgv, duv, W[(pre + "_w_gate", l)], W[(pre + "_w_up", l)], x_in, nw, dxo,
                                     f"{pre}_bwd_resid_{l}", comm=c),
            second_keys, scatter=True)
        gs[pre + "_norm"][l] = dn[0]
        return dxn, parts

    def ffn_grads(pre, l, hb, dgv, duv, av, dacc, chain=False):
        fs = dgv.shape[-1]
        tkf = min(2 * tk, T)
        hspec = pl.BlockSpec((tkf, D), lambda p, q, k: (k, 0))
        sspec = pl.BlockSpec((None, tkf, fs), lambda p, q, k: (p, k, 0))
        down_spec = pl.BlockSpec((None, fs, D), lambda p, q, k: (p, 0, 0))
        jobs = [(pre + "_w_gate", dgv, hb, sspec, hspec, (ns, fs, D), down_spec),
                (pre + "_w_up", duv, hb, sspec, hspec, (ns, fs, D), down_spec),
                (pre + "_w_down", av, dacc, sspec, hspec, (ns, fs, D), down_spec)]
        for idx, (nm, a, b, a_spec, b_spec, shape, o_spec) in enumerate(jobs):
            def product(c):
                r = _tn(a, b, a_spec, b_spec, _sds(shape, BF), o_spec, (ns, 1, T // tkf), f"d{nm}_{l}", comm=c)
                return (r, []) if c is None else r
            keys = [(jobs[0][0], l)] if chain and idx == 2 else []
            wx.g[(nm, l)] = hosted(product, keys, scatter=True)
        return [(jobs[1][0], l), (jobs[2][0], l)] if chain else []

    for l in reversed(range(L)):
        s = saved[l]
        above = _keys(FFN1, l + 1) if l + 1 < L else []
        dx, parts = ffn_back("ffn2", l, dx, s["x2"], s["g2"], s["u2"], above[:1], above[1:])
        ffn_grads("ffn2", l, *parts)
        dp, dbig, dyc, dyr, dya, dxb = _merge_bwd(dx, s["big"], s["p"], s["wb"], s["wo"], f"merge_bwd_{l}")
        wx.g[("w_out", l)] = _tn_rows(s["mg"], dxb, ns, tk, 1, f"dw_out_{l}").reshape(ns, dq, D)
        wx.g[("w_branch", l)] = _tn_branches((s["yc"], s["yr"], s["ya"]), dp, ns, tk, f"dw_branch_{l}")
        dbig, dcw = _conv_bwd(s["big"], dyc, convw_full[l], dbig, f"conv_bwd_{l}")
        gs["conv_w"][l] = dcw
        dbig = _ret_bwd(s["big"], s["o"], s["st"], dyr, tb, dbig, f"ret_bwd_{l}")
        dbig, dbias = hosted(lambda c: _att_bwd(s["big"], s["bias"], dya, dbig, f"att_bwd_{l}", comm=c),
                             _keys(FFN2, l), scatter=True)
        gs["rel_bias"][l] = _relbias_grad(jnp.transpose(dbias, (1, 0, 2)), f"relbias_grad_{l}")[:, :N_REL]
        n_in = N_SEG * BRANCH_W
        bn = 1024 if (3 * D) % 1024 == 0 else BRANCH_W
        dwp = _tn(s["h"], dbig, pl.BlockSpec((tk, D), lambda p, q, k: (k, 0)),
                  pl.BlockSpec((tk, bn), lambda p, q, k: (k, 3 * D // bn + q)),
                  _sds((D, n_in), BF), pl.BlockSpec((D, bn), lambda p, q, k: (0, q)), (1, n_in // bn, nk), f"dw_in_{l}")
        wx.g[("w_in", l)] = _ungroup_dw_in(dwp, ns, f"dw_in_shards_{l}")
        wx.g[("w_merge_gate", l)] = _tn_rows(s["h"], dbig, ns, tk, 3, f"dw_merge_gate_{l}")
        dx, dn = hosted(
            lambda c: _inproj_bwd(dbig, s["wbig"], s["x1"], small["mix_norm"][l][None], dx, f"inproj_bwd_{l}", comm=c),
            [("w_in", l)], scatter=True)
        gs["mix_norm"][l] = dn[0]
        rest = [("w_merge_gate", l), ("w_branch", l), ("w_out", l)]
        if l == 0:
            dx, _ = ffn_back("ffn1", l, dx, s["x0"], s["g1"], s["u1"], rest, [],
                             between=lambda parts: ffn_grads("ffn1", 0, *parts, chain=True))
        else:
            dx, parts = ffn_back("ffn1", l, dx, s["x0"], s["g1"], s["u1"], rest, [])
            ffn_grads("ffn1", l, *parts)

    for k in ("ffn1_norm", "mix_norm", "ffn2_norm", "rel_bias", "conv_w"):
        gs[k] = jnp.stack(gs[k])
    return loss_p, dx, gs


class _Exchange:
    def __init__(self, shards):
        self.shards = shards
        self.w = {}
        self.g = {}
        self.landed = {}

    def own(self, key):
        return self.shards[key[0]][key[1]].astype(BF)

    def pieces(self, keys, scatter):
        if not keys:
            return None
        if scatter:
            return _Scatter([self.g[k] for k in keys])
        return _HalfGather([_halves(self.own(k)) for k in keys])

    def arrived(self, keys, outs, scatter):
        for k, o in zip(keys, outs):
            if scatter:
                self.landed[k] = o
            else:
                self.w[k] = o.reshape((N_SHARD,) + self.shards[k[0]].shape[1:])


def _halves(a):
    return a.reshape(2, -1, a.shape[-1])


TRANSPOSED_GRADS = ("ffn1_w_gate", "ffn1_w_up", "ffn2_w_gate", "ffn2_w_up")
W_NAMES = ("ffn1_norm", "ffn1_w_gate", "ffn1_w_up", "ffn1_w_down", "mix_norm", "w_in", "conv_w", "rel_bias", "w_branch",
           "w_merge_gate", "w_out", "ffn2_norm", "ffn2_w_gate", "ffn2_w_up", "ffn2_w_down", "final_norm")


def _as2d(a):
    return a.reshape(1, -1) if a.ndim == 1 else a.reshape(-1, a.shape[-1])


def kernel(x, ffn1_norm, ffn1_w_gate, ffn1_w_up, ffn1_w_down, mix_norm, w_in, conv_w, rel_bias, w_branch, w_merge_gate, w_out, ffn2_norm, ffn2_w_gate, ffn2_w_up, ffn2_w_down, final_norm, loss_target, m_ffn1_norm, m_ffn1_w_gate, m_ffn1_w_up, m_ffn1_w_down, m_mix_norm, m_w_in, m_conv_w, m_rel_bias, m_w_branch, m_w_merge_gate, m_w_out, m_ffn2_norm, m_ffn2_w_gate, m_ffn2_w_up, m_ffn2_w_down, m_final_norm, v_ffn1_norm, v_ffn1_w_gate, v_ffn1_w_up, v_ffn1_w_down, v_mix_norm, v_w_in, v_conv_w, v_rel_bias, v_w_branch, v_w_merge_gate, v_w_out, v_ffn2_norm, v_ffn2_w_gate, v_ffn2_w_up, v_ffn2_w_down, v_final_norm):
    given = dict(locals())
    w = {n: given[n] for n in W_NAMES}
    m = {n: given["m_" + n] for n in W_NAMES}
    v = {n: given["v_" + n] for n in W_NAMES}
    my_chip = 2 * lax.axis_index("x") + lax.axis_index("y")
    L = w_in.shape[0]

    wx = _Exchange({n: jnp.swapaxes(w[n], 1, 2) if n in TRANSPOSED_GRADS else w[n] for n in BIG_NAMES})
    first = _keys(FFN1, 0)
    biases, got = _relbias_expand(
        rel_bias, "relbias_expand", comm=_HalfGather([_halves(wx.own(k)) for k in first] + [_halves(conv_w)]))
    wx.arrived(first, got[:-1], False)
    convw_full = jnp.transpose(got[-1].reshape((N_SHARD,) + conv_w.shape), (1, 2, 0, 3)).reshape(
        conv_w.shape[0], conv_w.shape[1], -1)

    small = {n: w[n] for n in ("ffn1_norm", "mix_norm", "ffn2_norm", "final_norm")}
    loss_p, grad_x, gs = _local_step(x[0], loss_target[0], small, convw_full, biases, wx, L)

    parts = [gs["ffn1_norm"].reshape(-1), gs["mix_norm"].reshape(-1), gs["ffn2_norm"].reshape(-1),
             gs["final_norm"].reshape(-1), gs["rel_bias"].reshape(-1), gs["conv_w"].reshape(-1), loss_p[0]]
    sizes = [p.shape[0] for p in parts]
    flat = jnp.concatenate(parts)
    rows = -(-flat.shape[0] // (8 * LANE)) * 8
    flat = jnp.pad(flat, (0, rows * LANE - flat.shape[0])).reshape(rows, LANE)
    red = _allreduce_small(flat, "allreduce_small").reshape(-1)
    offs = [0]
    for sz in sizes:
        offs.append(offs[-1] + sz)
    sm = {}
    for i, n in enumerate(("ffn1_norm", "mix_norm", "ffn2_norm", "final_norm", "rel_bias", "conv_w")):
        sm[n] = red[offs[i]:offs[i + 1]]
    loss = red[offs[6]]
    sm["conv_w"] = lax.dynamic_slice_in_dim(sm["conv_w"].reshape(conv_w.shape[0], conv_w.shape[1], -1),
                                            my_chip * conv_w.shape[2], conv_w.shape[2], axis=2)

    grads, deltas, new_m, new_v = {}, {}, {}, {}
    for n in W_NAMES:
        flip = n in TRANSPOSED_GRADS

        def view(a):
            return jnp.swapaxes(a, 1, 2) if flip else a

        shape = view(w[n]).shape
        wmv = [_as2d(view(a[n])) for a in (w, m, v)]
        if n in BIG_NAMES:
            lands = [wx.landed[(n, l)] for l in range(L)]
            out = _sum_adamw([a.reshape(a.shape[0], -1, a.shape[-1]) for a in lands], *wmv, f"adamw_{n}")
        else:
            g = _as2d(sm[n].reshape(shape))
            out = [g] + list(_adamw(wmv[0], g, wmv[1], wmv[2], f"adamw_{n}"))
        grads[n], deltas[n], new_m[n], new_v[n] = (view(o.reshape(shape)) for o in out)

    return (loss, grad_x[None], *[grads[n] for n in W_NAMES], *[deltas[n] for n in W_NAMES],
            *[new_m[n] for n in W_NAMES], *[new_v[n] for n in W_NAMES])
```

```python
import functools
import math

import jax
import jax.numpy as jnp
from jax import lax
from jax.experimental import pallas as pl
from jax.experimental.pallas import tpu as pltpu

F32 = jnp.float32
BF = jnp.bfloat16
MESH = pl.DeviceIdType.MESH
ARB = "arbitrary"
PAR = "parallel"

EPS = 1e-6
NEG_INF = -1e30
ROPE_BASE = 10000.0
CHUNK = 64
BRANCH_W = 512
H_RET = 4
DK_RET = 128
H_ATT = 8
DH_ATT = 64
N_PREV = 8
REL_CLIP = 128
N_REL = 2 * REL_CLIP + 1
N_SHARD = 4
LANE = 128
RET_L = 512
ATT_TQ = 128
ATT_SUB = 4
ATT_PAD = N_PREV * CHUNK
ATT_SPAN = ATT_TQ + ATT_PAD
ATT_TOEP = 2 * REL_CLIP
RB_PAD = 264
TM = 512
TM_FFN = 1024

ADAM_LR = 0.001
ADAM_B1 = 0.9
ADAM_B2 = 0.999
ADAM_EPS = 1e-08
ADAM_WD = 0.01
ADAM_STEP = 10

NT_DIMS = (((1,), (1,)), ((), ()))
TN_DIMS = (((0,), (0,)), ((), ()))


def _cp(sem, vmem_mb=48):
    return pltpu.CompilerParams(dimension_semantics=sem, vmem_limit_bytes=vmem_mb << 20)


def _sds(shape, dtype):
    return jax.ShapeDtypeStruct(tuple(shape), dtype)


def _rms_r(x):
    return lax.rsqrt(jnp.mean(x * x, axis=-1, keepdims=True) + EPS)


def _sigmoid(x):
    return 0.5 * jnp.tanh(0.5 * x) + 0.5


def _rms_bwd(dh, xv, nw):
    r = _rms_r(xv)
    xh = xv * r
    dxh = dh * nw
    dx = r * (dxh - xh * jnp.mean(dxh * xh, axis=-1, keepdims=True))
    return dx, jnp.sum(dh * xh, axis=0, keepdims=True)


def _place():
    return lax.axis_index("x"), lax.axis_index("y"), lax.axis_index("c")


def _other_chips(x, y):
    return [(1 - x, y), (x, 1 - y), (1 - x, 1 - y)]


class _Scatter:
    def __init__(self, srcs):
        self.srcs = list(srcs)
        n = len(self.srcs)
        self.out_shape = [_sds((2 * N_SHARD,) + s.shape[1:], s.dtype) for s in self.srcs]
        self.scratch = [pltpu.SemaphoreType.DMA((n,)), pltpu.SemaphoreType.DMA((3, n)), pltpu.SemaphoreType.DMA((3, n)),
                        pltpu.SemaphoreType.DMA((4, n)), pltpu.SemaphoreType.DMA((4, n))]

    def _plan(self, src, dst, sems, want):
        lsem, s1, r1, s2, r2 = sems
        x, y, c = _place()
        mine = 2 * x + y
        n = len(src)
        chips = list(enumerate(_other_chips(x, y)))

        def copy(s_ref, d_ref, ssem, rsem, to):
            return pltpu.make_async_remote_copy(src_ref=s_ref, dst_ref=d_ref, send_sem=ssem, recv_sem=rsem,
                                                device_id=to, device_id_type=MESH)

        local = [pltpu.make_async_copy(src[k].at[mine], dst[k].at[3], lsem.at[k]) for k in range(n)
                 ] if "local" in want else []
        sends = [copy(src[k].at[2 * ch[0] + ch[1]], dst[k].at[j], s1.at[j, k], r1.at[j, k], (ch[0], ch[1], c))
                 for j, ch in chips for k in range(n)] if "sends" in want else []
        passes = [copy(dst[k].at[j], dst[k].at[4 + j], s2.at[j, k], r2.at[j, k], (x, y, 1 - c))
                  for j, ch in chips for k in range(n)] if "passes" in want else []
        own_pass = [copy(src[k].at[mine], dst[k].at[7], s2.at[3, k], r2.at[3, k], (x, y, 1 - c))
                    for k in range(n)] if "own_pass" in want else []
        return local, sends, passes, own_pass

    def start(self, src, dst, sems):
        local, sends, _, own_pass = self._plan(src, dst, sems, ("local", "sends", "own_pass"))
        for cp in local + sends + own_pass:
            cp.start()

    def relay(self, src, dst, sems):
        _, sends, passes, _ = self._plan(src, dst, sems, ("sends", "passes"))
        for land, fwd in zip(sends, passes):
            land.wait_recv()
            fwd.start()

    def finish(self, src, dst, sems):
        local, sends, passes, own_pass = self._plan(src, dst, sems, ("local", "sends", "passes", "own_pass"))
        for cp in passes + own_pass:
            cp.wait_recv()
        for cp in sends + passes + own_pass:
            cp.wait_send()
        for cp in local:
            cp.wait()

    def wait(self, src, dst, sems):
        self.relay(src, dst, sems)
        self.finish(src, dst, sems)


class _HalfGather:
    def __init__(self, srcs):
        self.srcs = list(srcs)
        n = len(self.srcs)
        self.out_shape = [_sds((N_SHARD,) + s.shape, s.dtype) for s in self.srcs]
        self.scratch = [pltpu.SemaphoreType.DMA((n,))] + [pltpu.SemaphoreType.DMA((3, n)) for _ in range(4)]

    def _plan(self, src, dst, sems, want):
        lsem, s1, r1, s2, r2 = sems
        x, y, c = _place()
        mine = 2 * x + y
        n = len(src)
        chips = [(j, ch, 2 * ch[0] + ch[1]) for j, ch in enumerate(_other_chips(x, y))]

        def copy(s_ref, d_ref, ssem, rsem, to):
            return pltpu.make_async_remote_copy(src_ref=s_ref, dst_ref=d_ref, send_sem=ssem, recv_sem=rsem,
                                                device_id=to, device_id_type=MESH)

        def over(kind, make):
            return [make(j, ch, slot, k) for j, ch, slot in chips for k in range(n)] if kind in want else []

        local = [pltpu.make_async_copy(src[k], dst[k].at[mine], lsem.at[k]) for k in range(n)] if "local" in want else []
        sends = over("sends", lambda j, ch, slot, k: copy(src[k].at[c], dst[k].at[mine, c], s1.at[j, k], r1.at[j, k],
                                                          (ch[0], ch[1], c)))
        lands = over("lands", lambda j, ch, slot, k: copy(src[k].at[c], dst[k].at[slot, c], s1.at[j, k], r1.at[j, k],
                                                          (ch[0], ch[1], c)))
        passes = over("passes", lambda j, ch, slot, k: copy(dst[k].at[slot, c], dst[k].at[slot, c], s2.at[j, k],
                                                            r2.at[j, k], (x, y, 1 - c)))
        gets = over("gets", lambda j, ch, slot, k: copy(dst[k].at[slot, 1 - c], dst[k].at[slot, 1 - c], s2.at[j, k],
                                                        r2.at[j, k], (x, y, 1 - c)))
        return local, sends, lands, passes, gets

    def start(self, src, dst, sems):
        lsem, s1, r1, s2, r2 = sems
        x, y, c = _place()
        mine = 2 * x + y
        for k in range(len(src)):
            pltpu.make_async_copy(src[k], dst[k].at[mine], lsem.at[k]).start()
        for j, ch in enumerate(_other_chips(x, y)):
            for k in range(len(src)):
                pltpu.make_async_remote_copy(
                    src_ref=src[k].at[c], dst_ref=dst[k].at[mine, c], send_sem=s1.at[j, k], recv_sem=r1.at[j, k],
                    device_id=(ch[0], ch[1], c), device_id_type=MESH).start()

    def relay(self, src, dst, sems):
        _, _, lands, passes, _ = self._plan(src, dst, sems, ("lands", "passes"))
        for land, fwd in zip(lands, passes):
            land.wait_recv()
            fwd.start()

    def finish(self, src, dst, sems):
        local, sends, _, passes, gets = self._plan(src, dst, sems, ("local", "sends", "passes", "gets"))
        for cp in gets:
            cp.wait_recv()
        for cp in sends + passes:
            cp.wait_send()
        for cp in local:
            cp.wait()

    def wait(self, src, dst, sems):
        self.relay(src, dst, sems)
        self.finish(src, dst, sems)


def _call(body, *, name, args, in_specs, out_specs, out_shape, grid=(), scratch_shapes=(), sem=None, comm=None,
          aliases=None, vmem_mb=48):
    in_specs, out_specs, out_shape = list(in_specs), list(out_specs), list(out_shape)
    scratch, args = list(scratch_shapes), list(args)
    n_in, n_out, n_scr = len(in_specs), len(out_specs), len(scratch)
    if comm is None:
        def kernel_body(*refs):
            body(*refs)
    else:
        c_in, c_out = len(comm.srcs), len(comm.out_shape)

        def kernel_body(*refs):
            o0 = n_in + c_in
            s0 = o0 + n_out + c_out
            cin, cout, sems = refs[n_in:o0], refs[o0 + n_out:s0], refs[s0 + n_scr:]
            main = refs[:n_in] + refs[o0:o0 + n_out] + refs[s0:s0 + n_scr]
            if grid:
                ids = [pl.program_id(a) for a in range(len(grid))]
                first = functools.reduce(lambda p, q: p & q, [i == 0 for i in ids])
                last = functools.reduce(lambda p, q: p & q, [i == g - 1 for i, g in zip(ids, grid)])

                @pl.when(first)
                def _():
                    comm.start(cin, cout, sems)

                body(*main)

                steps = math.prod(grid)
                if hasattr(comm, "relay") and steps >= 4:
                    flat = functools.reduce(lambda p, q: p + q, [i * math.prod(grid[a + 1:]) for a, i in enumerate(ids)])

                    @pl.when(flat == (5 * steps) // 6)
                    def _():
                        comm.relay(cin, cout, sems)

                    @pl.when(last)
                    def _():
                        comm.finish(cin, cout, sems)
                else:
                    @pl.when(last)
                    def _():
                        comm.wait(cin, cout, sems)
            else:
                comm.start(cin, cout, sems)
                body(*main)
                comm.wait(cin, cout, sems)

        hbm = pl.BlockSpec(memory_space=pl.ANY)
        in_specs += [hbm] * c_in
        out_specs += [hbm] * c_out
        out_shape += comm.out_shape
        scratch += comm.scratch
        args += comm.srcs
    params = dict(vmem_limit_bytes=vmem_mb << 20)
    if grid:
        params["dimension_semantics"] = sem
    outs = pl.pallas_call(
        kernel_body, name=name, grid=grid, in_specs=in_specs, out_specs=out_specs, out_shape=out_shape,
        scratch_shapes=scratch, input_output_aliases=aliases or {}, compiler_params=pltpu.CompilerParams(**params),
    )(*args)
    return list(outs[:n_out]), list(outs[n_out:])


def _ffn_fwd(x, nw, wg, wu, wd, name, comm=None):
    T, D = x.shape
    ns, fs, _ = wg.shape
    tm = min(TM_FFN, T)

    def body(x_ref, nw_ref, wg_ref, wu_ref, wd_ref, xo_ref, g_ref, u_ref, h_s, acc_s):
        j = pl.program_id(1)

        @pl.when(j == 0)
        def _():
            xv = x_ref[...]
            h_s[...] = (xv * _rms_r(xv) * nw_ref[...]).astype(BF)
            acc_s[...] = jnp.zeros_like(acc_s)

        h = h_s[...]
        gb = lax.dot_general(h, wg_ref[...], NT_DIMS, preferred_element_type=F32).astype(BF)
        ub = lax.dot_general(h, wu_ref[...], NT_DIMS, preferred_element_type=F32).astype(BF)
        g_ref[...] = gb
        u_ref[...] = ub
        g = gb.astype(F32)
        a = (g * _sigmoid(g) * ub.astype(F32)).astype(BF)
        acc_s[...] += jnp.dot(a, wd_ref[...], preferred_element_type=F32)

        @pl.when(j == ns - 1)
        def _():
            xo_ref[...] = x_ref[...] + 0.5 * acc_s[...]

    wspec = pl.BlockSpec((None, fs, D), lambda i, j: (j, 0, 0))
    return _call(
        body, name=name, grid=(T // tm, ns), args=(x, nw, wg, wu, wd), comm=comm, vmem_mb=56,
        in_specs=[pl.BlockSpec((tm, D), lambda i, j: (i, 0)),
                  pl.BlockSpec((1, D), lambda i, j: (0, 0)),
                  wspec, wspec,
                  pl.BlockSpec((None, fs, D), lambda i, j: (j, 0, 0))],
        out_specs=[pl.BlockSpec((tm, D), lambda i, j: (i, 0)),
                   pl.BlockSpec((None, tm, fs), lambda i, j: (j, i, 0)),
                   pl.BlockSpec((None, tm, fs), lambda i, j: (j, i, 0))],
        out_shape=[_sds((T, D), F32), _sds((ns, T, fs), BF), _sds((ns, T, fs), BF)],
        scratch_shapes=[pltpu.VMEM((tm, D), BF), pltpu.VMEM((tm, D), F32)],
        sem=(ARB, ARB))


def _ffn_bwd_hidden(dxo, x, nw, g, u, wd, name, comm=None):
    T, D = x.shape
    ns, fs, _ = wd.shape
    tm = min(TM_FFN, T)

    def body(dxo_ref, x_ref, nw_ref, g_ref, u_ref, wd_ref, dg_ref, du_ref, a_ref, h_ref, dacc_ref, dacc_s):
        @pl.when(pl.program_id(1) == 0)
        def _():
            xv = x_ref[...]
            h_ref[...] = (xv * _rms_r(xv) * nw_ref[...]).astype(BF)
            db = (0.5 * dxo_ref[...]).astype(BF)
            dacc_ref[...] = db
            dacc_s[...] = db

        da = lax.dot_general(dacc_s[...], wd_ref[...], NT_DIMS, preferred_element_type=F32).astype(BF)
        gb = g_ref[...]
        ub = u_ref[...]
        s = _sigmoid(gb.astype(F32)).astype(BF)
        sg = gb * s
        a_ref[...] = sg * ub
        du_ref[...] = da * sg
        dg_ref[...] = (da * ub) * (s + sg * (1.0 - s))

    tok = pl.BlockSpec((tm, D), lambda i, j: (i, 0))
    hid = pl.BlockSpec((None, tm, fs), lambda i, j: (j, i, 0))
    return _call(
        body, name=name, grid=(T // tm, ns), args=(dxo, x, nw, g, u, wd), comm=comm, vmem_mb=56,
        in_specs=[tok, tok, pl.BlockSpec((1, D), lambda i, j: (0, 0)), hid, hid,
                  pl.BlockSpec((None, fs, D), lambda i, j: (j, 0, 0))],
        out_specs=[hid, hid, hid, tok, tok],
        out_shape=[_sds((ns, T, fs), BF)] * 3 + [_sds((T, D), BF)] * 2,
        scratch_shapes=[pltpu.VMEM((tm, D), BF)],
        sem=(ARB, ARB))


def _ffn_bwd_resid(dg, du, wg, wu, x, nw, dxo, name, comm=None):
    T, D = x.shape
    ns, fs, _ = wg.shape
    tm = min(TM_FFN, T)

    def body(dg_ref, du_ref, wg_ref, wu_ref, x_ref, nw_ref, dxo_ref, dx_ref, dnw_ref, acc_s):
        i = pl.program_id(0)
        j = pl.program_id(1)
        prod = (jnp.dot(dg_ref[...], wg_ref[...], preferred_element_type=F32)
                + jnp.dot(du_ref[...], wu_ref[...], preferred_element_type=F32))

        @pl.when((i == 0) & (j == 0))
        def _():
            dnw_ref[...] = jnp.zeros_like(dnw_ref)

        @pl.when(j == 0)
        def _():
            acc_s[...] = prod

        @pl.when(j > 0)
        def _():
            acc_s[...] += prod

        @pl.when(j == ns - 1)
        def _():
            dx, dn = _rms_bwd(acc_s[...], x_ref[...], nw_ref[...])
            dx_ref[...] = dxo_ref[...] + dx
            dnw_ref[...] += dn

    tok = pl.BlockSpec((tm, D), lambda i, j: (i, 0))
    row = pl.BlockSpec((1, D), lambda i, j: (0, 0))
    hid = pl.BlockSpec((None, tm, fs), lambda i, j: (j, i, 0))
    wspec = pl.BlockSpec((None, fs, D), lambda i, j: (j, 0, 0))
    return _call(
        body, name=name, grid=(T // tm, ns), args=(dg, du, wg, wu, x, nw, dxo), comm=comm, vmem_mb=56,
        in_specs=[hid, hid, wspec, wspec, tok, row, tok],
        out_specs=[tok, row],
        out_shape=[_sds((T, D), F32), _sds((1, D), F32)],
        scratch_shapes=[pltpu.VMEM((tm, D), F32)],
        sem=(ARB, ARB))


def _tn(a, b, a_spec, b_spec, out_shape, out_spec, grid, name, prev=None, comm=None):
    nk = grid[-1]
    acc_shape = tuple(d for d in out_spec.block_shape if d is not None)

    def body(*refs):
        a_ref, b_ref = refs[0], refs[1]
        o_ref, acc = refs[-2], refs[-1]
        k = pl.program_id(2)
        prod = lax.dot_general(a_ref[...], b_ref[...], TN_DIMS, preferred_element_type=F32)

        @pl.when(k == 0)
        def _():
            acc[...] = prod

        @pl.when(k > 0)
        def _():
            acc[...] += prod

        @pl.when(k == nk - 1)
        def _():
            o_ref[...] = acc[...].astype(o_ref.dtype)

    in_specs = [a_spec, b_spec]
    args = [a, b]
    aliases = {}
    if prev is not None:
        in_specs.append(pl.BlockSpec(memory_space=pl.ANY))
        args.append(prev)
        aliases = {2: 0}
    main, extra = _call(
        body, name=name, grid=grid, args=args, in_specs=in_specs, out_specs=[out_spec], out_shape=[out_shape],
        scratch_shapes=[pltpu.VMEM(acc_shape, F32)], aliases=aliases, sem=(ARB, ARB, ARB), comm=comm)
    return main[0] if comm is None else (main[0], extra)


def _tn_branches(ys, dp, ns, tk, name):
    T, w = ys[0].shape
    D = dp.shape[1] // 3
    dq = D // ns
    nk = T // tk

    def body(yc_ref, yr_ref, ya_ref, b_ref, o_ref, acc):
        i = pl.program_id(0)
        k = pl.program_id(1)

        @pl.when(k == 0)
        def _():
            acc[...] = jnp.zeros_like(acc)

        for branch, y_ref in enumerate((yc_ref, yr_ref, ya_ref)):
            @pl.when(i == branch)
            def _():
                acc[...] += lax.dot_general(y_ref[...], b_ref[...], TN_DIMS, preferred_element_type=F32)

        @pl.when(k == nk - 1)
        def _():
            for s in range(ns):
                o_ref[s] = acc[:, s * dq:(s + 1) * dq].astype(o_ref.dtype)

    yspec = pl.BlockSpec((tk, w), lambda i, k: (k, 0))
    return pl.pallas_call(
        body, name=name, grid=(3, nk),
        in_specs=[yspec, yspec, yspec, pl.BlockSpec((tk, D), lambda i, k: (k, i))],
        out_specs=pl.BlockSpec((ns, None, w, dq), lambda i, k: (0, i, 0, 0)),
        out_shape=_sds((ns, 3, w, dq), BF),
        scratch_shapes=[pltpu.VMEM((w, D), F32)],
        compiler_params=_cp((PAR, ARB)),
    )(*ys, dp)


def _tn_rows(h, dbig, ns, tk, n_mats, name):
    T, D = h.shape
    dq = D // ns
    nk = T // tk

    def body(a_ref, b_ref, o_ref, acc):
        k = pl.program_id(1)
        prod = lax.dot_general(a_ref[...], b_ref[...], TN_DIMS, preferred_element_type=F32)

        @pl.when(k == 0)
        def _():
            acc[...] = prod

        @pl.when(k > 0)
        def _():
            acc[...] += prod

        @pl.when(k == nk - 1)
        def _():
            for s in range(ns):
                o_ref[s] = acc[s * dq:(s + 1) * dq, :].astype(o_ref.dtype)

    return pl.pallas_call(
        body, name=name, grid=(n_mats, nk),
        in_specs=[pl.BlockSpec((tk, D), lambda q, k: (k, 0)), pl.BlockSpec((tk, D), lambda q, k: (k, q))],
        out_specs=pl.BlockSpec((ns, None, dq, D), lambda q, k: (0, q, 0, 0)),
        out_shape=_sds((ns, n_mats, dq, D), BF),
        scratch_shapes=[pltpu.VMEM((D, D), F32)],
        compiler_params=_cp((PAR, ARB)),
    )(h, dbig)


def _inproj_fwd(x, nw, wbig, name):
    T, D = x.shape
    nb = wbig.shape[-1]
    tm = min(2 * TM, T)
    bn = min(2048, nb)

    def body(x_ref, nw_ref, w_ref, o_ref, h_ref, h_s):
        @pl.when(pl.program_id(1) == 0)
        def _():
            xv = x_ref[...]
            hb = (xv * _rms_r(xv) * nw_ref[...]).astype(BF)
            h_s[...] = hb
            h_ref[...] = hb

        o_ref[...] = jnp.dot(h_s[...], w_ref[...], preferred_element_type=F32).astype(BF)

    return pl.pallas_call(
        body, name=name, grid=(T // tm, nb // bn),
        in_specs=[pl.BlockSpec((tm, D), lambda i, n: (i, 0)),
                  pl.BlockSpec((1, D), lambda i, n: (0, 0)),
                  pl.BlockSpec((D, bn), lambda i, n: (0, n))],
        out_specs=[pl.BlockSpec((tm, bn), lambda i, n: (i, n)),
                   pl.BlockSpec((tm, D), lambda i, n: (i, 0))],
        out_shape=[_sds((T, nb), BF), _sds((T, D), BF)],
        scratch_shapes=[pltpu.VMEM((tm, D), BF)],
        compiler_params=_cp((PAR, ARB)),
    )(x, nw, wbig)


def _inproj_bwd(dbig, wbig, x, nw, dxin, name, comm=None):
    T, D = x.shape
    nb = wbig.shape[-1]
    tm = min(TM_FFN, T)
    tk = min(2048, nb)
    nk = nb // tk

    def body(a_ref, w_ref, x_ref, nw_ref, dxin_ref, dx_ref, dnw_ref, acc_s):
        i = pl.program_id(0)
        k = pl.program_id(1)
        prod = lax.dot_general(a_ref[...], w_ref[...], NT_DIMS, preferred_element_type=F32)

        @pl.when((i == 0) & (k == 0))
        def _():
            dnw_ref[...] = jnp.zeros_like(dnw_ref)

        @pl.when(k == 0)
        def _():
            acc_s[...] = prod

        @pl.when(k > 0)
        def _():
            acc_s[...] += prod

        @pl.when(k == nk - 1)
        def _():
            dx, dn = _rms_bwd(acc_s[...], x_ref[...], nw_ref[...])
            dx_ref[...] = dxin_ref[...] + dx
            dnw_ref[...] += dn

    tok = pl.BlockSpec((tm, D), lambda i, k: (i, 0))
    row = pl.BlockSpec((1, D), lambda i, k: (0, 0))
    return _call(
        body, name=name, grid=(T // tm, nk), args=(dbig, wbig, x, nw, dxin), comm=comm, vmem_mb=56,
        in_specs=[pl.BlockSpec((tm, tk), lambda i, k: (i, k)),
                  pl.BlockSpec((D, tk), lambda i, k: (0, k)),
                  tok, row, tok],
        out_specs=[tok, row],
        out_shape=[_sds((T, D), F32), _sds((1, D), F32)],
        scratch_shapes=[pltpu.VMEM((tm, D), F32)],
        sem=(ARB, ARB))


CONV_R = 512
CONV_BASE, CONV_GROUP = 0, 3
ATT_BASE, ATT_GROUP = 12, 3
RET_BASE, RET_GROUP = 24, 4
N_SEG = 10


N_IN_BLOCKS = N_SEG * BRANCH_W // LANE


def _orig_block(p):
    nblk = BRANCH_W // LANE
    qa, qr = p - ATT_BASE, p - RET_BASE
    conv = (p % CONV_GROUP) * nblk + p // CONV_GROUP
    att = (7 + qa % ATT_GROUP) * nblk + qa // ATT_GROUP
    ret = (3 + qr % RET_GROUP) * nblk + qr // RET_GROUP
    return jnp.where(p < ATT_BASE, conv, jnp.where(p < RET_BASE, att, ret))


def _copy_blocks(src, in_spec, out_shape, out_spec, grid, name, prev=None):
    def body(*refs):
        refs[-1][...] = refs[0][...]

    in_specs, args, aliases = [in_spec], [src], {}
    if prev is not None:
        in_specs.append(pl.BlockSpec(memory_space=pl.ANY))
        args.append(prev)
        aliases = {1: 0}
    return pl.pallas_call(
        body, name=name, grid=grid, in_specs=in_specs, out_specs=out_spec, out_shape=out_shape,
        input_output_aliases=aliases, compiler_params=_cp(tuple(PAR for _ in grid)),
    )(*args)


def _build_wbig(gates4, win4, name):
    ns, _, dq, D = gates4.shape
    per = win4.shape[-1] // LANE
    shape = _sds((D, 3 * D + N_IN_BLOCKS * LANE), gates4.dtype)
    out = _copy_blocks(gates4, pl.BlockSpec((None, None, dq, D), lambda s, i: (s, i, 0, 0)), shape,
                       pl.BlockSpec((dq, D), lambda s, i: (s, i)), (ns, 3), name + "_gates")
    return _copy_blocks(
        win4, pl.BlockSpec((None, D, LANE), lambda p: (_orig_block(p) // per, 0, _orig_block(p) % per)), shape,
        pl.BlockSpec((D, LANE), lambda p: (0, 3 * D // LANE + p)), (N_IN_BLOCKS,), name + "_in", prev=out)


def _ungroup_dw_in(dwp, ns, name):
    D = dwp.shape[0]
    per = N_IN_BLOCKS // ns
    return _copy_blocks(
        dwp, pl.BlockSpec((D, LANE), lambda p: (0, p)), _sds((ns, D, per * LANE), dwp.dtype),
        pl.BlockSpec((None, D, LANE), lambda p: (_orig_block(p) // per, 0, _orig_block(p) % per)), (N_IN_BLOCKS,), name)


def _seg0(big):
    return (big.shape[1] - N_SEG * BRANCH_W) // LANE


def _group_spec(big, base, group, rows, where):
    first = (_seg0(big) + base) // group
    assert first * group == _seg0(big) + base

    def index(*ids):
        r, g = where(*ids)
        return r, first + g

    return pl.BlockSpec((rows, group * LANE), index)


CU, CB, CC = (slice(k * LANE, (k + 1) * LANE) for k in range(3))
AQ, AK, AV = CU, CB, CC
RQ, RK, RV, RG = (slice(k * LANE, (k + 1) * LANE) for k in range(4))


def _conv_fwd(big, cw, name):
    T = big.shape[0]
    R = min(CONV_R, T)

    def body(g_ref, w_ref, y_ref, z_s):
        z_s[pl.ds(0, 8), :] = jnp.zeros((8, LANE), F32)

        def fill(t, c):
            sl = pl.ds(pl.multiple_of(t * R, R), R)
            z_s[pl.ds(pl.multiple_of(t * R + 8, 8), R), :] = g_ref[sl, CC].astype(F32) * g_ref[sl, CU].astype(F32)
            return c

        lax.fori_loop(0, T // R, fill, 0)
        w0, w1, w2 = w_ref[0:1, :], w_ref[1:2, :], w_ref[2:3, :]

        def step(t, c):
            zz = z_s[pl.ds(pl.multiple_of(t * R, R), R + 8), :]
            z0 = zz[8:]
            z1 = pltpu.roll(zz, 1, 0)[8:]
            z2 = pltpu.roll(zz, 2, 0)[8:]
            sl = pl.ds(pl.multiple_of(t * R, R), R)
            y_ref[sl, :] = (g_ref[sl, CB].astype(F32) * (w2 * z0 + w1 * z1 + w0 * z2)).astype(BF)
            return c

        lax.fori_loop(0, T // R, step, 0)

    return pl.pallas_call(
        body, name=name, grid=(BRANCH_W // LANE,),
        in_specs=[_group_spec(big, CONV_BASE, CONV_GROUP, T, lambda j: (0, j)),
                  pl.BlockSpec((3, LANE), lambda j: (0, j))],
        out_specs=pl.BlockSpec((T, LANE), lambda j: (0, j)),
        out_shape=_sds((T, BRANCH_W), BF),
        scratch_shapes=[pltpu.VMEM((T + 8, LANE), F32)],
        compiler_params=_cp((PAR,)),
    )(big, cw)


def _conv_bwd(big, dy, cw, dbig, name):
    T = big.shape[0]
    R = min(CONV_R, T)

    def body(g_ref, dy_ref, w_ref, _, o_ref, dw_ref, z_s, d_s):
        z_s[pl.ds(0, 8), :] = jnp.zeros((8, LANE), F32)
        d_s[pl.ds(T, 8), :] = jnp.zeros((8, LANE), F32)

        def fill(t, c):
            sl = pl.ds(pl.multiple_of(t * R, R), R)
            z_s[pl.ds(pl.multiple_of(t * R + 8, 8), R), :] = g_ref[sl, CC].astype(F32) * g_ref[sl, CU].astype(F32)
            d_s[sl, :] = dy_ref[sl, :].astype(F32) * g_ref[sl, CB].astype(F32)
            return c

        lax.fori_loop(0, T // R, fill, 0)
        w0, w1, w2 = w_ref[0:1, :], w_ref[1:2, :], w_ref[2:3, :]

        def step(t, carry):
            a0, a1, a2 = carry
            zz = z_s[pl.ds(pl.multiple_of(t * R, R), R + 8), :]
            z0 = zz[8:]
            z1 = pltpu.roll(zz, 1, 0)[8:]
            z2 = pltpu.roll(zz, 2, 0)[8:]
            sl = pl.ds(pl.multiple_of(t * R, R), R)
            dyv = dy_ref[sl, :].astype(F32)
            o_ref[sl, CB] = (dyv * (w2 * z0 + w1 * z1 + w0 * z2)).astype(BF)
            dd = d_s[pl.ds(pl.multiple_of(t * R, R), R + 8), :]
            d0 = dd[:R]
            d1 = pltpu.roll(dd, R + 7, 0)[:R]
            d2 = pltpu.roll(dd, R + 6, 0)[:R]
            dz = w2 * d0 + w1 * d1 + w0 * d2
            o_ref[sl, CC] = (dz * g_ref[sl, CU].astype(F32)).astype(BF)
            o_ref[sl, CU] = (dz * g_ref[sl, CC].astype(F32)).astype(BF)
            a0 = a0 + jnp.sum(d0 * z2, axis=0, keepdims=True)
            a1 = a1 + jnp.sum(d0 * z1, axis=0, keepdims=True)
            a2 = a2 + jnp.sum(d0 * z0, axis=0, keepdims=True)
            return a0, a1, a2

        zero = jnp.zeros((1, LANE), F32)
        a0, a1, a2 = lax.fori_loop(0, T // R, step, (zero, zero, zero))
        dw_ref[0:1, :] = a0
        dw_ref[1:2, :] = a1
        dw_ref[2:3, :] = a2

    group = _group_spec(big, CONV_BASE, CONV_GROUP, T, lambda j: (0, j))
    w = pl.BlockSpec((3, LANE), lambda j: (0, j))
    return pl.pallas_call(
        body, name=name, grid=(BRANCH_W // LANE,),
        in_specs=[group, pl.BlockSpec((T, LANE), lambda j: (0, j)), w, pl.BlockSpec(memory_space=pl.ANY)],
        out_specs=[group, w],
        out_shape=[_sds(dbig.shape, BF), _sds((3, BRANCH_W), F32)],
        scratch_shapes=[pltpu.VMEM((T + 8, LANE), F32), pltpu.VMEM((T + 8, LANE), F32)],
        input_output_aliases={3: 0}, compiler_params=_cp((PAR,)),
    )(big, dy, cw, dbig)


def _ret_tables(T):
    L = min(RET_L, T)
    hh = jnp.arange(H_RET, dtype=F32)
    lg = jnp.log1p(-jnp.exp2(-5.0 - hh))
    n = jnp.arange(L, dtype=F32)
    a = jnp.exp(lg[:, None] * (n + 1.0))
    b = jnp.exp(lg[:, None] * (L - 1.0 - n))
    gl = jnp.exp(lg * L)
    ch = jnp.arange(L) // CHUNK
    m = jnp.exp(lg[:, None, None] * jnp.abs(n[:, None] - n[None, :])) * (ch[None, :] <= ch[:, None]).astype(F32)
    inv_freq = ROPE_BASE ** (-jnp.linspace(0.0, 1.0, DK_RET // 2, dtype=F32))
    ang = jnp.arange(T, dtype=F32)[:, None] * inv_freq[None, :]
    cos, sin = jnp.cos(ang), jnp.sin(ang)
    return dict(
        L=L, M=m,
        a=jnp.broadcast_to(a[:, :, None], (H_RET, L, DK_RET)),
        b=jnp.broadcast_to(b[:, :, None], (H_RET, L, DK_RET)),
        gl=jnp.broadcast_to(gl[:, None, None], (H_RET, 1, DK_RET)),
        cos=jnp.concatenate([cos, cos], axis=-1), sin=jnp.concatenate([-sin, sin], axis=-1))


def _rot(x, cs, sn):
    return x * cs + pltpu.roll(x, DK_RET // 2, 1) * sn


def _unrot(dy, cs, sn):
    return dy * cs + pltpu.roll(dy * sn, DK_RET // 2, 1)


def _ret_fwd(big, tb, name, comm=None):
    T = big.shape[0]
    L = tb["L"]
    nsc = T // L
    scale = DK_RET ** -0.5

    def body(x_ref, cos_ref, sin_ref, m_ref, a_ref, b_ref, gl_ref, y_ref, o_ref, st_ref, s_s):
        @pl.when(pl.program_id(1) == 0)
        def _():
            s_s[...] = jnp.zeros_like(s_s)

        cs, sn = cos_ref[...], sin_ref[...]
        qt = _rot(x_ref[:, RQ].astype(F32), cs, sn) * scale
        kt = _rot(x_ref[:, RK].astype(F32), cs, sn)
        qb, kb, vb = qt.astype(BF), kt.astype(BF), x_ref[:, RV]
        s_prev = s_s[...]
        st_ref[...] = s_prev
        p = lax.dot_general(qb, kb, NT_DIMS, preferred_element_type=F32) * m_ref[...]
        o = (jnp.dot(p.astype(BF), vb, preferred_element_type=F32)
             + jnp.dot((qt * a_ref[...]).astype(BF), s_prev.astype(BF), preferred_element_type=F32))
        s_s[...] = s_prev * gl_ref[...] + lax.dot_general((kt * b_ref[...]).astype(BF), vb, TN_DIMS,
                                                         preferred_element_type=F32)
        o_ref[...] = o
        gv = x_ref[:, RG].astype(F32)
        y_ref[...] = (gv * _sigmoid(gv) * o * _rms_r(o)).astype(BF)

    tab = pl.BlockSpec((L, DK_RET), lambda h, i: (i, 0))
    per_head = pl.BlockSpec((None, L, DK_RET), lambda h, i: (h, 0, 0))
    out = pl.BlockSpec((L, LANE), lambda h, i: (i, h))
    return _call(
        body, name=name, grid=(H_RET, nsc), comm=comm,
        args=(big, tb["cos"], tb["sin"], tb["M"], tb["a"], tb["b"], tb["gl"]),
        in_specs=[_group_spec(big, RET_BASE, RET_GROUP, L, lambda h, i: (i, h)), tab, tab,
                  pl.BlockSpec((None, L, L), lambda h, i: (h, 0, 0)), per_head, per_head,
                  pl.BlockSpec((None, 1, DK_RET), lambda h, i: (h, 0, 0))],
        out_specs=[out, out, pl.BlockSpec((None, None, DK_RET, DK_RET), lambda h, i: (i, h, 0, 0))],
        out_shape=[_sds((T, BRANCH_W), BF), _sds((T, BRANCH_W), F32), _sds((nsc, H_RET, DK_RET, DK_RET), F32)],
        scratch_shapes=[pltpu.VMEM((DK_RET, DK_RET), F32)],
        sem=(ARB, ARB))


def _ret_bwd(big, o, st, dy, tb, dbig, name):
    T = big.shape[0]
    L = tb["L"]
    nsc = T // L
    scale = DK_RET ** -0.5

    def body(x_ref, cos_ref, sin_ref, m_ref, a_ref, b_ref, gl_ref, o_ref, st_ref, dy_ref, _, d_ref, ds_s):
        @pl.when(pl.program_id(1) == 0)
        def _():
            ds_s[...] = jnp.zeros_like(ds_s)

        cs, sn = cos_ref[...], sin_ref[...]
        mm, av, bv = m_ref[...], a_ref[...], b_ref[...]
        qt = _rot(x_ref[:, RQ].astype(F32), cs, sn) * scale
        kt = _rot(x_ref[:, RK].astype(F32), cs, sn)
        qb, kb, vb = qt.astype(BF), kt.astype(BF), x_ref[:, RV]
        pb = (lax.dot_general(qb, kb, NT_DIMS, preferred_element_type=F32) * mm).astype(BF)
        ov = o_ref[...]
        r = _rms_r(ov)
        oh = ov * r
        gv = x_ref[:, RG].astype(F32)
        sg = _sigmoid(gv)
        dyv = dy_ref[...].astype(F32)
        d_ref[:, RG] = (dyv * oh * (sg * (1.0 + gv * (1.0 - sg)))).astype(BF)
        doh = dyv * gv * sg
        dob = (r * (doh - oh * jnp.mean(doh * oh, axis=-1, keepdims=True))).astype(BF)
        dsb = ds_s[...].astype(BF)
        spb = st_ref[...].astype(BF)
        dpb = (lax.dot_general(dob, vb, NT_DIMS, preferred_element_type=F32) * mm).astype(BF)
        dqt = (jnp.dot(dpb, kb, preferred_element_type=F32)
               + lax.dot_general(dob, spb, NT_DIMS, preferred_element_type=F32) * av)
        dkt = (lax.dot_general(dpb, qb, TN_DIMS, preferred_element_type=F32)
               + lax.dot_general(vb, dsb, NT_DIMS, preferred_element_type=F32) * bv)
        dv = (lax.dot_general(pb, dob, TN_DIMS, preferred_element_type=F32)
              + jnp.dot((kt * bv).astype(BF), dsb, preferred_element_type=F32))
        ds_s[...] = ds_s[...] * gl_ref[...] + lax.dot_general((qt * av).astype(BF), dob, TN_DIMS,
                                                              preferred_element_type=F32)
        d_ref[:, RQ] = (_unrot(dqt, cs, sn) * scale).astype(BF)
        d_ref[:, RK] = _unrot(dkt, cs, sn).astype(BF)
        d_ref[:, RV] = dv.astype(BF)

    def rev(i):
        return nsc - 1 - i

    group = _group_spec(big, RET_BASE, RET_GROUP, L, lambda h, i: (rev(i), h))
    tab = pl.BlockSpec((L, DK_RET), lambda h, i: (rev(i), 0))
    per_head = pl.BlockSpec((None, L, DK_RET), lambda h, i: (h, 0, 0))
    out = pl.BlockSpec((L, LANE), lambda h, i: (rev(i), h))
    return pl.pallas_call(
        body, name=name, grid=(H_RET, nsc),
        in_specs=[group, tab, tab,
                  pl.BlockSpec((None, L, L), lambda h, i: (h, 0, 0)), per_head, per_head,
                  pl.BlockSpec((None, 1, DK_RET), lambda h, i: (h, 0, 0)),
                  out, pl.BlockSpec((None, None, DK_RET, DK_RET), lambda h, i: (rev(i), h, 0, 0)), out,
                  pl.BlockSpec(memory_space=pl.ANY)],
        out_specs=group,
        out_shape=_sds(dbig.shape, BF),
        scratch_shapes=[pltpu.VMEM((DK_RET, DK_RET), F32)],
        input_output_aliases={10: 0}, compiler_params=_cp((PAR, ARB)),
    )(big, tb["cos"], tb["sin"], tb["M"], tb["a"], tb["b"], tb["gl"], o, st, dy, dbig)


def _relbias_onehot(n):
    mm = lax.broadcasted_iota(jnp.int32, (RB_PAD, ATT_TOEP), 1)
    rr = lax.broadcasted_iota(jnp.int32, (RB_PAD, ATT_TOEP), 0)
    idx = jnp.clip(n + ATT_TOEP - mm, 0, 2 * REL_CLIP)
    return (rr == idx).astype(F32)


def _split3(x):
    hi = x.astype(BF).astype(F32)
    mid = (x - hi).astype(BF).astype(F32)
    lo = x - hi - mid
    return jnp.concatenate([hi, mid, lo], axis=0).astype(BF)


def _join3(y):
    k = y.shape[0] // 3
    return (y[:k] + y[k:2 * k]) + y[2 * k:]


def _relbias_expand(rel_bias, name, comm=None):
    far = ATT_SPAN - ATT_TOEP
    n_layers = rel_bias.shape[0]
    rbp = jnp.pad(rel_bias, ((0, 0), (0, 0), (0, RB_PAD - N_REL)))

    def body(rb_ref, o_ref):
        for l in range(n_layers):
            rb = rb_ref[l]
            const = jnp.broadcast_to(rb[:, 2 * REL_CLIP:2 * REL_CLIP + 1], (H_ATT, far))
            rb3 = _split3(rb)

            def row(n, c):
                toep = _join3(jnp.dot(rb3, _relbias_onehot(n).astype(BF), preferred_element_type=F32))
                m = lax.broadcasted_iota(jnp.int32, (1, ATT_SPAN), 1)
                d = n // CHUNK + N_PREV - m // CHUNK
                neg = jnp.where((d >= 0) & (d <= N_PREV), 0.0, NEG_INF).astype(F32)
                o_ref[l, n] = jnp.concatenate([const, toep], axis=1) + neg
                return c

            lax.fori_loop(0, ATT_TQ, row, 0)

    (out,), extra = _call(
        body, name=name, args=(rbp,), comm=comm,
        in_specs=[pl.BlockSpec(memory_space=pltpu.VMEM)],
        out_specs=[pl.BlockSpec(memory_space=pltpu.VMEM)],
        out_shape=[_sds((n_layers, ATT_TQ, H_ATT, ATT_SPAN), F32)])
    return jnp.transpose(out, (0, 2, 1, 3)), extra


def _relbias_grad(dbt, name):
    far = ATT_SPAN - ATT_TOEP

    def body(d_ref, o_ref):
        def row(n, carry):
            acc, cs = carry
            dn = d_ref[n]
            acc = acc + _join3(lax.dot_general(_split3(dn[:, far:]), _relbias_onehot(n).astype(BF), NT_DIMS,
                                               preferred_element_type=F32))
            cs = cs + jnp.sum(dn[:, :far], axis=1, keepdims=True)
            return acc, cs

        acc, cs = lax.fori_loop(0, ATT_TQ, row, (jnp.zeros((H_ATT, RB_PAD), F32), jnp.zeros((H_ATT, 1), F32)))
        rr = lax.broadcasted_iota(jnp.int32, (H_ATT, RB_PAD), 1)
        o_ref[...] = acc + jnp.where(rr == 2 * REL_CLIP, cs, 0.0)

    return pl.pallas_call(
        body, name=name,
        in_specs=[pl.BlockSpec(memory_space=pltpu.VMEM)],
        out_specs=pl.BlockSpec(memory_space=pltpu.VMEM),
        out_shape=_sds((H_ATT, RB_PAD), F32),
    )(dbt)


def _att_pad_fill(dst_s, src_ref, cols, T):
    dst_s[pl.ds(0, ATT_PAD), :] = jnp.zeros((ATT_PAD, LANE), dst_s.dtype)
    R = min(512, T)

    def cp(t, c):
        dst_s[pl.ds(pl.multiple_of(ATT_PAD + t * R, LANE), R), :] = src_ref[pl.ds(pl.multiple_of(t * R, R), R), cols]
        return c

    lax.fori_loop(0, T // R, cp, 0)


ATT_WIN = ATT_SUB * ATT_TQ + ATT_PAD


def _att_probs(s_full, sub, bias, t0):
    s = s_full[sub * ATT_TQ:(sub + 1) * ATT_TQ, sub * ATT_TQ:sub * ATT_TQ + ATT_SPAN] * (DH_ATT ** -0.5) + bias
    key_pos = t0 + sub * ATT_TQ - ATT_PAD + lax.broadcasted_iota(jnp.int32, (1, ATT_SPAN), 1)
    s = jnp.where(key_pos >= 0, s, NEG_INF)
    p = jnp.exp(s - jnp.max(s, axis=-1, keepdims=True))
    return p * (1.0 / jnp.sum(p, axis=-1, keepdims=True))


def _att_band(tiles):
    rows = []
    for sub, t in enumerate(tiles):
        parts = []
        if sub:
            parts.append(jnp.zeros((ATT_TQ, sub * ATT_TQ), BF))
        parts.append(t)
        if sub < ATT_SUB - 1:
            parts.append(jnp.zeros((ATT_TQ, (ATT_SUB - 1 - sub) * ATT_TQ), BF))
        rows.append(jnp.concatenate(parts, axis=1))
    return jnp.concatenate(rows, axis=0)


def _att_head_masks(x):
    first = lax.broadcasted_iota(jnp.int32, (1, LANE), 1) < DH_ATT
    zero = jnp.zeros_like(x)
    return first, (jnp.where(first, x, zero), jnp.where(first, zero, x))


def _att_fwd(big, bias, name, comm=None):
    T = big.shape[0]
    rows = ATT_SUB * ATT_TQ
    nt = T // rows

    def body(x_ref, b_ref, y_ref, kp_s, vp_s):
        i = pl.program_id(1)

        @pl.when(i == 0)
        def _():
            _att_pad_fill(kp_s, x_ref, AK, T)
            _att_pad_fill(vp_s, x_ref, AV, T)

        t0 = pl.multiple_of(i * rows, rows)
        kw = kp_s[pl.ds(t0, ATT_WIN), :]
        vw = vp_s[pl.ds(t0, ATT_WIN), :]
        first, qm = _att_head_masks(x_ref[pl.ds(t0, rows), AQ])
        outs = []
        for hh in range(2):
            s_full = lax.dot_general(qm[hh], kw, NT_DIMS, preferred_element_type=F32)
            band = _att_band([_att_probs(s_full, sub, b_ref[hh], t0).astype(BF) for sub in range(ATT_SUB)])
            outs.append(jnp.dot(band, vw, preferred_element_type=F32))
        y_ref[...] = jnp.where(first, outs[0], outs[1]).astype(BF)

    return _call(
        body, name=name, grid=(H_ATT // 2, nt), args=(big, bias), comm=comm,
        in_specs=[_group_spec(big, ATT_BASE, ATT_GROUP, T, lambda p, i: (0, p)),
                  pl.BlockSpec((2, ATT_TQ, ATT_SPAN), lambda p, i: (p, 0, 0))],
        out_specs=[pl.BlockSpec((rows, LANE), lambda p, i: (i, p))],
        out_shape=[_sds((T, BRANCH_W), BF)],
        scratch_shapes=[pltpu.VMEM((T + ATT_PAD, LANE), BF), pltpu.VMEM((T + ATT_PAD, LANE), BF)],
        sem=(ARB, ARB))


def _att_bwd(big, bias, dy, dbig, name, comm=None):
    T = big.shape[0]
    rows = ATT_SUB * ATT_TQ
    nt = T // rows
    scale = DH_ATT ** -0.5

    def body(x_ref, b_ref, dy_ref, _, d_ref, db_ref, kp_s, vp_s, dk_s, dv_s):
        i = pl.program_id(1)

        @pl.when(i == 0)
        def _():
            _att_pad_fill(kp_s, x_ref, AK, T)
            _att_pad_fill(vp_s, x_ref, AV, T)
            dk_s[...] = jnp.zeros_like(dk_s)
            dv_s[...] = jnp.zeros_like(dv_s)
            db_ref[...] = jnp.zeros_like(db_ref)

        t0 = pl.multiple_of(i * rows, rows)
        win = pl.ds(t0, ATT_WIN)
        kw = kp_s[win, :]
        vw = vp_s[win, :]
        first, qm = _att_head_masks(x_ref[pl.ds(t0, rows), AQ])
        _, dom = _att_head_masks(dy_ref[...])
        dqs, dkt, dvt = [], None, None
        for hh in range(2):
            s_full = lax.dot_general(qm[hh], kw, NT_DIMS, preferred_element_type=F32)
            dp_full = lax.dot_general(dom[hh], vw, NT_DIMS, preferred_element_type=F32)
            ps, dss, db = [], [], None
            for sub in range(ATT_SUB):
                pn = _att_probs(s_full, sub, b_ref[hh], t0)
                dp = dp_full[sub * ATT_TQ:(sub + 1) * ATT_TQ, sub * ATT_TQ:sub * ATT_TQ + ATT_SPAN]
                ds = pn * (dp - jnp.sum(dp * pn, axis=-1, keepdims=True))
                db = ds if db is None else db + ds
                ps.append(pn.astype(BF))
                dss.append(ds.astype(BF))
            db_ref[hh] += db
            ds_band, p_band = _att_band(dss), _att_band(ps)
            dqs.append(jnp.dot(ds_band, kw, preferred_element_type=F32))
            qt = jnp.transpose(qm[hh].astype(F32)).astype(BF)
            dot_ = jnp.transpose(dom[hh].astype(F32)).astype(BF)
            dk_h = jnp.dot(qt, ds_band, preferred_element_type=F32)
            dv_h = jnp.dot(dot_, p_band, preferred_element_type=F32)
            dkt = dk_h if dkt is None else dkt + dk_h
            dvt = dv_h if dvt is None else dvt + dv_h
        d_ref[pl.ds(t0, rows), AQ] = (jnp.where(first, dqs[0], dqs[1]) * scale).astype(BF)
        dk_s[win, :] += jnp.transpose(dkt) * scale
        dv_s[win, :] += jnp.transpose(dvt)

        @pl.when(i == nt - 1)
        def _():
            R = min(512, T)

            def cp(t, c):
                src = pl.ds(pl.multiple_of(ATT_PAD + t * R, LANE), R)
                dst = pl.ds(pl.multiple_of(t * R, R), R)
                d_ref[dst, AK] = dk_s[src, :].astype(BF)
                d_ref[dst, AV] = dv_s[src, :].astype(BF)
                return c

            lax.fori_loop(0, T // R, cp, 0)

    group = _group_spec(big, ATT_BASE, ATT_GROUP, T, lambda p, i: (0, p))
    tile = pl.BlockSpec((rows, LANE), lambda p, i: (i, p))
    bspec = pl.BlockSpec((2, ATT_TQ, ATT_SPAN), lambda p, i: (p, 0, 0))
    return _call(
        body, name=name, grid=(H_ATT // 2, nt), args=(big, bias, dy, dbig), comm=comm, aliases={3: 0}, vmem_mb=56,
        in_specs=[group, bspec, tile, pl.BlockSpec(memory_space=pl.ANY)],
        out_specs=[group, bspec],
        out_shape=[_sds(dbig.shape, BF), _sds((H_ATT, ATT_TQ, ATT_SPAN), F32)],
        scratch_shapes=[pltpu.VMEM((T + ATT_PAD, LANE), BF), pltpu.VMEM((T + ATT_PAD, LANE), BF),
                        pltpu.VMEM((T + ATT_PAD, LANE), F32), pltpu.VMEM((T + ATT_PAD, LANE), F32)],
        sem=(ARB, ARB))


def _merge_fwd(x1, big, ys, wb, wo, name):
    T, D = x1.shape
    tm = min(TM, T)

    def body(x_ref, gp_ref, yc_ref, yr_ref, ya_ref, wb_ref, wo_ref, x2_ref, p_ref, mg_ref):
        merged = jnp.zeros((tm, D), F32)
        for i, y_ref in enumerate((yc_ref, yr_ref, ya_ref)):
            cols = slice(i * D, (i + 1) * D)
            pb = jnp.dot(y_ref[...], wb_ref[i], preferred_element_type=F32).astype(BF)
            p_ref[:, cols] = pb
            merged = merged + _sigmoid(gp_ref[:, cols].astype(F32)) * pb.astype(F32)
        mb = merged.astype(BF)
        mg_ref[...] = mb
        x2_ref[...] = x_ref[...] + jnp.dot(mb, wo_ref[...], preferred_element_type=F32)

    tok = pl.BlockSpec((tm, D), lambda i: (i, 0))
    wide = pl.BlockSpec((tm, 3 * D), lambda i: (i, 0))
    yspec = pl.BlockSpec((tm, BRANCH_W), lambda i: (i, 0))
    return pl.pallas_call(
        body, name=name, grid=(T // tm,),
        in_specs=[tok, wide, yspec, yspec, yspec,
                  pl.BlockSpec((3, BRANCH_W, D), lambda i: (0, 0, 0)),
                  pl.BlockSpec((D, D), lambda i: (0, 0))],
        out_specs=[tok, wide, tok],
        out_shape=[_sds((T, D), F32), _sds((T, 3 * D), BF), _sds((T, D), BF)],
        compiler_params=_cp((PAR,)),
    )(x1, big, *ys, wb, wo)


def _merge_bwd(dx2, big, p, wb, wo, name):
    T, D = dx2.shape
    tm = min(TM, T)

    def body(dx_ref, gp_ref, p_ref, wb_ref, wo_ref, dp_ref, dgp_ref, dyc_ref, dyr_ref, dya_ref, dxb_ref):
        dxb = dx_ref[...].astype(BF)
        dxb_ref[...] = dxb
        dm = lax.dot_general(dxb, wo_ref[...], NT_DIMS, preferred_element_type=F32).astype(BF)
        for i, dy_ref in enumerate((dyc_ref, dyr_ref, dya_ref)):
            cols = slice(i * D, (i + 1) * D)
            gt = _sigmoid(gp_ref[:, cols].astype(F32)).astype(BF)
            dpb = dm * gt
            dp_ref[:, cols] = dpb
            dgp_ref[:, cols] = (dm * p_ref[:, cols]) * (gt * (1.0 - gt))
            dy_ref[...] = lax.dot_general(dpb, wb_ref[i], NT_DIMS, preferred_element_type=F32).astype(BF)

    tok = pl.BlockSpec((tm, D), lambda i: (i, 0))
    wide = pl.BlockSpec((tm, 3 * D), lambda i: (i, 0))
    yspec = pl.BlockSpec((tm, BRANCH_W), lambda i: (i, 0))
    return pl.pallas_call(
        body, name=name, grid=(T // tm,),
        in_specs=[tok, wide, wide,
                  pl.BlockSpec((3, BRANCH_W, D), lambda i: (0, 0, 0)),
                  pl.BlockSpec((D, D), lambda i: (0, 0))],
        out_specs=[wide, wide, yspec, yspec, yspec, tok],
        out_shape=[_sds((T, 3 * D), BF), _sds(big.shape, BF)] + [_sds((T, BRANCH_W), BF)] * 3 + [_sds((T, D), BF)],
        compiler_params=_cp((PAR,)),
    )(dx2, big, p, wb, wo)


def _loss_head(x, tgt, fw, name):
    T, D = x.shape
    tm = min(TM, T)

    def body(x_ref, t_ref, w_ref, loss_ref, dx_ref, dw_ref):
        @pl.when(pl.program_id(0) == 0)
        def _():
            loss_ref[...] = jnp.zeros_like(loss_ref)
            dw_ref[...] = jnp.zeros_like(dw_ref)

        xv = x_ref[...]
        wv = w_ref[...]
        e = xv * _rms_r(xv) * wv - t_ref[...]
        loss_ref[...] += 0.5 * jnp.sum(jnp.mean(e * e, axis=-1, keepdims=True))
        dx, dn = _rms_bwd(e * (1.0 / D), xv, wv)
        dx_ref[...] = dx
        dw_ref[...] += dn

    tok = pl.BlockSpec((tm, D), lambda i: (i, 0))
    return pl.pallas_call(
        body, name=name, grid=(T // tm,),
        in_specs=[tok, tok, pl.BlockSpec((1, D), lambda i: (0, 0))],
        out_specs=[pl.BlockSpec((8, LANE), lambda i: (0, 0)), tok, pl.BlockSpec((1, D), lambda i: (0, 0))],
        out_shape=[_sds((8, LANE), F32), _sds((T, D), F32), _sds((1, D), F32)],
        compiler_params=_cp((ARB,)),
    )(x, tgt, fw)


def _block_rows(rows, cols):
    cap = max(8, (1 << 18) // cols)
    best = None
    for r in range(8, rows + 1, 8):
        if rows % r == 0 and r <= cap:
            best = r
    return best if best is not None else rows


def _sum8(l_ref):
    def four(base):
        return ((l_ref[base + 3].astype(F32) + l_ref[base].astype(F32)) + l_ref[base + 1].astype(F32)
                ) + l_ref[base + 2].astype(F32)

    return four(0) + four(4)


def _sum_adamw(lands, w, m, v, name):
    _, rows, cols = lands[0].shape
    br = _block_rows(rows, cols)
    nb = rows // br
    n_layers = len(lands)

    def body(*refs):
        l_refs = refs[:n_layers]
        w_ref, m_ref, v_ref, g_ref, d_ref, nm_ref, nv_ref = refs[n_layers:]
        i = pl.program_id(0)
        for l, l_ref in enumerate(l_refs):
            @pl.when((i >= l * nb) & (i < (l + 1) * nb))
            def _():
                g = _sum8(l_ref)
                d, nm, nv = _adamw_math(w_ref[...], g, m_ref[...], v_ref[...])
                g_ref[...] = g
                d_ref[...] = d
                nm_ref[...] = nm
                nv_ref[...] = nv

    def land_spec(l):
        return pl.BlockSpec((2 * N_SHARD, br, cols), lambda i: (0, jnp.clip(i - l * nb, 0, nb - 1), 0))

    blk = pl.BlockSpec((br, cols), lambda i: (i, 0))
    return pl.pallas_call(
        body, name=name, grid=(n_layers * nb,),
        in_specs=[land_spec(l) for l in range(n_layers)] + [blk] * 3, out_specs=[blk] * 4,
        out_shape=[_sds((n_layers * rows, cols), F32)] * 4,
        compiler_params=_cp((PAR,)),
    )(*lands, w, m, v)


def _adamw_math(w, g, m, v):
    m = ADAM_B1 * m + (1.0 - ADAM_B1) * g
    v = ADAM_B2 * v + (1.0 - ADAM_B2) * (g * g)
    m_hat = m / (1.0 - ADAM_B1 ** ADAM_STEP)
    v_hat = v / (1.0 - ADAM_B2 ** ADAM_STEP)
    delta = -ADAM_LR * (m_hat / (jnp.sqrt(v_hat) + ADAM_EPS) + ADAM_WD * w)
    return delta, m, v


def _adamw(w, g, m, v, name):
    rows, cols = w.shape
    br = _block_rows(rows, cols)

    def body(w_ref, g_ref, m_ref, v_ref, d_ref, nm_ref, nv_ref):
        d, nm, nv = _adamw_math(w_ref[...], g_ref[...], m_ref[...], v_ref[...])
        d_ref[...] = d
        nm_ref[...] = nm
        nv_ref[...] = nv

    blk = pl.BlockSpec((br, cols), lambda i: (i, 0))
    return pl.pallas_call(
        body, name=name, grid=(rows // br,),
        in_specs=[blk] * 4, out_specs=[blk] * 3,
        out_shape=[_sds((rows, cols), F32)] * 3,
        compiler_params=_cp((PAR,)),
    )(w, g, m, v)


def _allreduce_small(v, name):
    rows = v.shape[0]
    flips = [(fx, fy, fc) for fx in (0, 1) for fy in (0, 1) for fc in (0, 1) if fx or fy or fc]

    def body(v_ref, o_ref, all_s, ssem, rsem):
        x, y, c = _place()

        def peer(f):
            return (x + f[0] - 2 * x * f[0], y + f[1] - 2 * y * f[1], c + f[2] - 2 * c * f[2])

        def slot(p):
            return all_s.at[4 * p[0] + 2 * p[1] + p[2]]

        def copy(k, f, owner):
            return pltpu.make_async_remote_copy(
                src_ref=v_ref, dst_ref=slot(owner), send_sem=ssem.at[k], recv_sem=rsem.at[k],
                device_id=peer(f), device_id_type=MESH)

        sends = [copy(k, f, (x, y, c)) for k, f in enumerate(flips)]
        for cp in sends:
            cp.start()
        all_s[4 * x + 2 * y + c] = v_ref[...]
        for k, f in enumerate(flips):
            copy(k, f, peer(f)).wait_recv()
        for cp in sends:
            cp.wait_send()
        acc = all_s[0]
        for d in range(1, 8):
            acc = acc + all_s[d]
        o_ref[...] = acc

    return pl.pallas_call(
        body, name=name,
        in_specs=[pl.BlockSpec(memory_space=pltpu.VMEM)],
        out_specs=pl.BlockSpec(memory_space=pltpu.VMEM),
        out_shape=_sds((rows, LANE), F32),
        scratch_shapes=[pltpu.VMEM((8, rows, LANE), F32), pltpu.SemaphoreType.DMA((7,)), pltpu.SemaphoreType.DMA((7,))],
    )(v)


BIG_NAMES = ("ffn1_w_gate", "ffn1_w_up", "ffn1_w_down", "w_in", "w_branch", "w_merge_gate", "w_out",
             "ffn2_w_gate", "ffn2_w_up", "ffn2_w_down")


FFN1 = ("ffn1_w_gate", "ffn1_w_up", "ffn1_w_down")
FFN2 = ("ffn2_w_gate", "ffn2_w_up", "ffn2_w_down")
MIX_IN = ("w_in", "w_merge_gate")
MIX_OUT = ("w_branch", "w_out")


def _keys(names, l):
    return [(n, l) for n in names]


def _local_step(x, tgt, small, convw_full, biases, wx, n_layers):
    T, D = x.shape
    L = n_layers
    ns = N_SHARD
    dq = D // ns
    W = wx.w

    def hosted(call, keys, scatter=False):
        comm = wx.pieces(keys, scatter)
        main, extra = call(comm)
        if comm is not None:
            wx.arrived(keys, extra, scatter)
        return main

    def mixer_views(l):
        return _build_wbig(W[("w_merge_gate", l)], W[("w_in", l)], f"wbig_{l}")

    def out_views(l):
        wb4 = W[("w_branch", l)]
        wb = _copy_blocks(wb4, pl.BlockSpec((None, None, BRANCH_W, dq), lambda s_, i: (s_, i, 0, 0)),
                          _sds((3, BRANCH_W, D), wb4.dtype),
                          pl.BlockSpec((None, BRANCH_W, dq), lambda s_, i: (i, 0, s_)), (ns, 3), f"w_branch_whole_{l}")
        wo = W[("w_out", l)].reshape(D, D)
        return wb, wo

    tb = _ret_tables(T)

    saved = []
    h = x
    for l in range(L):
        s = {"x0": h}
        nxt = l + 1
        x1, s["g1"], s["u1"] = hosted(
            lambda c: _ffn_fwd(h, small["ffn1_norm"][l][None], W[("ffn1_w_gate", l)], W[("ffn1_w_up", l)],
                               W[("ffn1_w_down", l)], f"ffn1_fwd_{l}", comm=c), _keys(MIX_IN + MIX_OUT, l))
        s["x1"] = x1
        s["wbig"] = mixer_views(l)
        big, s["h"] = _inproj_fwd(x1, small["mix_norm"][l][None], s["wbig"], f"inproj_fwd_{l}")
        s["big"] = big
        s["bias"] = biases[l]
        s["yc"] = _conv_fwd(big, convw_full[l], f"conv_fwd_{l}")
        s["yr"], s["o"], s["st"] = hosted(lambda c: _ret_fwd(big, tb, f"ret_fwd_{l}", comm=c), [])
        (s["ya"],) = hosted(lambda c: _att_fwd(big, s["bias"], f"att_fwd_{l}", comm=c), _keys(FFN2, l))
        s["wb"], s["wo"] = out_views(l)
        x2, s["p"], s["mg"] = _merge_fwd(x1, big, (s["yc"], s["yr"], s["ya"]), s["wb"], s["wo"], f"merge_fwd_{l}")
        s["x2"] = x2
        h, s["g2"], s["u2"] = hosted(
            lambda c: _ffn_fwd(x2, small["ffn2_norm"][l][None], W[("ffn2_w_gate", l)], W[("ffn2_w_up", l)],
                               W[("ffn2_w_down", l)], f"ffn2_fwd_{l}", comm=c), _keys(FFN1, nxt) if nxt < L else [])
        saved.append(s)

    loss_p, dx, d_final = _loss_head(h, tgt, small["final_norm"][None], "loss_head")

    gs = {"final_norm": d_final[0]}
    for k in ("ffn1_norm", "mix_norm", "ffn2_norm", "rel_bias", "conv_w"):
        gs[k] = [None] * L
    tk = min(2048, T)
    nk = T // tk

    def ffn_back(pre, l, dxo, x_in, g, u, first_keys, second_keys, between=None):
        nw = small[pre + "_norm"][l][None]
        dgv, duv, av, hb, dacc = hosted(
            lambda c: _ffn_bwd_hidden(dxo, x_in, nw, g, u, W[(pre + "_w_down", l)], f"{pre}_bwd_hidden_{l}", comm=c),
            first_keys, scatter=True)
        parts = (hb, dgv, duv, av, dacc)
        if between is not None:
            second_keys = between(parts)
        dxn, dn = hosted(
            lambda c: _ffn_bwd_resid(dgv, duv, W[(pre + "_w_gate", l)], W[(pre + "_w_up", l)], x_in, nw, dxo,
                                     f"{pre}_bwd_resid_{l}", comm=c),
            second_keys, scatter=True)
        gs[pre + "_norm"][l] = dn[0]
        return dxn, parts

    def ffn_grads(pre, l, hb, dgv, duv, av, dacc, chain=False):
        fs = dgv.shape[-1]
        tkf = min(2 * tk, T)
        hspec = pl.BlockSpec((tkf, D), lambda p, q, k: (k, 0))
        sspec = pl.BlockSpec((None, tkf, fs), lambda p, q, k: (p, k, 0))
        down_spec = pl.BlockSpec((None, fs, D), lambda p, q, k: (p, 0, 0))
        jobs = [(pre + "_w_gate", dgv, hb, sspec, hspec, (ns, fs, D), down_spec),
                (pre + "_w_up", duv, hb, sspec, hspec, (ns, fs, D), down_spec),
                (pre + "_w_down", av, dacc, sspec, hspec, (ns, fs, D), down_spec)]
        for idx, (nm, a, b, a_spec, b_spec, shape, o_spec) in enumerate(jobs):
            def product(c):
                r = _tn(a, b, a_spec, b_spec, _sds(shape, BF), o_spec, (ns, 1, T // tkf), f"d{nm}_{l}", comm=c)
                return (r, []) if c is None else r
            keys = [(jobs[0][0], l)] if chain and idx == 2 else []
            wx.g[(nm, l)] = hosted(product, keys, scatter=True)
        return [(jobs[1][0], l), (jobs[2][0], l)] if chain else []

    for l in reversed(range(L)):
        s = saved[l]
        above = _keys(FFN1, l + 1) if l + 1 < L else []
        dx, parts = ffn_back("ffn2", l, dx, s["x2"], s["g2"], s["u2"], above[:1], above[1:])
        ffn_grads("ffn2", l, *parts)
        dp, dbig, dyc, dyr, dya, dxb = _merge_bwd(dx, s["big"], s["p"], s["wb"], s["wo"], f"merge_bwd_{l}")
        wx.g[("w_out", l)] = _tn_rows(s["mg"], dxb, ns, tk, 1, f"dw_out_{l}").reshape(ns, dq, D)
        wx.g[("w_branch", l)] = _tn_branches((s["yc"], s["yr"], s["ya"]), dp, ns, tk, f"dw_branch_{l}")
        dbig, dcw = _conv_bwd(s["big"], dyc, convw_full[l], dbig, f"conv_bwd_{l}")
        gs["conv_w"][l] = dcw
        dbig = _ret_bwd(s["big"], s["o"], s["st"], dyr, tb, dbig, f"ret_bwd_{l}")
        dbig, dbias = hosted(lambda c: _att_bwd(s["big"], s["bias"], dya, dbig, f"att_bwd_{l}", comm=c),
                             _keys(FFN2, l), scatter=True)
        gs["rel_bias"][l] = _relbias_grad(jnp.transpose(dbias, (1, 0, 2)), f"relbias_grad_{l}")[:, :N_REL]
        n_in = N_SEG * BRANCH_W
        bn = 1024 if (3 * D) % 1024 == 0 else BRANCH_W
        dwp = _tn(s["h"], dbig, pl.BlockSpec((tk, D), lambda p, q, k: (k, 0)),
                  pl.BlockSpec((tk, bn), lambda p, q, k: (k, 3 * D // bn + q)),
                  _sds((D, n_in), BF), pl.BlockSpec((D, bn), lambda p, q, k: (0, q)), (1, n_in // bn, nk), f"dw_in_{l}")
        wx.g[("w_in", l)] = _ungroup_dw_in(dwp, ns, f"dw_in_shards_{l}")
        wx.g[("w_merge_gate", l)] = _tn_rows(s["h"], dbig, ns, tk, 3, f"dw_merge_gate_{l}")
        dx, dn = hosted(
            lambda c: _inproj_bwd(dbig, s["wbig"], s["x1"], small["mix_norm"][l][None], dx, f"inproj_bwd_{l}", comm=c),
            [("w_in", l)], scatter=True)
        gs["mix_norm"][l] = dn[0]
        rest = [("w_merge_gate", l), ("w_branch", l), ("w_out", l)]
        if l == 0:
            dx, _ = ffn_back("ffn1", l, dx, s["x0"], s["g1"], s["u1"], rest, [],
                             between=lambda parts: ffn_grads("ffn1", 0, *parts, chain=True))
        else:
            dx, parts = ffn_back("ffn1", l, dx, s["x0"], s["g1"], s["u1"], rest, [])
            ffn_grads("ffn1", l, *parts)

    for k in ("ffn1_norm", "mix_norm", "ffn2_norm", "rel_bias", "conv_w"):
        gs[k] = jnp.stack(gs[k])
    return loss_p, dx, gs


class _Exchange:
    def __init__(self, shards):
        self.shards = shards
        self.w = {}
        self.g = {}
        self.landed = {}

    def own(self, key):
        return self.shards[key[0]][key[1]].astype(BF)

    def pieces(self, keys, scatter):
        if not keys:
            return None
        if scatter:
            return _Scatter([self.g[k] for k in keys])
        return _HalfGather([_halves(self.own(k)) for k in keys])

    def arrived(self, keys, outs, scatter):
        for k, o in zip(keys, outs):
            if scatter:
                self.landed[k] = o
            else:
                self.w[k] = o.reshape((N_SHARD,) + self.shards[k[0]].shape[1:])


def _halves(a):
    return a.reshape(2, -1, a.shape[-1])


TRANSPOSED_GRADS = ("ffn1_w_gate", "ffn1_w_up", "ffn2_w_gate", "ffn2_w_up")
W_NAMES = ("ffn1_norm", "ffn1_w_gate", "ffn1_w_up", "ffn1_w_down", "mix_norm", "w_in", "conv_w", "rel_bias", "w_branch",
           "w_merge_gate", "w_out", "ffn2_norm", "ffn2_w_gate", "ffn2_w_up", "ffn2_w_down", "final_norm")


def _as2d(a):
    return a.reshape(1, -1) if a.ndim == 1 else a.reshape(-1, a.shape[-1])


def kernel(x, ffn1_norm, ffn1_w_gate, ffn1_w_up, ffn1_w_down, mix_norm, w_in, conv_w, rel_bias, w_branch, w_merge_gate, w_out, ffn2_norm, ffn2_w_gate, ffn2_w_up, ffn2_w_down, final_norm, loss_target, m_ffn1_norm, m_ffn1_w_gate, m_ffn1_w_up, m_ffn1_w_down, m_mix_norm, m_w_in, m_conv_w, m_rel_bias, m_w_branch, m_w_merge_gate, m_w_out, m_ffn2_norm, m_ffn2_w_gate, m_ffn2_w_up, m_ffn2_w_down, m_final_norm, v_ffn1_norm, v_ffn1_w_gate, v_ffn1_w_up, v_ffn1_w_down, v_mix_norm, v_w_in, v_conv_w, v_rel_bias, v_w_branch, v_w_merge_gate, v_w_out, v_ffn2_norm, v_ffn2_w_gate, v_ffn2_w_up, v_ffn2_w_down, v_final_norm):
    given = dict(locals())
    w = {n: given[n] for n in W_NAMES}
    m = {n: given["m_" + n] for n in W_NAMES}
    v = {n: given["v_" + n] for n in W_NAMES}
    my_chip = 2 * lax.axis_index("x") + lax.axis_index("y")
    L = w_in.shape[0]

    wx = _Exchange({n: jnp.swapaxes(w[n], 1, 2) if n in TRANSPOSED_GRADS else w[n] for n in BIG_NAMES})
    first = _keys(FFN1, 0)
    biases, got = _relbias_expand(
        rel_bias, "relbias_expand", comm=_HalfGather([_halves(wx.own(k)) for k in first] + [_halves(conv_w)]))
    wx.arrived(first, got[:-1], False)
    convw_full = jnp.transpose(got[-1].reshape((N_SHARD,) + conv_w.shape), (1, 2, 0, 3)).reshape(
        conv_w.shape[0], conv_w.shape[1], -1)

    small = {n: w[n] for n in ("ffn1_norm", "mix_norm", "ffn2_norm", "final_norm")}
    loss_p, grad_x, gs = _local_step(x[0], loss_target[0], small, convw_full, biases, wx, L)

    parts = [gs["ffn1_norm"].reshape(-1), gs["mix_norm"].reshape(-1), gs["ffn2_norm"].reshape(-1),
             gs["final_norm"].reshape(-1), gs["rel_bias"].reshape(-1), gs["conv_w"].reshape(-1), loss_p[0]]
    sizes = [p.shape[0] for p in parts]
    flat = jnp.concatenate(parts)
    rows = -(-flat.shape[0] // (8 * LANE)) * 8
    flat = jnp.pad(flat, (0, rows * LANE - flat.shape[0])).reshape(rows, LANE)
    red = _allreduce_small(flat, "allreduce_small").reshape(-1)
    offs = [0]
    for sz in sizes:
        offs.append(offs[-1] + sz)
    sm = {}
    for i, n in enumerate(("ffn1_norm", "mix_norm", "ffn2_norm", "final_norm", "rel_bias", "conv_w")):
        sm[n] = red[offs[i]:offs[i + 1]]
    loss = red[offs[6]]
    sm["conv_w"] = lax.dynamic_slice_in_dim(sm["conv_w"].reshape(conv_w.shape[0], conv_w.shape[1], -1),
                                            my_chip * conv_w.shape[2], conv_w.shape[2], axis=2)

    grads, deltas, new_m, new_v = {}, {}, {}, {}
    for n in W_NAMES:
        flip = n in TRANSPOSED_GRADS

        def view(a):
            return jnp.swapaxes(a, 1, 2) if flip else a

        shape = view(w[n]).shape
        wmv = [_as2d(view(a[n])) for a in (w, m, v)]
        if n in BIG_NAMES:
            lands = [wx.landed[(n, l)] for l in range(L)]
            out = _sum_adamw([a.reshape(a.shape[0], -1, a.shape[-1]) for a in lands], *wmv, f"adamw_{n}")
        else:
            g = _as2d(sm[n].reshape(shape))
            out = [g] + list(_adamw(wmv[0], g, wmv[1], wmv[2], f"adamw_{n}"))
        grads[n], deltas[n], new_m[n], new_v[n] = (view(o.reshape(shape)) for o in out)

    return (loss, grad_x[None], *[grads[n] for n in W_NAMES], *[deltas[n] for n in W_NAMES],
            *[new_m[n] for n in W_NAMES], *[new_v[n] for n in W_NAMES])
```
